```python
import math
import jax, jax.numpy as jnp
from jax import lax
import numpy as np

D_MODEL = 1024
BATCH = 16
SEQ = 4096
DEPTH = 4

MIX_WIDTH = D_MODEL
A_WIDTH = MIX_WIDTH // 2
A_GROUPS = 8
A_GROUP_DIM = A_WIDTH // A_GROUPS
CHUNK = 128
B_HEADS = 8
HEAD_DIM = 64
B_KV_HEADS = 2
B_GROUP = B_HEADS // B_KV_HEADS
B_WIDTH = B_HEADS * HEAD_DIM
KV_WIDTH = B_KV_HEADS * HEAD_DIM
WINDOW = 128
BLOCK = 128
REL_BUCKETS = 32
REL_MAX_DIST = 128
NORM_EPS = 1e-6
IN_WIDTH = 3 * A_WIDTH + 2 * B_WIDTH + 2 * KV_WIDTH

kernel_name = "hybrid_gmlp_swa_sink_sandwich"


def rms_norm(x, g):
    xf = x.astype(jnp.float32)
    y = xf * lax.rsqrt(jnp.mean(xf * xf, axis=-1, keepdims=True) + NORM_EPS)
    return (y * g.astype(jnp.float32)).astype(x.dtype)


def layer_norm(x, g, b):
    xf = x.astype(jnp.float32)
    mu = jnp.mean(xf, axis=-1, keepdims=True)
    xc = xf - mu
    y = xc * lax.rsqrt(jnp.mean(xc * xc, axis=-1, keepdims=True) + NORM_EPS)
    return (y * g.astype(jnp.float32) + b.astype(jnp.float32)).astype(x.dtype)


def t5_bucket(dist):
    max_exact = REL_BUCKETS // 2
    safe = jnp.maximum(dist, 1).astype(jnp.float32)
    large = max_exact + (jnp.log(safe / max_exact) / math.log(REL_MAX_DIST / max_exact)
                         * (REL_BUCKETS - max_exact)).astype(jnp.int32)
    large = jnp.minimum(large, REL_BUCKETS - 1)
    return jnp.where(dist < max_exact, dist, large)


def chunked_spatial_gating(u, v, ln_g, ln_b, w_s, b_s):
    bsz, seq, _ = v.shape
    n_chunks = seq // CHUNK
    vn = layer_norm(v, ln_g, ln_b).reshape(bsz, n_chunks, CHUNK, A_GROUPS, A_GROUP_DIM)
    causal = jnp.tril(jnp.ones((CHUNK, CHUNK), dtype=bool))
    w = jnp.where(causal[None], w_s, jnp.zeros_like(w_s))
    mixed = jnp.einsum('hij,bcjhd->bcihd', w, vn) + b_s.T[None, None, :, :, None]
    return u * mixed.reshape(bsz, seq, A_WIDTH)


def sliding_window_sink_attention(q, k, v, sinks, rel_bias):
    bsz, seq = q.shape[0], q.shape[1]
    nb = seq // BLOCK
    qb = q.reshape(bsz, nb, BLOCK, B_KV_HEADS, B_GROUP, HEAD_DIM)
    pad = jnp.zeros((bsz, BLOCK, B_KV_HEADS, HEAD_DIM), k.dtype)
    kp = jnp.concatenate([pad, k], axis=1).reshape(bsz, nb + 1, BLOCK, B_KV_HEADS, HEAD_DIM)
    vp = jnp.concatenate([pad, v], axis=1).reshape(bsz, nb + 1, BLOCK, B_KV_HEADS, HEAD_DIM)
    kb = jnp.concatenate([kp[:, :-1], kp[:, 1:]], axis=2)
    vb = jnp.concatenate([vp[:, :-1], vp[:, 1:]], axis=2)

    logits = jnp.einsum('bnqkgd,bnskd->bnkgqs', qb, kb).astype(jnp.float32) * (HEAD_DIM ** -0.5)

    q_loc = jnp.arange(BLOCK)[:, None]
    k_loc = jnp.arange(2 * BLOCK)[None, :]
    dist = q_loc + BLOCK - k_loc
    in_window = (dist >= 0) & (dist < WINDOW)
    key_pos = (jnp.arange(nb)[:, None] - 1) * BLOCK + jnp.arange(2 * BLOCK)[None, :]
    mask = in_window[None] & (key_pos >= 0)[:, None, :]

    bias = rel_bias.astype(jnp.float32)[t5_bucket(jnp.maximum(dist, 0))]
    bias = bias.transpose(2, 0, 1).reshape(B_KV_HEADS, B_GROUP, BLOCK, 2 * BLOCK)
    logits = jnp.where(mask[None, :, None, None], logits + bias, jnp.float32(-1e30))

    sink = sinks.astype(jnp.float32).reshape(B_KV_HEADS, B_GROUP)[None, None, :, :, None, None]
    m = jnp.maximum(jnp.max(logits, axis=-1, keepdims=True), sink)
    e = jnp.exp(logits - m)
    probs = e / (jnp.sum(e, axis=-1, keepdims=True) + jnp.exp(sink - m))
    out = jnp.einsum('bnkgqs,bnskd->bnqkgd', probs.astype(vb.dtype), vb)
    return out.reshape(bsz, seq, B_WIDTH)


def hybrid_layer(x, pre_g, w_in, ln_g, ln_b, w_s, b_s, sinks, rel_bias, w_out, post_g):
    bsz, seq, _ = x.shape
    h = rms_norm(x, pre_g)
    proj = h @ w_in
    splits = [2 * A_WIDTH, 3 * A_WIDTH, 3 * A_WIDTH + B_WIDTH,
              3 * A_WIDTH + B_WIDTH + KV_WIDTH, 3 * A_WIDTH + B_WIDTH + 2 * KV_WIDTH]
    a_uv, a_z, q, k, v, b_z = jnp.split(proj, splits, axis=-1)

    a_uv = jax.nn.gelu(a_uv)
    u, vv = jnp.split(a_uv, 2, axis=-1)
    y_a = chunked_spatial_gating(u, vv, ln_g, ln_b, w_s, b_s) * jax.nn.silu(a_z)

    q = q.reshape(bsz, seq, B_HEADS, HEAD_DIM)
    k = k.reshape(bsz, seq, B_KV_HEADS, HEAD_DIM)
    v = v.reshape(bsz, seq, B_KV_HEADS, HEAD_DIM)
    y_b = sliding_window_sink_attention(q, k, v, sinks, rel_bias) * jax.nn.silu(b_z)

    y = jnp.concatenate([y_a, y_b], axis=-1) @ w_out
    return x + rms_norm(y, post_g)


def _fwd_setup_inputs(seed: int = 0) -> dict:
    key = jax.random.key(seed)
    ks = jax.random.split(key, 12)
    f32 = jnp.float32
    x = jax.random.normal(ks[0], (BATCH, SEQ, D_MODEL), f32)
    pre_norm_g = 1.0 + 0.05 * jax.random.normal(ks[1], (DEPTH, D_MODEL), f32)
    w_in = jax.random.normal(ks[2], (DEPTH, D_MODEL, IN_WIDTH), f32) * (D_MODEL ** -0.5)
    ln_v_g = 1.0 + 0.05 * jax.random.normal(ks[3], (DEPTH, A_WIDTH), f32)
    ln_v_b = 0.02 * jax.random.normal(ks[4], (DEPTH, A_WIDTH), f32)
    w_spatial = jax.random.normal(ks[5], (DEPTH, A_GROUPS, CHUNK, CHUNK), f32) * (CHUNK ** -0.5)
    b_spatial = 1.0 + 0.1 * jax.random.normal(ks[6], (DEPTH, A_GROUPS, CHUNK), f32)
    sinks = 0.5 * jax.random.normal(ks[7], (DEPTH, B_HEADS), f32)
    rel_bias = 0.5 * jax.random.normal(ks[8], (REL_BUCKETS, B_HEADS), f32)
    w_out = jax.random.normal(ks[9], (DEPTH, MIX_WIDTH, D_MODEL), f32) * (MIX_WIDTH ** -0.5)
    post_norm_g = 1.0 + 0.05 * jax.random.normal(ks[10], (DEPTH, D_MODEL), f32)
    return {"x": x, "pre_norm_g": pre_norm_g, "w_in": w_in, "ln_v_g": ln_v_g, "ln_v_b": ln_v_b,
            "w_spatial": w_spatial, "b_spatial": b_spatial, "sinks": sinks, "rel_bias": rel_bias,
            "w_out": w_out, "post_norm_g": post_norm_g}


def _fwd_reference(x, pre_norm_g, w_in, ln_v_g, ln_v_b, w_spatial, b_spatial, sinks, rel_bias,
              w_out, post_norm_g):
    for layer in range(DEPTH):
        x = hybrid_layer(x, pre_norm_g[layer], w_in[layer], ln_v_g[layer], ln_v_b[layer],
                         w_spatial[layer], b_spatial[layer], sinks[layer], rel_bias,
                         w_out[layer], post_norm_g[layer])
    return x


import jax as _jax
import jax.numpy as _jnp

TWIN_FORMAT = 'train_step'
FWD_PARAMS = ['x', 'pre_norm_g', 'w_in', 'ln_v_g', 'ln_v_b', 'w_spatial', 'b_spatial', 'sinks', 'rel_bias', 'w_out', 'post_norm_g']
TWIN_WEIGHTS = ['pre_norm_g', 'w_in', 'ln_v_g', 'ln_v_b', 'w_spatial', 'b_spatial', 'sinks', 'rel_bias', 'w_out', 'post_norm_g']
TWIN_DIFF_INPUT = 'x'
TWIN_INPUTS = ['x', 'pre_norm_g', 'w_in', 'ln_v_g', 'ln_v_b', 'w_spatial', 'b_spatial', 'sinks', 'rel_bias', 'w_out', 'post_norm_g', 'loss_target', 'm_pre_norm_g', 'm_w_in', 'm_ln_v_g', 'm_ln_v_b', 'm_w_spatial', 'm_b_spatial', 'm_sinks', 'm_rel_bias', 'm_w_out', 'm_post_norm_g', 'v_pre_norm_g', 'v_w_in', 'v_ln_v_g', 'v_ln_v_b', 'v_w_spatial', 'v_b_spatial', 'v_sinks', 'v_rel_bias', 'v_w_out', 'v_post_norm_g']
TWIN_OUTPUTS = ['loss', 'grad_x', 'grad_pre_norm_g', 'grad_w_in', 'grad_ln_v_g', 'grad_ln_v_b', 'grad_w_spatial', 'grad_b_spatial', 'grad_sinks', 'grad_rel_bias', 'grad_w_out', 'grad_post_norm_g', 'delta_pre_norm_g', 'delta_w_in', 'delta_ln_v_g', 'delta_ln_v_b', 'delta_w_spatial', 'delta_b_spatial', 'delta_sinks', 'delta_rel_bias', 'delta_w_out', 'delta_post_norm_g', 'new_m_pre_norm_g', 'new_m_w_in', 'new_m_ln_v_g', 'new_m_ln_v_b', 'new_m_w_spatial', 'new_m_b_spatial', 'new_m_sinks', 'new_m_rel_bias', 'new_m_w_out', 'new_m_post_norm_g', 'new_v_pre_norm_g', 'new_v_w_in', 'new_v_ln_v_g', 'new_v_ln_v_b', 'new_v_w_spatial', 'new_v_b_spatial', 'new_v_sinks', 'new_v_rel_bias', 'new_v_w_out', 'new_v_post_norm_g']
TWIN_LEAF_KINDS = {'loss': 'loss', 'grad_x': 'grad_x', 'grad_pre_norm_g': 'grad_w', 'grad_w_in': 'grad_w', 'grad_ln_v_g': 'grad_w', 'grad_ln_v_b': 'grad_w', 'grad_w_spatial': 'grad_w', 'grad_b_spatial': 'grad_w', 'grad_sinks': 'grad_w', 'grad_rel_bias': 'grad_w', 'grad_w_out': 'grad_w', 'grad_post_norm_g': 'grad_w', 'delta_pre_norm_g': 'delta_w', 'delta_w_in': 'delta_w', 'delta_ln_v_g': 'delta_w', 'delta_ln_v_b': 'delta_w', 'delta_w_spatial': 'delta_w', 'delta_b_spatial': 'delta_w', 'delta_sinks': 'delta_w', 'delta_rel_bias': 'delta_w', 'delta_w_out': 'delta_w', 'delta_post_norm_g': 'delta_w', 'new_m_pre_norm_g': 'new_m', 'new_m_w_in': 'new_m', 'new_m_ln_v_g': 'new_m', 'new_m_ln_v_b': 'new_m', 'new_m_w_spatial': 'new_m', 'new_m_b_spatial': 'new_m', 'new_m_sinks': 'new_m', 'new_m_rel_bias': 'new_m', 'new_m_w_out': 'new_m', 'new_m_post_norm_g': 'new_m', 'new_v_pre_norm_g': 'new_v', 'new_v_w_in': 'new_v', 'new_v_ln_v_g': 'new_v', 'new_v_ln_v_b': 'new_v', 'new_v_w_spatial': 'new_v', 'new_v_b_spatial': 'new_v', 'new_v_sinks': 'new_v', 'new_v_rel_bias': 'new_v', 'new_v_w_out': 'new_v', 'new_v_post_norm_g': 'new_v'}


def _forward(args):
    return _fwd_reference(*[args[k] for k in FWD_PARAMS])


def _output_shape():
    out = _jax.eval_shape(lambda: _forward(_fwd_setup_inputs(0)))
    return out.shape, out.dtype

N_MICROBATCH = 1
ADAM_LR = 0.001
ADAM_B1 = 0.9
ADAM_B2 = 0.999
ADAM_EPS = 1e-08
ADAM_WD = 0.01
ADAM_STEP = 10
PER_EXAMPLE_BATCH_AXIS = {'x': 0, 'loss_target': 0}
SHARED_INPUTS = []
_WEIGHT_DTYPES = {'pre_norm_g': _jnp.float32, 'w_in': _jnp.float32, 'ln_v_g': _jnp.float32, 'ln_v_b': _jnp.float32, 'w_spatial': _jnp.float32, 'b_spatial': _jnp.float32, 'sinks': _jnp.float32, 'rel_bias': _jnp.float32, 'w_out': _jnp.float32, 'post_norm_g': _jnp.float32}
MOMENT_SCALE = {'pre_norm_g': 1.739701e+00, 'w_in': 1.086404e+00, 'ln_v_g': 7.499862e-01, 'ln_v_b': 8.021370e-01, 'w_spatial': 5.484066e-01, 'b_spatial': 8.534466e-01, 'sinks': 2.705760e-01, 'rel_bias': 1.070050e+00, 'w_out': 1.684053e+00, 'post_norm_g': 6.354616e+01}


def _to_microbatches(a, axis):
    t = _jnp.moveaxis(a, axis, 0)
    t = t.reshape((N_MICROBATCH, t.shape[0] // N_MICROBATCH) + t.shape[1:])
    return _jnp.moveaxis(t, 1, axis + 1)


def setup_inputs(seed: int = 0) -> dict:
    inp = _fwd_setup_inputs(seed)
    key = _jax.random.fold_in(_jax.random.key(seed), 7919)
    shape, _ = _output_shape()
    out = dict(inp)
    out["loss_target"] = _jax.random.normal(_jax.random.fold_in(key, 0), shape, _jnp.float32)
    for i, name in enumerate(TWIN_WEIGHTS):
        w = inp[name].astype(_jnp.float32)
        if MOMENT_SCALE is None:
            s = _jnp.sqrt(_jnp.mean(_jnp.square(w)) + 1e-30)
        else:
            s = MOMENT_SCALE[name]
        km, kv = _jax.random.split(_jax.random.fold_in(key, i + 1))
        out[name] = w
        out["m_" + name] = s * _jax.random.normal(km, w.shape, _jnp.float32)
        out["v_" + name] = (s * s) * _jax.random.uniform(kv, w.shape, _jnp.float32, 0.5, 1.5)
    if N_MICROBATCH > 1:
        for name, axis in PER_EXAMPLE_BATCH_AXIS.items():
            out[name] = _to_microbatches(out[name], axis)
    return {'x': out['x'], 'pre_norm_g': out['pre_norm_g'], 'w_in': out['w_in'], 'ln_v_g': out['ln_v_g'], 'ln_v_b': out['ln_v_b'], 'w_spatial': out['w_spatial'], 'b_spatial': out['b_spatial'], 'sinks': out['sinks'], 'rel_bias': out['rel_bias'], 'w_out': out['w_out'], 'post_norm_g': out['post_norm_g'], 'loss_target': out['loss_target'], 'm_pre_norm_g': out['m_pre_norm_g'], 'm_w_in': out['m_w_in'], 'm_ln_v_g': out['m_ln_v_g'], 'm_ln_v_b': out['m_ln_v_b'], 'm_w_spatial': out['m_w_spatial'], 'm_b_spatial': out['m_b_spatial'], 'm_sinks': out['m_sinks'], 'm_rel_bias': out['m_rel_bias'], 'm_w_out': out['m_w_out'], 'm_post_norm_g': out['m_post_norm_g'], 'v_pre_norm_g': out['v_pre_norm_g'], 'v_w_in': out['v_w_in'], 'v_ln_v_g': out['v_ln_v_g'], 'v_ln_v_b': out['v_ln_v_b'], 'v_w_spatial': out['v_w_spatial'], 'v_b_spatial': out['v_b_spatial'], 'v_sinks': out['v_sinks'], 'v_rel_bias': out['v_rel_bias'], 'v_w_out': out['v_w_out'], 'v_post_norm_g': out['v_post_norm_g']}


def _loss(weights, diff, rest, loss_target):
    with _jax.named_scope("forward"):
        args = {**rest, TWIN_DIFF_INPUT: diff, **{k: w.astype(_WEIGHT_DTYPES[k]) for k, w in weights.items()}}
        y = _forward(args)
    with _jax.named_scope("loss_head"):
        err = _jnp.square(y.astype(_jnp.float32) - loss_target)
        return 0.5 * _jnp.sum(_jnp.mean(err, axis=-1)) if err.ndim else 0.5 * err


def _adamw(w, g, m, v):
    m = ADAM_B1 * m + (1.0 - ADAM_B1) * g
    v = ADAM_B2 * v + (1.0 - ADAM_B2) * _jnp.square(g)
    m_hat = m / (1.0 - ADAM_B1 ** ADAM_STEP)
    v_hat = v / (1.0 - ADAM_B2 ** ADAM_STEP)
    delta = -ADAM_LR * (m_hat / (_jnp.sqrt(v_hat) + ADAM_EPS) + ADAM_WD * w)
    return delta, m, v


def reference(x, pre_norm_g, w_in, ln_v_g, ln_v_b, w_spatial, b_spatial, sinks, rel_bias, w_out, post_norm_g, loss_target, m_pre_norm_g, m_w_in, m_ln_v_g, m_ln_v_b, m_w_spatial, m_b_spatial, m_sinks, m_rel_bias, m_w_out, m_post_norm_g, v_pre_norm_g, v_w_in, v_ln_v_g, v_ln_v_b, v_w_spatial, v_b_spatial, v_sinks, v_rel_bias, v_w_out, v_post_norm_g):
    given = dict(x=x, pre_norm_g=pre_norm_g, w_in=w_in, ln_v_g=ln_v_g, ln_v_b=ln_v_b, w_spatial=w_spatial, b_spatial=b_spatial, sinks=sinks, rel_bias=rel_bias, w_out=w_out, post_norm_g=post_norm_g, loss_target=loss_target, m_pre_norm_g=m_pre_norm_g, m_w_in=m_w_in, m_ln_v_g=m_ln_v_g, m_ln_v_b=m_ln_v_b, m_w_spatial=m_w_spatial, m_b_spatial=m_b_spatial, m_sinks=m_sinks, m_rel_bias=m_rel_bias, m_w_out=m_w_out, m_post_norm_g=m_post_norm_g, v_pre_norm_g=v_pre_norm_g, v_w_in=v_w_in, v_ln_v_g=v_ln_v_g, v_ln_v_b=v_ln_v_b, v_w_spatial=v_w_spatial, v_b_spatial=v_b_spatial, v_sinks=v_sinks, v_rel_bias=v_rel_bias, v_w_out=v_w_out, v_post_norm_g=v_post_norm_g)
    weights = {n: given[n] for n in TWIN_WEIGHTS}
    shared = {n: given[n] for n in SHARED_INPUTS}
    per_example = {n: given[n] for n in ['x']}
    grad_fn = _jax.value_and_grad(_loss, argnums=(0, 1))

    def one_microbatch(ex, loss_target):
        ex = dict(ex)
        diff = ex.pop(TWIN_DIFF_INPUT)
        return grad_fn(weights, diff, {**shared, **ex}, loss_target)

    if N_MICROBATCH == 1:
        loss, (grad_w, grad_x) = one_microbatch(per_example, given["loss_target"])
    else:
        def body(carry, xs):
            loss_sum, grad_sum = carry
            l_k, (gw_k, gx_k) = one_microbatch(xs[0], xs[1])
            with _jax.named_scope("update"):
                return (loss_sum + l_k, _jax.tree.map(_jnp.add, grad_sum, gw_k)), gx_k

        init = (_jnp.zeros((), _jnp.float32), _jax.tree.map(_jnp.zeros_like, weights))
        (loss, grad_w), grad_x = _jax.lax.scan(body, init, (per_example, given["loss_target"]))
    with _jax.named_scope("update"):
        delta_w, new_m, new_v = {}, {}, {}
        for n in TWIN_WEIGHTS:
            delta_w[n], new_m[n], new_v[n] = _adamw(weights[n], grad_w[n], given["m_" + n], given["v_" + n])
    return (loss, grad_x, *[grad_w[n] for n in TWIN_WEIGHTS], *[delta_w[n] for n in TWIN_WEIGHTS],
            *[new_m[n] for n in TWIN_WEIGHTS], *[new_v[n] for n in TWIN_WEIGHTS])
```

```python
import functools
import math

import jax
import jax.numpy as jnp
from jax import lax
from jax.experimental import pallas as pl
from jax.experimental.pallas import tpu as pltpu

F32 = jnp.float32
BF16 = jnp.bfloat16

D_MODEL = 1024
A_WIDTH = 512
B_WIDTH = 512
KV_WIDTH = 128
IN_WIDTH = 3 * A_WIDTH + 2 * B_WIDTH + 2 * KV_WIDTH
CHUNK = 128
HEADS = 8
HEAD_DIM = 64
REL_BUCKETS = 32
NORM_EPS = 1e-6
NEG = -1e30
Q_OFF = 3 * A_WIDTH
K_OFF = Q_OFF + B_WIDTH
BZ_OFF = K_OFF + 2 * KV_WIDTH

ADAM_LR = 0.001
ADAM_B1 = 0.9
ADAM_B2 = 0.999
ADAM_EPS = 1e-08
ADAM_WD = 0.01
ADAM_STEP = 10

TOK_TILE = 512
VMEM_LIMIT_V7X = 60 * 1024 * 1024


def _cparams(n_axes):
    return pltpu.CompilerParams(dimension_semantics=("arbitrary",) * n_axes, vmem_limit_bytes=VMEM_LIMIT_V7X)


_GELU_C = math.sqrt(2.0 / math.pi)


def _gelu(x):
    inner = _GELU_C * (x + 0.044715 * (x * x * x))
    return x * (0.5 * (1.0 + jnp.tanh(inner)))


def _gelu_and_grad(x):
    x2 = x * x
    t = jnp.tanh(_GELU_C * (x + 0.044715 * (x2 * x)))
    cdf = 0.5 * (1.0 + t)
    d = cdf + x * (0.5 * (1.0 - t * t)) * (_GELU_C * (1.0 + 3.0 * 0.044715 * x2))
    return x * cdf, d


def _silu(x):
    return x * (1.0 / (1.0 + jnp.exp(-x)))


def _silu_and_grad(x):
    s = 1.0 / (1.0 + jnp.exp(-x))
    return x * s, s * (1.0 + x * (1.0 - s))


def _dot(a, b):
    return jnp.dot(a, b, preferred_element_type=F32)


def _dot_nt(a, b):
    return lax.dot_general(a, b, (((1,), (1,)), ((), ())), preferred_element_type=F32)


def _dot_tn(a, b):
    return lax.dot_general(a, b, (((0,), (0,)), ((), ())), preferred_element_type=F32)


def _lo_mask(shape):
    return lax.broadcasted_iota(jnp.int32, shape, 1) < HEAD_DIM


def _bias_table(rel_bias, bucket, inwin):
    def body(rb_ref, bk_ref, win_ref, out_ref):
        bk = bk_ref[...]
        win = win_ref[...] > 0
        for h in range(HEADS):
            acc = jnp.zeros(bk.shape, F32)
            for b in range(REL_BUCKETS):
                acc = jnp.where(bk == b, rb_ref[b, h], acc)
            out_ref[h] = jnp.where(win, acc, NEG)

    return pl.pallas_call(
        body, name="bias_table",
        out_shape=jax.ShapeDtypeStruct((HEADS, CHUNK, 2 * CHUNK), F32),
        in_specs=[pl.BlockSpec(memory_space=pltpu.SMEM), pl.BlockSpec(memory_space=pltpu.VMEM),
                  pl.BlockSpec(memory_space=pltpu.VMEM)],
        out_specs=pl.BlockSpec(memory_space=pltpu.VMEM),
    )(rel_bias, bucket, inwin)


def _rel_bias_grad(dbias, bucket, inwin):
    n_layers = dbias.shape[0]

    def body(db_ref, bk_ref, win_ref, out_ref):
        bk = jnp.where(win_ref[...] > 0, bk_ref[...], -1)
        for h in range(HEADS):
            tot = db_ref[0, h]
            for l in range(1, n_layers):
                tot = tot + db_ref[l, h]
            for b in range(REL_BUCKETS):
                out_ref[b, h] = jnp.sum(jnp.where(bk == b, tot, 0.0))

    return pl.pallas_call(
        body, name="rel_bias_grad",
        out_shape=jax.ShapeDtypeStruct((REL_BUCKETS, HEADS), F32),
        in_specs=[pl.BlockSpec(memory_space=pltpu.VMEM)] * 3,
        out_specs=pl.BlockSpec(memory_space=pltpu.SMEM),
    )(dbias, bucket, inwin)


def _spatial_tables(w_spatial, b_spatial_t):
    n_layers = w_spatial.shape[0]

    def body(w_ref, b_ref, wcat_ref, wtcat_ref, bs_ref):
        row = lax.broadcasted_iota(jnp.int32, (CHUNK, CHUNK), 0)
        col = lax.broadcasted_iota(jnp.int32, (CHUNK, CHUNK), 1)
        causal = col <= row
        lo = _lo_mask((CHUNK, CHUNK))
        for p in range(4):
            for half in range(2):
                w = jnp.where(causal, w_ref[0, 2 * p + half], 0.0)
                wcat_ref[0, p, :, half * CHUNK:(half + 1) * CHUNK] = w.astype(BF16)
                wtcat_ref[0, p, :, half * CHUNK:(half + 1) * CHUNK] = w.T.astype(BF16)
            b = b_ref[0]
            bs_ref[0, p] = jnp.where(lo, b[:, 2 * p:2 * p + 1], b[:, 2 * p + 1:2 * p + 2])

    return pl.pallas_call(
        body, name="spatial_tables", grid=(n_layers,),
        out_shape=(jax.ShapeDtypeStruct((n_layers, 4, CHUNK, 2 * CHUNK), BF16),
                   jax.ShapeDtypeStruct((n_layers, 4, CHUNK, 2 * CHUNK), BF16),
                   jax.ShapeDtypeStruct((n_layers, 4, CHUNK, CHUNK), F32)),
        in_specs=[pl.BlockSpec((1, HEADS, CHUNK, CHUNK), lambda l: (l, 0, 0, 0)),
                  pl.BlockSpec((1, CHUNK, HEADS), lambda l: (l, 0, 0))],
        out_specs=(pl.BlockSpec((1, 4, CHUNK, 2 * CHUNK), lambda l: (l, 0, 0, 0)),
                   pl.BlockSpec((1, 4, CHUNK, 2 * CHUNK), lambda l: (l, 0, 0, 0)),
                   pl.BlockSpec((1, 4, CHUNK, CHUNK), lambda l: (l, 0, 0, 0))),
        compiler_params=_cparams(1),
    )(w_spatial, b_spatial_t)


def _fwd_in(x, g_pre, w_in):
    n_tok = x.shape[0]

    def body(x_ref, g_ref, w_ref, a_ref, q_ref, kv_ref, bz_ref):
        xf = x_ref[...]
        r = lax.rsqrt(jnp.mean(xf * xf, axis=-1, keepdims=True) + NORM_EPS)
        h = ((xf * r) * g_ref[...]).astype(BF16)
        a_ref[...] = _dot(h, w_ref[:, 0:Q_OFF])
        q_ref[...] = _dot(h, w_ref[:, Q_OFF:K_OFF]).astype(BF16)
        kv_ref[...] = _dot(h, w_ref[:, K_OFF:BZ_OFF]).astype(BF16)
        bz_ref[...] = _dot(h, w_ref[:, BZ_OFF:IN_WIDTH])

    tm = TOK_TILE
    return pl.pallas_call(
        body, name="fwd_in", grid=(n_tok // tm,),
        out_shape=(jax.ShapeDtypeStruct((n_tok, Q_OFF), F32), jax.ShapeDtypeStruct((n_tok, B_WIDTH), BF16),
                   jax.ShapeDtypeStruct((n_tok, 2 * KV_WIDTH), BF16), jax.ShapeDtypeStruct((n_tok, B_WIDTH), F32)),
        in_specs=[pl.BlockSpec((tm, D_MODEL), lambda i: (i, 0)), pl.BlockSpec((1, D_MODEL), lambda i: (0, 0)),
                  pl.BlockSpec((D_MODEL, IN_WIDTH), lambda i: (0, 0))],
        out_specs=(pl.BlockSpec((tm, Q_OFF), lambda i: (i, 0)), pl.BlockSpec((tm, B_WIDTH), lambda i: (i, 0)),
                   pl.BlockSpec((tm, 2 * KV_WIDTH), lambda i: (i, 0)), pl.BlockSpec((tm, B_WIDTH), lambda i: (i, 0))),
        compiler_params=_cparams(1),
    )(x, g_pre, w_in)


def _layer_norm_stats(vv):
    mu = jnp.mean(vv, axis=-1, keepdims=True)
    xc = vv - mu
    rs = lax.rsqrt(jnp.mean(xc * xc, axis=-1, keepdims=True) + NORM_EPS)
    return xc * rs, rs


def _blockdiag(v, lo):
    zero = jnp.zeros_like(v)
    return jnp.concatenate([jnp.where(lo, v, zero), jnp.where(lo, zero, v)], axis=0)


def _softmax_sink(s, sink):
    m = jnp.maximum(jnp.max(s, axis=-1, keepdims=True), sink)
    e = jnp.exp(s - m)
    esink = jnp.exp(sink - m)
    den = jnp.sum(e, axis=-1, keepdims=True) + esink
    return e / den, esink / den


def _kv_rows(cur_ref, halo_ref, r0, c):
    prev_in_tile = cur_ref[pl.ds(pl.multiple_of(jnp.maximum(r0 - CHUNK, 0), CHUNK), CHUNK), :]
    prev = jnp.where(c == 0, halo_ref[...], prev_in_tile)
    return jnp.concatenate([prev, cur_ref[pl.ds(r0, CHUNK), :]], axis=0)


def _fwd_mix(x, a, q, kv, bz, ln_g, ln_b, wcat, bs, sinks, bias, w_out, g_post, seq_len):
    n_tok = x.shape[0]
    tb = TOK_TILE
    n_chunks = tb // CHUNK

    def body(x_ref, a_ref, q_ref, kv_ref, halo_ref, bz_ref, lng_ref, lnb_ref, wcat_ref, bs_ref, sink_ref, bias_ref,
             wout_ref, gpost_ref, xn_ref, y_ref, ycat_s):
        i = pl.program_id(0)
        lo = _lo_mask((CHUNK, CHUNK))
        kidx = lax.broadcasted_iota(jnp.int32, (CHUNK, 2 * CHUNK), 1)

        def chunk(c, carry):
            r0 = pl.multiple_of(c * CHUNK, CHUNK)
            rows = pl.ds(r0, CHUNK)
            u = _gelu(a_ref[rows, 0:A_WIDTH])
            vv = _gelu(a_ref[rows, A_WIDTH:2 * A_WIDTH])
            xhat, _ = _layer_norm_stats(vv)
            vnb = (xhat * lng_ref[...] + lnb_ref[...]).astype(BF16)
            for p in range(4):
                blk = slice(p * CHUNK, (p + 1) * CHUNK)
                mixed = _dot(wcat_ref[p], _blockdiag(vnb[:, blk], lo)) + bs_ref[p]
                sz = _silu(a_ref[rows, 2 * A_WIDTH + p * CHUNK:2 * A_WIDTH + (p + 1) * CHUNK])
                ycat_s[rows, blk] = ((u[:, blk] * mixed) * sz).astype(BF16)
            kv2 = _kv_rows(kv_ref, halo_ref, r0, c)
            k2 = kv2[:, 0:KV_WIDTH]
            v2 = kv2[:, KV_WIDTH:2 * KV_WIDTH]
            first = lax.rem(i * tb + r0, seq_len) == 0
            dead = kidx < jnp.where(first, CHUNK, 0)
            for p in range(4):
                blk = slice(p * CHUNK, (p + 1) * CHUNK)
                qt = q_ref[rows, blk]
                zero = jnp.zeros_like(qt)
                res = []
                for half, head in ((0, p), (1, 4 + p)):
                    qm = jnp.where(lo, qt, zero) if half == 0 else jnp.where(lo, zero, qt)
                    s = _dot_nt(qm, k2) * (HEAD_DIM ** -0.5) + bias_ref[head]
                    s = jnp.where(dead, NEG, s)
                    prob, _ = _softmax_sink(s, sink_ref[head])
                    res.append(_dot(prob.astype(BF16), v2))
                o = jnp.where(lo, res[0], res[1])
                ycat_s[rows, B_WIDTH + p * CHUNK:B_WIDTH + (p + 1) * CHUNK] = (o * _silu(bz_ref[rows, blk])).astype(BF16)
            return carry

        lax.fori_loop(0, n_chunks, chunk, 0)
        y = _dot(ycat_s[...], wout_ref[...])
        r = lax.rsqrt(jnp.mean(y * y, axis=-1, keepdims=True) + NORM_EPS)
        y_ref[...] = y
        xn_ref[...] = x_ref[...] + (y * r) * gpost_ref[...]

    tile = lambda w: pl.BlockSpec((tb, w), lambda i: (i, 0))
    whole = lambda shape: pl.BlockSpec(shape, lambda i: (0,) * len(shape))
    return pl.pallas_call(
        body, name="fwd_mix", grid=(n_tok // tb,),
        out_shape=(jax.ShapeDtypeStruct((n_tok, D_MODEL), F32), jax.ShapeDtypeStruct((n_tok, D_MODEL), F32)),
        in_specs=[tile(D_MODEL), tile(Q_OFF), tile(B_WIDTH), tile(2 * KV_WIDTH),
                  pl.BlockSpec((CHUNK, 2 * KV_WIDTH), lambda i: (jnp.maximum(i * n_chunks - 1, 0), 0)),
                  tile(B_WIDTH), whole((1, A_WIDTH)), whole((1, A_WIDTH)), whole((4, CHUNK, 2 * CHUNK)),
                  whole((4, CHUNK, CHUNK)), pl.BlockSpec(memory_space=pltpu.SMEM), whole((HEADS, CHUNK, 2 * CHUNK)),
                  whole((D_MODEL, D_MODEL)), whole((1, D_MODEL))],
        out_specs=(tile(D_MODEL), tile(D_MODEL)),
        scratch_shapes=[pltpu.VMEM((tb, D_MODEL), BF16)],
        compiler_params=_cparams(1),
    )(x, a, q, kv, kv, bz, ln_g, ln_b, wcat, bs, sinks, bias, w_out, g_post)


def _loss_head(x, target):
    n_tok = x.shape[0]
    tm = TOK_TILE

    def body(x_ref, t_ref, dx_ref, loss_ref):
        d = x_ref[...] - t_ref[...]
        dx_ref[...] = d * (1.0 / D_MODEL)
        part = 0.5 * jnp.sum(jnp.mean(d * d, axis=-1, keepdims=True))

        @pl.when(pl.program_id(0) == 0)
        def _():
            loss_ref[0, 0] = 0.0

        loss_ref[0, 0] += part

    return pl.pallas_call(
        body, name="loss_head", grid=(n_tok // tm,),
        out_shape=(jax.ShapeDtypeStruct((n_tok, D_MODEL), F32), jax.ShapeDtypeStruct((1, 1), F32)),
        in_specs=[pl.BlockSpec((tm, D_MODEL), lambda i: (i, 0))] * 2,
        out_specs=(pl.BlockSpec((tm, D_MODEL), lambda i: (i, 0)), pl.BlockSpec(memory_space=pltpu.SMEM)),
        compiler_params=_cparams(1),
    )(x, target)


def _bwd_mix(dout, y, a, q, kv, bz, ln_g, ln_b, wcat, wtcat, bs, sinks, bias, w_out, g_post, seq_len):
    n_tok = y.shape[0]
    tb = TOK_TILE
    n_chunks = tb // CHUNK
    n_tiles = n_tok // tb

    def body(dout_ref, y_ref, a_ref, q_ref, kv_ref, halo_ref, bz_ref, lng_ref, lnb_ref, wcat_ref, wtcat_ref, bs_ref,
             sink_ref, bias_ref, wout_ref, gpost_ref,
             dproj_ref, dwout_ref, dws_ref, dmix_ref, dlng_ref, dlnb_ref, dgpost_ref, dsink_ref, dbias_ref,
             ycat_s, dy_s, dyc_s, dkv_s, carry_s):
        step = pl.program_id(0)
        tile_idx = n_tiles - 1 - step
        lo = _lo_mask((CHUNK, CHUNK))
        kidx = lax.broadcasted_iota(jnp.int32, (CHUNK, 2 * CHUNK), 1)

        @pl.when(step == 0)
        def _():
            dwout_ref[...] = jnp.zeros_like(dwout_ref)
            dws_ref[...] = jnp.zeros_like(dws_ref)
            dmix_ref[...] = jnp.zeros_like(dmix_ref)
            dlng_ref[...] = jnp.zeros_like(dlng_ref)
            dlnb_ref[...] = jnp.zeros_like(dlnb_ref)
            dgpost_ref[...] = jnp.zeros_like(dgpost_ref)
            dbias_ref[...] = jnp.zeros_like(dbias_ref)
            carry_s[...] = jnp.zeros_like(carry_s)
            for h in range(HEADS):
                dsink_ref[h] = 0.0

        yv = y_ref[...]
        dout = dout_ref[...]
        r = lax.rsqrt(jnp.mean(yv * yv, axis=-1, keepdims=True) + NORM_EPS)
        yn = yv * r
        dgpost_ref[...] += jnp.sum(dout * yn, axis=0, keepdims=True)
        dyn = dout * gpost_ref[...]
        dy = r * (dyn - yn * jnp.mean(dyn * yn, axis=-1, keepdims=True))
        dy_s[...] = dy.astype(BF16)
        dyc_s[...] = _dot_nt(dy_s[...], wout_ref[...])
        dkv_s[0:tb, :] = jnp.zeros((tb, 2 * KV_WIDTH), F32)
        dkv_s[tb:tb + CHUNK, :] = carry_s[...]

        def chunk(c, carry):
            r0 = pl.multiple_of(c * CHUNK, CHUNK)
            rows = pl.ds(r0, CHUNK)
            u, gu = _gelu_and_grad(a_ref[rows, 0:A_WIDTH])
            vv, gv = _gelu_and_grad(a_ref[rows, A_WIDTH:2 * A_WIDTH])
            xhat, rs = _layer_norm_stats(vv)
            vnb = (xhat * lng_ref[...] + lnb_ref[...]).astype(BF16)
            d_vn, d_u, d_az = [], [], []
            for p in range(4):
                blk = slice(p * CHUNK, (p + 1) * CHUNK)
                vbd = _blockdiag(vnb[:, blk], lo)
                mixed = _dot(wcat_ref[p], vbd) + bs_ref[p]
                sz, gz = _silu_and_grad(a_ref[rows, 2 * A_WIDTH + p * CHUNK:2 * A_WIDTH + (p + 1) * CHUNK])
                ub = u[:, blk]
                dya = dyc_s[rows, blk]
                um = ub * mixed
                ycat_s[rows, blk] = (um * sz).astype(BF16)
                d_mixed = (dya * ub) * sz
                d_u.append((dya * mixed) * sz)
                d_az.append((dya * um) * gz)
                dmix_ref[:, blk] += d_mixed
                dmbd = _blockdiag(d_mixed.astype(BF16), lo)
                d_vn.append(_dot(wtcat_ref[p], dmbd))
                dws_ref[p] += _dot_nt(dmbd, vnb[:, blk])
            d_vn = jnp.concatenate(d_vn, axis=1)
            dlng_ref[...] += jnp.sum(d_vn * xhat, axis=0, keepdims=True)
            dlnb_ref[...] += jnp.sum(d_vn, axis=0, keepdims=True)
            dxh = d_vn * lng_ref[...]
            d_vv = rs * (dxh - jnp.mean(dxh, axis=-1, keepdims=True)
                         - xhat * jnp.mean(dxh * xhat, axis=-1, keepdims=True))
            dproj_ref[rows, 0:A_WIDTH] = (jnp.concatenate(d_u, axis=1) * gu).astype(BF16)
            dproj_ref[rows, A_WIDTH:2 * A_WIDTH] = (d_vv * gv).astype(BF16)
            dproj_ref[rows, 2 * A_WIDTH:Q_OFF] = jnp.concatenate(d_az, axis=1).astype(BF16)
            kv2 = _kv_rows(kv_ref, halo_ref, r0, c)
            k2 = kv2[:, 0:KV_WIDTH]
            v2 = kv2[:, KV_WIDTH:2 * KV_WIDTH]
            first = lax.rem(tile_idx * tb + r0, seq_len) == 0
            dead = kidx < jnp.where(first, CHUNK, 0)
            dk2 = jnp.zeros((2 * CHUNK, KV_WIDTH), F32)
            dv2 = jnp.zeros((2 * CHUNK, KV_WIDTH), F32)
            for p in range(4):
                blk = slice(p * CHUNK, (p + 1) * CHUNK)
                qt = q_ref[rows, blk]
                zero = jnp.zeros_like(qt)
                sb, gb = _silu_and_grad(bz_ref[rows, blk])
                dyb = dyc_s[rows, B_WIDTH + p * CHUNK:B_WIDTH + (p + 1) * CHUNK]
                qms, probs, psinks, outs = [], [], [], []
                for half, head in ((0, p), (1, 4 + p)):
                    qm = jnp.where(lo, qt, zero) if half == 0 else jnp.where(lo, zero, qt)
                    s = _dot_nt(qm, k2) * (HEAD_DIM ** -0.5) + bias_ref[head]
                    s = jnp.where(dead, NEG, s)
                    prob, psink = _softmax_sink(s, sink_ref[head])
                    qms.append(qm)
                    probs.append(prob)
                    psinks.append(psink)
                    outs.append(_dot(prob.astype(BF16), v2))
                o = jnp.where(lo, outs[0], outs[1])
                ycat_s[rows, B_WIDTH + p * CHUNK:B_WIDTH + (p + 1) * CHUNK] = (o * sb).astype(BF16)
                dproj_ref[rows, BZ_OFF + p * CHUNK:BZ_OFF + (p + 1) * CHUNK] = ((dyb * o) * gb).astype(BF16)
                d_ob = (dyb * sb).astype(BF16)
                zero_o = jnp.zeros_like(d_ob)
                dss, doms, dqs = [], [], []
                for half, head in ((0, p), (1, 4 + p)):
                    dom = jnp.where(lo, d_ob, zero_o) if half == 0 else jnp.where(lo, zero_o, d_ob)
                    dp = _dot_nt(dom, v2)
                    delta = jnp.sum(probs[half] * dp, axis=-1, keepdims=True)
                    ds = probs[half] * (dp - delta)
                    dsink_ref[head] += -jnp.sum(psinks[half] * delta)
                    dbias_ref[head] += ds
                    dsb = ds.astype(BF16)
                    dss.append(dsb)
                    doms.append(dom)
                    dqs.append(_dot(dsb, k2))
                dq = jnp.where(lo, dqs[0], dqs[1]) * (HEAD_DIM ** -0.5)
                dproj_ref[rows, Q_OFF + p * CHUNK:Q_OFF + (p + 1) * CHUNK] = dq.astype(BF16)
                dk2 = dk2 + _dot_tn(jnp.concatenate(dss, axis=0), jnp.concatenate(qms, axis=0))
                pb = jnp.concatenate([probs[0].astype(BF16), probs[1].astype(BF16)], axis=0)
                dv2 = dv2 + _dot_tn(pb, jnp.concatenate(doms, axis=0))
            both = pl.ds(r0, 2 * CHUNK)
            dkv_s[both, 0:KV_WIDTH] += dk2 * (HEAD_DIM ** -0.5)
            dkv_s[both, KV_WIDTH:2 * KV_WIDTH] += dv2
            return carry

        lax.fori_loop(0, n_chunks, chunk, 0)
        dwout_ref[...] += _dot_tn(ycat_s[...], dy_s[...])
        dproj_ref[:, K_OFF:BZ_OFF] = dkv_s[CHUNK:CHUNK + tb, :].astype(BF16)
        carry_s[...] = dkv_s[0:CHUNK, :]

        @pl.when(step == n_tiles - 1)
        def _():
            row = lax.broadcasted_iota(jnp.int32, (2 * CHUNK, CHUNK), 0)
            col = lax.broadcasted_iota(jnp.int32, (2 * CHUNK, CHUNK), 1)
            causal = col <= jnp.where(row >= CHUNK, row - CHUNK, row)
            for p in range(4):
                dws_ref[p] = jnp.where(causal, dws_ref[p], 0.0)

    tile = lambda w: pl.BlockSpec((tb, w), lambda s: (n_tiles - 1 - s, 0))
    whole = lambda shape: pl.BlockSpec(shape, lambda s: (0,) * len(shape))
    smem = pl.BlockSpec(memory_space=pltpu.SMEM)
    return pl.pallas_call(
        body, name="bwd_mix", grid=(n_tiles,),
        out_shape=(jax.ShapeDtypeStruct((n_tok, IN_WIDTH), BF16), jax.ShapeDtypeStruct((D_MODEL, D_MODEL), F32),
                   jax.ShapeDtypeStruct((4, 2 * CHUNK, CHUNK), F32), jax.ShapeDtypeStruct((CHUNK, A_WIDTH), F32),
                   jax.ShapeDtypeStruct((1, A_WIDTH), F32), jax.ShapeDtypeStruct((1, A_WIDTH), F32),
                   jax.ShapeDtypeStruct((1, D_MODEL), F32), jax.ShapeDtypeStruct((HEADS,), F32),
                   jax.ShapeDtypeStruct((HEADS, CHUNK, 2 * CHUNK), F32)),
        in_specs=[tile(D_MODEL), tile(D_MODEL), tile(Q_OFF), tile(B_WIDTH), tile(2 * KV_WIDTH),
                  pl.BlockSpec((CHUNK, 2 * KV_WIDTH), lambda s: (jnp.maximum((n_tiles - 1 - s) * n_chunks - 1, 0), 0)),
                  tile(B_WIDTH), whole((1, A_WIDTH)), whole((1, A_WIDTH)), whole((4, CHUNK, 2 * CHUNK)),
                  whole((4, CHUNK, 2 * CHUNK)), whole((4, CHUNK, CHUNK)), smem, whole((HEADS, CHUNK, 2 * CHUNK)),
                  whole((D_MODEL, D_MODEL)), whole((1, D_MODEL))],
        out_specs=(tile(IN_WIDTH), whole((D_MODEL, D_MODEL)), whole((4, 2 * CHUNK, CHUNK)), whole((CHUNK, A_WIDTH)),
                   whole((1, A_WIDTH)), whole((1, A_WIDTH)), whole((1, D_MODEL)), smem,
                   whole((HEADS, CHUNK, 2 * CHUNK))),
        scratch_shapes=[pltpu.VMEM((tb, D_MODEL), BF16), pltpu.VMEM((tb, D_MODEL), BF16), pltpu.VMEM((tb, D_MODEL), F32),
                        pltpu.VMEM((tb + CHUNK, 2 * KV_WIDTH), F32), pltpu.VMEM((CHUNK, 2 * KV_WIDTH), F32)],
        compiler_params=_cparams(1),
    )(dout, y, a, q, kv, kv, bz, ln_g, ln_b, wcat, wtcat, bs, sinks, bias, w_out, g_post)


def _bwd_in(dproj, x, dout, g_pre, w_in_t):
    n_tok = x.shape[0]
    tm = TOK_TILE
    n_tiles = n_tok // tm

    def body(dp_ref, x_ref, dout_ref, g_ref, wt_ref, dx_ref, dwt_ref, dg_ref, acc_s):
        i = pl.program_id(0)

        @pl.when(i == 0)
        def _():
            acc_s[...] = jnp.zeros_like(acc_s)
            dg_ref[...] = jnp.zeros_like(dg_ref)

        xf = x_ref[...]
        r = lax.rsqrt(jnp.mean(xf * xf, axis=-1, keepdims=True) + NORM_EPS)
        xn = xf * r
        h = (xn * g_ref[...]).astype(BF16)
        dp = dp_ref[...]
        dh = _dot(dp, wt_ref[...])
        acc_s[...] += _dot_tn(dp, h)
        dg_ref[...] += jnp.sum(dh * xn, axis=0, keepdims=True)
        dhn = dh * g_ref[...]
        dx_ref[...] = dout_ref[...] + r * (dhn - xn * jnp.mean(dhn * xn, axis=-1, keepdims=True))

        @pl.when(i == n_tiles - 1)
        def _():
            dwt_ref[...] = acc_s[...].astype(BF16)

    return pl.pallas_call(
        body, name="bwd_in", grid=(n_tiles,),
        out_shape=(jax.ShapeDtypeStruct((n_tok, D_MODEL), F32), jax.ShapeDtypeStruct((IN_WIDTH, D_MODEL), BF16),
                   jax.ShapeDtypeStruct((1, D_MODEL), F32)),
        in_specs=[pl.BlockSpec((tm, IN_WIDTH), lambda i: (i, 0)), pl.BlockSpec((tm, D_MODEL), lambda i: (i, 0)),
                  pl.BlockSpec((tm, D_MODEL), lambda i: (i, 0)), pl.BlockSpec((1, D_MODEL), lambda i: (0, 0)),
                  pl.BlockSpec((IN_WIDTH, D_MODEL), lambda i: (0, 0), pipeline_mode=pl.Buffered(1))],
        out_specs=(pl.BlockSpec((tm, D_MODEL), lambda i: (i, 0)),
                   pl.BlockSpec((IN_WIDTH, D_MODEL), lambda i: (0, 0), pipeline_mode=pl.Buffered(1)),
                   pl.BlockSpec((1, D_MODEL), lambda i: (0, 0))),
        scratch_shapes=[pltpu.VMEM((IN_WIDTH, D_MODEL), F32)],
        compiler_params=_cparams(1),
    )(dproj, x, dout, g_pre, w_in_t)


_PAIR = tuple(n // 2 + 4 * (n % 2) for n in range(HEADS))
_UNPAIR = tuple(_PAIR.index(n) for n in range(HEADS))


def _permute_head_rows(a, start, order):
    blocks = [a[..., start + HEAD_DIM * o:start + HEAD_DIM * (o + 1), :] for o in order]
    return jnp.concatenate([a[..., :start, :], *blocks, a[..., start + HEADS * HEAD_DIM:, :]], axis=-2)


def _pair_w_in_t(wt):
    return _permute_head_rows(_permute_head_rows(wt, Q_OFF, _PAIR), BZ_OFF, _PAIR)


def _unpair_w_in_t(wt):
    return _permute_head_rows(_permute_head_rows(wt, Q_OFF, _UNPAIR), BZ_OFF, _UNPAIR)


def _window_tables():
    q_loc = jnp.arange(CHUNK)[:, None]
    k_loc = jnp.arange(2 * CHUNK)[None, :]
    dist = q_loc + CHUNK - k_loc
    inwin = (dist >= 0) & (dist < CHUNK)
    d = jnp.maximum(dist, 0)
    max_exact = REL_BUCKETS // 2
    safe = jnp.maximum(d, 1).astype(F32)
    large = max_exact + (jnp.log(safe / max_exact) / math.log(CHUNK / max_exact)
                         * (REL_BUCKETS - max_exact)).astype(jnp.int32)
    large = jnp.minimum(large, REL_BUCKETS - 1)
    return jnp.where(d < max_exact, d, large).astype(jnp.int32), inwin.astype(jnp.int32)


def _local_step(x, target, seq_len, pre_g, w_in_p, w_in_t_p, ln_g, ln_b, w_spatial, b_spatial, sinks, rel_bias, w_out_p,
                post_g):
    n_layers = pre_g.shape[0]
    bucket, inwin = _window_tables()
    bias = _bias_table(rel_bias, bucket, inwin)
    wcat, wtcat, bs = _spatial_tables(w_spatial, jnp.swapaxes(b_spatial, 1, 2))
    xs, saved = [x], []
    for l in range(n_layers):
        a, q, kv, bz = _fwd_in(xs[-1], pre_g[l][None], w_in_p[l])
        xn, y = _fwd_mix(xs[-1], a, q, kv, bz, ln_g[l][None], ln_b[l][None], wcat[l], bs[l], sinks[l], bias, w_out_p[l],
                         post_g[l][None], seq_len)
        saved.append((a, q, kv, bz, y))
        xs.append(xn)
    dx, loss = _loss_head(xs[-1], target)
    grads = [None] * n_layers
    for l in reversed(range(n_layers)):
        a, q, kv, bz, y = saved[l]
        dproj, dwout, dws, dmix, dlng, dlnb, dgpost, dsink, dbias = _bwd_mix(
            dx, y, a, q, kv, bz, ln_g[l][None], ln_b[l][None], wcat[l], wtcat[l], bs[l], sinks[l], bias, w_out_p[l],
            post_g[l][None], seq_len)
        dx, dwt, dgpre = _bwd_in(dproj, xs[l], dx, pre_g[l][None], w_in_t_p[l])
        grads[l] = dict(w_in_t=dwt, w_out=dwout, w_spatial=dws, dmix=dmix, ln_g=dlng, ln_b=dlnb, post_g=dgpost,
                        sinks=dsink, dbias=dbias, pre_g=dgpre)
    return loss, dx, grads, (bucket, inwin)


N_DEV = 8
N_CHIPS = 4
MESH_ID = pl.DeviceIdType.MESH
ANY = pl.BlockSpec(memory_space=pl.ANY)


def _place():
    x, y, c = lax.axis_index("x"), lax.axis_index("y"), lax.axis_index("c")
    other_chips = [(1 - x, y), (x, 1 - y), (1 - x, 1 - y)]
    return x, y, c, other_chips


def _gather_weights(wt_shard, wo_shard):
    n_layers, wt_rows, _ = wt_shard.shape
    wo_rows = wo_shard.shape[1]

    def body(wt_ref, wo_ref, wt_full, wo_full, send_sems, recv_sems, local_sems):
        x, y, c, chips = _place()

        def rows(full, n_rows, chip):
            return full.at[:, pl.ds(pl.multiple_of((2 * chip[0] + chip[1]) * n_rows, 16), n_rows), :]

        arrays = ((wt_ref, wt_full, wt_rows), (wo_ref, wo_full, wo_rows))
        local = [pltpu.make_async_copy(src, rows(full, n, (x, y)), local_sems.at[a])
                 for a, (src, full, n) in enumerate(arrays)]
        for cp in local:
            cp.start()

        def copy(k, a, block_chip, to_chip):
            src, full, n = arrays[a]
            return pltpu.make_async_remote_copy(
                src_ref=src, dst_ref=rows(full, n, block_chip), send_sem=send_sems.at[2 * k + a],
                recv_sem=recv_sems.at[2 * k + a], device_id=(to_chip[0], to_chip[1], c), device_id_type=MESH_ID)

        sends = [copy(k, a, (x, y), chip) for k, chip in enumerate(chips) for a in range(2)]
        for cp in sends:
            cp.start()
        for k, chip in enumerate(chips):
            for a in range(2):
                copy(k, a, chip, (x, y)).wait_recv()
        for cp in sends:
            cp.wait_send()
        for cp in local:
            cp.wait()

    return pl.pallas_call(
        body, name="gather_weights",
        out_shape=(jax.ShapeDtypeStruct((n_layers, N_CHIPS * wt_rows, D_MODEL), BF16),
                   jax.ShapeDtypeStruct((n_layers, N_CHIPS * wo_rows, D_MODEL), BF16)),
        in_specs=[ANY, ANY], out_specs=(ANY, ANY),
        scratch_shapes=[pltpu.SemaphoreType.DMA((6,)), pltpu.SemaphoreType.DMA((6,)), pltpu.SemaphoreType.DMA((2,))],
    )(wt_shard, wo_shard)


def _exchange_grads(dwt, dwo):
    n_layers = dwt.shape[0]
    wt_rows, wo_rows = dwt.shape[1] // N_CHIPS, dwo.shape[1] // N_CHIPS

    def body(dwt_ref, dwo_ref, pt_ref, po_ref, send_sems, recv_sems, local_sems):
        x, y, c, chips = _place()
        me, sibling = (x, y, c), (x, y, 1 - c)
        arrays = ((dwt_ref, pt_ref, wt_rows), (dwo_ref, po_ref, wo_rows))

        def block(a, chip):
            src, _, n = arrays[a]
            return src.at[:, pl.ds(pl.multiple_of((2 * chip[0] + chip[1]) * n, 16), n), :]

        def slot(a, dev):
            return arrays[a][1].at[4 * dev[0] + 2 * dev[1] + dev[2]]

        def copy(k, a, src, origin, to):
            return pltpu.make_async_remote_copy(
                src_ref=src, dst_ref=slot(a, origin), send_sem=send_sems.at[2 * k + a], recv_sem=recv_sems.at[2 * k + a],
                device_id=to, device_id_type=MESH_ID)

        local = [pltpu.make_async_copy(block(a, (x, y)), slot(a, me), local_sems.at[a]) for a in range(2)]
        for cp in local:
            cp.start()
        first = [copy(0, a, block(a, (x, y)), me, sibling) for a in range(2)]
        first += [copy(1 + k, a, block(a, chip), me, (chip[0], chip[1], c)) for k, chip in enumerate(chips) for a in range(2)]
        for cp in first:
            cp.start()
        passed = []
        for k, chip in enumerate(chips):
            origin = (chip[0], chip[1], c)
            for a in range(2):
                copy(1 + k, a, slot(a, origin), origin, me).wait_recv()
                fwd = copy(4 + k, a, slot(a, origin), origin, sibling)
                fwd.start()
                passed.append(fwd)
        for a in range(2):
            copy(0, a, slot(a, sibling), sibling, me).wait_recv()
        for k, chip in enumerate(chips):
            origin = (chip[0], chip[1], 1 - c)
            for a in range(2):
                copy(4 + k, a, slot(a, origin), origin, me).wait_recv()
        for cp in first + passed:
            cp.wait_send()
        for cp in local:
            cp.wait()

    return pl.pallas_call(
        body, name="exchange_grads",
        out_shape=(jax.ShapeDtypeStruct((N_DEV, n_layers, wt_rows, D_MODEL), BF16),
                   jax.ShapeDtypeStruct((N_DEV, n_layers, wo_rows, D_MODEL), BF16)),
        in_specs=[ANY, ANY], out_specs=(ANY, ANY),
        scratch_shapes=[pltpu.SemaphoreType.DMA((14,)), pltpu.SemaphoreType.DMA((14,)), pltpu.SemaphoreType.DMA((2,))],
    )(dwt, dwo)


def _allreduce_small(part):
    n_rows = part.shape[0]

    def body(p_ref, tot_ref, all_ref, send_sems, recv_sems, local_sem):
        x, y, c, chips = _place()
        me, sibling = (x, y, c), (x, y, 1 - c)

        def rows(dev):
            return all_ref.at[pl.ds(pl.multiple_of((4 * dev[0] + 2 * dev[1] + dev[2]) * n_rows, 8), n_rows), :]

        def copy(k, origin, to, src=None):
            return pltpu.make_async_remote_copy(
                src_ref=rows(origin) if src is None else src, dst_ref=rows(origin), send_sem=send_sems.at[k],
                recv_sem=recv_sems.at[k], device_id=to, device_id_type=MESH_ID)

        mine = pltpu.make_async_copy(p_ref, rows(me), local_sem)
        mine.start()
        first = [copy(0, me, sibling, src=p_ref)]
        first += [copy(1 + k, me, (chip[0], chip[1], c), src=p_ref) for k, chip in enumerate(chips)]
        for cp in first:
            cp.start()
        passed = []
        for k, chip in enumerate(chips):
            origin = (chip[0], chip[1], c)
            copy(1 + k, origin, me).wait_recv()
            fwd = copy(4 + k, origin, sibling)
            fwd.start()
            passed.append(fwd)
        copy(0, sibling, me).wait_recv()
        for k, chip in enumerate(chips):
            copy(4 + k, (chip[0], chip[1], 1 - c), me).wait_recv()
        for cp in first + passed:
            cp.wait_send()
        mine.wait()
        tot = all_ref[0:n_rows, :]
        for d in range(1, N_DEV):
            tot = tot + all_ref[d * n_rows:(d + 1) * n_rows, :]
        tot_ref[...] = tot

    vmem = pl.BlockSpec(memory_space=pltpu.VMEM)
    return pl.pallas_call(
        body, name="allreduce_small",
        out_shape=jax.ShapeDtypeStruct((n_rows, 128), F32),
        in_specs=[vmem], out_specs=vmem,
        scratch_shapes=[pltpu.VMEM((N_DEV * n_rows, 128), F32), pltpu.SemaphoreType.DMA((7,)),
                        pltpu.SemaphoreType.DMA((7,)), pltpu.SemaphoreType.DMA],
        compiler_params=pltpu.CompilerParams(vmem_limit_bytes=VMEM_LIMIT_V7X),
    )(part)


def _sum_parts(parts):
    _, n_rows, n_cols = parts.shape
    tr = max(t for t in range(16, 705, 16) if n_rows % t == 0)

    def body(p_ref, out_ref):
        tot = p_ref[0].astype(F32)
        for d in range(1, N_DEV):
            tot = tot + p_ref[d].astype(F32)
        out_ref[...] = tot

    return pl.pallas_call(
        body, name="sum_parts", grid=(n_rows // tr,),
        out_shape=jax.ShapeDtypeStruct((n_rows, n_cols), F32),
        in_specs=[pl.BlockSpec((N_DEV, tr, n_cols), lambda i: (0, i, 0))],
        out_specs=pl.BlockSpec((tr, n_cols), lambda i: (i, 0)),
        compiler_params=_cparams(1),
    )(parts)


def _adamw(w, g, m, v, name):
    n_rows, n_cols = w.shape
    tr = 512 if n_rows % 512 == 0 else n_rows

    def body(w_ref, g_ref, m_ref, v_ref, d_ref, nm_ref, nv_ref):
        gv = g_ref[...]
        nm = ADAM_B1 * m_ref[...] + (1.0 - ADAM_B1) * gv
        nv = ADAM_B2 * v_ref[...] + (1.0 - ADAM_B2) * (gv * gv)
        m_hat = nm / (1.0 - ADAM_B1 ** ADAM_STEP)
        v_hat = nv / (1.0 - ADAM_B2 ** ADAM_STEP)
        d_ref[...] = -ADAM_LR * (m_hat / (jnp.sqrt(v_hat) + ADAM_EPS) + ADAM_WD * w_ref[...])
        nm_ref[...] = nm
        nv_ref[...] = nv

    spec = pl.BlockSpec((tr, n_cols), lambda i: (i, 0))
    shape = jax.ShapeDtypeStruct((n_rows, n_cols), F32)
    return pl.pallas_call(
        body, name="adamw_" + name, grid=(n_rows // tr,), out_shape=(shape, shape, shape),
        in_specs=[spec] * 4, out_specs=(spec, spec, spec), compiler_params=_cparams(1),
    )(w, g, m, v)


def _b_spatial_grad(dmix):
    n_layers = dmix.shape[0]

    def body(d_ref, out_ref):
        lane = lax.broadcasted_iota(jnp.int32, (CHUNK, CHUNK), 1)
        acc = jnp.zeros((CHUNK, CHUNK), F32)
        for p in range(4):
            t = d_ref[0, :, p * CHUNK:(p + 1) * CHUNK]
            s_lo = jnp.sum(jnp.where(lane < HEAD_DIM, t, 0.0), axis=1, keepdims=True)
            s_hi = jnp.sum(jnp.where(lane < HEAD_DIM, 0.0, t), axis=1, keepdims=True)
            acc = jnp.where(lane == 2 * p, s_lo, acc)
            acc = jnp.where(lane == 2 * p + 1, s_hi, acc)
        out_ref[0] = acc

    return pl.pallas_call(
        body, name="b_spatial_grad", grid=(n_layers,),
        out_shape=jax.ShapeDtypeStruct((n_layers, CHUNK, CHUNK), F32),
        in_specs=[pl.BlockSpec((1, CHUNK, A_WIDTH), lambda l: (l, 0, 0))],
        out_specs=pl.BlockSpec((1, CHUNK, CHUNK), lambda l: (l, 0, 0)),
        compiler_params=_cparams(1),
    )(dmix)


_SMALL = ("pre_norm_g", "ln_v_g", "ln_v_b", "w_spatial", "b_spatial", "sinks", "rel_bias", "post_norm_g")


def _pack(parts):
    flat = [p.reshape(-1) for p in parts]
    offsets, n = [], 0
    for p in flat:
        offsets.append(n)
        n += p.shape[0]
    pad = (-n) % (8 * 128)
    return jnp.concatenate(flat + [jnp.zeros((pad,), F32)]).reshape(-1, 128), offsets


def kernel(x, pre_norm_g, w_in, ln_v_g, ln_v_b, w_spatial, b_spatial, sinks, rel_bias, w_out, post_norm_g, loss_target, m_pre_norm_g, m_w_in, m_ln_v_g, m_ln_v_b, m_w_spatial, m_b_spatial, m_sinks, m_rel_bias, m_w_out, m_post_norm_g, v_pre_norm_g, v_w_in, v_ln_v_g, v_ln_v_b, v_w_spatial, v_b_spatial, v_sinks, v_rel_bias, v_w_out, v_post_norm_g):
    weights = dict(pre_norm_g=pre_norm_g, w_in=w_in, ln_v_g=ln_v_g, ln_v_b=ln_v_b, w_spatial=w_spatial, b_spatial=b_spatial,
                   sinks=sinks, rel_bias=rel_bias, w_out=w_out, post_norm_g=post_norm_g)
    mom_m = dict(pre_norm_g=m_pre_norm_g, w_in=m_w_in, ln_v_g=m_ln_v_g, ln_v_b=m_ln_v_b, w_spatial=m_w_spatial,
                 b_spatial=m_b_spatial, sinks=m_sinks, rel_bias=m_rel_bias, w_out=m_w_out, post_norm_g=m_post_norm_g)
    mom_v = dict(pre_norm_g=v_pre_norm_g, w_in=v_w_in, ln_v_g=v_ln_v_g, ln_v_b=v_ln_v_b, w_spatial=v_w_spatial,
                 b_spatial=v_b_spatial, sinks=v_sinks, rel_bias=v_rel_bias, w_out=v_w_out, post_norm_g=v_post_norm_g)
    n_seq, seq_len, _ = x.shape
    n_layers = w_in.shape[0]

    wt_full, wo_full = _gather_weights(jnp.swapaxes(w_in, 1, 2).astype(BF16), w_out.astype(BF16))
    wt_p = _pair_w_in_t(wt_full)
    w_in_p = jnp.swapaxes(wt_p, 1, 2)
    wo_p = _permute_head_rows(wo_full, A_WIDTH, _PAIR)

    loss, dx, grads, (bucket, inwin) = _local_step(
        x.reshape(n_seq * seq_len, D_MODEL), loss_target.reshape(n_seq * seq_len, D_MODEL), seq_len, pre_norm_g, w_in_p,
        wt_p, ln_v_g, ln_v_b, w_spatial, b_spatial, sinks, rel_bias, wo_p, post_norm_g)

    dwt = _unpair_w_in_t(jnp.stack([g["w_in_t"] for g in grads]))
    dwo = _permute_head_rows(jnp.stack([g["w_out"] for g in grads]), A_WIDTH, _UNPAIR).astype(BF16)
    parts_t, parts_o = _exchange_grads(dwt, dwo)
    g_wt = _sum_parts(parts_t.reshape(N_DEV, -1, D_MODEL)).reshape(n_layers, -1, D_MODEL)
    grad = {"w_in": jnp.swapaxes(g_wt, 1, 2),
            "w_out": _sum_parts(parts_o.reshape(N_DEV, -1, D_MODEL)).reshape(n_layers, -1, D_MODEL)}

    db = _b_spatial_grad(jnp.stack([g["dmix"] for g in grads]))
    pieces = {
        "pre_norm_g": jnp.stack([g["pre_g"][0] for g in grads]),
        "ln_v_g": jnp.stack([g["ln_g"][0] for g in grads]),
        "ln_v_b": jnp.stack([g["ln_b"][0] for g in grads]),
        "w_spatial": jnp.stack([g["w_spatial"] for g in grads]),
        "b_spatial": jnp.swapaxes(db[:, :, :HEADS], 1, 2),
        "sinks": jnp.stack([g["sinks"] for g in grads]),
        "rel_bias": _rel_bias_grad(jnp.stack([g["dbias"] for g in grads]), bucket, inwin),
        "post_norm_g": jnp.stack([g["post_g"][0] for g in grads]),
    }
    packed, offsets = _pack([pieces[n] for n in _SMALL] + [loss])
    total = _allreduce_small(packed).reshape(-1)
    for n, off in zip(_SMALL, offsets):
        grad[n] = total[off:off + weights[n].size].reshape(weights[n].shape)
    loss_out = total[offsets[-1]]

    delta, new_m, new_v = {}, {}, {}
    for n, w in weights.items():
        two_d = (-1, w.shape[-1])
        d, nm, nv = _adamw(w.reshape(two_d), grad[n].reshape(two_d), mom_m[n].reshape(two_d), mom_v[n].reshape(two_d), n)
        delta[n], new_m[n], new_v[n] = d.reshape(w.shape), nm.reshape(w.shape), nv.reshape(w.shape)

    names = tuple(weights)
    return (loss_out, dx.reshape(x.shape), *[grad[n] for n in names], *[delta[n] for n in names],
            *[new_m[n] for n in names], *[new_v[n] for n in names])
```

```python
import math

import jax
import jax.numpy as jnp
from jax import lax
from jax.experimental import pallas as pl
from jax.experimental.pallas import tpu as pltpu

F32 = jnp.float32
BF16 = jnp.bfloat16

D_MODEL = 1024
A_WIDTH = 512
B_WIDTH = 512
KV_WIDTH = 128
IN_WIDTH = 3 * A_WIDTH + 2 * B_WIDTH + 2 * KV_WIDTH
CHUNK = 128
HEADS = 8
HEAD_DIM = 64
REL_BUCKETS = 32
NORM_EPS = 1e-6
NEG = -1e30
Q_OFF = 3 * A_WIDTH
K_OFF = Q_OFF + B_WIDTH
BZ_OFF = K_OFF + 2 * KV_WIDTH
QK_SCALE = HEAD_DIM ** -0.5

ADAM_LR = 0.001
ADAM_B1 = 0.9
ADAM_B2 = 0.999
ADAM_EPS = 1e-08
ADAM_WD = 0.01
ADAM_STEP = 10

TOK_TILE = 512
VMEM_LIMIT_V7X = 60 * 1024 * 1024

N_DEV = 8
N_CHIPS = 4
MESH_ID = pl.DeviceIdType.MESH
ANY = pl.BlockSpec(memory_space=pl.ANY)
SMEM = pl.BlockSpec(memory_space=pltpu.SMEM)


def _cparams(n_axes):
    return pltpu.CompilerParams(dimension_semantics=("arbitrary",) * n_axes, vmem_limit_bytes=VMEM_LIMIT_V7X)


_GELU_C = math.sqrt(2.0 / math.pi)


def _gelu(x):
    inner = _GELU_C * (x + 0.044715 * (x * x * x))
    return x * (0.5 * (1.0 + jnp.tanh(inner)))


def _gelu_and_grad(x):
    x2 = x * x
    t = jnp.tanh(_GELU_C * (x + 0.044715 * (x2 * x)))
    cdf = 0.5 * (1.0 + t)
    d = cdf + x * (0.5 * (1.0 - t * t)) * (_GELU_C * (1.0 + 3.0 * 0.044715 * x2))
    return x * cdf, d


def _silu(x):
    return x * (1.0 / (1.0 + jnp.exp(-x)))


def _silu_and_grad(x):
    s = 1.0 / (1.0 + jnp.exp(-x))
    return x * s, s * (1.0 + x * (1.0 - s))


def _dot(a, b):
    return jnp.dot(a, b, preferred_element_type=F32)


def _dot_nt(a, b):
    return lax.dot_general(a, b, (((1,), (1,)), ((), ())), preferred_element_type=F32)


def _dot_tn(a, b):
    return lax.dot_general(a, b, (((0,), (0,)), ((), ())), preferred_element_type=F32)


def _lo_mask(shape):
    return lax.broadcasted_iota(jnp.int32, shape, 1) < HEAD_DIM


def _swap_halves(v):
    return pltpu.roll(v, HEAD_DIM, 1)


def _window_tables():
    q_loc = jnp.arange(CHUNK)[:, None]
    k_loc = jnp.arange(2 * CHUNK)[None, :]
    dist = q_loc + CHUNK - k_loc
    inwin = (dist >= 0) & (dist < CHUNK)
    d = jnp.maximum(dist, 0)
    max_exact = REL_BUCKETS // 2
    safe = jnp.maximum(d, 1).astype(F32)
    large = max_exact + (jnp.log(safe / max_exact) / math.log(CHUNK / max_exact)
                         * (REL_BUCKETS - max_exact)).astype(jnp.int32)
    large = jnp.minimum(large, REL_BUCKETS - 1)
    return jnp.where(d < max_exact, d, large).astype(jnp.int32), inwin.astype(jnp.int32)


def _bias_table(rel_bias, bucket, inwin):
    def body(rb_ref, bk_ref, win_ref, out_ref):
        bk = bk_ref[...]
        win = win_ref[...] > 0
        for h in range(HEADS):
            acc = jnp.zeros(bk.shape, F32)
            for b in range(REL_BUCKETS):
                acc = jnp.where(bk == b, rb_ref[b, h], acc)
            out_ref[h] = jnp.where(win, acc, NEG)

    vmem = pl.BlockSpec(memory_space=pltpu.VMEM)
    return pl.pallas_call(
        body, name="bias_table",
        out_shape=jax.ShapeDtypeStruct((HEADS, CHUNK, 2 * CHUNK), F32),
        in_specs=[SMEM, vmem, vmem], out_specs=vmem,
    )(rel_bias, bucket, inwin)


def _rel_bias_grad(dbias, bucket, inwin):
    n_layers = dbias.shape[0]

    def body(db_ref, bk_ref, win_ref, out_ref):
        bk = jnp.where(win_ref[...] > 0, bk_ref[...], -1)
        for h in range(HEADS):
            tot = db_ref[0, h]
            for l in range(1, n_layers):
                tot = tot + db_ref[l, h]
            for b in range(REL_BUCKETS):
                out_ref[b, h] = jnp.sum(jnp.where(bk == b, tot, 0.0))

    vmem = pl.BlockSpec(memory_space=pltpu.VMEM)
    return pl.pallas_call(
        body, name="rel_bias_grad",
        out_shape=jax.ShapeDtypeStruct((REL_BUCKETS, HEADS), F32),
        in_specs=[vmem] * 3, out_specs=SMEM,
    )(dbias, bucket, inwin)


def _spatial_tables(w_spatial, b_spatial_t):
    n_layers = w_spatial.shape[0]

    def body(w_ref, b_ref, wcat_ref, wtcat_ref, bs_ref):
        row = lax.broadcasted_iota(jnp.int32, (CHUNK, CHUNK), 0)
        col = lax.broadcasted_iota(jnp.int32, (CHUNK, CHUNK), 1)
        causal = col <= row
        lo = _lo_mask((CHUNK, CHUNK))
        for p in range(4):
            for half in range(2):
                w = jnp.where(causal, w_ref[0, 2 * p + half], 0.0)
                wcat_ref[0, p, :, half * CHUNK:(half + 1) * CHUNK] = w.astype(BF16)
                wtcat_ref[0, p, :, half * CHUNK:(half + 1) * CHUNK] = w.T.astype(BF16)
            b = b_ref[0]
            bs_ref[0, p] = jnp.where(lo, b[:, 2 * p:2 * p + 1], b[:, 2 * p + 1:2 * p + 2])

    return pl.pallas_call(
        body, name="spatial_tables", grid=(n_layers,),
        out_shape=(jax.ShapeDtypeStruct((n_layers, 4, CHUNK, 2 * CHUNK), BF16),
                   jax.ShapeDtypeStruct((n_layers, 4, CHUNK, 2 * CHUNK), BF16),
                   jax.ShapeDtypeStruct((n_layers, 4, CHUNK, CHUNK), F32)),
        in_specs=[pl.BlockSpec((1, HEADS, CHUNK, CHUNK), lambda l: (l, 0, 0, 0)),
                  pl.BlockSpec((1, CHUNK, HEADS), lambda l: (l, 0, 0))],
        out_specs=(pl.BlockSpec((1, 4, CHUNK, 2 * CHUNK), lambda l: (l, 0, 0, 0)),
                   pl.BlockSpec((1, 4, CHUNK, 2 * CHUNK), lambda l: (l, 0, 0, 0)),
                   pl.BlockSpec((1, 4, CHUNK, CHUNK), lambda l: (l, 0, 0, 0))),
        compiler_params=_cparams(1),
    )(w_spatial, b_spatial_t)


def _b_spatial_grad(dmix):
    n_layers = dmix.shape[0]

    def body(d_ref, out_ref):
        lane = lax.broadcasted_iota(jnp.int32, (CHUNK, CHUNK), 1)
        acc = jnp.zeros((CHUNK, CHUNK), F32)
        for p in range(4):
            t = d_ref[0, :, p * CHUNK:(p + 1) * CHUNK]
            s_lo = jnp.sum(jnp.where(lane < HEAD_DIM, t, 0.0), axis=1, keepdims=True)
            s_hi = jnp.sum(jnp.where(lane < HEAD_DIM, 0.0, t), axis=1, keepdims=True)
            acc = jnp.where(lane == 2 * p, s_lo, acc)
            acc = jnp.where(lane == 2 * p + 1, s_hi, acc)
        out_ref[0] = acc

    return pl.pallas_call(
        body, name="b_spatial_grad", grid=(n_layers,),
        out_shape=jax.ShapeDtypeStruct((n_layers, CHUNK, CHUNK), F32),
        in_specs=[pl.BlockSpec((1, CHUNK, A_WIDTH), lambda l: (l, 0, 0))],
        out_specs=pl.BlockSpec((1, CHUNK, CHUNK), lambda l: (l, 0, 0)),
        compiler_params=_cparams(1),
    )(dmix)


def _place():
    x, y, c = lax.axis_index("x"), lax.axis_index("y"), lax.axis_index("c")
    other_chips = [(1 - x, y), (x, 1 - y), (1 - x, 1 - y)]
    return x, y, c, other_chips


N_GATHER_SEMS = 12


def _gather_phases(shards, fulls, send_sems, recv_sems, local_sems):
    x, y, c, chips = _place()
    sibling = (x, y, 1 - c)
    n_arr = len(shards)

    def half_rows(a, chip, half):
        n = shards[a].shape[0]
        start = (2 * chip[0] + chip[1]) * n + half * (n // 2)
        return fulls[a].at[pl.ds(pl.multiple_of(start, 16), n // 2), :]

    def my_half(a):
        n = shards[a].shape[0]
        return shards[a].at[pl.ds(pl.multiple_of(c * (n // 2), 16), n // 2), :]

    def copy(k, a, src, chip, half, to):
        return pltpu.make_async_remote_copy(
            src_ref=src, dst_ref=half_rows(a, chip, half), send_sem=send_sems.at[n_arr * k + a],
            recv_sem=recv_sems.at[n_arr * k + a], device_id=to, device_id_type=MESH_ID)

    def local(a):
        n = shards[a].shape[0]
        mine = fulls[a].at[pl.ds(pl.multiple_of((2 * x + y) * n, 16), n), :]
        return pltpu.make_async_copy(shards[a], mine, local_sems.at[a])

    def first(k, a):
        return copy(k, a, my_half(a), (x, y), c, (chips[k][0], chips[k][1], c))

    def passed(k, a):
        return copy(3 + k, a, half_rows(a, chips[k], c), chips[k], c, sibling)

    def phase_a():
        for a in range(n_arr):
            local(a).start()
        for k in range(3):
            for a in range(n_arr):
                first(k, a).start()

    def phase_b():
        for k in range(3):
            for a in range(n_arr):
                copy(k, a, my_half(a), chips[k], c, sibling).wait_recv()
                passed(k, a).start()

    def phase_c():
        for k in range(3):
            for a in range(n_arr):
                copy(3 + k, a, my_half(a), chips[k], 1 - c, sibling).wait_recv()
        for k in range(3):
            for a in range(n_arr):
                first(k, a).wait_send()
                passed(k, a).wait_send()
        for a in range(n_arr):
            local(a).wait()

    return phase_a, phase_b, phase_c


N_EXCHANGE_SEMS = 14


def _exchange_phases(partials, parts, send_sems, recv_sems, local_sems):
    x, y, c, chips = _place()
    me, sibling = (x, y, c), (x, y, 1 - c)
    n_arr = len(partials)

    def block(a, chip):
        n = partials[a].shape[0] // N_CHIPS
        return partials[a].at[pl.ds(pl.multiple_of((2 * chip[0] + chip[1]) * n, 16), n), :]

    def slot(a, dev):
        return parts[a].at[4 * dev[0] + 2 * dev[1] + dev[2]]

    def copy(k, a, src, origin, to):
        return pltpu.make_async_remote_copy(
            src_ref=src, dst_ref=slot(a, origin), send_sem=send_sems.at[n_arr * k + a],
            recv_sem=recv_sems.at[n_arr * k + a], device_id=to, device_id_type=MESH_ID)

    def local(a):
        return pltpu.make_async_copy(block(a, (x, y)), slot(a, me), local_sems.at[a])

    def first(k, a):
        if k == 0:
            return copy(0, a, block(a, (x, y)), me, sibling)
        chip = chips[k - 1]
        return copy(k, a, block(a, chip), me, (chip[0], chip[1], c))

    def passed(k, a):
        origin = (chips[k][0], chips[k][1], c)
        return copy(4 + k, a, slot(a, origin), origin, sibling)

    def phase_a():
        for a in range(n_arr):
            local(a).start()
        for k in range(4):
            for a in range(n_arr):
                first(k, a).start()

    def phase_b():
        for k in range(3):
            for a in range(n_arr):
                copy(1 + k, a, block(a, (x, y)), (chips[k][0], chips[k][1], c), me).wait_recv()
                passed(k, a).start()

    def phase_c():
        for a in range(n_arr):
            copy(0, a, block(a, (x, y)), sibling, me).wait_recv()
        for k in range(3):
            for a in range(n_arr):
                copy(4 + k, a, block(a, (x, y)), (chips[k][0], chips[k][1], 1 - c), me).wait_recv()
        for k in range(4):
            for a in range(n_arr):
                first(k, a).wait_send()
        for k in range(3):
            for a in range(n_arr):
                passed(k, a).wait_send()
        for a in range(n_arr):
            local(a).wait()

    return phase_a, phase_b, phase_c


def _comm_scratch(n_sems):
    return [pltpu.SemaphoreType.DMA((n_sems,)), pltpu.SemaphoreType.DMA((n_sems,)), pltpu.SemaphoreType.DMA((2,))]


def _gather_weights(wt_shards, wo_shards, layer):
    wt_rows, wo_rows = wt_shards.shape[1], wo_shards.shape[1]

    def body(wt_ref, wo_ref, wt_full, wo_full, send_sems, recv_sems, local_sems):
        phases = _gather_phases((wt_ref.at[layer], wo_ref.at[layer]), (wt_full, wo_full), send_sems, recv_sems, local_sems)
        for phase in phases:
            phase()

    return pl.pallas_call(
        body, name="gather_weights",
        out_shape=(jax.ShapeDtypeStruct((N_CHIPS * wt_rows, D_MODEL), BF16),
                   jax.ShapeDtypeStruct((N_CHIPS * wo_rows, D_MODEL), BF16)),
        in_specs=[ANY, ANY], out_specs=(ANY, ANY), scratch_shapes=_comm_scratch(N_GATHER_SEMS),
    )(wt_shards, wo_shards)


def _exchange_grads(dwt, dwo):
    def body(dwt_ref, dwo_ref, pt_ref, po_ref, send_sems, recv_sems, local_sems):
        for phase in _exchange_phases((dwt_ref, dwo_ref), (pt_ref, po_ref), send_sems, recv_sems, local_sems):
            phase()

    return pl.pallas_call(
        body, name="exchange_grads",
        out_shape=(jax.ShapeDtypeStruct((N_DEV, dwt.shape[0] // N_CHIPS, D_MODEL), BF16),
                   jax.ShapeDtypeStruct((N_DEV, dwo.shape[0] // N_CHIPS, D_MODEL), BF16)),
        in_specs=[ANY, ANY], out_specs=(ANY, ANY), scratch_shapes=_comm_scratch(N_EXCHANGE_SEMS),
    )(dwt, dwo)


def _allreduce_small(part):
    n_rows = part.shape[0]

    def body(p_ref, tot_ref, all_ref, send_sems, recv_sems, local_sem):
        x, y, c, chips = _place()
        me, sibling = (x, y, c), (x, y, 1 - c)

        def rows(dev):
            return all_ref.at[pl.ds(pl.multiple_of((4 * dev[0] + 2 * dev[1] + dev[2]) * n_rows, 8), n_rows), :]

        def copy(k, origin, to, src=None):
            return pltpu.make_async_remote_copy(
                src_ref=rows(origin) if src is None else src, dst_ref=rows(origin), send_sem=send_sems.at[k],
                recv_sem=recv_sems.at[k], device_id=to, device_id_type=MESH_ID)

        mine = pltpu.make_async_copy(p_ref, rows(me), local_sem)
        mine.start()
        first = [copy(0, me, sibling, src=p_ref)]
        first += [copy(1 + k, me, (chip[0], chip[1], c), src=p_ref) for k, chip in enumerate(chips)]
        for cp in first:
            cp.start()
        passed = []
        for k, chip in enumerate(chips):
            origin = (chip[0], chip[1], c)
            copy(1 + k, origin, me).wait_recv()
            fwd = copy(4 + k, origin, sibling)
            fwd.start()
            passed.append(fwd)
        copy(0, sibling, me).wait_recv()
        for k, chip in enumerate(chips):
            copy(4 + k, (chip[0], chip[1], 1 - c), me).wait_recv()
        for cp in first + passed:
            cp.wait_send()
        mine.wait()
        tot = all_ref[0:n_rows, :]
        for d in range(1, N_DEV):
            tot = tot + all_ref[d * n_rows:(d + 1) * n_rows, :]
        tot_ref[...] = tot

    vmem = pl.BlockSpec(memory_space=pltpu.VMEM)
    return pl.pallas_call(
        body, name="allreduce_small",
        out_shape=jax.ShapeDtypeStruct((n_rows, 128), F32),
        in_specs=[vmem], out_specs=vmem,
        scratch_shapes=[pltpu.VMEM((N_DEV * n_rows, 128), F32), pltpu.SemaphoreType.DMA((7,)),
                        pltpu.SemaphoreType.DMA((7,)), pltpu.SemaphoreType.DMA],
        compiler_params=pltpu.CompilerParams(vmem_limit_bytes=VMEM_LIMIT_V7X),
    )(part)


def _hosted(phases, step, n_steps):
    phase_a, phase_b, phase_c = phases

    def at_start():
        pl.when(step == 0)(phase_a)

    def at_end():
        pl.when(step == n_steps // 2)(phase_b)
        pl.when(step == n_steps - 1)(phase_c)

    return at_start, at_end


def _fwd_in(x, g_pre, w_in_t):
    n_tok = x.shape[0]

    def body(x_ref, g_ref, w_ref, a_ref, q_ref, kv_ref, bz_ref):
        xf = x_ref[...]
        r = lax.rsqrt(jnp.mean(xf * xf, axis=-1, keepdims=True) + NORM_EPS)
        h = ((xf * r) * g_ref[...]).astype(BF16)
        a_ref[...] = _dot_nt(h, w_ref[0:Q_OFF, :])
        q_ref[...] = _dot_nt(h, w_ref[Q_OFF:K_OFF, :]).astype(BF16)
        kv_ref[...] = _dot_nt(h, w_ref[K_OFF:BZ_OFF, :]).astype(BF16)
        bz_ref[...] = _dot_nt(h, w_ref[BZ_OFF:IN_WIDTH, :])

    tm = TOK_TILE
    return pl.pallas_call(
        body, name="fwd_in", grid=(n_tok // tm,),
        out_shape=(jax.ShapeDtypeStruct((n_tok, Q_OFF), F32), jax.ShapeDtypeStruct((n_tok, B_WIDTH), BF16),
                   jax.ShapeDtypeStruct((n_tok, 2 * KV_WIDTH), BF16), jax.ShapeDtypeStruct((n_tok, B_WIDTH), F32)),
        in_specs=[pl.BlockSpec((tm, D_MODEL), lambda i: (i, 0)), pl.BlockSpec((1, D_MODEL), lambda i: (0, 0)),
                  pl.BlockSpec((IN_WIDTH, D_MODEL), lambda i: (0, 0))],
        out_specs=(pl.BlockSpec((tm, Q_OFF), lambda i: (i, 0)), pl.BlockSpec((tm, B_WIDTH), lambda i: (i, 0)),
                   pl.BlockSpec((tm, 2 * KV_WIDTH), lambda i: (i, 0)), pl.BlockSpec((tm, B_WIDTH), lambda i: (i, 0))),
        compiler_params=_cparams(1),
    )(x, g_pre, w_in_t)


def _layer_norm_stats(vv):
    mu = jnp.mean(vv, axis=-1, keepdims=True)
    xc = vv - mu
    rs = lax.rsqrt(jnp.mean(xc * xc, axis=-1, keepdims=True) + NORM_EPS)
    return xc * rs, rs


def _blockdiag(v, lo):
    zero = jnp.zeros_like(v)
    return jnp.concatenate([jnp.where(lo, v, zero), jnp.where(lo, zero, v)], axis=0)


def _softmax_sink(s, sink):
    m = jnp.maximum(jnp.max(s, axis=-1, keepdims=True), sink)
    e = jnp.exp(s - m)
    esink = jnp.exp(sink - m)
    den = jnp.sum(e, axis=-1, keepdims=True) + esink
    return e / den, esink / den


def _kv_rows(cur_ref, halo_ref, r0, c):
    prev_in_tile = cur_ref[pl.ds(pl.multiple_of(jnp.maximum(r0 - CHUNK, 0), CHUNK), CHUNK), :]
    prev = jnp.where(c == 0, halo_ref[...], prev_in_tile)
    kv2 = jnp.concatenate([prev, cur_ref[pl.ds(r0, CHUNK), :]], axis=0)
    k2, v2 = kv2[:, 0:KV_WIDTH], kv2[:, KV_WIDTH:2 * KV_WIDTH]
    return (k2, _swap_halves(k2)), (v2, _swap_halves(v2))


def _head_of(p, half):
    return 2 * p + half, int(half != p // 2)


def _fwd_mix(x, a, q, kv, bz, ln_g, ln_b, wcat, bs, sinks, bias, w_out, g_post, seq_len, gather=None):
    n_tok = x.shape[0]
    tb = TOK_TILE
    n_chunks = tb // CHUNK
    n_tiles = n_tok // tb
    n_in = 14

    def body(*refs):
        (x_ref, a_ref, q_ref, kv_ref, halo_ref, bz_ref, lng_ref, lnb_ref, wcat_ref, bs_ref, sink_ref, bias_ref,
         wout_ref, gpost_ref) = refs[:n_in]
        if gather is None:
            xn_ref, y_ref, ycat_s = refs[n_in:]
            at_start = at_end = lambda: None
        else:
            wts_ref, wos_ref, xn_ref, y_ref, wt_full, wo_full, ycat_s, send_sems, recv_sems, local_sems = refs[n_in:]
            phases = _gather_phases((wts_ref.at[gather[2]], wos_ref.at[gather[2]]), (wt_full, wo_full), send_sems,
                                    recv_sems, local_sems)
            at_start, at_end = _hosted(phases, pl.program_id(0), n_tiles)
        at_start()
        i = pl.program_id(0)
        lo = _lo_mask((CHUNK, CHUNK))
        kidx = lax.broadcasted_iota(jnp.int32, (CHUNK, 2 * CHUNK), 1)

        def chunk(c, carry):
            r0 = pl.multiple_of(c * CHUNK, CHUNK)
            rows = pl.ds(r0, CHUNK)
            u = _gelu(a_ref[rows, 0:A_WIDTH])
            vv = _gelu(a_ref[rows, A_WIDTH:2 * A_WIDTH])
            xhat, _ = _layer_norm_stats(vv)
            vnb = (xhat * lng_ref[...] + lnb_ref[...]).astype(BF16)
            for p in range(4):
                blk = slice(p * CHUNK, (p + 1) * CHUNK)
                mixed = _dot(wcat_ref[p], _blockdiag(vnb[:, blk], lo)) + bs_ref[p]
                sz = _silu(a_ref[rows, 2 * A_WIDTH + p * CHUNK:2 * A_WIDTH + (p + 1) * CHUNK])
                ycat_s[rows, blk] = ((u[:, blk] * mixed) * sz).astype(BF16)
            ks, vs = _kv_rows(kv_ref, halo_ref, r0, c)
            first = lax.rem(i * tb + r0, seq_len) == 0
            dead = kidx < jnp.where(first, CHUNK, 0)
            for p in range(4):
                blk = slice(p * CHUNK, (p + 1) * CHUNK)
                qt = q_ref[rows, blk]
                zero = jnp.zeros_like(qt)
                res = []
                for half in range(2):
                    head, sw = _head_of(p, half)
                    qm = jnp.where(lo, qt, zero) if half == 0 else jnp.where(lo, zero, qt)
                    s = _dot_nt(qm, ks[sw]) * QK_SCALE + bias_ref[head]
                    s = jnp.where(dead, NEG, s)
                    prob, _ = _softmax_sink(s, sink_ref[head])
                    res.append(_dot(prob.astype(BF16), vs[sw]))
                o = jnp.where(lo, res[0], res[1])
                ycat_s[rows, B_WIDTH + p * CHUNK:B_WIDTH + (p + 1) * CHUNK] = (o * _silu(bz_ref[rows, blk])).astype(BF16)
            return carry

        lax.fori_loop(0, n_chunks, chunk, 0)
        y = _dot(ycat_s[...], wout_ref[...])
        r = lax.rsqrt(jnp.mean(y * y, axis=-1, keepdims=True) + NORM_EPS)
        y_ref[...] = y
        xn_ref[...] = x_ref[...] + (y * r) * gpost_ref[...]
        at_end()

    tile = lambda w: pl.BlockSpec((tb, w), lambda i: (i, 0))
    whole = lambda shape: pl.BlockSpec(shape, lambda i: (0,) * len(shape))
    in_specs = [tile(D_MODEL), tile(Q_OFF), tile(B_WIDTH), tile(2 * KV_WIDTH),
                pl.BlockSpec((CHUNK, 2 * KV_WIDTH), lambda i: (jnp.maximum(i * n_chunks - 1, 0), 0)),
                tile(B_WIDTH), whole((1, A_WIDTH)), whole((1, A_WIDTH)), whole((4, CHUNK, 2 * CHUNK)),
                whole((4, CHUNK, CHUNK)), SMEM, whole((HEADS, CHUNK, 2 * CHUNK)), whole((D_MODEL, D_MODEL)),
                whole((1, D_MODEL))]
    out_shape = [jax.ShapeDtypeStruct((n_tok, D_MODEL), F32), jax.ShapeDtypeStruct((n_tok, D_MODEL), F32)]
    out_specs = [tile(D_MODEL), tile(D_MODEL)]
    scratch = [pltpu.VMEM((tb, D_MODEL), BF16)]
    args = [x, a, q, kv, kv, bz, ln_g, ln_b, wcat, bs, sinks, bias, w_out, g_post]
    if gather is not None:
        wt_shards, wo_shards, _ = gather
        in_specs += [ANY, ANY]
        args += [wt_shards, wo_shards]
        out_shape += [jax.ShapeDtypeStruct((N_CHIPS * wt_shards.shape[1], D_MODEL), BF16),
                      jax.ShapeDtypeStruct((N_CHIPS * wo_shards.shape[1], D_MODEL), BF16)]
        out_specs += [ANY, ANY]
        scratch += _comm_scratch(N_GATHER_SEMS)
    return pl.pallas_call(
        body, name="fwd_mix" if gather is None else "fwd_mix_gather", grid=(n_tiles,),
        out_shape=tuple(out_shape), in_specs=in_specs, out_specs=tuple(out_specs), scratch_shapes=scratch,
        compiler_params=_cparams(1),
    )(*args)


def _loss_head(x, target):
    n_tok = x.shape[0]
    tm = TOK_TILE

    def body(x_ref, t_ref, dx_ref, loss_ref):
        d = x_ref[...] - t_ref[...]
        dx_ref[...] = d * (1.0 / D_MODEL)
        part = 0.5 * jnp.sum(jnp.mean(d * d, axis=-1, keepdims=True))

        @pl.when(pl.program_id(0) == 0)
        def _():
            loss_ref[0, 0] = 0.0

        loss_ref[0, 0] += part

    return pl.pallas_call(
        body, name="loss_head", grid=(n_tok // tm,),
        out_shape=(jax.ShapeDtypeStruct((n_tok, D_MODEL), F32), jax.ShapeDtypeStruct((1, 1), F32)),
        in_specs=[pl.BlockSpec((tm, D_MODEL), lambda i: (i, 0))] * 2,
        out_specs=(pl.BlockSpec((tm, D_MODEL), lambda i: (i, 0)), SMEM),
        compiler_params=_cparams(1),
    )(x, target)


def _bwd_mix(dout, y, a, q, kv, bz, ln_g, ln_b, wcat, wtcat, bs, sinks, bias, w_out, g_post, seq_len, exchange=None):
    n_tok = y.shape[0]
    tb = TOK_TILE
    n_chunks = tb // CHUNK
    n_tiles = n_tok // tb
    n_in, n_out = 16, 9

    def body(*refs):
        (dout_ref, y_ref, a_ref, q_ref, kv_ref, halo_ref, bz_ref, lng_ref, lnb_ref, wcat_ref, wtcat_ref, bs_ref,
         sink_ref, bias_ref, wout_ref, gpost_ref) = refs[:n_in]
        if exchange is None:
            outs, rest = refs[n_in:n_in + n_out], refs[n_in + n_out:]
            at_start = at_end = lambda: None
        else:
            dwt_ref, dwo_ref = refs[n_in:n_in + 2]
            outs = refs[n_in + 2:n_in + 2 + n_out]
            pt_ref, po_ref = refs[n_in + 2 + n_out:n_in + 4 + n_out]
            rest = refs[n_in + 4 + n_out:]
            phases = _exchange_phases((dwt_ref, dwo_ref), (pt_ref, po_ref), *rest[6:])
            at_start, at_end = _hosted(phases, pl.program_id(0), n_tiles)
        dproj_ref, dwout_ref, dws_ref, dmix_ref, dlng_ref, dlnb_ref, dgpost_ref, dsink_ref, dbias_ref = outs
        ycat_s, dy_s, dyc_s, dkv_s, carry_s, dwout_s = rest[:6]
        at_start()
        step = pl.program_id(0)
        tile_idx = n_tiles - 1 - step
        lo = _lo_mask((CHUNK, CHUNK))
        kidx = lax.broadcasted_iota(jnp.int32, (CHUNK, 2 * CHUNK), 1)

        @pl.when(step == 0)
        def _():
            dwout_s[...] = jnp.zeros_like(dwout_s)
            dws_ref[...] = jnp.zeros_like(dws_ref)
            dmix_ref[...] = jnp.zeros_like(dmix_ref)
            dlng_ref[...] = jnp.zeros_like(dlng_ref)
            dlnb_ref[...] = jnp.zeros_like(dlnb_ref)
            dgpost_ref[...] = jnp.zeros_like(dgpost_ref)
            dbias_ref[...] = jnp.zeros_like(dbias_ref)
            carry_s[...] = jnp.zeros_like(carry_s)
            for h in range(HEADS):
                dsink_ref[h] = 0.0

        yv = y_ref[...]
        dout = dout_ref[...]
        r = lax.rsqrt(jnp.mean(yv * yv, axis=-1, keepdims=True) + NORM_EPS)
        yn = yv * r
        dgpost_ref[...] += jnp.sum(dout * yn, axis=0, keepdims=True)
        dyn = dout * gpost_ref[...]
        dy = r * (dyn - yn * jnp.mean(dyn * yn, axis=-1, keepdims=True))
        dy_s[...] = dy.astype(BF16)
        dyc_s[...] = _dot_nt(dy_s[...], wout_ref[...])
        dkv_s[0:tb, :] = jnp.zeros((tb, 2 * KV_WIDTH), F32)
        dkv_s[tb:tb + CHUNK, :] = carry_s[...]

        def chunk(c, carry):
            r0 = pl.multiple_of(c * CHUNK, CHUNK)
            rows = pl.ds(r0, CHUNK)
            u, gu = _gelu_and_grad(a_ref[rows, 0:A_WIDTH])
            vv, gv = _gelu_and_grad(a_ref[rows, A_WIDTH:2 * A_WIDTH])
            xhat, rs = _layer_norm_stats(vv)
            vnb = (xhat * lng_ref[...] + lnb_ref[...]).astype(BF16)
            d_vn, d_u, d_az = [], [], []
            for p in range(4):
                blk = slice(p * CHUNK, (p + 1) * CHUNK)
                vbd = _blockdiag(vnb[:, blk], lo)
                mixed = _dot(wcat_ref[p], vbd) + bs_ref[p]
                sz, gz = _silu_and_grad(a_ref[rows, 2 * A_WIDTH + p * CHUNK:2 * A_WIDTH + (p + 1) * CHUNK])
                ub = u[:, blk]
                dya = dyc_s[rows, blk]
                um = ub * mixed
                ycat_s[rows, blk] = (um * sz).astype(BF16)
                d_mixed = (dya * ub) * sz
                d_u.append((dya * mixed) * sz)
                d_az.append((dya * um) * gz)
                dmix_ref[:, blk] += d_mixed
                dmbd = _blockdiag(d_mixed.astype(BF16), lo)
                d_vn.append(_dot(wtcat_ref[p], dmbd))
                dws_ref[p] += _dot_nt(dmbd, vnb[:, blk])
            d_vn = jnp.concatenate(d_vn, axis=1)
            dlng_ref[...] += jnp.sum(d_vn * xhat, axis=0, keepdims=True)
            dlnb_ref[...] += jnp.sum(d_vn, axis=0, keepdims=True)
            dxh = d_vn * lng_ref[...]
            d_vv = rs * (dxh - jnp.mean(dxh, axis=-1, keepdims=True)
                         - xhat * jnp.mean(dxh * xhat, axis=-1, keepdims=True))
            dproj_ref[rows, 0:A_WIDTH] = (jnp.concatenate(d_u, axis=1) * gu).astype(BF16)
            dproj_ref[rows, A_WIDTH:2 * A_WIDTH] = (d_vv * gv).astype(BF16)
            dproj_ref[rows, 2 * A_WIDTH:Q_OFF] = jnp.concatenate(d_az, axis=1).astype(BF16)
            ks, vs = _kv_rows(kv_ref, halo_ref, r0, c)
            first = lax.rem(tile_idx * tb + r0, seq_len) == 0
            dead = kidx < jnp.where(first, CHUNK, 0)
            dk_acc = [jnp.zeros((2 * CHUNK, KV_WIDTH), F32), jnp.zeros((2 * CHUNK, KV_WIDTH), F32)]
            dv_acc = [jnp.zeros((2 * CHUNK, KV_WIDTH), F32), jnp.zeros((2 * CHUNK, KV_WIDTH), F32)]
            pending = {0: [], 1: []}
            for p in range(4):
                blk = slice(p * CHUNK, (p + 1) * CHUNK)
                qt = q_ref[rows, blk]
                zero = jnp.zeros_like(qt)
                sb, gb = _silu_and_grad(bz_ref[rows, blk])
                dyb = dyc_s[rows, B_WIDTH + p * CHUNK:B_WIDTH + (p + 1) * CHUNK]
                qms, probs, psinks, outs_p = [], [], [], []
                for half in range(2):
                    head, sw = _head_of(p, half)
                    qm = jnp.where(lo, qt, zero) if half == 0 else jnp.where(lo, zero, qt)
                    s = _dot_nt(qm, ks[sw]) * QK_SCALE + bias_ref[head]
                    s = jnp.where(dead, NEG, s)
                    prob, psink = _softmax_sink(s, sink_ref[head])
                    qms.append(qm)
                    probs.append(prob)
                    psinks.append(psink)
                    outs_p.append(_dot(prob.astype(BF16), vs[sw]))
                o = jnp.where(lo, outs_p[0], outs_p[1])
                ycat_s[rows, B_WIDTH + p * CHUNK:B_WIDTH + (p + 1) * CHUNK] = (o * sb).astype(BF16)
                dproj_ref[rows, BZ_OFF + p * CHUNK:BZ_OFF + (p + 1) * CHUNK] = ((dyb * o) * gb).astype(BF16)
                d_ob = (dyb * sb).astype(BF16)
                zero_o = jnp.zeros_like(d_ob)
                dqs = []
                for half in range(2):
                    head, sw = _head_of(p, half)
                    dom = jnp.where(lo, d_ob, zero_o) if half == 0 else jnp.where(lo, zero_o, d_ob)
                    dp = _dot_nt(dom, vs[sw])
                    delta = jnp.sum(probs[half] * dp, axis=-1, keepdims=True)
                    ds = probs[half] * (dp - delta)
                    dsink_ref[head] += -jnp.sum(psinks[half] * delta)
                    dbias_ref[head] += ds
                    dsb = ds.astype(BF16)
                    dqs.append(_dot(dsb, ks[sw]))
                    pending[sw].append((dsb, qms[half], probs[half].astype(BF16), dom))
                    if len(pending[sw]) == 2:
                        (d1, q1, p1, o1), (d2, q2, p2, o2) = pending[sw]
                        dk_acc[sw] = dk_acc[sw] + _dot_tn(jnp.concatenate([d1, d2], axis=0),
                                                          jnp.concatenate([q1, q2], axis=0))
                        dv_acc[sw] = dv_acc[sw] + _dot_tn(jnp.concatenate([p1, p2], axis=0),
                                                          jnp.concatenate([o1, o2], axis=0))
                        pending[sw] = []
                dq = jnp.where(lo, dqs[0], dqs[1]) * QK_SCALE
                dproj_ref[rows, Q_OFF + p * CHUNK:Q_OFF + (p + 1) * CHUNK] = dq.astype(BF16)
            both = pl.ds(r0, 2 * CHUNK)
            dkv_s[both, 0:KV_WIDTH] += (dk_acc[0] + _swap_halves(dk_acc[1])) * QK_SCALE
            dkv_s[both, KV_WIDTH:2 * KV_WIDTH] += dv_acc[0] + _swap_halves(dv_acc[1])
            return carry

        lax.fori_loop(0, n_chunks, chunk, 0)
        dwout_s[...] += _dot_tn(ycat_s[...], dy_s[...])
        dproj_ref[:, K_OFF:BZ_OFF] = dkv_s[CHUNK:CHUNK + tb, :].astype(BF16)
        carry_s[...] = dkv_s[0:CHUNK, :]

        @pl.when(step == n_tiles - 1)
        def _():
            row = lax.broadcasted_iota(jnp.int32, (2 * CHUNK, CHUNK), 0)
            col = lax.broadcasted_iota(jnp.int32, (2 * CHUNK, CHUNK), 1)
            causal = col <= jnp.where(row >= CHUNK, row - CHUNK, row)
            for p in range(4):
                dws_ref[p] = jnp.where(causal, dws_ref[p], 0.0)
            dwout_ref[...] = dwout_s[...].astype(BF16)

        at_end()

    tile = lambda w: pl.BlockSpec((tb, w), lambda s: (n_tiles - 1 - s, 0))
    whole = lambda shape: pl.BlockSpec(shape, lambda s: (0,) * len(shape))
    in_specs = [tile(D_MODEL), tile(D_MODEL), tile(Q_OFF), tile(B_WIDTH), tile(2 * KV_WIDTH),
                pl.BlockSpec((CHUNK, 2 * KV_WIDTH), lambda s: (jnp.maximum((n_tiles - 1 - s) * n_chunks - 1, 0), 0)),
                tile(B_WIDTH), whole((1, A_WIDTH)), whole((1, A_WIDTH)), whole((4, CHUNK, 2 * CHUNK)),
                whole((4, CHUNK, 2 * CHUNK)), whole((4, CHUNK, CHUNK)), SMEM, whole((HEADS, CHUNK, 2 * CHUNK)),
                whole((D_MODEL, D_MODEL)), whole((1, D_MODEL))]
    out_shape = [jax.ShapeDtypeStruct((n_tok, IN_WIDTH), BF16), jax.ShapeDtypeStruct((D_MODEL, D_MODEL), BF16),
                 jax.ShapeDtypeStruct((4, 2 * CHUNK, CHUNK), F32), jax.ShapeDtypeStruct((CHUNK, A_WIDTH), F32),
                 jax.ShapeDtypeStruct((1, A_WIDTH), F32), jax.ShapeDtypeStruct((1, A_WIDTH), F32),
                 jax.ShapeDtypeStruct((1, D_MODEL), F32), jax.ShapeDtypeStruct((HEADS,), F32),
                 jax.ShapeDtypeStruct((HEADS, CHUNK, 2 * CHUNK), F32)]
    out_specs = [tile(IN_WIDTH), whole((D_MODEL, D_MODEL)), whole((4, 2 * CHUNK, CHUNK)), whole((CHUNK, A_WIDTH)),
                 whole((1, A_WIDTH)), whole((1, A_WIDTH)), whole((1, D_MODEL)), SMEM, whole((HEADS, CHUNK, 2 * CHUNK))]
    scratch = [pltpu.VMEM((tb, D_MODEL), BF16), pltpu.VMEM((tb, D_MODEL), BF16), pltpu.VMEM((tb, D_MODEL), F32),
               pltpu.VMEM((tb + CHUNK, 2 * KV_WIDTH), F32), pltpu.VMEM((CHUNK, 2 * KV_WIDTH), F32),
               pltpu.VMEM((D_MODEL, D_MODEL), F32)]
    args = [dout, y, a, q, kv, kv, bz, ln_g, ln_b, wcat, wtcat, bs, sinks, bias, w_out, g_post]
    if exchange is not None:
        dwt, dwo = exchange
        in_specs += [ANY, ANY]
        args += [dwt, dwo]
        out_shape += [jax.ShapeDtypeStruct((N_DEV, dwt.shape[0] // N_CHIPS, D_MODEL), BF16),
                      jax.ShapeDtypeStruct((N_DEV, dwo.shape[0] // N_CHIPS, D_MODEL), BF16)]
        out_specs += [ANY, ANY]
        scratch += _comm_scratch(N_EXCHANGE_SEMS)
    return pl.pallas_call(
        body, name="bwd_mix" if exchange is None else "bwd_mix_exchange", grid=(n_tiles,),
        out_shape=tuple(out_shape), in_specs=in_specs, out_specs=tuple(out_specs), scratch_shapes=scratch,
        compiler_params=_cparams(1),
    )(*args)


def _bwd_in(dproj, x, dout, g_pre, w_in_t):
    n_tok = x.shape[0]
    tm = TOK_TILE
    n_tiles = n_tok // tm

    def body(dp_ref, x_ref, dout_ref, g_ref, wt_ref, dx_ref, dwt_ref, dg_ref, acc_s):
        i = pl.program_id(0)

        @pl.when(i == 0)
        def _():
            acc_s[...] = jnp.zeros_like(acc_s)
            dg_ref[...] = jnp.zeros_like(dg_ref)

        xf = x_ref[...]
        r = lax.rsqrt(jnp.mean(xf * xf, axis=-1, keepdims=True) + NORM_EPS)
        xn = xf * r
        h = (xn * g_ref[...]).astype(BF16)
        dp = dp_ref[...]
        dh = _dot(dp, wt_ref[...])
        acc_s[...] += _dot_tn(dp, h)
        dg_ref[...] += jnp.sum(dh * xn, axis=0, keepdims=True)
        dhn = dh * g_ref[...]
        dx_ref[...] = dout_ref[...] + r * (dhn - xn * jnp.mean(dhn * xn, axis=-1, keepdims=True))

        @pl.when(i == n_tiles - 1)
        def _():
            dwt_ref[...] = acc_s[...].astype(BF16)

    return pl.pallas_call(
        body, name="bwd_in", grid=(n_tiles,),
        out_shape=(jax.ShapeDtypeStruct((n_tok, D_MODEL), F32), jax.ShapeDtypeStruct((IN_WIDTH, D_MODEL), BF16),
                   jax.ShapeDtypeStruct((1, D_MODEL), F32)),
        in_specs=[pl.BlockSpec((tm, IN_WIDTH), lambda i: (i, 0)), pl.BlockSpec((tm, D_MODEL), lambda i: (i, 0)),
                  pl.BlockSpec((tm, D_MODEL), lambda i: (i, 0)), pl.BlockSpec((1, D_MODEL), lambda i: (0, 0)),
                  pl.BlockSpec((IN_WIDTH, D_MODEL), lambda i: (0, 0), pipeline_mode=pl.Buffered(1))],
        out_specs=(pl.BlockSpec((tm, D_MODEL), lambda i: (i, 0)),
                   pl.BlockSpec((IN_WIDTH, D_MODEL), lambda i: (0, 0), pipeline_mode=pl.Buffered(1)),
                   pl.BlockSpec((1, D_MODEL), lambda i: (0, 0))),
        scratch_shapes=[pltpu.VMEM((IN_WIDTH, D_MODEL), F32)],
        compiler_params=_cparams(1),
    )(dproj, x, dout, g_pre, w_in_t)


def _adam_update(w, g, m, v):
    nm = ADAM_B1 * m + (1.0 - ADAM_B1) * g
    nv = ADAM_B2 * v + (1.0 - ADAM_B2) * (g * g)
    m_hat = nm / (1.0 - ADAM_B1 ** ADAM_STEP)
    v_hat = nv / (1.0 - ADAM_B2 ** ADAM_STEP)
    return -ADAM_LR * (m_hat / (jnp.sqrt(v_hat) + ADAM_EPS) + ADAM_WD * w), nm, nv


def _sum_slots(p_ref):
    tot = p_ref[0].astype(F32)
    for d in range(1, N_DEV):
        tot = tot + p_ref[d].astype(F32)
    return tot


def _adamw_w_in(parts, w, m, v):
    n_in, n_cols = w.shape
    tr = 256

    def body(p_ref, w_ref, m_ref, v_ref, g_ref, d_ref, nm_ref, nv_ref):
        g = _sum_slots(p_ref).T
        g_ref[...] = g
        d_ref[...], nm_ref[...], nv_ref[...] = _adam_update(w_ref[...], g, m_ref[...], v_ref[...])

    spec = pl.BlockSpec((tr, n_cols), lambda i: (i, 0))
    shape = jax.ShapeDtypeStruct((n_in, n_cols), F32)
    return pl.pallas_call(
        body, name="adamw_w_in", grid=(n_in // tr,), out_shape=(shape,) * 4,
        in_specs=[pl.BlockSpec((N_DEV, n_cols, tr), lambda i: (0, 0, i)), spec, spec, spec], out_specs=(spec,) * 4,
        compiler_params=_cparams(1),
    )(parts, w, m, v)


def _adamw_w_out(parts, w, m, v):
    def body(p_ref, w_ref, m_ref, v_ref, g_ref, d_ref, nm_ref, nv_ref):
        g = _sum_slots(p_ref)
        g_ref[...] = g
        d_ref[...], nm_ref[...], nv_ref[...] = _adam_update(w_ref[...], g, m_ref[...], v_ref[...])

    vmem = pl.BlockSpec(memory_space=pltpu.VMEM)
    shape = jax.ShapeDtypeStruct(w.shape, F32)
    return pl.pallas_call(
        body, name="adamw_w_out", out_shape=(shape,) * 4, in_specs=[vmem] * 4, out_specs=(vmem,) * 4,
        compiler_params=pltpu.CompilerParams(vmem_limit_bytes=VMEM_LIMIT_V7X),
    )(parts, w, m, v)


def _adamw(w, g, m, v, name):
    n_rows, n_cols = w.shape
    tr = 512 if n_rows % 512 == 0 else n_rows

    def body(w_ref, g_ref, m_ref, v_ref, d_ref, nm_ref, nv_ref):
        d_ref[...], nm_ref[...], nv_ref[...] = _adam_update(w_ref[...], g_ref[...], m_ref[...], v_ref[...])

    spec = pl.BlockSpec((tr, n_cols), lambda i: (i, 0))
    shape = jax.ShapeDtypeStruct((n_rows, n_cols), F32)
    return pl.pallas_call(
        body, name="adamw_" + name, grid=(n_rows // tr,), out_shape=(shape, shape, shape),
        in_specs=[spec] * 4, out_specs=(spec, spec, spec), compiler_params=_cparams(1),
    )(w, g, m, v)


_SMALL = ("pre_norm_g", "ln_v_g", "ln_v_b", "w_spatial", "b_spatial", "sinks", "rel_bias", "post_norm_g")


def _pack(parts):
    flat = [p.reshape(-1) for p in parts]
    offsets, n = [], 0
    for p in flat:
        offsets.append(n)
        n += p.shape[0]
    pad = (-n) % (8 * 128)
    return jnp.concatenate(flat + [jnp.zeros((pad,), F32)]).reshape(-1, 128), offsets


def kernel(x, pre_norm_g, w_in, ln_v_g, ln_v_b, w_spatial, b_spatial, sinks, rel_bias, w_out, post_norm_g, loss_target, m_pre_norm_g, m_w_in, m_ln_v_g, m_ln_v_b, m_w_spatial, m_b_spatial, m_sinks, m_rel_bias, m_w_out, m_post_norm_g, v_pre_norm_g, v_w_in, v_ln_v_g, v_ln_v_b, v_w_spatial, v_b_spatial, v_sinks, v_rel_bias, v_w_out, v_post_norm_g):
    weights = dict(pre_norm_g=pre_norm_g, w_in=w_in, ln_v_g=ln_v_g, ln_v_b=ln_v_b, w_spatial=w_spatial, b_spatial=b_spatial,
                   sinks=sinks, rel_bias=rel_bias, w_out=w_out, post_norm_g=post_norm_g)
    mom_m = dict(pre_norm_g=m_pre_norm_g, w_in=m_w_in, ln_v_g=m_ln_v_g, ln_v_b=m_ln_v_b, w_spatial=m_w_spatial,
                 b_spatial=m_b_spatial, sinks=m_sinks, rel_bias=m_rel_bias, w_out=m_w_out, post_norm_g=m_post_norm_g)
    mom_v = dict(pre_norm_g=v_pre_norm_g, w_in=v_w_in, ln_v_g=v_ln_v_g, ln_v_b=v_ln_v_b, w_spatial=v_w_spatial,
                 b_spatial=v_b_spatial, sinks=v_sinks, rel_bias=v_rel_bias, w_out=v_w_out, post_norm_g=v_post_norm_g)
    n_seq, seq_len, _ = x.shape
    n_layers = w_in.shape[0]
    x2 = x.reshape(n_seq * seq_len, D_MODEL)
    target2 = loss_target.reshape(n_seq * seq_len, D_MODEL)
    row = lambda p, l: p[l][None]

    wt_shards = jnp.swapaxes(w_in, 1, 2).astype(BF16)
    wo_shards = w_out.astype(BF16)
    bucket, inwin = _window_tables()
    bias = _bias_table(rel_bias, bucket, inwin)
    wcat, wtcat, bs = _spatial_tables(w_spatial, jnp.swapaxes(b_spatial, 1, 2))

    wt, wo = [None] * n_layers, [None] * n_layers
    wt[0], wo[0] = _gather_weights(wt_shards, wo_shards, 0)
    xs, saved = [x2], []
    for l in range(n_layers):
        a, q, kv, bz = _fwd_in(xs[-1], row(pre_norm_g, l), wt[l])
        mix_args = (xs[-1], a, q, kv, bz, row(ln_v_g, l), row(ln_v_b, l), wcat[l], bs[l], sinks[l], bias, wo[l],
                    row(post_norm_g, l), seq_len)
        if l + 1 < n_layers:
            xn, y, wt[l + 1], wo[l + 1] = _fwd_mix(*mix_args, gather=(wt_shards, wo_shards, l + 1))
        else:
            xn, y = _fwd_mix(*mix_args)
        saved.append((a, q, kv, bz, y))
        xs.append(xn)
    dx, loss = _loss_head(xs[-1], target2)

    small = [None] * n_layers
    parts = [None] * n_layers
    waiting = None
    for l in reversed(range(n_layers)):
        a, q, kv, bz, y = saved[l]
        outs = _bwd_mix(dx, y, a, q, kv, bz, row(ln_v_g, l), row(ln_v_b, l), wcat[l], wtcat[l], bs[l], sinks[l], bias,
                        wo[l], row(post_norm_g, l), seq_len, exchange=waiting)
        dproj, dwo, dws, dmix, dlng, dlnb, dgpost, dsink, dbias = outs[:9]
        if waiting is not None:
            parts[l + 1] = outs[9:]
        dx, dwt, dgpre = _bwd_in(dproj, xs[l], dx, row(pre_norm_g, l), wt[l])
        waiting = (dwt, dwo)
        small[l] = dict(pre_norm_g=dgpre[0], ln_v_g=dlng[0], ln_v_b=dlnb[0], w_spatial=dws, dmix=dmix, sinks=dsink,
                        dbias=dbias, post_norm_g=dgpost[0])
    parts[0] = _exchange_grads(*waiting)

    db = _b_spatial_grad(jnp.stack([s["dmix"] for s in small]))
    pieces = {n: jnp.stack([s[n] for s in small]) for n in ("pre_norm_g", "ln_v_g", "ln_v_b", "w_spatial", "sinks", "post_norm_g")}
    pieces["b_spatial"] = jnp.swapaxes(db[:, :, :HEADS], 1, 2)
    pieces["rel_bias"] = _rel_bias_grad(jnp.stack([s["dbias"] for s in small]), bucket, inwin)
    packed, offsets = _pack([pieces[n] for n in _SMALL] + [loss])
    total = _allreduce_small(packed).reshape(-1)
    grad = {n: total[off:off + weights[n].size].reshape(weights[n].shape) for n, off in zip(_SMALL, offsets)}
    loss_out = total[offsets[-1]]

    delta, new_m, new_v = {}, {}, {}
    for n in _SMALL:
        w = weights[n]
        two_d = (-1, w.shape[-1])
        d, nm, nv = _adamw(w.reshape(two_d), grad[n].reshape(two_d), mom_m[n].reshape(two_d), mom_v[n].reshape(two_d), n)
        delta[n], new_m[n], new_v[n] = d.reshape(w.shape), nm.reshape(w.shape), nv.reshape(w.shape)
    for n, step, which in (("w_in", _adamw_w_in, 0), ("w_out", _adamw_w_out, 1)):
        per_layer = [step(parts[l][which], weights[n][l], mom_m[n][l], mom_v[n][l]) for l in range(n_layers)]
        grad[n], delta[n], new_m[n], new_v[n] = (jnp.stack([o[j] for o in per_layer]) for j in range(4))

    names = tuple(weights)
    return (loss_out, dx.reshape(x.shape), *[grad[n] for n in names], *[delta[n] for n in names],
            *[new_m[n] for n in names], *[new_v[n] for n in names])
```

```python
import math

import jax
import jax.numpy as jnp
from jax import lax
from jax.experimental import pallas as pl
from jax.experimental.pallas import tpu as pltpu

F32 = jnp.float32
BF16 = jnp.bfloat16

D_MODEL = 1024
A_WIDTH = 512
B_WIDTH = 512
KV_WIDTH = 128
IN_WIDTH = 3 * A_WIDTH + 2 * B_WIDTH + 2 * KV_WIDTH
CHUNK = 128
HEADS = 8
HEAD_DIM = 64
REL_BUCKETS = 32
NORM_EPS = 1e-6
NEG = -1e30
Q_OFF = 3 * A_WIDTH
K_OFF = Q_OFF + B_WIDTH
BZ_OFF = K_OFF + 2 * KV_WIDTH
QK_SCALE = HEAD_DIM ** -0.5

ADAM_LR = 0.001
ADAM_B1 = 0.9
ADAM_B2 = 0.999
ADAM_EPS = 1e-08
ADAM_WD = 0.01
ADAM_STEP = 10

TOK_TILE = 512
VMEM_LIMIT_V7X = 60 * 1024 * 1024

N_DEV = 8
N_CHIPS = 4
MESH_ID = pl.DeviceIdType.MESH
ANY = pl.BlockSpec(memory_space=pl.ANY)
SMEM = pl.BlockSpec(memory_space=pltpu.SMEM)


def _cparams(n_axes):
    return pltpu.CompilerParams(dimension_semantics=("arbitrary",) * n_axes, vmem_limit_bytes=VMEM_LIMIT_V7X)


_GELU_C = math.sqrt(2.0 / math.pi)


def _gelu(x):
    inner = _GELU_C * (x + 0.044715 * (x * x * x))
    return x * (0.5 * (1.0 + jnp.tanh(inner)))


def _gelu_and_grad(x):
    x2 = x * x
    t = jnp.tanh(_GELU_C * (x + 0.044715 * (x2 * x)))
    cdf = 0.5 * (1.0 + t)
    d = cdf + x * (0.5 * (1.0 - t * t)) * (_GELU_C * (1.0 + 3.0 * 0.044715 * x2))
    return x * cdf, d


def _silu(x):
    return x * (1.0 / (1.0 + jnp.exp(-x)))


def _silu_and_grad(x):
    s = 1.0 / (1.0 + jnp.exp(-x))
    return x * s, s * (1.0 + x * (1.0 - s))


def _dot(a, b):
    return jnp.dot(a, b, preferred_element_type=F32)


def _dot_nt(a, b):
    return lax.dot_general(a, b, (((1,), (1,)), ((), ())), preferred_element_type=F32)


def _dot_tn(a, b):
    return lax.dot_general(a, b, (((0,), (0,)), ((), ())), preferred_element_type=F32)


def _lo_mask(shape):
    return lax.broadcasted_iota(jnp.int32, shape, 1) < HEAD_DIM


def _swap_halves(v):
    return pltpu.roll(v, HEAD_DIM, 1)


def _window_tables():
    q_loc = jnp.arange(CHUNK)[:, None]
    k_loc = jnp.arange(2 * CHUNK)[None, :]
    dist = q_loc + CHUNK - k_loc
    inwin = (dist >= 0) & (dist < CHUNK)
    d = jnp.maximum(dist, 0)
    max_exact = REL_BUCKETS // 2
    safe = jnp.maximum(d, 1).astype(F32)
    large = max_exact + (jnp.log(safe / max_exact) / math.log(CHUNK / max_exact)
                         * (REL_BUCKETS - max_exact)).astype(jnp.int32)
    large = jnp.minimum(large, REL_BUCKETS - 1)
    return jnp.where(d < max_exact, d, large).astype(jnp.int32), inwin.astype(jnp.int32)


def _bias_table(rel_bias, bucket, inwin):
    def body(rb_ref, bk_ref, win_ref, out_ref):
        bk = bk_ref[...]
        win = win_ref[...] > 0
        for h in range(HEADS):
            acc = jnp.zeros(bk.shape, F32)
            for b in range(REL_BUCKETS):
                acc = jnp.where(bk == b, rb_ref[b, h], acc)
            out_ref[h] = jnp.where(win, acc, NEG)

    vmem = pl.BlockSpec(memory_space=pltpu.VMEM)
    return pl.pallas_call(
        body, name="bias_table",
        out_shape=jax.ShapeDtypeStruct((HEADS, CHUNK, 2 * CHUNK), F32),
        in_specs=[SMEM, vmem, vmem], out_specs=vmem,
    )(rel_bias, bucket, inwin)


def _rel_bias_grad(dbias, bucket, inwin):
    n_layers = dbias.shape[0]

    def body(db_ref, bk_ref, win_ref, out_ref):
        bk = jnp.where(win_ref[...] > 0, bk_ref[...], -1)
        for h in range(HEADS):
            tot = db_ref[0, h]
            for l in range(1, n_layers):
                tot = tot + db_ref[l, h]
            for b in range(REL_BUCKETS):
                out_ref[b, h] = jnp.sum(jnp.where(bk == b, tot, 0.0))

    vmem = pl.BlockSpec(memory_space=pltpu.VMEM)
    return pl.pallas_call(
        body, name="rel_bias_grad",
        out_shape=jax.ShapeDtypeStruct((REL_BUCKETS, HEADS), F32),
        in_specs=[vmem] * 3, out_specs=SMEM,
    )(dbias, bucket, inwin)


def _spatial_tables(w_spatial, b_spatial_t):
    n_layers = w_spatial.shape[0]

    def body(w_ref, b_ref, wcat_ref, wtcat_ref, bs_ref):
        row = lax.broadcasted_iota(jnp.int32, (CHUNK, CHUNK), 0)
        col = lax.broadcasted_iota(jnp.int32, (CHUNK, CHUNK), 1)
        causal = col <= row
        lo = _lo_mask((CHUNK, CHUNK))
        for p in range(4):
            for half in range(2):
                w = jnp.where(causal, w_ref[0, 2 * p + half], 0.0)
                wcat_ref[0, p, :, half * CHUNK:(half + 1) * CHUNK] = w.astype(BF16)
                wtcat_ref[0, p, :, half * CHUNK:(half + 1) * CHUNK] = w.T.astype(BF16)
            b = b_ref[0]
            bs_ref[0, p] = jnp.where(lo, b[:, 2 * p:2 * p + 1], b[:, 2 * p + 1:2 * p + 2])

    return pl.pallas_call(
        body, name="spatial_tables", grid=(n_layers,),
        out_shape=(jax.ShapeDtypeStruct((n_layers, 4, CHUNK, 2 * CHUNK), BF16),
                   jax.ShapeDtypeStruct((n_layers, 4, CHUNK, 2 * CHUNK), BF16),
                   jax.ShapeDtypeStruct((n_layers, 4, CHUNK, CHUNK), F32)),
        in_specs=[pl.BlockSpec((1, HEADS, CHUNK, CHUNK), lambda l: (l, 0, 0, 0)),
                  pl.BlockSpec((1, CHUNK, HEADS), lambda l: (l, 0, 0))],
        out_specs=(pl.BlockSpec((1, 4, CHUNK, 2 * CHUNK), lambda l: (l, 0, 0, 0)),
                   pl.BlockSpec((1, 4, CHUNK, 2 * CHUNK), lambda l: (l, 0, 0, 0)),
                   pl.BlockSpec((1, 4, CHUNK, CHUNK), lambda l: (l, 0, 0, 0))),
        compiler_params=_cparams(1),
    )(w_spatial, b_spatial_t)


def _b_spatial_grad(dmix):
    n_layers = dmix.shape[0]

    def body(d_ref, out_ref):
        lane = lax.broadcasted_iota(jnp.int32, (CHUNK, CHUNK), 1)
        acc = jnp.zeros((CHUNK, CHUNK), F32)
        for p in range(4):
            t = d_ref[0, :, p * CHUNK:(p + 1) * CHUNK]
            s_lo = jnp.sum(jnp.where(lane < HEAD_DIM, t, 0.0), axis=1, keepdims=True)
            s_hi = jnp.sum(jnp.where(lane < HEAD_DIM, 0.0, t), axis=1, keepdims=True)
            acc = jnp.where(lane == 2 * p, s_lo, acc)
            acc = jnp.where(lane == 2 * p + 1, s_hi, acc)
        out_ref[0] = acc

    return pl.pallas_call(
        body, name="b_spatial_grad", grid=(n_layers,),
        out_shape=jax.ShapeDtypeStruct((n_layers, CHUNK, CHUNK), F32),
        in_specs=[pl.BlockSpec((1, CHUNK, A_WIDTH), lambda l: (l, 0, 0))],
        out_specs=pl.BlockSpec((1, CHUNK, CHUNK), lambda l: (l, 0, 0)),
        compiler_params=_cparams(1),
    )(dmix)


def _place():
    x, y, c = lax.axis_index("x"), lax.axis_index("y"), lax.axis_index("c")
    other_chips = [(1 - x, y), (x, 1 - y), (1 - x, 1 - y)]
    return x, y, c, other_chips


N_GATHER_SEMS = 12


def _gather_phases(shards, fulls, send_sems, recv_sems, local_sems):
    x, y, c, chips = _place()
    sibling = (x, y, 1 - c)
    n_arr = len(shards)

    def half_rows(a, chip, half):
        n = shards[a].shape[0]
        start = (2 * chip[0] + chip[1]) * n + half * (n // 2)
        return fulls[a].at[pl.ds(pl.multiple_of(start, 16), n // 2), :]

    def my_half(a):
        n = shards[a].shape[0]
        return shards[a].at[pl.ds(pl.multiple_of(c * (n // 2), 16), n // 2), :]

    def copy(k, a, src, chip, half, to):
        return pltpu.make_async_remote_copy(
            src_ref=src, dst_ref=half_rows(a, chip, half), send_sem=send_sems.at[n_arr * k + a],
            recv_sem=recv_sems.at[n_arr * k + a], device_id=to, device_id_type=MESH_ID)

    def local(a):
        n = shards[a].shape[0]
        mine = fulls[a].at[pl.ds(pl.multiple_of((2 * x + y) * n, 16), n), :]
        return pltpu.make_async_copy(shards[a], mine, local_sems.at[a])

    def first(k, a):
        return copy(k, a, my_half(a), (x, y), c, (chips[k][0], chips[k][1], c))

    def passed(k, a):
        return copy(3 + k, a, half_rows(a, chips[k], c), chips[k], c, sibling)

    def phase_a():
        for a in range(n_arr):
            local(a).start()
        for k in range(3):
            for a in range(n_arr):
                first(k, a).start()

    def phase_b():
        for k in range(3):
            for a in range(n_arr):
                copy(k, a, my_half(a), chips[k], c, sibling).wait_recv()
                passed(k, a).start()

    def phase_c():
        for k in range(3):
            for a in range(n_arr):
                copy(3 + k, a, my_half(a), chips[k], 1 - c, sibling).wait_recv()
        for k in range(3):
            for a in range(n_arr):
                first(k, a).wait_send()
                passed(k, a).wait_send()
        for a in range(n_arr):
            local(a).wait()

    return phase_a, phase_b, phase_c


N_EXCHANGE_SEMS = 14


def _exchange_phases(partials, parts, send_sems, recv_sems, local_sems, by_chip=True):
    x, y, c, chips = _place()
    me, sibling = (x, y, c), (x, y, 1 - c)
    n_arr = len(partials)

    def block(a, chip):
        if not by_chip:
            return partials[a]
        n = partials[a].shape[0] // N_CHIPS
        return partials[a].at[pl.ds(pl.multiple_of((2 * chip[0] + chip[1]) * n, 16), n), :]

    def slot(a, dev):
        return parts[a].at[4 * dev[0] + 2 * dev[1] + dev[2]]

    def copy(k, a, src, origin, to):
        return pltpu.make_async_remote_copy(
            src_ref=src, dst_ref=slot(a, origin), send_sem=send_sems.at[n_arr * k + a],
            recv_sem=recv_sems.at[n_arr * k + a], device_id=to, device_id_type=MESH_ID)

    def local(a):
        return pltpu.make_async_copy(block(a, (x, y)), slot(a, me), local_sems.at[a])

    def first(k, a):
        if k == 0:
            return copy(0, a, block(a, (x, y)), me, sibling)
        chip = chips[k - 1]
        return copy(k, a, block(a, chip), me, (chip[0], chip[1], c))

    def passed(k, a):
        origin = (chips[k][0], chips[k][1], c)
        return copy(4 + k, a, slot(a, origin), origin, sibling)

    def phase_a():
        for a in range(n_arr):
            local(a).start()
        for k in range(4):
            for a in range(n_arr):
                first(k, a).start()

    def phase_b():
        for k in range(3):
            for a in range(n_arr):
                copy(1 + k, a, block(a, (x, y)), (chips[k][0], chips[k][1], c), me).wait_recv()
                passed(k, a).start()

    def phase_c():
        for a in range(n_arr):
            copy(0, a, block(a, (x, y)), sibling, me).wait_recv()
        for k in range(3):
            for a in range(n_arr):
                copy(4 + k, a, block(a, (x, y)), (chips[k][0], chips[k][1], 1 - c), me).wait_recv()
        for k in range(4):
            for a in range(n_arr):
                first(k, a).wait_send()
        for k in range(3):
            for a in range(n_arr):
                passed(k, a).wait_send()
        for a in range(n_arr):
            local(a).wait()

    return phase_a, phase_b, phase_c


def _comm_scratch(n_sems):
    return [pltpu.SemaphoreType.DMA((n_sems,)), pltpu.SemaphoreType.DMA((n_sems,)), pltpu.SemaphoreType.DMA((2,))]


def _gather_weights(wt_shards, wo_shards, layer):
    wt_rows, wo_rows = wt_shards.shape[1], wo_shards.shape[1]

    def body(wt_ref, wo_ref, wt_full, wo_full, send_sems, recv_sems, local_sems):
        phases = _gather_phases((wt_ref.at[layer], wo_ref.at[layer]), (wt_full, wo_full), send_sems, recv_sems, local_sems)
        for phase in phases:
            phase()

    return pl.pallas_call(
        body, name="gather_weights",
        out_shape=(jax.ShapeDtypeStruct((N_CHIPS * wt_rows, D_MODEL), BF16),
                   jax.ShapeDtypeStruct((N_CHIPS * wo_rows, D_MODEL), BF16)),
        in_specs=[ANY, ANY], out_specs=(ANY, ANY), scratch_shapes=_comm_scratch(N_GATHER_SEMS),
    )(wt_shards, wo_shards)


def _exchange_grads(dwt, dwo):
    def body(dwt_ref, dwo_ref, pt_ref, po_ref, send_sems, recv_sems, local_sems):
        for phase in _exchange_phases((dwt_ref, dwo_ref), (pt_ref, po_ref), send_sems, recv_sems, local_sems):
            phase()

    return pl.pallas_call(
        body, name="exchange_grads",
        out_shape=(jax.ShapeDtypeStruct((N_DEV, dwt.shape[0] // N_CHIPS, D_MODEL), BF16),
                   jax.ShapeDtypeStruct((N_DEV, dwo.shape[0] // N_CHIPS, D_MODEL), BF16)),
        in_specs=[ANY, ANY], out_specs=(ANY, ANY), scratch_shapes=_comm_scratch(N_EXCHANGE_SEMS),
    )(dwt, dwo)


def _allreduce_small(part):
    n_rows = part.shape[0]

    def body(p_ref, tot_ref, all_ref, send_sems, recv_sems, local_sem):
        x, y, c, chips = _place()
        me, sibling = (x, y, c), (x, y, 1 - c)

        def rows(dev):
            return all_ref.at[pl.ds(pl.multiple_of((4 * dev[0] + 2 * dev[1] + dev[2]) * n_rows, 8), n_rows), :]

        def copy(k, origin, to, src=None):
            return pltpu.make_async_remote_copy(
                src_ref=rows(origin) if src is None else src, dst_ref=rows(origin), send_sem=send_sems.at[k],
                recv_sem=recv_sems.at[k], device_id=to, device_id_type=MESH_ID)

        mine = pltpu.make_async_copy(p_ref, rows(me), local_sem)
        mine.start()
        first = [copy(0, me, sibling, src=p_ref)]
        first += [copy(1 + k, me, (chip[0], chip[1], c), src=p_ref) for k, chip in enumerate(chips)]
        for cp in first:
            cp.start()
        passed = []
        for k, chip in enumerate(chips):
            origin = (chip[0], chip[1], c)
            copy(1 + k, origin, me).wait_recv()
            fwd = copy(4 + k, origin, sibling)
            fwd.start()
            passed.append(fwd)
        copy(0, sibling, me).wait_recv()
        for k, chip in enumerate(chips):
            copy(4 + k, (chip[0], chip[1], 1 - c), me).wait_recv()
        for cp in first + passed:
            cp.wait_send()
        mine.wait()
        tot = all_ref[0:n_rows, :]
        for d in range(1, N_DEV):
            tot = tot + all_ref[d * n_rows:(d + 1) * n_rows, :]
        tot_ref[...] = tot

    vmem = pl.BlockSpec(memory_space=pltpu.VMEM)
    return pl.pallas_call(
        body, name="allreduce_small",
        out_shape=jax.ShapeDtypeStruct((n_rows, 128), F32),
        in_specs=[vmem], out_specs=vmem,
        scratch_shapes=[pltpu.VMEM((N_DEV * n_rows, 128), F32), pltpu.SemaphoreType.DMA((7,)),
                        pltpu.SemaphoreType.DMA((7,)), pltpu.SemaphoreType.DMA],
        compiler_params=pltpu.CompilerParams(vmem_limit_bytes=VMEM_LIMIT_V7X),
    )(part)


def _hosted(phases, step, n_steps):
    phase_a, phase_b, phase_c = phases

    def at_start():
        pl.when(step == 0)(phase_a)

    def at_end():
        pl.when(step == n_steps // 2)(phase_b)
        pl.when(step == n_steps - 1)(phase_c)

    return at_start, at_end


def _fwd_in(x, g_pre, w_in_t):
    n_tok = x.shape[0]

    def body(x_ref, g_ref, w_ref, a_ref, q_ref, kv_ref, bz_ref):
        xf = x_ref[...]
        r = lax.rsqrt(jnp.mean(xf * xf, axis=-1, keepdims=True) + NORM_EPS)
        h = ((xf * r) * g_ref[...]).astype(BF16)
        a_ref[...] = _dot_nt(h, w_ref[0:Q_OFF, :])
        q_ref[...] = _dot_nt(h, w_ref[Q_OFF:K_OFF, :]).astype(BF16)
        kv_ref[...] = _dot_nt(h, w_ref[K_OFF:BZ_OFF, :]).astype(BF16)
        bz_ref[...] = _dot_nt(h, w_ref[BZ_OFF:IN_WIDTH, :])

    tm = TOK_TILE
    return pl.pallas_call(
        body, name="fwd_in", grid=(n_tok // tm,),
        out_shape=(jax.ShapeDtypeStruct((n_tok, Q_OFF), F32), jax.ShapeDtypeStruct((n_tok, B_WIDTH), BF16),
                   jax.ShapeDtypeStruct((n_tok, 2 * KV_WIDTH), BF16), jax.ShapeDtypeStruct((n_tok, B_WIDTH), F32)),
        in_specs=[pl.BlockSpec((tm, D_MODEL), lambda i: (i, 0)), pl.BlockSpec((1, D_MODEL), lambda i: (0, 0)),
                  pl.BlockSpec((IN_WIDTH, D_MODEL), lambda i: (0, 0))],
        out_specs=(pl.BlockSpec((tm, Q_OFF), lambda i: (i, 0)), pl.BlockSpec((tm, B_WIDTH), lambda i: (i, 0)),
                   pl.BlockSpec((tm, 2 * KV_WIDTH), lambda i: (i, 0)), pl.BlockSpec((tm, B_WIDTH), lambda i: (i, 0))),
        compiler_params=_cparams(1),
    )(x, g_pre, w_in_t)


def _layer_norm_stats(vv):
    mu = jnp.mean(vv, axis=-1, keepdims=True)
    xc = vv - mu
    rs = lax.rsqrt(jnp.mean(xc * xc, axis=-1, keepdims=True) + NORM_EPS)
    return xc * rs, rs


def _blockdiag(v, lo):
    zero = jnp.zeros_like(v)
    return jnp.concatenate([jnp.where(lo, v, zero), jnp.where(lo, zero, v)], axis=0)


def _softmax_sink(s, sink):
    m = jnp.maximum(jnp.max(s, axis=-1, keepdims=True), sink)
    e = jnp.exp(s - m)
    esink = jnp.exp(sink - m)
    den = jnp.sum(e, axis=-1, keepdims=True) + esink
    return e / den, esink / den


def _kv_rows(cur_ref, halo_ref, r0, c):
    prev_in_tile = cur_ref[pl.ds(pl.multiple_of(jnp.maximum(r0 - CHUNK, 0), CHUNK), CHUNK), :]
    prev = jnp.where(c == 0, halo_ref[...], prev_in_tile)
    kv2 = jnp.concatenate([prev, cur_ref[pl.ds(r0, CHUNK), :]], axis=0)
    k2, v2 = kv2[:, 0:KV_WIDTH], kv2[:, KV_WIDTH:2 * KV_WIDTH]
    return (k2, _swap_halves(k2)), (v2, _swap_halves(v2))


def _head_of(p, half):
    return 2 * p + half, int(half != p // 2)


def _fwd_mix(x, a, q, kv, bz, ln_g, ln_b, wcat, bs, sinks, bias, w_out, g_post, seq_len, gather=None):
    n_tok = x.shape[0]
    tb = TOK_TILE
    n_chunks = tb // CHUNK
    n_tiles = n_tok // tb
    n_in = 14

    def body(*refs):
        (x_ref, a_ref, q_ref, kv_ref, halo_ref, bz_ref, lng_ref, lnb_ref, wcat_ref, bs_ref, sink_ref, bias_ref,
         wout_ref, gpost_ref) = refs[:n_in]
        if gather is None:
            xn_ref, y_ref, ycat_s = refs[n_in:]
            at_start = at_end = lambda: None
        else:
            wts_ref, wos_ref, xn_ref, y_ref, wt_full, wo_full, ycat_s, send_sems, recv_sems, local_sems = refs[n_in:]
            phases = _gather_phases((wts_ref.at[gather[2]], wos_ref.at[gather[2]]), (wt_full, wo_full), send_sems,
                                    recv_sems, local_sems)
            at_start, at_end = _hosted(phases, pl.program_id(0), n_tiles)
        at_start()
        i = pl.program_id(0)
        lo = _lo_mask((CHUNK, CHUNK))
        kidx = lax.broadcasted_iota(jnp.int32, (CHUNK, 2 * CHUNK), 1)

        def chunk(c, carry):
            r0 = pl.multiple_of(c * CHUNK, CHUNK)
            rows = pl.ds(r0, CHUNK)
            u = _gelu(a_ref[rows, 0:A_WIDTH])
            vv = _gelu(a_ref[rows, A_WIDTH:2 * A_WIDTH])
            xhat, _ = _layer_norm_stats(vv)
            vnb = (xhat * lng_ref[...] + lnb_ref[...]).astype(BF16)
            for p in range(4):
                blk = slice(p * CHUNK, (p + 1) * CHUNK)
                mixed = _dot(wcat_ref[p], _blockdiag(vnb[:, blk], lo)) + bs_ref[p]
                sz = _silu(a_ref[rows, 2 * A_WIDTH + p * CHUNK:2 * A_WIDTH + (p + 1) * CHUNK])
                ycat_s[rows, blk] = ((u[:, blk] * mixed) * sz).astype(BF16)
            ks, vs = _kv_rows(kv_ref, halo_ref, r0, c)
            first = lax.rem(i * tb + r0, seq_len) == 0
            dead = kidx < jnp.where(first, CHUNK, 0)
            for p in range(4):
                blk = slice(p * CHUNK, (p + 1) * CHUNK)
                qt = q_ref[rows, blk]
                zero = jnp.zeros_like(qt)
                res = []
                for half in range(2):
                    head, sw = _head_of(p, half)
                    qm = jnp.where(lo, qt, zero) if half == 0 else jnp.where(lo, zero, qt)
                    s = _dot_nt(qm, ks[sw]) * QK_SCALE + bias_ref[head]
                    s = jnp.where(dead, NEG, s)
                    prob, _ = _softmax_sink(s, sink_ref[head])
                    res.append(_dot(prob.astype(BF16), vs[sw]))
                o = jnp.where(lo, res[0], res[1])
                ycat_s[rows, B_WIDTH + p * CHUNK:B_WIDTH + (p + 1) * CHUNK] = (o * _silu(bz_ref[rows, blk])).astype(BF16)
            return carry

        lax.fori_loop(0, n_chunks, chunk, 0)
        y = _dot(ycat_s[...], wout_ref[...])
        r = lax.rsqrt(jnp.mean(y * y, axis=-1, keepdims=True) + NORM_EPS)
        y_ref[...] = y
        xn_ref[...] = x_ref[...] + (y * r) * gpost_ref[...]
        at_end()

    tile = lambda w: pl.BlockSpec((tb, w), lambda i: (i, 0))
    whole = lambda shape: pl.BlockSpec(shape, lambda i: (0,) * len(shape))
    in_specs = [tile(D_MODEL), tile(Q_OFF), tile(B_WIDTH), tile(2 * KV_WIDTH),
                pl.BlockSpec((CHUNK, 2 * KV_WIDTH), lambda i: (jnp.maximum(i * n_chunks - 1, 0), 0)),
                tile(B_WIDTH), whole((1, A_WIDTH)), whole((1, A_WIDTH)), whole((4, CHUNK, 2 * CHUNK)),
                whole((4, CHUNK, CHUNK)), SMEM, whole((HEADS, CHUNK, 2 * CHUNK)), whole((D_MODEL, D_MODEL)),
                whole((1, D_MODEL))]
    out_shape = [jax.ShapeDtypeStruct((n_tok, D_MODEL), F32), jax.ShapeDtypeStruct((n_tok, D_MODEL), F32)]
    out_specs = [tile(D_MODEL), tile(D_MODEL)]
    scratch = [pltpu.VMEM((tb, D_MODEL), BF16)]
    args = [x, a, q, kv, kv, bz, ln_g, ln_b, wcat, bs, sinks, bias, w_out, g_post]
    if gather is not None:
        wt_shards, wo_shards, _ = gather
        in_specs += [ANY, ANY]
        args += [wt_shards, wo_shards]
        out_shape += [jax.ShapeDtypeStruct((N_CHIPS * wt_shards.shape[1], D_MODEL), BF16),
                      jax.ShapeDtypeStruct((N_CHIPS * wo_shards.shape[1], D_MODEL), BF16)]
        out_specs += [ANY, ANY]
        scratch += _comm_scratch(N_GATHER_SEMS)
    return pl.pallas_call(
        body, name="fwd_mix" if gather is None else "fwd_mix_gather", grid=(n_tiles,),
        out_shape=tuple(out_shape), in_specs=in_specs, out_specs=tuple(out_specs), scratch_shapes=scratch,
        compiler_params=_cparams(1),
    )(*args)


def _loss_head(x, target):
    n_tok = x.shape[0]
    tm = TOK_TILE

    def body(x_ref, t_ref, dx_ref, loss_ref):
        d = x_ref[...] - t_ref[...]
        dx_ref[...] = d * (1.0 / D_MODEL)
        part = 0.5 * jnp.sum(jnp.mean(d * d, axis=-1, keepdims=True))

        @pl.when(pl.program_id(0) == 0)
        def _():
            loss_ref[0, 0] = 0.0

        loss_ref[0, 0] += part

    return pl.pallas_call(
        body, name="loss_head", grid=(n_tok // tm,),
        out_shape=(jax.ShapeDtypeStruct((n_tok, D_MODEL), F32), jax.ShapeDtypeStruct((1, 1), F32)),
        in_specs=[pl.BlockSpec((tm, D_MODEL), lambda i: (i, 0))] * 2,
        out_specs=(pl.BlockSpec((tm, D_MODEL), lambda i: (i, 0)), SMEM),
        compiler_params=_cparams(1),
    )(x, target)


def _bwd_mix(dout, y, a, q, kv, bz, ln_g, ln_b, wcat, wtcat, bs, sinks, bias, w_out, g_post, seq_len, exchange=None):
    n_tok = y.shape[0]
    tb = TOK_TILE
    n_chunks = tb // CHUNK
    n_tiles = n_tok // tb
    n_in, n_out = 16, 9

    def body(*refs):
        (dout_ref, y_ref, a_ref, q_ref, kv_ref, halo_ref, bz_ref, lng_ref, lnb_ref, wcat_ref, wtcat_ref, bs_ref,
         sink_ref, bias_ref, wout_ref, gpost_ref) = refs[:n_in]
        if exchange is None:
            outs, rest = refs[n_in:n_in + n_out], refs[n_in + n_out:]
            at_start = at_end = lambda: None
        else:
            dwt_ref, dwo_ref = refs[n_in:n_in + 2]
            outs = refs[n_in + 2:n_in + 2 + n_out]
            pt_ref, po_ref = refs[n_in + 2 + n_out:n_in + 4 + n_out]
            rest = refs[n_in + 4 + n_out:]
            phases = _exchange_phases((dwt_ref, dwo_ref), (pt_ref, po_ref), *rest[6:])
            at_start, at_end = _hosted(phases, pl.program_id(0), n_tiles)
        dproj_ref, dwout_ref, dws_ref, dmix_ref, dlng_ref, dlnb_ref, dgpost_ref, dsink_ref, dbias_ref = outs
        ycat_s, dy_s, dyc_s, dkv_s, carry_s, dwout_s = rest[:6]
        at_start()
        step = pl.program_id(0)
        tile_idx = n_tiles - 1 - step
        lo = _lo_mask((CHUNK, CHUNK))
        kidx = lax.broadcasted_iota(jnp.int32, (CHUNK, 2 * CHUNK), 1)

        @pl.when(step == 0)
        def _():
            dwout_s[...] = jnp.zeros_like(dwout_s)
            dws_ref[...] = jnp.zeros_like(dws_ref)
            dmix_ref[...] = jnp.zeros_like(dmix_ref)
            dlng_ref[...] = jnp.zeros_like(dlng_ref)
            dlnb_ref[...] = jnp.zeros_like(dlnb_ref)
            dgpost_ref[...] = jnp.zeros_like(dgpost_ref)
            dbias_ref[...] = jnp.zeros_like(dbias_ref)
            carry_s[...] = jnp.zeros_like(carry_s)
            for h in range(HEADS):
                dsink_ref[h] = 0.0

        yv = y_ref[...]
        dout = dout_ref[...]
        r = lax.rsqrt(jnp.mean(yv * yv, axis=-1, keepdims=True) + NORM_EPS)
        yn = yv * r
        dgpost_ref[...] += jnp.sum(dout * yn, axis=0, keepdims=True)
        dyn = dout * gpost_ref[...]
        dy = r * (dyn - yn * jnp.mean(dyn * yn, axis=-1, keepdims=True))
        dy_s[...] = dy.astype(BF16)
        dyc_s[...] = _dot_nt(dy_s[...], wout_ref[...])
        dkv_s[0:tb, :] = jnp.zeros((tb, 2 * KV_WIDTH), F32)
        dkv_s[tb:tb + CHUNK, :] = carry_s[...]

        def chunk(c, carry):
            r0 = pl.multiple_of(c * CHUNK, CHUNK)
            rows = pl.ds(r0, CHUNK)
            u, gu = _gelu_and_grad(a_ref[rows, 0:A_WIDTH])
            vv, gv = _gelu_and_grad(a_ref[rows, A_WIDTH:2 * A_WIDTH])
            xhat, rs = _layer_norm_stats(vv)
            vnb = (xhat * lng_ref[...] + lnb_ref[...]).astype(BF16)
            d_vn, d_u, d_az = [], [], []
            for p in range(4):
                blk = slice(p * CHUNK, (p + 1) * CHUNK)
                vbd = _blockdiag(vnb[:, blk], lo)
                mixed = _dot(wcat_ref[p], vbd) + bs_ref[p]
                sz, gz = _silu_and_grad(a_ref[rows, 2 * A_WIDTH + p * CHUNK:2 * A_WIDTH + (p + 1) * CHUNK])
                ub = u[:, blk]
                dya = dyc_s[rows, blk]
                um = ub * mixed
                ycat_s[rows, blk] = (um * sz).astype(BF16)
                d_mixed = (dya * ub) * sz
                d_u.append((dya * mixed) * sz)
                d_az.append((dya * um) * gz)
                dmix_ref[:, blk] += d_mixed
                dmbd = _blockdiag(d_mixed.astype(BF16), lo)
                d_vn.append(_dot(wtcat_ref[p], dmbd))
                dws_ref[p] += _dot_nt(dmbd, vnb[:, blk])
            d_vn = jnp.concatenate(d_vn, axis=1)
            dlng_ref[...] += jnp.sum(d_vn * xhat, axis=0, keepdims=True)
            dlnb_ref[...] += jnp.sum(d_vn, axis=0, keepdims=True)
            dxh = d_vn * lng_ref[...]
            d_vv = rs * (dxh - jnp.mean(dxh, axis=-1, keepdims=True)
                         - xhat * jnp.mean(dxh * xhat, axis=-1, keepdims=True))
            dproj_ref[rows, 0:A_WIDTH] = (jnp.concatenate(d_u, axis=1) * gu).astype(BF16)
            dproj_ref[rows, A_WIDTH:2 * A_WIDTH] = (d_vv * gv).astype(BF16)
            dproj_ref[rows, 2 * A_WIDTH:Q_OFF] = jnp.concatenate(d_az, axis=1).astype(BF16)
            ks, vs = _kv_rows(kv_ref, halo_ref, r0, c)
            first = lax.rem(tile_idx * tb + r0, seq_len) == 0
            dead = kidx < jnp.where(first, CHUNK, 0)
            dk_acc = [jnp.zeros((2 * CHUNK, KV_WIDTH), F32), jnp.zeros((2 * CHUNK, KV_WIDTH), F32)]
            dv_acc = [jnp.zeros((2 * CHUNK, KV_WIDTH), F32), jnp.zeros((2 * CHUNK, KV_WIDTH), F32)]
            pending = {0: [], 1: []}
            for p in range(4):
                blk = slice(p * CHUNK, (p + 1) * CHUNK)
                qt = q_ref[rows, blk]
                zero = jnp.zeros_like(qt)
                sb, gb = _silu_and_grad(bz_ref[rows, blk])
                dyb = dyc_s[rows, B_WIDTH + p * CHUNK:B_WIDTH + (p + 1) * CHUNK]
                qms, probs, psinks, outs_p = [], [], [], []
                for half in range(2):
                    head, sw = _head_of(p, half)
                    qm = jnp.where(lo, qt, zero) if half == 0 else jnp.where(lo, zero, qt)
                    s = _dot_nt(qm, ks[sw]) * QK_SCALE + bias_ref[head]
                    s = jnp.where(dead, NEG, s)
                    prob, psink = _softmax_sink(s, sink_ref[head])
                    qms.append(qm)
                    probs.append(prob)
                    psinks.append(psink)
                    outs_p.append(_dot(prob.astype(BF16), vs[sw]))
                o = jnp.where(lo, outs_p[0], outs_p[1])
                ycat_s[rows, B_WIDTH + p * CHUNK:B_WIDTH + (p + 1) * CHUNK] = (o * sb).astype(BF16)
                dproj_ref[rows, BZ_OFF + p * CHUNK:BZ_OFF + (p + 1) * CHUNK] = ((dyb * o) * gb).astype(BF16)
                d_ob = (dyb * sb).astype(BF16)
                zero_o = jnp.zeros_like(d_ob)
                dqs = []
                for half in range(2):
                    head, sw = _head_of(p, half)
                    dom = jnp.where(lo, d_ob, zero_o) if half == 0 else jnp.where(lo, zero_o, d_ob)
                    dp = _dot_nt(dom, vs[sw])
                    delta = jnp.sum(probs[half] * dp, axis=-1, keepdims=True)
                    ds = probs[half] * (dp - delta)
                    dsink_ref[head] += -jnp.sum(psinks[half] * delta)
                    dbias_ref[head] += ds
                    dsb = ds.astype(BF16)
                    dqs.append(_dot(dsb, ks[sw]))
                    pending[sw].append((dsb, qms[half], probs[half].astype(BF16), dom))
                    if len(pending[sw]) == 2:
                        (d1, q1, p1, o1), (d2, q2, p2, o2) = pending[sw]
                        dk_acc[sw] = dk_acc[sw] + _dot_tn(jnp.concatenate([d1, d2], axis=0),
                                                          jnp.concatenate([q1, q2], axis=0))
                        dv_acc[sw] = dv_acc[sw] + _dot_tn(jnp.concatenate([p1, p2], axis=0),
                                                          jnp.concatenate([o1, o2], axis=0))
                        pending[sw] = []
                dq = jnp.where(lo, dqs[0], dqs[1]) * QK_SCALE
                dproj_ref[rows, Q_OFF + p * CHUNK:Q_OFF + (p + 1) * CHUNK] = dq.astype(BF16)
            both = pl.ds(r0, 2 * CHUNK)
            dkv_s[both, 0:KV_WIDTH] += (dk_acc[0] + _swap_halves(dk_acc[1])) * QK_SCALE
            dkv_s[both, KV_WIDTH:2 * KV_WIDTH] += dv_acc[0] + _swap_halves(dv_acc[1])
            return carry

        lax.fori_loop(0, n_chunks, chunk, 0)
        dwout_s[...] += _dot_tn(ycat_s[...], dy_s[...])
        dproj_ref[:, K_OFF:BZ_OFF] = dkv_s[CHUNK:CHUNK + tb, :].astype(BF16)
        carry_s[...] = dkv_s[0:CHUNK, :]

        @pl.when(step == n_tiles - 1)
        def _():
            row = lax.broadcasted_iota(jnp.int32, (2 * CHUNK, CHUNK), 0)
            col = lax.broadcasted_iota(jnp.int32, (2 * CHUNK, CHUNK), 1)
            causal = col <= jnp.where(row >= CHUNK, row - CHUNK, row)
            for p in range(4):
                dws_ref[p] = jnp.where(causal, dws_ref[p], 0.0)
            dwout_ref[...] = dwout_s[...].astype(BF16)

        at_end()

    tile = lambda w: pl.BlockSpec((tb, w), lambda s: (n_tiles - 1 - s, 0))
    whole = lambda shape: pl.BlockSpec(shape, lambda s: (0,) * len(shape))
    in_specs = [tile(D_MODEL), tile(D_MODEL), tile(Q_OFF), tile(B_WIDTH), tile(2 * KV_WIDTH),
                pl.BlockSpec((CHUNK, 2 * KV_WIDTH), lambda s: (jnp.maximum((n_tiles - 1 - s) * n_chunks - 1, 0), 0)),
                tile(B_WIDTH), whole((1, A_WIDTH)), whole((1, A_WIDTH)), whole((4, CHUNK, 2 * CHUNK)),
                whole((4, CHUNK, 2 * CHUNK)), whole((4, CHUNK, CHUNK)), SMEM, whole((HEADS, CHUNK, 2 * CHUNK)),
                whole((D_MODEL, D_MODEL)), whole((1, D_MODEL))]
    out_shape = [jax.ShapeDtypeStruct((n_tok, IN_WIDTH), BF16), jax.ShapeDtypeStruct((D_MODEL, D_MODEL), BF16),
                 jax.ShapeDtypeStruct((4, 2 * CHUNK, CHUNK), F32), jax.ShapeDtypeStruct((CHUNK, A_WIDTH), F32),
                 jax.ShapeDtypeStruct((1, A_WIDTH), F32), jax.ShapeDtypeStruct((1, A_WIDTH), F32),
                 jax.ShapeDtypeStruct((1, D_MODEL), F32), jax.ShapeDtypeStruct((HEADS,), F32),
                 jax.ShapeDtypeStruct((HEADS, CHUNK, 2 * CHUNK), F32)]
    out_specs = [tile(IN_WIDTH), whole((D_MODEL, D_MODEL)), whole((4, 2 * CHUNK, CHUNK)), whole((CHUNK, A_WIDTH)),
                 whole((1, A_WIDTH)), whole((1, A_WIDTH)), whole((1, D_MODEL)), SMEM, whole((HEADS, CHUNK, 2 * CHUNK))]
    scratch = [pltpu.VMEM((tb, D_MODEL), BF16), pltpu.VMEM((tb, D_MODEL), BF16), pltpu.VMEM((tb, D_MODEL), F32),
               pltpu.VMEM((tb + CHUNK, 2 * KV_WIDTH), F32), pltpu.VMEM((CHUNK, 2 * KV_WIDTH), F32),
               pltpu.VMEM((D_MODEL, D_MODEL), F32)]
    args = [dout, y, a, q, kv, kv, bz, ln_g, ln_b, wcat, wtcat, bs, sinks, bias, w_out, g_post]
    if exchange is not None:
        dwt, dwo = exchange
        in_specs += [ANY, ANY]
        args += [dwt, dwo]
        out_shape += [jax.ShapeDtypeStruct((N_DEV, dwt.shape[0] // N_CHIPS, D_MODEL), BF16),
                      jax.ShapeDtypeStruct((N_DEV, dwo.shape[0] // N_CHIPS, D_MODEL), BF16)]
        out_specs += [ANY, ANY]
        scratch += _comm_scratch(N_EXCHANGE_SEMS)
    return pl.pallas_call(
        body, name="bwd_mix" if exchange is None else "bwd_mix_exchange", grid=(n_tiles,),
        out_shape=tuple(out_shape), in_specs=in_specs, out_specs=tuple(out_specs), scratch_shapes=scratch,
        compiler_params=_cparams(1),
    )(*args)


def _bwd_in(dproj, x, dout, g_pre, w_in_t, allgather=None):
    n_tok = x.shape[0]
    tm = TOK_TILE
    n_tiles = n_tok // tm

    def body(*refs):
        dp_ref, x_ref, dout_ref, g_ref, wt_ref = refs[:5]
        if allgather is None:
            dx_ref, dwt_ref, dg_ref, acc_s = refs[5:]
            at_start = at_end = lambda: None
        else:
            part_ref, dx_ref, dwt_ref, dg_ref, all_ref, acc_s, send_sems, recv_sems, local_sems = refs[5:]
            phases = _exchange_phases((part_ref,), (all_ref,), send_sems, recv_sems, local_sems, by_chip=False)
            at_start, at_end = _hosted(phases, pl.program_id(0), n_tiles)
        at_start()
        i = pl.program_id(0)

        @pl.when(i == 0)
        def _():
            acc_s[...] = jnp.zeros_like(acc_s)
            dg_ref[...] = jnp.zeros_like(dg_ref)

        xf = x_ref[...]
        r = lax.rsqrt(jnp.mean(xf * xf, axis=-1, keepdims=True) + NORM_EPS)
        xn = xf * r
        h = (xn * g_ref[...]).astype(BF16)
        dp = dp_ref[...]
        dh = _dot(dp, wt_ref[...])
        acc_s[...] += _dot_tn(dp, h)
        dg_ref[...] += jnp.sum(dh * xn, axis=0, keepdims=True)
        dhn = dh * g_ref[...]
        dx_ref[...] = dout_ref[...] + r * (dhn - xn * jnp.mean(dhn * xn, axis=-1, keepdims=True))

        @pl.when(i == n_tiles - 1)
        def _():
            dwt_ref[...] = acc_s[...].astype(BF16)

        at_end()

    in_specs = [pl.BlockSpec((tm, IN_WIDTH), lambda i: (i, 0)), pl.BlockSpec((tm, D_MODEL), lambda i: (i, 0)),
                pl.BlockSpec((tm, D_MODEL), lambda i: (i, 0)), pl.BlockSpec((1, D_MODEL), lambda i: (0, 0)),
                pl.BlockSpec((IN_WIDTH, D_MODEL), lambda i: (0, 0), pipeline_mode=pl.Buffered(1))]
    out_shape = [jax.ShapeDtypeStruct((n_tok, D_MODEL), F32), jax.ShapeDtypeStruct((IN_WIDTH, D_MODEL), BF16),
                 jax.ShapeDtypeStruct((1, D_MODEL), F32)]
    out_specs = [pl.BlockSpec((tm, D_MODEL), lambda i: (i, 0)),
                 pl.BlockSpec((IN_WIDTH, D_MODEL), lambda i: (0, 0), pipeline_mode=pl.Buffered(1)),
                 pl.BlockSpec((1, D_MODEL), lambda i: (0, 0))]
    scratch = [pltpu.VMEM((IN_WIDTH, D_MODEL), F32)]
    args = [dproj, x, dout, g_pre, w_in_t]
    if allgather is not None:
        in_specs.append(ANY)
        args.append(allgather)
        out_shape.append(jax.ShapeDtypeStruct((N_DEV,) + allgather.shape, allgather.dtype))
        out_specs.append(ANY)
        scratch += _comm_scratch(N_EXCHANGE_SEMS)
    return pl.pallas_call(
        body, name="bwd_in" if allgather is None else "bwd_in_allgather", grid=(n_tiles,),
        out_shape=tuple(out_shape), in_specs=in_specs, out_specs=tuple(out_specs), scratch_shapes=scratch,
        compiler_params=_cparams(1),
    )(*args)


def _adam_update(w, g, m, v):
    nm = ADAM_B1 * m + (1.0 - ADAM_B1) * g
    nv = ADAM_B2 * v + (1.0 - ADAM_B2) * (g * g)
    m_hat = nm / (1.0 - ADAM_B1 ** ADAM_STEP)
    v_hat = nv / (1.0 - ADAM_B2 ** ADAM_STEP)
    return -ADAM_LR * (m_hat / (jnp.sqrt(v_hat) + ADAM_EPS) + ADAM_WD * w), nm, nv


def _sum_slots(p_ref):
    tot = p_ref[0].astype(F32)
    for d in range(1, N_DEV):
        tot = tot + p_ref[d].astype(F32)
    return tot


def _adamw_parts(parts, w, m, v, layer, results, name):
    n_layers, n_rows, n_cols = w.shape
    tr = n_rows // 2 if n_rows * n_cols > 512 * 1024 else n_rows

    def body(*refs):
        p_ref, w_ref, m_ref, v_ref = refs[:4]
        g_ref, d_ref, nm_ref, nv_ref = refs[-4:]
        g = _sum_slots(p_ref)
        g_ref[0] = g
        d_ref[0], nm_ref[0], nv_ref[0] = _adam_update(w_ref[0], g, m_ref[0], v_ref[0])

    spec = pl.BlockSpec((1, tr, n_cols), lambda i: (layer, i, 0))
    shape = jax.ShapeDtypeStruct((n_layers, n_rows, n_cols), F32)
    kept = [] if results is None else list(results)
    return pl.pallas_call(
        body, name="adamw_" + name, grid=(n_rows // tr,), out_shape=(shape,) * 4,
        in_specs=[pl.BlockSpec((N_DEV, tr, n_cols), lambda i: (0, i, 0)), spec, spec, spec] + [ANY] * len(kept),
        out_specs=(spec,) * 4, input_output_aliases={4 + j: j for j in range(len(kept))},
        compiler_params=_cparams(1),
    )(parts, w, m, v, *kept)


def _adamw(w, g, m, v, name):
    n_rows, n_cols = w.shape
    tr = 512 if n_rows % 512 == 0 else n_rows

    def body(w_ref, g_ref, m_ref, v_ref, d_ref, nm_ref, nv_ref):
        d_ref[...], nm_ref[...], nv_ref[...] = _adam_update(w_ref[...], g_ref[...], m_ref[...], v_ref[...])

    spec = pl.BlockSpec((tr, n_cols), lambda i: (i, 0))
    shape = jax.ShapeDtypeStruct((n_rows, n_cols), F32)
    return pl.pallas_call(
        body, name="adamw_" + name, grid=(n_rows // tr,), out_shape=(shape, shape, shape),
        in_specs=[spec] * 4, out_specs=(spec, spec, spec), compiler_params=_cparams(1),
    )(w, g, m, v)


_SMALL = ("pre_norm_g", "ln_v_g", "ln_v_b", "b_spatial", "sinks", "rel_bias", "post_norm_g")


def _pack(pieces):
    blocks, first_rows, n = [], [], 0
    for p in pieces:
        flat = p.reshape(-1)
        flat = jnp.concatenate([flat, jnp.zeros(((-flat.shape[0]) % (8 * 128),), F32)]).reshape(-1, 128)
        blocks.append(flat)
        first_rows.append(n)
        n += flat.shape[0]
    return jnp.concatenate(blocks, axis=0), first_rows


def _unpack(block, first_row, shape):
    size = math.prod(shape)
    return block[first_row:first_row + -(-size // 128)].reshape(-1)[:size].reshape(shape)


def kernel(x, pre_norm_g, w_in, ln_v_g, ln_v_b, w_spatial, b_spatial, sinks, rel_bias, w_out, post_norm_g, loss_target, m_pre_norm_g, m_w_in, m_ln_v_g, m_ln_v_b, m_w_spatial, m_b_spatial, m_sinks, m_rel_bias, m_w_out, m_post_norm_g, v_pre_norm_g, v_w_in, v_ln_v_g, v_ln_v_b, v_w_spatial, v_b_spatial, v_sinks, v_rel_bias, v_w_out, v_post_norm_g):
    weights = dict(pre_norm_g=pre_norm_g, w_in=w_in, ln_v_g=ln_v_g, ln_v_b=ln_v_b, w_spatial=w_spatial, b_spatial=b_spatial,
                   sinks=sinks, rel_bias=rel_bias, w_out=w_out, post_norm_g=post_norm_g)
    mom_m = dict(pre_norm_g=m_pre_norm_g, w_in=m_w_in, ln_v_g=m_ln_v_g, ln_v_b=m_ln_v_b, w_spatial=m_w_spatial,
                 b_spatial=m_b_spatial, sinks=m_sinks, rel_bias=m_rel_bias, w_out=m_w_out, post_norm_g=m_post_norm_g)
    mom_v = dict(pre_norm_g=v_pre_norm_g, w_in=v_w_in, ln_v_g=v_ln_v_g, ln_v_b=v_ln_v_b, w_spatial=v_w_spatial,
                 b_spatial=v_b_spatial, sinks=v_sinks, rel_bias=v_rel_bias, w_out=v_w_out, post_norm_g=v_post_norm_g)
    n_seq, seq_len, _ = x.shape
    n_layers = w_in.shape[0]
    x2 = x.reshape(n_seq * seq_len, D_MODEL)
    target2 = loss_target.reshape(n_seq * seq_len, D_MODEL)
    row = lambda p, l: p[l][None]

    wt_shards = jnp.swapaxes(w_in, 1, 2).astype(BF16)
    wo_shards = w_out.astype(BF16)
    bucket, inwin = _window_tables()
    bias = _bias_table(rel_bias, bucket, inwin)
    wcat, wtcat, bs = _spatial_tables(w_spatial, jnp.swapaxes(b_spatial, 1, 2))

    wt, wo = [None] * n_layers, [None] * n_layers
    wt[0], wo[0] = _gather_weights(wt_shards, wo_shards, 0)
    xs, saved = [x2], []
    for l in range(n_layers):
        a, q, kv, bz = _fwd_in(xs[-1], row(pre_norm_g, l), wt[l])
        mix_args = (xs[-1], a, q, kv, bz, row(ln_v_g, l), row(ln_v_b, l), wcat[l], bs[l], sinks[l], bias, wo[l],
                    row(post_norm_g, l), seq_len)
        if l + 1 < n_layers:
            xn, y, wt[l + 1], wo[l + 1] = _fwd_mix(*mix_args, gather=(wt_shards, wo_shards, l + 1))
        else:
            xn, y = _fwd_mix(*mix_args)
        saved.append((a, q, kv, bz, y))
        xs.append(xn)
    dx, loss = _loss_head(xs[-1], target2)

    small = [None] * n_layers
    parts = [None] * n_layers
    waiting = None
    for l in reversed(range(n_layers)):
        a, q, kv, bz, y = saved[l]
        outs = _bwd_mix(dx, y, a, q, kv, bz, row(ln_v_g, l), row(ln_v_b, l), wcat[l], wtcat[l], bs[l], sinks[l], bias,
                        wo[l], row(post_norm_g, l), seq_len, exchange=waiting)
        dproj, dwo, dws, dmix, dlng, dlnb, dgpost, dsink, dbias = outs[:9]
        exchanged = outs[9:]
        dx, dwt, dgpre, ws_all = _bwd_in(dproj, xs[l], dx, row(pre_norm_g, l), wt[l],
                                         allgather=dws.reshape(HEADS * CHUNK, CHUNK))
        waiting = (dwt, dwo)
        parts[l] = [None, None, ws_all]
        small[l] = dict(pre_norm_g=dgpre[0], ln_v_g=dlng[0], ln_v_b=dlnb[0], dmix=dmix, sinks=dsink, dbias=dbias,
                        post_norm_g=dgpost[0])
        if l + 1 < n_layers:
            parts[l + 1][0:2] = exchanged
    parts[0][0:2] = _exchange_grads(*waiting)

    db = _b_spatial_grad(jnp.stack([s["dmix"] for s in small]))
    pieces = {n: jnp.stack([s[n] for s in small]) for n in ("pre_norm_g", "ln_v_g", "ln_v_b", "sinks", "post_norm_g")}
    pieces["b_spatial"] = jnp.swapaxes(db[:, :, :HEADS], 1, 2)
    pieces["rel_bias"] = _rel_bias_grad(jnp.stack([s["dbias"] for s in small]), bucket, inwin)
    packed, first_rows = _pack([pieces[n] for n in _SMALL] + [loss])
    total = _allreduce_small(packed)
    grad = {n: _unpack(total, r, weights[n].shape) for n, r in zip(_SMALL, first_rows)}
    loss_out = total[first_rows[-1], 0]

    delta, new_m, new_v = {}, {}, {}
    for n in _SMALL:
        w = weights[n]
        two_d = (-1, w.shape[-1])
        d, nm, nv = _adamw(w.reshape(two_d), grad[n].reshape(two_d), mom_m[n].reshape(two_d), mom_v[n].reshape(two_d), n)
        delta[n], new_m[n], new_v[n] = d.reshape(w.shape), nm.reshape(w.shape), nv.reshape(w.shape)

    t3 = lambda p: jnp.swapaxes(p, 1, 2)
    flat3 = lambda p: p.reshape(n_layers, HEADS * CHUNK, CHUNK)
    for n, which, view, back in (("w_in", 0, t3, t3), ("w_out", 1, lambda p: p, lambda p: p),
                                 ("w_spatial", 2, flat3, lambda p: p.reshape(w_spatial.shape))):
        results = None
        for l in reversed(range(n_layers)):
            results = _adamw_parts(parts[l][which], view(weights[n]), view(mom_m[n]), view(mom_v[n]), l, results, n)
        grad[n], delta[n], new_m[n], new_v[n] = (back(r) for r in results)

    names = tuple(weights)
    return (loss_out, dx.reshape(x.shape), *[grad[n] for n in names], *[delta[n] for n in names],
            *[new_m[n] for n in names], *[new_v[n] for n in names])
```

```python
import math

import jax
import jax.numpy as jnp
from jax import lax
from jax.experimental import pallas as pl
from jax.experimental.pallas import tpu as pltpu

F32 = jnp.float32
BF16 = jnp.bfloat16

D_MODEL = 1024
A_WIDTH = 512
B_WIDTH = 512
KV_WIDTH = 128
IN_WIDTH = 3 * A_WIDTH + 2 * B_WIDTH + 2 * KV_WIDTH
CHUNK = 128
HEADS = 8
HEAD_DIM = 64
REL_BUCKETS = 32
NORM_EPS = 1e-6
NEG = -1e30
Q_OFF = 3 * A_WIDTH
K_OFF = Q_OFF + B_WIDTH
BZ_OFF = K_OFF + 2 * KV_WIDTH
QK_SCALE = HEAD_DIM ** -0.5

ADAM_LR = 0.001
ADAM_B1 = 0.9
ADAM_B2 = 0.999
ADAM_EPS = 1e-08
ADAM_WD = 0.01
ADAM_STEP = 10

TOK_TILE = 512
VMEM_LIMIT_V7X = 60 * 1024 * 1024

N_DEV = 8
N_CHIPS = 4
MESH_ID = pl.DeviceIdType.MESH
ANY = pl.BlockSpec(memory_space=pl.ANY)
SMEM = pl.BlockSpec(memory_space=pltpu.SMEM)


def _cparams(n_axes):
    return pltpu.CompilerParams(dimension_semantics=("arbitrary",) * n_axes, vmem_limit_bytes=VMEM_LIMIT_V7X)


_GELU_C = math.sqrt(2.0 / math.pi)


def _gelu(x):
    inner = _GELU_C * (x + 0.044715 * (x * x * x))
    return x * (0.5 * (1.0 + jnp.tanh(inner)))


def _gelu_and_grad(x):
    x2 = x * x
    t = jnp.tanh(_GELU_C * (x + 0.044715 * (x2 * x)))
    cdf = 0.5 * (1.0 + t)
    d = cdf + x * (0.5 * (1.0 - t * t)) * (_GELU_C * (1.0 + 3.0 * 0.044715 * x2))
    return x * cdf, d


def _sigmoid(x):
    return 0.5 + 0.5 * jnp.tanh(0.5 * x)


def _silu(x):
    return x * _sigmoid(x)


def _silu_and_grad(x):
    s = _sigmoid(x)
    return x * s, s * (1.0 + x * (1.0 - s))


def _dot(a, b):
    return jnp.dot(a, b, preferred_element_type=F32)


def _dot_nt(a, b):
    return lax.dot_general(a, b, (((1,), (1,)), ((), ())), preferred_element_type=F32)


def _dot_tn(a, b):
    return lax.dot_general(a, b, (((0,), (0,)), ((), ())), preferred_element_type=F32)


def _lo_mask(shape):
    return lax.broadcasted_iota(jnp.int32, shape, 1) < HEAD_DIM


def _swap_halves(v):
    return pltpu.roll(v, HEAD_DIM, 1)


def _from_prev(shape=(CHUNK, CHUNK)):
    return lax.broadcasted_iota(jnp.int32, shape, 1) > lax.broadcasted_iota(jnp.int32, shape, 0)


def _window_buckets():
    q_loc = jnp.arange(CHUNK)[:, None]
    j_loc = jnp.arange(CHUNK)[None, :]
    d = q_loc - j_loc + jnp.where(j_loc > q_loc, CHUNK, 0)
    max_exact = REL_BUCKETS // 2
    safe = jnp.maximum(d, 1).astype(F32)
    large = max_exact + (jnp.log(safe / max_exact) / math.log(CHUNK / max_exact)
                         * (REL_BUCKETS - max_exact)).astype(jnp.int32)
    large = jnp.minimum(large, REL_BUCKETS - 1)
    return jnp.where(d < max_exact, d, large).astype(jnp.int32)


def _bias_table(rel_bias, bucket):
    def body(rb_ref, bk_ref, out_ref):
        bk = bk_ref[...]
        for h in range(HEADS):
            acc = jnp.zeros(bk.shape, F32)
            for b in range(REL_BUCKETS):
                acc = jnp.where(bk == b, rb_ref[b, h], acc)
            out_ref[h] = acc

    vmem = pl.BlockSpec(memory_space=pltpu.VMEM)
    return pl.pallas_call(
        body, name="bias_table",
        out_shape=jax.ShapeDtypeStruct((HEADS, CHUNK, CHUNK), F32),
        in_specs=[SMEM, vmem], out_specs=vmem,
    )(rel_bias, bucket)


def _rel_bias_grad(dbias, bucket):
    n_layers = dbias.shape[0]

    def body(db_ref, bk_ref, out_ref):
        bk = bk_ref[...]
        for h in range(HEADS):
            tot = db_ref[0, h]
            for l in range(1, n_layers):
                tot = tot + db_ref[l, h]
            for b in range(REL_BUCKETS):
                out_ref[b, h] = jnp.sum(jnp.where(bk == b, tot, 0.0))

    vmem = pl.BlockSpec(memory_space=pltpu.VMEM)
    return pl.pallas_call(
        body, name="rel_bias_grad",
        out_shape=jax.ShapeDtypeStruct((REL_BUCKETS, HEADS), F32),
        in_specs=[vmem] * 2, out_specs=SMEM,
    )(dbias, bucket)


def _spatial_tables(w_spatial, b_spatial_t):
    n_layers = w_spatial.shape[0]

    def body(w_ref, b_ref, wcat_ref, wtcat_ref, bs_ref):
        row = lax.broadcasted_iota(jnp.int32, (CHUNK, CHUNK), 0)
        col = lax.broadcasted_iota(jnp.int32, (CHUNK, CHUNK), 1)
        causal = col <= row
        lo = _lo_mask((CHUNK, CHUNK))
        for p in range(4):
            for half in range(2):
                w = jnp.where(causal, w_ref[0, 2 * p + half], 0.0)
                wcat_ref[0, p, :, half * CHUNK:(half + 1) * CHUNK] = w.astype(BF16)
                wtcat_ref[0, p, :, half * CHUNK:(half + 1) * CHUNK] = w.T.astype(BF16)
            b = b_ref[0]
            bs_ref[0, p] = jnp.where(lo, b[:, 2 * p:2 * p + 1], b[:, 2 * p + 1:2 * p + 2])

    return pl.pallas_call(
        body, name="spatial_tables", grid=(n_layers,),
        out_shape=(jax.ShapeDtypeStruct((n_layers, 4, CHUNK, 2 * CHUNK), BF16),
                   jax.ShapeDtypeStruct((n_layers, 4, CHUNK, 2 * CHUNK), BF16),
                   jax.ShapeDtypeStruct((n_layers, 4, CHUNK, CHUNK), F32)),
        in_specs=[pl.BlockSpec((1, HEADS, CHUNK, CHUNK), lambda l: (l, 0, 0, 0)),
                  pl.BlockSpec((1, CHUNK, HEADS), lambda l: (l, 0, 0))],
        out_specs=(pl.BlockSpec((1, 4, CHUNK, 2 * CHUNK), lambda l: (l, 0, 0, 0)),
                   pl.BlockSpec((1, 4, CHUNK, 2 * CHUNK), lambda l: (l, 0, 0, 0)),
                   pl.BlockSpec((1, 4, CHUNK, CHUNK), lambda l: (l, 0, 0, 0))),
        compiler_params=_cparams(1),
    )(w_spatial, b_spatial_t)


def _b_spatial_grad(dmix):
    n_layers = dmix.shape[0]

    def body(d_ref, out_ref):
        lane = lax.broadcasted_iota(jnp.int32, (CHUNK, CHUNK), 1)
        acc = jnp.zeros((CHUNK, CHUNK), F32)
        for p in range(4):
            t = d_ref[0, :, p * CHUNK:(p + 1) * CHUNK]
            s_lo = jnp.sum(jnp.where(lane < HEAD_DIM, t, 0.0), axis=1, keepdims=True)
            s_hi = jnp.sum(jnp.where(lane < HEAD_DIM, 0.0, t), axis=1, keepdims=True)
            acc = jnp.where(lane == 2 * p, s_lo, acc)
            acc = jnp.where(lane == 2 * p + 1, s_hi, acc)
        out_ref[0] = acc

    return pl.pallas_call(
        body, name="b_spatial_grad", grid=(n_layers,),
        out_shape=jax.ShapeDtypeStruct((n_layers, CHUNK, CHUNK), F32),
        in_specs=[pl.BlockSpec((1, CHUNK, A_WIDTH), lambda l: (l, 0, 0))],
        out_specs=pl.BlockSpec((1, CHUNK, CHUNK), lambda l: (l, 0, 0)),
        compiler_params=_cparams(1),
    )(dmix)


def _place():
    x, y, c = lax.axis_index("x"), lax.axis_index("y"), lax.axis_index("c")
    other_chips = [(1 - x, y), (x, 1 - y), (1 - x, 1 - y)]
    return x, y, c, other_chips


N_GATHER_SEMS = 12


def _gather_phases(shards, fulls, send_sems, recv_sems, local_sems):
    x, y, c, chips = _place()
    sibling = (x, y, 1 - c)
    n_arr = len(shards)

    def half_rows(a, chip, half):
        n = shards[a].shape[0]
        start = (2 * chip[0] + chip[1]) * n + half * (n // 2)
        return fulls[a].at[pl.ds(pl.multiple_of(start, 16), n // 2), :]

    def my_half(a):
        n = shards[a].shape[0]
        return shards[a].at[pl.ds(pl.multiple_of(c * (n // 2), 16), n // 2), :]

    def copy(k, a, src, chip, half, to):
        return pltpu.make_async_remote_copy(
            src_ref=src, dst_ref=half_rows(a, chip, half), send_sem=send_sems.at[n_arr * k + a],
            recv_sem=recv_sems.at[n_arr * k + a], device_id=to, device_id_type=MESH_ID)

    def local(a):
        n = shards[a].shape[0]
        mine = fulls[a].at[pl.ds(pl.multiple_of((2 * x + y) * n, 16), n), :]
        return pltpu.make_async_copy(shards[a], mine, local_sems.at[a])

    def first(k, a):
        return copy(k, a, my_half(a), (x, y), c, (chips[k][0], chips[k][1], c))

    def passed(k, a):
        return copy(3 + k, a, half_rows(a, chips[k], c), chips[k], c, sibling)

    def phase_a():
        for a in range(n_arr):
            local(a).start()
        for k in range(3):
            for a in range(n_arr):
                first(k, a).start()

    def phase_b():
        for k in range(3):
            for a in range(n_arr):
                copy(k, a, my_half(a), chips[k], c, sibling).wait_recv()
                passed(k, a).start()

    def phase_c():
        for k in range(3):
            for a in range(n_arr):
                copy(3 + k, a, my_half(a), chips[k], 1 - c, sibling).wait_recv()
        for k in range(3):
            for a in range(n_arr):
                first(k, a).wait_send()
                passed(k, a).wait_send()
        for a in range(n_arr):
            local(a).wait()

    return phase_a, phase_b, phase_c


N_EXCHANGE_SEMS = 14


def _exchange_phases(partials, parts, send_sems, recv_sems, local_sems, by_chip=True):
    x, y, c, chips = _place()
    me, sibling = (x, y, c), (x, y, 1 - c)
    n_arr = len(partials)

    def block(a, chip):
        if not by_chip:
            return partials[a]
        n = partials[a].shape[0] // N_CHIPS
        return partials[a].at[pl.ds(pl.multiple_of((2 * chip[0] + chip[1]) * n, 16), n), :]

    def slot(a, dev):
        return parts[a].at[4 * dev[0] + 2 * dev[1] + dev[2]]

    def copy(k, a, src, origin, to):
        return pltpu.make_async_remote_copy(
            src_ref=src, dst_ref=slot(a, origin), send_sem=send_sems.at[n_arr * k + a],
            recv_sem=recv_sems.at[n_arr * k + a], device_id=to, device_id_type=MESH_ID)

    def local(a):
        return pltpu.make_async_copy(block(a, (x, y)), slot(a, me), local_sems.at[a])

    def first(k, a):
        if k == 0:
            return copy(0, a, block(a, (x, y)), me, sibling)
        chip = chips[k - 1]
        return copy(k, a, block(a, chip), me, (chip[0], chip[1], c))

    def passed(k, a):
        origin = (chips[k][0], chips[k][1], c)
        return copy(4 + k, a, slot(a, origin), origin, sibling)

    def phase_a():
        for a in range(n_arr):
            local(a).start()
        for k in range(4):
            for a in range(n_arr):
                first(k, a).start()

    def phase_b():
        for k in range(3):
            for a in range(n_arr):
                copy(1 + k, a, block(a, (x, y)), (chips[k][0], chips[k][1], c), me).wait_recv()
                passed(k, a).start()

    def phase_c():
        for a in range(n_arr):
            copy(0, a, block(a, (x, y)), sibling, me).wait_recv()
        for k in range(3):
            for a in range(n_arr):
                copy(4 + k, a, block(a, (x, y)), (chips[k][0], chips[k][1], 1 - c), me).wait_recv()
        for k in range(4):
            for a in range(n_arr):
                first(k, a).wait_send()
        for k in range(3):
            for a in range(n_arr):
                passed(k, a).wait_send()
        for a in range(n_arr):
            local(a).wait()

    return phase_a, phase_b, phase_c


def _comm_scratch(n_sems):
    return [pltpu.SemaphoreType.DMA((n_sems,)), pltpu.SemaphoreType.DMA((n_sems,)), pltpu.SemaphoreType.DMA((2,))]


def _gather_weights(wt_shards, wo_shards, layer):
    wt_rows, wo_rows = wt_shards.shape[1], wo_shards.shape[1]

    def body(wt_ref, wo_ref, wt_full, wo_full, send_sems, recv_sems, local_sems):
        phases = _gather_phases((wt_ref.at[layer], wo_ref.at[layer]), (wt_full, wo_full), send_sems, recv_sems, local_sems)
        for phase in phases:
            phase()

    return pl.pallas_call(
        body, name="gather_weights",
        out_shape=(jax.ShapeDtypeStruct((N_CHIPS * wt_rows, D_MODEL), BF16),
                   jax.ShapeDtypeStruct((N_CHIPS * wo_rows, D_MODEL), BF16)),
        in_specs=[ANY, ANY], out_specs=(ANY, ANY), scratch_shapes=_comm_scratch(N_GATHER_SEMS),
    )(wt_shards, wo_shards)


def _exchange_grads(dwt, dwo, small):
    def body(dwt_ref, dwo_ref, small_ref, pt_ref, po_ref, all_ref, *sems):
        blocks = _exchange_phases((dwt_ref, dwo_ref), (pt_ref, po_ref), *sems[:3])
        whole = _exchange_phases((small_ref,), (all_ref,), *sems[3:], by_chip=False)
        for phase_of_blocks, phase_of_whole in zip(blocks, whole):
            phase_of_blocks()
            phase_of_whole()

    return pl.pallas_call(
        body, name="exchange_grads",
        out_shape=(jax.ShapeDtypeStruct((N_DEV, dwt.shape[0] // N_CHIPS, D_MODEL), BF16),
                   jax.ShapeDtypeStruct((N_DEV, dwo.shape[0] // N_CHIPS, D_MODEL), BF16),
                   jax.ShapeDtypeStruct((N_DEV,) + small.shape, small.dtype)),
        in_specs=[ANY, ANY, ANY], out_specs=(ANY, ANY, ANY),
        scratch_shapes=_comm_scratch(N_EXCHANGE_SEMS) + _comm_scratch(N_EXCHANGE_SEMS),
    )(dwt, dwo, small)


def _allreduce_small(part):
    n_rows = part.shape[0]

    def body(p_ref, tot_ref, all_ref, send_sems, recv_sems, local_sem):
        x, y, c, chips = _place()
        me, sibling = (x, y, c), (x, y, 1 - c)

        def rows(dev):
            return all_ref.at[pl.ds(pl.multiple_of((4 * dev[0] + 2 * dev[1] + dev[2]) * n_rows, 8), n_rows), :]

        def copy(k, origin, to, src=None):
            return pltpu.make_async_remote_copy(
                src_ref=rows(origin) if src is None else src, dst_ref=rows(origin), send_sem=send_sems.at[k],
                recv_sem=recv_sems.at[k], device_id=to, device_id_type=MESH_ID)

        mine = pltpu.make_async_copy(p_ref, rows(me), local_sem)
        mine.start()
        first = [copy(0, me, sibling, src=p_ref)]
        first += [copy(1 + k, me, (chip[0], chip[1], c), src=p_ref) for k, chip in enumerate(chips)]
        for cp in first:
            cp.start()
        passed = []
        for k, chip in enumerate(chips):
            origin = (chip[0], chip[1], c)
            copy(1 + k, origin, me).wait_recv()
            fwd = copy(4 + k, origin, sibling)
            fwd.start()
            passed.append(fwd)
        copy(0, sibling, me).wait_recv()
        for k, chip in enumerate(chips):
            copy(4 + k, (chip[0], chip[1], 1 - c), me).wait_recv()
        for cp in first + passed:
            cp.wait_send()
        mine.wait()
        tot = all_ref[0:n_rows, :]
        for d in range(1, N_DEV):
            tot = tot + all_ref[d * n_rows:(d + 1) * n_rows, :]
        tot_ref[...] = tot

    vmem = pl.BlockSpec(memory_space=pltpu.VMEM)
    return pl.pallas_call(
        body, name="allreduce_small",
        out_shape=jax.ShapeDtypeStruct((n_rows, 128), F32),
        in_specs=[vmem], out_specs=vmem,
        scratch_shapes=[pltpu.VMEM((N_DEV * n_rows, 128), F32), pltpu.SemaphoreType.DMA((7,)),
                        pltpu.SemaphoreType.DMA((7,)), pltpu.SemaphoreType.DMA],
        compiler_params=pltpu.CompilerParams(vmem_limit_bytes=VMEM_LIMIT_V7X),
    )(part)


def _hosted(phases, step, n_steps):
    phase_a, phase_b, phase_c = phases

    def at_start():
        pl.when(step == 0)(phase_a)

    def at_end():
        pl.when(step == n_steps // 2)(phase_b)
        pl.when(step == n_steps - 1)(phase_c)

    return at_start, at_end


def _fwd_in(x, g_pre, w_in_t):
    n_tok = x.shape[0]

    def body(x_ref, g_ref, w_ref, a_ref, q_ref, kv_ref, bz_ref):
        xf = x_ref[...]
        r = lax.rsqrt(jnp.mean(xf * xf, axis=-1, keepdims=True) + NORM_EPS)
        h = ((xf * r) * g_ref[...]).astype(BF16)
        a_ref[...] = _dot_nt(h, w_ref[0:Q_OFF, :])
        q_ref[...] = _dot_nt(h, w_ref[Q_OFF:K_OFF, :]).astype(BF16)
        kv_ref[...] = _dot_nt(h, w_ref[K_OFF:BZ_OFF, :]).astype(BF16)
        bz_ref[...] = _dot_nt(h, w_ref[BZ_OFF:IN_WIDTH, :])

    tm = TOK_TILE
    return pl.pallas_call(
        body, name="fwd_in", grid=(n_tok // tm,),
        out_shape=(jax.ShapeDtypeStruct((n_tok, Q_OFF), F32), jax.ShapeDtypeStruct((n_tok, B_WIDTH), BF16),
                   jax.ShapeDtypeStruct((n_tok, 2 * KV_WIDTH), BF16), jax.ShapeDtypeStruct((n_tok, B_WIDTH), F32)),
        in_specs=[pl.BlockSpec((tm, D_MODEL), lambda i: (i, 0)), pl.BlockSpec((1, D_MODEL), lambda i: (0, 0)),
                  pl.BlockSpec((IN_WIDTH, D_MODEL), lambda i: (0, 0))],
        out_specs=(pl.BlockSpec((tm, Q_OFF), lambda i: (i, 0)), pl.BlockSpec((tm, B_WIDTH), lambda i: (i, 0)),
                   pl.BlockSpec((tm, 2 * KV_WIDTH), lambda i: (i, 0)), pl.BlockSpec((tm, B_WIDTH), lambda i: (i, 0))),
        compiler_params=_cparams(1),
    )(x, g_pre, w_in_t)


def _layer_norm_stats(vv):
    mu = jnp.mean(vv, axis=-1, keepdims=True)
    xc = vv - mu
    rs = lax.rsqrt(jnp.mean(xc * xc, axis=-1, keepdims=True) + NORM_EPS)
    return xc * rs, rs


def _blockdiag(v, lo):
    zero = jnp.zeros_like(v)
    return jnp.concatenate([jnp.where(lo, v, zero), jnp.where(lo, zero, v)], axis=0)


def _softmax_sink(s, sink):
    m = jnp.maximum(jnp.max(s, axis=-1, keepdims=True), sink)
    e = jnp.exp(s - m)
    esink = jnp.exp(sink - m)
    den = jnp.sum(e, axis=-1, keepdims=True) + esink
    return e / den, esink / den


def _kv_rows(cur_ref, halo_ref, r0, c):
    prev_in_tile = cur_ref[pl.ds(pl.multiple_of(jnp.maximum(r0 - CHUNK, 0), CHUNK), CHUNK), :]
    prev = jnp.where(c == 0, halo_ref[...], prev_in_tile)
    kv2 = jnp.concatenate([prev, cur_ref[pl.ds(r0, CHUNK), :]], axis=0)
    k2, v2 = kv2[:, 0:KV_WIDTH], kv2[:, KV_WIDTH:2 * KV_WIDTH]
    return (k2, _swap_halves(k2)), (v2, _swap_halves(v2))


def _window_square(over_keys, prev):
    return jnp.where(prev, over_keys[:, 0:CHUNK], over_keys[:, CHUNK:2 * CHUNK])


def _window_keys(square, prev):
    zero = jnp.zeros_like(square)
    return jnp.concatenate([jnp.where(prev, square, zero), jnp.where(prev, zero, square)], axis=1)


def _dead_mask(first):
    shape = (CHUNK, CHUNK)
    ahead = lax.broadcasted_iota(jnp.int32, shape, 1) - lax.broadcasted_iota(jnp.int32, shape, 0)
    return ahead > jnp.where(first, 0, CHUNK)


def _head_of(p, half):
    return 2 * p + half, int(half != p // 2)


def _fwd_mix(x, a, q, kv, bz, ln_g, ln_b, wcat, bs, sinks, bias, w_out, g_post, seq_len, gather=None):
    n_tok = x.shape[0]
    tb = TOK_TILE
    n_chunks = tb // CHUNK
    n_tiles = n_tok // tb
    n_in = 14

    def body(*refs):
        (x_ref, a_ref, q_ref, kv_ref, halo_ref, bz_ref, lng_ref, lnb_ref, wcat_ref, bs_ref, sink_ref, bias_ref,
         wout_ref, gpost_ref) = refs[:n_in]
        if gather is None:
            xn_ref, y_ref, ycat_s = refs[n_in:]
            at_start = at_end = lambda: None
        else:
            wts_ref, wos_ref, xn_ref, y_ref, wt_full, wo_full, ycat_s, send_sems, recv_sems, local_sems = refs[n_in:]
            phases = _gather_phases((wts_ref.at[gather[2]], wos_ref.at[gather[2]]), (wt_full, wo_full), send_sems,
                                    recv_sems, local_sems)
            at_start, at_end = _hosted(phases, pl.program_id(0), n_tiles)
        at_start()
        i = pl.program_id(0)
        lo = _lo_mask((CHUNK, CHUNK))
        prev = _from_prev()

        def chunk(c, carry):
            r0 = pl.multiple_of(c * CHUNK, CHUNK)
            rows = pl.ds(r0, CHUNK)
            u = _gelu(a_ref[rows, 0:A_WIDTH])
            vv = _gelu(a_ref[rows, A_WIDTH:2 * A_WIDTH])
            xhat, _ = _layer_norm_stats(vv)
            vnb = (xhat * lng_ref[...] + lnb_ref[...]).astype(BF16)
            for p in range(4):
                blk = slice(p * CHUNK, (p + 1) * CHUNK)
                mixed = _dot(wcat_ref[p], _blockdiag(vnb[:, blk], lo)) + bs_ref[p]
                sz = _silu(a_ref[rows, 2 * A_WIDTH + p * CHUNK:2 * A_WIDTH + (p + 1) * CHUNK])
                ycat_s[rows, blk] = ((u[:, blk] * mixed) * sz).astype(BF16)
            ks, vs = _kv_rows(kv_ref, halo_ref, r0, c)
            dead = _dead_mask(lax.rem(i * tb + r0, seq_len) == 0)
            for p in range(4):
                blk = slice(p * CHUNK, (p + 1) * CHUNK)
                qt = q_ref[rows, blk]
                zero = jnp.zeros_like(qt)
                res = []
                for half in range(2):
                    head, sw = _head_of(p, half)
                    qm = jnp.where(lo, qt, zero) if half == 0 else jnp.where(lo, zero, qt)
                    s = _window_square(_dot_nt(qm, ks[sw]), prev) * QK_SCALE + bias_ref[head]
                    s = jnp.where(dead, NEG, s)
                    prob, _ = _softmax_sink(s, sink_ref[head])
                    res.append(_dot(_window_keys(prob.astype(BF16), prev), vs[sw]))
                o = jnp.where(lo, res[0], res[1])
                ycat_s[rows, B_WIDTH + p * CHUNK:B_WIDTH + (p + 1) * CHUNK] = (o * _silu(bz_ref[rows, blk])).astype(BF16)
            return carry

        lax.fori_loop(0, n_chunks, chunk, 0)
        y = _dot(ycat_s[...], wout_ref[...])
        r = lax.rsqrt(jnp.mean(y * y, axis=-1, keepdims=True) + NORM_EPS)
        y_ref[...] = y
        xn_ref[...] = x_ref[...] + (y * r) * gpost_ref[...]
        at_end()

    tile = lambda w: pl.BlockSpec((tb, w), lambda i: (i, 0))
    whole = lambda shape: pl.BlockSpec(shape, lambda i: (0,) * len(shape))
    in_specs = [tile(D_MODEL), tile(Q_OFF), tile(B_WIDTH), tile(2 * KV_WIDTH),
                pl.BlockSpec((CHUNK, 2 * KV_WIDTH), lambda i: (jnp.maximum(i * n_chunks - 1, 0), 0)),
                tile(B_WIDTH), whole((1, A_WIDTH)), whole((1, A_WIDTH)), whole((4, CHUNK, 2 * CHUNK)),
                whole((4, CHUNK, CHUNK)), SMEM, whole((HEADS, CHUNK, CHUNK)), whole((D_MODEL, D_MODEL)),
                whole((1, D_MODEL))]
    out_shape = [jax.ShapeDtypeStruct((n_tok, D_MODEL), F32), jax.ShapeDtypeStruct((n_tok, D_MODEL), F32)]
    out_specs = [tile(D_MODEL), tile(D_MODEL)]
    scratch = [pltpu.VMEM((tb, D_MODEL), BF16)]
    args = [x, a, q, kv, kv, bz, ln_g, ln_b, wcat, bs, sinks, bias, w_out, g_post]
    if gather is not None:
        wt_shards, wo_shards, _ = gather
        in_specs += [ANY, ANY]
        args += [wt_shards, wo_shards]
        out_shape += [jax.ShapeDtypeStruct((N_CHIPS * wt_shards.shape[1], D_MODEL), BF16),
                      jax.ShapeDtypeStruct((N_CHIPS * wo_shards.shape[1], D_MODEL), BF16)]
        out_specs += [ANY, ANY]
        scratch += _comm_scratch(N_GATHER_SEMS)
    return pl.pallas_call(
        body, name="fwd_mix" if gather is None else "fwd_mix_gather", grid=(n_tiles,),
        out_shape=tuple(out_shape), in_specs=in_specs, out_specs=tuple(out_specs), scratch_shapes=scratch,
        compiler_params=_cparams(1),
    )(*args)


def _loss_head(x, target):
    n_tok = x.shape[0]
    tm = TOK_TILE

    def body(x_ref, t_ref, dx_ref, loss_ref):
        d = x_ref[...] - t_ref[...]
        dx_ref[...] = d * (1.0 / D_MODEL)
        part = 0.5 * jnp.sum(jnp.mean(d * d, axis=-1, keepdims=True))

        @pl.when(pl.program_id(0) == 0)
        def _():
            loss_ref[0, 0] = 0.0

        loss_ref[0, 0] += part

    return pl.pallas_call(
        body, name="loss_head", grid=(n_tok // tm,),
        out_shape=(jax.ShapeDtypeStruct((n_tok, D_MODEL), F32), jax.ShapeDtypeStruct((1, 1), F32)),
        in_specs=[pl.BlockSpec((tm, D_MODEL), lambda i: (i, 0))] * 2,
        out_specs=(pl.BlockSpec((tm, D_MODEL), lambda i: (i, 0)), SMEM),
        compiler_params=_cparams(1),
    )(x, target)


def _bwd_mix(dout, y, a, q, kv, bz, ln_g, ln_b, wcat, wtcat, bs, sinks, bias, w_out, g_post, seq_len, exchange=None):
    n_tok = y.shape[0]
    tb = TOK_TILE
    n_chunks = tb // CHUNK
    n_tiles = n_tok // tb
    n_in, n_out = 16, 9

    def body(*refs):
        (dout_ref, y_ref, a_ref, q_ref, kv_ref, halo_ref, bz_ref, lng_ref, lnb_ref, wcat_ref, wtcat_ref, bs_ref,
         sink_ref, bias_ref, wout_ref, gpost_ref) = refs[:n_in]
        if exchange is None:
            outs, rest = refs[n_in:n_in + n_out], refs[n_in + n_out:]
            at_start = at_end = lambda: None
        else:
            dwt_ref, dwo_ref = refs[n_in:n_in + 2]
            outs = refs[n_in + 2:n_in + 2 + n_out]
            pt_ref, po_ref = refs[n_in + 2 + n_out:n_in + 4 + n_out]
            rest = refs[n_in + 4 + n_out:]
            phases = _exchange_phases((dwt_ref, dwo_ref), (pt_ref, po_ref), *rest[6:])
            at_start, at_end = _hosted(phases, pl.program_id(0), n_tiles)
        dproj_ref, dwout_ref, dws_ref, dmix_ref, dlng_ref, dlnb_ref, dgpost_ref, dsink_ref, dbias_ref = outs
        ycat_s, dy_s, dyc_s, dkv_s, carry_s, dwout_s = rest[:6]
        at_start()
        step = pl.program_id(0)
        tile_idx = n_tiles - 1 - step
        lo = _lo_mask((CHUNK, CHUNK))
        prev = _from_prev()

        @pl.when(step == 0)
        def _():
            dwout_s[...] = jnp.zeros_like(dwout_s)
            dws_ref[...] = jnp.zeros_like(dws_ref)
            dmix_ref[...] = jnp.zeros_like(dmix_ref)
            dlng_ref[...] = jnp.zeros_like(dlng_ref)
            dlnb_ref[...] = jnp.zeros_like(dlnb_ref)
            dgpost_ref[...] = jnp.zeros_like(dgpost_ref)
            dbias_ref[...] = jnp.zeros_like(dbias_ref)
            carry_s[...] = jnp.zeros_like(carry_s)
            for h in range(HEADS):
                dsink_ref[h] = 0.0

        yv = y_ref[...]
        dout = dout_ref[...]
        r = lax.rsqrt(jnp.mean(yv * yv, axis=-1, keepdims=True) + NORM_EPS)
        yn = yv * r
        dgpost_ref[...] += jnp.sum(dout * yn, axis=0, keepdims=True)
        dyn = dout * gpost_ref[...]
        dy = r * (dyn - yn * jnp.mean(dyn * yn, axis=-1, keepdims=True))
        dy_s[...] = dy.astype(BF16)
        dyc_s[...] = _dot_nt(dy_s[...], wout_ref[...])
        dkv_s[0:tb, :] = jnp.zeros((tb, 2 * KV_WIDTH), F32)
        dkv_s[tb:tb + CHUNK, :] = carry_s[...]

        def chunk(c, carry):
            r0 = pl.multiple_of(c * CHUNK, CHUNK)
            rows = pl.ds(r0, CHUNK)
            u, gu = _gelu_and_grad(a_ref[rows, 0:A_WIDTH])
            vv, gv = _gelu_and_grad(a_ref[rows, A_WIDTH:2 * A_WIDTH])
            xhat, rs = _layer_norm_stats(vv)
            vnb = (xhat * lng_ref[...] + lnb_ref[...]).astype(BF16)
            d_vn, d_u, d_az = [], [], []
            for p in range(4):
                blk = slice(p * CHUNK, (p + 1) * CHUNK)
                vbd = _blockdiag(vnb[:, blk], lo)
                mixed = _dot(wcat_ref[p], vbd) + bs_ref[p]
                sz, gz = _silu_and_grad(a_ref[rows, 2 * A_WIDTH + p * CHUNK:2 * A_WIDTH + (p + 1) * CHUNK])
                ub = u[:, blk]
                dya = dyc_s[rows, blk]
                um = ub * mixed
                ycat_s[rows, blk] = (um * sz).astype(BF16)
                d_mixed = (dya * ub) * sz
                d_u.append((dya * mixed) * sz)
                d_az.append((dya * um) * gz)
                dmix_ref[:, blk] += d_mixed
                dmbd = _blockdiag(d_mixed.astype(BF16), lo)
                d_vn.append(_dot(wtcat_ref[p], dmbd))
                dws_ref[p] += _dot_nt(dmbd, vnb[:, blk])
            d_vn = jnp.concatenate(d_vn, axis=1)
            dlng_ref[...] += jnp.sum(d_vn * xhat, axis=0, keepdims=True)
            dlnb_ref[...] += jnp.sum(d_vn, axis=0, keepdims=True)
            dxh = d_vn * lng_ref[...]
            d_vv = rs * (dxh - jnp.mean(dxh, axis=-1, keepdims=True)
                         - xhat * jnp.mean(dxh * xhat, axis=-1, keepdims=True))
            dproj_ref[rows, 0:A_WIDTH] = (jnp.concatenate(d_u, axis=1) * gu).astype(BF16)
            dproj_ref[rows, A_WIDTH:2 * A_WIDTH] = (d_vv * gv).astype(BF16)
            dproj_ref[rows, 2 * A_WIDTH:Q_OFF] = jnp.concatenate(d_az, axis=1).astype(BF16)
            ks, vs = _kv_rows(kv_ref, halo_ref, r0, c)
            dead = _dead_mask(lax.rem(tile_idx * tb + r0, seq_len) == 0)
            dk_acc =[jnp.zeros((2 * CHUNK, KV_WIDTH), F32), jnp.zeros((2 * CHUNK, KV_WIDTH), F32)]
            dv_acc = [jnp.zeros((2 * CHUNK, KV_WIDTH), F32), jnp.zeros((2 * CHUNK, KV_WIDTH), F32)]
            pending = {0: [], 1: []}
            for p in range(4):
                blk = slice(p * CHUNK, (p + 1) * CHUNK)
                qt = q_ref[rows, blk]
                zero = jnp.zeros_like(qt)
                sb, gb = _silu_and_grad(bz_ref[rows, blk])
                dyb = dyc_s[rows, B_WIDTH + p * CHUNK:B_WIDTH + (p + 1) * CHUNK]
                qms, probs, pws, psinks, outs_p = [], [], [], [], []
                for half in range(2):
                    head, sw = _head_of(p, half)
                    qm = jnp.where(lo, qt, zero) if half == 0 else jnp.where(lo, zero, qt)
                    s = _window_square(_dot_nt(qm, ks[sw]), prev) * QK_SCALE + bias_ref[head]
                    s = jnp.where(dead, NEG, s)
                    prob, psink = _softmax_sink(s, sink_ref[head])
                    pw = _window_keys(prob.astype(BF16), prev)
                    qms.append(qm)
                    probs.append(prob)
                    pws.append(pw)
                    psinks.append(psink)
                    outs_p.append(_dot(pw, vs[sw]))
                o = jnp.where(lo, outs_p[0], outs_p[1])
                ycat_s[rows, B_WIDTH + p * CHUNK:B_WIDTH + (p + 1) * CHUNK] = (o * sb).astype(BF16)
                dproj_ref[rows, BZ_OFF + p * CHUNK:BZ_OFF + (p + 1) * CHUNK] = ((dyb * o) * gb).astype(BF16)
                d_ob = (dyb * sb).astype(BF16)
                zero_o = jnp.zeros_like(d_ob)
                dqs = []
                for half in range(2):
                    head, sw = _head_of(p, half)
                    dom = jnp.where(lo, d_ob, zero_o) if half == 0 else jnp.where(lo, zero_o, d_ob)
                    dp = _window_square(_dot_nt(dom, vs[sw]), prev)
                    delta = jnp.sum(probs[half] * dp, axis=-1, keepdims=True)
                    ds = probs[half] * (dp - delta)
                    dsink_ref[head] += -jnp.sum(psinks[half] * delta)
                    dbias_ref[head] += ds
                    dsw = _window_keys(ds.astype(BF16), prev)
                    dqs.append(_dot(dsw, ks[sw]))
                    pending[sw].append((dsw, qms[half], pws[half], dom))
                    if len(pending[sw]) == 2:
                        (d1, q1, p1, o1), (d2, q2, p2, o2) = pending[sw]
                        dk_acc[sw] = dk_acc[sw] + _dot_tn(jnp.concatenate([d1, d2], axis=0),
                                                          jnp.concatenate([q1, q2], axis=0))
                        dv_acc[sw] = dv_acc[sw] + _dot_tn(jnp.concatenate([p1, p2], axis=0),
                                                          jnp.concatenate([o1, o2], axis=0))
                        pending[sw] = []
                dq = jnp.where(lo, dqs[0], dqs[1]) * QK_SCALE
                dproj_ref[rows, Q_OFF + p * CHUNK:Q_OFF + (p + 1) * CHUNK] = dq.astype(BF16)
            both = pl.ds(r0, 2 * CHUNK)
            dkv_s[both, 0:KV_WIDTH] += (dk_acc[0] + _swap_halves(dk_acc[1])) * QK_SCALE
            dkv_s[both, KV_WIDTH:2 * KV_WIDTH] += dv_acc[0] + _swap_halves(dv_acc[1])
            return carry

        lax.fori_loop(0, n_chunks, chunk, 0)
        dwout_s[...] += _dot_tn(ycat_s[...], dy_s[...])
        dproj_ref[:, K_OFF:BZ_OFF] = dkv_s[CHUNK:CHUNK + tb, :].astype(BF16)
        carry_s[...] = dkv_s[0:CHUNK, :]

        @pl.when(step == n_tiles - 1)
        def _():
            row = lax.broadcasted_iota(jnp.int32, (2 * CHUNK, CHUNK), 0)
            col = lax.broadcasted_iota(jnp.int32, (2 * CHUNK, CHUNK), 1)
            causal = col <= jnp.where(row >= CHUNK, row - CHUNK, row)
            for p in range(4):
                dws_ref[p] = jnp.where(causal, dws_ref[p], 0.0)
            dwout_ref[...] = dwout_s[...].astype(BF16)

        at_end()

    tile = lambda w: pl.BlockSpec((tb, w), lambda s: (n_tiles - 1 - s, 0))
    whole = lambda shape: pl.BlockSpec(shape, lambda s: (0,) * len(shape))
    in_specs = [tile(D_MODEL), tile(D_MODEL), tile(Q_OFF), tile(B_WIDTH), tile(2 * KV_WIDTH),
                pl.BlockSpec((CHUNK, 2 * KV_WIDTH), lambda s: (jnp.maximum((n_tiles - 1 - s) * n_chunks - 1, 0), 0)),
                tile(B_WIDTH), whole((1, A_WIDTH)), whole((1, A_WIDTH)), whole((4, CHUNK, 2 * CHUNK)),
                whole((4, CHUNK, 2 * CHUNK)), whole((4, CHUNK, CHUNK)), SMEM, whole((HEADS, CHUNK, CHUNK)),
                whole((D_MODEL, D_MODEL)), whole((1, D_MODEL))]
    out_shape = [jax.ShapeDtypeStruct((n_tok, IN_WIDTH), BF16), jax.ShapeDtypeStruct((D_MODEL, D_MODEL), BF16),
                 jax.ShapeDtypeStruct((4, 2 * CHUNK, CHUNK), F32), jax.ShapeDtypeStruct((CHUNK, A_WIDTH), F32),
                 jax.ShapeDtypeStruct((1, A_WIDTH), F32), jax.ShapeDtypeStruct((1, A_WIDTH), F32),
                 jax.ShapeDtypeStruct((1, D_MODEL), F32), jax.ShapeDtypeStruct((HEADS,), F32),
                 jax.ShapeDtypeStruct((HEADS, CHUNK, CHUNK), F32)]
    out_specs = [tile(IN_WIDTH), whole((D_MODEL, D_MODEL)), whole((4, 2 * CHUNK, CHUNK)), whole((CHUNK, A_WIDTH)),
                 whole((1, A_WIDTH)), whole((1, A_WIDTH)), whole((1, D_MODEL)), SMEM, whole((HEADS, CHUNK, CHUNK))]
    scratch = [pltpu.VMEM((tb, D_MODEL), BF16), pltpu.VMEM((tb, D_MODEL), BF16), pltpu.VMEM((tb, D_MODEL), F32),
               pltpu.VMEM((tb + CHUNK, 2 * KV_WIDTH), F32), pltpu.VMEM((CHUNK, 2 * KV_WIDTH), F32),
               pltpu.VMEM((D_MODEL, D_MODEL), F32)]
    args = [dout, y, a, q, kv, kv, bz, ln_g, ln_b, wcat, wtcat, bs, sinks, bias, w_out, g_post]
    if exchange is not None:
        dwt, dwo = exchange
        in_specs += [ANY, ANY]
        args += [dwt, dwo]
        out_shape += [jax.ShapeDtypeStruct((N_DEV, dwt.shape[0] // N_CHIPS, D_MODEL), BF16),
                      jax.ShapeDtypeStruct((N_DEV, dwo.shape[0] // N_CHIPS, D_MODEL), BF16)]
        out_specs += [ANY, ANY]
        scratch += _comm_scratch(N_EXCHANGE_SEMS)
    return pl.pallas_call(
        body, name="bwd_mix" if exchange is None else "bwd_mix_exchange", grid=(n_tiles,),
        out_shape=tuple(out_shape), in_specs=in_specs, out_specs=tuple(out_specs), scratch_shapes=scratch,
        compiler_params=_cparams(1),
    )(*args)


def _bwd_in(dproj, x, dout, g_pre, w_in_t, allgather=None):
    n_tok = x.shape[0]
    tm = TOK_TILE
    n_tiles = n_tok // tm

    def body(*refs):
        dp_ref, x_ref, dout_ref, g_ref, wt_ref = refs[:5]
        if allgather is None:
            dx_ref, dwt_ref, dg_ref, acc_s = refs[5:]
            at_start = at_end = lambda: None
        else:
            part_ref, dx_ref, dwt_ref, dg_ref, all_ref, acc_s, send_sems, recv_sems, local_sems = refs[5:]
            phases = _exchange_phases((part_ref,), (all_ref,), send_sems, recv_sems, local_sems, by_chip=False)
            at_start, at_end = _hosted(phases, pl.program_id(0), n_tiles)
        at_start()
        i = pl.program_id(0)

        @pl.when(i == 0)
        def _():
            acc_s[...] = jnp.zeros_like(acc_s)
            dg_ref[...] = jnp.zeros_like(dg_ref)

        xf = x_ref[...]
        r = lax.rsqrt(jnp.mean(xf * xf, axis=-1, keepdims=True) + NORM_EPS)
        xn = xf * r
        h = (xn * g_ref[...]).astype(BF16)
        dp = dp_ref[...]
        dh = _dot(dp, wt_ref[...])
        acc_s[...] += _dot_tn(dp, h)
        dg_ref[...] += jnp.sum(dh * xn, axis=0, keepdims=True)
        dhn = dh * g_ref[...]
        dx_ref[...] = dout_ref[...] + r * (dhn - xn * jnp.mean(dhn * xn, axis=-1, keepdims=True))

        @pl.when(i == n_tiles - 1)
        def _():
            dwt_ref[...] = acc_s[...].astype(BF16)

        at_end()

    in_specs = [pl.BlockSpec((tm, IN_WIDTH), lambda i: (i, 0)), pl.BlockSpec((tm, D_MODEL), lambda i: (i, 0)),
                pl.BlockSpec((tm, D_MODEL), lambda i: (i, 0)), pl.BlockSpec((1, D_MODEL), lambda i: (0, 0)),
                pl.BlockSpec((IN_WIDTH, D_MODEL), lambda i: (0, 0), pipeline_mode=pl.Buffered(1))]
    out_shape = [jax.ShapeDtypeStruct((n_tok, D_MODEL), F32), jax.ShapeDtypeStruct((IN_WIDTH, D_MODEL), BF16),
                 jax.ShapeDtypeStruct((1, D_MODEL), F32)]
    out_specs = [pl.BlockSpec((tm, D_MODEL), lambda i: (i, 0)),
                 pl.BlockSpec((IN_WIDTH, D_MODEL), lambda i: (0, 0), pipeline_mode=pl.Buffered(1)),
                 pl.BlockSpec((1, D_MODEL), lambda i: (0, 0))]
    scratch = [pltpu.VMEM((IN_WIDTH, D_MODEL), F32)]
    args = [dproj, x, dout, g_pre, w_in_t]
    if allgather is not None:
        in_specs.append(ANY)
        args.append(allgather)
        out_shape.append(jax.ShapeDtypeStruct((N_DEV,) + allgather.shape, allgather.dtype))
        out_specs.append(ANY)
        scratch += _comm_scratch(N_EXCHANGE_SEMS)
    return pl.pallas_call(
        body, name="bwd_in" if allgather is None else "bwd_in_allgather", grid=(n_tiles,),
        out_shape=tuple(out_shape), in_specs=in_specs, out_specs=tuple(out_specs), scratch_shapes=scratch,
        compiler_params=_cparams(1),
    )(*args)


def _adam_update(w, g, m, v):
    nm = ADAM_B1 * m + (1.0 - ADAM_B1) * g
    nv = ADAM_B2 * v + (1.0 - ADAM_B2) * (g * g)
    m_hat = nm / (1.0 - ADAM_B1 ** ADAM_STEP)
    v_hat = nv / (1.0 - ADAM_B2 ** ADAM_STEP)
    return -ADAM_LR * (m_hat / (jnp.sqrt(v_hat) + ADAM_EPS) + ADAM_WD * w), nm, nv


def _sum_slots(p_ref):
    tot = p_ref[0].astype(F32)
    for d in range(1, N_DEV):
        tot = tot + p_ref[d].astype(F32)
    return tot


def _adamw_parts(parts, w, m, v, layer, results, name):
    n_layers, n_rows, n_cols = w.shape
    tr = n_rows // 2 if n_rows * n_cols > 512 * 1024 else n_rows

    def body(*refs):
        p_ref, w_ref, m_ref, v_ref = refs[:4]
        g_ref, d_ref, nm_ref, nv_ref = refs[-4:]
        g = _sum_slots(p_ref)
        g_ref[0] = g
        d_ref[0], nm_ref[0], nv_ref[0] = _adam_update(w_ref[0], g, m_ref[0], v_ref[0])

    spec = pl.BlockSpec((1, tr, n_cols), lambda i: (layer, i, 0))
    shape = jax.ShapeDtypeStruct((n_layers, n_rows, n_cols), F32)
    kept = [] if results is None else list(results)
    return pl.pallas_call(
        body, name="adamw_" + name, grid=(n_rows // tr,), out_shape=(shape,) * 4,
        in_specs=[pl.BlockSpec((N_DEV, tr, n_cols), lambda i: (0, i, 0)), spec, spec, spec] + [ANY] * len(kept),
        out_specs=(spec,) * 4, input_output_aliases={4 + j: j for j in range(len(kept))},
        compiler_params=_cparams(1),
    )(parts, w, m, v, *kept)


def _adamw(w, g, m, v, name):
    n_rows, n_cols = w.shape
    tr = 512 if n_rows % 512 == 0 else n_rows

    def body(w_ref, g_ref, m_ref, v_ref, d_ref, nm_ref, nv_ref):
        d_ref[...], nm_ref[...], nv_ref[...] = _adam_update(w_ref[...], g_ref[...], m_ref[...], v_ref[...])

    spec = pl.BlockSpec((tr, n_cols), lambda i: (i, 0))
    shape = jax.ShapeDtypeStruct((n_rows, n_cols), F32)
    return pl.pallas_call(
        body, name="adamw_" + name, grid=(n_rows // tr,), out_shape=(shape, shape, shape),
        in_specs=[spec] * 4, out_specs=(spec, spec, spec), compiler_params=_cparams(1),
    )(w, g, m, v)


_SMALL = ("pre_norm_g", "ln_v_g", "ln_v_b", "b_spatial", "sinks", "rel_bias", "post_norm_g")


def _pack(pieces):
    blocks, first_rows, n = [], [], 0
    for p in pieces:
        flat = p.reshape(-1)
        flat = jnp.concatenate([flat, jnp.zeros(((-flat.shape[0]) % (8 * 128),), F32)]).reshape(-1, 128)
        blocks.append(flat)
        first_rows.append(n)
        n += flat.shape[0]
    return jnp.concatenate(blocks, axis=0), first_rows


def _unpack(block, first_row, shape):
    size = math.prod(shape)
    return block[first_row:first_row + -(-size // 128)].reshape(-1)[:size].reshape(shape)


def kernel(x, pre_norm_g, w_in, ln_v_g, ln_v_b, w_spatial, b_spatial, sinks, rel_bias, w_out, post_norm_g, loss_target, m_pre_norm_g, m_w_in, m_ln_v_g, m_ln_v_b, m_w_spatial, m_b_spatial, m_sinks, m_rel_bias, m_w_out, m_post_norm_g, v_pre_norm_g, v_w_in, v_ln_v_g, v_ln_v_b, v_w_spatial, v_b_spatial, v_sinks, v_rel_bias, v_w_out, v_post_norm_g):
    weights = dict(pre_norm_g=pre_norm_g, w_in=w_in, ln_v_g=ln_v_g, ln_v_b=ln_v_b, w_spatial=w_spatial, b_spatial=b_spatial,
                   sinks=sinks, rel_bias=rel_bias, w_out=w_out, post_norm_g=post_norm_g)
    mom_m = dict(pre_norm_g=m_pre_norm_g, w_in=m_w_in, ln_v_g=m_ln_v_g, ln_v_b=m_ln_v_b, w_spatial=m_w_spatial,
                 b_spatial=m_b_spatial, sinks=m_sinks, rel_bias=m_rel_bias, w_out=m_w_out, post_norm_g=m_post_norm_g)
    mom_v = dict(pre_norm_g=v_pre_norm_g, w_in=v_w_in, ln_v_g=v_ln_v_g, ln_v_b=v_ln_v_b, w_spatial=v_w_spatial,
                 b_spatial=v_b_spatial, sinks=v_sinks, rel_bias=v_rel_bias, w_out=v_w_out, post_norm_g=v_post_norm_g)
    n_seq, seq_len, _ = x.shape
    n_layers = w_in.shape[0]
    x2 = x.reshape(n_seq * seq_len, D_MODEL)
    target2 = loss_target.reshape(n_seq * seq_len, D_MODEL)
    row = lambda p, l: p[l][None]

    wt_shards = jnp.swapaxes(w_in, 1, 2).astype(BF16)
    wo_shards = w_out.astype(BF16)
    bucket = _window_buckets()
    bias = _bias_table(rel_bias, bucket)
    wcat, wtcat, bs = _spatial_tables(w_spatial, jnp.swapaxes(b_spatial, 1, 2))

    wt, wo = [None] * n_layers, [None] * n_layers
    wt[0], wo[0] = _gather_weights(wt_shards, wo_shards, 0)
    xs, saved = [x2], []
    for l in range(n_layers):
        a, q, kv, bz = _fwd_in(xs[-1], row(pre_norm_g, l), wt[l])
        mix_args = (xs[-1], a, q, kv, bz, row(ln_v_g, l), row(ln_v_b, l), wcat[l], bs[l], sinks[l], bias, wo[l],
                    row(post_norm_g, l), seq_len)
        if l + 1 < n_layers:
            xn, y, wt[l + 1], wo[l + 1] = _fwd_mix(*mix_args, gather=(wt_shards, wo_shards, l + 1))
        else:
            xn, y = _fwd_mix(*mix_args)
        saved.append((a, q, kv, bz, y))
        xs.append(xn)
    dx, loss = _loss_head(xs[-1], target2)

    small = [None] * n_layers
    parts = [None] * n_layers
    waiting = None
    for l in reversed(range(n_layers)):
        a, q, kv, bz, y = saved[l]
        outs = _bwd_mix(dx, y, a, q, kv, bz, row(ln_v_g, l), row(ln_v_b, l), wcat[l], wtcat[l], bs[l], sinks[l], bias,
                        wo[l], row(post_norm_g, l), seq_len, exchange=None if waiting is None else waiting[:2])
        dproj, dwo, dws, dmix, dlng, dlnb, dgpost, dsink, dbias = outs[:9]
        exchanged = outs[9:]
        dws = dws.reshape(HEADS * CHUNK, CHUNK)
        if l > 0:
            dx, dwt, dgpre, ws_all = _bwd_in(dproj, xs[l], dx, row(pre_norm_g, l), wt[l], allgather=dws)
        else:
            dx, dwt, dgpre = _bwd_in(dproj, xs[l], dx, row(pre_norm_g, l), wt[l])
            ws_all = None
        waiting = (dwt, dwo, dws)
        parts[l] = [None, None, ws_all]
        small[l] = dict(pre_norm_g=dgpre[0], ln_v_g=dlng[0], ln_v_b=dlnb[0], dmix=dmix, sinks=dsink, dbias=dbias,
                        post_norm_g=dgpost[0])
        if l + 1 < n_layers:
            parts[l + 1][0:2] = exchanged
    parts[0][0:3] = _exchange_grads(*waiting)

    db = _b_spatial_grad(jnp.stack([s["dmix"] for s in small]))
    pieces = {n: jnp.stack([s[n] for s in small]) for n in ("pre_norm_g", "ln_v_g", "ln_v_b", "sinks", "post_norm_g")}
    pieces["b_spatial"] = jnp.swapaxes(db[:, :, :HEADS], 1, 2)
    pieces["rel_bias"] = _rel_bias_grad(jnp.stack([s["dbias"] for s in small]), bucket)
    packed, first_rows = _pack([pieces[n] for n in _SMALL] + [loss])
    total = _allreduce_small(packed)
    grad = {n: _unpack(total, r, weights[n].shape) for n, r in zip(_SMALL, first_rows)}
    loss_out = total[first_rows[-1], 0]

    delta, new_m, new_v = {}, {}, {}
    for n in _SMALL:
        w = weights[n]
        two_d = (-1, w.shape[-1])
        d, nm, nv = _adamw(w.reshape(two_d), grad[n].reshape(two_d), mom_m[n].reshape(two_d), mom_v[n].reshape(two_d), n)
        delta[n], new_m[n], new_v[n] = d.reshape(w.shape), nm.reshape(w.shape), nv.reshape(w.shape)

    t3 = lambda p: jnp.swapaxes(p, 1, 2)
    flat3 = lambda p: p.reshape(n_layers, HEADS * CHUNK, CHUNK)
    for n, which, view, back in (("w_in", 0, t3, t3), ("w_out", 1, lambda p: p, lambda p: p),
                                 ("w_spatial", 2, flat3, lambda p: p.reshape(w_spatial.shape))):
        results = None
        for l in reversed(range(n_layers)):
            results = _adamw_parts(parts[l][which], view(weights[n]), view(mom_m[n]), view(mom_v[n]), l, results, n)
        grad[n], delta[n], new_m[n], new_v[n] = (back(r) for r in results)

    names = tuple(weights)
    return (loss_out, dx.reshape(x.shape), *[grad[n] for n in names], *[delta[n] for n in names],
            *[new_m[n] for n in names], *[new_v[n] for n in names])
```

```python
import math

import jax
import jax.numpy as jnp
from jax import lax
from jax.experimental import pallas as pl
from jax.experimental.pallas import tpu as pltpu

F32 = jnp.float32
BF16 = jnp.bfloat16

D_MODEL = 1024
A_WIDTH = 512
B_WIDTH = 512
KV_WIDTH = 128
IN_WIDTH = 3 * A_WIDTH + 2 * B_WIDTH + 2 * KV_WIDTH
CHUNK = 128
HEADS = 8
HEAD_DIM = 64
REL_BUCKETS = 32
NORM_EPS = 1e-6
NEG = -1e30
Q_OFF = 3 * A_WIDTH
K_OFF = Q_OFF + B_WIDTH
BZ_OFF = K_OFF + 2 * KV_WIDTH
QK_SCALE = HEAD_DIM ** -0.5

ADAM_LR = 0.001
ADAM_B1 = 0.9
ADAM_B2 = 0.999
ADAM_EPS = 1e-08
ADAM_WD = 0.01
ADAM_STEP = 10

TOK_TILE = 512
CHUNK_UNROLL = 4
VMEM_LIMIT_V7X = 60 * 1024 * 1024

N_DEV = 8
N_CHIPS = 4
MESH_ID = pl.DeviceIdType.MESH
ANY = pl.BlockSpec(memory_space=pl.ANY)
SMEM = pl.BlockSpec(memory_space=pltpu.SMEM)


def _cparams(n_axes):
    return pltpu.CompilerParams(dimension_semantics=("arbitrary",) * n_axes, vmem_limit_bytes=VMEM_LIMIT_V7X)


_GELU_C = math.sqrt(2.0 / math.pi)


def _gelu(x):
    inner = _GELU_C * (x + 0.044715 * (x * x * x))
    return x * (0.5 * (1.0 + jnp.tanh(inner)))


def _gelu_and_grad(x):
    x2 = x * x
    t = jnp.tanh(_GELU_C * (x + 0.044715 * (x2 * x)))
    cdf = 0.5 * (1.0 + t)
    d = cdf + x * (0.5 * (1.0 - t * t)) * (_GELU_C * (1.0 + 3.0 * 0.044715 * x2))
    return x * cdf, d


def _sigmoid(x):
    return 0.5 + 0.5 * jnp.tanh(0.5 * x)


def _silu(x):
    return x * _sigmoid(x)


def _silu_and_grad(x):
    s = _sigmoid(x)
    return x * s, s * (1.0 + x * (1.0 - s))


def _dot(a, b):
    return jnp.dot(a, b, preferred_element_type=F32)


def _dot_nt(a, b):
    return lax.dot_general(a, b, (((1,), (1,)), ((), ())), preferred_element_type=F32)


def _dot_tn(a, b):
    return lax.dot_general(a, b, (((0,), (0,)), ((), ())), preferred_element_type=F32)


def _lo_mask(shape):
    return lax.broadcasted_iota(jnp.int32, shape, 1) < HEAD_DIM


def _swap_halves(v):
    return pltpu.roll(v, HEAD_DIM, 1)


def _from_prev(shape=(CHUNK, CHUNK)):
    return lax.broadcasted_iota(jnp.int32, shape, 1) > lax.broadcasted_iota(jnp.int32, shape, 0)


def _window_buckets():
    q_loc = jnp.arange(CHUNK)[:, None]
    j_loc = jnp.arange(CHUNK)[None, :]
    d = q_loc - j_loc + jnp.where(j_loc > q_loc, CHUNK, 0)
    max_exact = REL_BUCKETS // 2
    safe = jnp.maximum(d, 1).astype(F32)
    large = max_exact + (jnp.log(safe / max_exact) / math.log(CHUNK / max_exact)
                         * (REL_BUCKETS - max_exact)).astype(jnp.int32)
    large = jnp.minimum(large, REL_BUCKETS - 1)
    return jnp.where(d < max_exact, d, large).astype(jnp.int32)


def _bias_table(rel_bias, bucket):
    def body(rb_ref, bk_ref, out_ref):
        bk = bk_ref[...]
        for h in range(HEADS):
            acc = jnp.zeros(bk.shape, F32)
            for b in range(REL_BUCKETS):
                acc = jnp.where(bk == b, rb_ref[b, h], acc)
            out_ref[h] = acc

    vmem = pl.BlockSpec(memory_space=pltpu.VMEM)
    return pl.pallas_call(
        body, name="bias_table",
        out_shape=jax.ShapeDtypeStruct((HEADS, CHUNK, CHUNK), F32),
        in_specs=[SMEM, vmem], out_specs=vmem,
    )(rel_bias, bucket)


def _rel_bias_grad(dbias, bucket):
    n_layers = dbias.shape[0]

    def body(db_ref, bk_ref, out_ref):
        bk = bk_ref[...]
        for h in range(HEADS):
            tot = db_ref[0, h]
            for l in range(1, n_layers):
                tot = tot + db_ref[l, h]
            for b in range(REL_BUCKETS):
                out_ref[b, h] = jnp.sum(jnp.where(bk == b, tot, 0.0))

    vmem = pl.BlockSpec(memory_space=pltpu.VMEM)
    return pl.pallas_call(
        body, name="rel_bias_grad",
        out_shape=jax.ShapeDtypeStruct((REL_BUCKETS, HEADS), F32),
        in_specs=[vmem] * 2, out_specs=SMEM,
    )(dbias, bucket)


def _spatial_tables(w_spatial, b_spatial_t):
    n_layers = w_spatial.shape[0]

    def body(w_ref, b_ref, wcat_ref, wtcat_ref, bs_ref):
        row = lax.broadcasted_iota(jnp.int32, (CHUNK, CHUNK), 0)
        col = lax.broadcasted_iota(jnp.int32, (CHUNK, CHUNK), 1)
        causal = col <= row
        lo = _lo_mask((CHUNK, CHUNK))
        for p in range(4):
            for half in range(2):
                w = jnp.where(causal, w_ref[0, 2 * p + half], 0.0)
                wcat_ref[0, p, :, half * CHUNK:(half + 1) * CHUNK] = w.astype(BF16)
                wtcat_ref[0, p, :, half * CHUNK:(half + 1) * CHUNK] = w.T.astype(BF16)
            b = b_ref[0]
            bs_ref[0, p] = jnp.where(lo, b[:, 2 * p:2 * p + 1], b[:, 2 * p + 1:2 * p + 2])

    return pl.pallas_call(
        body, name="spatial_tables", grid=(n_layers,),
        out_shape=(jax.ShapeDtypeStruct((n_layers, 4, CHUNK, 2 * CHUNK), BF16),
                   jax.ShapeDtypeStruct((n_layers, 4, CHUNK, 2 * CHUNK), BF16),
                   jax.ShapeDtypeStruct((n_layers, 4, CHUNK, CHUNK), F32)),
        in_specs=[pl.BlockSpec((1, HEADS, CHUNK, CHUNK), lambda l: (l, 0, 0, 0)),
                  pl.BlockSpec((1, CHUNK, HEADS), lambda l: (l, 0, 0))],
        out_specs=(pl.BlockSpec((1, 4, CHUNK, 2 * CHUNK), lambda l: (l, 0, 0, 0)),
                   pl.BlockSpec((1, 4, CHUNK, 2 * CHUNK), lambda l: (l, 0, 0, 0)),
                   pl.BlockSpec((1, 4, CHUNK, CHUNK), lambda l: (l, 0, 0, 0))),
        compiler_params=_cparams(1),
    )(w_spatial, b_spatial_t)


def _b_spatial_grad(dmix):
    n_layers = dmix.shape[0]

    def body(d_ref, out_ref):
        lane = lax.broadcasted_iota(jnp.int32, (CHUNK, CHUNK), 1)
        acc = jnp.zeros((CHUNK, CHUNK), F32)
        for p in range(4):
            t = d_ref[0, :, p * CHUNK:(p + 1) * CHUNK]
            s_lo = jnp.sum(jnp.where(lane < HEAD_DIM, t, 0.0), axis=1, keepdims=True)
            s_hi = jnp.sum(jnp.where(lane < HEAD_DIM, 0.0, t), axis=1, keepdims=True)
            acc = jnp.where(lane == 2 * p, s_lo, acc)
            acc = jnp.where(lane == 2 * p + 1, s_hi, acc)
        out_ref[0] = acc

    return pl.pallas_call(
        body, name="b_spatial_grad", grid=(n_layers,),
        out_shape=jax.ShapeDtypeStruct((n_layers, CHUNK, CHUNK), F32),
        in_specs=[pl.BlockSpec((1, CHUNK, A_WIDTH), lambda l: (l, 0, 0))],
        out_specs=pl.BlockSpec((1, CHUNK, CHUNK), lambda l: (l, 0, 0)),
        compiler_params=_cparams(1),
    )(dmix)


def _place():
    x, y, c = lax.axis_index("x"), lax.axis_index("y"), lax.axis_index("c")
    other_chips = [(1 - x, y), (x, 1 - y), (1 - x, 1 - y)]
    return x, y, c, other_chips


N_GATHER_SEMS = 12


def _gather_phases(shards, fulls, send_sems, recv_sems, local_sems):
    x, y, c, chips = _place()
    sibling = (x, y, 1 - c)
    n_arr = len(shards)

    def half_rows(a, chip, half):
        n = shards[a].shape[0]
        start = (2 * chip[0] + chip[1]) * n + half * (n // 2)
        return fulls[a].at[pl.ds(pl.multiple_of(start, 16), n // 2), :]

    def my_half(a):
        n = shards[a].shape[0]
        return shards[a].at[pl.ds(pl.multiple_of(c * (n // 2), 16), n // 2), :]

    def copy(k, a, src, chip, half, to):
        return pltpu.make_async_remote_copy(
            src_ref=src, dst_ref=half_rows(a, chip, half), send_sem=send_sems.at[n_arr * k + a],
            recv_sem=recv_sems.at[n_arr * k + a], device_id=to, device_id_type=MESH_ID)

    def local(a):
        n = shards[a].shape[0]
        mine = fulls[a].at[pl.ds(pl.multiple_of((2 * x + y) * n, 16), n), :]
        return pltpu.make_async_copy(shards[a], mine, local_sems.at[a])

    def first(k, a):
        return copy(k, a, my_half(a), (x, y), c, (chips[k][0], chips[k][1], c))

    def passed(k, a):
        return copy(3 + k, a, half_rows(a, chips[k], c), chips[k], c, sibling)

    def phase_a():
        for a in range(n_arr):
            local(a).start()
        for k in range(3):
            for a in range(n_arr):
                first(k, a).start()

    def phase_b():
        for k in range(3):
            for a in range(n_arr):
                copy(k, a, my_half(a), chips[k], c, sibling).wait_recv()
                passed(k, a).start()

    def phase_c():
        for k in range(3):
            for a in range(n_arr):
                copy(3 + k, a, my_half(a), chips[k], 1 - c, sibling).wait_recv()
        for k in range(3):
            for a in range(n_arr):
                first(k, a).wait_send()
                passed(k, a).wait_send()
        for a in range(n_arr):
            local(a).wait()

    return phase_a, phase_b, phase_c


N_EXCHANGE_SEMS = 14


def _exchange_phases(partials, parts, send_sems, recv_sems, local_sems, by_chip=True):
    x, y, c, chips = _place()
    me, sibling = (x, y, c), (x, y, 1 - c)
    n_arr = len(partials)

    def block(a, chip):
        if not by_chip:
            return partials[a]
        n = partials[a].shape[0] // N_CHIPS
        return partials[a].at[pl.ds(pl.multiple_of((2 * chip[0] + chip[1]) * n, 16), n), :]

    def slot(a, dev):
        return parts[a].at[4 * dev[0] + 2 * dev[1] + dev[2]]

    def copy(k, a, src, origin, to):
        return pltpu.make_async_remote_copy(
            src_ref=src, dst_ref=slot(a, origin), send_sem=send_sems.at[n_arr * k + a],
            recv_sem=recv_sems.at[n_arr * k + a], device_id=to, device_id_type=MESH_ID)

    def local(a):
        return pltpu.make_async_copy(block(a, (x, y)), slot(a, me), local_sems.at[a])

    def first(k, a):
        if k == 0:
            return copy(0, a, block(a, (x, y)), me, sibling)
        chip = chips[k - 1]
        return copy(k, a, block(a, chip), me, (chip[0], chip[1], c))

    def passed(k, a):
        origin = (chips[k][0], chips[k][1], c)
        return copy(4 + k, a, slot(a, origin), origin, sibling)

    def phase_a():
        for a in range(n_arr):
            local(a).start()
        for k in range(4):
            for a in range(n_arr):
                first(k, a).start()

    def phase_b():
        for k in range(3):
            for a in range(n_arr):
                copy(1 + k, a, block(a, (x, y)), (chips[k][0], chips[k][1], c), me).wait_recv()
                passed(k, a).start()

    def phase_c():
        for a in range(n_arr):
            copy(0, a, block(a, (x, y)), sibling, me).wait_recv()
        for k in range(3):
            for a in range(n_arr):
                copy(4 + k, a, block(a, (x, y)), (chips[k][0], chips[k][1], 1 - c), me).wait_recv()
        for k in range(4):
            for a in range(n_arr):
                first(k, a).wait_send()
        for k in range(3):
            for a in range(n_arr):
                passed(k, a).wait_send()
        for a in range(n_arr):
            local(a).wait()

    return phase_a, phase_b, phase_c


def _comm_scratch(n_sems):
    return [pltpu.SemaphoreType.DMA((n_sems,)), pltpu.SemaphoreType.DMA((n_sems,)), pltpu.SemaphoreType.DMA((2,))]


def _gather_weights(wt_shards, wo_shards, layer):
    wt_rows, wo_rows = wt_shards.shape[1], wo_shards.shape[1]

    def body(wt_ref, wo_ref, wt_full, wo_full, send_sems, recv_sems, local_sems):
        phases = _gather_phases((wt_ref.at[layer], wo_ref.at[layer]), (wt_full, wo_full), send_sems, recv_sems, local_sems)
        for phase in phases:
            phase()

    return pl.pallas_call(
        body, name="gather_weights",
        out_shape=(jax.ShapeDtypeStruct((N_CHIPS * wt_rows, D_MODEL), BF16),
                   jax.ShapeDtypeStruct((N_CHIPS * wo_rows, D_MODEL), BF16)),
        in_specs=[ANY, ANY], out_specs=(ANY, ANY), scratch_shapes=_comm_scratch(N_GATHER_SEMS),
    )(wt_shards, wo_shards)


def _exchange_grads(dwt, dwo, small):
    def body(dwt_ref, dwo_ref, small_ref, pt_ref, po_ref, all_ref, *sems):
        blocks = _exchange_phases((dwt_ref, dwo_ref), (pt_ref, po_ref), *sems[:3])
        whole = _exchange_phases((small_ref,), (all_ref,), *sems[3:], by_chip=False)
        for phase_of_blocks, phase_of_whole in zip(blocks, whole):
            phase_of_blocks()
            phase_of_whole()

    return pl.pallas_call(
        body, name="exchange_grads",
        out_shape=(jax.ShapeDtypeStruct((N_DEV, dwt.shape[0] // N_CHIPS, D_MODEL), BF16),
                   jax.ShapeDtypeStruct((N_DEV, dwo.shape[0] // N_CHIPS, D_MODEL), BF16),
                   jax.ShapeDtypeStruct((N_DEV,) + small.shape, small.dtype)),
        in_specs=[ANY, ANY, ANY], out_specs=(ANY, ANY, ANY),
        scratch_shapes=_comm_scratch(N_EXCHANGE_SEMS) + _comm_scratch(N_EXCHANGE_SEMS),
    )(dwt, dwo, small)


def _allreduce_small(part):
    n_rows = part.shape[0]

    def body(p_ref, tot_ref, all_ref, send_sems, recv_sems, local_sem):
        x, y, c, chips = _place()
        me, sibling = (x, y, c), (x, y, 1 - c)

        def rows(dev):
            return all_ref.at[pl.ds(pl.multiple_of((4 * dev[0] + 2 * dev[1] + dev[2]) * n_rows, 8), n_rows), :]

        def copy(k, origin, to, src=None):
            return pltpu.make_async_remote_copy(
                src_ref=rows(origin) if src is None else src, dst_ref=rows(origin), send_sem=send_sems.at[k],
                recv_sem=recv_sems.at[k], device_id=to, device_id_type=MESH_ID)

        mine = pltpu.make_async_copy(p_ref, rows(me), local_sem)
        mine.start()
        first = [copy(0, me, sibling, src=p_ref)]
        first += [copy(1 + k, me, (chip[0], chip[1], c), src=p_ref) for k, chip in enumerate(chips)]
        for cp in first:
            cp.start()
        passed = []
        for k, chip in enumerate(chips):
            origin = (chip[0], chip[1], c)
            copy(1 + k, origin, me).wait_recv()
            fwd = copy(4 + k, origin, sibling)
            fwd.start()
            passed.append(fwd)
        copy(0, sibling, me).wait_recv()
        for k, chip in enumerate(chips):
            copy(4 + k, (chip[0], chip[1], 1 - c), me).wait_recv()
        for cp in first + passed:
            cp.wait_send()
        mine.wait()
        tot = all_ref[0:n_rows, :]
        for d in range(1, N_DEV):
            tot = tot + all_ref[d * n_rows:(d + 1) * n_rows, :]
        tot_ref[...] = tot

    vmem = pl.BlockSpec(memory_space=pltpu.VMEM)
    return pl.pallas_call(
        body, name="allreduce_small",
        out_shape=jax.ShapeDtypeStruct((n_rows, 128), F32),
        in_specs=[vmem], out_specs=vmem,
        scratch_shapes=[pltpu.VMEM((N_DEV * n_rows, 128), F32), pltpu.SemaphoreType.DMA((7,)),
                        pltpu.SemaphoreType.DMA((7,)), pltpu.SemaphoreType.DMA],
        compiler_params=pltpu.CompilerParams(vmem_limit_bytes=VMEM_LIMIT_V7X),
    )(part)


def _hosted(phases, step, n_steps):
    phase_a, phase_b, phase_c = phases

    def at_start():
        pl.when(step == 0)(phase_a)

    def at_end():
        pl.when(step == n_steps // 2)(phase_b)
        pl.when(step == n_steps - 1)(phase_c)

    return at_start, at_end


def _fwd_in(x, g_pre, w_in_t):
    n_tok = x.shape[0]

    def body(x_ref, g_ref, w_ref, a_ref, q_ref, kv_ref, bz_ref):
        xf = x_ref[...]
        r = lax.rsqrt(jnp.mean(xf * xf, axis=-1, keepdims=True) + NORM_EPS)
        h = ((xf * r) * g_ref[...]).astype(BF16)
        a_ref[...] = _dot_nt(h, w_ref[0:Q_OFF, :])
        q_ref[...] = _dot_nt(h, w_ref[Q_OFF:K_OFF, :]).astype(BF16)
        kv_ref[...] = _dot_nt(h, w_ref[K_OFF:BZ_OFF, :]).astype(BF16)
        bz_ref[...] = _dot_nt(h, w_ref[BZ_OFF:IN_WIDTH, :])

    tm = TOK_TILE
    return pl.pallas_call(
        body, name="fwd_in", grid=(n_tok // tm,),
        out_shape=(jax.ShapeDtypeStruct((n_tok, Q_OFF), F32), jax.ShapeDtypeStruct((n_tok, B_WIDTH), BF16),
                   jax.ShapeDtypeStruct((n_tok, 2 * KV_WIDTH), BF16), jax.ShapeDtypeStruct((n_tok, B_WIDTH), F32)),
        in_specs=[pl.BlockSpec((tm, D_MODEL), lambda i: (i, 0)), pl.BlockSpec((1, D_MODEL), lambda i: (0, 0)),
                  pl.BlockSpec((IN_WIDTH, D_MODEL), lambda i: (0, 0))],
        out_specs=(pl.BlockSpec((tm, Q_OFF), lambda i: (i, 0)), pl.BlockSpec((tm, B_WIDTH), lambda i: (i, 0)),
                   pl.BlockSpec((tm, 2 * KV_WIDTH), lambda i: (i, 0)), pl.BlockSpec((tm, B_WIDTH), lambda i: (i, 0))),
        compiler_params=_cparams(1),
    )(x, g_pre, w_in_t)


def _layer_norm_stats(vv):
    mu = jnp.mean(vv, axis=-1, keepdims=True)
    xc = vv - mu
    rs = lax.rsqrt(jnp.mean(xc * xc, axis=-1, keepdims=True) + NORM_EPS)
    return xc * rs, rs


def _blockdiag(v, lo):
    zero = jnp.zeros_like(v)
    return jnp.concatenate([jnp.where(lo, v, zero), jnp.where(lo, zero, v)], axis=0)


def _softmax_sink(s, sink):
    m = jnp.maximum(jnp.max(s, axis=-1, keepdims=True), sink)
    e = jnp.exp(s - m)
    esink = jnp.exp(sink - m)
    den = jnp.sum(e, axis=-1, keepdims=True) + esink
    return e / den, esink / den


def _kv_rows(cur_ref, halo_ref, r0, c):
    prev_in_tile = cur_ref[pl.ds(pl.multiple_of(jnp.maximum(r0 - CHUNK, 0), CHUNK), CHUNK), :]
    prev = jnp.where(c == 0, halo_ref[...], prev_in_tile)
    kv2 = jnp.concatenate([prev, cur_ref[pl.ds(r0, CHUNK), :]], axis=0)
    k2, v2 = kv2[:, 0:KV_WIDTH], kv2[:, KV_WIDTH:2 * KV_WIDTH]
    return (k2, _swap_halves(k2)), (v2, _swap_halves(v2))


def _window_square(over_keys, prev):
    return jnp.where(prev, over_keys[:, 0:CHUNK], over_keys[:, CHUNK:2 * CHUNK])


def _window_keys(square, prev):
    zero = jnp.zeros_like(square)
    return jnp.concatenate([jnp.where(prev, square, zero), jnp.where(prev, zero, square)], axis=1)


def _dead_mask(first):
    shape = (CHUNK, CHUNK)
    ahead = lax.broadcasted_iota(jnp.int32, shape, 1) - lax.broadcasted_iota(jnp.int32, shape, 0)
    return ahead > jnp.where(first, 0, CHUNK)


def _head_of(p, half):
    return 2 * p + half, int(half != p // 2)


def _fwd_mix(x, a, q, kv, bz, ln_g, ln_b, wcat, bs, sinks, bias, w_out, g_post, seq_len, gather=None, target=None):
    n_tok = x.shape[0]
    tb = TOK_TILE
    n_chunks = tb // CHUNK
    n_tiles = n_tok // tb
    n_in = 14
    assert gather is None or target is None

    def body(*refs):
        (x_ref, a_ref, q_ref, kv_ref, halo_ref, bz_ref, lng_ref, lnb_ref, wcat_ref, bs_ref, sink_ref, bias_ref,
         wout_ref, gpost_ref) = refs[:n_in]
        at_start = at_end = lambda: None
        if gather is not None:
            wts_ref, wos_ref, xn_ref, y_ref, wt_full, wo_full, ycat_s, send_sems, recv_sems, local_sems = refs[n_in:]
            phases = _gather_phases((wts_ref.at[gather[2]], wos_ref.at[gather[2]]), (wt_full, wo_full), send_sems,
                                    recv_sems, local_sems)
            at_start, at_end = _hosted(phases, pl.program_id(0), n_tiles)
        elif target is not None:
            target_ref, xn_ref, y_ref, loss_ref, ycat_s = refs[n_in:]
        else:
            xn_ref, y_ref, ycat_s = refs[n_in:]
        at_start()
        i = pl.program_id(0)
        lo = _lo_mask((CHUNK, CHUNK))
        prev = _from_prev()

        def chunk(c, carry):
            r0 = pl.multiple_of(c * CHUNK, CHUNK)
            rows = pl.ds(r0, CHUNK)
            u = _gelu(a_ref[rows, 0:A_WIDTH])
            vv = _gelu(a_ref[rows, A_WIDTH:2 * A_WIDTH])
            xhat, _ = _layer_norm_stats(vv)
            vnb = (xhat * lng_ref[...] + lnb_ref[...]).astype(BF16)
            for p in range(4):
                blk = slice(p * CHUNK, (p + 1) * CHUNK)
                mixed = _dot(wcat_ref[p], _blockdiag(vnb[:, blk], lo)) + bs_ref[p]
                sz = _silu(a_ref[rows, 2 * A_WIDTH + p * CHUNK:2 * A_WIDTH + (p + 1) * CHUNK])
                ycat_s[rows, blk] = ((u[:, blk] * mixed) * sz).astype(BF16)
            ks, vs = _kv_rows(kv_ref, halo_ref, r0, c)
            dead = _dead_mask(lax.rem(i * tb + r0, seq_len) == 0)
            for p in range(4):
                blk = slice(p * CHUNK, (p + 1) * CHUNK)
                qt = q_ref[rows, blk]
                zero = jnp.zeros_like(qt)
                res = []
                for half in range(2):
                    head, sw = _head_of(p, half)
                    qm = jnp.where(lo, qt, zero) if half == 0 else jnp.where(lo, zero, qt)
                    s = _window_square(_dot_nt(qm, ks[sw]), prev) * QK_SCALE + bias_ref[head]
                    s = jnp.where(dead, NEG, s)
                    prob, _ = _softmax_sink(s, sink_ref[head])
                    res.append(_dot(_window_keys(prob.astype(BF16), prev), vs[sw]))
                o = jnp.where(lo, res[0], res[1])
                ycat_s[rows, B_WIDTH + p * CHUNK:B_WIDTH + (p + 1) * CHUNK] = (o * _silu(bz_ref[rows, blk])).astype(BF16)
            return carry

        lax.fori_loop(0, n_chunks, chunk, 0, unroll=CHUNK_UNROLL)
        y = _dot(ycat_s[...], wout_ref[...])
        r = lax.rsqrt(jnp.mean(y * y, axis=-1, keepdims=True) + NORM_EPS)
        y_ref[...] = y
        xn = x_ref[...] + (y * r) * gpost_ref[...]
        if target is None:
            xn_ref[...] = xn
        else:
            d = xn - target_ref[...]
            xn_ref[...] = d * (1.0 / D_MODEL)

            @pl.when(i == 0)
            def _():
                loss_ref[0, 0] = 0.0

            loss_ref[0, 0] += 0.5 * jnp.sum(jnp.mean(d * d, axis=-1, keepdims=True))
        at_end()

    tile = lambda w: pl.BlockSpec((tb, w), lambda i: (i, 0))
    whole = lambda shape: pl.BlockSpec(shape, lambda i: (0,) * len(shape))
    in_specs = [tile(D_MODEL), tile(Q_OFF), tile(B_WIDTH), tile(2 * KV_WIDTH),
                pl.BlockSpec((CHUNK, 2 * KV_WIDTH), lambda i: (jnp.maximum(i * n_chunks - 1, 0), 0)),
                tile(B_WIDTH), whole((1, A_WIDTH)), whole((1, A_WIDTH)), whole((4, CHUNK, 2 * CHUNK)),
                whole((4, CHUNK, CHUNK)), SMEM, whole((HEADS, CHUNK, CHUNK)), whole((D_MODEL, D_MODEL)),
                whole((1, D_MODEL))]
    out_shape = [jax.ShapeDtypeStruct((n_tok, D_MODEL), F32), jax.ShapeDtypeStruct((n_tok, D_MODEL), F32)]
    out_specs = [tile(D_MODEL), tile(D_MODEL)]
    scratch = [pltpu.VMEM((tb, D_MODEL), BF16)]
    args = [x, a, q, kv, kv, bz, ln_g, ln_b, wcat, bs, sinks, bias, w_out, g_post]
    if gather is not None:
        wt_shards, wo_shards, _ = gather
        in_specs += [ANY, ANY]
        args += [wt_shards, wo_shards]
        out_shape += [jax.ShapeDtypeStruct((N_CHIPS * wt_shards.shape[1], D_MODEL), BF16),
                      jax.ShapeDtypeStruct((N_CHIPS * wo_shards.shape[1], D_MODEL), BF16)]
        out_specs += [ANY, ANY]
        scratch += _comm_scratch(N_GATHER_SEMS)
    if target is not None:
        in_specs.append(tile(D_MODEL))
        args.append(target)
        out_shape.append(jax.ShapeDtypeStruct((1, 1), F32))
        out_specs.append(SMEM)
    name = "fwd_mix" + ("" if gather is None else "_gather") + ("" if target is None else "_loss")
    return pl.pallas_call(
        body, name=name, grid=(n_tiles,),
        out_shape=tuple(out_shape), in_specs=in_specs, out_specs=tuple(out_specs), scratch_shapes=scratch,
        compiler_params=_cparams(1),
    )(*args)


def _bwd_mix(dout, y, a, q, kv, bz, ln_g, ln_b, wcat, wtcat, bs, sinks, bias, w_out, g_post, seq_len, exchange=None):
    n_tok = y.shape[0]
    tb = TOK_TILE
    n_chunks = tb // CHUNK
    n_tiles = n_tok // tb
    n_in, n_out = 16, 9

    def body(*refs):
        (dout_ref, y_ref, a_ref, q_ref, kv_ref, halo_ref, bz_ref, lng_ref, lnb_ref, wcat_ref, wtcat_ref, bs_ref,
         sink_ref, bias_ref, wout_ref, gpost_ref) = refs[:n_in]
        if exchange is None:
            outs, rest = refs[n_in:n_in + n_out], refs[n_in + n_out:]
            at_start = at_end = lambda: None
        else:
            dwt_ref, dwo_ref = refs[n_in:n_in + 2]
            outs = refs[n_in + 2:n_in + 2 + n_out]
            pt_ref, po_ref = refs[n_in + 2 + n_out:n_in + 4 + n_out]
            rest = refs[n_in + 4 + n_out:]
            phases = _exchange_phases((dwt_ref, dwo_ref), (pt_ref, po_ref), *rest[6:])
            at_start, at_end = _hosted(phases, pl.program_id(0), n_tiles)
        dproj_ref, dwout_ref, dws_ref, dmix_ref, dlng_ref, dlnb_ref, dgpost_ref, dsink_ref, dbias_ref = outs
        ycat_s, dy_s, dyc_s, dkv_s, carry_s, dwout_s = rest[:6]
        at_start()
        step = pl.program_id(0)
        tile_idx = n_tiles - 1 - step
        lo = _lo_mask((CHUNK, CHUNK))
        prev = _from_prev()

        @pl.when(step == 0)
        def _():
            dwout_s[...] = jnp.zeros_like(dwout_s)
            dws_ref[...] = jnp.zeros_like(dws_ref)
            dmix_ref[...] = jnp.zeros_like(dmix_ref)
            dlng_ref[...] = jnp.zeros_like(dlng_ref)
            dlnb_ref[...] = jnp.zeros_like(dlnb_ref)
            dgpost_ref[...] = jnp.zeros_like(dgpost_ref)
            dbias_ref[...] = jnp.zeros_like(dbias_ref)
            carry_s[...] = jnp.zeros_like(carry_s)
            for h in range(HEADS):
                dsink_ref[h] = 0.0

        yv = y_ref[...]
        dout = dout_ref[...]
        r = lax.rsqrt(jnp.mean(yv * yv, axis=-1, keepdims=True) + NORM_EPS)
        yn = yv * r
        dgpost_ref[...] += jnp.sum(dout * yn, axis=0, keepdims=True)
        dyn = dout * gpost_ref[...]
        dy = r * (dyn - yn * jnp.mean(dyn * yn, axis=-1, keepdims=True))
        dy_s[...] = dy.astype(BF16)
        dyc_s[...] = _dot_nt(dy_s[...], wout_ref[...])
        dkv_s[0:tb, :] = jnp.zeros((tb, 2 * KV_WIDTH), F32)
        dkv_s[tb:tb + CHUNK, :] = carry_s[...]

        def chunk(c, carry):
            r0 = pl.multiple_of(c * CHUNK, CHUNK)
            rows = pl.ds(r0, CHUNK)
            u, gu = _gelu_and_grad(a_ref[rows, 0:A_WIDTH])
            vv, gv = _gelu_and_grad(a_ref[rows, A_WIDTH:2 * A_WIDTH])
            xhat, rs = _layer_norm_stats(vv)
            vnb = (xhat * lng_ref[...] + lnb_ref[...]).astype(BF16)
            d_vn, d_u, d_az = [], [], []
            for p in range(4):
                blk = slice(p * CHUNK, (p + 1) * CHUNK)
                vbd = _blockdiag(vnb[:, blk], lo)
                mixed = _dot(wcat_ref[p], vbd) + bs_ref[p]
                sz, gz = _silu_and_grad(a_ref[rows, 2 * A_WIDTH + p * CHUNK:2 * A_WIDTH + (p + 1) * CHUNK])
                ub = u[:, blk]
                dya = dyc_s[rows, blk]
                um = ub * mixed
                ycat_s[rows, blk] = (um * sz).astype(BF16)
                d_mixed = (dya * ub) * sz
                d_u.append((dya * mixed) * sz)
                d_az.append((dya * um) * gz)
                dmix_ref[:, blk] += d_mixed
                dmbd = _blockdiag(d_mixed.astype(BF16), lo)
                d_vn.append(_dot(wtcat_ref[p], dmbd))
                dws_ref[p] += _dot_nt(dmbd, vnb[:, blk])
            d_vn = jnp.concatenate(d_vn, axis=1)
            dlng_ref[...] += jnp.sum(d_vn * xhat, axis=0, keepdims=True)
            dlnb_ref[...] += jnp.sum(d_vn, axis=0, keepdims=True)
            dxh = d_vn * lng_ref[...]
            d_vv = rs * (dxh - jnp.mean(dxh, axis=-1, keepdims=True)
                         - xhat * jnp.mean(dxh * xhat, axis=-1, keepdims=True))
            dproj_ref[rows, 0:A_WIDTH] = (jnp.concatenate(d_u, axis=1) * gu).astype(BF16)
            dproj_ref[rows, A_WIDTH:2 * A_WIDTH] = (d_vv * gv).astype(BF16)
            dproj_ref[rows, 2 * A_WIDTH:Q_OFF] = jnp.concatenate(d_az, axis=1).astype(BF16)
            ks, vs = _kv_rows(kv_ref, halo_ref, r0, c)
            dead = _dead_mask(lax.rem(tile_idx * tb + r0, seq_len) == 0)
            dk_acc =[jnp.zeros((2 * CHUNK, KV_WIDTH), F32), jnp.zeros((2 * CHUNK, KV_WIDTH), F32)]
            dv_acc = [jnp.zeros((2 * CHUNK, KV_WIDTH), F32), jnp.zeros((2 * CHUNK, KV_WIDTH), F32)]
            pending = {0: [], 1: []}
            for p in range(4):
                blk = slice(p * CHUNK, (p + 1) * CHUNK)
                qt = q_ref[rows, blk]
                zero = jnp.zeros_like(qt)
                sb, gb = _silu_and_grad(bz_ref[rows, blk])
                dyb = dyc_s[rows, B_WIDTH + p * CHUNK:B_WIDTH + (p + 1) * CHUNK]
                qms, probs, pws, psinks, outs_p = [], [], [], [], []
                for half in range(2):
                    head, sw = _head_of(p, half)
                    qm = jnp.where(lo, qt, zero) if half == 0 else jnp.where(lo, zero, qt)
                    s = _window_square(_dot_nt(qm, ks[sw]), prev) * QK_SCALE + bias_ref[head]
                    s = jnp.where(dead, NEG, s)
                    prob, psink = _softmax_sink(s, sink_ref[head])
                    pw = _window_keys(prob.astype(BF16), prev)
                    qms.append(qm)
                    probs.append(prob)
                    pws.append(pw)
                    psinks.append(psink)
                    outs_p.append(_dot(pw, vs[sw]))
                o = jnp.where(lo, outs_p[0], outs_p[1])
                ycat_s[rows, B_WIDTH + p * CHUNK:B_WIDTH + (p + 1) * CHUNK] = (o * sb).astype(BF16)
                dproj_ref[rows, BZ_OFF + p * CHUNK:BZ_OFF + (p + 1) * CHUNK] = ((dyb * o) * gb).astype(BF16)
                d_ob = (dyb * sb).astype(BF16)
                zero_o = jnp.zeros_like(d_ob)
                dqs = []
                for half in range(2):
                    head, sw = _head_of(p, half)
                    dom = jnp.where(lo, d_ob, zero_o) if half == 0 else jnp.where(lo, zero_o, d_ob)
                    dp = _window_square(_dot_nt(dom, vs[sw]), prev)
                    delta = jnp.sum(probs[half] * dp, axis=-1, keepdims=True)
                    ds = probs[half] * (dp - delta)
                    dsink_ref[head] += -jnp.sum(psinks[half] * delta)
                    dbias_ref[head] += ds
                    dsw = _window_keys(ds.astype(BF16), prev)
                    dqs.append(_dot(dsw, ks[sw]))
                    pending[sw].append((dsw, qms[half], pws[half], dom))
                    if len(pending[sw]) == 2:
                        (d1, q1, p1, o1), (d2, q2, p2, o2) = pending[sw]
                        dk_acc[sw] = dk_acc[sw] + _dot_tn(jnp.concatenate([d1, d2], axis=0),
                                                          jnp.concatenate([q1, q2], axis=0))
                        dv_acc[sw] = dv_acc[sw] + _dot_tn(jnp.concatenate([p1, p2], axis=0),
                                                          jnp.concatenate([o1, o2], axis=0))
                        pending[sw] = []
                dq = jnp.where(lo, dqs[0], dqs[1]) * QK_SCALE
                dproj_ref[rows, Q_OFF + p * CHUNK:Q_OFF + (p + 1) * CHUNK] = dq.astype(BF16)
            both = pl.ds(r0, 2 * CHUNK)
            dkv_s[both, 0:KV_WIDTH] += (dk_acc[0] + _swap_halves(dk_acc[1])) * QK_SCALE
            dkv_s[both, KV_WIDTH:2 * KV_WIDTH] += dv_acc[0] + _swap_halves(dv_acc[1])
            return carry

        lax.fori_loop(0, n_chunks, chunk, 0, unroll=CHUNK_UNROLL)
        dwout_s[...] += _dot_tn(ycat_s[...], dy_s[...])
        dproj_ref[:, K_OFF:BZ_OFF] = dkv_s[CHUNK:CHUNK + tb, :].astype(BF16)
        carry_s[...] = dkv_s[0:CHUNK, :]

        @pl.when(step == n_tiles - 1)
        def _():
            row = lax.broadcasted_iota(jnp.int32, (2 * CHUNK, CHUNK), 0)
            col = lax.broadcasted_iota(jnp.int32, (2 * CHUNK, CHUNK), 1)
            causal = col <= jnp.where(row >= CHUNK, row - CHUNK, row)
            for p in range(4):
                dws_ref[p] = jnp.where(causal, dws_ref[p], 0.0)
            dwout_ref[...] = dwout_s[...].astype(BF16)

        at_end()

    tile = lambda w: pl.BlockSpec((tb, w), lambda s: (n_tiles - 1 - s, 0))
    whole = lambda shape: pl.BlockSpec(shape, lambda s: (0,) * len(shape))
    in_specs = [tile(D_MODEL), tile(D_MODEL), tile(Q_OFF), tile(B_WIDTH), tile(2 * KV_WIDTH),
                pl.BlockSpec((CHUNK, 2 * KV_WIDTH), lambda s: (jnp.maximum((n_tiles - 1 - s) * n_chunks - 1, 0), 0)),
                tile(B_WIDTH), whole((1, A_WIDTH)), whole((1, A_WIDTH)), whole((4, CHUNK, 2 * CHUNK)),
                whole((4, CHUNK, 2 * CHUNK)), whole((4, CHUNK, CHUNK)), SMEM, whole((HEADS, CHUNK, CHUNK)),
                whole((D_MODEL, D_MODEL)), whole((1, D_MODEL))]
    out_shape = [jax.ShapeDtypeStruct((n_tok, IN_WIDTH), BF16), jax.ShapeDtypeStruct((D_MODEL, D_MODEL), BF16),
                 jax.ShapeDtypeStruct((4, 2 * CHUNK, CHUNK), F32), jax.ShapeDtypeStruct((CHUNK, A_WIDTH), F32),
                 jax.ShapeDtypeStruct((1, A_WIDTH), F32), jax.ShapeDtypeStruct((1, A_WIDTH), F32),
                 jax.ShapeDtypeStruct((1, D_MODEL), F32), jax.ShapeDtypeStruct((HEADS,), F32),
                 jax.ShapeDtypeStruct((HEADS, CHUNK, CHUNK), F32)]
    out_specs = [tile(IN_WIDTH), whole((D_MODEL, D_MODEL)), whole((4, 2 * CHUNK, CHUNK)), whole((CHUNK, A_WIDTH)),
                 whole((1, A_WIDTH)), whole((1, A_WIDTH)), whole((1, D_MODEL)), SMEM, whole((HEADS, CHUNK, CHUNK))]
    scratch = [pltpu.VMEM((tb, D_MODEL), BF16), pltpu.VMEM((tb, D_MODEL), BF16), pltpu.VMEM((tb, D_MODEL), F32),
               pltpu.VMEM((tb + CHUNK, 2 * KV_WIDTH), F32), pltpu.VMEM((CHUNK, 2 * KV_WIDTH), F32),
               pltpu.VMEM((D_MODEL, D_MODEL), F32)]
    args = [dout, y, a, q, kv, kv, bz, ln_g, ln_b, wcat, wtcat, bs, sinks, bias, w_out, g_post]
    if exchange is not None:
        dwt, dwo = exchange
        in_specs += [ANY, ANY]
        args += [dwt, dwo]
        out_shape += [jax.ShapeDtypeStruct((N_DEV, dwt.shape[0] // N_CHIPS, D_MODEL), BF16),
                      jax.ShapeDtypeStruct((N_DEV, dwo.shape[0] // N_CHIPS, D_MODEL), BF16)]
        out_specs += [ANY, ANY]
        scratch += _comm_scratch(N_EXCHANGE_SEMS)
    return pl.pallas_call(
        body, name="bwd_mix" if exchange is None else "bwd_mix_exchange", grid=(n_tiles,),
        out_shape=tuple(out_shape), in_specs=in_specs, out_specs=tuple(out_specs), scratch_shapes=scratch,
        compiler_params=_cparams(1),
    )(*args)


def _bwd_in(dproj, x, dout, g_pre, w_in_t, allgather=None):
    n_tok = x.shape[0]
    tm = TOK_TILE
    n_tiles = n_tok // tm

    def body(*refs):
        dp_ref, x_ref, dout_ref, g_ref, wt_ref = refs[:5]
        if allgather is None:
            dx_ref, dwt_ref, dg_ref, acc_s = refs[5:]
            at_start = at_end = lambda: None
        else:
            part_ref, dx_ref, dwt_ref, dg_ref, all_ref, acc_s, send_sems, recv_sems, local_sems = refs[5:]
            phases = _exchange_phases((part_ref,), (all_ref,), send_sems, recv_sems, local_sems, by_chip=False)
            at_start, at_end = _hosted(phases, pl.program_id(0), n_tiles)
        at_start()
        i = pl.program_id(0)

        @pl.when(i == 0)
        def _():
            acc_s[...] = jnp.zeros_like(acc_s)
            dg_ref[...] = jnp.zeros_like(dg_ref)

        xf = x_ref[...]
        r = lax.rsqrt(jnp.mean(xf * xf, axis=-1, keepdims=True) + NORM_EPS)
        xn = xf * r
        h = (xn * g_ref[...]).astype(BF16)
        dp = dp_ref[...]
        dh = _dot(dp, wt_ref[...])
        acc_s[...] += _dot_tn(dp, h)
        dg_ref[...] += jnp.sum(dh * xn, axis=0, keepdims=True)
        dhn = dh * g_ref[...]
        dx_ref[...] = dout_ref[...] + r * (dhn - xn * jnp.mean(dhn * xn, axis=-1, keepdims=True))

        @pl.when(i == n_tiles - 1)
        def _():
            dwt_ref[...] = acc_s[...].astype(BF16)

        at_end()

    in_specs = [pl.BlockSpec((tm, IN_WIDTH), lambda i: (i, 0)), pl.BlockSpec((tm, D_MODEL), lambda i: (i, 0)),
                pl.BlockSpec((tm, D_MODEL), lambda i: (i, 0)), pl.BlockSpec((1, D_MODEL), lambda i: (0, 0)),
                pl.BlockSpec((IN_WIDTH, D_MODEL), lambda i: (0, 0), pipeline_mode=pl.Buffered(1))]
    out_shape = [jax.ShapeDtypeStruct((n_tok, D_MODEL), F32), jax.ShapeDtypeStruct((IN_WIDTH, D_MODEL), BF16),
                 jax.ShapeDtypeStruct((1, D_MODEL), F32)]
    out_specs = [pl.BlockSpec((tm, D_MODEL), lambda i: (i, 0)),
                 pl.BlockSpec((IN_WIDTH, D_MODEL), lambda i: (0, 0), pipeline_mode=pl.Buffered(1)),
                 pl.BlockSpec((1, D_MODEL), lambda i: (0, 0))]
    scratch = [pltpu.VMEM((IN_WIDTH, D_MODEL), F32)]
    args = [dproj, x, dout, g_pre, w_in_t]
    if allgather is not None:
        in_specs.append(ANY)
        args.append(allgather)
        out_shape.append(jax.ShapeDtypeStruct((N_DEV,) + allgather.shape, allgather.dtype))
        out_specs.append(ANY)
        scratch += _comm_scratch(N_EXCHANGE_SEMS)
    return pl.pallas_call(
        body, name="bwd_in" if allgather is None else "bwd_in_allgather", grid=(n_tiles,),
        out_shape=tuple(out_shape), in_specs=in_specs, out_specs=tuple(out_specs), scratch_shapes=scratch,
        compiler_params=_cparams(1),
    )(*args)


def _adam_update(w, g, m, v):
    nm = ADAM_B1 * m + (1.0 - ADAM_B1) * g
    nv = ADAM_B2 * v + (1.0 - ADAM_B2) * (g * g)
    m_hat = nm / (1.0 - ADAM_B1 ** ADAM_STEP)
    v_hat = nv / (1.0 - ADAM_B2 ** ADAM_STEP)
    return -ADAM_LR * (m_hat / (jnp.sqrt(v_hat) + ADAM_EPS) + ADAM_WD * w), nm, nv


def _sum_slots(p_ref):
    tot = p_ref[0].astype(F32)
    for d in range(1, N_DEV):
        tot = tot + p_ref[d].astype(F32)
    return tot


def _adamw_parts(parts, w, m, v, layer, results, name):
    n_layers, n_rows, n_cols = w.shape
    tr = n_rows // 2 if n_rows * n_cols > 512 * 1024 else n_rows

    def body(*refs):
        p_ref, w_ref, m_ref, v_ref = refs[:4]
        g_ref, d_ref, nm_ref, nv_ref = refs[-4:]
        g = _sum_slots(p_ref)
        g_ref[0] = g
        d_ref[0], nm_ref[0], nv_ref[0] = _adam_update(w_ref[0], g, m_ref[0], v_ref[0])

    spec = pl.BlockSpec((1, tr, n_cols), lambda i: (layer, i, 0))
    shape = jax.ShapeDtypeStruct((n_layers, n_rows, n_cols), F32)
    kept = [] if results is None else list(results)
    return pl.pallas_call(
        body, name="adamw_" + name, grid=(n_rows // tr,), out_shape=(shape,) * 4,
        in_specs=[pl.BlockSpec((N_DEV, tr, n_cols), lambda i: (0, i, 0)), spec, spec, spec] + [ANY] * len(kept),
        out_specs=(spec,) * 4, input_output_aliases={4 + j: j for j in range(len(kept))},
        compiler_params=_cparams(1),
    )(parts, w, m, v, *kept)


def _adamw(w, g, m, v, name):
    n_rows, n_cols = w.shape
    tr = 512 if n_rows % 512 == 0 else n_rows

    def body(w_ref, g_ref, m_ref, v_ref, d_ref, nm_ref, nv_ref):
        d_ref[...], nm_ref[...], nv_ref[...] = _adam_update(w_ref[...], g_ref[...], m_ref[...], v_ref[...])

    spec = pl.BlockSpec((tr, n_cols), lambda i: (i, 0))
    shape = jax.ShapeDtypeStruct((n_rows, n_cols), F32)
    return pl.pallas_call(
        body, name="adamw_" + name, grid=(n_rows // tr,), out_shape=(shape, shape, shape),
        in_specs=[spec] * 4, out_specs=(spec, spec, spec), compiler_params=_cparams(1),
    )(w, g, m, v)


_SMALL = ("pre_norm_g", "ln_v_g", "ln_v_b", "b_spatial", "sinks", "rel_bias", "post_norm_g")


def _pack(pieces):
    blocks, first_rows, n = [], [], 0
    for p in pieces:
        flat = p.reshape(-1)
        flat = jnp.concatenate([flat, jnp.zeros(((-flat.shape[0]) % (8 * 128),), F32)]).reshape(-1, 128)
        blocks.append(flat)
        first_rows.append(n)
        n += flat.shape[0]
    return jnp.concatenate(blocks, axis=0), first_rows


def _unpack(block, first_row, shape):
    size = math.prod(shape)
    return block[first_row:first_row + -(-size // 128)].reshape(-1)[:size].reshape(shape)


def kernel(x, pre_norm_g, w_in, ln_v_g, ln_v_b, w_spatial, b_spatial, sinks, rel_bias, w_out, post_norm_g, loss_target, m_pre_norm_g, m_w_in, m_ln_v_g, m_ln_v_b, m_w_spatial, m_b_spatial, m_sinks, m_rel_bias, m_w_out, m_post_norm_g, v_pre_norm_g, v_w_in, v_ln_v_g, v_ln_v_b, v_w_spatial, v_b_spatial, v_sinks, v_rel_bias, v_w_out, v_post_norm_g):
    weights = dict(pre_norm_g=pre_norm_g, w_in=w_in, ln_v_g=ln_v_g, ln_v_b=ln_v_b, w_spatial=w_spatial, b_spatial=b_spatial,
                   sinks=sinks, rel_bias=rel_bias, w_out=w_out, post_norm_g=post_norm_g)
    mom_m = dict(pre_norm_g=m_pre_norm_g, w_in=m_w_in, ln_v_g=m_ln_v_g, ln_v_b=m_ln_v_b, w_spatial=m_w_spatial,
                 b_spatial=m_b_spatial, sinks=m_sinks, rel_bias=m_rel_bias, w_out=m_w_out, post_norm_g=m_post_norm_g)
    mom_v = dict(pre_norm_g=v_pre_norm_g, w_in=v_w_in, ln_v_g=v_ln_v_g, ln_v_b=v_ln_v_b, w_spatial=v_w_spatial,
                 b_spatial=v_b_spatial, sinks=v_sinks, rel_bias=v_rel_bias, w_out=v_w_out, post_norm_g=v_post_norm_g)
    n_seq, seq_len, _ = x.shape
    n_layers = w_in.shape[0]
    x2 = x.reshape(n_seq * seq_len, D_MODEL)
    target2 = loss_target.reshape(n_seq * seq_len, D_MODEL)
    row = lambda p, l: p[l][None]

    wt_shards = jnp.swapaxes(w_in, 1, 2).astype(BF16)
    wo_shards = w_out.astype(BF16)
    bucket = _window_buckets()
    bias = _bias_table(rel_bias, bucket)
    wcat, wtcat, bs = _spatial_tables(w_spatial, jnp.swapaxes(b_spatial, 1, 2))

    wt, wo = [None] * n_layers, [None] * n_layers
    wt[0], wo[0] = _gather_weights(wt_shards, wo_shards, 0)
    xs, saved = [x2], []
    for l in range(n_layers):
        a, q, kv, bz = _fwd_in(xs[-1], row(pre_norm_g, l), wt[l])
        mix_args = (xs[-1], a, q, kv, bz, row(ln_v_g, l), row(ln_v_b, l), wcat[l], bs[l], sinks[l], bias, wo[l],
                    row(post_norm_g, l), seq_len)
        if l + 1 < n_layers:
            xn, y, wt[l + 1], wo[l + 1] = _fwd_mix(*mix_args, gather=(wt_shards, wo_shards, l + 1))
            xs.append(xn)
        else:
            dx, y, loss = _fwd_mix(*mix_args, target=target2)
        saved.append((a, q, kv, bz, y))

    small = [None] * n_layers
    parts = [None] * n_layers
    waiting = None
    for l in reversed(range(n_layers)):
        a, q, kv, bz, y = saved[l]
        outs = _bwd_mix(dx, y, a, q, kv, bz, row(ln_v_g, l), row(ln_v_b, l), wcat[l], wtcat[l], bs[l], sinks[l], bias,
                        wo[l], row(post_norm_g, l), seq_len, exchange=None if waiting is None else waiting[:2])
        dproj, dwo, dws, dmix, dlng, dlnb, dgpost, dsink, dbias = outs[:9]
        exchanged = outs[9:]
        dws = dws.reshape(HEADS * CHUNK, CHUNK)
        if l > 0:
            dx, dwt, dgpre, ws_all = _bwd_in(dproj, xs[l], dx, row(pre_norm_g, l), wt[l], allgather=dws)
        else:
            dx, dwt, dgpre = _bwd_in(dproj, xs[l], dx, row(pre_norm_g, l), wt[l])
            ws_all = None
        waiting = (dwt, dwo, dws)
        parts[l] = [None, None, ws_all]
        small[l] = dict(pre_norm_g=dgpre[0], ln_v_g=dlng[0], ln_v_b=dlnb[0], dmix=dmix, sinks=dsink, dbias=dbias,
                        post_norm_g=dgpost[0])
        if l + 1 < n_layers:
            parts[l + 1][0:2] = exchanged
    parts[0][0:3] = _exchange_grads(*waiting)

    db = _b_spatial_grad(jnp.stack([s["dmix"] for s in small]))
    pieces = {n: jnp.stack([s[n] for s in small]) for n in ("pre_norm_g", "ln_v_g", "ln_v_b", "sinks", "post_norm_g")}
    pieces["b_spatial"] = jnp.swapaxes(db[:, :, :HEADS], 1, 2)
    pieces["rel_bias"] = _rel_bias_grad(jnp.stack([s["dbias"] for s in small]), bucket)
    packed, first_rows = _pack([pieces[n] for n in _SMALL] + [loss])
    total = _allreduce_small(packed)
    grad = {n: _unpack(total, r, weights[n].shape) for n, r in zip(_SMALL, first_rows)}
    loss_out = total[first_rows[-1], 0]

    delta, new_m, new_v = {}, {}, {}
    for n in _SMALL:
        w = weights[n]
        two_d = (-1, w.shape[-1])
        d, nm, nv = _adamw(w.reshape(two_d), grad[n].reshape(two_d), mom_m[n].reshape(two_d), mom_v[n].reshape(two_d), n)
        delta[n], new_m[n], new_v[n] = d.reshape(w.shape), nm.reshape(w.shape), nv.reshape(w.shape)

    t3 = lambda p: jnp.swapaxes(p, 1, 2)
    flat3 = lambda p: p.reshape(n_layers, HEADS * CHUNK, CHUNK)
    for n, which, view, back in (("w_in", 0, t3, t3), ("w_out", 1, lambda p: p, lambda p: p),
                                 ("w_spatial", 2, flat3, lambda p: p.reshape(w_spatial.shape))):
        results = None
        for l in reversed(range(n_layers)):
            results = _adamw_parts(parts[l][which], view(weights[n]), view(mom_m[n]), view(mom_v[n]), l, results, n)
        grad[n], delta[n], new_m[n], new_v[n] = (back(r) for r in results)

    names = tuple(weights)
    return (loss_out, dx.reshape(x.shape), *[grad[n] for n in names], *[delta[n] for n in names],
            *[new_m[n] for n in names], *[new_v[n] for n in names])
```

```python
import math

import jax
import jax.numpy as jnp
from jax import lax
from jax.experimental import pallas as pl
from jax.experimental.pallas import tpu as pltpu

F32 = jnp.float32
BF16 = jnp.bfloat16

D_MODEL = 1024
A_WIDTH = 512
B_WIDTH = 512
KV_WIDTH = 128
IN_WIDTH = 3 * A_WIDTH + 2 * B_WIDTH + 2 * KV_WIDTH
CHUNK = 128
HEADS = 8
HEAD_DIM = 64
REL_BUCKETS = 32
NORM_EPS = 1e-6
NEG = -1e30
Q_OFF = 3 * A_WIDTH
K_OFF = Q_OFF + B_WIDTH
BZ_OFF = K_OFF + 2 * KV_WIDTH
QK_SCALE = HEAD_DIM ** -0.5

ADAM_LR = 0.001
ADAM_B1 = 0.9
ADAM_B2 = 0.999
ADAM_EPS = 1e-08
ADAM_WD = 0.01
ADAM_STEP = 10

TOK_TILE = 512
CHUNK_UNROLL = 4
VMEM_LIMIT_V7X = 60 * 1024 * 1024

N_DEV = 8
N_CHIPS = 4
MESH_ID = pl.DeviceIdType.MESH
ANY = pl.BlockSpec(memory_space=pl.ANY)
SMEM = pl.BlockSpec(memory_space=pltpu.SMEM)


def _cparams(n_axes):
    return pltpu.CompilerParams(dimension_semantics=("arbitrary",) * n_axes, vmem_limit_bytes=VMEM_LIMIT_V7X)


_GELU_C = math.sqrt(2.0 / math.pi)


def _gelu(x):
    inner = _GELU_C * (x + 0.044715 * (x * x * x))
    return x * (0.5 * (1.0 + jnp.tanh(inner)))


def _gelu_and_grad(x):
    x2 = x * x
    t = jnp.tanh(_GELU_C * (x + 0.044715 * (x2 * x)))
    cdf = 0.5 * (1.0 + t)
    d = cdf + x * (0.5 * (1.0 - t * t)) * (_GELU_C * (1.0 + 3.0 * 0.044715 * x2))
    return x * cdf, d


def _sigmoid(x):
    return 0.5 + 0.5 * jnp.tanh(0.5 * x)


def _silu(x):
    return x * _sigmoid(x)


def _silu_and_grad(x):
    s = _sigmoid(x)
    return x * s, s * (1.0 + x * (1.0 - s))


def _dot(a, b):
    return jnp.dot(a, b, preferred_element_type=F32)


def _dot_nt(a, b):
    return lax.dot_general(a, b, (((1,), (1,)), ((), ())), preferred_element_type=F32)


def _dot_tn(a, b):
    return lax.dot_general(a, b, (((0,), (0,)), ((), ())), preferred_element_type=F32)


def _lo_mask(shape):
    return lax.broadcasted_iota(jnp.int32, shape, 1) < HEAD_DIM


def _swap_halves(v):
    return pltpu.roll(v, HEAD_DIM, 1)


def _from_prev(shape=(CHUNK, CHUNK)):
    return lax.broadcasted_iota(jnp.int32, shape, 1) > lax.broadcasted_iota(jnp.int32, shape, 0)


def _window_buckets():
    q_loc = jnp.arange(CHUNK)[:, None]
    j_loc = jnp.arange(CHUNK)[None, :]
    d = q_loc - j_loc + jnp.where(j_loc > q_loc, CHUNK, 0)
    max_exact = REL_BUCKETS // 2
    safe = jnp.maximum(d, 1).astype(F32)
    large = max_exact + (jnp.log(safe / max_exact) / math.log(CHUNK / max_exact)
                         * (REL_BUCKETS - max_exact)).astype(jnp.int32)
    large = jnp.minimum(large, REL_BUCKETS - 1)
    return jnp.where(d < max_exact, d, large).astype(jnp.int32)


def _bias_table(rel_bias, bucket):
    def body(rb_ref, bk_ref, out_ref):
        bk = bk_ref[...]
        for h in range(HEADS):
            acc = jnp.zeros(bk.shape, F32)
            for b in range(REL_BUCKETS):
                acc = jnp.where(bk == b, rb_ref[b, h], acc)
            out_ref[h] = acc

    vmem = pl.BlockSpec(memory_space=pltpu.VMEM)
    return pl.pallas_call(
        body, name="bias_table",
        out_shape=jax.ShapeDtypeStruct((HEADS, CHUNK, CHUNK), F32),
        in_specs=[SMEM, vmem], out_specs=vmem,
    )(rel_bias, bucket)


def _rel_bias_grad(dbias, bucket):
    n_layers = dbias.shape[0]

    def body(db_ref, bk_ref, out_ref):
        bk = bk_ref[...]
        for h in range(HEADS):
            tot = db_ref[0, h]
            for l in range(1, n_layers):
                tot = tot + db_ref[l, h]
            for b in range(REL_BUCKETS):
                out_ref[b, h] = jnp.sum(jnp.where(bk == b, tot, 0.0))

    vmem = pl.BlockSpec(memory_space=pltpu.VMEM)
    return pl.pallas_call(
        body, name="rel_bias_grad",
        out_shape=jax.ShapeDtypeStruct((REL_BUCKETS, HEADS), F32),
        in_specs=[vmem] * 2, out_specs=SMEM,
    )(dbias, bucket)


def _spatial_tables(w_spatial, b_spatial_t):
    n_layers = w_spatial.shape[0]

    def body(w_ref, b_ref, wcat_ref, wtcat_ref, bs_ref):
        row = lax.broadcasted_iota(jnp.int32, (CHUNK, CHUNK), 0)
        col = lax.broadcasted_iota(jnp.int32, (CHUNK, CHUNK), 1)
        causal = col <= row
        lo = _lo_mask((CHUNK, CHUNK))
        for p in range(4):
            for half in range(2):
                w = jnp.where(causal, w_ref[0, 2 * p + half], 0.0)
                wcat_ref[0, p, :, half * CHUNK:(half + 1) * CHUNK] = w.astype(BF16)
                wtcat_ref[0, p, :, half * CHUNK:(half + 1) * CHUNK] = w.T.astype(BF16)
            b = b_ref[0]
            bs_ref[0, p] = jnp.where(lo, b[:, 2 * p:2 * p + 1], b[:, 2 * p + 1:2 * p + 2])

    return pl.pallas_call(
        body, name="spatial_tables", grid=(n_layers,),
        out_shape=(jax.ShapeDtypeStruct((n_layers, 4, CHUNK, 2 * CHUNK), BF16),
                   jax.ShapeDtypeStruct((n_layers, 4, CHUNK, 2 * CHUNK), BF16),
                   jax.ShapeDtypeStruct((n_layers, 4, CHUNK, CHUNK), F32)),
        in_specs=[pl.BlockSpec((1, HEADS, CHUNK, CHUNK), lambda l: (l, 0, 0, 0)),
                  pl.BlockSpec((1, CHUNK, HEADS), lambda l: (l, 0, 0))],
        out_specs=(pl.BlockSpec((1, 4, CHUNK, 2 * CHUNK), lambda l: (l, 0, 0, 0)),
                   pl.BlockSpec((1, 4, CHUNK, 2 * CHUNK), lambda l: (l, 0, 0, 0)),
                   pl.BlockSpec((1, 4, CHUNK, CHUNK), lambda l: (l, 0, 0, 0))),
        compiler_params=_cparams(1),
    )(w_spatial, b_spatial_t)


def _b_spatial_grad(dmix):
    n_layers = dmix.shape[0]

    def body(d_ref, out_ref):
        lane = lax.broadcasted_iota(jnp.int32, (CHUNK, CHUNK), 1)
        acc = jnp.zeros((CHUNK, CHUNK), F32)
        for p in range(4):
            t = d_ref[0, :, p * CHUNK:(p + 1) * CHUNK]
            s_lo = jnp.sum(jnp.where(lane < HEAD_DIM, t, 0.0), axis=1, keepdims=True)
            s_hi = jnp.sum(jnp.where(lane < HEAD_DIM, 0.0, t), axis=1, keepdims=True)
            acc = jnp.where(lane == 2 * p, s_lo, acc)
            acc = jnp.where(lane == 2 * p + 1, s_hi, acc)
        out_ref[0] = acc

    return pl.pallas_call(
        body, name="b_spatial_grad", grid=(n_layers,),
        out_shape=jax.ShapeDtypeStruct((n_layers, CHUNK, CHUNK), F32),
        in_specs=[pl.BlockSpec((1, CHUNK, A_WIDTH), lambda l: (l, 0, 0))],
        out_specs=pl.BlockSpec((1, CHUNK, CHUNK), lambda l: (l, 0, 0)),
        compiler_params=_cparams(1),
    )(dmix)


def _place():
    x, y, c = lax.axis_index("x"), lax.axis_index("y"), lax.axis_index("c")
    other_chips = [(1 - x, y), (x, 1 - y), (1 - x, 1 - y)]
    return x, y, c, other_chips


N_GATHER_SEMS = 12


def _gather_phases(shards, fulls, send_sems, recv_sems, local_sems):
    x, y, c, chips = _place()
    sibling = (x, y, 1 - c)
    n_arr = len(shards)

    def half_rows(a, chip, half):
        n = shards[a].shape[0]
        start = (2 * chip[0] + chip[1]) * n + half * (n // 2)
        return fulls[a].at[pl.ds(pl.multiple_of(start, 16), n // 2), :]

    def my_half(a):
        n = shards[a].shape[0]
        return shards[a].at[pl.ds(pl.multiple_of(c * (n // 2), 16), n // 2), :]

    def copy(k, a, src, chip, half, to):
        return pltpu.make_async_remote_copy(
            src_ref=src, dst_ref=half_rows(a, chip, half), send_sem=send_sems.at[n_arr * k + a],
            recv_sem=recv_sems.at[n_arr * k + a], device_id=to, device_id_type=MESH_ID)

    def local(a):
        n = shards[a].shape[0]
        mine = fulls[a].at[pl.ds(pl.multiple_of((2 * x + y) * n, 16), n), :]
        return pltpu.make_async_copy(shards[a], mine, local_sems.at[a])

    def first(k, a):
        return copy(k, a, my_half(a), (x, y), c, (chips[k][0], chips[k][1], c))

    def passed(k, a):
        return copy(3 + k, a, half_rows(a, chips[k], c), chips[k], c, sibling)

    def phase_a():
        for a in range(n_arr):
            local(a).start()
        for k in range(3):
            for a in range(n_arr):
                first(k, a).start()

    def phase_b():
        for k in range(3):
            for a in range(n_arr):
                copy(k, a, my_half(a), chips[k], c, sibling).wait_recv()
                passed(k, a).start()

    def phase_c():
        for k in range(3):
            for a in range(n_arr):
                copy(3 + k, a, my_half(a), chips[k], 1 - c, sibling).wait_recv()
        for k in range(3):
            for a in range(n_arr):
                first(k, a).wait_send()
                passed(k, a).wait_send()
        for a in range(n_arr):
            local(a).wait()

    return phase_a, phase_b, phase_c


N_EXCHANGE_SEMS = 14


def _exchange_phases(partials, parts, send_sems, recv_sems, local_sems, by_chip=True):
    x, y, c, chips = _place()
    me, sibling = (x, y, c), (x, y, 1 - c)
    n_arr = len(partials)

    def block(a, chip):
        if not by_chip:
            return partials[a]
        n = partials[a].shape[0] // N_CHIPS
        return partials[a].at[pl.ds(pl.multiple_of((2 * chip[0] + chip[1]) * n, 16), n), :]

    def slot(a, dev):
        return parts[a].at[4 * dev[0] + 2 * dev[1] + dev[2]]

    def copy(k, a, src, origin, to):
        return pltpu.make_async_remote_copy(
            src_ref=src, dst_ref=slot(a, origin), send_sem=send_sems.at[n_arr * k + a],
            recv_sem=recv_sems.at[n_arr * k + a], device_id=to, device_id_type=MESH_ID)

    def local(a):
        return pltpu.make_async_copy(block(a, (x, y)), slot(a, me), local_sems.at[a])

    def first(k, a):
        if k == 0:
            return copy(0, a, block(a, (x, y)), me, sibling)
        chip = chips[k - 1]
        return copy(k, a, block(a, chip), me, (chip[0], chip[1], c))

    def passed(k, a):
        origin = (chips[k][0], chips[k][1], c)
        return copy(4 + k, a, slot(a, origin), origin, sibling)

    def phase_a():
        for a in range(n_arr):
            local(a).start()
        for k in range(4):
            for a in range(n_arr):
                first(k, a).start()

    def phase_b():
        for k in range(3):
            for a in range(n_arr):
                copy(1 + k, a, block(a, (x, y)), (chips[k][0], chips[k][1], c), me).wait_recv()
                passed(k, a).start()

    def phase_c():
        for a in range(n_arr):
            copy(0, a, block(a, (x, y)), sibling, me).wait_recv()
        for k in range(3):
            for a in range(n_arr):
                copy(4 + k, a, block(a, (x, y)), (chips[k][0], chips[k][1], 1 - c), me).wait_recv()
        for k in range(4):
            for a in range(n_arr):
                first(k, a).wait_send()
        for k in range(3):
            for a in range(n_arr):
                passed(k, a).wait_send()
        for a in range(n_arr):
            local(a).wait()

    return phase_a, phase_b, phase_c


def _comm_scratch(n_sems):
    return [pltpu.SemaphoreType.DMA((n_sems,)), pltpu.SemaphoreType.DMA((n_sems,)), pltpu.SemaphoreType.DMA((2,))]


def _gather_weights(wt_shards, wo_shards, layer):
    wt_rows, wo_rows = wt_shards.shape[1], wo_shards.shape[1]

    def body(wt_ref, wo_ref, wt_full, wo_full, send_sems, recv_sems, local_sems):
        phases = _gather_phases((wt_ref.at[layer], wo_ref.at[layer]), (wt_full, wo_full), send_sems, recv_sems, local_sems)
        for phase in phases:
            phase()

    return pl.pallas_call(
        body, name="gather_weights",
        out_shape=(jax.ShapeDtypeStruct((N_CHIPS * wt_rows, D_MODEL), BF16),
                   jax.ShapeDtypeStruct((N_CHIPS * wo_rows, D_MODEL), BF16)),
        in_specs=[ANY, ANY], out_specs=(ANY, ANY), scratch_shapes=_comm_scratch(N_GATHER_SEMS),
    )(wt_shards, wo_shards)


def _exchange_grads(dwt, dwo, small):
    def body(dwt_ref, dwo_ref, small_ref, pt_ref, po_ref, all_ref, *sems):
        blocks = _exchange_phases((dwt_ref, dwo_ref), (pt_ref, po_ref), *sems[:3])
        whole = _exchange_phases((small_ref,), (all_ref,), *sems[3:], by_chip=False)
        for phase_of_blocks, phase_of_whole in zip(blocks, whole):
            phase_of_blocks()
            phase_of_whole()

    return pl.pallas_call(
        body, name="exchange_grads",
        out_shape=(jax.ShapeDtypeStruct((N_DEV, dwt.shape[0] // N_CHIPS, D_MODEL), BF16),
                   jax.ShapeDtypeStruct((N_DEV, dwo.shape[0] // N_CHIPS, D_MODEL), BF16),
                   jax.ShapeDtypeStruct((N_DEV,) + small.shape, small.dtype)),
        in_specs=[ANY, ANY, ANY], out_specs=(ANY, ANY, ANY),
        scratch_shapes=_comm_scratch(N_EXCHANGE_SEMS) + _comm_scratch(N_EXCHANGE_SEMS),
    )(dwt, dwo, small)


def _allreduce_small(part):
    n_rows = part.shape[0]

    def body(p_ref, tot_ref, all_ref, send_sems, recv_sems, local_sem):
        x, y, c, chips = _place()
        me, sibling = (x, y, c), (x, y, 1 - c)

        def rows(dev):
            return all_ref.at[pl.ds(pl.multiple_of((4 * dev[0] + 2 * dev[1] + dev[2]) * n_rows, 8), n_rows), :]

        def copy(k, origin, to, src=None):
            return pltpu.make_async_remote_copy(
                src_ref=rows(origin) if src is None else src, dst_ref=rows(origin), send_sem=send_sems.at[k],
                recv_sem=recv_sems.at[k], device_id=to, device_id_type=MESH_ID)

        mine = pltpu.make_async_copy(p_ref, rows(me), local_sem)
        mine.start()
        first = [copy(0, me, sibling, src=p_ref)]
        first += [copy(1 + k, me, (chip[0], chip[1], c), src=p_ref) for k, chip in enumerate(chips)]
        for cp in first:
            cp.start()
        passed = []
        for k, chip in enumerate(chips):
            origin = (chip[0], chip[1], c)
            copy(1 + k, origin, me).wait_recv()
            fwd = copy(4 + k, origin, sibling)
            fwd.start()
            passed.append(fwd)
        copy(0, sibling, me).wait_recv()
        for k, chip in enumerate(chips):
            copy(4 + k, (chip[0], chip[1], 1 - c), me).wait_recv()
        for cp in first + passed:
            cp.wait_send()
        mine.wait()
        tot = all_ref[0:n_rows, :]
        for d in range(1, N_DEV):
            tot = tot + all_ref[d * n_rows:(d + 1) * n_rows, :]
        tot_ref[...] = tot

    vmem = pl.BlockSpec(memory_space=pltpu.VMEM)
    return pl.pallas_call(
        body, name="allreduce_small",
        out_shape=jax.ShapeDtypeStruct((n_rows, 128), F32),
        in_specs=[vmem], out_specs=vmem,
        scratch_shapes=[pltpu.VMEM((N_DEV * n_rows, 128), F32), pltpu.SemaphoreType.DMA((7,)),
                        pltpu.SemaphoreType.DMA((7,)), pltpu.SemaphoreType.DMA],
        compiler_params=pltpu.CompilerParams(vmem_limit_bytes=VMEM_LIMIT_V7X),
    )(part)


def _hosted(phases, step, n_steps):
    phase_a, phase_b, phase_c = phases

    def at_start():
        pl.when(step == 0)(phase_a)

    def at_end():
        pl.when(step == n_steps // 2)(phase_b)
        pl.when(step == n_steps - 1)(phase_c)

    return at_start, at_end


def _fwd_in(x, g_pre, w_in_t):
    n_tok = x.shape[0]

    def body(x_ref, g_ref, w_ref, a_ref, q_ref, kv_ref, bz_ref):
        xf = x_ref[...]
        r = lax.rsqrt(jnp.mean(xf * xf, axis=-1, keepdims=True) + NORM_EPS)
        h = ((xf * r) * g_ref[...]).astype(BF16)
        a_ref[...] = _dot_nt(h, w_ref[0:Q_OFF, :])
        q_ref[...] = _dot_nt(h, w_ref[Q_OFF:K_OFF, :]).astype(BF16)
        kv_ref[...] = _dot_nt(h, w_ref[K_OFF:BZ_OFF, :]).astype(BF16)
        bz_ref[...] = _dot_nt(h, w_ref[BZ_OFF:IN_WIDTH, :])

    tm = TOK_TILE
    return pl.pallas_call(
        body, name="fwd_in", grid=(n_tok // tm,),
        out_shape=(jax.ShapeDtypeStruct((n_tok, Q_OFF), F32), jax.ShapeDtypeStruct((n_tok, B_WIDTH), BF16),
                   jax.ShapeDtypeStruct((n_tok, 2 * KV_WIDTH), BF16), jax.ShapeDtypeStruct((n_tok, B_WIDTH), F32)),
        in_specs=[pl.BlockSpec((tm, D_MODEL), lambda i: (i, 0)), pl.BlockSpec((1, D_MODEL), lambda i: (0, 0)),
                  pl.BlockSpec((IN_WIDTH, D_MODEL), lambda i: (0, 0))],
        out_specs=(pl.BlockSpec((tm, Q_OFF), lambda i: (i, 0)), pl.BlockSpec((tm, B_WIDTH), lambda i: (i, 0)),
                   pl.BlockSpec((tm, 2 * KV_WIDTH), lambda i: (i, 0)), pl.BlockSpec((tm, B_WIDTH), lambda i: (i, 0))),
        compiler_params=_cparams(1),
    )(x, g_pre, w_in_t)


def _layer_norm_stats(vv):
    mu = jnp.mean(vv, axis=-1, keepdims=True)
    xc = vv - mu
    rs = lax.rsqrt(jnp.mean(xc * xc, axis=-1, keepdims=True) + NORM_EPS)
    return xc * rs, rs


def _blockdiag(v, lo):
    zero = jnp.zeros_like(v)
    return jnp.concatenate([jnp.where(lo, v, zero), jnp.where(lo, zero, v)], axis=0)


def _softmax_sink(s, sink):
    m = jnp.maximum(jnp.max(s, axis=-1, keepdims=True), sink)
    e = jnp.exp(s - m)
    esink = jnp.exp(sink - m)
    den = jnp.sum(e, axis=-1, keepdims=True) + esink
    return e / den, esink / den


def _kv_rows(cur_ref, halo_ref, r0, c):
    prev_in_tile = cur_ref[pl.ds(pl.multiple_of(jnp.maximum(r0 - CHUNK, 0), CHUNK), CHUNK), :]
    prev = jnp.where(c == 0, halo_ref[...], prev_in_tile)
    kv2 = jnp.concatenate([prev, cur_ref[pl.ds(r0, CHUNK), :]], axis=0)
    k2, v2 = kv2[:, 0:KV_WIDTH], kv2[:, KV_WIDTH:2 * KV_WIDTH]
    return (k2, _swap_halves(k2)), (v2, _swap_halves(v2))


def _window_square(over_keys, prev):
    return jnp.where(prev, over_keys[:, 0:CHUNK], over_keys[:, CHUNK:2 * CHUNK])


def _window_keys(square, prev):
    zero = jnp.zeros_like(square)
    return jnp.concatenate([jnp.where(prev, square, zero), jnp.where(prev, zero, square)], axis=1)


def _dead_mask(first):
    shape = (CHUNK, CHUNK)
    ahead = lax.broadcasted_iota(jnp.int32, shape, 1) - lax.broadcasted_iota(jnp.int32, shape, 0)
    return ahead > jnp.where(first, 0, CHUNK)


def _head_of(p, half):
    return 2 * p + half, int(half != p // 2)


_HEADS_BY_COPY = tuple(tuple((p, half) for p in range(4) for half in range(2) if _head_of(p, half)[1] == sw)
                       for sw in range(2))


def _masked_halves(tile, lo):
    zero = jnp.zeros_like(tile)
    return {0: jnp.where(lo, tile, zero), 1: jnp.where(lo, zero, tile)}


def _attention_probs(q_tiles, ks, bias_ref, sink_ref, dead, prev, lo):
    q_stacks, probs, psinks = [], {}, {}
    qm = {p: _masked_halves(q_tiles[p], lo) for p in range(4)}
    for sw, members in enumerate(_HEADS_BY_COPY):
        q_stack = jnp.concatenate([qm[p][half] for p, half in members], axis=0)
        s_stack = _dot_nt(q_stack, ks[sw])
        q_stacks.append(q_stack)
        for i, (p, half) in enumerate(members):
            head = 2 * p + half
            s = _window_square(s_stack[i * CHUNK:(i + 1) * CHUNK, :], prev) * QK_SCALE + bias_ref[head]
            s = jnp.where(dead, NEG, s)
            probs[p, half], psinks[p, half] = _softmax_sink(s, sink_ref[head])
    return q_stacks, probs, psinks


def _attention_values(probs, vs, prev):
    outs, p_stacks = {}, []
    for sw, members in enumerate(_HEADS_BY_COPY):
        p_stack = jnp.concatenate([_window_keys(probs[m].astype(BF16), prev) for m in members], axis=0)
        r_stack = _dot(p_stack, vs[sw])
        p_stacks.append(p_stack)
        for i, m in enumerate(members):
            outs[m] = r_stack[i * CHUNK:(i + 1) * CHUNK, :]
    return outs, p_stacks


def _fwd_mix(x, a, q, kv, bz, ln_g, ln_b, wcat, bs, sinks, bias, w_out, g_post, seq_len, gather=None, target=None):
    n_tok = x.shape[0]
    tb = TOK_TILE
    n_chunks = tb // CHUNK
    n_tiles = n_tok // tb
    n_in = 14
    assert gather is None or target is None

    def body(*refs):
        (x_ref, a_ref, q_ref, kv_ref, halo_ref, bz_ref, lng_ref, lnb_ref, wcat_ref, bs_ref, sink_ref, bias_ref,
         wout_ref, gpost_ref) = refs[:n_in]
        at_start = at_end = lambda: None
        if gather is not None:
            wts_ref, wos_ref, xn_ref, y_ref, wt_full, wo_full, ycat_s, send_sems, recv_sems, local_sems = refs[n_in:]
            phases = _gather_phases((wts_ref.at[gather[2]], wos_ref.at[gather[2]]), (wt_full, wo_full), send_sems,
                                    recv_sems, local_sems)
            at_start, at_end = _hosted(phases, pl.program_id(0), n_tiles)
        elif target is not None:
            target_ref, xn_ref, y_ref, loss_ref, ycat_s = refs[n_in:]
        else:
            xn_ref, y_ref, ycat_s = refs[n_in:]
        at_start()
        i = pl.program_id(0)
        lo = _lo_mask((CHUNK, CHUNK))
        prev = _from_prev()

        def chunk(c, carry):
            r0 = pl.multiple_of(c * CHUNK, CHUNK)
            rows = pl.ds(r0, CHUNK)
            u = _gelu(a_ref[rows, 0:A_WIDTH])
            vv = _gelu(a_ref[rows, A_WIDTH:2 * A_WIDTH])
            xhat, _ = _layer_norm_stats(vv)
            vnb = (xhat * lng_ref[...] + lnb_ref[...]).astype(BF16)
            for p in range(4):
                blk = slice(p * CHUNK, (p + 1) * CHUNK)
                mixed = _dot(wcat_ref[p], _blockdiag(vnb[:, blk], lo)) + bs_ref[p]
                sz = _silu(a_ref[rows, 2 * A_WIDTH + p * CHUNK:2 * A_WIDTH + (p + 1) * CHUNK])
                ycat_s[rows, blk] = ((u[:, blk] * mixed) * sz).astype(BF16)
            ks, vs = _kv_rows(kv_ref, halo_ref, r0, c)
            dead = _dead_mask(lax.rem(i * tb + r0, seq_len) == 0)
            q_tiles = [q_ref[rows, p * CHUNK:(p + 1) * CHUNK] for p in range(4)]
            _, probs, _ = _attention_probs(q_tiles, ks, bias_ref, sink_ref, dead, prev, lo)
            outs, _ = _attention_values(probs, vs, prev)
            for p in range(4):
                blk = slice(p * CHUNK, (p + 1) * CHUNK)
                o = jnp.where(lo, outs[p, 0], outs[p, 1])
                ycat_s[rows, B_WIDTH + p * CHUNK:B_WIDTH + (p + 1) * CHUNK] = (o * _silu(bz_ref[rows, blk])).astype(BF16)
            return carry

        lax.fori_loop(0, n_chunks, chunk, 0, unroll=CHUNK_UNROLL)
        y = _dot(ycat_s[...], wout_ref[...])
        r = lax.rsqrt(jnp.mean(y * y, axis=-1, keepdims=True) + NORM_EPS)
        y_ref[...] = y
        xn = x_ref[...] + (y * r) * gpost_ref[...]
        if target is None:
            xn_ref[...] = xn
        else:
            d = xn - target_ref[...]
            xn_ref[...] = d * (1.0 / D_MODEL)

            @pl.when(i == 0)
            def _():
                loss_ref[0, 0] = 0.0

            loss_ref[0, 0] += 0.5 * jnp.sum(jnp.mean(d * d, axis=-1, keepdims=True))
        at_end()

    tile = lambda w: pl.BlockSpec((tb, w), lambda i: (i, 0))
    whole = lambda shape: pl.BlockSpec(shape, lambda i: (0,) * len(shape))
    in_specs = [tile(D_MODEL), tile(Q_OFF), tile(B_WIDTH), tile(2 * KV_WIDTH),
                pl.BlockSpec((CHUNK, 2 * KV_WIDTH), lambda i: (jnp.maximum(i * n_chunks - 1, 0), 0)),
                tile(B_WIDTH), whole((1, A_WIDTH)), whole((1, A_WIDTH)), whole((4, CHUNK, 2 * CHUNK)),
                whole((4, CHUNK, CHUNK)), SMEM, whole((HEADS, CHUNK, CHUNK)), whole((D_MODEL, D_MODEL)),
                whole((1, D_MODEL))]
    out_shape = [jax.ShapeDtypeStruct((n_tok, D_MODEL), F32), jax.ShapeDtypeStruct((n_tok, D_MODEL), F32)]
    out_specs = [tile(D_MODEL), tile(D_MODEL)]
    scratch = [pltpu.VMEM((tb, D_MODEL), BF16)]
    args = [x, a, q, kv, kv, bz, ln_g, ln_b, wcat, bs, sinks, bias, w_out, g_post]
    if gather is not None:
        wt_shards, wo_shards, _ = gather
        in_specs += [ANY, ANY]
        args += [wt_shards, wo_shards]
        out_shape += [jax.ShapeDtypeStruct((N_CHIPS * wt_shards.shape[1], D_MODEL), BF16),
                      jax.ShapeDtypeStruct((N_CHIPS * wo_shards.shape[1], D_MODEL), BF16)]
        out_specs += [ANY, ANY]
        scratch += _comm_scratch(N_GATHER_SEMS)
    if target is not None:
        in_specs.append(tile(D_MODEL))
        args.append(target)
        out_shape.append(jax.ShapeDtypeStruct((1, 1), F32))
        out_specs.append(SMEM)
    name = "fwd_mix" + ("" if gather is None else "_gather") + ("" if target is None else "_loss")
    return pl.pallas_call(
        body, name=name, grid=(n_tiles,),
        out_shape=tuple(out_shape), in_specs=in_specs, out_specs=tuple(out_specs), scratch_shapes=scratch,
        compiler_params=_cparams(1),
    )(*args)


def _bwd_mix(dout, y, a, q, kv, bz, ln_g, ln_b, wcat, wtcat, bs, sinks, bias, w_out, g_post, seq_len, exchange=None):
    n_tok = y.shape[0]
    tb = TOK_TILE
    n_chunks = tb // CHUNK
    n_tiles = n_tok // tb
    n_in, n_out = 16, 9

    def body(*refs):
        (dout_ref, y_ref, a_ref, q_ref, kv_ref, halo_ref, bz_ref, lng_ref, lnb_ref, wcat_ref, wtcat_ref, bs_ref,
         sink_ref, bias_ref, wout_ref, gpost_ref) = refs[:n_in]
        if exchange is None:
            outs, rest = refs[n_in:n_in + n_out], refs[n_in + n_out:]
            at_start = at_end = lambda: None
        else:
            dwt_ref, dwo_ref = refs[n_in:n_in + 2]
            outs = refs[n_in + 2:n_in + 2 + n_out]
            pt_ref, po_ref = refs[n_in + 2 + n_out:n_in + 4 + n_out]
            rest = refs[n_in + 4 + n_out:]
            phases = _exchange_phases((dwt_ref, dwo_ref), (pt_ref, po_ref), *rest[6:])
            at_start, at_end = _hosted(phases, pl.program_id(0), n_tiles)
        dproj_ref, dwout_ref, dws_ref, dmix_ref, dlng_ref, dlnb_ref, dgpost_ref, dsink_ref, dbias_ref = outs
        ycat_s, dy_s, dyc_s, dkv_s, carry_s, dwout_s = rest[:6]
        at_start()
        step = pl.program_id(0)
        tile_idx = n_tiles - 1 - step
        lo = _lo_mask((CHUNK, CHUNK))
        prev = _from_prev()

        @pl.when(step == 0)
        def _():
            dwout_s[...] = jnp.zeros_like(dwout_s)
            dws_ref[...] = jnp.zeros_like(dws_ref)
            dmix_ref[...] = jnp.zeros_like(dmix_ref)
            dlng_ref[...] = jnp.zeros_like(dlng_ref)
            dlnb_ref[...] = jnp.zeros_like(dlnb_ref)
            dgpost_ref[...] = jnp.zeros_like(dgpost_ref)
            dbias_ref[...] = jnp.zeros_like(dbias_ref)
            carry_s[...] = jnp.zeros_like(carry_s)
            for h in range(HEADS):
                dsink_ref[h] = 0.0

        yv = y_ref[...]
        dout = dout_ref[...]
        r = lax.rsqrt(jnp.mean(yv * yv, axis=-1, keepdims=True) + NORM_EPS)
        yn = yv * r
        dgpost_ref[...] += jnp.sum(dout * yn, axis=0, keepdims=True)
        dyn = dout * gpost_ref[...]
        dy = r * (dyn - yn * jnp.mean(dyn * yn, axis=-1, keepdims=True))
        dy_s[...] = dy.astype(BF16)
        dyc_s[...] = _dot_nt(dy_s[...], wout_ref[...])
        dkv_s[0:tb, :] = jnp.zeros((tb, 2 * KV_WIDTH), F32)
        dkv_s[tb:tb + CHUNK, :] = carry_s[...]

        def chunk(c, carry):
            r0 = pl.multiple_of(c * CHUNK, CHUNK)
            rows = pl.ds(r0, CHUNK)
            u, gu = _gelu_and_grad(a_ref[rows, 0:A_WIDTH])
            vv, gv = _gelu_and_grad(a_ref[rows, A_WIDTH:2 * A_WIDTH])
            xhat, rs = _layer_norm_stats(vv)
            vnb = (xhat * lng_ref[...] + lnb_ref[...]).astype(BF16)
            d_vn, d_u, d_az = [], [], []
            for p in range(4):
                blk = slice(p * CHUNK, (p + 1) * CHUNK)
                vbd = _blockdiag(vnb[:, blk], lo)
                mixed = _dot(wcat_ref[p], vbd) + bs_ref[p]
                sz, gz = _silu_and_grad(a_ref[rows, 2 * A_WIDTH + p * CHUNK:2 * A_WIDTH + (p + 1) * CHUNK])
                ub = u[:, blk]
                dya = dyc_s[rows, blk]
                um = ub * mixed
                ycat_s[rows, blk] = (um * sz).astype(BF16)
                d_mixed = (dya * ub) * sz
                d_u.append((dya * mixed) * sz)
                d_az.append((dya * um) * gz)
                dmix_ref[:, blk] += d_mixed
                dmbd = _blockdiag(d_mixed.astype(BF16), lo)
                d_vn.append(_dot(wtcat_ref[p], dmbd))
                dws_ref[p] += _dot_nt(dmbd, vnb[:, blk])
            d_vn = jnp.concatenate(d_vn, axis=1)
            dlng_ref[...] += jnp.sum(d_vn * xhat, axis=0, keepdims=True)
            dlnb_ref[...] += jnp.sum(d_vn, axis=0, keepdims=True)
            dxh = d_vn * lng_ref[...]
            d_vv = rs * (dxh - jnp.mean(dxh, axis=-1, keepdims=True)
                         - xhat * jnp.mean(dxh * xhat, axis=-1, keepdims=True))
            dproj_ref[rows, 0:A_WIDTH] = (jnp.concatenate(d_u, axis=1) * gu).astype(BF16)
            dproj_ref[rows, A_WIDTH:2 * A_WIDTH] = (d_vv * gv).astype(BF16)
            dproj_ref[rows, 2 * A_WIDTH:Q_OFF] = jnp.concatenate(d_az, axis=1).astype(BF16)
            ks, vs = _kv_rows(kv_ref, halo_ref, r0, c)
            dead = _dead_mask(lax.rem(tile_idx * tb + r0, seq_len) == 0)
            q_tiles = [q_ref[rows, p * CHUNK:(p + 1) * CHUNK] for p in range(4)]
            q_stacks, probs, psinks = _attention_probs(q_tiles, ks, bias_ref, sink_ref, dead, prev, lo)
            outs_h, p_stacks = _attention_values(probs, vs, prev)
            dom = {}
            for p in range(4):
                blk = slice(p * CHUNK, (p + 1) * CHUNK)
                sb, gb = _silu_and_grad(bz_ref[rows, blk])
                dyb = dyc_s[rows, B_WIDTH + p * CHUNK:B_WIDTH + (p + 1) * CHUNK]
                o = jnp.where(lo, outs_h[p, 0], outs_h[p, 1])
                ycat_s[rows, B_WIDTH + p * CHUNK:B_WIDTH + (p + 1) * CHUNK] = (o * sb).astype(BF16)
                dproj_ref[rows, BZ_OFF + p * CHUNK:BZ_OFF + (p + 1) * CHUNK] = ((dyb * o) * gb).astype(BF16)
                dom[p] = _masked_halves((dyb * sb).astype(BF16), lo)
            dqs, dk_by_copy, dv_by_copy = {}, [], []
            for sw, members in enumerate(_HEADS_BY_COPY):
                do_stack = jnp.concatenate([dom[p][half] for p, half in members], axis=0)
                dp_stack = _dot_nt(do_stack, vs[sw])
                dsws = []
                for i, (p, half) in enumerate(members):
                    head = 2 * p + half
                    dp = _window_square(dp_stack[i * CHUNK:(i + 1) * CHUNK, :], prev)
                    delta = jnp.sum(probs[p, half] * dp, axis=-1, keepdims=True)
                    ds = probs[p, half] * (dp - delta)
                    dsink_ref[head] += -jnp.sum(psinks[p, half] * delta)
                    dbias_ref[head] += ds
                    dsws.append(_window_keys(ds.astype(BF16), prev))
                ds_stack = jnp.concatenate(dsws, axis=0)
                dq_stack = _dot(ds_stack, ks[sw])
                for i, m in enumerate(members):
                    dqs[m] = dq_stack[i * CHUNK:(i + 1) * CHUNK, :]
                dk_by_copy.append(_dot_tn(ds_stack, q_stacks[sw]))
                dv_by_copy.append(_dot_tn(p_stacks[sw], do_stack))
            for p in range(4):
                dq = jnp.where(lo, dqs[p, 0], dqs[p, 1]) * QK_SCALE
                dproj_ref[rows, Q_OFF + p * CHUNK:Q_OFF + (p + 1) * CHUNK] = dq.astype(BF16)
            both = pl.ds(r0, 2 * CHUNK)
            dkv_s[both, 0:KV_WIDTH] += (dk_by_copy[0] + _swap_halves(dk_by_copy[1])) * QK_SCALE
            dkv_s[both, KV_WIDTH:2 * KV_WIDTH] += dv_by_copy[0] + _swap_halves(dv_by_copy[1])
            return carry

        lax.fori_loop(0, n_chunks, chunk, 0, unroll=CHUNK_UNROLL)
        dwout_s[...] += _dot_tn(ycat_s[...], dy_s[...])
        dproj_ref[:, K_OFF:BZ_OFF] = dkv_s[CHUNK:CHUNK + tb, :].astype(BF16)
        carry_s[...] = dkv_s[0:CHUNK, :]

        @pl.when(step == n_tiles - 1)
        def _():
            row = lax.broadcasted_iota(jnp.int32, (2 * CHUNK, CHUNK), 0)
            col = lax.broadcasted_iota(jnp.int32, (2 * CHUNK, CHUNK), 1)
            causal = col <= jnp.where(row >= CHUNK, row - CHUNK, row)
            for p in range(4):
                dws_ref[p] = jnp.where(causal, dws_ref[p], 0.0)
            dwout_ref[...] = dwout_s[...].astype(BF16)

        at_end()

    tile = lambda w: pl.BlockSpec((tb, w), lambda s: (n_tiles - 1 - s, 0))
    whole = lambda shape: pl.BlockSpec(shape, lambda s: (0,) * len(shape))
    in_specs = [tile(D_MODEL), tile(D_MODEL), tile(Q_OFF), tile(B_WIDTH), tile(2 * KV_WIDTH),
                pl.BlockSpec((CHUNK, 2 * KV_WIDTH), lambda s: (jnp.maximum((n_tiles - 1 - s) * n_chunks - 1, 0), 0)),
                tile(B_WIDTH), whole((1, A_WIDTH)), whole((1, A_WIDTH)), whole((4, CHUNK, 2 * CHUNK)),
                whole((4, CHUNK, 2 * CHUNK)), whole((4, CHUNK, CHUNK)), SMEM, whole((HEADS, CHUNK, CHUNK)),
                whole((D_MODEL, D_MODEL)), whole((1, D_MODEL))]
    out_shape = [jax.ShapeDtypeStruct((n_tok, IN_WIDTH), BF16), jax.ShapeDtypeStruct((D_MODEL, D_MODEL), BF16),
                 jax.ShapeDtypeStruct((4, 2 * CHUNK, CHUNK), F32), jax.ShapeDtypeStruct((CHUNK, A_WIDTH), F32),
                 jax.ShapeDtypeStruct((1, A_WIDTH), F32), jax.ShapeDtypeStruct((1, A_WIDTH), F32),
                 jax.ShapeDtypeStruct((1, D_MODEL), F32), jax.ShapeDtypeStruct((HEADS,), F32),
                 jax.ShapeDtypeStruct((HEADS, CHUNK, CHUNK), F32)]
    out_specs = [tile(IN_WIDTH), whole((D_MODEL, D_MODEL)), whole((4, 2 * CHUNK, CHUNK)), whole((CHUNK, A_WIDTH)),
                 whole((1, A_WIDTH)), whole((1, A_WIDTH)), whole((1, D_MODEL)), SMEM, whole((HEADS, CHUNK, CHUNK))]
    scratch = [pltpu.VMEM((tb, D_MODEL), BF16), pltpu.VMEM((tb, D_MODEL), BF16), pltpu.VMEM((tb, D_MODEL), F32),
               pltpu.VMEM((tb + CHUNK, 2 * KV_WIDTH), F32), pltpu.VMEM((CHUNK, 2 * KV_WIDTH), F32),
               pltpu.VMEM((D_MODEL, D_MODEL), F32)]
    args = [dout, y, a, q, kv, kv, bz, ln_g, ln_b, wcat, wtcat, bs, sinks, bias, w_out, g_post]
    if exchange is not None:
        dwt, dwo = exchange
        in_specs += [ANY, ANY]
        args += [dwt, dwo]
        out_shape += [jax.ShapeDtypeStruct((N_DEV, dwt.shape[0] // N_CHIPS, D_MODEL), BF16),
                      jax.ShapeDtypeStruct((N_DEV, dwo.shape[0] // N_CHIPS, D_MODEL), BF16)]
        out_specs += [ANY, ANY]
        scratch += _comm_scratch(N_EXCHANGE_SEMS)
    return pl.pallas_call(
        body, name="bwd_mix" if exchange is None else "bwd_mix_exchange", grid=(n_tiles,),
        out_shape=tuple(out_shape), in_specs=in_specs, out_specs=tuple(out_specs), scratch_shapes=scratch,
        compiler_params=_cparams(1),
    )(*args)


def _bwd_in(dproj, x, dout, g_pre, w_in_t, allgather=None):
    n_tok = x.shape[0]
    tm = TOK_TILE
    n_tiles = n_tok // tm

    def body(*refs):
        dp_ref, x_ref, dout_ref, g_ref, wt_ref = refs[:5]
        if allgather is None:
            dx_ref, dwt_ref, dg_ref, acc_s = refs[5:]
            at_start = at_end = lambda: None
        else:
            part_ref, dx_ref, dwt_ref, dg_ref, all_ref, acc_s, send_sems, recv_sems, local_sems = refs[5:]
            phases = _exchange_phases((part_ref,), (all_ref,), send_sems, recv_sems, local_sems, by_chip=False)
            at_start, at_end = _hosted(phases, pl.program_id(0), n_tiles)
        at_start()
        i = pl.program_id(0)

        @pl.when(i == 0)
        def _():
            acc_s[...] = jnp.zeros_like(acc_s)
            dg_ref[...] = jnp.zeros_like(dg_ref)

        xf = x_ref[...]
        r = lax.rsqrt(jnp.mean(xf * xf, axis=-1, keepdims=True) + NORM_EPS)
        xn = xf * r
        h = (xn * g_ref[...]).astype(BF16)
        dp = dp_ref[...]
        dh = _dot(dp, wt_ref[...])
        acc_s[...] += _dot_tn(dp, h)
        dg_ref[...] += jnp.sum(dh * xn, axis=0, keepdims=True)
        dhn = dh * g_ref[...]
        dx_ref[...] = dout_ref[...] + r * (dhn - xn * jnp.mean(dhn * xn, axis=-1, keepdims=True))

        @pl.when(i == n_tiles - 1)
        def _():
            dwt_ref[...] = acc_s[...].astype(BF16)

        at_end()

    in_specs = [pl.BlockSpec((tm, IN_WIDTH), lambda i: (i, 0)), pl.BlockSpec((tm, D_MODEL), lambda i: (i, 0)),
                pl.BlockSpec((tm, D_MODEL), lambda i: (i, 0)), pl.BlockSpec((1, D_MODEL), lambda i: (0, 0)),
                pl.BlockSpec((IN_WIDTH, D_MODEL), lambda i: (0, 0), pipeline_mode=pl.Buffered(1))]
    out_shape = [jax.ShapeDtypeStruct((n_tok, D_MODEL), F32), jax.ShapeDtypeStruct((IN_WIDTH, D_MODEL), BF16),
                 jax.ShapeDtypeStruct((1, D_MODEL), F32)]
    out_specs = [pl.BlockSpec((tm, D_MODEL), lambda i: (i, 0)),
                 pl.BlockSpec((IN_WIDTH, D_MODEL), lambda i: (0, 0), pipeline_mode=pl.Buffered(1)),
                 pl.BlockSpec((1, D_MODEL), lambda i: (0, 0))]
    scratch = [pltpu.VMEM((IN_WIDTH, D_MODEL), F32)]
    args = [dproj, x, dout, g_pre, w_in_t]
    if allgather is not None:
        in_specs.append(ANY)
        args.append(allgather)
        out_shape.append(jax.ShapeDtypeStruct((N_DEV,) + allgather.shape, allgather.dtype))
        out_specs.append(ANY)
        scratch += _comm_scratch(N_EXCHANGE_SEMS)
    return pl.pallas_call(
        body, name="bwd_in" if allgather is None else "bwd_in_allgather", grid=(n_tiles,),
        out_shape=tuple(out_shape), in_specs=in_specs, out_specs=tuple(out_specs), scratch_shapes=scratch,
        compiler_params=_cparams(1),
    )(*args)


def _adam_update(w, g, m, v):
    nm = ADAM_B1 * m + (1.0 - ADAM_B1) * g
    nv = ADAM_B2 * v + (1.0 - ADAM_B2) * (g * g)
    m_hat = nm / (1.0 - ADAM_B1 ** ADAM_STEP)
    v_hat = nv / (1.0 - ADAM_B2 ** ADAM_STEP)
    return -ADAM_LR * (m_hat / (jnp.sqrt(v_hat) + ADAM_EPS) + ADAM_WD * w), nm, nv


def _sum_slots(p_ref):
    tot = p_ref[0].astype(F32)
    for d in range(1, N_DEV):
        tot = tot + p_ref[d].astype(F32)
    return tot


def _adamw_parts(parts, w, m, v, layer, results, name):
    n_layers, n_rows, n_cols = w.shape
    tr = n_rows // 2 if n_rows * n_cols > 512 * 1024 else n_rows

    def body(*refs):
        p_ref, w_ref, m_ref, v_ref = refs[:4]
        g_ref, d_ref, nm_ref, nv_ref = refs[-4:]
        g = _sum_slots(p_ref)
        g_ref[0] = g
        d_ref[0], nm_ref[0], nv_ref[0] = _adam_update(w_ref[0], g, m_ref[0], v_ref[0])

    spec = pl.BlockSpec((1, tr, n_cols), lambda i: (layer, i, 0))
    shape = jax.ShapeDtypeStruct((n_layers, n_rows, n_cols), F32)
    kept = [] if results is None else list(results)
    return pl.pallas_call(
        body, name="adamw_" + name, grid=(n_rows // tr,), out_shape=(shape,) * 4,
        in_specs=[pl.BlockSpec((N_DEV, tr, n_cols), lambda i: (0, i, 0)), spec, spec, spec] + [ANY] * len(kept),
        out_specs=(spec,) * 4, input_output_aliases={4 + j: j for j in range(len(kept))},
        compiler_params=_cparams(1),
    )(parts, w, m, v, *kept)


def _adamw(w, g, m, v, name):
    n_rows, n_cols = w.shape
    tr = 512 if n_rows % 512 == 0 else n_rows

    def body(w_ref, g_ref, m_ref, v_ref, d_ref, nm_ref, nv_ref):
        d_ref[...], nm_ref[...], nv_ref[...] = _adam_update(w_ref[...], g_ref[...], m_ref[...], v_ref[...])

    spec = pl.BlockSpec((tr, n_cols), lambda i: (i, 0))
    shape = jax.ShapeDtypeStruct((n_rows, n_cols), F32)
    return pl.pallas_call(
        body, name="adamw_" + name, grid=(n_rows // tr,), out_shape=(shape, shape, shape),
        in_specs=[spec] * 4, out_specs=(spec, spec, spec), compiler_params=_cparams(1),
    )(w, g, m, v)


_SMALL = ("pre_norm_g", "ln_v_g", "ln_v_b", "b_spatial", "sinks", "rel_bias", "post_norm_g")


def _pack(pieces):
    blocks, first_rows, n = [], [], 0
    for p in pieces:
        flat = p.reshape(-1)
        flat = jnp.concatenate([flat, jnp.zeros(((-flat.shape[0]) % (8 * 128),), F32)]).reshape(-1, 128)
        blocks.append(flat)
        first_rows.append(n)
        n += flat.shape[0]
    return jnp.concatenate(blocks, axis=0), first_rows


def _unpack(block, first_row, shape):
    size = math.prod(shape)
    return block[first_row:first_row + -(-size // 128)].reshape(-1)[:size].reshape(shape)


def kernel(x, pre_norm_g, w_in, ln_v_g, ln_v_b, w_spatial, b_spatial, sinks, rel_bias, w_out, post_norm_g, loss_target, m_pre_norm_g, m_w_in, m_ln_v_g, m_ln_v_b, m_w_spatial, m_b_spatial, m_sinks, m_rel_bias, m_w_out, m_post_norm_g, v_pre_norm_g, v_w_in, v_ln_v_g, v_ln_v_b, v_w_spatial, v_b_spatial, v_sinks, v_rel_bias, v_w_out, v_post_norm_g):
    weights = dict(pre_norm_g=pre_norm_g, w_in=w_in, ln_v_g=ln_v_g, ln_v_b=ln_v_b, w_spatial=w_spatial, b_spatial=b_spatial,
                   sinks=sinks, rel_bias=rel_bias, w_out=w_out, post_norm_g=post_norm_g)
    mom_m = dict(pre_norm_g=m_pre_norm_g, w_in=m_w_in, ln_v_g=m_ln_v_g, ln_v_b=m_ln_v_b, w_spatial=m_w_spatial,
                 b_spatial=m_b_spatial, sinks=m_sinks, rel_bias=m_rel_bias, w_out=m_w_out, post_norm_g=m_post_norm_g)
    mom_v = dict(pre_norm_g=v_pre_norm_g, w_in=v_w_in, ln_v_g=v_ln_v_g, ln_v_b=v_ln_v_b, w_spatial=v_w_spatial,
                 b_spatial=v_b_spatial, sinks=v_sinks, rel_bias=v_rel_bias, w_out=v_w_out, post_norm_g=v_post_norm_g)
    n_seq, seq_len, _ = x.shape
    n_layers = w_in.shape[0]
    x2 = x.reshape(n_seq * seq_len, D_MODEL)
    target2 = loss_target.reshape(n_seq * seq_len, D_MODEL)
    row = lambda p, l: p[l][None]

    wt_shards = jnp.swapaxes(w_in, 1, 2).astype(BF16)
    wo_shards = w_out.astype(BF16)
    bucket = _window_buckets()
    bias = _bias_table(rel_bias, bucket)
    wcat, wtcat, bs = _spatial_tables(w_spatial, jnp.swapaxes(b_spatial, 1, 2))

    wt, wo = [None] * n_layers, [None] * n_layers
    wt[0], wo[0] = _gather_weights(wt_shards, wo_shards, 0)
    xs, saved = [x2], []
    for l in range(n_layers):
        a, q, kv, bz = _fwd_in(xs[-1], row(pre_norm_g, l), wt[l])
        mix_args = (xs[-1], a, q, kv, bz, row(ln_v_g, l), row(ln_v_b, l), wcat[l], bs[l], sinks[l], bias, wo[l],
                    row(post_norm_g, l), seq_len)
        if l + 1 < n_layers:
            xn, y, wt[l + 1], wo[l + 1] = _fwd_mix(*mix_args, gather=(wt_shards, wo_shards, l + 1))
            xs.append(xn)
        else:
            dx, y, loss = _fwd_mix(*mix_args, target=target2)
        saved.append((a, q, kv, bz, y))

    small = [None] * n_layers
    parts = [None] * n_layers
    waiting = None
    for l in reversed(range(n_layers)):
        a, q, kv, bz, y = saved[l]
        outs = _bwd_mix(dx, y, a, q, kv, bz, row(ln_v_g, l), row(ln_v_b, l), wcat[l], wtcat[l], bs[l], sinks[l], bias,
                        wo[l], row(post_norm_g, l), seq_len, exchange=None if waiting is None else waiting[:2])
        dproj, dwo, dws, dmix, dlng, dlnb, dgpost, dsink, dbias = outs[:9]
        exchanged = outs[9:]
        dws = dws.reshape(HEADS * CHUNK, CHUNK)
        if l > 0:
            dx, dwt, dgpre, ws_all = _bwd_in(dproj, xs[l], dx, row(pre_norm_g, l), wt[l], allgather=dws)
        else:
            dx, dwt, dgpre = _bwd_in(dproj, xs[l], dx, row(pre_norm_g, l), wt[l])
            ws_all = None
        waiting = (dwt, dwo, dws)
        parts[l] = [None, None, ws_all]
        small[l] = dict(pre_norm_g=dgpre[0], ln_v_g=dlng[0], ln_v_b=dlnb[0], dmix=dmix, sinks=dsink, dbias=dbias,
                        post_norm_g=dgpost[0])
        if l + 1 < n_layers:
            parts[l + 1][0:2] = exchanged
    parts[0][0:3] = _exchange_grads(*waiting)

    db = _b_spatial_grad(jnp.stack([s["dmix"] for s in small]))
    pieces = {n: jnp.stack([s[n] for s in small]) for n in ("pre_norm_g", "ln_v_g", "ln_v_b", "sinks", "post_norm_g")}
    pieces["b_spatial"] = jnp.swapaxes(db[:, :, :HEADS], 1, 2)
    pieces["rel_bias"] = _rel_bias_grad(jnp.stack([s["dbias"] for s in small]), bucket)
    packed, first_rows = _pack([pieces[n] for n in _SMALL] + [loss])
    total = _allreduce_small(packed)
    grad = {n: _unpack(total, r, weights[n].shape) for n, r in zip(_SMALL, first_rows)}
    loss_out = total[first_rows[-1], 0]

    delta, new_m, new_v = {}, {}, {}
    for n in _SMALL:
        w = weights[n]
        two_d = (-1, w.shape[-1])
        d, nm, nv = _adamw(w.reshape(two_d), grad[n].reshape(two_d), mom_m[n].reshape(two_d), mom_v[n].reshape(two_d), n)
        delta[n], new_m[n], new_v[n] = d.reshape(w.shape), nm.reshape(w.shape), nv.reshape(w.shape)

    t3 = lambda p: jnp.swapaxes(p, 1, 2)
    flat3 = lambda p: p.reshape(n_layers, HEADS * CHUNK, CHUNK)
    for n, which, view, back in (("w_in", 0, t3, t3), ("w_out", 1, lambda p: p, lambda p: p),
                                 ("w_spatial", 2, flat3, lambda p: p.reshape(w_spatial.shape))):
        results = None
        for l in reversed(range(n_layers)):
            results = _adamw_parts(parts[l][which], view(weights[n]), view(mom_m[n]), view(mom_v[n]), l, results, n)
        grad[n], delta[n], new_m[n], new_v[n] = (back(r) for r in results)

    names = tuple(weights)
    return (loss_out, dx.reshape(x.shape), *[grad[n] for n in names], *[delta[n] for n in names],
            *[new_m[n] for n in names], *[new_v[n] for n in names])
```

```python
import math

import jax
import jax.numpy as jnp
from jax import lax
from jax.experimental import pallas as pl
from jax.experimental.pallas import tpu as pltpu

F32 = jnp.float32
BF16 = jnp.bfloat16

D_MODEL = 1024
A_WIDTH = 512
B_WIDTH = 512
KV_WIDTH = 128
IN_WIDTH = 3 * A_WIDTH + 2 * B_WIDTH + 2 * KV_WIDTH
CHUNK = 128
HEADS = 8
HEAD_DIM = 64
REL_BUCKETS = 32
NORM_EPS = 1e-6
NEG = -1e30
Q_OFF = 3 * A_WIDTH
K_OFF = Q_OFF + B_WIDTH
BZ_OFF = K_OFF + 2 * KV_WIDTH
QK_SCALE = HEAD_DIM ** -0.5

ADAM_LR = 0.001
ADAM_B1 = 0.9
ADAM_B2 = 0.999
ADAM_EPS = 1e-08
ADAM_WD = 0.01
ADAM_STEP = 10

TOK_TILE = 512
CHUNK_UNROLL = 4
BWD_CHUNK_UNROLL = 4
VMEM_LIMIT_V7X = 60 * 1024 * 1024

N_DEV = 8
N_CHIPS = 4
MESH_ID = pl.DeviceIdType.MESH
ANY = pl.BlockSpec(memory_space=pl.ANY)
SMEM = pl.BlockSpec(memory_space=pltpu.SMEM)


def _cparams(n_axes):
    return pltpu.CompilerParams(dimension_semantics=("arbitrary",) * n_axes, vmem_limit_bytes=VMEM_LIMIT_V7X)


_GELU_C = math.sqrt(2.0 / math.pi)
_GELU_C3 = _GELU_C * 0.044715


def _gelu(x):
    t = jnp.tanh(x * (_GELU_C3 * (x * x) + _GELU_C))
    return x * (0.5 * t + 0.5)


def _gelu_and_grad(x):
    x2 = x * x
    t = jnp.tanh(x * (_GELU_C3 * x2 + _GELU_C))
    cdf = 0.5 * t + 0.5
    d = cdf + (x * (cdf * (1.0 - cdf))) * ((6.0 * _GELU_C3) * x2 + 2.0 * _GELU_C)
    return x * cdf, d


def _sigmoid(x):
    return 0.5 + 0.5 * jnp.tanh(0.5 * x)


def _silu(x):
    return x * _sigmoid(x)


def _silu_and_grad(x):
    s = _sigmoid(x)
    return x * s, s * (1.0 + x * (1.0 - s))


def _dot(a, b):
    return jnp.dot(a, b, preferred_element_type=F32)


def _dot_nt(a, b):
    return lax.dot_general(a, b, (((1,), (1,)), ((), ())), preferred_element_type=F32)


def _dot_tn(a, b):
    return lax.dot_general(a, b, (((0,), (0,)), ((), ())), preferred_element_type=F32)


def _lo_mask(shape):
    return lax.broadcasted_iota(jnp.int32, shape, 1) < HEAD_DIM


def _swap_halves(v):
    return pltpu.roll(v, HEAD_DIM, 1)


def _window_buckets():
    q_loc = jnp.arange(CHUNK)[:, None]
    j_loc = jnp.arange(CHUNK)[None, :]
    d = q_loc - j_loc + jnp.where(j_loc > q_loc, CHUNK, 0)
    max_exact = REL_BUCKETS // 2
    safe = jnp.maximum(d, 1).astype(F32)
    large = max_exact + (jnp.log(safe / max_exact) / math.log(CHUNK / max_exact)
                         * (REL_BUCKETS - max_exact)).astype(jnp.int32)
    large = jnp.minimum(large, REL_BUCKETS - 1)
    return jnp.where(d < max_exact, d, large).astype(jnp.int32)


def _bias_table(rel_bias, bucket):
    def body(rb_ref, bk_ref, out_ref):
        bk = bk_ref[...]
        for h in range(HEADS):
            acc = jnp.zeros(bk.shape, F32)
            for b in range(REL_BUCKETS):
                acc = jnp.where(bk == b, rb_ref[b, h], acc)
            out_ref[h] = acc

    vmem = pl.BlockSpec(memory_space=pltpu.VMEM)
    return pl.pallas_call(
        body, name="bias_table",
        out_shape=jax.ShapeDtypeStruct((HEADS, CHUNK, CHUNK), F32),
        in_specs=[SMEM, vmem], out_specs=vmem,
    )(rel_bias, bucket)


def _rel_bias_grad(dbias, bucket):
    n_layers = dbias.shape[0]

    def body(db_ref, bk_ref, out_ref):
        bk = bk_ref[...]
        for h in range(HEADS):
            tot = db_ref[0, h]
            for l in range(1, n_layers):
                tot = tot + db_ref[l, h]
            for b in range(REL_BUCKETS):
                out_ref[b, h] = jnp.sum(jnp.where(bk == b, tot, 0.0))

    vmem = pl.BlockSpec(memory_space=pltpu.VMEM)
    return pl.pallas_call(
        body, name="rel_bias_grad",
        out_shape=jax.ShapeDtypeStruct((REL_BUCKETS, HEADS), F32),
        in_specs=[vmem] * 2, out_specs=SMEM,
    )(dbias, bucket)


def _spatial_tables(w_spatial, b_spatial_t):
    n_layers = w_spatial.shape[0]

    def body(w_ref, b_ref, wcat_ref, wtcat_ref, bs_ref):
        row = lax.broadcasted_iota(jnp.int32, (CHUNK, CHUNK), 0)
        col = lax.broadcasted_iota(jnp.int32, (CHUNK, CHUNK), 1)
        causal = col <= row
        lo = _lo_mask((CHUNK, CHUNK))
        for p in range(4):
            for half in range(2):
                w = jnp.where(causal, w_ref[0, 2 * p + half], 0.0)
                wcat_ref[0, p, :, half * CHUNK:(half + 1) * CHUNK] = w.astype(BF16)
                wtcat_ref[0, p, :, half * CHUNK:(half + 1) * CHUNK] = w.T.astype(BF16)
            b = b_ref[0]
            bs_ref[0, p] = jnp.where(lo, b[:, 2 * p:2 * p + 1], b[:, 2 * p + 1:2 * p + 2])

    return pl.pallas_call(
        body, name="spatial_tables", grid=(n_layers,),
        out_shape=(jax.ShapeDtypeStruct((n_layers, 4, CHUNK, 2 * CHUNK), BF16),
                   jax.ShapeDtypeStruct((n_layers, 4, CHUNK, 2 * CHUNK), BF16),
                   jax.ShapeDtypeStruct((n_layers, 4, CHUNK, CHUNK), F32)),
        in_specs=[pl.BlockSpec((1, HEADS, CHUNK, CHUNK), lambda l: (l, 0, 0, 0)),
                  pl.BlockSpec((1, CHUNK, HEADS), lambda l: (l, 0, 0))],
        out_specs=(pl.BlockSpec((1, 4, CHUNK, 2 * CHUNK), lambda l: (l, 0, 0, 0)),
                   pl.BlockSpec((1, 4, CHUNK, 2 * CHUNK), lambda l: (l, 0, 0, 0)),
                   pl.BlockSpec((1, 4, CHUNK, CHUNK), lambda l: (l, 0, 0, 0))),
        compiler_params=_cparams(1),
    )(w_spatial, b_spatial_t)


def _b_spatial_grad(dmix):
    n_layers = dmix.shape[0]

    def body(d_ref, out_ref):
        lane = lax.broadcasted_iota(jnp.int32, (CHUNK, CHUNK), 1)
        acc = jnp.zeros((CHUNK, CHUNK), F32)
        for p in range(4):
            t = d_ref[0, :, p * CHUNK:(p + 1) * CHUNK]
            s_lo = jnp.sum(jnp.where(lane < HEAD_DIM, t, 0.0), axis=1, keepdims=True)
            s_hi = jnp.sum(jnp.where(lane < HEAD_DIM, 0.0, t), axis=1, keepdims=True)
            acc = jnp.where(lane == 2 * p, s_lo, acc)
            acc = jnp.where(lane == 2 * p + 1, s_hi, acc)
        out_ref[0] = acc

    return pl.pallas_call(
        body, name="b_spatial_grad", grid=(n_layers,),
        out_shape=jax.ShapeDtypeStruct((n_layers, CHUNK, CHUNK), F32),
        in_specs=[pl.BlockSpec((1, CHUNK, A_WIDTH), lambda l: (l, 0, 0))],
        out_specs=pl.BlockSpec((1, CHUNK, CHUNK), lambda l: (l, 0, 0)),
        compiler_params=_cparams(1),
    )(dmix)


def _place():
    x, y, c = lax.axis_index("x"), lax.axis_index("y"), lax.axis_index("c")
    other_chips = [(1 - x, y), (x, 1 - y), (1 - x, 1 - y)]
    return x, y, c, other_chips


N_GATHER_SEMS = 12


def _gather_phases(shards, fulls, send_sems, recv_sems, local_sems):
    x, y, c, chips = _place()
    sibling = (x, y, 1 - c)
    n_arr = len(shards)

    def half_rows(a, chip, half):
        n = shards[a].shape[0]
        start = (2 * chip[0] + chip[1]) * n + half * (n // 2)
        return fulls[a].at[pl.ds(pl.multiple_of(start, 16), n // 2), :]

    def my_half(a):
        n = shards[a].shape[0]
        return shards[a].at[pl.ds(pl.multiple_of(c * (n // 2), 16), n // 2), :]

    def copy(k, a, src, chip, half, to):
        return pltpu.make_async_remote_copy(
            src_ref=src, dst_ref=half_rows(a, chip, half), send_sem=send_sems.at[n_arr * k + a],
            recv_sem=recv_sems.at[n_arr * k + a], device_id=to, device_id_type=MESH_ID)

    def local(a):
        n = shards[a].shape[0]
        mine = fulls[a].at[pl.ds(pl.multiple_of((2 * x + y) * n, 16), n), :]
        return pltpu.make_async_copy(shards[a], mine, local_sems.at[a])

    def first(k, a):
        return copy(k, a, my_half(a), (x, y), c, (chips[k][0], chips[k][1], c))

    def passed(k, a):
        return copy(3 + k, a, half_rows(a, chips[k], c), chips[k], c, sibling)

    def phase_a():
        for a in range(n_arr):
            local(a).start()
        for k in range(3):
            for a in range(n_arr):
                first(k, a).start()

    def phase_b():
        for k in range(3):
            for a in range(n_arr):
                copy(k, a, my_half(a), chips[k], c, sibling).wait_recv()
                passed(k, a).start()

    def phase_c():
        for k in range(3):
            for a in range(n_arr):
                copy(3 + k, a, my_half(a), chips[k], 1 - c, sibling).wait_recv()
        for k in range(3):
            for a in range(n_arr):
                first(k, a).wait_send()
                passed(k, a).wait_send()
        for a in range(n_arr):
            local(a).wait()

    return phase_a, phase_b, phase_c


N_EXCHANGE_SEMS = 14


def _exchange_phases(partials, parts, send_sems, recv_sems, local_sems, by_chip=True):
    x, y, c, chips = _place()
    me, sibling = (x, y, c), (x, y, 1 - c)
    n_arr = len(partials)

    def block(a, chip):
        if not by_chip:
            return partials[a]
        n = partials[a].shape[0] // N_CHIPS
        return partials[a].at[pl.ds(pl.multiple_of((2 * chip[0] + chip[1]) * n, 16), n), :]

    def slot(a, dev):
        return parts[a].at[4 * dev[0] + 2 * dev[1] + dev[2]]

    def copy(k, a, src, origin, to):
        return pltpu.make_async_remote_copy(
            src_ref=src, dst_ref=slot(a, origin), send_sem=send_sems.at[n_arr * k + a],
            recv_sem=recv_sems.at[n_arr * k + a], device_id=to, device_id_type=MESH_ID)

    def local(a):
        return pltpu.make_async_copy(block(a, (x, y)), slot(a, me), local_sems.at[a])

    def first(k, a):
        if k == 0:
            return copy(0, a, block(a, (x, y)), me, sibling)
        chip = chips[k - 1]
        return copy(k, a, block(a, chip), me, (chip[0], chip[1], c))

    def passed(k, a):
        origin = (chips[k][0], chips[k][1], c)
        return copy(4 + k, a, slot(a, origin), origin, sibling)

    def phase_a():
        for a in range(n_arr):
            local(a).start()
        for k in range(4):
            for a in range(n_arr):
                first(k, a).start()

    def phase_b():
        for k in range(3):
            for a in range(n_arr):
                copy(1 + k, a, block(a, (x, y)), (chips[k][0], chips[k][1], c), me).wait_recv()
                passed(k, a).start()

    def phase_c():
        for a in range(n_arr):
            copy(0, a, block(a, (x, y)), sibling, me).wait_recv()
        for k in range(3):
            for a in range(n_arr):
                copy(4 + k, a, block(a, (x, y)), (chips[k][0], chips[k][1], 1 - c), me).wait_recv()
        for k in range(4):
            for a in range(n_arr):
                first(k, a).wait_send()
        for k in range(3):
            for a in range(n_arr):
                passed(k, a).wait_send()
        for a in range(n_arr):
            local(a).wait()

    return phase_a, phase_b, phase_c


def _comm_scratch(n_sems):
    return [pltpu.SemaphoreType.DMA((n_sems,)), pltpu.SemaphoreType.DMA((n_sems,)), pltpu.SemaphoreType.DMA((2,))]


def _gather_weights(wt_shards, wo_shards, layer):
    wt_rows, wo_rows = wt_shards.shape[1], wo_shards.shape[1]

    def body(wt_ref, wo_ref, wt_full, wo_full, send_sems, recv_sems, local_sems):
        phases = _gather_phases((wt_ref.at[layer], wo_ref.at[layer]), (wt_full, wo_full), send_sems, recv_sems, local_sems)
        for phase in phases:
            phase()

    return pl.pallas_call(
        body, name="gather_weights",
        out_shape=(jax.ShapeDtypeStruct((N_CHIPS * wt_rows, D_MODEL), BF16),
                   jax.ShapeDtypeStruct((N_CHIPS * wo_rows, D_MODEL), BF16)),
        in_specs=[ANY, ANY], out_specs=(ANY, ANY), scratch_shapes=_comm_scratch(N_GATHER_SEMS),
    )(wt_shards, wo_shards)


def _exchange_grads(dwt, dwo, small):
    def body(dwt_ref, dwo_ref, small_ref, pt_ref, po_ref, all_ref, *sems):
        blocks = _exchange_phases((dwt_ref, dwo_ref), (pt_ref, po_ref), *sems[:3])
        whole = _exchange_phases((small_ref,), (all_ref,), *sems[3:], by_chip=False)
        for phase_of_blocks, phase_of_whole in zip(blocks, whole):
            phase_of_blocks()
            phase_of_whole()

    return pl.pallas_call(
        body, name="exchange_grads",
        out_shape=(jax.ShapeDtypeStruct((N_DEV, dwt.shape[0] // N_CHIPS, D_MODEL), BF16),
                   jax.ShapeDtypeStruct((N_DEV, dwo.shape[0] // N_CHIPS, D_MODEL), BF16),
                   jax.ShapeDtypeStruct((N_DEV,) + small.shape, small.dtype)),
        in_specs=[ANY, ANY, ANY], out_specs=(ANY, ANY, ANY),
        scratch_shapes=_comm_scratch(N_EXCHANGE_SEMS) + _comm_scratch(N_EXCHANGE_SEMS),
    )(dwt, dwo, small)


def _allreduce_small(part):
    n_rows = part.shape[0]

    def body(p_ref, tot_ref, all_ref, send_sems, recv_sems, local_sem):
        x, y, c, chips = _place()
        me, sibling = (x, y, c), (x, y, 1 - c)

        def rows(dev):
            return all_ref.at[pl.ds(pl.multiple_of((4 * dev[0] + 2 * dev[1] + dev[2]) * n_rows, 8), n_rows), :]

        def copy(k, origin, to, src=None):
            return pltpu.make_async_remote_copy(
                src_ref=rows(origin) if src is None else src, dst_ref=rows(origin), send_sem=send_sems.at[k],
                recv_sem=recv_sems.at[k], device_id=to, device_id_type=MESH_ID)

        mine = pltpu.make_async_copy(p_ref, rows(me), local_sem)
        mine.start()
        first = [copy(0, me, sibling, src=p_ref)]
        first += [copy(1 + k, me, (chip[0], chip[1], c), src=p_ref) for k, chip in enumerate(chips)]
        for cp in first:
            cp.start()
        passed = []
        for k, chip in enumerate(chips):
            origin = (chip[0], chip[1], c)
            copy(1 + k, origin, me).wait_recv()
            fwd = copy(4 + k, origin, sibling)
            fwd.start()
            passed.append(fwd)
        copy(0, sibling, me).wait_recv()
        for k, chip in enumerate(chips):
            copy(4 + k, (chip[0], chip[1], 1 - c), me).wait_recv()
        for cp in first + passed:
            cp.wait_send()
        mine.wait()
        tot = all_ref[0:n_rows, :]
        for d in range(1, N_DEV):
            tot = tot + all_ref[d * n_rows:(d + 1) * n_rows, :]
        tot_ref[...] = tot

    vmem = pl.BlockSpec(memory_space=pltpu.VMEM)
    return pl.pallas_call(
        body, name="allreduce_small",
        out_shape=jax.ShapeDtypeStruct((n_rows, 128), F32),
        in_specs=[vmem], out_specs=vmem,
        scratch_shapes=[pltpu.VMEM((N_DEV * n_rows, 128), F32), pltpu.SemaphoreType.DMA((7,)),
                        pltpu.SemaphoreType.DMA((7,)), pltpu.SemaphoreType.DMA],
        compiler_params=pltpu.CompilerParams(vmem_limit_bytes=VMEM_LIMIT_V7X),
    )(part)


def _hosted(phases, step, n_steps):
    phase_a, phase_b, phase_c = phases

    def at_start():
        pl.when(step == 0)(phase_a)

    def at_end():
        pl.when(step == n_steps // 2)(phase_b)
        pl.when(step == n_steps - 1)(phase_c)

    return at_start, at_end


def _layer_norm_stats(vv):
    mu = jnp.mean(vv, axis=-1, keepdims=True)
    xc = vv - mu
    rs = lax.rsqrt(jnp.mean(xc * xc, axis=-1, keepdims=True) + NORM_EPS)
    return xc * rs, rs


def _blockdiag(v, lo):
    zero = jnp.zeros_like(v)
    return jnp.concatenate([jnp.where(lo, v, zero), jnp.where(lo, zero, v)], axis=0)


def _softmax_sink(s, sink):
    m = jnp.maximum(jnp.max(s, axis=-1, keepdims=True), sink)
    e = jnp.exp(s - m)
    esink = jnp.exp(sink - m)
    den = jnp.sum(e, axis=-1, keepdims=True) + esink
    return e / den, esink / den


def _kv_rows(cur_ref, halo_ref, r0, c):
    prev_in_tile = cur_ref[pl.ds(pl.multiple_of(jnp.maximum(r0 - CHUNK, 0), CHUNK), CHUNK), :]
    prev = jnp.where(c == 0, halo_ref[...], prev_in_tile)
    kv2 = jnp.concatenate([prev, cur_ref[pl.ds(r0, CHUNK), :]], axis=0)
    k2, v2 = kv2[:, 0:KV_WIDTH], kv2[:, KV_WIDTH:2 * KV_WIDTH]
    return (k2, _swap_halves(k2)), (v2, _swap_halves(v2))


def _window_square(over_keys, prev):
    return jnp.where(prev, over_keys[:, 0:CHUNK], over_keys[:, CHUNK:2 * CHUNK])


def _window_keys(square, prev):
    zero = jnp.zeros_like(square)
    return jnp.concatenate([jnp.where(prev, square, zero), jnp.where(prev, zero, square)], axis=1)


def _ahead(shape=(CHUNK, CHUNK)):
    return lax.broadcasted_iota(jnp.int32, shape, 1) - lax.broadcasted_iota(jnp.int32, shape, 0)


def _dead_mask(first, ahead):
    return ahead > jnp.where(first, 0, CHUNK)


def _head_of(p, half):
    return 2 * p + half, int(half != p // 2)


_HEADS_BY_COPY = tuple(tuple((p, half) for p in range(4) for half in range(2) if _head_of(p, half)[1] == sw)
                       for sw in range(2))


def _masked_halves(tile, lo):
    zero = jnp.zeros_like(tile)
    return {0: jnp.where(lo, tile, zero), 1: jnp.where(lo, zero, tile)}


def _attention_probs(q_tiles, ks, bias_ref, sink_ref, dead, prev, lo):
    q_stacks, probs, psinks = [], {}, {}
    qm = {p: _masked_halves(q_tiles[p] * QK_SCALE, lo) for p in range(4)}
    for sw, members in enumerate(_HEADS_BY_COPY):
        q_stack = jnp.concatenate([qm[p][half] for p, half in members], axis=0)
        s_stack = _dot_nt(q_stack, ks[sw])
        q_stacks.append(q_stack)
        for i, (p, half) in enumerate(members):
            head = 2 * p + half
            s = _window_square(s_stack[i * CHUNK:(i + 1) * CHUNK, :], prev) + bias_ref[head]
            s = jnp.where(dead, NEG, s)
            probs[p, half], psinks[p, half] = _softmax_sink(s, sink_ref[head])
    return q_stacks, probs, psinks


def _attention_values(probs, vs, prev):
    outs, p_stacks = {}, []
    for sw, members in enumerate(_HEADS_BY_COPY):
        p_stack = jnp.concatenate([_window_keys(probs[m].astype(BF16), prev) for m in members], axis=0)
        r_stack = _dot(p_stack, vs[sw])
        p_stacks.append(p_stack)
        for i, m in enumerate(members):
            outs[m] = r_stack[i * CHUNK:(i + 1) * CHUNK, :]
    return outs, p_stacks


def _fwd_layer(x, g_pre, w_in_t, ln_g, ln_b, wcat, bs, sinks, bias, w_out, g_post, seq_len, gather=None, target=None):
    n_tok = x.shape[0]
    tb = TOK_TILE
    n_chunks = tb // CHUNK
    n_tiles = n_tok // tb
    n_in = 11
    assert gather is None or target is None

    def body(*refs):
        (x_ref, gpre_ref, wt_ref, lng_ref, lnb_ref, wcat_ref, bs_ref, sink_ref, bias_ref, wout_ref,
         gpost_ref) = refs[:n_in]
        at_start = at_end = lambda: None
        if gather is not None:
            (wts_ref, wos_ref, a_ref, q_ref, kv_ref, bz_ref, xn_ref, y_ref, wt_full, wo_full, ycat_s, halo_ref, send_sems,
             recv_sems, local_sems) = refs[n_in:]
            phases = _gather_phases((wts_ref.at[gather[2]], wos_ref.at[gather[2]]), (wt_full, wo_full), send_sems,
                                    recv_sems, local_sems)
            at_start, at_end = _hosted(phases, pl.program_id(0), n_tiles)
        elif target is not None:
            target_ref, a_ref, q_ref, kv_ref, bz_ref, xn_ref, y_ref, loss_ref, ycat_s, halo_ref = refs[n_in:]
        else:
            a_ref, q_ref, kv_ref, bz_ref, xn_ref, y_ref, ycat_s, halo_ref = refs[n_in:]
        at_start()
        i = pl.program_id(0)
        lo = _lo_mask((CHUNK, CHUNK))
        ahead = _ahead()
        prev = ahead > 0

        @pl.when(i == 0)
        def _():
            halo_ref[...] = jnp.zeros_like(halo_ref)

        xf = x_ref[...]
        r1 = lax.rsqrt(jnp.mean(xf * xf, axis=-1, keepdims=True) + NORM_EPS)
        h = ((xf * r1) * gpre_ref[...]).astype(BF16)
        a_ref[...] = _dot_nt(h, wt_ref[0:Q_OFF, :])
        q_ref[...] = _dot_nt(h, wt_ref[Q_OFF:K_OFF, :]).astype(BF16)
        kv_ref[...] = _dot_nt(h, wt_ref[K_OFF:BZ_OFF, :]).astype(BF16)
        bz_ref[...] = _dot_nt(h, wt_ref[BZ_OFF:IN_WIDTH, :])

        def chunk(c, carry):
            r0 = pl.multiple_of(c * CHUNK, CHUNK)
            rows = pl.ds(r0, CHUNK)
            u = _gelu(a_ref[rows, 0:A_WIDTH])
            vv = _gelu(a_ref[rows, A_WIDTH:2 * A_WIDTH])
            xhat, _ = _layer_norm_stats(vv)
            vnb = (xhat * lng_ref[...] + lnb_ref[...]).astype(BF16)
            for p in range(4):
                blk = slice(p * CHUNK, (p + 1) * CHUNK)
                mixed = _dot(wcat_ref[p], _blockdiag(vnb[:, blk], lo)) + bs_ref[p]
                sz = _silu(a_ref[rows, 2 * A_WIDTH + p * CHUNK:2 * A_WIDTH + (p + 1) * CHUNK])
                ycat_s[rows, blk] = ((u[:, blk] * mixed) * sz).astype(BF16)
            ks, vs = _kv_rows(kv_ref, halo_ref, r0, c)
            dead = _dead_mask(lax.rem(i * tb + r0, seq_len) == 0, ahead)
            q_tiles = [q_ref[rows, p * CHUNK:(p + 1) * CHUNK] for p in range(4)]
            _, probs, _ = _attention_probs(q_tiles, ks, bias_ref, sink_ref, dead, prev, lo)
            outs, _ = _attention_values(probs, vs, prev)
            for p in range(4):
                blk = slice(p * CHUNK, (p + 1) * CHUNK)
                o = jnp.where(lo, outs[p, 0], outs[p, 1])
                ycat_s[rows, B_WIDTH + p * CHUNK:B_WIDTH + (p + 1) * CHUNK] = (o * _silu(bz_ref[rows, blk])).astype(BF16)
            return carry

        lax.fori_loop(0, n_chunks, chunk, 0, unroll=CHUNK_UNROLL)
        halo_ref[...] = kv_ref[tb - CHUNK:tb, :]
        y = _dot(ycat_s[...], wout_ref[...])
        r = lax.rsqrt(jnp.mean(y * y, axis=-1, keepdims=True) + NORM_EPS)
        y_ref[...] = y
        xn = x_ref[...] + (y * r) * gpost_ref[...]
        if target is None:
            xn_ref[...] = xn
        else:
            d = xn - target_ref[...]
            xn_ref[...] = d * (1.0 / D_MODEL)

            @pl.when(i == 0)
            def _():
                loss_ref[0, 0] = 0.0

            loss_ref[0, 0] += 0.5 * jnp.sum(jnp.mean(d * d, axis=-1, keepdims=True))
        at_end()

    tile = lambda w: pl.BlockSpec((tb, w), lambda i: (i, 0))
    whole = lambda shape, **kw: pl.BlockSpec(shape, lambda i: (0,) * len(shape), **kw)
    in_specs = [tile(D_MODEL), whole((1, D_MODEL)), whole((IN_WIDTH, D_MODEL), pipeline_mode=pl.Buffered(1)),
                whole((1, A_WIDTH)), whole((1, A_WIDTH)), whole((4, CHUNK, 2 * CHUNK)), whole((4, CHUNK, CHUNK)), SMEM,
                whole((HEADS, CHUNK, CHUNK)), whole((D_MODEL, D_MODEL)), whole((1, D_MODEL))]
    out_shape = [jax.ShapeDtypeStruct((n_tok, Q_OFF), F32), jax.ShapeDtypeStruct((n_tok, B_WIDTH), BF16),
                 jax.ShapeDtypeStruct((n_tok, 2 * KV_WIDTH), BF16), jax.ShapeDtypeStruct((n_tok, B_WIDTH), F32),
                 jax.ShapeDtypeStruct((n_tok, D_MODEL), F32), jax.ShapeDtypeStruct((n_tok, D_MODEL), F32)]
    out_specs = [tile(Q_OFF), tile(B_WIDTH), tile(2 * KV_WIDTH), tile(B_WIDTH), tile(D_MODEL), tile(D_MODEL)]
    scratch = [pltpu.VMEM((tb, D_MODEL), BF16), pltpu.VMEM((CHUNK, 2 * KV_WIDTH), BF16)]
    args = [x, g_pre, w_in_t, ln_g, ln_b, wcat, bs, sinks, bias, w_out, g_post]
    if gather is not None:
        wt_shards, wo_shards, _ = gather
        in_specs += [ANY, ANY]
        args += [wt_shards, wo_shards]
        out_shape += [jax.ShapeDtypeStruct((N_CHIPS * wt_shards.shape[1], D_MODEL), BF16),
                      jax.ShapeDtypeStruct((N_CHIPS * wo_shards.shape[1], D_MODEL), BF16)]
        out_specs += [ANY, ANY]
        scratch += _comm_scratch(N_GATHER_SEMS)
    if target is not None:
        in_specs.append(tile(D_MODEL))
        args.append(target)
        out_shape.append(jax.ShapeDtypeStruct((1, 1), F32))
        out_specs.append(SMEM)
    name = "fwd_layer" + ("" if gather is None else "_gather") + ("" if target is None else "_loss")
    return pl.pallas_call(
        body, name=name, grid=(n_tiles,),
        out_shape=tuple(out_shape), in_specs=in_specs, out_specs=tuple(out_specs), scratch_shapes=scratch,
        compiler_params=_cparams(1),
    )(*args)


def _bwd_mix(dout, y, a, q, kv, bz, ln_g, ln_b, wcat, wtcat, bs, sinks, bias, w_out, g_post, seq_len, exchange=None):
    n_tok = y.shape[0]
    tb = TOK_TILE
    n_chunks = tb // CHUNK
    n_tiles = n_tok // tb
    n_in, n_out = 16, 9

    def body(*refs):
        (dout_ref, y_ref, a_ref, q_ref, kv_ref, halo_ref, bz_ref, lng_ref, lnb_ref, wcat_ref, wtcat_ref, bs_ref,
         sink_ref, bias_ref, wout_ref, gpost_ref) = refs[:n_in]
        if exchange is None:
            outs, rest = refs[n_in:n_in + n_out], refs[n_in + n_out:]
            at_start = at_end = lambda: None
        else:
            dwt_ref, dwo_ref = refs[n_in:n_in + 2]
            outs = refs[n_in + 2:n_in + 2 + n_out]
            pt_ref, po_ref = refs[n_in + 2 + n_out:n_in + 4 + n_out]
            rest = refs[n_in + 4 + n_out:]
            phases = _exchange_phases((dwt_ref, dwo_ref), (pt_ref, po_ref), *rest[6:])
            at_start, at_end = _hosted(phases, pl.program_id(0), n_tiles)
        dproj_ref, dwout_ref, dws_ref, dmix_ref, dlng_ref, dlnb_ref, dgpost_ref, dsink_ref, dbias_ref = outs
        ycat_s, dy_s, dyc_s, dkv_s, carry_s, dwout_s = rest[:6]
        at_start()
        step = pl.program_id(0)
        tile_idx = n_tiles - 1 - step
        lo = _lo_mask((CHUNK, CHUNK))
        ahead = _ahead()
        prev = ahead > 0

        @pl.when(step == 0)
        def _():
            dwout_s[...] = jnp.zeros_like(dwout_s)
            dws_ref[...] = jnp.zeros_like(dws_ref)
            dmix_ref[...] = jnp.zeros_like(dmix_ref)
            dlng_ref[...] = jnp.zeros_like(dlng_ref)
            dlnb_ref[...] = jnp.zeros_like(dlnb_ref)
            dgpost_ref[...] = jnp.zeros_like(dgpost_ref)
            dbias_ref[...] = jnp.zeros_like(dbias_ref)
            carry_s[...] = jnp.zeros_like(carry_s)
            for h in range(HEADS):
                dsink_ref[h] = 0.0

        yv = y_ref[...]
        dout = dout_ref[...]
        r = lax.rsqrt(jnp.mean(yv * yv, axis=-1, keepdims=True) + NORM_EPS)
        yn = yv * r
        dgpost_ref[...] += jnp.sum(dout * yn, axis=0, keepdims=True)
        dyn = dout * gpost_ref[...]
        dy = r * (dyn - yn * jnp.mean(dyn * yn, axis=-1, keepdims=True))
        dy_s[...] = dy.astype(BF16)
        dyc_s[...] = _dot_nt(dy_s[...], wout_ref[...])
        dkv_s[0:tb, :] = jnp.zeros((tb, 2 * KV_WIDTH), F32)
        dkv_s[tb:tb + CHUNK, :] = carry_s[...]

        def chunk(c, carry):
            r0 = pl.multiple_of(c * CHUNK, CHUNK)
            rows = pl.ds(r0, CHUNK)
            u, gu = _gelu_and_grad(a_ref[rows, 0:A_WIDTH])
            vv, gv = _gelu_and_grad(a_ref[rows, A_WIDTH:2 * A_WIDTH])
            xhat, rs = _layer_norm_stats(vv)
            vnb = (xhat * lng_ref[...] + lnb_ref[...]).astype(BF16)
            d_vn, d_u, d_az = [], [], []
            for p in range(4):
                blk = slice(p * CHUNK, (p + 1) * CHUNK)
                vbd = _blockdiag(vnb[:, blk], lo)
                mixed = _dot(wcat_ref[p], vbd) + bs_ref[p]
                sz, gz = _silu_and_grad(a_ref[rows, 2 * A_WIDTH + p * CHUNK:2 * A_WIDTH + (p + 1) * CHUNK])
                ub = u[:, blk]
                dya = dyc_s[rows, blk]
                um = ub * mixed
                ycat_s[rows, blk] = (um * sz).astype(BF16)
                d_mixed = (dya * ub) * sz
                d_u.append((dya * mixed) * sz)
                d_az.append((dya * um) * gz)
                dmix_ref[:, blk] += d_mixed
                dmbd = _blockdiag(d_mixed.astype(BF16), lo)
                d_vn.append(_dot(wtcat_ref[p], dmbd))
                dws_ref[p] += _dot_nt(dmbd, vnb[:, blk])
            d_vn = jnp.concatenate(d_vn, axis=1)
            dlng_ref[...] += jnp.sum(d_vn * xhat, axis=0, keepdims=True)
            dlnb_ref[...] += jnp.sum(d_vn, axis=0, keepdims=True)
            dxh = d_vn * lng_ref[...]
            d_vv = rs * (dxh - jnp.mean(dxh, axis=-1, keepdims=True)
                         - xhat * jnp.mean(dxh * xhat, axis=-1, keepdims=True))
            dproj_ref[rows, 0:A_WIDTH] = (jnp.concatenate(d_u, axis=1) * gu).astype(BF16)
            dproj_ref[rows, A_WIDTH:2 * A_WIDTH] = (d_vv * gv).astype(BF16)
            dproj_ref[rows, 2 * A_WIDTH:Q_OFF] = jnp.concatenate(d_az, axis=1).astype(BF16)
            ks, vs = _kv_rows(kv_ref, halo_ref, r0, c)
            dead = _dead_mask(lax.rem(tile_idx * tb + r0, seq_len) == 0, ahead)
            q_tiles = [q_ref[rows, p * CHUNK:(p + 1) * CHUNK] for p in range(4)]
            q_stacks, probs, psinks = _attention_probs(q_tiles, ks, bias_ref, sink_ref, dead, prev, lo)
            outs_h, p_stacks = _attention_values(probs, vs, prev)
            dom = {}
            for p in range(4):
                blk = slice(p * CHUNK, (p + 1) * CHUNK)
                sb, gb = _silu_and_grad(bz_ref[rows, blk])
                dyb = dyc_s[rows, B_WIDTH + p * CHUNK:B_WIDTH + (p + 1) * CHUNK]
                o = jnp.where(lo, outs_h[p, 0], outs_h[p, 1])
                ycat_s[rows, B_WIDTH + p * CHUNK:B_WIDTH + (p + 1) * CHUNK] = (o * sb).astype(BF16)
                dproj_ref[rows, BZ_OFF + p * CHUNK:BZ_OFF + (p + 1) * CHUNK] = ((dyb * o) * gb).astype(BF16)
                dom[p] = _masked_halves((dyb * sb).astype(BF16), lo)
            dqs, dk_by_copy, dv_by_copy = {}, [], []
            for sw, members in enumerate(_HEADS_BY_COPY):
                do_stack = jnp.concatenate([dom[p][half] for p, half in members], axis=0)
                dp_stack = _dot_nt(do_stack, vs[sw])
                dsws = []
                for i, (p, half) in enumerate(members):
                    head = 2 * p + half
                    dp = _window_square(dp_stack[i * CHUNK:(i + 1) * CHUNK, :], prev)
                    delta = jnp.sum(probs[p, half] * dp, axis=-1, keepdims=True)
                    ds = probs[p, half] * (dp - delta)
                    dsink_ref[head] += -jnp.sum(psinks[p, half] * delta)
                    dbias_ref[head] += ds
                    dsws.append(_window_keys(ds.astype(BF16), prev))
                ds_stack = jnp.concatenate(dsws, axis=0)
                dq_stack = _dot(ds_stack, ks[sw])
                for i, m in enumerate(members):
                    dqs[m] = dq_stack[i * CHUNK:(i + 1) * CHUNK, :]
                dk_by_copy.append(_dot_tn(ds_stack, q_stacks[sw]))
                dv_by_copy.append(_dot_tn(p_stacks[sw], do_stack))
            for p in range(4):
                dq = jnp.where(lo, dqs[p, 0], dqs[p, 1]) * QK_SCALE
                dproj_ref[rows, Q_OFF + p * CHUNK:Q_OFF + (p + 1) * CHUNK] = dq.astype(BF16)
            both = pl.ds(r0, 2 * CHUNK)
            dkv_s[both, 0:KV_WIDTH] += dk_by_copy[0] + _swap_halves(dk_by_copy[1])
            dkv_s[both, KV_WIDTH:2 * KV_WIDTH] += dv_by_copy[0] + _swap_halves(dv_by_copy[1])
            return carry

        lax.fori_loop(0, n_chunks, chunk, 0, unroll=BWD_CHUNK_UNROLL)
        dwout_s[...] += _dot_tn(ycat_s[...], dy_s[...])
        dproj_ref[:, K_OFF:BZ_OFF] = dkv_s[CHUNK:CHUNK + tb, :].astype(BF16)
        carry_s[...] = dkv_s[0:CHUNK, :]

        @pl.when(step == n_tiles - 1)
        def _():
            row = lax.broadcasted_iota(jnp.int32, (2 * CHUNK, CHUNK), 0)
            col = lax.broadcasted_iota(jnp.int32, (2 * CHUNK, CHUNK), 1)
            causal = col <= jnp.where(row >= CHUNK, row - CHUNK, row)
            for p in range(4):
                dws_ref[p] = jnp.where(causal, dws_ref[p], 0.0)
            dwout_ref[...] = dwout_s[...].astype(BF16)

        at_end()

    tile = lambda w: pl.BlockSpec((tb, w), lambda s: (n_tiles - 1 - s, 0))
    whole = lambda shape: pl.BlockSpec(shape, lambda s: (0,) * len(shape))
    in_specs = [tile(D_MODEL), tile(D_MODEL), tile(Q_OFF), tile(B_WIDTH), tile(2 * KV_WIDTH),
                pl.BlockSpec((CHUNK, 2 * KV_WIDTH), lambda s: (jnp.maximum((n_tiles - 1 - s) * n_chunks - 1, 0), 0)),
                tile(B_WIDTH), whole((1, A_WIDTH)), whole((1, A_WIDTH)), whole((4, CHUNK, 2 * CHUNK)),
                whole((4, CHUNK, 2 * CHUNK)), whole((4, CHUNK, CHUNK)), SMEM, whole((HEADS, CHUNK, CHUNK)),
                whole((D_MODEL, D_MODEL)), whole((1, D_MODEL))]
    out_shape = [jax.ShapeDtypeStruct((n_tok, IN_WIDTH), BF16), jax.ShapeDtypeStruct((D_MODEL, D_MODEL), BF16),
                 jax.ShapeDtypeStruct((4, 2 * CHUNK, CHUNK), F32), jax.ShapeDtypeStruct((CHUNK, A_WIDTH), F32),
                 jax.ShapeDtypeStruct((1, A_WIDTH), F32), jax.ShapeDtypeStruct((1, A_WIDTH), F32),
                 jax.ShapeDtypeStruct((1, D_MODEL), F32), jax.ShapeDtypeStruct((HEADS,), F32),
                 jax.ShapeDtypeStruct((HEADS, CHUNK, CHUNK), F32)]
    out_specs = [tile(IN_WIDTH), whole((D_MODEL, D_MODEL)), whole((4, 2 * CHUNK, CHUNK)), whole((CHUNK, A_WIDTH)),
                 whole((1, A_WIDTH)), whole((1, A_WIDTH)), whole((1, D_MODEL)), SMEM, whole((HEADS, CHUNK, CHUNK))]
    scratch = [pltpu.VMEM((tb, D_MODEL), BF16), pltpu.VMEM((tb, D_MODEL), BF16), pltpu.VMEM((tb, D_MODEL), F32),
               pltpu.VMEM((tb + CHUNK, 2 * KV_WIDTH), F32), pltpu.VMEM((CHUNK, 2 * KV_WIDTH), F32),
               pltpu.VMEM((D_MODEL, D_MODEL), F32)]
    args = [dout, y, a, q, kv, kv, bz, ln_g, ln_b, wcat, wtcat, bs, sinks, bias, w_out, g_post]
    if exchange is not None:
        dwt, dwo = exchange
        in_specs += [ANY, ANY]
        args += [dwt, dwo]
        out_shape += [jax.ShapeDtypeStruct((N_DEV, dwt.shape[0] // N_CHIPS, D_MODEL), BF16),
                      jax.ShapeDtypeStruct((N_DEV, dwo.shape[0] // N_CHIPS, D_MODEL), BF16)]
        out_specs += [ANY, ANY]
        scratch += _comm_scratch(N_EXCHANGE_SEMS)
    return pl.pallas_call(
        body, name="bwd_mix" if exchange is None else "bwd_mix_exchange", grid=(n_tiles,),
        out_shape=tuple(out_shape), in_specs=in_specs, out_specs=tuple(out_specs), scratch_shapes=scratch,
        compiler_params=_cparams(1),
    )(*args)


def _bwd_in(dproj, x, dout, g_pre, w_in_t, allgather=None):
    n_tok = x.shape[0]
    tm = TOK_TILE
    n_tiles = n_tok // tm

    def body(*refs):
        dp_ref, x_ref, dout_ref, g_ref, wt_ref = refs[:5]
        if allgather is None:
            dx_ref, dwt_ref, dg_ref, acc_s = refs[5:]
            at_start = at_end = lambda: None
        else:
            part_ref, dx_ref, dwt_ref, dg_ref, all_ref, acc_s, send_sems, recv_sems, local_sems = refs[5:]
            phases = _exchange_phases((part_ref,), (all_ref,), send_sems, recv_sems, local_sems, by_chip=False)
            at_start, at_end = _hosted(phases, pl.program_id(0), n_tiles)
        at_start()
        i = pl.program_id(0)

        @pl.when(i == 0)
        def _():
            acc_s[...] = jnp.zeros_like(acc_s)
            dg_ref[...] = jnp.zeros_like(dg_ref)

        xf = x_ref[...]
        r = lax.rsqrt(jnp.mean(xf * xf, axis=-1, keepdims=True) + NORM_EPS)
        xn = xf * r
        h = (xn * g_ref[...]).astype(BF16)
        dp = dp_ref[...]
        dh = _dot(dp, wt_ref[...])
        acc_s[...] += _dot_tn(dp, h)
        dg_ref[...] += jnp.sum(dh * xn, axis=0, keepdims=True)
        dhn = dh * g_ref[...]
        dx_ref[...] = dout_ref[...] + r * (dhn - xn * jnp.mean(dhn * xn, axis=-1, keepdims=True))

        @pl.when(i == n_tiles - 1)
        def _():
            dwt_ref[...] = acc_s[...].astype(BF16)

        at_end()

    in_specs = [pl.BlockSpec((tm, IN_WIDTH), lambda i: (i, 0)), pl.BlockSpec((tm, D_MODEL), lambda i: (i, 0)),
                pl.BlockSpec((tm, D_MODEL), lambda i: (i, 0)), pl.BlockSpec((1, D_MODEL), lambda i: (0, 0)),
                pl.BlockSpec((IN_WIDTH, D_MODEL), lambda i: (0, 0), pipeline_mode=pl.Buffered(1))]
    out_shape = [jax.ShapeDtypeStruct((n_tok, D_MODEL), F32), jax.ShapeDtypeStruct((IN_WIDTH, D_MODEL), BF16),
                 jax.ShapeDtypeStruct((1, D_MODEL), F32)]
    out_specs = [pl.BlockSpec((tm, D_MODEL), lambda i: (i, 0)),
                 pl.BlockSpec((IN_WIDTH, D_MODEL), lambda i: (0, 0), pipeline_mode=pl.Buffered(1)),
                 pl.BlockSpec((1, D_MODEL), lambda i: (0, 0))]
    scratch = [pltpu.VMEM((IN_WIDTH, D_MODEL), F32)]
    args = [dproj, x, dout, g_pre, w_in_t]
    if allgather is not None:
        in_specs.append(ANY)
        args.append(allgather)
        out_shape.append(jax.ShapeDtypeStruct((N_DEV,) + allgather.shape, allgather.dtype))
        out_specs.append(ANY)
        scratch += _comm_scratch(N_EXCHANGE_SEMS)
    return pl.pallas_call(
        body, name="bwd_in" if allgather is None else "bwd_in_allgather", grid=(n_tiles,),
        out_shape=tuple(out_shape), in_specs=in_specs, out_specs=tuple(out_specs), scratch_shapes=scratch,
        compiler_params=_cparams(1),
    )(*args)


def _adam_update(w, g, m, v):
    nm = ADAM_B1 * m + (1.0 - ADAM_B1) * g
    nv = ADAM_B2 * v + (1.0 - ADAM_B2) * (g * g)
    m_hat = nm / (1.0 - ADAM_B1 ** ADAM_STEP)
    v_hat = nv / (1.0 - ADAM_B2 ** ADAM_STEP)
    return -ADAM_LR * (m_hat / (jnp.sqrt(v_hat) + ADAM_EPS) + ADAM_WD * w), nm, nv


def _sum_slots(p_ref):
    tot = p_ref[0].astype(F32)
    for d in range(1, N_DEV):
        tot = tot + p_ref[d].astype(F32)
    return tot


def _adamw_parts(parts, w, m, v, layer, results, name):
    n_layers, n_rows, n_cols = w.shape
    tr = n_rows // 2 if n_rows * n_cols > 512 * 1024 else n_rows

    def body(*refs):
        p_ref, w_ref, m_ref, v_ref = refs[:4]
        g_ref, d_ref, nm_ref, nv_ref = refs[-4:]
        g = _sum_slots(p_ref)
        g_ref[0] = g
        d_ref[0], nm_ref[0], nv_ref[0] = _adam_update(w_ref[0], g, m_ref[0], v_ref[0])

    spec = pl.BlockSpec((1, tr, n_cols), lambda i: (layer, i, 0))
    shape = jax.ShapeDtypeStruct((n_layers, n_rows, n_cols), F32)
    kept = [] if results is None else list(results)
    return pl.pallas_call(
        body, name="adamw_" + name, grid=(n_rows // tr,), out_shape=(shape,) * 4,
        in_specs=[pl.BlockSpec((N_DEV, tr, n_cols), lambda i: (0, i, 0)), spec, spec, spec] + [ANY] * len(kept),
        out_specs=(spec,) * 4, input_output_aliases={4 + j: j for j in range(len(kept))},
        compiler_params=_cparams(1),
    )(parts, w, m, v, *kept)


def _adamw(w, g, m, v, name):
    n_rows, n_cols = w.shape
    tr = 512 if n_rows % 512 == 0 else n_rows

    def body(w_ref, g_ref, m_ref, v_ref, d_ref, nm_ref, nv_ref):
        d_ref[...], nm_ref[...], nv_ref[...] = _adam_update(w_ref[...], g_ref[...], m_ref[...], v_ref[...])

    spec = pl.BlockSpec((tr, n_cols), lambda i: (i, 0))
    shape = jax.ShapeDtypeStruct((n_rows, n_cols), F32)
    return pl.pallas_call(
        body, name="adamw_" + name, grid=(n_rows // tr,), out_shape=(shape, shape, shape),
        in_specs=[spec] * 4, out_specs=(spec, spec, spec), compiler_params=_cparams(1),
    )(w, g, m, v)


_SMALL = ("pre_norm_g", "ln_v_g", "ln_v_b", "b_spatial", "sinks", "rel_bias", "post_norm_g")


def _pack(pieces):
    blocks, first_rows, n = [], [], 0
    for p in pieces:
        flat = p.reshape(-1)
        flat = jnp.concatenate([flat, jnp.zeros(((-flat.shape[0]) % (8 * 128),), F32)]).reshape(-1, 128)
        blocks.append(flat)
        first_rows.append(n)
        n += flat.shape[0]
    return jnp.concatenate(blocks, axis=0), first_rows


def _unpack(block, first_row, shape):
    size = math.prod(shape)
    return block[first_row:first_row + -(-size // 128)].reshape(-1)[:size].reshape(shape)


def kernel(x, pre_norm_g, w_in, ln_v_g, ln_v_b, w_spatial, b_spatial, sinks, rel_bias, w_out, post_norm_g, loss_target, m_pre_norm_g, m_w_in, m_ln_v_g, m_ln_v_b, m_w_spatial, m_b_spatial, m_sinks, m_rel_bias, m_w_out, m_post_norm_g, v_pre_norm_g, v_w_in, v_ln_v_g, v_ln_v_b, v_w_spatial, v_b_spatial, v_sinks, v_rel_bias, v_w_out, v_post_norm_g):
    weights = dict(pre_norm_g=pre_norm_g, w_in=w_in, ln_v_g=ln_v_g, ln_v_b=ln_v_b, w_spatial=w_spatial, b_spatial=b_spatial,
                   sinks=sinks, rel_bias=rel_bias, w_out=w_out, post_norm_g=post_norm_g)
    mom_m = dict(pre_norm_g=m_pre_norm_g, w_in=m_w_in, ln_v_g=m_ln_v_g, ln_v_b=m_ln_v_b, w_spatial=m_w_spatial,
                 b_spatial=m_b_spatial, sinks=m_sinks, rel_bias=m_rel_bias, w_out=m_w_out, post_norm_g=m_post_norm_g)
    mom_v = dict(pre_norm_g=v_pre_norm_g, w_in=v_w_in, ln_v_g=v_ln_v_g, ln_v_b=v_ln_v_b, w_spatial=v_w_spatial,
                 b_spatial=v_b_spatial, sinks=v_sinks, rel_bias=v_rel_bias, w_out=v_w_out, post_norm_g=v_post_norm_g)
    n_seq, seq_len, _ = x.shape
    n_layers = w_in.shape[0]
    x2 = x.reshape(n_seq * seq_len, D_MODEL)
    target2 = loss_target.reshape(n_seq * seq_len, D_MODEL)
    row = lambda p, l: p[l][None]

    wt_shards = jnp.swapaxes(w_in, 1, 2).astype(BF16)
    wo_shards = w_out.astype(BF16)
    bucket = _window_buckets()
    bias = _bias_table(rel_bias, bucket)
    wcat, wtcat, bs = _spatial_tables(w_spatial, jnp.swapaxes(b_spatial, 1, 2))

    wt, wo = [None] * n_layers, [None] * n_layers
    wt[0], wo[0] = _gather_weights(wt_shards, wo_shards, 0)
    xs, saved = [x2], []
    for l in range(n_layers):
        layer_args = (xs[-1], row(pre_norm_g, l), wt[l], row(ln_v_g, l), row(ln_v_b, l), wcat[l], bs[l], sinks[l], bias,
                      wo[l], row(post_norm_g, l), seq_len)
        if l + 1 < n_layers:
            a, q, kv, bz, xn, y, wt[l + 1], wo[l + 1] = _fwd_layer(*layer_args, gather=(wt_shards, wo_shards, l + 1))
            xs.append(xn)
        else:
            a, q, kv, bz, dx, y, loss = _fwd_layer(*layer_args, target=target2)
        saved.append((a, q, kv, bz, y))

    small = [None] * n_layers
    parts = [None] * n_layers
    waiting = None
    for l in reversed(range(n_layers)):
        a, q, kv, bz, y = saved[l]
        outs = _bwd_mix(dx, y, a, q, kv, bz, row(ln_v_g, l), row(ln_v_b, l), wcat[l], wtcat[l], bs[l], sinks[l], bias,
                        wo[l], row(post_norm_g, l), seq_len, exchange=None if waiting is None else waiting[:2])
        dproj, dwo, dws, dmix, dlng, dlnb, dgpost, dsink, dbias = outs[:9]
        exchanged = outs[9:]
        dws = dws.reshape(HEADS * CHUNK, CHUNK)
        if l > 0:
            dx, dwt, dgpre, ws_all = _bwd_in(dproj, xs[l], dx, row(pre_norm_g, l), wt[l], allgather=dws)
        else:
            dx, dwt, dgpre = _bwd_in(dproj, xs[l], dx, row(pre_norm_g, l), wt[l])
            ws_all = None
        waiting = (dwt, dwo, dws)
        parts[l] = [None, None, ws_all]
        small[l] = dict(pre_norm_g=dgpre[0], ln_v_g=dlng[0], ln_v_b=dlnb[0], dmix=dmix, sinks=dsink, dbias=dbias,
                        post_norm_g=dgpost[0])
        if l + 1 < n_layers:
            parts[l + 1][0:2] = exchanged
    parts[0][0:3] = _exchange_grads(*waiting)

    db = _b_spatial_grad(jnp.stack([s["dmix"] for s in small]))
    pieces = {n: jnp.stack([s[n] for s in small]) for n in ("pre_norm_g", "ln_v_g", "ln_v_b", "sinks", "post_norm_g")}
    pieces["b_spatial"] = jnp.swapaxes(db[:, :, :HEADS], 1, 2)
    pieces["rel_bias"] = _rel_bias_grad(jnp.stack([s["dbias"] for s in small]), bucket)
    packed, first_rows = _pack([pieces[n] for n in _SMALL] + [loss])
    total = _allreduce_small(packed)
    grad = {n: _unpack(total, r, weights[n].shape) for n, r in zip(_SMALL, first_rows)}
    loss_out = total[first_rows[-1], 0]

    delta, new_m, new_v = {}, {}, {}
    for n in _SMALL:
        w = weights[n]
        two_d = (-1, w.shape[-1])
        d, nm, nv = _adamw(w.reshape(two_d), grad[n].reshape(two_d), mom_m[n].reshape(two_d), mom_v[n].reshape(two_d), n)
        delta[n], new_m[n], new_v[n] = d.reshape(w.shape), nm.reshape(w.shape), nv.reshape(w.shape)

    t3 = lambda p: jnp.swapaxes(p, 1, 2)
    flat3 = lambda p: p.reshape(n_layers, HEADS * CHUNK, CHUNK)
    for n, which, view, back in (("w_in", 0, t3, t3), ("w_out", 1, lambda p: p, lambda p: p),
                                 ("w_spatial", 2, flat3, lambda p: p.reshape(w_spatial.shape))):
        results = None
        for l in reversed(range(n_layers)):
            results = _adamw_parts(parts[l][which], view(weights[n]), view(mom_m[n]), view(mom_v[n]), l, results, n)
        grad[n], delta[n], new_m[n], new_v[n] = (back(r) for r in results)

    names = tuple(weights)
    return (loss_out, dx.reshape(x.shape), *[grad[n] for n in names], *[delta[n] for n in names],
            *[new_m[n] for n in names], *[new_v[n] for n in names])
```

```python
import math

import jax
import jax.numpy as jnp
from jax import lax
from jax.experimental import pallas as pl
from jax.experimental.pallas import tpu as pltpu

F32 = jnp.float32
BF16 = jnp.bfloat16

D_MODEL = 1024
A_WIDTH = 512
B_WIDTH = 512
KV_WIDTH = 128
IN_WIDTH = 3 * A_WIDTH + 2 * B_WIDTH + 2 * KV_WIDTH
CHUNK = 128
HEADS = 8
HEAD_DIM = 64
REL_BUCKETS = 32
NORM_EPS = 1e-6
NEG = -1e30
Q_OFF = 3 * A_WIDTH
K_OFF = Q_OFF + B_WIDTH
BZ_OFF = K_OFF + 2 * KV_WIDTH
QK_SCALE = HEAD_DIM ** -0.5

ADAM_LR = 0.001
ADAM_B1 = 0.9
ADAM_B2 = 0.999
ADAM_EPS = 1e-08
ADAM_WD = 0.01
ADAM_STEP = 10

TOK_TILE = 512
CHUNK_UNROLL = 4
BWD_CHUNK_UNROLL = 4
VMEM_LIMIT_V7X = 60 * 1024 * 1024

N_DEV = 8
N_CHIPS = 4
MESH_ID = pl.DeviceIdType.MESH
ANY = pl.BlockSpec(memory_space=pl.ANY)
SMEM = pl.BlockSpec(memory_space=pltpu.SMEM)


def _cparams(n_axes):
    return pltpu.CompilerParams(dimension_semantics=("arbitrary",) * n_axes, vmem_limit_bytes=VMEM_LIMIT_V7X)


_GELU_C = math.sqrt(2.0 / math.pi)
_GELU_C3 = _GELU_C * 0.044715


def _gelu(x):
    t = jnp.tanh(x * (_GELU_C3 * (x * x) + _GELU_C))
    return x * (0.5 * t + 0.5)


def _gelu_and_grad(x):
    x2 = x * x
    t = jnp.tanh(x * (_GELU_C3 * x2 + _GELU_C))
    cdf = 0.5 * t + 0.5
    d = cdf + (x * (cdf * (1.0 - cdf))) * ((6.0 * _GELU_C3) * x2 + 2.0 * _GELU_C)
    return x * cdf, d


def _sigmoid(x):
    return 0.5 + 0.5 * jnp.tanh(0.5 * x)


def _silu(x):
    return x * _sigmoid(x)


def _silu_and_grad(x):
    s = _sigmoid(x)
    return x * s, s * (1.0 + x * (1.0 - s))


def _dot(a, b):
    return jnp.dot(a, b, preferred_element_type=F32)


def _dot_nt(a, b):
    return lax.dot_general(a, b, (((1,), (1,)), ((), ())), preferred_element_type=F32)


def _dot_tn(a, b):
    return lax.dot_general(a, b, (((0,), (0,)), ((), ())), preferred_element_type=F32)


def _lo_mask(shape):
    return lax.broadcasted_iota(jnp.int32, shape, 1) < HEAD_DIM


def _swap_halves(v):
    return pltpu.roll(v, HEAD_DIM, 1)


def _window_buckets():
    q_loc = jnp.arange(CHUNK)[:, None]
    j_loc = jnp.arange(CHUNK)[None, :]
    d = q_loc - j_loc + jnp.where(j_loc > q_loc, CHUNK, 0)
    max_exact = REL_BUCKETS // 2
    safe = jnp.maximum(d, 1).astype(F32)
    large = max_exact + (jnp.log(safe / max_exact) / math.log(CHUNK / max_exact)
                         * (REL_BUCKETS - max_exact)).astype(jnp.int32)
    large = jnp.minimum(large, REL_BUCKETS - 1)
    return jnp.where(d < max_exact, d, large).astype(jnp.int32)


def _bias_table(rel_bias, bucket):
    def body(rb_ref, bk_ref, out_ref):
        bk = bk_ref[...]
        for h in range(HEADS):
            acc = jnp.zeros(bk.shape, F32)
            for b in range(REL_BUCKETS):
                acc = jnp.where(bk == b, rb_ref[b, h], acc)
            out_ref[h] = acc

    vmem = pl.BlockSpec(memory_space=pltpu.VMEM)
    return pl.pallas_call(
        body, name="bias_table",
        out_shape=jax.ShapeDtypeStruct((HEADS, CHUNK, CHUNK), F32),
        in_specs=[SMEM, vmem], out_specs=vmem,
    )(rel_bias, bucket)


def _rel_bias_grad(dbias, bucket):
    n_layers = dbias.shape[0]

    def body(db_ref, bk_ref, out_ref):
        bk = bk_ref[...]
        for h in range(HEADS):
            tot = db_ref[0, h]
            for l in range(1, n_layers):
                tot = tot + db_ref[l, h]
            for b in range(REL_BUCKETS):
                out_ref[b, h] = jnp.sum(jnp.where(bk == b, tot, 0.0))

    vmem = pl.BlockSpec(memory_space=pltpu.VMEM)
    return pl.pallas_call(
        body, name="rel_bias_grad",
        out_shape=jax.ShapeDtypeStruct((REL_BUCKETS, HEADS), F32),
        in_specs=[vmem] * 2, out_specs=SMEM,
    )(dbias, bucket)


def _spatial_tables(w_spatial, b_spatial_t):
    n_layers = w_spatial.shape[0]

    def body(w_ref, b_ref, wcat_ref, wtcat_ref, bs_ref):
        row = lax.broadcasted_iota(jnp.int32, (CHUNK, CHUNK), 0)
        col = lax.broadcasted_iota(jnp.int32, (CHUNK, CHUNK), 1)
        causal = col <= row
        lo = _lo_mask((CHUNK, CHUNK))
        for p in range(4):
            for half in range(2):
                w = jnp.where(causal, w_ref[0, 2 * p + half], 0.0)
                wcat_ref[0, p, :, half * CHUNK:(half + 1) * CHUNK] = w.astype(BF16)
                wtcat_ref[0, p, :, half * CHUNK:(half + 1) * CHUNK] = w.T.astype(BF16)
            b = b_ref[0]
            bs_ref[0, p] = jnp.where(lo, b[:, 2 * p:2 * p + 1], b[:, 2 * p + 1:2 * p + 2])

    return pl.pallas_call(
        body, name="spatial_tables", grid=(n_layers,),
        out_shape=(jax.ShapeDtypeStruct((n_layers, 4, CHUNK, 2 * CHUNK), BF16),
                   jax.ShapeDtypeStruct((n_layers, 4, CHUNK, 2 * CHUNK), BF16),
                   jax.ShapeDtypeStruct((n_layers, 4, CHUNK, CHUNK), F32)),
        in_specs=[pl.BlockSpec((1, HEADS, CHUNK, CHUNK), lambda l: (l, 0, 0, 0)),
                  pl.BlockSpec((1, CHUNK, HEADS), lambda l: (l, 0, 0))],
        out_specs=(pl.BlockSpec((1, 4, CHUNK, 2 * CHUNK), lambda l: (l, 0, 0, 0)),
                   pl.BlockSpec((1, 4, CHUNK, 2 * CHUNK), lambda l: (l, 0, 0, 0)),
                   pl.BlockSpec((1, 4, CHUNK, CHUNK), lambda l: (l, 0, 0, 0))),
        compiler_params=_cparams(1),
    )(w_spatial, b_spatial_t)


def _b_spatial_grad(dmix):
    n_layers = dmix.shape[0]

    def body(d_ref, out_ref):
        lane = lax.broadcasted_iota(jnp.int32, (CHUNK, CHUNK), 1)
        acc = jnp.zeros((CHUNK, CHUNK), F32)
        for p in range(4):
            t = d_ref[0, :, p * CHUNK:(p + 1) * CHUNK]
            s_lo = jnp.sum(jnp.where(lane < HEAD_DIM, t, 0.0), axis=1, keepdims=True)
            s_hi = jnp.sum(jnp.where(lane < HEAD_DIM, 0.0, t), axis=1, keepdims=True)
            acc = jnp.where(lane == 2 * p, s_lo, acc)
            acc = jnp.where(lane == 2 * p + 1, s_hi, acc)
        out_ref[0] = acc

    return pl.pallas_call(
        body, name="b_spatial_grad", grid=(n_layers,),
        out_shape=jax.ShapeDtypeStruct((n_layers, CHUNK, CHUNK), F32),
        in_specs=[pl.BlockSpec((1, CHUNK, A_WIDTH), lambda l: (l, 0, 0))],
        out_specs=pl.BlockSpec((1, CHUNK, CHUNK), lambda l: (l, 0, 0)),
        compiler_params=_cparams(1),
    )(dmix)


def _place():
    x, y, c = lax.axis_index("x"), lax.axis_index("y"), lax.axis_index("c")
    other_chips = [(1 - x, y), (x, 1 - y), (1 - x, 1 - y)]
    return x, y, c, other_chips


N_GATHER_SEMS = 12


def _gather_phases(shards, fulls, send_sems, recv_sems, local_sems):
    x, y, c, chips = _place()
    sibling = (x, y, 1 - c)
    n_arr = len(shards)

    def half_rows(a, chip, half):
        n = shards[a].shape[0]
        start = (2 * chip[0] + chip[1]) * n + half * (n // 2)
        return fulls[a].at[pl.ds(pl.multiple_of(start, 16), n // 2), :]

    def my_half(a):
        n = shards[a].shape[0]
        return shards[a].at[pl.ds(pl.multiple_of(c * (n // 2), 16), n // 2), :]

    def copy(k, a, src, chip, half, to):
        return pltpu.make_async_remote_copy(
            src_ref=src, dst_ref=half_rows(a, chip, half), send_sem=send_sems.at[n_arr * k + a],
            recv_sem=recv_sems.at[n_arr * k + a], device_id=to, device_id_type=MESH_ID)

    def local(a):
        n = shards[a].shape[0]
        mine = fulls[a].at[pl.ds(pl.multiple_of((2 * x + y) * n, 16), n), :]
        return pltpu.make_async_copy(shards[a], mine, local_sems.at[a])

    def first(k, a):
        return copy(k, a, my_half(a), (x, y), c, (chips[k][0], chips[k][1], c))

    def passed(k, a):
        return copy(3 + k, a, half_rows(a, chips[k], c), chips[k], c, sibling)

    def phase_a():
        for a in range(n_arr):
            local(a).start()
        for k in range(3):
            for a in range(n_arr):
                first(k, a).start()

    def phase_b():
        for k in range(3):
            for a in range(n_arr):
                copy(k, a, my_half(a), chips[k], c, sibling).wait_recv()
                passed(k, a).start()

    def phase_c():
        for k in range(3):
            for a in range(n_arr):
                copy(3 + k, a, my_half(a), chips[k], 1 - c, sibling).wait_recv()
        for k in range(3):
            for a in range(n_arr):
                first(k, a).wait_send()
                passed(k, a).wait_send()
        for a in range(n_arr):
            local(a).wait()

    return phase_a, phase_b, phase_c


N_EXCHANGE_SEMS = 14


def _exchange_phases(partials, parts, send_sems, recv_sems, local_sems, by_chip=True):
    x, y, c, chips = _place()
    me, sibling = (x, y, c), (x, y, 1 - c)
    n_arr = len(partials)

    def block(a, chip):
        if not by_chip:
            return partials[a]
        n = partials[a].shape[0] // N_CHIPS
        return partials[a].at[pl.ds(pl.multiple_of((2 * chip[0] + chip[1]) * n, 16), n), :]

    def slot(a, dev):
        return parts[a].at[4 * dev[0] + 2 * dev[1] + dev[2]]

    def copy(k, a, src, origin, to):
        return pltpu.make_async_remote_copy(
            src_ref=src, dst_ref=slot(a, origin), send_sem=send_sems.at[n_arr * k + a],
            recv_sem=recv_sems.at[n_arr * k + a], device_id=to, device_id_type=MESH_ID)

    def local(a):
        return pltpu.make_async_copy(block(a, (x, y)), slot(a, me), local_sems.at[a])

    def first(k, a):
        if k == 0:
            return copy(0, a, block(a, (x, y)), me, sibling)
        chip = chips[k - 1]
        return copy(k, a, block(a, chip), me, (chip[0], chip[1], c))

    def passed(k, a):
        origin = (chips[k][0], chips[k][1], c)
        return copy(4 + k, a, slot(a, origin), origin, sibling)

    def phase_a():
        for a in range(n_arr):
            local(a).start()
        for k in range(4):
            for a in range(n_arr):
                first(k, a).start()

    def phase_b():
        for k in range(3):
            for a in range(n_arr):
                copy(1 + k, a, block(a, (x, y)), (chips[k][0], chips[k][1], c), me).wait_recv()
                passed(k, a).start()

    def phase_c():
        for a in range(n_arr):
            copy(0, a, block(a, (x, y)), sibling, me).wait_recv()
        for k in range(3):
            for a in range(n_arr):
                copy(4 + k, a, block(a, (x, y)), (chips[k][0], chips[k][1], 1 - c), me).wait_recv()
        for k in range(4):
            for a in range(n_arr):
                first(k, a).wait_send()
        for k in range(3):
            for a in range(n_arr):
                passed(k, a).wait_send()
        for a in range(n_arr):
            local(a).wait()

    return phase_a, phase_b, phase_c


def _comm_scratch(n_sems):
    return [pltpu.SemaphoreType.DMA((n_sems,)), pltpu.SemaphoreType.DMA((n_sems,)), pltpu.SemaphoreType.DMA((2,))]


def _gather_weights(wt_shards, wo_shards, layer):
    wt_rows, wo_rows = wt_shards.shape[1], wo_shards.shape[1]

    def body(wt_ref, wo_ref, wt_full, wo_full, send_sems, recv_sems, local_sems):
        phases = _gather_phases((wt_ref.at[layer], wo_ref.at[layer]), (wt_full, wo_full), send_sems, recv_sems, local_sems)
        for phase in phases:
            phase()

    return pl.pallas_call(
        body, name="gather_weights",
        out_shape=(jax.ShapeDtypeStruct((N_CHIPS * wt_rows, D_MODEL), BF16),
                   jax.ShapeDtypeStruct((N_CHIPS * wo_rows, D_MODEL), BF16)),
        in_specs=[ANY, ANY], out_specs=(ANY, ANY), scratch_shapes=_comm_scratch(N_GATHER_SEMS),
    )(wt_shards, wo_shards)


def _exchange_grads(dwt, dwo, small):
    def body(dwt_ref, dwo_ref, small_ref, pt_ref, po_ref, all_ref, *sems):
        blocks = _exchange_phases((dwt_ref, dwo_ref), (pt_ref, po_ref), *sems[:3])
        whole = _exchange_phases((small_ref,), (all_ref,), *sems[3:], by_chip=False)
        for phase_of_blocks, phase_of_whole in zip(blocks, whole):
            phase_of_blocks()
            phase_of_whole()

    return pl.pallas_call(
        body, name="exchange_grads",
        out_shape=(jax.ShapeDtypeStruct((N_DEV, dwt.shape[0] // N_CHIPS, D_MODEL), BF16),
                   jax.ShapeDtypeStruct((N_DEV, dwo.shape[0] // N_CHIPS, D_MODEL), BF16),
                   jax.ShapeDtypeStruct((N_DEV,) + small.shape, small.dtype)),
        in_specs=[ANY, ANY, ANY], out_specs=(ANY, ANY, ANY),
        scratch_shapes=_comm_scratch(N_EXCHANGE_SEMS) + _comm_scratch(N_EXCHANGE_SEMS),
    )(dwt, dwo, small)


def _allreduce_small(part):
    n_rows = part.shape[0]

    def body(p_ref, tot_ref, all_ref, send_sems, recv_sems, local_sem):
        x, y, c, chips = _place()
        me, sibling = (x, y, c), (x, y, 1 - c)

        def rows(dev):
            return all_ref.at[pl.ds(pl.multiple_of((4 * dev[0] + 2 * dev[1] + dev[2]) * n_rows, 8), n_rows), :]

        def copy(k, origin, to, src=None):
            return pltpu.make_async_remote_copy(
                src_ref=rows(origin) if src is None else src, dst_ref=rows(origin), send_sem=send_sems.at[k],
                recv_sem=recv_sems.at[k], device_id=to, device_id_type=MESH_ID)

        mine = pltpu.make_async_copy(p_ref, rows(me), local_sem)
        mine.start()
        first = [copy(0, me, sibling, src=p_ref)]
        first += [copy(1 + k, me, (chip[0], chip[1], c), src=p_ref) for k, chip in enumerate(chips)]
        for cp in first:
            cp.start()
        passed = []
        for k, chip in enumerate(chips):
            origin = (chip[0], chip[1], c)
            copy(1 + k, origin, me).wait_recv()
            fwd = copy(4 + k, origin, sibling)
            fwd.start()
            passed.append(fwd)
        copy(0, sibling, me).wait_recv()
        for k, chip in enumerate(chips):
            copy(4 + k, (chip[0], chip[1], 1 - c), me).wait_recv()
        for cp in first + passed:
            cp.wait_send()
        mine.wait()
        tot = all_ref[0:n_rows, :]
        for d in range(1, N_DEV):
            tot = tot + all_ref[d * n_rows:(d + 1) * n_rows, :]
        tot_ref[...] = tot

    vmem = pl.BlockSpec(memory_space=pltpu.VMEM)
    return pl.pallas_call(
        body, name="allreduce_small",
        out_shape=jax.ShapeDtypeStruct((n_rows, 128), F32),
        in_specs=[vmem], out_specs=vmem,
        scratch_shapes=[pltpu.VMEM((N_DEV * n_rows, 128), F32), pltpu.SemaphoreType.DMA((7,)),
                        pltpu.SemaphoreType.DMA((7,)), pltpu.SemaphoreType.DMA],
        compiler_params=pltpu.CompilerParams(vmem_limit_bytes=VMEM_LIMIT_V7X),
    )(part)


def _hosted(phases, step, n_steps):
    phase_a, phase_b, phase_c = phases

    def at_start():
        pl.when(step == 0)(phase_a)

    def at_end():
        pl.when(step == n_steps // 2)(phase_b)
        pl.when(step == n_steps - 1)(phase_c)

    return at_start, at_end


def _layer_norm_stats(vv):
    mu = jnp.mean(vv, axis=-1, keepdims=True)
    xc = vv - mu
    rs = lax.rsqrt(jnp.mean(xc * xc, axis=-1, keepdims=True) + NORM_EPS)
    return xc * rs, rs


def _blockdiag(v, lo):
    zero = jnp.zeros_like(v)
    return jnp.concatenate([jnp.where(lo, v, zero), jnp.where(lo, zero, v)], axis=0)


def _softmax_sink(s, sink):
    m = jnp.maximum(jnp.max(s, axis=-1, keepdims=True), sink)
    e = jnp.exp(s - m)
    esink = jnp.exp(sink - m)
    den = jnp.sum(e, axis=-1, keepdims=True) + esink
    return e / den


def _kv_rows(cur_ref, halo_ref, r0, c):
    prev_in_tile = cur_ref[pl.ds(pl.multiple_of(jnp.maximum(r0 - CHUNK, 0), CHUNK), CHUNK), :]
    prev = jnp.where(c == 0, halo_ref[...], prev_in_tile)
    kv2 = jnp.concatenate([prev, cur_ref[pl.ds(r0, CHUNK), :]], axis=0)
    k2, v2 = kv2[:, 0:KV_WIDTH], kv2[:, KV_WIDTH:2 * KV_WIDTH]
    return (k2, _swap_halves(k2)), (v2, _swap_halves(v2))


def _window_square(over_keys, prev):
    return jnp.where(prev, over_keys[:, 0:CHUNK], over_keys[:, CHUNK:2 * CHUNK])


def _window_keys(square, prev):
    zero = jnp.zeros_like(square)
    return jnp.concatenate([jnp.where(prev, square, zero), jnp.where(prev, zero, square)], axis=1)


def _ahead(shape=(CHUNK, CHUNK)):
    return lax.broadcasted_iota(jnp.int32, shape, 1) - lax.broadcasted_iota(jnp.int32, shape, 0)


def _dead_mask(first, ahead):
    return ahead > jnp.where(first, 0, CHUNK)


def _head_of(p, half):
    return 2 * p + half, int(half != p // 2)


_HEADS_BY_COPY = tuple(tuple((p, half) for p in range(4) for half in range(2) if _head_of(p, half)[1] == sw)
                       for sw in range(2))


def _masked_halves(tile, lo):
    zero = jnp.zeros_like(tile)
    return {0: jnp.where(lo, tile, zero), 1: jnp.where(lo, zero, tile)}


def _query_stacks(q_tiles, lo):
    qm = {p: _masked_halves(q_tiles[p] * QK_SCALE, lo) for p in range(4)}
    return [jnp.concatenate([qm[p][half] for p, half in members], axis=0) for members in _HEADS_BY_COPY]


def _attention_probs(q_stacks, ks, bias_ref, sink_ref, dead, prev):
    probs = {}
    for sw, members in enumerate(_HEADS_BY_COPY):
        s_stack = _dot_nt(q_stacks[sw], ks[sw])
        for i, (p, half) in enumerate(members):
            head = 2 * p + half
            s = _window_square(s_stack[i * CHUNK:(i + 1) * CHUNK, :], prev) + bias_ref[head]
            s = jnp.where(dead, NEG, s)
            probs[p, half] = _softmax_sink(s, sink_ref[head]).astype(BF16)
    return probs


def _attention_values(probs, vs, prev):
    outs, p_stacks = {}, []
    for sw, members in enumerate(_HEADS_BY_COPY):
        p_stack = jnp.concatenate([_window_keys(probs[m], prev) for m in members], axis=0)
        r_stack = _dot(p_stack, vs[sw])
        p_stacks.append(p_stack)
        for i, m in enumerate(members):
            outs[m] = r_stack[i * CHUNK:(i + 1) * CHUNK, :]
    return outs, p_stacks


def _fwd_layer(x, g_pre, w_in_t, ln_g, ln_b, wcat, bs, sinks, bias, w_out, g_post, seq_len, gather=None, target=None):
    n_tok = x.shape[0]
    tb = TOK_TILE
    n_chunks = tb // CHUNK
    n_tiles = n_tok // tb
    n_in = 11
    assert gather is None or target is None

    def body(*refs):
        (x_ref, gpre_ref, wt_ref, lng_ref, lnb_ref, wcat_ref, bs_ref, sink_ref, bias_ref, wout_ref,
         gpost_ref) = refs[:n_in]
        at_start = at_end = lambda: None
        if gather is not None:
            (wts_ref, wos_ref, a_ref, q_ref, kv_ref, bz_ref, probs_ref, xn_ref, y_ref, wt_full, wo_full, ycat_s, halo_ref,
             send_sems, recv_sems, local_sems) = refs[n_in:]
            phases = _gather_phases((wts_ref.at[gather[2]], wos_ref.at[gather[2]]), (wt_full, wo_full), send_sems,
                                    recv_sems, local_sems)
            at_start, at_end = _hosted(phases, pl.program_id(0), n_tiles)
        elif target is not None:
            target_ref, a_ref, q_ref, kv_ref, bz_ref, probs_ref, xn_ref, y_ref, loss_ref, ycat_s, halo_ref = refs[n_in:]
        else:
            a_ref, q_ref, kv_ref, bz_ref, probs_ref, xn_ref, y_ref, ycat_s, halo_ref = refs[n_in:]
        at_start()
        i = pl.program_id(0)
        lo = _lo_mask((CHUNK, CHUNK))
        ahead = _ahead()
        prev = ahead > 0

        @pl.when(i == 0)
        def _():
            halo_ref[...] = jnp.zeros_like(halo_ref)

        xf = x_ref[...]
        r1 = lax.rsqrt(jnp.mean(xf * xf, axis=-1, keepdims=True) + NORM_EPS)
        h = ((xf * r1) * gpre_ref[...]).astype(BF16)
        a_ref[...] = _dot_nt(h, wt_ref[0:Q_OFF, :])
        q_ref[...] = _dot_nt(h, wt_ref[Q_OFF:K_OFF, :]).astype(BF16)
        kv_ref[...] = _dot_nt(h, wt_ref[K_OFF:BZ_OFF, :]).astype(BF16)
        bz_ref[...] = _dot_nt(h, wt_ref[BZ_OFF:IN_WIDTH, :])

        def chunk(c, carry):
            r0 = pl.multiple_of(c * CHUNK, CHUNK)
            rows = pl.ds(r0, CHUNK)
            u = _gelu(a_ref[rows, 0:A_WIDTH])
            vv = _gelu(a_ref[rows, A_WIDTH:2 * A_WIDTH])
            xhat, _ = _layer_norm_stats(vv)
            vnb = (xhat * lng_ref[...] + lnb_ref[...]).astype(BF16)
            for p in range(4):
                blk = slice(p * CHUNK, (p + 1) * CHUNK)
                mixed = _dot(wcat_ref[p], _blockdiag(vnb[:, blk], lo)) + bs_ref[p]
                sz = _silu(a_ref[rows, 2 * A_WIDTH + p * CHUNK:2 * A_WIDTH + (p + 1) * CHUNK])
                ycat_s[rows, blk] = ((u[:, blk] * mixed) * sz).astype(BF16)
            ks, vs = _kv_rows(kv_ref, halo_ref, r0, c)
            dead = _dead_mask(lax.rem(i * tb + r0, seq_len) == 0, ahead)
            q_tiles = [q_ref[rows, p * CHUNK:(p + 1) * CHUNK] for p in range(4)]
            probs = _attention_probs(_query_stacks(q_tiles, lo), ks, bias_ref, sink_ref, dead, prev)
            for (p, half), prob in probs.items():
                head = 2 * p + half
                probs_ref[rows, head * CHUNK:(head + 1) * CHUNK] = prob
            outs, _ = _attention_values(probs, vs, prev)
            for p in range(4):
                blk = slice(p * CHUNK, (p + 1) * CHUNK)
                o = jnp.where(lo, outs[p, 0], outs[p, 1])
                ycat_s[rows, B_WIDTH + p * CHUNK:B_WIDTH + (p + 1) * CHUNK] = (o * _silu(bz_ref[rows, blk])).astype(BF16)
            return carry

        lax.fori_loop(0, n_chunks, chunk, 0, unroll=CHUNK_UNROLL)
        halo_ref[...] = kv_ref[tb - CHUNK:tb, :]
        y = _dot(ycat_s[...], wout_ref[...])
        r = lax.rsqrt(jnp.mean(y * y, axis=-1, keepdims=True) + NORM_EPS)
        y_ref[...] = y
        xn = x_ref[...] + (y * r) * gpost_ref[...]
        if target is None:
            xn_ref[...] = xn
        else:
            d = xn - target_ref[...]
            xn_ref[...] = d * (1.0 / D_MODEL)

            @pl.when(i == 0)
            def _():
                loss_ref[0, 0] = 0.0

            loss_ref[0, 0] += 0.5 * jnp.sum(jnp.mean(d * d, axis=-1, keepdims=True))
        at_end()

    tile = lambda w: pl.BlockSpec((tb, w), lambda i: (i, 0))
    whole = lambda shape, **kw: pl.BlockSpec(shape, lambda i: (0,) * len(shape), **kw)
    in_specs = [tile(D_MODEL), whole((1, D_MODEL)), whole((IN_WIDTH, D_MODEL), pipeline_mode=pl.Buffered(1)),
                whole((1, A_WIDTH)), whole((1, A_WIDTH)), whole((4, CHUNK, 2 * CHUNK)), whole((4, CHUNK, CHUNK)), SMEM,
                whole((HEADS, CHUNK, CHUNK)), whole((D_MODEL, D_MODEL)), whole((1, D_MODEL))]
    out_shape = [jax.ShapeDtypeStruct((n_tok, Q_OFF), F32), jax.ShapeDtypeStruct((n_tok, B_WIDTH), BF16),
                 jax.ShapeDtypeStruct((n_tok, 2 * KV_WIDTH), BF16), jax.ShapeDtypeStruct((n_tok, B_WIDTH), F32),
                 jax.ShapeDtypeStruct((n_tok, HEADS * CHUNK), BF16), jax.ShapeDtypeStruct((n_tok, D_MODEL), F32),
                 jax.ShapeDtypeStruct((n_tok, D_MODEL), F32)]
    out_specs = [tile(Q_OFF), tile(B_WIDTH), tile(2 * KV_WIDTH), tile(B_WIDTH), tile(HEADS * CHUNK), tile(D_MODEL),
                 tile(D_MODEL)]
    scratch = [pltpu.VMEM((tb, D_MODEL), BF16), pltpu.VMEM((CHUNK, 2 * KV_WIDTH), BF16)]
    args = [x, g_pre, w_in_t, ln_g, ln_b, wcat, bs, sinks, bias, w_out, g_post]
    if gather is not None:
        wt_shards, wo_shards, _ = gather
        in_specs += [ANY, ANY]
        args += [wt_shards, wo_shards]
        out_shape += [jax.ShapeDtypeStruct((N_CHIPS * wt_shards.shape[1], D_MODEL), BF16),
                      jax.ShapeDtypeStruct((N_CHIPS * wo_shards.shape[1], D_MODEL), BF16)]
        out_specs += [ANY, ANY]
        scratch += _comm_scratch(N_GATHER_SEMS)
    if target is not None:
        in_specs.append(tile(D_MODEL))
        args.append(target)
        out_shape.append(jax.ShapeDtypeStruct((1, 1), F32))
        out_specs.append(SMEM)
    name = "fwd_layer" + ("" if gather is None else "_gather") + ("" if target is None else "_loss")
    return pl.pallas_call(
        body, name=name, grid=(n_tiles,),
        out_shape=tuple(out_shape), in_specs=in_specs, out_specs=tuple(out_specs), scratch_shapes=scratch,
        compiler_params=_cparams(1),
    )(*args)


def _bwd_mix(dout, y, a, q, kv, bz, probs, ln_g, ln_b, wcat, wtcat, bs, w_out, g_post, exchange=None, allgather=None):
    n_tok = y.shape[0]
    tb = TOK_TILE
    n_chunks = tb // CHUNK
    n_tiles = n_tok // tb

    def body(*refs):
        refs = list(refs)
        take = lambda n: [refs.pop(0) for _ in range(n)]
        (dout_ref, y_ref, a_ref, q_ref, kv_ref, halo_ref, bz_ref, probs_ref, lng_ref, lnb_ref, wcat_ref, wtcat_ref, bs_ref,
         wout_ref, gpost_ref) = take(15)
        if exchange is not None:
            dwt_in, dwo_in = take(2)
        if allgather is not None:
            (small_in,) = take(1)
        dproj_ref, dwout_ref, dws_ref, dmix_ref, dlng_ref, dlnb_ref, dgpost_ref, dsink_ref, dbias_ref = take(9)
        if exchange is not None:
            pt_ref, po_ref = take(2)
        if allgather is not None:
            (small_all,) = take(1)
        ycat_s, dy_s, dyc_s, dkv_s, carry_s, dwout_s, stage_s = take(7)
        starts, ends = [], []
        if exchange is not None:
            hosted = _hosted(_exchange_phases((dwt_in, dwo_in), (pt_ref, po_ref), *take(3)), pl.program_id(0), n_tiles)
            starts.append(hosted[0])
            ends.append(hosted[1])
        if allgather is not None:
            hosted = _hosted(_exchange_phases((small_in,), (small_all,), *take(3), by_chip=False), pl.program_id(0),
                             n_tiles)
            starts.append(hosted[0])
            ends.append(hosted[1])
        at_start = lambda: [f() for f in starts]
        at_end = lambda: [f() for f in ends]
        at_start()
        step = pl.program_id(0)
        lo = _lo_mask((CHUNK, CHUNK))
        prev = _ahead() > 0

        @pl.when(step == 0)
        def _():
            dwout_s[...] = jnp.zeros_like(dwout_s)
            dws_ref[...] = jnp.zeros_like(dws_ref)
            dmix_ref[...] = jnp.zeros_like(dmix_ref)
            dlng_ref[...] = jnp.zeros_like(dlng_ref)
            dlnb_ref[...] = jnp.zeros_like(dlnb_ref)
            dgpost_ref[...] = jnp.zeros_like(dgpost_ref)
            dbias_ref[...] = jnp.zeros_like(dbias_ref)
            carry_s[...] = jnp.zeros_like(carry_s)
            for h in range(HEADS):
                dsink_ref[h] = 0.0

        for r0 in range(0, tb, CHUNK):
            rows = slice(r0, r0 + CHUNK)
            yv = y_ref[rows, :]
            dout = dout_ref[rows, :]
            r = lax.rsqrt(jnp.mean(yv * yv, axis=-1, keepdims=True) + NORM_EPS)
            yn = yv * r
            dgpost_ref[...] += jnp.sum(dout * yn, axis=0, keepdims=True)
            dyn = dout * gpost_ref[...]
            dy_s[rows, :] = (r * (dyn - yn * jnp.mean(dyn * yn, axis=-1, keepdims=True))).astype(BF16)
        dyc_s[...] = _dot_nt(dy_s[...], wout_ref[...])
        dkv_s[0:tb, :] = jnp.zeros((tb, 2 * KV_WIDTH), F32)
        dkv_s[tb:tb + CHUNK, :] = carry_s[...]

        def chunk(c, carry):
            r0 = pl.multiple_of(c * CHUNK, CHUNK)
            rows = pl.ds(r0, CHUNK)
            u, gu = _gelu_and_grad(a_ref[rows, 0:A_WIDTH])
            vv, gv = _gelu_and_grad(a_ref[rows, A_WIDTH:2 * A_WIDTH])
            xhat, rs = _layer_norm_stats(vv)
            vnb = (xhat * lng_ref[...] + lnb_ref[...]).astype(BF16)
            d_vn, d_u, d_az = [], [], []
            for p in range(4):
                blk = slice(p * CHUNK, (p + 1) * CHUNK)
                vbd = _blockdiag(vnb[:, blk], lo)
                mixed = _dot(wcat_ref[p], vbd) + bs_ref[p]
                sz, gz = _silu_and_grad(a_ref[rows, 2 * A_WIDTH + p * CHUNK:2 * A_WIDTH + (p + 1) * CHUNK])
                ub = u[:, blk]
                dya = dyc_s[rows, blk]
                um = ub * mixed
                ycat_s[rows, blk] = (um * sz).astype(BF16)
                d_mixed = (dya * ub) * sz
                d_u.append((dya * mixed) * sz)
                d_az.append((dya * um) * gz)
                dmix_ref[:, blk] += d_mixed
                dmbd = _blockdiag(d_mixed.astype(BF16), lo)
                d_vn.append(_dot(wtcat_ref[p], dmbd))
                dws_ref[p] += _dot_nt(dmbd, vnb[:, blk])
            d_vn = jnp.concatenate(d_vn, axis=1)
            dlng_ref[...] += jnp.sum(d_vn * xhat, axis=0, keepdims=True)
            dlnb_ref[...] += jnp.sum(d_vn, axis=0, keepdims=True)
            dxh = d_vn * lng_ref[...]
            d_vv = rs * (dxh - jnp.mean(dxh, axis=-1, keepdims=True)
                         - xhat * jnp.mean(dxh * xhat, axis=-1, keepdims=True))
            dproj_ref[rows, 0:A_WIDTH] = (jnp.concatenate(d_u, axis=1) * gu).astype(BF16)
            dproj_ref[rows, A_WIDTH:2 * A_WIDTH] = (d_vv * gv).astype(BF16)
            dproj_ref[rows, 2 * A_WIDTH:Q_OFF] = jnp.concatenate(d_az, axis=1).astype(BF16)
            ks, vs = _kv_rows(kv_ref, halo_ref, r0, c)
            q_stacks = _query_stacks([q_ref[rows, p * CHUNK:(p + 1) * CHUNK] for p in range(4)], lo)
            probs = {(p, half): probs_ref[rows, (2 * p + half) * CHUNK:(2 * p + half + 1) * CHUNK]
                     for p in range(4) for half in range(2)}
            outs_h, p_stacks = _attention_values(probs, vs, prev)
            dom = {}
            for p in range(4):
                blk = slice(p * CHUNK, (p + 1) * CHUNK)
                sb, gb = _silu_and_grad(bz_ref[rows, blk])
                dyb = dyc_s[rows, B_WIDTH + p * CHUNK:B_WIDTH + (p + 1) * CHUNK]
                o = jnp.where(lo, outs_h[p, 0], outs_h[p, 1])
                ycat_s[rows, B_WIDTH + p * CHUNK:B_WIDTH + (p + 1) * CHUNK] = (o * sb).astype(BF16)
                dproj_ref[rows, BZ_OFF + p * CHUNK:BZ_OFF + (p + 1) * CHUNK] = ((dyb * o) * gb).astype(BF16)
                dom[p] = _masked_halves((dyb * sb).astype(BF16), lo)
            dqs, dk_by_copy, dv_by_copy = {}, [], []
            for sw, members in enumerate(_HEADS_BY_COPY):
                do_stack = jnp.concatenate([dom[p][half] for p, half in members], axis=0)
                dp_stack = _dot_nt(do_stack, vs[sw])
                dsws = []
                for i, (p, half) in enumerate(members):
                    head = 2 * p + half
                    dp = _window_square(dp_stack[i * CHUNK:(i + 1) * CHUNK, :], prev)
                    prob = probs[p, half].astype(F32)
                    delta = jnp.sum(prob * dp, axis=-1, keepdims=True)
                    ds = prob * (dp - delta)
                    dsink_ref[head] += -jnp.sum((1.0 - jnp.sum(prob, axis=-1, keepdims=True)) * delta)
                    dbias_ref[head] += ds
                    dsws.append(_window_keys(ds.astype(BF16), prev))
                ds_stack = jnp.concatenate(dsws, axis=0)
                dq_stack = _dot(ds_stack, ks[sw])
                for i, m in enumerate(members):
                    dqs[m] = dq_stack[i * CHUNK:(i + 1) * CHUNK, :]
                dk_by_copy.append(_dot_tn(ds_stack, q_stacks[sw]))
                dv_by_copy.append(_dot_tn(p_stacks[sw], do_stack))
            for p in range(4):
                dq = jnp.where(lo, dqs[p, 0], dqs[p, 1]) * QK_SCALE
                dproj_ref[rows, Q_OFF + p * CHUNK:Q_OFF + (p + 1) * CHUNK] = dq.astype(BF16)
            both = pl.ds(r0, 2 * CHUNK)
            dkv_s[both, 0:KV_WIDTH] += dk_by_copy[0] + _swap_halves(dk_by_copy[1])
            dkv_s[both, KV_WIDTH:2 * KV_WIDTH] += dv_by_copy[0] + _swap_halves(dv_by_copy[1])
            return carry

        lax.fori_loop(0, n_chunks, chunk, 0, unroll=BWD_CHUNK_UNROLL)
        dwout_s[...] += _dot_tn(ycat_s[...], dy_s[...])
        dproj_ref[:, K_OFF:BZ_OFF] = dkv_s[CHUNK:CHUNK + tb, :].astype(BF16)
        carry_s[...] = dkv_s[0:CHUNK, :]

        @pl.when(step == n_tiles - 1)
        def _():
            row = lax.broadcasted_iota(jnp.int32, (2 * CHUNK, CHUNK), 0)
            col = lax.broadcasted_iota(jnp.int32, (2 * CHUNK, CHUNK), 1)
            causal = col <= jnp.where(row >= CHUNK, row - CHUNK, row)
            for p in range(4):
                dws_ref[p] = jnp.where(causal, dws_ref[p], 0.0)
            rows = stage_s.shape[0]
            for r0 in range(0, D_MODEL, rows):
                stage_s[...] = dwout_s[r0:r0 + rows, :].astype(BF16)
                pltpu.sync_copy(stage_s, dwout_ref.at[r0:r0 + rows, :])

        at_end()

    tile = lambda w: pl.BlockSpec((tb, w), lambda s: (n_tiles - 1 - s, 0))
    whole = lambda shape, **kw: pl.BlockSpec(shape, lambda s: (0,) * len(shape), **kw)
    once = dict(pipeline_mode=pl.Buffered(1))
    in_specs = [tile(D_MODEL), tile(D_MODEL), tile(Q_OFF), tile(B_WIDTH), tile(2 * KV_WIDTH),
                pl.BlockSpec((CHUNK, 2 * KV_WIDTH), lambda s: (jnp.maximum((n_tiles - 1 - s) * n_chunks - 1, 0), 0)),
                tile(B_WIDTH), tile(HEADS * CHUNK), whole((1, A_WIDTH)), whole((1, A_WIDTH)), whole((4, CHUNK, 2 * CHUNK)),
                whole((4, CHUNK, 2 * CHUNK)), whole((4, CHUNK, CHUNK)), whole((D_MODEL, D_MODEL), **once),
                whole((1, D_MODEL))]
    args = [dout, y, a, q, kv, kv, bz, probs, ln_g, ln_b, wcat, wtcat, bs, w_out, g_post]
    small_shapes = [jax.ShapeDtypeStruct((D_MODEL, D_MODEL), BF16), jax.ShapeDtypeStruct((4, 2 * CHUNK, CHUNK), F32),
                    jax.ShapeDtypeStruct((CHUNK, A_WIDTH), F32), jax.ShapeDtypeStruct((1, A_WIDTH), F32),
                    jax.ShapeDtypeStruct((1, A_WIDTH), F32), jax.ShapeDtypeStruct((1, D_MODEL), F32),
                    jax.ShapeDtypeStruct((HEADS,), F32), jax.ShapeDtypeStruct((HEADS, CHUNK, CHUNK), F32)]
    small_specs = [ANY, whole((4, 2 * CHUNK, CHUNK)), whole((CHUNK, A_WIDTH)), whole((1, A_WIDTH)), whole((1, A_WIDTH)),
                   whole((1, D_MODEL)), SMEM, whole((HEADS, CHUNK, CHUNK))]
    scratch = [pltpu.VMEM((tb, D_MODEL), BF16), pltpu.VMEM((tb, D_MODEL), BF16), pltpu.VMEM((tb, D_MODEL), F32),
               pltpu.VMEM((tb + CHUNK, 2 * KV_WIDTH), F32), pltpu.VMEM((CHUNK, 2 * KV_WIDTH), F32),
               pltpu.VMEM((D_MODEL, D_MODEL), F32), pltpu.VMEM((2 * CHUNK, D_MODEL), BF16)]
    out_shape = [jax.ShapeDtypeStruct((n_tok, IN_WIDTH), BF16)] + small_shapes
    out_specs = [tile(IN_WIDTH)] + small_specs
    if exchange is not None:
        dwt, dwo = exchange
        in_specs += [ANY, ANY]
        args += [dwt, dwo]
        out_shape += [jax.ShapeDtypeStruct((N_DEV, dwt.shape[0] // N_CHIPS, D_MODEL), BF16),
                      jax.ShapeDtypeStruct((N_DEV, dwo.shape[0] // N_CHIPS, D_MODEL), BF16)]
        out_specs += [ANY, ANY]
    if allgather is not None:
        in_specs.append(ANY)
        args.append(allgather)
        out_shape.append(jax.ShapeDtypeStruct((N_DEV,) + allgather.shape, allgather.dtype))
        out_specs.append(ANY)
    for hosted in (exchange, allgather):
        if hosted is not None:
            scratch += _comm_scratch(N_EXCHANGE_SEMS)
    return pl.pallas_call(
        body, name="bwd_mix" + ("" if exchange is None and allgather is None else "_exchange"), grid=(n_tiles,),
        out_shape=tuple(out_shape), in_specs=in_specs, out_specs=tuple(out_specs), scratch_shapes=scratch,
        compiler_params=_cparams(1),
    )(*args)


def _bwd_in(dproj, x, dout, g_pre, w_in_t):
    n_tok = x.shape[0]
    tm = TOK_TILE
    n_tiles = n_tok // tm

    def body(dp_ref, x_ref, dout_ref, g_ref, wt_ref, dx_ref, dwt_ref, dg_ref, acc_s):
        i = pl.program_id(0)

        @pl.when(i == 0)
        def _():
            acc_s[...] = jnp.zeros_like(acc_s)
            dg_ref[...] = jnp.zeros_like(dg_ref)

        xf = x_ref[...]
        r = lax.rsqrt(jnp.mean(xf * xf, axis=-1, keepdims=True) + NORM_EPS)
        xn = xf * r
        h = (xn * g_ref[...]).astype(BF16)
        dp = dp_ref[...]
        dh = _dot(dp, wt_ref[...])
        acc_s[...] += _dot_tn(dp, h)
        dg_ref[...] += jnp.sum(dh * xn, axis=0, keepdims=True)
        dhn = dh * g_ref[...]
        dx_ref[...] = dout_ref[...] + r * (dhn - xn * jnp.mean(dhn * xn, axis=-1, keepdims=True))

        @pl.when(i == n_tiles - 1)
        def _():
            dwt_ref[...] = acc_s[...].astype(BF16)

    return pl.pallas_call(
        body, name="bwd_in", grid=(n_tiles,),
        out_shape=(jax.ShapeDtypeStruct((n_tok, D_MODEL), F32), jax.ShapeDtypeStruct((IN_WIDTH, D_MODEL), BF16),
                   jax.ShapeDtypeStruct((1, D_MODEL), F32)),
        in_specs=[pl.BlockSpec((tm, IN_WIDTH), lambda i: (i, 0)), pl.BlockSpec((tm, D_MODEL), lambda i: (i, 0)),
                  pl.BlockSpec((tm, D_MODEL), lambda i: (i, 0)), pl.BlockSpec((1, D_MODEL), lambda i: (0, 0)),
                  pl.BlockSpec((IN_WIDTH, D_MODEL), lambda i: (0, 0), pipeline_mode=pl.Buffered(1))],
        out_specs=(pl.BlockSpec((tm, D_MODEL), lambda i: (i, 0)),
                   pl.BlockSpec((IN_WIDTH, D_MODEL), lambda i: (0, 0), pipeline_mode=pl.Buffered(1)),
                   pl.BlockSpec((1, D_MODEL), lambda i: (0, 0))),
        scratch_shapes=[pltpu.VMEM((IN_WIDTH, D_MODEL), F32)],
        compiler_params=_cparams(1),
    )(dproj, x, dout, g_pre, w_in_t)


def _adam_update(w, g, m, v):
    nm = ADAM_B1 * m + (1.0 - ADAM_B1) * g
    nv = ADAM_B2 * v + (1.0 - ADAM_B2) * (g * g)
    m_hat = nm / (1.0 - ADAM_B1 ** ADAM_STEP)
    v_hat = nv / (1.0 - ADAM_B2 ** ADAM_STEP)
    return -ADAM_LR * (m_hat / (jnp.sqrt(v_hat) + ADAM_EPS) + ADAM_WD * w), nm, nv


def _sum_slots(p_ref):
    tot = p_ref[0].astype(F32)
    for d in range(1, N_DEV):
        tot = tot + p_ref[d].astype(F32)
    return tot


def _adamw_parts(parts, w, m, v, layer, results, name):
    n_layers, n_rows, n_cols = w.shape
    tr = n_rows // 2 if n_rows * n_cols > 512 * 1024 else n_rows

    def body(*refs):
        p_ref, w_ref, m_ref, v_ref = refs[:4]
        g_ref, d_ref, nm_ref, nv_ref = refs[-4:]
        g = _sum_slots(p_ref)
        g_ref[0] = g
        d_ref[0], nm_ref[0], nv_ref[0] = _adam_update(w_ref[0], g, m_ref[0], v_ref[0])

    spec = pl.BlockSpec((1, tr, n_cols), lambda i: (layer, i, 0))
    shape = jax.ShapeDtypeStruct((n_layers, n_rows, n_cols), F32)
    kept = [] if results is None else list(results)
    return pl.pallas_call(
        body, name="adamw_" + name, grid=(n_rows // tr,), out_shape=(shape,) * 4,
        in_specs=[pl.BlockSpec((N_DEV, tr, n_cols), lambda i: (0, i, 0)), spec, spec, spec] + [ANY] * len(kept),
        out_specs=(spec,) * 4, input_output_aliases={4 + j: j for j in range(len(kept))},
        compiler_params=_cparams(1),
    )(parts, w, m, v, *kept)


def _adamw(w, g, m, v, name):
    n_rows, n_cols = w.shape
    tr = 512 if n_rows % 512 == 0 else n_rows

    def body(w_ref, g_ref, m_ref, v_ref, d_ref, nm_ref, nv_ref):
        d_ref[...], nm_ref[...], nv_ref[...] = _adam_update(w_ref[...], g_ref[...], m_ref[...], v_ref[...])

    spec = pl.BlockSpec((tr, n_cols), lambda i: (i, 0))
    shape = jax.ShapeDtypeStruct((n_rows, n_cols), F32)
    return pl.pallas_call(
        body, name="adamw_" + name, grid=(n_rows // tr,), out_shape=(shape, shape, shape),
        in_specs=[spec] * 4, out_specs=(spec, spec, spec), compiler_params=_cparams(1),
    )(w, g, m, v)


_SMALL = ("pre_norm_g", "ln_v_g", "ln_v_b", "b_spatial", "sinks", "rel_bias", "post_norm_g")


def _pack(pieces):
    blocks, first_rows, n = [], [], 0
    for p in pieces:
        flat = p.reshape(-1)
        flat = jnp.concatenate([flat, jnp.zeros(((-flat.shape[0]) % (8 * 128),), F32)]).reshape(-1, 128)
        blocks.append(flat)
        first_rows.append(n)
        n += flat.shape[0]
    return jnp.concatenate(blocks, axis=0), first_rows


def _unpack(block, first_row, shape):
    size = math.prod(shape)
    return block[first_row:first_row + -(-size // 128)].reshape(-1)[:size].reshape(shape)


def kernel(x, pre_norm_g, w_in, ln_v_g, ln_v_b, w_spatial, b_spatial, sinks, rel_bias, w_out, post_norm_g, loss_target, m_pre_norm_g, m_w_in, m_ln_v_g, m_ln_v_b, m_w_spatial, m_b_spatial, m_sinks, m_rel_bias, m_w_out, m_post_norm_g, v_pre_norm_g, v_w_in, v_ln_v_g, v_ln_v_b, v_w_spatial, v_b_spatial, v_sinks, v_rel_bias, v_w_out, v_post_norm_g):
    weights = dict(pre_norm_g=pre_norm_g, w_in=w_in, ln_v_g=ln_v_g, ln_v_b=ln_v_b, w_spatial=w_spatial, b_spatial=b_spatial,
                   sinks=sinks, rel_bias=rel_bias, w_out=w_out, post_norm_g=post_norm_g)
    mom_m = dict(pre_norm_g=m_pre_norm_g, w_in=m_w_in, ln_v_g=m_ln_v_g, ln_v_b=m_ln_v_b, w_spatial=m_w_spatial,
                 b_spatial=m_b_spatial, sinks=m_sinks, rel_bias=m_rel_bias, w_out=m_w_out, post_norm_g=m_post_norm_g)
    mom_v = dict(pre_norm_g=v_pre_norm_g, w_in=v_w_in, ln_v_g=v_ln_v_g, ln_v_b=v_ln_v_b, w_spatial=v_w_spatial,
                 b_spatial=v_b_spatial, sinks=v_sinks, rel_bias=v_rel_bias, w_out=v_w_out, post_norm_g=v_post_norm_g)
    n_seq, seq_len, _ = x.shape
    n_layers = w_in.shape[0]
    x2 = x.reshape(n_seq * seq_len, D_MODEL)
    target2 = loss_target.reshape(n_seq * seq_len, D_MODEL)
    row = lambda p, l: p[l][None]

    wt_shards = jnp.swapaxes(w_in, 1, 2).astype(BF16)
    wo_shards = w_out.astype(BF16)
    bucket = _window_buckets()
    bias = _bias_table(rel_bias, bucket)
    wcat, wtcat, bs = _spatial_tables(w_spatial, jnp.swapaxes(b_spatial, 1, 2))

    wt, wo = [None] * n_layers, [None] * n_layers
    wt[0], wo[0] = _gather_weights(wt_shards, wo_shards, 0)
    xs, saved = [x2], []
    for l in range(n_layers):
        layer_args = (xs[-1], row(pre_norm_g, l), wt[l], row(ln_v_g, l), row(ln_v_b, l), wcat[l], bs[l], sinks[l], bias,
                      wo[l], row(post_norm_g, l), seq_len)
        if l + 1 < n_layers:
            *acts, xn, y, wt[l + 1], wo[l + 1] = _fwd_layer(*layer_args, gather=(wt_shards, wo_shards, l + 1))
            xs.append(xn)
        else:
            *acts, dx, y, loss = _fwd_layer(*layer_args, target=target2)
        saved.append((y, *acts))

    small = [None] * n_layers
    parts = [None] * n_layers
    waiting = None
    for l in reversed(range(n_layers)):
        hosted = {} if waiting is None else dict(exchange=waiting[:2], allgather=waiting[2])
        dproj, *rest = _bwd_mix(dx, *saved[l], row(ln_v_g, l), row(ln_v_b, l), wcat[l], wtcat[l], bs[l], wo[l],
                                row(post_norm_g, l), **hosted)
        dx, dwt, dgpre = _bwd_in(dproj, xs[l], dx, row(pre_norm_g, l), wt[l])
        dwo, dws, dmix, dlng, dlnb, dgpost, dsink, dbias = rest[:8]
        if waiting is not None:
            parts[l + 1] = list(rest[8:])
        waiting = (dwt, dwo, dws.reshape(HEADS * CHUNK, CHUNK))
        small[l] = dict(pre_norm_g=dgpre[0], ln_v_g=dlng[0], ln_v_b=dlnb[0], dmix=dmix, sinks=dsink, dbias=dbias,
                        post_norm_g=dgpost[0])
    parts[0] = list(_exchange_grads(*waiting))

    db = _b_spatial_grad(jnp.stack([s["dmix"] for s in small]))
    pieces = {n: jnp.stack([s[n] for s in small]) for n in ("pre_norm_g", "ln_v_g", "ln_v_b", "sinks", "post_norm_g")}
    pieces["b_spatial"] = jnp.swapaxes(db[:, :, :HEADS], 1, 2)
    pieces["rel_bias"] = _rel_bias_grad(jnp.stack([s["dbias"] for s in small]), bucket)
    packed, first_rows = _pack([pieces[n] for n in _SMALL] + [loss])
    total = _allreduce_small(packed)
    grad = {n: _unpack(total, r, weights[n].shape) for n, r in zip(_SMALL, first_rows)}
    loss_out = total[first_rows[-1], 0]

    delta, new_m, new_v = {}, {}, {}
    for n in _SMALL:
        w = weights[n]
        two_d = (-1, w.shape[-1])
        d, nm, nv = _adamw(w.reshape(two_d), grad[n].reshape(two_d), mom_m[n].reshape(two_d), mom_v[n].reshape(two_d), n)
        delta[n], new_m[n], new_v[n] = d.reshape(w.shape), nm.reshape(w.shape), nv.reshape(w.shape)

    t3 = lambda p: jnp.swapaxes(p, 1, 2)
    flat3 = lambda p: p.reshape(n_layers, HEADS * CHUNK, CHUNK)
    for n, which, view, back in (("w_in", 0, t3, t3), ("w_out", 1, lambda p: p, lambda p: p),
                                 ("w_spatial", 2, flat3, lambda p: p.reshape(w_spatial.shape))):
        results = None
        for l in reversed(range(n_layers)):
            results = _adamw_parts(parts[l][which], view(weights[n]), view(mom_m[n]), view(mom_v[n]), l, results, n)
        grad[n], delta[n], new_m[n], new_v[n] = (back(r) for r in results)

    names = tuple(weights)
    return (loss_out, dx.reshape(x.shape), *[grad[n] for n in names], *[delta[n] for n in names],
            *[new_m[n] for n in names], *[new_v[n] for n in names])
```

```python
import math

import jax
import jax.numpy as jnp
from jax import lax
from jax.experimental import pallas as pl
from jax.experimental.pallas import tpu as pltpu

F32 = jnp.float32
BF16 = jnp.bfloat16

D_MODEL = 1024
A_WIDTH = 512
B_WIDTH = 512
KV_WIDTH = 128
IN_WIDTH = 3 * A_WIDTH + 2 * B_WIDTH + 2 * KV_WIDTH
CHUNK = 128
HEADS = 8
HEAD_DIM = 64
REL_BUCKETS = 32
NORM_EPS = 1e-6
NEG = -1e30
Q_OFF = 3 * A_WIDTH
K_OFF = Q_OFF + B_WIDTH
BZ_OFF = K_OFF + 2 * KV_WIDTH
QK_SCALE = HEAD_DIM ** -0.5

ADAM_LR = 0.001
ADAM_B1 = 0.9
ADAM_B2 = 0.999
ADAM_EPS = 1e-08
ADAM_WD = 0.01
ADAM_STEP = 10

TOK_TILE = 512
CHUNK_UNROLL = 4
BWD_CHUNK_UNROLL = 4
VMEM_LIMIT_V7X = 60 * 1024 * 1024

N_DEV = 8
N_CHIPS = 4
MESH_ID = pl.DeviceIdType.MESH
ANY = pl.BlockSpec(memory_space=pl.ANY)
SMEM = pl.BlockSpec(memory_space=pltpu.SMEM)


def _cparams(n_axes):
    return pltpu.CompilerParams(dimension_semantics=("arbitrary",) * n_axes, vmem_limit_bytes=VMEM_LIMIT_V7X)


_GELU_C = math.sqrt(2.0 / math.pi)
_GELU_C3 = _GELU_C * 0.044715


def _gelu(x):
    t = jnp.tanh(x * (_GELU_C3 * (x * x) + _GELU_C))
    return x * (0.5 * t + 0.5)


def _gelu_and_grad(x):
    x2 = x * x
    t = jnp.tanh(x * (_GELU_C3 * x2 + _GELU_C))
    cdf = 0.5 * t + 0.5
    d = cdf + (x * (cdf * (1.0 - cdf))) * ((6.0 * _GELU_C3) * x2 + 2.0 * _GELU_C)
    return x * cdf, d


def _sigmoid(x):
    return 0.5 + 0.5 * jnp.tanh(0.5 * x)


def _silu(x):
    return x * _sigmoid(x)


def _silu_and_grad(x):
    s = _sigmoid(x)
    return x * s, s * (1.0 + x * (1.0 - s))


def _dot(a, b):
    return jnp.dot(a, b, preferred_element_type=F32)


def _dot_nt(a, b):
    return lax.dot_general(a, b, (((1,), (1,)), ((), ())), preferred_element_type=F32)


def _dot_tn(a, b):
    return lax.dot_general(a, b, (((0,), (0,)), ((), ())), preferred_element_type=F32)


def _lo_mask(shape):
    return lax.broadcasted_iota(jnp.int32, shape, 1) < HEAD_DIM


def _swap_halves(v):
    return pltpu.roll(v, HEAD_DIM, 1)


def _window_buckets():
    q_loc = jnp.arange(CHUNK)[:, None]
    j_loc = jnp.arange(CHUNK)[None, :]
    d = q_loc - j_loc + jnp.where(j_loc > q_loc, CHUNK, 0)
    max_exact = REL_BUCKETS // 2
    safe = jnp.maximum(d, 1).astype(F32)
    large = max_exact + (jnp.log(safe / max_exact) / math.log(CHUNK / max_exact)
                         * (REL_BUCKETS - max_exact)).astype(jnp.int32)
    large = jnp.minimum(large, REL_BUCKETS - 1)
    return jnp.where(d < max_exact, d, large).astype(jnp.int32)


def _bias_table(rel_bias, bucket):
    def body(rb_ref, bk_ref, out_ref):
        bk = bk_ref[...]
        for h in range(HEADS):
            acc = jnp.zeros(bk.shape, F32)
            for b in range(REL_BUCKETS):
                acc = jnp.where(bk == b, rb_ref[b, h], acc)
            out_ref[h] = acc

    vmem = pl.BlockSpec(memory_space=pltpu.VMEM)
    return pl.pallas_call(
        body, name="bias_table",
        out_shape=jax.ShapeDtypeStruct((HEADS, CHUNK, CHUNK), F32),
        in_specs=[SMEM, vmem], out_specs=vmem,
    )(rel_bias, bucket)


def _rel_bias_grad(dbias, bucket):
    n_layers = dbias.shape[0]

    def body(db_ref, bk_ref, out_ref):
        bk = bk_ref[...]
        for h in range(HEADS):
            tot = db_ref[0, h]
            for l in range(1, n_layers):
                tot = tot + db_ref[l, h]
            for b in range(REL_BUCKETS):
                out_ref[b, h] = jnp.sum(jnp.where(bk == b, tot, 0.0))

    vmem = pl.BlockSpec(memory_space=pltpu.VMEM)
    return pl.pallas_call(
        body, name="rel_bias_grad",
        out_shape=jax.ShapeDtypeStruct((REL_BUCKETS, HEADS), F32),
        in_specs=[vmem] * 2, out_specs=SMEM,
    )(dbias, bucket)


def _spatial_tables(w_spatial, b_spatial_t):
    n_layers = w_spatial.shape[0]

    def body(w_ref, b_ref, wcat_ref, wtcat_ref, bs_ref):
        row = lax.broadcasted_iota(jnp.int32, (CHUNK, CHUNK), 0)
        col = lax.broadcasted_iota(jnp.int32, (CHUNK, CHUNK), 1)
        causal = col <= row
        lo = _lo_mask((CHUNK, CHUNK))
        for p in range(4):
            for half in range(2):
                w = jnp.where(causal, w_ref[0, 2 * p + half], 0.0)
                wcat_ref[0, p, :, half * CHUNK:(half + 1) * CHUNK] = w.astype(BF16)
                wtcat_ref[0, p, :, half * CHUNK:(half + 1) * CHUNK] = w.T.astype(BF16)
            b = b_ref[0]
            bs_ref[0, p] = jnp.where(lo, b[:, 2 * p:2 * p + 1], b[:, 2 * p + 1:2 * p + 2])

    return pl.pallas_call(
        body, name="spatial_tables", grid=(n_layers,),
        out_shape=(jax.ShapeDtypeStruct((n_layers, 4, CHUNK, 2 * CHUNK), BF16),
                   jax.ShapeDtypeStruct((n_layers, 4, CHUNK, 2 * CHUNK), BF16),
                   jax.ShapeDtypeStruct((n_layers, 4, CHUNK, CHUNK), F32)),
        in_specs=[pl.BlockSpec((1, HEADS, CHUNK, CHUNK), lambda l: (l, 0, 0, 0)),
                  pl.BlockSpec((1, CHUNK, HEADS), lambda l: (l, 0, 0))],
        out_specs=(pl.BlockSpec((1, 4, CHUNK, 2 * CHUNK), lambda l: (l, 0, 0, 0)),
                   pl.BlockSpec((1, 4, CHUNK, 2 * CHUNK), lambda l: (l, 0, 0, 0)),
                   pl.BlockSpec((1, 4, CHUNK, CHUNK), lambda l: (l, 0, 0, 0))),
        compiler_params=_cparams(1),
    )(w_spatial, b_spatial_t)


def _b_spatial_grad(dmix):
    n_layers = dmix.shape[0]

    def body(d_ref, out_ref):
        lane = lax.broadcasted_iota(jnp.int32, (CHUNK, CHUNK), 1)
        acc = jnp.zeros((CHUNK, CHUNK), F32)
        for p in range(4):
            t = d_ref[0, :, p * CHUNK:(p + 1) * CHUNK]
            s_lo = jnp.sum(jnp.where(lane < HEAD_DIM, t, 0.0), axis=1, keepdims=True)
            s_hi = jnp.sum(jnp.where(lane < HEAD_DIM, 0.0, t), axis=1, keepdims=True)
            acc = jnp.where(lane == 2 * p, s_lo, acc)
            acc = jnp.where(lane == 2 * p + 1, s_hi, acc)
        out_ref[0] = acc

    return pl.pallas_call(
        body, name="b_spatial_grad", grid=(n_layers,),
        out_shape=jax.ShapeDtypeStruct((n_layers, CHUNK, CHUNK), F32),
        in_specs=[pl.BlockSpec((1, CHUNK, A_WIDTH), lambda l: (l, 0, 0))],
        out_specs=pl.BlockSpec((1, CHUNK, CHUNK), lambda l: (l, 0, 0)),
        compiler_params=_cparams(1),
    )(dmix)


def _place():
    x, y, c = lax.axis_index("x"), lax.axis_index("y"), lax.axis_index("c")
    other_chips = [(1 - x, y), (x, 1 - y), (1 - x, 1 - y)]
    return x, y, c, other_chips


N_GATHER_SEMS = 12


def _gather_phases(shards, fulls, send_sems, recv_sems, local_sems):
    x, y, c, chips = _place()
    sibling = (x, y, 1 - c)
    n_arr = len(shards)

    def half_rows(a, chip, half):
        n = shards[a].shape[0]
        start = (2 * chip[0] + chip[1]) * n + half * (n // 2)
        return fulls[a].at[pl.ds(pl.multiple_of(start, 16), n // 2), :]

    def my_half(a):
        n = shards[a].shape[0]
        return shards[a].at[pl.ds(pl.multiple_of(c * (n // 2), 16), n // 2), :]

    def copy(k, a, src, chip, half, to):
        return pltpu.make_async_remote_copy(
            src_ref=src, dst_ref=half_rows(a, chip, half), send_sem=send_sems.at[n_arr * k + a],
            recv_sem=recv_sems.at[n_arr * k + a], device_id=to, device_id_type=MESH_ID)

    def local(a):
        n = shards[a].shape[0]
        mine = fulls[a].at[pl.ds(pl.multiple_of((2 * x + y) * n, 16), n), :]
        return pltpu.make_async_copy(shards[a], mine, local_sems.at[a])

    def first(k, a):
        return copy(k, a, my_half(a), (x, y), c, (chips[k][0], chips[k][1], c))

    def passed(k, a):
        return copy(3 + k, a, half_rows(a, chips[k], c), chips[k], c, sibling)

    def phase_a():
        for a in range(n_arr):
            local(a).start()
        for k in range(3):
            for a in range(n_arr):
                first(k, a).start()

    def phase_b():
        for k in range(3):
            for a in range(n_arr):
                copy(k, a, my_half(a), chips[k], c, sibling).wait_recv()
                passed(k, a).start()

    def phase_c():
        for k in range(3):
            for a in range(n_arr):
                copy(3 + k, a, my_half(a), chips[k], 1 - c, sibling).wait_recv()
        for k in range(3):
            for a in range(n_arr):
                first(k, a).wait_send()
                passed(k, a).wait_send()
        for a in range(n_arr):
            local(a).wait()

    return phase_a, phase_b, phase_c


N_EXCHANGE_SEMS = 14


def _exchange_phases(partials, parts, send_sems, recv_sems, local_sems, by_chip=True):
    x, y, c, chips = _place()
    me, sibling = (x, y, c), (x, y, 1 - c)
    n_arr = len(partials)

    def block(a, chip):
        if not by_chip:
            return partials[a]
        n = partials[a].shape[0] // N_CHIPS
        return partials[a].at[pl.ds(pl.multiple_of((2 * chip[0] + chip[1]) * n, 16), n), :]

    def slot(a, dev):
        return parts[a].at[4 * dev[0] + 2 * dev[1] + dev[2]]

    def copy(k, a, src, origin, to):
        return pltpu.make_async_remote_copy(
            src_ref=src, dst_ref=slot(a, origin), send_sem=send_sems.at[n_arr * k + a],
            recv_sem=recv_sems.at[n_arr * k + a], device_id=to, device_id_type=MESH_ID)

    def local(a):
        return pltpu.make_async_copy(block(a, (x, y)), slot(a, me), local_sems.at[a])

    def first(k, a):
        if k == 0:
            return copy(0, a, block(a, (x, y)), me, sibling)
        chip = chips[k - 1]
        return copy(k, a, block(a, chip), me, (chip[0], chip[1], c))

    def passed(k, a):
        origin = (chips[k][0], chips[k][1], c)
        return copy(4 + k, a, slot(a, origin), origin, sibling)

    def phase_a():
        for a in range(n_arr):
            local(a).start()
        for k in range(4):
            for a in range(n_arr):
                first(k, a).start()

    def phase_b():
        for k in range(3):
            for a in range(n_arr):
                copy(1 + k, a, block(a, (x, y)), (chips[k][0], chips[k][1], c), me).wait_recv()
                passed(k, a).start()

    def phase_c():
        for a in range(n_arr):
            copy(0, a, block(a, (x, y)), sibling, me).wait_recv()
        for k in range(3):
            for a in range(n_arr):
                copy(4 + k, a, block(a, (x, y)), (chips[k][0], chips[k][1], 1 - c), me).wait_recv()
        for k in range(4):
            for a in range(n_arr):
                first(k, a).wait_send()
        for k in range(3):
            for a in range(n_arr):
                passed(k, a).wait_send()
        for a in range(n_arr):
            local(a).wait()

    return phase_a, phase_b, phase_c


def _comm_scratch(n_sems):
    return [pltpu.SemaphoreType.DMA((n_sems,)), pltpu.SemaphoreType.DMA((n_sems,)), pltpu.SemaphoreType.DMA((2,))]


def _gather_weights(wt_shards, wo_shards, layer):
    wt_rows, wo_rows = wt_shards.shape[1], wo_shards.shape[1]

    def body(wt_ref, wo_ref, wt_full, wo_full, send_sems, recv_sems, local_sems):
        phases = _gather_phases((wt_ref.at[layer], wo_ref.at[layer]), (wt_full, wo_full), send_sems, recv_sems, local_sems)
        for phase in phases:
            phase()

    return pl.pallas_call(
        body, name="gather_weights",
        out_shape=(jax.ShapeDtypeStruct((N_CHIPS * wt_rows, D_MODEL), BF16),
                   jax.ShapeDtypeStruct((N_CHIPS * wo_rows, D_MODEL), BF16)),
        in_specs=[ANY, ANY], out_specs=(ANY, ANY), scratch_shapes=_comm_scratch(N_GATHER_SEMS),
    )(wt_shards, wo_shards)


def _exchange_grads(dwt, dwo, small):
    def body(dwt_ref, dwo_ref, small_ref, pt_ref, po_ref, all_ref, *sems):
        blocks = _exchange_phases((dwt_ref, dwo_ref), (pt_ref, po_ref), *sems[:3])
        whole = _exchange_phases((small_ref,), (all_ref,), *sems[3:], by_chip=False)
        for phase_of_blocks, phase_of_whole in zip(blocks, whole):
            phase_of_blocks()
            phase_of_whole()

    return pl.pallas_call(
        body, name="exchange_grads",
        out_shape=(jax.ShapeDtypeStruct((N_DEV, dwt.shape[0] // N_CHIPS, D_MODEL), BF16),
                   jax.ShapeDtypeStruct((N_DEV, dwo.shape[0] // N_CHIPS, D_MODEL), BF16),
                   jax.ShapeDtypeStruct((N_DEV,) + small.shape, small.dtype)),
        in_specs=[ANY, ANY, ANY], out_specs=(ANY, ANY, ANY),
        scratch_shapes=_comm_scratch(N_EXCHANGE_SEMS) + _comm_scratch(N_EXCHANGE_SEMS),
    )(dwt, dwo, small)


def _allreduce_small(part):
    n_rows = part.shape[0]

    def body(p_ref, tot_ref, all_ref, send_sems, recv_sems, local_sem):
        x, y, c, chips = _place()
        me, sibling = (x, y, c), (x, y, 1 - c)

        def rows(dev):
            return all_ref.at[pl.ds(pl.multiple_of((4 * dev[0] + 2 * dev[1] + dev[2]) * n_rows, 8), n_rows), :]

        def copy(k, origin, to, src=None):
            return pltpu.make_async_remote_copy(
                src_ref=rows(origin) if src is None else src, dst_ref=rows(origin), send_sem=send_sems.at[k],
                recv_sem=recv_sems.at[k], device_id=to, device_id_type=MESH_ID)

        mine = pltpu.make_async_copy(p_ref, rows(me), local_sem)
        mine.start()
        first = [copy(0, me, sibling, src=p_ref)]
        first += [copy(1 + k, me, (chip[0], chip[1], c), src=p_ref) for k, chip in enumerate(chips)]
        for cp in first:
            cp.start()
        passed = []
        for k, chip in enumerate(chips):
            origin = (chip[0], chip[1], c)
            copy(1 + k, origin, me).wait_recv()
            fwd = copy(4 + k, origin, sibling)
            fwd.start()
            passed.append(fwd)
        copy(0, sibling, me).wait_recv()
        for k, chip in enumerate(chips):
            copy(4 + k, (chip[0], chip[1], 1 - c), me).wait_recv()
        for cp in first + passed:
            cp.wait_send()
        mine.wait()
        tot = all_ref[0:n_rows, :]
        for d in range(1, N_DEV):
            tot = tot + all_ref[d * n_rows:(d + 1) * n_rows, :]
        tot_ref[...] = tot

    vmem = pl.BlockSpec(memory_space=pltpu.VMEM)
    return pl.pallas_call(
        body, name="allreduce_small",
        out_shape=jax.ShapeDtypeStruct((n_rows, 128), F32),
        in_specs=[vmem], out_specs=vmem,
        scratch_shapes=[pltpu.VMEM((N_DEV * n_rows, 128), F32), pltpu.SemaphoreType.DMA((7,)),
                        pltpu.SemaphoreType.DMA((7,)), pltpu.SemaphoreType.DMA],
        compiler_params=pltpu.CompilerParams(vmem_limit_bytes=VMEM_LIMIT_V7X),
    )(part)


def _hosted(phases, step, n_steps):
    phase_a, phase_b, phase_c = phases

    def at_start():
        pl.when(step == 0)(phase_a)

    def at_end():
        pl.when(step == (3 * n_steps) // 4)(phase_b)
        pl.when(step == n_steps - 1)(phase_c)

    return at_start, at_end


def _layer_norm_stats(vv):
    mu = jnp.mean(vv, axis=-1, keepdims=True)
    xc = vv - mu
    rs = lax.rsqrt(jnp.mean(xc * xc, axis=-1, keepdims=True) + NORM_EPS)
    return xc * rs, rs


def _blockdiag(v, lo):
    zero = jnp.zeros_like(v)
    return jnp.concatenate([jnp.where(lo, v, zero), jnp.where(lo, zero, v)], axis=0)


def _softmax_sink(s, sink):
    m = jnp.maximum(jnp.max(s, axis=-1, keepdims=True), sink)
    e = jnp.exp(s - m)
    esink = jnp.exp(sink - m)
    den = jnp.sum(e, axis=-1, keepdims=True) + esink
    return e / den


def _kv_rows(cur_ref, halo_ref, r0, c):
    prev_in_tile = cur_ref[pl.ds(pl.multiple_of(jnp.maximum(r0 - CHUNK, 0), CHUNK), CHUNK), :]
    prev = jnp.where(c == 0, halo_ref[...], prev_in_tile)
    kv2 = jnp.concatenate([prev, cur_ref[pl.ds(r0, CHUNK), :]], axis=0)
    k2, v2 = kv2[:, 0:KV_WIDTH], kv2[:, KV_WIDTH:2 * KV_WIDTH]
    return (k2, _swap_halves(k2)), (v2, _swap_halves(v2))


def _window_square(over_keys, prev):
    return jnp.where(prev, over_keys[:, 0:CHUNK], over_keys[:, CHUNK:2 * CHUNK])


def _window_keys(square, prev):
    zero = jnp.zeros_like(square)
    return jnp.concatenate([jnp.where(prev, square, zero), jnp.where(prev, zero, square)], axis=1)


def _ahead(shape=(CHUNK, CHUNK)):
    return lax.broadcasted_iota(jnp.int32, shape, 1) - lax.broadcasted_iota(jnp.int32, shape, 0)


def _dead_mask(first, ahead):
    return ahead > jnp.where(first, 0, CHUNK)


def _head_of(p, half):
    return 2 * p + half, int(half != p // 2)


_HEADS_BY_COPY = tuple(tuple((p, half) for p in range(4) for half in range(2) if _head_of(p, half)[1] == sw)
                       for sw in range(2))


def _masked_halves(tile, lo):
    zero = jnp.zeros_like(tile)
    return {0: jnp.where(lo, tile, zero), 1: jnp.where(lo, zero, tile)}


def _query_stacks(q_tiles, lo):
    qm = {p: _masked_halves(q_tiles[p] * QK_SCALE, lo) for p in range(4)}
    return [jnp.concatenate([qm[p][half] for p, half in members], axis=0) for members in _HEADS_BY_COPY]


def _attention_probs(q_stacks, ks, bias_ref, sink_ref, dead, prev):
    probs = {}
    for sw, members in enumerate(_HEADS_BY_COPY):
        s_stack = _dot_nt(q_stacks[sw], ks[sw])
        for i, (p, half) in enumerate(members):
            head = 2 * p + half
            s = _window_square(s_stack[i * CHUNK:(i + 1) * CHUNK, :], prev) + bias_ref[head]
            s = jnp.where(dead, NEG, s)
            probs[p, half] = _softmax_sink(s, sink_ref[head]).astype(BF16)
    return probs


def _attention_values(probs, vs, prev):
    outs, p_stacks = {}, []
    for sw, members in enumerate(_HEADS_BY_COPY):
        p_stack = jnp.concatenate([_window_keys(probs[m], prev) for m in members], axis=0)
        r_stack = _dot(p_stack, vs[sw])
        p_stacks.append(p_stack)
        for i, m in enumerate(members):
            outs[m] = r_stack[i * CHUNK:(i + 1) * CHUNK, :]
    return outs, p_stacks


def _fwd_layer(x, g_pre, w_in_t, ln_g, ln_b, wcat, bs, sinks, bias, w_out, g_post, seq_len, gather=None, target=None):
    n_tok = x.shape[0]
    tb = TOK_TILE
    n_chunks = tb // CHUNK
    n_tiles = n_tok // tb
    n_in = 11
    assert gather is None or target is None

    def body(*refs):
        (x_ref, gpre_ref, wt_ref, lng_ref, lnb_ref, wcat_ref, bs_ref, sink_ref, bias_ref, wout_ref,
         gpost_ref) = refs[:n_in]
        at_start = at_end = lambda: None
        if gather is not None:
            (wts_ref, wos_ref, a_ref, q_ref, kv_ref, bz_ref, probs_ref, xn_ref, y_ref, wt_full, wo_full, ycat_s, halo_ref,
             send_sems, recv_sems, local_sems) = refs[n_in:]
            phases = _gather_phases((wts_ref.at[gather[2]], wos_ref.at[gather[2]]), (wt_full, wo_full), send_sems,
                                    recv_sems, local_sems)
            at_start, at_end = _hosted(phases, pl.program_id(0), n_tiles)
        elif target is not None:
            target_ref, a_ref, q_ref, kv_ref, bz_ref, probs_ref, xn_ref, y_ref, loss_ref, ycat_s, halo_ref = refs[n_in:]
        else:
            a_ref, q_ref, kv_ref, bz_ref, probs_ref, xn_ref, y_ref, ycat_s, halo_ref = refs[n_in:]
        at_start()
        i = pl.program_id(0)
        lo = _lo_mask((CHUNK, CHUNK))
        ahead = _ahead()
        prev = ahead > 0

        @pl.when(i == 0)
        def _():
            halo_ref[...] = jnp.zeros_like(halo_ref)

        xf = x_ref[...]
        r1 = lax.rsqrt(jnp.mean(xf * xf, axis=-1, keepdims=True) + NORM_EPS)
        h = ((xf * r1) * gpre_ref[...]).astype(BF16)
        a_ref[...] = _dot_nt(h, wt_ref[0:Q_OFF, :])
        q_ref[...] = _dot_nt(h, wt_ref[Q_OFF:K_OFF, :]).astype(BF16)
        kv_ref[...] = _dot_nt(h, wt_ref[K_OFF:BZ_OFF, :]).astype(BF16)
        bz_ref[...] = _dot_nt(h, wt_ref[BZ_OFF:IN_WIDTH, :])

        def chunk(c, carry):
            r0 = pl.multiple_of(c * CHUNK, CHUNK)
            rows = pl.ds(r0, CHUNK)
            u = _gelu(a_ref[rows, 0:A_WIDTH])
            vv = _gelu(a_ref[rows, A_WIDTH:2 * A_WIDTH])
            xhat, _ = _layer_norm_stats(vv)
            vnb = (xhat * lng_ref[...] + lnb_ref[...]).astype(BF16)
            for p in range(4):
                blk = slice(p * CHUNK, (p + 1) * CHUNK)
                mixed = _dot(wcat_ref[p], _blockdiag(vnb[:, blk], lo)) + bs_ref[p]
                sz = _silu(a_ref[rows, 2 * A_WIDTH + p * CHUNK:2 * A_WIDTH + (p + 1) * CHUNK])
                ycat_s[rows, blk] = ((u[:, blk] * mixed) * sz).astype(BF16)
            ks, vs = _kv_rows(kv_ref, halo_ref, r0, c)
            dead = _dead_mask(lax.rem(i * tb + r0, seq_len) == 0, ahead)
            q_tiles = [q_ref[rows, p * CHUNK:(p + 1) * CHUNK] for p in range(4)]
            probs = _attention_probs(_query_stacks(q_tiles, lo), ks, bias_ref, sink_ref, dead, prev)
            for (p, half), prob in probs.items():
                head = 2 * p + half
                probs_ref[rows, head * CHUNK:(head + 1) * CHUNK] = prob
            outs, _ = _attention_values(probs, vs, prev)
            for p in range(4):
                blk = slice(p * CHUNK, (p + 1) * CHUNK)
                o = jnp.where(lo, outs[p, 0], outs[p, 1])
                ycat_s[rows, B_WIDTH + p * CHUNK:B_WIDTH + (p + 1) * CHUNK] = (o * _silu(bz_ref[rows, blk])).astype(BF16)
            return carry

        lax.fori_loop(0, n_chunks, chunk, 0, unroll=CHUNK_UNROLL)
        halo_ref[...] = kv_ref[tb - CHUNK:tb, :]
        y = _dot(ycat_s[...], wout_ref[...])
        r = lax.rsqrt(jnp.mean(y * y, axis=-1, keepdims=True) + NORM_EPS)
        y_ref[...] = y
        xn = x_ref[...] + (y * r) * gpost_ref[...]
        if target is None:
            xn_ref[...] = xn
        else:
            d = xn - target_ref[...]
            xn_ref[...] = d * (1.0 / D_MODEL)

            @pl.when(i == 0)
            def _():
                loss_ref[0, 0] = 0.0

            loss_ref[0, 0] += 0.5 * jnp.sum(jnp.mean(d * d, axis=-1, keepdims=True))
        at_end()

    tile = lambda w: pl.BlockSpec((tb, w), lambda i: (i, 0))
    whole = lambda shape, **kw: pl.BlockSpec(shape, lambda i: (0,) * len(shape), **kw)
    in_specs = [tile(D_MODEL), whole((1, D_MODEL)), whole((IN_WIDTH, D_MODEL), pipeline_mode=pl.Buffered(1)),
                whole((1, A_WIDTH)), whole((1, A_WIDTH)), whole((4, CHUNK, 2 * CHUNK)), whole((4, CHUNK, CHUNK)), SMEM,
                whole((HEADS, CHUNK, CHUNK)), whole((D_MODEL, D_MODEL)), whole((1, D_MODEL))]
    out_shape = [jax.ShapeDtypeStruct((n_tok, Q_OFF), F32), jax.ShapeDtypeStruct((n_tok, B_WIDTH), BF16),
                 jax.ShapeDtypeStruct((n_tok, 2 * KV_WIDTH), BF16), jax.ShapeDtypeStruct((n_tok, B_WIDTH), F32),
                 jax.ShapeDtypeStruct((n_tok, HEADS * CHUNK), BF16), jax.ShapeDtypeStruct((n_tok, D_MODEL), F32),
                 jax.ShapeDtypeStruct((n_tok, D_MODEL), F32)]
    out_specs = [tile(Q_OFF), tile(B_WIDTH), tile(2 * KV_WIDTH), tile(B_WIDTH), tile(HEADS * CHUNK), tile(D_MODEL),
                 tile(D_MODEL)]
    scratch = [pltpu.VMEM((tb, D_MODEL), BF16), pltpu.VMEM((CHUNK, 2 * KV_WIDTH), BF16)]
    args = [x, g_pre, w_in_t, ln_g, ln_b, wcat, bs, sinks, bias, w_out, g_post]
    if gather is not None:
        wt_shards, wo_shards, _ = gather
        in_specs += [ANY, ANY]
        args += [wt_shards, wo_shards]
        out_shape += [jax.ShapeDtypeStruct((N_CHIPS * wt_shards.shape[1], D_MODEL), BF16),
                      jax.ShapeDtypeStruct((N_CHIPS * wo_shards.shape[1], D_MODEL), BF16)]
        out_specs += [ANY, ANY]
        scratch += _comm_scratch(N_GATHER_SEMS)
    if target is not None:
        in_specs.append(tile(D_MODEL))
        args.append(target)
        out_shape.append(jax.ShapeDtypeStruct((1, 1), F32))
        out_specs.append(SMEM)
    name = "fwd_layer" + ("" if gather is None else "_gather") + ("" if target is None else "_loss")
    return pl.pallas_call(
        body, name=name, grid=(n_tiles,),
        out_shape=tuple(out_shape), in_specs=in_specs, out_specs=tuple(out_specs), scratch_shapes=scratch,
        compiler_params=_cparams(1),
    )(*args)


def _bwd_mix(dout, y, a, q, kv, bz, probs, ln_g, ln_b, wcat, wtcat, bs, w_out, g_post, exchange=None, allgather=None):
    n_tok = y.shape[0]
    tb = TOK_TILE
    n_chunks = tb // CHUNK
    n_tiles = n_tok // tb

    def body(*refs):
        refs = list(refs)
        take = lambda n: [refs.pop(0) for _ in range(n)]
        (dout_ref, y_ref, a_ref, q_ref, kv_ref, halo_ref, bz_ref, probs_ref, lng_ref, lnb_ref, wcat_ref, wtcat_ref, bs_ref,
         wout_ref, gpost_ref) = take(15)
        if exchange is not None:
            dwt_in, dwo_in = take(2)
        if allgather is not None:
            (small_in,) = take(1)
        dproj_ref, dwout_ref, dws_ref, dmix_ref, dlng_ref, dlnb_ref, dgpost_ref, dsink_ref, dbias_ref = take(9)
        if exchange is not None:
            pt_ref, po_ref = take(2)
        if allgather is not None:
            (small_all,) = take(1)
        ycat_s, dy_s, dyc_s, dkv_s, carry_s, dwout_s, stage_s = take(7)
        starts, ends = [], []
        if exchange is not None:
            hosted = _hosted(_exchange_phases((dwt_in, dwo_in), (pt_ref, po_ref), *take(3)), pl.program_id(0), n_tiles)
            starts.append(hosted[0])
            ends.append(hosted[1])
        if allgather is not None:
            hosted = _hosted(_exchange_phases((small_in,), (small_all,), *take(3), by_chip=False), pl.program_id(0),
                             n_tiles)
            starts.append(hosted[0])
            ends.append(hosted[1])
        at_start = lambda: [f() for f in starts]
        at_end = lambda: [f() for f in ends]
        at_start()
        step = pl.program_id(0)
        lo = _lo_mask((CHUNK, CHUNK))
        prev = _ahead() > 0

        @pl.when(step == 0)
        def _():
            dwout_s[...] = jnp.zeros_like(dwout_s)
            dws_ref[...] = jnp.zeros_like(dws_ref)
            dmix_ref[...] = jnp.zeros_like(dmix_ref)
            dlng_ref[...] = jnp.zeros_like(dlng_ref)
            dlnb_ref[...] = jnp.zeros_like(dlnb_ref)
            dgpost_ref[...] = jnp.zeros_like(dgpost_ref)
            dbias_ref[...] = jnp.zeros_like(dbias_ref)
            carry_s[...] = jnp.zeros_like(carry_s)
            for h in range(HEADS):
                dsink_ref[h] = 0.0

        for r0 in range(0, tb, CHUNK):
            rows = slice(r0, r0 + CHUNK)
            yv = y_ref[rows, :]
            dout = dout_ref[rows, :]
            r = lax.rsqrt(jnp.mean(yv * yv, axis=-1, keepdims=True) + NORM_EPS)
            yn = yv * r
            dgpost_ref[...] += jnp.sum(dout * yn, axis=0, keepdims=True)
            dyn = dout * gpost_ref[...]
            dy_s[rows, :] = (r * (dyn - yn * jnp.mean(dyn * yn, axis=-1, keepdims=True))).astype(BF16)
        dyc_s[...] = _dot_nt(dy_s[...], wout_ref[...])
        dkv_s[0:tb, :] = jnp.zeros((tb, 2 * KV_WIDTH), F32)
        dkv_s[tb:tb + CHUNK, :] = carry_s[...]

        def chunk(c, carry):
            r0 = pl.multiple_of(c * CHUNK, CHUNK)
            rows = pl.ds(r0, CHUNK)
            u, gu = _gelu_and_grad(a_ref[rows, 0:A_WIDTH])
            vv, gv = _gelu_and_grad(a_ref[rows, A_WIDTH:2 * A_WIDTH])
            xhat, rs = _layer_norm_stats(vv)
            vnb = (xhat * lng_ref[...] + lnb_ref[...]).astype(BF16)
            d_vn, d_u, d_az = [], [], []
            for p in range(4):
                blk = slice(p * CHUNK, (p + 1) * CHUNK)
                vbd = _blockdiag(vnb[:, blk], lo)
                mixed = _dot(wcat_ref[p], vbd) + bs_ref[p]
                sz, gz = _silu_and_grad(a_ref[rows, 2 * A_WIDTH + p * CHUNK:2 * A_WIDTH + (p + 1) * CHUNK])
                ub = u[:, blk]
                dya = dyc_s[rows, blk]
                um = ub * mixed
                ycat_s[rows, blk] = (um * sz).astype(BF16)
                d_mixed = (dya * ub) * sz
                d_u.append((dya * mixed) * sz)
                d_az.append((dya * um) * gz)
                dmix_ref[:, blk] += d_mixed
                dmbd = _blockdiag(d_mixed.astype(BF16), lo)
                d_vn.append(_dot(wtcat_ref[p], dmbd))
                dws_ref[p] += _dot_nt(dmbd, vnb[:, blk])
            d_vn = jnp.concatenate(d_vn, axis=1)
            dlng_ref[...] += jnp.sum(d_vn * xhat, axis=0, keepdims=True)
            dlnb_ref[...] += jnp.sum(d_vn, axis=0, keepdims=True)
            dxh = d_vn * lng_ref[...]
            d_vv = rs * (dxh - jnp.mean(dxh, axis=-1, keepdims=True)
                         - xhat * jnp.mean(dxh * xhat, axis=-1, keepdims=True))
            dproj_ref[rows, 0:A_WIDTH] = (jnp.concatenate(d_u, axis=1) * gu).astype(BF16)
            dproj_ref[rows, A_WIDTH:2 * A_WIDTH] = (d_vv * gv).astype(BF16)
            dproj_ref[rows, 2 * A_WIDTH:Q_OFF] = jnp.concatenate(d_az, axis=1).astype(BF16)
            ks, vs = _kv_rows(kv_ref, halo_ref, r0, c)
            q_stacks = _query_stacks([q_ref[rows, p * CHUNK:(p + 1) * CHUNK] for p in range(4)], lo)
            probs = {(p, half): probs_ref[rows, (2 * p + half) * CHUNK:(2 * p + half + 1) * CHUNK]
                     for p in range(4) for half in range(2)}
            outs_h, p_stacks = _attention_values(probs, vs, prev)
            dom = {}
            for p in range(4):
                blk = slice(p * CHUNK, (p + 1) * CHUNK)
                sb, gb = _silu_and_grad(bz_ref[rows, blk])
                dyb = dyc_s[rows, B_WIDTH + p * CHUNK:B_WIDTH + (p + 1) * CHUNK]
                o = jnp.where(lo, outs_h[p, 0], outs_h[p, 1])
                ycat_s[rows, B_WIDTH + p * CHUNK:B_WIDTH + (p + 1) * CHUNK] = (o * sb).astype(BF16)
                dproj_ref[rows, BZ_OFF + p * CHUNK:BZ_OFF + (p + 1) * CHUNK] = ((dyb * o) * gb).astype(BF16)
                dom[p] = _masked_halves((dyb * sb).astype(BF16), lo)
            dqs, dk_by_copy, dv_by_copy = {}, [], []
            for sw, members in enumerate(_HEADS_BY_COPY):
                do_stack = jnp.concatenate([dom[p][half] for p, half in members], axis=0)
                dp_stack = _dot_nt(do_stack, vs[sw])
                dsws = []
                for i, (p, half) in enumerate(members):
                    head = 2 * p + half
                    dp = _window_square(dp_stack[i * CHUNK:(i + 1) * CHUNK, :], prev)
                    prob = probs[p, half].astype(F32)
                    delta = jnp.sum(prob * dp, axis=-1, keepdims=True)
                    ds = prob * (dp - delta)
                    dsink_ref[head] += -jnp.sum((1.0 - jnp.sum(prob, axis=-1, keepdims=True)) * delta)
                    dbias_ref[head] += ds
                    dsws.append(_window_keys(ds.astype(BF16), prev))
                ds_stack = jnp.concatenate(dsws, axis=0)
                dq_stack = _dot(ds_stack, ks[sw])
                for i, m in enumerate(members):
                    dqs[m] = dq_stack[i * CHUNK:(i + 1) * CHUNK, :]
                dk_by_copy.append(_dot_tn(ds_stack, q_stacks[sw]))
                dv_by_copy.append(_dot_tn(p_stacks[sw], do_stack))
            for p in range(4):
                dq = jnp.where(lo, dqs[p, 0], dqs[p, 1]) * QK_SCALE
                dproj_ref[rows, Q_OFF + p * CHUNK:Q_OFF + (p + 1) * CHUNK] = dq.astype(BF16)
            both = pl.ds(r0, 2 * CHUNK)
            dkv_s[both, 0:KV_WIDTH] += dk_by_copy[0] + _swap_halves(dk_by_copy[1])
            dkv_s[both, KV_WIDTH:2 * KV_WIDTH] += dv_by_copy[0] + _swap_halves(dv_by_copy[1])
            return carry

        lax.fori_loop(0, n_chunks, chunk, 0, unroll=BWD_CHUNK_UNROLL)
        dwout_s[...] += _dot_tn(ycat_s[...], dy_s[...])
        dproj_ref[:, K_OFF:BZ_OFF] = dkv_s[CHUNK:CHUNK + tb, :].astype(BF16)
        carry_s[...] = dkv_s[0:CHUNK, :]

        @pl.when(step == n_tiles - 1)
        def _():
            row = lax.broadcasted_iota(jnp.int32, (2 * CHUNK, CHUNK), 0)
            col = lax.broadcasted_iota(jnp.int32, (2 * CHUNK, CHUNK), 1)
            causal = col <= jnp.where(row >= CHUNK, row - CHUNK, row)
            for p in range(4):
                dws_ref[p] = jnp.where(causal, dws_ref[p], 0.0)
            rows = stage_s.shape[0]
            for r0 in range(0, D_MODEL, rows):
                stage_s[...] = dwout_s[r0:r0 + rows, :].astype(BF16)
                pltpu.sync_copy(stage_s, dwout_ref.at[r0:r0 + rows, :])

        at_end()

    tile = lambda w: pl.BlockSpec((tb, w), lambda s: (n_tiles - 1 - s, 0))
    whole = lambda shape, **kw: pl.BlockSpec(shape, lambda s: (0,) * len(shape), **kw)
    once = dict(pipeline_mode=pl.Buffered(1))
    in_specs = [tile(D_MODEL), tile(D_MODEL), tile(Q_OFF), tile(B_WIDTH), tile(2 * KV_WIDTH),
                pl.BlockSpec((CHUNK, 2 * KV_WIDTH), lambda s: (jnp.maximum((n_tiles - 1 - s) * n_chunks - 1, 0), 0)),
                tile(B_WIDTH), tile(HEADS * CHUNK), whole((1, A_WIDTH)), whole((1, A_WIDTH)), whole((4, CHUNK, 2 * CHUNK)),
                whole((4, CHUNK, 2 * CHUNK)), whole((4, CHUNK, CHUNK)), whole((D_MODEL, D_MODEL), **once),
                whole((1, D_MODEL))]
    args = [dout, y, a, q, kv, kv, bz, probs, ln_g, ln_b, wcat, wtcat, bs, w_out, g_post]
    small_shapes = [jax.ShapeDtypeStruct((D_MODEL, D_MODEL), BF16), jax.ShapeDtypeStruct((4, 2 * CHUNK, CHUNK), F32),
                    jax.ShapeDtypeStruct((CHUNK, A_WIDTH), F32), jax.ShapeDtypeStruct((1, A_WIDTH), F32),
                    jax.ShapeDtypeStruct((1, A_WIDTH), F32), jax.ShapeDtypeStruct((1, D_MODEL), F32),
                    jax.ShapeDtypeStruct((HEADS,), F32), jax.ShapeDtypeStruct((HEADS, CHUNK, CHUNK), F32)]
    small_specs = [ANY, whole((4, 2 * CHUNK, CHUNK)), whole((CHUNK, A_WIDTH)), whole((1, A_WIDTH)), whole((1, A_WIDTH)),
                   whole((1, D_MODEL)), SMEM, whole((HEADS, CHUNK, CHUNK))]
    scratch = [pltpu.VMEM((tb, D_MODEL), BF16), pltpu.VMEM((tb, D_MODEL), BF16), pltpu.VMEM((tb, D_MODEL), F32),
               pltpu.VMEM((tb + CHUNK, 2 * KV_WIDTH), F32), pltpu.VMEM((CHUNK, 2 * KV_WIDTH), F32),
               pltpu.VMEM((D_MODEL, D_MODEL), F32), pltpu.VMEM((2 * CHUNK, D_MODEL), BF16)]
    out_shape = [jax.ShapeDtypeStruct((n_tok, IN_WIDTH), BF16)] + small_shapes
    out_specs = [tile(IN_WIDTH)] + small_specs
    if exchange is not None:
        dwt, dwo = exchange
        in_specs += [ANY, ANY]
        args += [dwt, dwo]
        out_shape += [jax.ShapeDtypeStruct((N_DEV, dwt.shape[0] // N_CHIPS, D_MODEL), BF16),
                      jax.ShapeDtypeStruct((N_DEV, dwo.shape[0] // N_CHIPS, D_MODEL), BF16)]
        out_specs += [ANY, ANY]
    if allgather is not None:
        in_specs.append(ANY)
        args.append(allgather)
        out_shape.append(jax.ShapeDtypeStruct((N_DEV,) + allgather.shape, allgather.dtype))
        out_specs.append(ANY)
    for hosted in (exchange, allgather):
        if hosted is not None:
            scratch += _comm_scratch(N_EXCHANGE_SEMS)
    return pl.pallas_call(
        body, name="bwd_mix" + ("" if exchange is None and allgather is None else "_exchange"), grid=(n_tiles,),
        out_shape=tuple(out_shape), in_specs=in_specs, out_specs=tuple(out_specs), scratch_shapes=scratch,
        compiler_params=_cparams(1),
    )(*args)


def _bwd_in(dproj, x, dout, g_pre, w_in_t):
    n_tok = x.shape[0]
    tm = TOK_TILE
    n_tiles = n_tok // tm

    def body(dp_ref, x_ref, dout_ref, g_ref, wt_ref, dx_ref, dwt_ref, dg_ref, acc_s):
        i = pl.program_id(0)

        @pl.when(i == 0)
        def _():
            acc_s[...] = jnp.zeros_like(acc_s)
            dg_ref[...] = jnp.zeros_like(dg_ref)

        xf = x_ref[...]
        r = lax.rsqrt(jnp.mean(xf * xf, axis=-1, keepdims=True) + NORM_EPS)
        xn = xf * r
        h = (xn * g_ref[...]).astype(BF16)
        dp = dp_ref[...]
        dh = _dot(dp, wt_ref[...])
        acc_s[...] += _dot_tn(dp, h)
        dg_ref[...] += jnp.sum(dh * xn, axis=0, keepdims=True)
        dhn = dh * g_ref[...]
        dx_ref[...] = dout_ref[...] + r * (dhn - xn * jnp.mean(dhn * xn, axis=-1, keepdims=True))

        @pl.when(i == n_tiles - 1)
        def _():
            dwt_ref[...] = acc_s[...].astype(BF16)

    return pl.pallas_call(
        body, name="bwd_in", grid=(n_tiles,),
        out_shape=(jax.ShapeDtypeStruct((n_tok, D_MODEL), F32), jax.ShapeDtypeStruct((IN_WIDTH, D_MODEL), BF16),
                   jax.ShapeDtypeStruct((1, D_MODEL), F32)),
        in_specs=[pl.BlockSpec((tm, IN_WIDTH), lambda i: (i, 0)), pl.BlockSpec((tm, D_MODEL), lambda i: (i, 0)),
                  pl.BlockSpec((tm, D_MODEL), lambda i: (i, 0)), pl.BlockSpec((1, D_MODEL), lambda i: (0, 0)),
                  pl.BlockSpec((IN_WIDTH, D_MODEL), lambda i: (0, 0), pipeline_mode=pl.Buffered(1))],
        out_specs=(pl.BlockSpec((tm, D_MODEL), lambda i: (i, 0)),
                   pl.BlockSpec((IN_WIDTH, D_MODEL), lambda i: (0, 0), pipeline_mode=pl.Buffered(1)),
                   pl.BlockSpec((1, D_MODEL), lambda i: (0, 0))),
        scratch_shapes=[pltpu.VMEM((IN_WIDTH, D_MODEL), F32)],
        compiler_params=_cparams(1),
    )(dproj, x, dout, g_pre, w_in_t)


def _adam_update(w, g, m, v):
    nm = ADAM_B1 * m + (1.0 - ADAM_B1) * g
    nv = ADAM_B2 * v + (1.0 - ADAM_B2) * (g * g)
    m_hat = nm / (1.0 - ADAM_B1 ** ADAM_STEP)
    v_hat = nv / (1.0 - ADAM_B2 ** ADAM_STEP)
    return -ADAM_LR * (m_hat / (jnp.sqrt(v_hat) + ADAM_EPS) + ADAM_WD * w), nm, nv


def _sum_slots(p_ref):
    tot = p_ref[0].astype(F32)
    for d in range(1, N_DEV):
        tot = tot + p_ref[d].astype(F32)
    return tot


def _adamw_parts(parts, w, m, v, layer, results, name):
    n_layers, n_rows, n_cols = w.shape
    tr = n_rows // 2 if n_rows * n_cols > 512 * 1024 else n_rows

    def body(*refs):
        p_ref, w_ref, m_ref, v_ref = refs[:4]
        g_ref, d_ref, nm_ref, nv_ref = refs[-4:]
        g = _sum_slots(p_ref)
        g_ref[0] = g
        d_ref[0], nm_ref[0], nv_ref[0] = _adam_update(w_ref[0], g, m_ref[0], v_ref[0])

    spec = pl.BlockSpec((1, tr, n_cols), lambda i: (layer, i, 0))
    shape = jax.ShapeDtypeStruct((n_layers, n_rows, n_cols), F32)
    kept = [] if results is None else list(results)
    return pl.pallas_call(
        body, name="adamw_" + name, grid=(n_rows // tr,), out_shape=(shape,) * 4,
        in_specs=[pl.BlockSpec((N_DEV, tr, n_cols), lambda i: (0, i, 0)), spec, spec, spec] + [ANY] * len(kept),
        out_specs=(spec,) * 4, input_output_aliases={4 + j: j for j in range(len(kept))},
        compiler_params=_cparams(1),
    )(parts, w, m, v, *kept)


def _adamw(w, g, m, v, name):
    n_rows, n_cols = w.shape
    tr = 512 if n_rows % 512 == 0 else n_rows

    def body(w_ref, g_ref, m_ref, v_ref, d_ref, nm_ref, nv_ref):
        d_ref[...], nm_ref[...], nv_ref[...] = _adam_update(w_ref[...], g_ref[...], m_ref[...], v_ref[...])

    spec = pl.BlockSpec((tr, n_cols), lambda i: (i, 0))
    shape = jax.ShapeDtypeStruct((n_rows, n_cols), F32)
    return pl.pallas_call(
        body, name="adamw_" + name, grid=(n_rows // tr,), out_shape=(shape, shape, shape),
        in_specs=[spec] * 4, out_specs=(spec, spec, spec), compiler_params=_cparams(1),
    )(w, g, m, v)


_SMALL = ("pre_norm_g", "ln_v_g", "ln_v_b", "b_spatial", "sinks", "rel_bias", "post_norm_g")


def _pack(pieces):
    blocks, first_rows, n = [], [], 0
    for p in pieces:
        flat = p.reshape(-1)
        flat = jnp.concatenate([flat, jnp.zeros(((-flat.shape[0]) % (8 * 128),), F32)]).reshape(-1, 128)
        blocks.append(flat)
        first_rows.append(n)
        n += flat.shape[0]
    return jnp.concatenate(blocks, axis=0), first_rows


def _unpack(block, first_row, shape):
    size = math.prod(shape)
    return block[first_row:first_row + -(-size // 128)].reshape(-1)[:size].reshape(shape)


def kernel(x, pre_norm_g, w_in, ln_v_g, ln_v_b, w_spatial, b_spatial, sinks, rel_bias, w_out, post_norm_g, loss_target, m_pre_norm_g, m_w_in, m_ln_v_g, m_ln_v_b, m_w_spatial, m_b_spatial, m_sinks, m_rel_bias, m_w_out, m_post_norm_g, v_pre_norm_g, v_w_in, v_ln_v_g, v_ln_v_b, v_w_spatial, v_b_spatial, v_sinks, v_rel_bias, v_w_out, v_post_norm_g):
    weights = dict(pre_norm_g=pre_norm_g, w_in=w_in, ln_v_g=ln_v_g, ln_v_b=ln_v_b, w_spatial=w_spatial, b_spatial=b_spatial,
                   sinks=sinks, rel_bias=rel_bias, w_out=w_out, post_norm_g=post_norm_g)
    mom_m = dict(pre_norm_g=m_pre_norm_g, w_in=m_w_in, ln_v_g=m_ln_v_g, ln_v_b=m_ln_v_b, w_spatial=m_w_spatial,
                 b_spatial=m_b_spatial, sinks=m_sinks, rel_bias=m_rel_bias, w_out=m_w_out, post_norm_g=m_post_norm_g)
    mom_v = dict(pre_norm_g=v_pre_norm_g, w_in=v_w_in, ln_v_g=v_ln_v_g, ln_v_b=v_ln_v_b, w_spatial=v_w_spatial,
                 b_spatial=v_b_spatial, sinks=v_sinks, rel_bias=v_rel_bias, w_out=v_w_out, post_norm_g=v_post_norm_g)
    n_seq, seq_len, _ = x.shape
    n_layers = w_in.shape[0]
    x2 = x.reshape(n_seq * seq_len, D_MODEL)
    target2 = loss_target.reshape(n_seq * seq_len, D_MODEL)
    row = lambda p, l: p[l][None]

    wt_shards = jnp.swapaxes(w_in, 1, 2).astype(BF16)
    wo_shards = w_out.astype(BF16)
    bucket = _window_buckets()
    bias = _bias_table(rel_bias, bucket)
    wcat, wtcat, bs = _spatial_tables(w_spatial, jnp.swapaxes(b_spatial, 1, 2))

    wt, wo = [None] * n_layers, [None] * n_layers
    wt[0], wo[0] = _gather_weights(wt_shards, wo_shards, 0)
    xs, saved = [x2], []
    for l in range(n_layers):
        layer_args = (xs[-1], row(pre_norm_g, l), wt[l], row(ln_v_g, l), row(ln_v_b, l), wcat[l], bs[l], sinks[l], bias,
                      wo[l], row(post_norm_g, l), seq_len)
        if l + 1 < n_layers:
            *acts, xn, y, wt[l + 1], wo[l + 1] = _fwd_layer(*layer_args, gather=(wt_shards, wo_shards, l + 1))
            xs.append(xn)
        else:
            *acts, dx, y, loss = _fwd_layer(*layer_args, target=target2)
        saved.append((y, *acts))

    small = [None] * n_layers
    parts = [None] * n_layers
    waiting = None
    for l in reversed(range(n_layers)):
        hosted = {} if waiting is None else dict(exchange=waiting[:2], allgather=waiting[2])
        dproj, *rest = _bwd_mix(dx, *saved[l], row(ln_v_g, l), row(ln_v_b, l), wcat[l], wtcat[l], bs[l], wo[l],
                                row(post_norm_g, l), **hosted)
        dx, dwt, dgpre = _bwd_in(dproj, xs[l], dx, row(pre_norm_g, l), wt[l])
        dwo, dws, dmix, dlng, dlnb, dgpost, dsink, dbias = rest[:8]
        if waiting is not None:
            parts[l + 1] = list(rest[8:])
        waiting = (dwt, dwo, dws.reshape(HEADS * CHUNK, CHUNK))
        small[l] = dict(pre_norm_g=dgpre[0], ln_v_g=dlng[0], ln_v_b=dlnb[0], dmix=dmix, sinks=dsink, dbias=dbias,
                        post_norm_g=dgpost[0])
    parts[0] = list(_exchange_grads(*waiting))

    db = _b_spatial_grad(jnp.stack([s["dmix"] for s in small]))
    pieces = {n: jnp.stack([s[n] for s in small]) for n in ("pre_norm_g", "ln_v_g", "ln_v_b", "sinks", "post_norm_g")}
    pieces["b_spatial"] = jnp.swapaxes(db[:, :, :HEADS], 1, 2)
    pieces["rel_bias"] = _rel_bias_grad(jnp.stack([s["dbias"] for s in small]), bucket)
    packed, first_rows = _pack([pieces[n] for n in _SMALL] + [loss])
    total = _allreduce_small(packed)
    grad = {n: _unpack(total, r, weights[n].shape) for n, r in zip(_SMALL, first_rows)}
    loss_out = total[first_rows[-1], 0]

    delta, new_m, new_v = {}, {}, {}
    for n in _SMALL:
        w = weights[n]
        two_d = (-1, w.shape[-1])
        d, nm, nv = _adamw(w.reshape(two_d), grad[n].reshape(two_d), mom_m[n].reshape(two_d), mom_v[n].reshape(two_d), n)
        delta[n], new_m[n], new_v[n] = d.reshape(w.shape), nm.reshape(w.shape), nv.reshape(w.shape)

    t3 = lambda p: jnp.swapaxes(p, 1, 2)
    flat3 = lambda p: p.reshape(n_layers, HEADS * CHUNK, CHUNK)
    for n, which, view, back in (("w_in", 0, t3, t3), ("w_out", 1, lambda p: p, lambda p: p),
                                 ("w_spatial", 2, flat3, lambda p: p.reshape(w_spatial.shape))):
        results = None
        for l in reversed(range(n_layers)):
            results = _adamw_parts(parts[l][which], view(weights[n]), view(mom_m[n]), view(mom_v[n]), l, results, n)
        grad[n], delta[n], new_m[n], new_v[n] = (back(r) for r in results)

    names = tuple(weights)
    return (loss_out, dx.reshape(x.shape), *[grad[n] for n in names], *[delta[n] for n in names],
            *[new_m[n] for n in names], *[new_v[n] for n in names])
```

```python
import math

import jax
import jax.numpy as jnp
from jax import lax
from jax.experimental import pallas as pl
from jax.experimental.pallas import tpu as pltpu

F32 = jnp.float32
BF16 = jnp.bfloat16

D_MODEL = 1024
A_WIDTH = 512
B_WIDTH = 512
KV_WIDTH = 128
IN_WIDTH = 3 * A_WIDTH + 2 * B_WIDTH + 2 * KV_WIDTH
CHUNK = 128
HEADS = 8
HEAD_DIM = 64
REL_BUCKETS = 32
NORM_EPS = 1e-6
NEG = -1e30
Q_OFF = 3 * A_WIDTH
K_OFF = Q_OFF + B_WIDTH
BZ_OFF = K_OFF + 2 * KV_WIDTH
QK_SCALE = HEAD_DIM ** -0.5

ADAM_LR = 0.001
ADAM_B1 = 0.9
ADAM_B2 = 0.999
ADAM_EPS = 1e-08
ADAM_WD = 0.01
ADAM_STEP = 10

TOK_TILE = 512
CHUNK_UNROLL = 4
BWD_CHUNK_UNROLL = 4
VMEM_LIMIT_V7X = 60 * 1024 * 1024

N_DEV = 8
N_CHIPS = 4
MESH_ID = pl.DeviceIdType.MESH
ANY = pl.BlockSpec(memory_space=pl.ANY)
SMEM = pl.BlockSpec(memory_space=pltpu.SMEM)


def _cparams(n_axes):
    return pltpu.CompilerParams(dimension_semantics=("arbitrary",) * n_axes, vmem_limit_bytes=VMEM_LIMIT_V7X)


_GELU_C = math.sqrt(2.0 / math.pi)
_GELU_C3 = _GELU_C * 0.044715


def _gelu(x):
    t = jnp.tanh(x * (_GELU_C3 * (x * x) + _GELU_C))
    return x * (0.5 * t + 0.5)


def _gelu_and_grad(x):
    x2 = x * x
    t = jnp.tanh(x * (_GELU_C3 * x2 + _GELU_C))
    cdf = 0.5 * t + 0.5
    d = cdf + (x * (cdf * (1.0 - cdf))) * ((6.0 * _GELU_C3) * x2 + 2.0 * _GELU_C)
    return x * cdf, d


def _sigmoid(x):
    return 0.5 + 0.5 * jnp.tanh(0.5 * x)


def _silu(x):
    return x * _sigmoid(x)


def _silu_and_grad(x):
    s = _sigmoid(x)
    return x * s, s * (1.0 + x * (1.0 - s))


def _dot(a, b):
    return jnp.dot(a, b, preferred_element_type=F32)


def _dot_nt(a, b):
    return lax.dot_general(a, b, (((1,), (1,)), ((), ())), preferred_element_type=F32)


def _dot_tn(a, b):
    return lax.dot_general(a, b, (((0,), (0,)), ((), ())), preferred_element_type=F32)


def _lo_mask(shape):
    return lax.broadcasted_iota(jnp.int32, shape, 1) < HEAD_DIM


def _swap_halves(v):
    return pltpu.roll(v, HEAD_DIM, 1)


def _window_buckets():
    q_loc = jnp.arange(CHUNK)[:, None]
    j_loc = jnp.arange(CHUNK)[None, :]
    d = q_loc - j_loc + jnp.where(j_loc > q_loc, CHUNK, 0)
    max_exact = REL_BUCKETS // 2
    safe = jnp.maximum(d, 1).astype(F32)
    large = max_exact + (jnp.log(safe / max_exact) / math.log(CHUNK / max_exact)
                         * (REL_BUCKETS - max_exact)).astype(jnp.int32)
    large = jnp.minimum(large, REL_BUCKETS - 1)
    return jnp.where(d < max_exact, d, large).astype(jnp.int32)


def _bias_table(rel_bias, bucket):
    def body(rb_ref, bk_ref, out_ref):
        bk = bk_ref[...]
        for h in range(HEADS):
            acc = jnp.zeros(bk.shape, F32)
            for b in range(REL_BUCKETS):
                acc = jnp.where(bk == b, rb_ref[b, h], acc)
            out_ref[h] = acc

    vmem = pl.BlockSpec(memory_space=pltpu.VMEM)
    return pl.pallas_call(
        body, name="bias_table",
        out_shape=jax.ShapeDtypeStruct((HEADS, CHUNK, CHUNK), F32),
        in_specs=[SMEM, vmem], out_specs=vmem,
    )(rel_bias, bucket)


def _rel_bias_grad(dbias, bucket):
    n_layers = dbias.shape[0]

    def body(db_ref, bk_ref, out_ref):
        bk = bk_ref[...]
        for h in range(HEADS):
            tot = db_ref[0, h]
            for l in range(1, n_layers):
                tot = tot + db_ref[l, h]
            for b in range(REL_BUCKETS):
                out_ref[b, h] = jnp.sum(jnp.where(bk == b, tot, 0.0))

    vmem = pl.BlockSpec(memory_space=pltpu.VMEM)
    return pl.pallas_call(
        body, name="rel_bias_grad",
        out_shape=jax.ShapeDtypeStruct((REL_BUCKETS, HEADS), F32),
        in_specs=[vmem] * 2, out_specs=SMEM,
    )(dbias, bucket)


def _spatial_tables(w_spatial, b_spatial_t):
    n_layers = w_spatial.shape[0]

    def body(w_ref, b_ref, wcat_ref, wtcat_ref, bs_ref):
        row = lax.broadcasted_iota(jnp.int32, (CHUNK, CHUNK), 0)
        col = lax.broadcasted_iota(jnp.int32, (CHUNK, CHUNK), 1)
        causal = col <= row
        lo = _lo_mask((CHUNK, CHUNK))
        for p in range(4):
            for half in range(2):
                w = jnp.where(causal, w_ref[0, 2 * p + half], 0.0)
                wcat_ref[0, p, :, half * CHUNK:(half + 1) * CHUNK] = w.astype(BF16)
                wtcat_ref[0, p, :, half * CHUNK:(half + 1) * CHUNK] = w.T.astype(BF16)
            b = b_ref[0]
            bs_ref[0, p] = jnp.where(lo, b[:, 2 * p:2 * p + 1], b[:, 2 * p + 1:2 * p + 2])

    return pl.pallas_call(
        body, name="spatial_tables", grid=(n_layers,),
        out_shape=(jax.ShapeDtypeStruct((n_layers, 4, CHUNK, 2 * CHUNK), BF16),
                   jax.ShapeDtypeStruct((n_layers, 4, CHUNK, 2 * CHUNK), BF16),
                   jax.ShapeDtypeStruct((n_layers, 4, CHUNK, CHUNK), F32)),
        in_specs=[pl.BlockSpec((1, HEADS, CHUNK, CHUNK), lambda l: (l, 0, 0, 0)),
                  pl.BlockSpec((1, CHUNK, HEADS), lambda l: (l, 0, 0))],
        out_specs=(pl.BlockSpec((1, 4, CHUNK, 2 * CHUNK), lambda l: (l, 0, 0, 0)),
                   pl.BlockSpec((1, 4, CHUNK, 2 * CHUNK), lambda l: (l, 0, 0, 0)),
                   pl.BlockSpec((1, 4, CHUNK, CHUNK), lambda l: (l, 0, 0, 0))),
        compiler_params=_cparams(1),
    )(w_spatial, b_spatial_t)


def _b_spatial_grad(dmix):
    n_layers = dmix.shape[0]

    def body(d_ref, out_ref):
        lane = lax.broadcasted_iota(jnp.int32, (CHUNK, CHUNK), 1)
        acc = jnp.zeros((CHUNK, CHUNK), F32)
        for p in range(4):
            t = d_ref[0, :, p * CHUNK:(p + 1) * CHUNK]
            s_lo = jnp.sum(jnp.where(lane < HEAD_DIM, t, 0.0), axis=1, keepdims=True)
            s_hi = jnp.sum(jnp.where(lane < HEAD_DIM, 0.0, t), axis=1, keepdims=True)
            acc = jnp.where(lane == 2 * p, s_lo, acc)
            acc = jnp.where(lane == 2 * p + 1, s_hi, acc)
        out_ref[0] = acc

    return pl.pallas_call(
        body, name="b_spatial_grad", grid=(n_layers,),
        out_shape=jax.ShapeDtypeStruct((n_layers, CHUNK, CHUNK), F32),
        in_specs=[pl.BlockSpec((1, CHUNK, A_WIDTH), lambda l: (l, 0, 0))],
        out_specs=pl.BlockSpec((1, CHUNK, CHUNK), lambda l: (l, 0, 0)),
        compiler_params=_cparams(1),
    )(dmix)


def _place():
    x, y, c = lax.axis_index("x"), lax.axis_index("y"), lax.axis_index("c")
    other_chips = [(1 - x, y), (x, 1 - y), (1 - x, 1 - y)]
    return x, y, c, other_chips


N_GATHER_SEMS = 12


def _gather_phases(shards, fulls, send_sems, recv_sems, local_sems):
    x, y, c, chips = _place()
    sibling = (x, y, 1 - c)
    n_arr = len(shards)

    def half_rows(a, chip, half):
        n = shards[a].shape[0]
        start = (2 * chip[0] + chip[1]) * n + half * (n // 2)
        return fulls[a].at[pl.ds(pl.multiple_of(start, 16), n // 2), :]

    def my_half(a):
        n = shards[a].shape[0]
        return shards[a].at[pl.ds(pl.multiple_of(c * (n // 2), 16), n // 2), :]

    def copy(k, a, src, chip, half, to):
        return pltpu.make_async_remote_copy(
            src_ref=src, dst_ref=half_rows(a, chip, half), send_sem=send_sems.at[n_arr * k + a],
            recv_sem=recv_sems.at[n_arr * k + a], device_id=to, device_id_type=MESH_ID)

    def local(a):
        n = shards[a].shape[0]
        mine = fulls[a].at[pl.ds(pl.multiple_of((2 * x + y) * n, 16), n), :]
        return pltpu.make_async_copy(shards[a], mine, local_sems.at[a])

    def first(k, a):
        return copy(k, a, my_half(a), (x, y), c, (chips[k][0], chips[k][1], c))

    def passed(k, a):
        return copy(3 + k, a, half_rows(a, chips[k], c), chips[k], c, sibling)

    def phase_a():
        for a in range(n_arr):
            local(a).start()
        for k in range(3):
            for a in range(n_arr):
                first(k, a).start()

    def phase_b():
        for k in range(3):
            for a in range(n_arr):
                copy(k, a, my_half(a), chips[k], c, sibling).wait_recv()
                passed(k, a).start()

    def phase_c():
        for k in range(3):
            for a in range(n_arr):
                copy(3 + k, a, my_half(a), chips[k], 1 - c, sibling).wait_recv()
        for k in range(3):
            for a in range(n_arr):
                first(k, a).wait_send()
                passed(k, a).wait_send()
        for a in range(n_arr):
            local(a).wait()

    return phase_a, phase_b, phase_c


N_EXCHANGE_SEMS = 14


def _exchange_phases(partials, parts, send_sems, recv_sems, local_sems, by_chip=True):
    x, y, c, chips = _place()
    me, sibling = (x, y, c), (x, y, 1 - c)
    n_arr = len(partials)

    def block(a, chip):
        if not by_chip:
            return partials[a]
        n = partials[a].shape[0] // N_CHIPS
        return partials[a].at[pl.ds(pl.multiple_of((2 * chip[0] + chip[1]) * n, 16), n), :]

    def slot(a, dev):
        return parts[a].at[4 * dev[0] + 2 * dev[1] + dev[2]]

    def copy(k, a, src, origin, to):
        return pltpu.make_async_remote_copy(
            src_ref=src, dst_ref=slot(a, origin), send_sem=send_sems.at[n_arr * k + a],
            recv_sem=recv_sems.at[n_arr * k + a], device_id=to, device_id_type=MESH_ID)

    def local(a):
        return pltpu.make_async_copy(block(a, (x, y)), slot(a, me), local_sems.at[a])

    def first(k, a):
        if k == 0:
            return copy(0, a, block(a, (x, y)), me, sibling)
        chip = chips[k - 1]
        return copy(k, a, block(a, chip), me, (chip[0], chip[1], c))

    def passed(k, a):
        origin = (chips[k][0], chips[k][1], c)
        return copy(4 + k, a, slot(a, origin), origin, sibling)

    def phase_a():
        for a in range(n_arr):
            local(a).start()
        for k in range(4):
            for a in range(n_arr):
                first(k, a).start()

    def phase_b():
        for k in range(3):
            for a in range(n_arr):
                copy(1 + k, a, block(a, (x, y)), (chips[k][0], chips[k][1], c), me).wait_recv()
                passed(k, a).start()

    def phase_c():
        for a in range(n_arr):
            copy(0, a, block(a, (x, y)), sibling, me).wait_recv()
        for k in range(3):
            for a in range(n_arr):
                copy(4 + k, a, block(a, (x, y)), (chips[k][0], chips[k][1], 1 - c), me).wait_recv()
        for k in range(4):
            for a in range(n_arr):
                first(k, a).wait_send()
        for k in range(3):
            for a in range(n_arr):
                passed(k, a).wait_send()
        for a in range(n_arr):
            local(a).wait()

    return phase_a, phase_b, phase_c


def _comm_scratch(n_sems):
    return [pltpu.SemaphoreType.DMA((n_sems,)), pltpu.SemaphoreType.DMA((n_sems,)), pltpu.SemaphoreType.DMA((2,))]


def _gather_weights(wt_shards, wo_shards, layer):
    wt_rows, wo_rows = wt_shards.shape[1], wo_shards.shape[1]

    def body(wt_ref, wo_ref, wt_full, wo_full, send_sems, recv_sems, local_sems):
        phases = _gather_phases((wt_ref.at[layer], wo_ref.at[layer]), (wt_full, wo_full), send_sems, recv_sems, local_sems)
        for phase in phases:
            phase()

    return pl.pallas_call(
        body, name="gather_weights",
        out_shape=(jax.ShapeDtypeStruct((N_CHIPS * wt_rows, D_MODEL), BF16),
                   jax.ShapeDtypeStruct((N_CHIPS * wo_rows, D_MODEL), BF16)),
        in_specs=[ANY, ANY], out_specs=(ANY, ANY), scratch_shapes=_comm_scratch(N_GATHER_SEMS),
    )(wt_shards, wo_shards)


HBM = pl.BlockSpec(memory_space=pltpu.HBM)
SEMS = pl.BlockSpec(memory_space=pltpu.SEMAPHORE)
N_SPLIT_COPIES = 12


def _split_copies(srcs, lands, send_sems, recv_sems):
    x, y, c, chips = _place()
    me, sibling = (x, y, c), (x, y, 1 - c)
    pairs = []
    for a, (src, land) in enumerate(zip(srcs, lands)):
        def block(chip, a=a, src=src):
            if a == 2:
                return src
            n = src.shape[0] // N_CHIPS
            return src.at[pl.ds(pl.multiple_of((2 * chip[0] + chip[1]) * n, 16), n), :]

        def copy(k, part, origin, to, a=a, land=land):
            return pltpu.make_async_remote_copy(
                src_ref=part, dst_ref=land.at[4 * origin[0] + 2 * origin[1] + origin[2]],
                send_sem=send_sems.at[3 * k + a], recv_sem=recv_sems.at[3 * k + a], device_id=to, device_id_type=MESH_ID)

        pairs.append((copy(0, block((x, y)), me, sibling), copy(0, block((x, y)), sibling, me)))
        for k, chip in enumerate(chips):
            peer = (chip[0], chip[1], c)
            pairs.append((copy(1 + k, block(chip), me, peer), copy(1 + k, block((x, y)), peer, me)))
    return pairs


def _exchange_start(dwt, dwo, small):
    lands = [lax.empty((N_DEV, dwt.shape[0] // N_CHIPS, D_MODEL), BF16),
             lax.empty((N_DEV, dwo.shape[0] // N_CHIPS, D_MODEL), BF16), lax.empty((N_DEV,) + small.shape, small.dtype)]
    flying = [pltpu.with_memory_space_constraint(v, pltpu.HBM) for v in [dwt, dwo, small] + lands]

    def body(dwt_ref, dwo_ref, small_ref, pt_ref, po_ref, all_ref, send_sems, recv_sems, *thru_and_token):
        for out, _ in _split_copies((dwt_ref, dwo_ref, small_ref), (pt_ref, po_ref, all_ref), send_sems, recv_sems):
            out.start()
        token = thru_and_token[-1]
        token[...] = jnp.zeros_like(token)

    return pl.pallas_call(
        body, name="exchange_start",
        out_shape=(pltpu.SemaphoreType.DMA((N_SPLIT_COPIES,)), pltpu.SemaphoreType.DMA((N_SPLIT_COPIES,)),
                   *[pltpu.HBM(v.shape, v.dtype) for v in flying], jax.ShapeDtypeStruct((8, 128), F32)),
        in_specs=[HBM] * 6, out_specs=(SEMS, SEMS, *[HBM] * 6, pl.BlockSpec(memory_space=pltpu.VMEM)),
        input_output_aliases={i: 2 + i for i in range(6)},
        compiler_params=pltpu.CompilerParams(has_side_effects=pltpu.SideEffectType.DATAFLOW_SIDE_EFFECTING),
    )(*flying)


def _exchange_wait(send_sems, recv_sems, flying, after):
    def body(dwt_ref, dwo_ref, small_ref, pt_ref, po_ref, all_ref, send_sems, recv_sems, *rest):
        for out, incoming in _split_copies((dwt_ref, dwo_ref, small_ref), (pt_ref, po_ref, all_ref), send_sems, recv_sems):
            out.wait_send()
            incoming.wait_recv()

    return pl.pallas_call(
        body, name="exchange_wait",
        out_shape=tuple(pltpu.HBM(v.shape, v.dtype) for v in flying),
        in_specs=[HBM] * 6 + [SEMS, SEMS] + [ANY] * len(after), out_specs=(HBM,) * 6,
        input_output_aliases={i: i for i in range(6)},
        compiler_params=pltpu.CompilerParams(has_side_effects=pltpu.SideEffectType.DATAFLOW_SIDE_EFFECTING),
    )(*flying, send_sems, recv_sems, *after)


def _exchange_finish(dwt, dwo, small, lands):
    def body(dwt_ref, dwo_ref, small_ref, pt_in, po_in, all_in, pt_ref, po_ref, all_ref, send_sems, recv_sems, local_sems):
        x, y, c, chips = _place()
        me, sibling = (x, y, c), (x, y, 1 - c)
        arrays = ((dwt_ref, pt_ref), (dwo_ref, po_ref), (small_ref, all_ref))

        def slot(a, dev):
            return arrays[a][1].at[4 * dev[0] + 2 * dev[1] + dev[2]]

        def mine(a):
            src = arrays[a][0]
            if a == 2:
                return src
            n = src.shape[0] // N_CHIPS
            return src.at[pl.ds(pl.multiple_of((2 * x + y) * n, 16), n), :]

        def passed(k, a, origin_core):
            origin = (chips[k][0], chips[k][1], origin_core)
            return pltpu.make_async_remote_copy(
                src_ref=slot(a, origin), dst_ref=slot(a, origin), send_sem=send_sems.at[3 * k + a],
                recv_sem=recv_sems.at[3 * k + a], device_id=sibling, device_id_type=MESH_ID)

        local = [pltpu.make_async_copy(mine(a), slot(a, me), local_sems.at[a]) for a in range(3)]
        for cp in local:
            cp.start()
        for k in range(3):
            for a in range(3):
                passed(k, a, c).start()
        for k in range(3):
            for a in range(3):
                passed(k, a, 1 - c).wait_recv()
                passed(k, a, c).wait_send()
        for cp in local:
            cp.wait()

    return pl.pallas_call(
        body, name="exchange_finish",
        out_shape=tuple(jax.ShapeDtypeStruct(v.shape, v.dtype) for v in lands),
        in_specs=[ANY] * 6, out_specs=(ANY,) * 3, input_output_aliases={3: 0, 4: 1, 5: 2},
        scratch_shapes=[pltpu.SemaphoreType.DMA((9,)), pltpu.SemaphoreType.DMA((9,)), pltpu.SemaphoreType.DMA((3,))],
    )(dwt, dwo, small, *lands)


def _allreduce_small(part):
    n_rows = part.shape[0]

    def body(p_ref, tot_ref, all_ref, send_sems, recv_sems, local_sem):
        x, y, c, chips = _place()
        me, sibling = (x, y, c), (x, y, 1 - c)

        def rows(dev):
            return all_ref.at[pl.ds(pl.multiple_of((4 * dev[0] + 2 * dev[1] + dev[2]) * n_rows, 8), n_rows), :]

        def copy(k, origin, to, src=None):
            return pltpu.make_async_remote_copy(
                src_ref=rows(origin) if src is None else src, dst_ref=rows(origin), send_sem=send_sems.at[k],
                recv_sem=recv_sems.at[k], device_id=to, device_id_type=MESH_ID)

        mine = pltpu.make_async_copy(p_ref, rows(me), local_sem)
        mine.start()
        first = [copy(0, me, sibling, src=p_ref)]
        first += [copy(1 + k, me, (chip[0], chip[1], c), src=p_ref) for k, chip in enumerate(chips)]
        for cp in first:
            cp.start()
        passed = []
        for k, chip in enumerate(chips):
            origin = (chip[0], chip[1], c)
            copy(1 + k, origin, me).wait_recv()
            fwd = copy(4 + k, origin, sibling)
            fwd.start()
            passed.append(fwd)
        copy(0, sibling, me).wait_recv()
        for k, chip in enumerate(chips):
            copy(4 + k, (chip[0], chip[1], 1 - c), me).wait_recv()
        for cp in first + passed:
            cp.wait_send()
        mine.wait()
        tot = all_ref[0:n_rows, :]
        for d in range(1, N_DEV):
            tot = tot + all_ref[d * n_rows:(d + 1) * n_rows, :]
        tot_ref[...] = tot

    vmem = pl.BlockSpec(memory_space=pltpu.VMEM)
    return pl.pallas_call(
        body, name="allreduce_small",
        out_shape=jax.ShapeDtypeStruct((n_rows, 128), F32),
        in_specs=[vmem], out_specs=vmem,
        scratch_shapes=[pltpu.VMEM((N_DEV * n_rows, 128), F32), pltpu.SemaphoreType.DMA((7,)),
                        pltpu.SemaphoreType.DMA((7,)), pltpu.SemaphoreType.DMA],
        compiler_params=pltpu.CompilerParams(vmem_limit_bytes=VMEM_LIMIT_V7X),
    )(part)


def _hosted(phases, step, n_steps):
    phase_a, phase_b, phase_c = phases

    def at_start():
        pl.when(step == 0)(phase_a)

    def at_end():
        pl.when(step == (3 * n_steps) // 4)(phase_b)
        pl.when(step == n_steps - 1)(phase_c)

    return at_start, at_end


def _layer_norm_stats(vv):
    mu = jnp.mean(vv, axis=-1, keepdims=True)
    xc = vv - mu
    rs = lax.rsqrt(jnp.mean(xc * xc, axis=-1, keepdims=True) + NORM_EPS)
    return xc * rs, rs


def _blockdiag(v, lo):
    zero = jnp.zeros_like(v)
    return jnp.concatenate([jnp.where(lo, v, zero), jnp.where(lo, zero, v)], axis=0)


def _softmax_sink(s, sink):
    m = jnp.maximum(jnp.max(s, axis=-1, keepdims=True), sink)
    e = jnp.exp(s - m)
    esink = jnp.exp(sink - m)
    den = jnp.sum(e, axis=-1, keepdims=True) + esink
    return e / den


def _kv_rows(cur_ref, halo_ref, r0, c):
    prev_in_tile = cur_ref[pl.ds(pl.multiple_of(jnp.maximum(r0 - CHUNK, 0), CHUNK), CHUNK), :]
    prev = jnp.where(c == 0, halo_ref[...], prev_in_tile)
    kv2 = jnp.concatenate([prev, cur_ref[pl.ds(r0, CHUNK), :]], axis=0)
    k2, v2 = kv2[:, 0:KV_WIDTH], kv2[:, KV_WIDTH:2 * KV_WIDTH]
    return (k2, _swap_halves(k2)), (v2, _swap_halves(v2))


def _window_square(over_keys, prev):
    return jnp.where(prev, over_keys[:, 0:CHUNK], over_keys[:, CHUNK:2 * CHUNK])


def _window_keys(square, prev):
    zero = jnp.zeros_like(square)
    return jnp.concatenate([jnp.where(prev, square, zero), jnp.where(prev, zero, square)], axis=1)


def _ahead(shape=(CHUNK, CHUNK)):
    return lax.broadcasted_iota(jnp.int32, shape, 1) - lax.broadcasted_iota(jnp.int32, shape, 0)


def _dead_mask(first, ahead):
    return ahead > jnp.where(first, 0, CHUNK)


def _head_of(p, half):
    return 2 * p + half, int(half != p // 2)


_HEADS_BY_COPY = tuple(tuple((p, half) for p in range(4) for half in range(2) if _head_of(p, half)[1] == sw)
                       for sw in range(2))


def _masked_halves(tile, lo):
    zero = jnp.zeros_like(tile)
    return {0: jnp.where(lo, tile, zero), 1: jnp.where(lo, zero, tile)}


def _query_stacks(q_tiles, lo):
    qm = {p: _masked_halves(q_tiles[p] * QK_SCALE, lo) for p in range(4)}
    return [jnp.concatenate([qm[p][half] for p, half in members], axis=0) for members in _HEADS_BY_COPY]


def _attention_probs(q_stacks, ks, bias_ref, sink_ref, dead, prev):
    probs = {}
    for sw, members in enumerate(_HEADS_BY_COPY):
        s_stack = _dot_nt(q_stacks[sw], ks[sw])
        for i, (p, half) in enumerate(members):
            head = 2 * p + half
            s = _window_square(s_stack[i * CHUNK:(i + 1) * CHUNK, :], prev) + bias_ref[head]
            s = jnp.where(dead, NEG, s)
            probs[p, half] = _softmax_sink(s, sink_ref[head]).astype(BF16)
    return probs


def _attention_values(probs, vs, prev):
    outs, p_stacks = {}, []
    for sw, members in enumerate(_HEADS_BY_COPY):
        p_stack = jnp.concatenate([_window_keys(probs[m], prev) for m in members], axis=0)
        r_stack = _dot(p_stack, vs[sw])
        p_stacks.append(p_stack)
        for i, m in enumerate(members):
            outs[m] = r_stack[i * CHUNK:(i + 1) * CHUNK, :]
    return outs, p_stacks


def _fwd_layer(x, g_pre, w_in_t, ln_g, ln_b, wcat, bs, sinks, bias, w_out, g_post, seq_len, gather=None, target=None):
    n_tok = x.shape[0]
    tb = TOK_TILE
    n_chunks = tb // CHUNK
    n_tiles = n_tok // tb
    n_in = 11
    assert gather is None or target is None

    def body(*refs):
        (x_ref, gpre_ref, wt_ref, lng_ref, lnb_ref, wcat_ref, bs_ref, sink_ref, bias_ref, wout_ref,
         gpost_ref) = refs[:n_in]
        at_start = at_end = lambda: None
        if gather is not None:
            (wts_ref, wos_ref, a_ref, q_ref, kv_ref, bz_ref, probs_ref, xn_ref, y_ref, wt_full, wo_full, ycat_s, halo_ref,
             send_sems, recv_sems, local_sems) = refs[n_in:]
            phases = _gather_phases((wts_ref.at[gather[2]], wos_ref.at[gather[2]]), (wt_full, wo_full), send_sems,
                                    recv_sems, local_sems)
            at_start, at_end = _hosted(phases, pl.program_id(0), n_tiles)
        elif target is not None:
            target_ref, a_ref, q_ref, kv_ref, bz_ref, probs_ref, xn_ref, y_ref, loss_ref, ycat_s, halo_ref = refs[n_in:]
        else:
            a_ref, q_ref, kv_ref, bz_ref, probs_ref, xn_ref, y_ref, ycat_s, halo_ref = refs[n_in:]
        at_start()
        i = pl.program_id(0)
        lo = _lo_mask((CHUNK, CHUNK))
        ahead = _ahead()
        prev = ahead > 0

        @pl.when(i == 0)
        def _():
            halo_ref[...] = jnp.zeros_like(halo_ref)

        xf = x_ref[...]
        r1 = lax.rsqrt(jnp.mean(xf * xf, axis=-1, keepdims=True) + NORM_EPS)
        h = ((xf * r1) * gpre_ref[...]).astype(BF16)
        a_ref[...] = _dot_nt(h, wt_ref[0:Q_OFF, :])
        q_ref[...] = _dot_nt(h, wt_ref[Q_OFF:K_OFF, :]).astype(BF16)
        kv_ref[...] = _dot_nt(h, wt_ref[K_OFF:BZ_OFF, :]).astype(BF16)
        bz_ref[...] = _dot_nt(h, wt_ref[BZ_OFF:IN_WIDTH, :])

        def chunk(c, carry):
            r0 = pl.multiple_of(c * CHUNK, CHUNK)
            rows = pl.ds(r0, CHUNK)
            u = _gelu(a_ref[rows, 0:A_WIDTH])
            vv = _gelu(a_ref[rows, A_WIDTH:2 * A_WIDTH])
            xhat, _ = _layer_norm_stats(vv)
            vnb = (xhat * lng_ref[...] + lnb_ref[...]).astype(BF16)
            for p in range(4):
                blk = slice(p * CHUNK, (p + 1) * CHUNK)
                mixed = _dot(wcat_ref[p], _blockdiag(vnb[:, blk], lo)) + bs_ref[p]
                sz = _silu(a_ref[rows, 2 * A_WIDTH + p * CHUNK:2 * A_WIDTH + (p + 1) * CHUNK])
                ycat_s[rows, blk] = ((u[:, blk] * mixed) * sz).astype(BF16)
            ks, vs = _kv_rows(kv_ref, halo_ref, r0, c)
            dead = _dead_mask(lax.rem(i * tb + r0, seq_len) == 0, ahead)
            q_tiles = [q_ref[rows, p * CHUNK:(p + 1) * CHUNK] for p in range(4)]
            probs = _attention_probs(_query_stacks(q_tiles, lo), ks, bias_ref, sink_ref, dead, prev)
            for (p, half), prob in probs.items():
                head = 2 * p + half
                probs_ref[rows, head * CHUNK:(head + 1) * CHUNK] = prob
            outs, _ = _attention_values(probs, vs, prev)
            for p in range(4):
                blk = slice(p * CHUNK, (p + 1) * CHUNK)
                o = jnp.where(lo, outs[p, 0], outs[p, 1])
                ycat_s[rows, B_WIDTH + p * CHUNK:B_WIDTH + (p + 1) * CHUNK] = (o * _silu(bz_ref[rows, blk])).astype(BF16)
            return carry

        lax.fori_loop(0, n_chunks, chunk, 0, unroll=CHUNK_UNROLL)
        halo_ref[...] = kv_ref[tb - CHUNK:tb, :]
        y = _dot(ycat_s[...], wout_ref[...])
        r = lax.rsqrt(jnp.mean(y * y, axis=-1, keepdims=True) + NORM_EPS)
        y_ref[...] = y
        xn = x_ref[...] + (y * r) * gpost_ref[...]
        if target is None:
            xn_ref[...] = xn
        else:
            d = xn - target_ref[...]
            xn_ref[...] = d * (1.0 / D_MODEL)

            @pl.when(i == 0)
            def _():
                loss_ref[0, 0] = 0.0

            loss_ref[0, 0] += 0.5 * jnp.sum(jnp.mean(d * d, axis=-1, keepdims=True))
        at_end()

    tile = lambda w: pl.BlockSpec((tb, w), lambda i: (i, 0))
    whole = lambda shape, **kw: pl.BlockSpec(shape, lambda i: (0,) * len(shape), **kw)
    in_specs = [tile(D_MODEL), whole((1, D_MODEL)), whole((IN_WIDTH, D_MODEL), pipeline_mode=pl.Buffered(1)),
                whole((1, A_WIDTH)), whole((1, A_WIDTH)), whole((4, CHUNK, 2 * CHUNK)), whole((4, CHUNK, CHUNK)), SMEM,
                whole((HEADS, CHUNK, CHUNK)), whole((D_MODEL, D_MODEL)), whole((1, D_MODEL))]
    out_shape = [jax.ShapeDtypeStruct((n_tok, Q_OFF), F32), jax.ShapeDtypeStruct((n_tok, B_WIDTH), BF16),
                 jax.ShapeDtypeStruct((n_tok, 2 * KV_WIDTH), BF16), jax.ShapeDtypeStruct((n_tok, B_WIDTH), F32),
                 jax.ShapeDtypeStruct((n_tok, HEADS * CHUNK), BF16), jax.ShapeDtypeStruct((n_tok, D_MODEL), F32),
                 jax.ShapeDtypeStruct((n_tok, D_MODEL), F32)]
    out_specs = [tile(Q_OFF), tile(B_WIDTH), tile(2 * KV_WIDTH), tile(B_WIDTH), tile(HEADS * CHUNK), tile(D_MODEL),
                 tile(D_MODEL)]
    scratch = [pltpu.VMEM((tb, D_MODEL), BF16), pltpu.VMEM((CHUNK, 2 * KV_WIDTH), BF16)]
    args = [x, g_pre, w_in_t, ln_g, ln_b, wcat, bs, sinks, bias, w_out, g_post]
    if gather is not None:
        wt_shards, wo_shards, _ = gather
        in_specs += [ANY, ANY]
        args += [wt_shards, wo_shards]
        out_shape += [jax.ShapeDtypeStruct((N_CHIPS * wt_shards.shape[1], D_MODEL), BF16),
                      jax.ShapeDtypeStruct((N_CHIPS * wo_shards.shape[1], D_MODEL), BF16)]
        out_specs += [ANY, ANY]
        scratch += _comm_scratch(N_GATHER_SEMS)
    if target is not None:
        in_specs.append(tile(D_MODEL))
        args.append(target)
        out_shape.append(jax.ShapeDtypeStruct((1, 1), F32))
        out_specs.append(SMEM)
    name = "fwd_layer" + ("" if gather is None else "_gather") + ("" if target is None else "_loss")
    return pl.pallas_call(
        body, name=name, grid=(n_tiles,),
        out_shape=tuple(out_shape), in_specs=in_specs, out_specs=tuple(out_specs), scratch_shapes=scratch,
        compiler_params=_cparams(1),
    )(*args)


def _bwd_mix(dout, y, a, q, kv, bz, probs, ln_g, ln_b, wcat, wtcat, bs, w_out, g_post, exchange=None, allgather=None):
    n_tok = y.shape[0]
    tb = TOK_TILE
    n_chunks = tb // CHUNK
    n_tiles = n_tok // tb

    def body(*refs):
        refs = list(refs)
        take = lambda n: [refs.pop(0) for _ in range(n)]
        (dout_ref, y_ref, a_ref, q_ref, kv_ref, halo_ref, bz_ref, probs_ref, lng_ref, lnb_ref, wcat_ref, wtcat_ref, bs_ref,
         wout_ref, gpost_ref) = take(15)
        if exchange is not None:
            dwt_in, dwo_in = take(2)
        if allgather is not None:
            (small_in,) = take(1)
        dproj_ref, dwout_ref, dws_ref, dmix_ref, dlng_ref, dlnb_ref, dgpost_ref, dsink_ref, dbias_ref = take(9)
        if exchange is not None:
            pt_ref, po_ref = take(2)
        if allgather is not None:
            (small_all,) = take(1)
        ycat_s, dy_s, dyc_s, dkv_s, carry_s, dwout_s, stage_s = take(7)
        starts, ends = [], []
        if exchange is not None:
            hosted = _hosted(_exchange_phases((dwt_in, dwo_in), (pt_ref, po_ref), *take(3)), pl.program_id(0), n_tiles)
            starts.append(hosted[0])
            ends.append(hosted[1])
        if allgather is not None:
            hosted = _hosted(_exchange_phases((small_in,), (small_all,), *take(3), by_chip=False), pl.program_id(0),
                             n_tiles)
            starts.append(hosted[0])
            ends.append(hosted[1])
        at_start = lambda: [f() for f in starts]
        at_end = lambda: [f() for f in ends]
        at_start()
        step = pl.program_id(0)
        lo = _lo_mask((CHUNK, CHUNK))
        prev = _ahead() > 0

        @pl.when(step == 0)
        def _():
            dwout_s[...] = jnp.zeros_like(dwout_s)
            dws_ref[...] = jnp.zeros_like(dws_ref)
            dmix_ref[...] = jnp.zeros_like(dmix_ref)
            dlng_ref[...] = jnp.zeros_like(dlng_ref)
            dlnb_ref[...] = jnp.zeros_like(dlnb_ref)
            dgpost_ref[...] = jnp.zeros_like(dgpost_ref)
            dbias_ref[...] = jnp.zeros_like(dbias_ref)
            carry_s[...] = jnp.zeros_like(carry_s)
            for h in range(HEADS):
                dsink_ref[h] = 0.0

        for r0 in range(0, tb, CHUNK):
            rows = slice(r0, r0 + CHUNK)
            yv = y_ref[rows, :]
            dout = dout_ref[rows, :]
            r = lax.rsqrt(jnp.mean(yv * yv, axis=-1, keepdims=True) + NORM_EPS)
            yn = yv * r
            dgpost_ref[...] += jnp.sum(dout * yn, axis=0, keepdims=True)
            dyn = dout * gpost_ref[...]
            dy_s[rows, :] = (r * (dyn - yn * jnp.mean(dyn * yn, axis=-1, keepdims=True))).astype(BF16)
        dyc_s[...] = _dot_nt(dy_s[...], wout_ref[...])
        dkv_s[0:tb, :] = jnp.zeros((tb, 2 * KV_WIDTH), F32)
        dkv_s[tb:tb + CHUNK, :] = carry_s[...]

        def chunk(c, carry):
            r0 = pl.multiple_of(c * CHUNK, CHUNK)
            rows = pl.ds(r0, CHUNK)
            u, gu = _gelu_and_grad(a_ref[rows, 0:A_WIDTH])
            vv, gv = _gelu_and_grad(a_ref[rows, A_WIDTH:2 * A_WIDTH])
            xhat, rs = _layer_norm_stats(vv)
            vnb = (xhat * lng_ref[...] + lnb_ref[...]).astype(BF16)
            d_vn, d_u, d_az = [], [], []
            for p in range(4):
                blk = slice(p * CHUNK, (p + 1) * CHUNK)
                vbd = _blockdiag(vnb[:, blk], lo)
                mixed = _dot(wcat_ref[p], vbd) + bs_ref[p]
                sz, gz = _silu_and_grad(a_ref[rows, 2 * A_WIDTH + p * CHUNK:2 * A_WIDTH + (p + 1) * CHUNK])
                ub = u[:, blk]
                dya = dyc_s[rows, blk]
                um = ub * mixed
                ycat_s[rows, blk] = (um * sz).astype(BF16)
                d_mixed = (dya * ub) * sz
                d_u.append((dya * mixed) * sz)
                d_az.append((dya * um) * gz)
                dmix_ref[:, blk] += d_mixed
                dmbd = _blockdiag(d_mixed.astype(BF16), lo)
                d_vn.append(_dot(wtcat_ref[p], dmbd))
                dws_ref[p] += _dot_nt(dmbd, vnb[:, blk])
            d_vn = jnp.concatenate(d_vn, axis=1)
            dlng_ref[...] += jnp.sum(d_vn * xhat, axis=0, keepdims=True)
            dlnb_ref[...] += jnp.sum(d_vn, axis=0, keepdims=True)
            dxh = d_vn * lng_ref[...]
            d_vv = rs * (dxh - jnp.mean(dxh, axis=-1, keepdims=True)
                         - xhat * jnp.mean(dxh * xhat, axis=-1, keepdims=True))
            dproj_ref[rows, 0:A_WIDTH] = (jnp.concatenate(d_u, axis=1) * gu).astype(BF16)
            dproj_ref[rows, A_WIDTH:2 * A_WIDTH] = (d_vv * gv).astype(BF16)
            dproj_ref[rows, 2 * A_WIDTH:Q_OFF] = jnp.concatenate(d_az, axis=1).astype(BF16)
            ks, vs = _kv_rows(kv_ref, halo_ref, r0, c)
            q_stacks = _query_stacks([q_ref[rows, p * CHUNK:(p + 1) * CHUNK] for p in range(4)], lo)
            probs = {(p, half): probs_ref[rows, (2 * p + half) * CHUNK:(2 * p + half + 1) * CHUNK]
                     for p in range(4) for half in range(2)}
            outs_h, p_stacks = _attention_values(probs, vs, prev)
            dom = {}
            for p in range(4):
                blk = slice(p * CHUNK, (p + 1) * CHUNK)
                sb, gb = _silu_and_grad(bz_ref[rows, blk])
                dyb = dyc_s[rows, B_WIDTH + p * CHUNK:B_WIDTH + (p + 1) * CHUNK]
                o = jnp.where(lo, outs_h[p, 0], outs_h[p, 1])
                ycat_s[rows, B_WIDTH + p * CHUNK:B_WIDTH + (p + 1) * CHUNK] = (o * sb).astype(BF16)
                dproj_ref[rows, BZ_OFF + p * CHUNK:BZ_OFF + (p + 1) * CHUNK] = ((dyb * o) * gb).astype(BF16)
                dom[p] = _masked_halves((dyb * sb).astype(BF16), lo)
            dqs, dk_by_copy, dv_by_copy = {}, [], []
            for sw, members in enumerate(_HEADS_BY_COPY):
                do_stack = jnp.concatenate([dom[p][half] for p, half in members], axis=0)
                dp_stack = _dot_nt(do_stack, vs[sw])
                dsws = []
                for i, (p, half) in enumerate(members):
                    head = 2 * p + half
                    dp = _window_square(dp_stack[i * CHUNK:(i + 1) * CHUNK, :], prev)
                    prob = probs[p, half].astype(F32)
                    delta = jnp.sum(prob * dp, axis=-1, keepdims=True)
                    ds = prob * (dp - delta)
                    dsink_ref[head] += -jnp.sum((1.0 - jnp.sum(prob, axis=-1, keepdims=True)) * delta)
                    dbias_ref[head] += ds
                    dsws.append(_window_keys(ds.astype(BF16), prev))
                ds_stack = jnp.concatenate(dsws, axis=0)
                dq_stack = _dot(ds_stack, ks[sw])
                for i, m in enumerate(members):
                    dqs[m] = dq_stack[i * CHUNK:(i + 1) * CHUNK, :]
                dk_by_copy.append(_dot_tn(ds_stack, q_stacks[sw]))
                dv_by_copy.append(_dot_tn(p_stacks[sw], do_stack))
            for p in range(4):
                dq = jnp.where(lo, dqs[p, 0], dqs[p, 1]) * QK_SCALE
                dproj_ref[rows, Q_OFF + p * CHUNK:Q_OFF + (p + 1) * CHUNK] = dq.astype(BF16)
            both = pl.ds(r0, 2 * CHUNK)
            dkv_s[both, 0:KV_WIDTH] += dk_by_copy[0] + _swap_halves(dk_by_copy[1])
            dkv_s[both, KV_WIDTH:2 * KV_WIDTH] += dv_by_copy[0] + _swap_halves(dv_by_copy[1])
            return carry

        lax.fori_loop(0, n_chunks, chunk, 0, unroll=BWD_CHUNK_UNROLL)
        dwout_s[...] += _dot_tn(ycat_s[...], dy_s[...])
        dproj_ref[:, K_OFF:BZ_OFF] = dkv_s[CHUNK:CHUNK + tb, :].astype(BF16)
        carry_s[...] = dkv_s[0:CHUNK, :]

        @pl.when(step == n_tiles - 1)
        def _():
            row = lax.broadcasted_iota(jnp.int32, (2 * CHUNK, CHUNK), 0)
            col = lax.broadcasted_iota(jnp.int32, (2 * CHUNK, CHUNK), 1)
            causal = col <= jnp.where(row >= CHUNK, row - CHUNK, row)
            for p in range(4):
                dws_ref[p] = jnp.where(causal, dws_ref[p], 0.0)
            rows = stage_s.shape[0]
            for r0 in range(0, D_MODEL, rows):
                stage_s[...] = dwout_s[r0:r0 + rows, :].astype(BF16)
                pltpu.sync_copy(stage_s, dwout_ref.at[r0:r0 + rows, :])

        at_end()

    tile = lambda w: pl.BlockSpec((tb, w), lambda s: (n_tiles - 1 - s, 0))
    whole = lambda shape, **kw: pl.BlockSpec(shape, lambda s: (0,) * len(shape), **kw)
    once = dict(pipeline_mode=pl.Buffered(1))
    in_specs = [tile(D_MODEL), tile(D_MODEL), tile(Q_OFF), tile(B_WIDTH), tile(2 * KV_WIDTH),
                pl.BlockSpec((CHUNK, 2 * KV_WIDTH), lambda s: (jnp.maximum((n_tiles - 1 - s) * n_chunks - 1, 0), 0)),
                tile(B_WIDTH), tile(HEADS * CHUNK), whole((1, A_WIDTH)), whole((1, A_WIDTH)), whole((4, CHUNK, 2 * CHUNK)),
                whole((4, CHUNK, 2 * CHUNK)), whole((4, CHUNK, CHUNK)), whole((D_MODEL, D_MODEL), **once),
                whole((1, D_MODEL))]
    args = [dout, y, a, q, kv, kv, bz, probs, ln_g, ln_b, wcat, wtcat, bs, w_out, g_post]
    small_shapes = [jax.ShapeDtypeStruct((D_MODEL, D_MODEL), BF16), jax.ShapeDtypeStruct((4, 2 * CHUNK, CHUNK), F32),
                    jax.ShapeDtypeStruct((CHUNK, A_WIDTH), F32), jax.ShapeDtypeStruct((1, A_WIDTH), F32),
                    jax.ShapeDtypeStruct((1, A_WIDTH), F32), jax.ShapeDtypeStruct((1, D_MODEL), F32),
                    jax.ShapeDtypeStruct((HEADS,), F32), jax.ShapeDtypeStruct((HEADS, CHUNK, CHUNK), F32)]
    small_specs = [ANY, whole((4, 2 * CHUNK, CHUNK)), whole((CHUNK, A_WIDTH)), whole((1, A_WIDTH)), whole((1, A_WIDTH)),
                   whole((1, D_MODEL)), SMEM, whole((HEADS, CHUNK, CHUNK))]
    scratch = [pltpu.VMEM((tb, D_MODEL), BF16), pltpu.VMEM((tb, D_MODEL), BF16), pltpu.VMEM((tb, D_MODEL), F32),
               pltpu.VMEM((tb + CHUNK, 2 * KV_WIDTH), F32), pltpu.VMEM((CHUNK, 2 * KV_WIDTH), F32),
               pltpu.VMEM((D_MODEL, D_MODEL), F32), pltpu.VMEM((2 * CHUNK, D_MODEL), BF16)]
    out_shape = [jax.ShapeDtypeStruct((n_tok, IN_WIDTH), BF16)] + small_shapes
    out_specs = [tile(IN_WIDTH)] + small_specs
    if exchange is not None:
        dwt, dwo = exchange
        in_specs += [ANY, ANY]
        args += [dwt, dwo]
        out_shape += [jax.ShapeDtypeStruct((N_DEV, dwt.shape[0] // N_CHIPS, D_MODEL), BF16),
                      jax.ShapeDtypeStruct((N_DEV, dwo.shape[0] // N_CHIPS, D_MODEL), BF16)]
        out_specs += [ANY, ANY]
    if allgather is not None:
        in_specs.append(ANY)
        args.append(allgather)
        out_shape.append(jax.ShapeDtypeStruct((N_DEV,) + allgather.shape, allgather.dtype))
        out_specs.append(ANY)
    for hosted in (exchange, allgather):
        if hosted is not None:
            scratch += _comm_scratch(N_EXCHANGE_SEMS)
    return pl.pallas_call(
        body, name="bwd_mix" + ("" if exchange is None and allgather is None else "_exchange"), grid=(n_tiles,),
        out_shape=tuple(out_shape), in_specs=in_specs, out_specs=tuple(out_specs), scratch_shapes=scratch,
        compiler_params=_cparams(1),
    )(*args)


def _bwd_in(dproj, x, dout, g_pre, w_in_t):
    n_tok = x.shape[0]
    tm = TOK_TILE
    n_tiles = n_tok // tm

    def body(dp_ref, x_ref, dout_ref, g_ref, wt_ref, dx_ref, dwt_ref, dg_ref, acc_s):
        i = pl.program_id(0)

        @pl.when(i == 0)
        def _():
            acc_s[...] = jnp.zeros_like(acc_s)
            dg_ref[...] = jnp.zeros_like(dg_ref)

        xf = x_ref[...]
        r = lax.rsqrt(jnp.mean(xf * xf, axis=-1, keepdims=True) + NORM_EPS)
        xn = xf * r
        h = (xn * g_ref[...]).astype(BF16)
        dp = dp_ref[...]
        dh = _dot(dp, wt_ref[...])
        acc_s[...] += _dot_tn(dp, h)
        dg_ref[...] += jnp.sum(dh * xn, axis=0, keepdims=True)
        dhn = dh * g_ref[...]
        dx_ref[...] = dout_ref[...] + r * (dhn - xn * jnp.mean(dhn * xn, axis=-1, keepdims=True))

        @pl.when(i == n_tiles - 1)
        def _():
            dwt_ref[...] = acc_s[...].astype(BF16)

    return pl.pallas_call(
        body, name="bwd_in", grid=(n_tiles,),
        out_shape=(jax.ShapeDtypeStruct((n_tok, D_MODEL), F32), jax.ShapeDtypeStruct((IN_WIDTH, D_MODEL), BF16),
                   jax.ShapeDtypeStruct((1, D_MODEL), F32)),
        in_specs=[pl.BlockSpec((tm, IN_WIDTH), lambda i: (i, 0)), pl.BlockSpec((tm, D_MODEL), lambda i: (i, 0)),
                  pl.BlockSpec((tm, D_MODEL), lambda i: (i, 0)), pl.BlockSpec((1, D_MODEL), lambda i: (0, 0)),
                  pl.BlockSpec((IN_WIDTH, D_MODEL), lambda i: (0, 0), pipeline_mode=pl.Buffered(1))],
        out_specs=(pl.BlockSpec((tm, D_MODEL), lambda i: (i, 0)),
                   pl.BlockSpec((IN_WIDTH, D_MODEL), lambda i: (0, 0), pipeline_mode=pl.Buffered(1)),
                   pl.BlockSpec((1, D_MODEL), lambda i: (0, 0))),
        scratch_shapes=[pltpu.VMEM((IN_WIDTH, D_MODEL), F32)],
        compiler_params=_cparams(1),
    )(dproj, x, dout, g_pre, w_in_t)


def _adam_update(w, g, m, v):
    nm = ADAM_B1 * m + (1.0 - ADAM_B1) * g
    nv = ADAM_B2 * v + (1.0 - ADAM_B2) * (g * g)
    m_hat = nm / (1.0 - ADAM_B1 ** ADAM_STEP)
    v_hat = nv / (1.0 - ADAM_B2 ** ADAM_STEP)
    return -ADAM_LR * (m_hat / (jnp.sqrt(v_hat) + ADAM_EPS) + ADAM_WD * w), nm, nv


def _sum_slots(p_ref):
    tot = p_ref[0].astype(F32)
    for d in range(1, N_DEV):
        tot = tot + p_ref[d].astype(F32)
    return tot


def _adamw_parts(parts, w, m, v, layer, results, name, after=None):
    n_layers, n_rows, n_cols = w.shape
    tr = n_rows // 2 if n_rows * n_cols > 512 * 1024 else n_rows

    def body(*refs):
        p_ref, w_ref, m_ref, v_ref = refs[:4]
        g_ref, d_ref, nm_ref, nv_ref = refs[-4:]
        g = _sum_slots(p_ref)
        g_ref[0] = g
        d_ref[0], nm_ref[0], nv_ref[0] = _adam_update(w_ref[0], g, m_ref[0], v_ref[0])

    spec = pl.BlockSpec((1, tr, n_cols), lambda i: (layer, i, 0))
    shape = jax.ShapeDtypeStruct((n_layers, n_rows, n_cols), F32)
    kept = [] if results is None else list(results)
    order = [] if after is None else [after]
    return pl.pallas_call(
        body, name="adamw_" + name, grid=(n_rows // tr,), out_shape=(shape,) * 4,
        in_specs=[pl.BlockSpec((N_DEV, tr, n_cols), lambda i: (0, i, 0)), spec, spec, spec] + [ANY] * len(kept + order),
        out_specs=(spec,) * 4, input_output_aliases={4 + j: j for j in range(len(kept))},
        compiler_params=_cparams(1),
    )(parts, w, m, v, *kept, *order)


def _adamw(w, g, m, v, name):
    n_rows, n_cols = w.shape
    tr = 512 if n_rows % 512 == 0 else n_rows

    def body(w_ref, g_ref, m_ref, v_ref, d_ref, nm_ref, nv_ref):
        d_ref[...], nm_ref[...], nv_ref[...] = _adam_update(w_ref[...], g_ref[...], m_ref[...], v_ref[...])

    spec = pl.BlockSpec((tr, n_cols), lambda i: (i, 0))
    shape = jax.ShapeDtypeStruct((n_rows, n_cols), F32)
    return pl.pallas_call(
        body, name="adamw_" + name, grid=(n_rows // tr,), out_shape=(shape, shape, shape),
        in_specs=[spec] * 4, out_specs=(spec, spec, spec), compiler_params=_cparams(1),
    )(w, g, m, v)


_SMALL = ("pre_norm_g", "ln_v_g", "ln_v_b", "b_spatial", "sinks", "rel_bias", "post_norm_g")


def _pack(pieces):
    blocks, first_rows, n = [], [], 0
    for p in pieces:
        flat = p.reshape(-1)
        flat = jnp.concatenate([flat, jnp.zeros(((-flat.shape[0]) % (8 * 128),), F32)]).reshape(-1, 128)
        blocks.append(flat)
        first_rows.append(n)
        n += flat.shape[0]
    return jnp.concatenate(blocks, axis=0), first_rows


def _unpack(block, first_row, shape):
    size = math.prod(shape)
    return block[first_row:first_row + -(-size // 128)].reshape(-1)[:size].reshape(shape)


def kernel(x, pre_norm_g, w_in, ln_v_g, ln_v_b, w_spatial, b_spatial, sinks, rel_bias, w_out, post_norm_g, loss_target, m_pre_norm_g, m_w_in, m_ln_v_g, m_ln_v_b, m_w_spatial, m_b_spatial, m_sinks, m_rel_bias, m_w_out, m_post_norm_g, v_pre_norm_g, v_w_in, v_ln_v_g, v_ln_v_b, v_w_spatial, v_b_spatial, v_sinks, v_rel_bias, v_w_out, v_post_norm_g):
    weights = dict(pre_norm_g=pre_norm_g, w_in=w_in, ln_v_g=ln_v_g, ln_v_b=ln_v_b, w_spatial=w_spatial, b_spatial=b_spatial,
                   sinks=sinks, rel_bias=rel_bias, w_out=w_out, post_norm_g=post_norm_g)
    mom_m = dict(pre_norm_g=m_pre_norm_g, w_in=m_w_in, ln_v_g=m_ln_v_g, ln_v_b=m_ln_v_b, w_spatial=m_w_spatial,
                 b_spatial=m_b_spatial, sinks=m_sinks, rel_bias=m_rel_bias, w_out=m_w_out, post_norm_g=m_post_norm_g)
    mom_v = dict(pre_norm_g=v_pre_norm_g, w_in=v_w_in, ln_v_g=v_ln_v_g, ln_v_b=v_ln_v_b, w_spatial=v_w_spatial,
                 b_spatial=v_b_spatial, sinks=v_sinks, rel_bias=v_rel_bias, w_out=v_w_out, post_norm_g=v_post_norm_g)
    n_seq, seq_len, _ = x.shape
    n_layers = w_in.shape[0]
    x2 = x.reshape(n_seq * seq_len, D_MODEL)
    target2 = loss_target.reshape(n_seq * seq_len, D_MODEL)
    row = lambda p, l: p[l][None]

    wt_shards = jnp.swapaxes(w_in, 1, 2).astype(BF16)
    wo_shards = w_out.astype(BF16)
    bucket = _window_buckets()
    bias = _bias_table(rel_bias, bucket)
    wcat, wtcat, bs = _spatial_tables(w_spatial, jnp.swapaxes(b_spatial, 1, 2))

    wt, wo = [None] * n_layers, [None] * n_layers
    wt[0], wo[0] = _gather_weights(wt_shards, wo_shards, 0)
    xs, saved = [x2], []
    for l in range(n_layers):
        layer_args = (xs[-1], row(pre_norm_g, l), wt[l], row(ln_v_g, l), row(ln_v_b, l), wcat[l], bs[l], sinks[l], bias,
                      wo[l], row(post_norm_g, l), seq_len)
        if l + 1 < n_layers:
            *acts, xn, y, wt[l + 1], wo[l + 1] = _fwd_layer(*layer_args, gather=(wt_shards, wo_shards, l + 1))
            xs.append(xn)
        else:
            *acts, dx, y, loss = _fwd_layer(*layer_args, target=target2)
        saved.append((y, *acts))

    small = [None] * n_layers
    parts = [None] * n_layers
    waiting = None
    for l in reversed(range(n_layers)):
        hosted = {} if waiting is None else dict(exchange=waiting[:2], allgather=waiting[2])
        dproj, *rest = _bwd_mix(dx, *saved[l], row(ln_v_g, l), row(ln_v_b, l), wcat[l], wtcat[l], bs[l], wo[l],
                                row(post_norm_g, l), **hosted)
        dx, dwt, dgpre = _bwd_in(dproj, xs[l], dx, row(pre_norm_g, l), wt[l])
        dwo, dws, dmix, dlng, dlnb, dgpost, dsink, dbias = rest[:8]
        if waiting is not None:
            parts[l + 1] = list(rest[8:])
        waiting = (dwt, dwo, dws.reshape(HEADS * CHUNK, CHUNK))
        small[l] = dict(pre_norm_g=dgpre[0], ln_v_g=dlng[0], ln_v_b=dlnb[0], dmix=dmix, sinks=dsink, dbias=dbias,
                        post_norm_g=dgpost[0])
    send_sems, recv_sems, *flying, token = _exchange_start(*waiting)

    db = _b_spatial_grad(jnp.stack([s["dmix"] for s in small]))
    pieces = {n: jnp.stack([s[n] for s in small]) for n in ("pre_norm_g", "ln_v_g", "ln_v_b", "sinks", "post_norm_g")}
    pieces["b_spatial"] = jnp.swapaxes(db[:, :, :HEADS], 1, 2)
    pieces["rel_bias"] = _rel_bias_grad(jnp.stack([s["dbias"] for s in small]), bucket)
    packed, first_rows = _pack([pieces[n] for n in _SMALL] + [loss])
    total = _allreduce_small(packed)
    grad = {n: _unpack(total, r, weights[n].shape) for n, r in zip(_SMALL, first_rows)}
    loss_out = total[first_rows[-1], 0]

    delta, new_m, new_v = {}, {}, {}
    for n in _SMALL:
        w = weights[n]
        two_d = (-1, w.shape[-1])
        d, nm, nv = _adamw(w.reshape(two_d), grad[n].reshape(two_d), mom_m[n].reshape(two_d), mom_v[n].reshape(two_d), n)
        delta[n], new_m[n], new_v[n] = d.reshape(w.shape), nm.reshape(w.shape), nv.reshape(w.shape)

    t3 = lambda p: jnp.swapaxes(p, 1, 2)
    flat3 = lambda p: p.reshape(n_layers, HEADS * CHUNK, CHUNK)
    kinds = (("w_in", 0, t3, t3), ("w_out", 1, lambda p: p, lambda p: p),
             ("w_spatial", 2, flat3, lambda p: p.reshape(w_spatial.shape)))
    results = {}
    for n, which, view, _ in kinds:
        results[n] = None
        for l in reversed(range(1, n_layers)):
            results[n] = _adamw_parts(parts[l][which], view(weights[n]), view(mom_m[n]), view(mom_v[n]), l, results[n], n,
                                      after=token if results[n] is None else None)
    *sources, pt, po, ws_all = _exchange_wait(send_sems, recv_sems, flying, [results[n][0] for n, *_ in kinds])
    parts[0] = _exchange_finish(*sources, (pt, po, ws_all))
    for n, which, view, back in kinds:
        final = _adamw_parts(parts[0][which], view(weights[n]), view(mom_m[n]), view(mom_v[n]), 0, results[n], n)
        grad[n], delta[n], new_m[n], new_v[n] = (back(r) for r in final)

    names = tuple(weights)
    return (loss_out, dx.reshape(x.shape), *[grad[n] for n in names], *[delta[n] for n in names],
            *[new_m[n] for n in names], *[new_v[n] for n in names])
```

```python
import math

import jax
import jax.numpy as jnp
from jax import lax
from jax.experimental import pallas as pl
from jax.experimental.pallas import tpu as pltpu

F32 = jnp.float32
BF16 = jnp.bfloat16

D_MODEL = 1024
A_WIDTH = 512
B_WIDTH = 512
KV_WIDTH = 128
IN_WIDTH = 3 * A_WIDTH + 2 * B_WIDTH + 2 * KV_WIDTH
CHUNK = 128
HEADS = 8
HEAD_DIM = 64
REL_BUCKETS = 32
NORM_EPS = 1e-6
NEG = -1e30
Q_OFF = 3 * A_WIDTH
K_OFF = Q_OFF + B_WIDTH
BZ_OFF = K_OFF + 2 * KV_WIDTH
QK_SCALE = HEAD_DIM ** -0.5

ADAM_LR = 0.001
ADAM_B1 = 0.9
ADAM_B2 = 0.999
ADAM_EPS = 1e-08
ADAM_WD = 0.01
ADAM_STEP = 10

TOK_TILE = 512
CHUNK_UNROLL = 4
BWD_CHUNK_UNROLL = 4
VMEM_LIMIT_V7X = 60 * 1024 * 1024

N_DEV = 8
N_CHIPS = 4
MESH_ID = pl.DeviceIdType.MESH
ANY = pl.BlockSpec(memory_space=pl.ANY)
SMEM = pl.BlockSpec(memory_space=pltpu.SMEM)


def _cparams(n_axes):
    return pltpu.CompilerParams(dimension_semantics=("arbitrary",) * n_axes, vmem_limit_bytes=VMEM_LIMIT_V7X)


_GELU_C = math.sqrt(2.0 / math.pi)
_GELU_C3 = _GELU_C * 0.044715


def _gelu(x):
    t = jnp.tanh(x * (_GELU_C3 * (x * x) + _GELU_C))
    return x * (0.5 * t + 0.5)


def _gelu_and_grad(x):
    x2 = x * x
    t = jnp.tanh(x * (_GELU_C3 * x2 + _GELU_C))
    cdf = 0.5 * t + 0.5
    d = cdf + (x * (cdf * (1.0 - cdf))) * ((6.0 * _GELU_C3) * x2 + 2.0 * _GELU_C)
    return x * cdf, d


def _sigmoid(x):
    return 0.5 + 0.5 * jnp.tanh(0.5 * x)


def _silu(x):
    return x * _sigmoid(x)


def _silu_and_grad(x):
    s = _sigmoid(x)
    return x * s, s * (1.0 + x * (1.0 - s))


def _dot(a, b):
    return jnp.dot(a, b, preferred_element_type=F32)


def _dot_nt(a, b):
    return lax.dot_general(a, b, (((1,), (1,)), ((), ())), preferred_element_type=F32)


def _dot_tn(a, b):
    return lax.dot_general(a, b, (((0,), (0,)), ((), ())), preferred_element_type=F32)


def _lo_mask(shape):
    return lax.broadcasted_iota(jnp.int32, shape, 1) < HEAD_DIM


def _swap_halves(v):
    return pltpu.roll(v, HEAD_DIM, 1)


def _window_buckets():
    q_loc = jnp.arange(CHUNK)[:, None]
    j_loc = jnp.arange(CHUNK)[None, :]
    d = q_loc - j_loc + jnp.where(j_loc > q_loc, CHUNK, 0)
    max_exact = REL_BUCKETS // 2
    safe = jnp.maximum(d, 1).astype(F32)
    large = max_exact + (jnp.log(safe / max_exact) / math.log(CHUNK / max_exact)
                         * (REL_BUCKETS - max_exact)).astype(jnp.int32)
    large = jnp.minimum(large, REL_BUCKETS - 1)
    return jnp.where(d < max_exact, d, large).astype(jnp.int32)


def _bias_table(rel_bias, bucket):
    def body(rb_ref, bk_ref, out_ref):
        bk = bk_ref[...]
        for h in range(HEADS):
            acc = jnp.zeros(bk.shape, F32)
            for b in range(REL_BUCKETS):
                acc = jnp.where(bk == b, rb_ref[b, h], acc)
            out_ref[h] = acc

    vmem = pl.BlockSpec(memory_space=pltpu.VMEM)
    return pl.pallas_call(
        body, name="bias_table",
        out_shape=jax.ShapeDtypeStruct((HEADS, CHUNK, CHUNK), F32),
        in_specs=[SMEM, vmem], out_specs=vmem,
    )(rel_bias, bucket)


def _rel_bias_grad(dbias, bucket):
    n_layers = dbias.shape[0]

    def body(db_ref, bk_ref, out_ref):
        bk = bk_ref[...]
        for h in range(HEADS):
            tot = db_ref[0, h]
            for l in range(1, n_layers):
                tot = tot + db_ref[l, h]
            for b in range(REL_BUCKETS):
                out_ref[b, h] = jnp.sum(jnp.where(bk == b, tot, 0.0))

    vmem = pl.BlockSpec(memory_space=pltpu.VMEM)
    return pl.pallas_call(
        body, name="rel_bias_grad",
        out_shape=jax.ShapeDtypeStruct((REL_BUCKETS, HEADS), F32),
        in_specs=[vmem] * 2, out_specs=SMEM,
    )(dbias, bucket)


def _spatial_tables(w_spatial, b_spatial_t):
    n_layers = w_spatial.shape[0]

    def body(w_ref, b_ref, wcat_ref, wtcat_ref, bs_ref):
        row = lax.broadcasted_iota(jnp.int32, (CHUNK, CHUNK), 0)
        col = lax.broadcasted_iota(jnp.int32, (CHUNK, CHUNK), 1)
        causal = col <= row
        lo = _lo_mask((CHUNK, CHUNK))
        for p in range(4):
            for half in range(2):
                w = jnp.where(causal, w_ref[0, 2 * p + half], 0.0)
                wcat_ref[0, p, :, half * CHUNK:(half + 1) * CHUNK] = w.astype(BF16)
                wtcat_ref[0, p, :, half * CHUNK:(half + 1) * CHUNK] = w.T.astype(BF16)
            b = b_ref[0]
            bs_ref[0, p] = jnp.where(lo, b[:, 2 * p:2 * p + 1], b[:, 2 * p + 1:2 * p + 2])

    return pl.pallas_call(
        body, name="spatial_tables", grid=(n_layers,),
        out_shape=(jax.ShapeDtypeStruct((n_layers, 4, CHUNK, 2 * CHUNK), BF16),
                   jax.ShapeDtypeStruct((n_layers, 4, CHUNK, 2 * CHUNK), BF16),
                   jax.ShapeDtypeStruct((n_layers, 4, CHUNK, CHUNK), F32)),
        in_specs=[pl.BlockSpec((1, HEADS, CHUNK, CHUNK), lambda l: (l, 0, 0, 0)),
                  pl.BlockSpec((1, CHUNK, HEADS), lambda l: (l, 0, 0))],
        out_specs=(pl.BlockSpec((1, 4, CHUNK, 2 * CHUNK), lambda l: (l, 0, 0, 0)),
                   pl.BlockSpec((1, 4, CHUNK, 2 * CHUNK), lambda l: (l, 0, 0, 0)),
                   pl.BlockSpec((1, 4, CHUNK, CHUNK), lambda l: (l, 0, 0, 0))),
        compiler_params=_cparams(1),
    )(w_spatial, b_spatial_t)


def _b_spatial_grad(dmix):
    n_layers = dmix.shape[0]

    def body(d_ref, out_ref):
        lane = lax.broadcasted_iota(jnp.int32, (CHUNK, CHUNK), 1)
        acc = jnp.zeros((CHUNK, CHUNK), F32)
        for p in range(4):
            t = d_ref[0, :, p * CHUNK:(p + 1) * CHUNK]
            s_lo = jnp.sum(jnp.where(lane < HEAD_DIM, t, 0.0), axis=1, keepdims=True)
            s_hi = jnp.sum(jnp.where(lane < HEAD_DIM, 0.0, t), axis=1, keepdims=True)
            acc = jnp.where(lane == 2 * p, s_lo, acc)
            acc = jnp.where(lane == 2 * p + 1, s_hi, acc)
        out_ref[0] = acc

    return pl.pallas_call(
        body, name="b_spatial_grad", grid=(n_layers,),
        out_shape=jax.ShapeDtypeStruct((n_layers, CHUNK, CHUNK), F32),
        in_specs=[pl.BlockSpec((1, CHUNK, A_WIDTH), lambda l: (l, 0, 0))],
        out_specs=pl.BlockSpec((1, CHUNK, CHUNK), lambda l: (l, 0, 0)),
        compiler_params=_cparams(1),
    )(dmix)


def _place():
    x, y, c = lax.axis_index("x"), lax.axis_index("y"), lax.axis_index("c")
    other_chips = [(1 - x, y), (x, 1 - y), (1 - x, 1 - y)]
    return x, y, c, other_chips


N_GATHER_SEMS = 12


def _gather_phases(shards, fulls, send_sems, recv_sems, local_sems):
    x, y, c, chips = _place()
    sibling = (x, y, 1 - c)
    n_arr = len(shards)

    def half_rows(a, chip, half):
        n = shards[a].shape[0]
        start = (2 * chip[0] + chip[1]) * n + half * (n // 2)
        return fulls[a].at[pl.ds(pl.multiple_of(start, 16), n // 2), :]

    def my_half(a):
        n = shards[a].shape[0]
        return shards[a].at[pl.ds(pl.multiple_of(c * (n // 2), 16), n // 2), :]

    def copy(k, a, src, chip, half, to):
        return pltpu.make_async_remote_copy(
            src_ref=src, dst_ref=half_rows(a, chip, half), send_sem=send_sems.at[n_arr * k + a],
            recv_sem=recv_sems.at[n_arr * k + a], device_id=to, device_id_type=MESH_ID)

    def local(a):
        n = shards[a].shape[0]
        mine = fulls[a].at[pl.ds(pl.multiple_of((2 * x + y) * n, 16), n), :]
        return pltpu.make_async_copy(shards[a], mine, local_sems.at[a])

    def first(k, a):
        return copy(k, a, my_half(a), (x, y), c, (chips[k][0], chips[k][1], c))

    def passed(k, a):
        return copy(3 + k, a, half_rows(a, chips[k], c), chips[k], c, sibling)

    def phase_a():
        for a in range(n_arr):
            local(a).start()
        for k in range(3):
            for a in range(n_arr):
                first(k, a).start()

    def phase_b():
        for k in range(3):
            for a in range(n_arr):
                copy(k, a, my_half(a), chips[k], c, sibling).wait_recv()
                passed(k, a).start()

    def phase_c():
        for k in range(3):
            for a in range(n_arr):
                copy(3 + k, a, my_half(a), chips[k], 1 - c, sibling).wait_recv()
        for k in range(3):
            for a in range(n_arr):
                first(k, a).wait_send()
                passed(k, a).wait_send()
        for a in range(n_arr):
            local(a).wait()

    return phase_a, phase_b, phase_c


N_EXCHANGE_SEMS = 14


def _exchange_phases(partials, parts, send_sems, recv_sems, local_sems, by_chip=True):
    x, y, c, chips = _place()
    me, sibling = (x, y, c), (x, y, 1 - c)
    n_arr = len(partials)

    def block(a, chip):
        if not by_chip:
            return partials[a]
        n = partials[a].shape[0] // N_CHIPS
        return partials[a].at[pl.ds(pl.multiple_of((2 * chip[0] + chip[1]) * n, 16), n), :]

    def slot(a, dev):
        return parts[a].at[4 * dev[0] + 2 * dev[1] + dev[2]]

    def copy(k, a, src, origin, to):
        return pltpu.make_async_remote_copy(
            src_ref=src, dst_ref=slot(a, origin), send_sem=send_sems.at[n_arr * k + a],
            recv_sem=recv_sems.at[n_arr * k + a], device_id=to, device_id_type=MESH_ID)

    def local(a):
        return pltpu.make_async_copy(block(a, (x, y)), slot(a, me), local_sems.at[a])

    def first(k, a):
        if k == 0:
            return copy(0, a, block(a, (x, y)), me, sibling)
        chip = chips[k - 1]
        return copy(k, a, block(a, chip), me, (chip[0], chip[1], c))

    def passed(k, a):
        origin = (chips[k][0], chips[k][1], c)
        return copy(4 + k, a, slot(a, origin), origin, sibling)

    def phase_a():
        for a in range(n_arr):
            local(a).start()
        for k in range(4):
            for a in range(n_arr):
                first(k, a).start()

    def phase_b():
        for k in range(3):
            for a in range(n_arr):
                copy(1 + k, a, block(a, (x, y)), (chips[k][0], chips[k][1], c), me).wait_recv()
                passed(k, a).start()

    def phase_c():
        for a in range(n_arr):
            copy(0, a, block(a, (x, y)), sibling, me).wait_recv()
        for k in range(3):
            for a in range(n_arr):
                copy(4 + k, a, block(a, (x, y)), (chips[k][0], chips[k][1], 1 - c), me).wait_recv()
        for k in range(4):
            for a in range(n_arr):
                first(k, a).wait_send()
        for k in range(3):
            for a in range(n_arr):
                passed(k, a).wait_send()
        for a in range(n_arr):
            local(a).wait()

    return phase_a, phase_b, phase_c


def _comm_scratch(n_sems):
    return [pltpu.SemaphoreType.DMA((n_sems,)), pltpu.SemaphoreType.DMA((n_sems,)), pltpu.SemaphoreType.DMA((2,))]


def _gather_weights(wt_shards, wo_shards, layer):
    wt_rows, wo_rows = wt_shards.shape[1], wo_shards.shape[1]

    def body(wt_ref, wo_ref, wt_full, wo_full, send_sems, recv_sems, local_sems):
        phases = _gather_phases((wt_ref.at[layer], wo_ref.at[layer]), (wt_full, wo_full), send_sems, recv_sems, local_sems)
        for phase in phases:
            phase()

    return pl.pallas_call(
        body, name="gather_weights",
        out_shape=(jax.ShapeDtypeStruct((N_CHIPS * wt_rows, D_MODEL), BF16),
                   jax.ShapeDtypeStruct((N_CHIPS * wo_rows, D_MODEL), BF16)),
        in_specs=[ANY, ANY], out_specs=(ANY, ANY), scratch_shapes=_comm_scratch(N_GATHER_SEMS),
    )(wt_shards, wo_shards)


def _exchange_grads(dwt, dwo, small):
    def body(dwt_ref, dwo_ref, small_ref, pt_ref, po_ref, all_ref, *sems):
        blocks = _exchange_phases((dwt_ref, dwo_ref), (pt_ref, po_ref), *sems[:3])
        whole = _exchange_phases((small_ref,), (all_ref,), *sems[3:], by_chip=False)
        for phase_of_blocks, phase_of_whole in zip(blocks, whole):
            phase_of_blocks()
            phase_of_whole()

    return pl.pallas_call(
        body, name="exchange_grads",
        out_shape=(jax.ShapeDtypeStruct((N_DEV, dwt.shape[0] // N_CHIPS, D_MODEL), BF16),
                   jax.ShapeDtypeStruct((N_DEV, dwo.shape[0] // N_CHIPS, D_MODEL), BF16),
                   jax.ShapeDtypeStruct((N_DEV,) + small.shape, small.dtype)),
        in_specs=[ANY, ANY, ANY], out_specs=(ANY, ANY, ANY),
        scratch_shapes=_comm_scratch(N_EXCHANGE_SEMS) + _comm_scratch(N_EXCHANGE_SEMS),
    )(dwt, dwo, small)


def _allreduce_small(part):
    n_rows = part.shape[0]

    def body(p_ref, tot_ref, all_ref, send_sems, recv_sems, local_sem):
        x, y, c, chips = _place()
        me, sibling = (x, y, c), (x, y, 1 - c)

        def rows(dev):
            return all_ref.at[pl.ds(pl.multiple_of((4 * dev[0] + 2 * dev[1] + dev[2]) * n_rows, 8), n_rows), :]

        def copy(k, origin, to, src=None):
            return pltpu.make_async_remote_copy(
                src_ref=rows(origin) if src is None else src, dst_ref=rows(origin), send_sem=send_sems.at[k],
                recv_sem=recv_sems.at[k], device_id=to, device_id_type=MESH_ID)

        mine = pltpu.make_async_copy(p_ref, rows(me), local_sem)
        mine.start()
        first = [copy(0, me, sibling, src=p_ref)]
        first += [copy(1 + k, me, (chip[0], chip[1], c), src=p_ref) for k, chip in enumerate(chips)]
        for cp in first:
            cp.start()
        passed = []
        for k, chip in enumerate(chips):
            origin = (chip[0], chip[1], c)
            copy(1 + k, origin, me).wait_recv()
            fwd = copy(4 + k, origin, sibling)
            fwd.start()
            passed.append(fwd)
        copy(0, sibling, me).wait_recv()
        for k, chip in enumerate(chips):
            copy(4 + k, (chip[0], chip[1], 1 - c), me).wait_recv()
        for cp in first + passed:
            cp.wait_send()
        mine.wait()
        tot = all_ref[0:n_rows, :]
        for d in range(1, N_DEV):
            tot = tot + all_ref[d * n_rows:(d + 1) * n_rows, :]
        tot_ref[...] = tot

    vmem = pl.BlockSpec(memory_space=pltpu.VMEM)
    return pl.pallas_call(
        body, name="allreduce_small",
        out_shape=jax.ShapeDtypeStruct((n_rows, 128), F32),
        in_specs=[vmem], out_specs=vmem,
        scratch_shapes=[pltpu.VMEM((N_DEV * n_rows, 128), F32), pltpu.SemaphoreType.DMA((7,)),
                        pltpu.SemaphoreType.DMA((7,)), pltpu.SemaphoreType.DMA],
        compiler_params=pltpu.CompilerParams(vmem_limit_bytes=VMEM_LIMIT_V7X),
    )(part)


def _hosted(phases, step, n_steps):
    phase_a, phase_b, phase_c = phases

    def at_start():
        pl.when(step == 0)(phase_a)

    def at_end():
        pl.when(step == (3 * n_steps) // 4)(phase_b)
        pl.when(step == n_steps - 1)(phase_c)

    return at_start, at_end


def _layer_norm_stats(vv):
    mu = jnp.mean(vv, axis=-1, keepdims=True)
    xc = vv - mu
    rs = lax.rsqrt(jnp.mean(xc * xc, axis=-1, keepdims=True) + NORM_EPS)
    return xc * rs, rs


def _blockdiag(v, lo):
    zero = jnp.zeros_like(v)
    return jnp.concatenate([jnp.where(lo, v, zero), jnp.where(lo, zero, v)], axis=0)


def _softmax_sink(s, sink):
    m = jnp.maximum(jnp.max(s, axis=-1, keepdims=True), sink)
    e = jnp.exp(s - m)
    esink = jnp.exp(sink - m)
    den = jnp.sum(e, axis=-1, keepdims=True) + esink
    return e / den


def _kv_rows(cur_ref, halo_ref, r0, c):
    prev_in_tile = cur_ref[pl.ds(pl.multiple_of(jnp.maximum(r0 - CHUNK, 0), CHUNK), CHUNK), :]
    prev = jnp.where(c == 0, halo_ref[...], prev_in_tile)
    kv2 = jnp.concatenate([prev, cur_ref[pl.ds(r0, CHUNK), :]], axis=0)
    k2, v2 = kv2[:, 0:KV_WIDTH], kv2[:, KV_WIDTH:2 * KV_WIDTH]
    return (k2, _swap_halves(k2)), (v2, _swap_halves(v2))


def _window_square(over_keys, prev):
    return jnp.where(prev, over_keys[:, 0:CHUNK], over_keys[:, CHUNK:2 * CHUNK])


def _window_keys(square, prev):
    zero = jnp.zeros_like(square)
    return jnp.concatenate([jnp.where(prev, square, zero), jnp.where(prev, zero, square)], axis=1)


def _ahead(shape=(CHUNK, CHUNK)):
    return lax.broadcasted_iota(jnp.int32, shape, 1) - lax.broadcasted_iota(jnp.int32, shape, 0)


def _dead_mask(first, ahead):
    return ahead > jnp.where(first, 0, CHUNK)


def _head_of(p, half):
    return 2 * p + half, int(half != p // 2)


_HEADS_BY_COPY = tuple(tuple((p, half) for p in range(4) for half in range(2) if _head_of(p, half)[1] == sw)
                       for sw in range(2))


def _masked_halves(tile, lo):
    zero = jnp.zeros_like(tile)
    return {0: jnp.where(lo, tile, zero), 1: jnp.where(lo, zero, tile)}


def _query_stacks(q_tiles, lo):
    qm = {p: _masked_halves(q_tiles[p] * QK_SCALE, lo) for p in range(4)}
    return [jnp.concatenate([qm[p][half] for p, half in members], axis=0) for members in _HEADS_BY_COPY]


def _attention_probs(q_stacks, ks, bias_ref, sink_ref, dead, prev):
    probs = {}
    for sw, members in enumerate(_HEADS_BY_COPY):
        s_stack = _dot_nt(q_stacks[sw], ks[sw])
        for i, (p, half) in enumerate(members):
            head = 2 * p + half
            s = _window_square(s_stack[i * CHUNK:(i + 1) * CHUNK, :], prev) + bias_ref[head]
            s = jnp.where(dead, NEG, s)
            probs[p, half] = _softmax_sink(s, sink_ref[head]).astype(BF16)
    return probs


def _attention_values(probs, vs, prev):
    outs, p_stacks = {}, []
    for sw, members in enumerate(_HEADS_BY_COPY):
        p_stack = jnp.concatenate([_window_keys(probs[m], prev) for m in members], axis=0)
        r_stack = _dot(p_stack, vs[sw])
        p_stacks.append(p_stack)
        for i, m in enumerate(members):
            outs[m] = r_stack[i * CHUNK:(i + 1) * CHUNK, :]
    return outs, p_stacks


def _fwd_layer(x, g_pre, w_in_t, ln_g, ln_b, wcat, bs, sinks, bias, w_out, g_post, seq_len, gather=None, target=None):
    n_tok = x.shape[0]
    tb = TOK_TILE
    n_chunks = tb // CHUNK
    n_tiles = n_tok // tb
    n_in = 11
    assert gather is None or target is None

    def body(*refs):
        (x_ref, gpre_ref, wt_ref, lng_ref, lnb_ref, wcat_ref, bs_ref, sink_ref, bias_ref, wout_ref,
         gpost_ref) = refs[:n_in]
        at_start = at_end = lambda: None
        if gather is not None:
            (wts_ref, wos_ref, a_ref, q_ref, kv_ref, bz_ref, probs_ref, xn_ref, y_ref, wt_full, wo_full, ycat_s, halo_ref,
             send_sems, recv_sems, local_sems) = refs[n_in:]
            phases = _gather_phases((wts_ref.at[gather[2]], wos_ref.at[gather[2]]), (wt_full, wo_full), send_sems,
                                    recv_sems, local_sems)
            at_start, at_end = _hosted(phases, pl.program_id(0), n_tiles)
        elif target is not None:
            target_ref, a_ref, q_ref, kv_ref, bz_ref, probs_ref, xn_ref, y_ref, loss_ref, ycat_s, halo_ref = refs[n_in:]
        else:
            a_ref, q_ref, kv_ref, bz_ref, probs_ref, xn_ref, y_ref, ycat_s, halo_ref = refs[n_in:]
        at_start()
        i = pl.program_id(0)
        lo = _lo_mask((CHUNK, CHUNK))
        ahead = _ahead()
        prev = ahead > 0

        @pl.when(i == 0)
        def _():
            halo_ref[...] = jnp.zeros_like(halo_ref)

        xf = x_ref[...]
        r1 = lax.rsqrt(jnp.mean(xf * xf, axis=-1, keepdims=True) + NORM_EPS)
        h = ((xf * r1) * gpre_ref[...]).astype(BF16)
        a_ref[...] = _dot_nt(h, wt_ref[0:Q_OFF, :])
        q_ref[...] = _dot_nt(h, wt_ref[Q_OFF:K_OFF, :]).astype(BF16)
        kv_ref[...] = _dot_nt(h, wt_ref[K_OFF:BZ_OFF, :]).astype(BF16)
        bz_ref[...] = _dot_nt(h, wt_ref[BZ_OFF:IN_WIDTH, :])

        def chunk(c, carry):
            r0 = pl.multiple_of(c * CHUNK, CHUNK)
            rows = pl.ds(r0, CHUNK)
            u = _gelu(a_ref[rows, 0:A_WIDTH])
            vv = _gelu(a_ref[rows, A_WIDTH:2 * A_WIDTH])
            xhat, _ = _layer_norm_stats(vv)
            vnb = (xhat * lng_ref[...] + lnb_ref[...]).astype(BF16)
            for p in range(4):
                blk = slice(p * CHUNK, (p + 1) * CHUNK)
                mixed = _dot(wcat_ref[p], _blockdiag(vnb[:, blk], lo)) + bs_ref[p]
                sz = _silu(a_ref[rows, 2 * A_WIDTH + p * CHUNK:2 * A_WIDTH + (p + 1) * CHUNK])
                ycat_s[rows, blk] = ((u[:, blk] * mixed) * sz).astype(BF16)
            ks, vs = _kv_rows(kv_ref, halo_ref, r0, c)
            dead = _dead_mask(lax.rem(i * tb + r0, seq_len) == 0, ahead)
            q_tiles = [q_ref[rows, p * CHUNK:(p + 1) * CHUNK] for p in range(4)]
            probs = _attention_probs(_query_stacks(q_tiles, lo), ks, bias_ref, sink_ref, dead, prev)
            for (p, half), prob in probs.items():
                head = 2 * p + half
                probs_ref[rows, head * CHUNK:(head + 1) * CHUNK] = prob
            outs, _ = _attention_values(probs, vs, prev)
            for p in range(4):
                blk = slice(p * CHUNK, (p + 1) * CHUNK)
                o = jnp.where(lo, outs[p, 0], outs[p, 1])
                ycat_s[rows, B_WIDTH + p * CHUNK:B_WIDTH + (p + 1) * CHUNK] = (o * _silu(bz_ref[rows, blk])).astype(BF16)
            return carry

        lax.fori_loop(0, n_chunks, chunk, 0, unroll=CHUNK_UNROLL)
        halo_ref[...] = kv_ref[tb - CHUNK:tb, :]
        y = _dot(ycat_s[...], wout_ref[...])
        r = lax.rsqrt(jnp.mean(y * y, axis=-1, keepdims=True) + NORM_EPS)
        y_ref[...] = y
        xn = x_ref[...] + (y * r) * gpost_ref[...]
        if target is None:
            xn_ref[...] = xn
        else:
            d = xn - target_ref[...]
            xn_ref[...] = d * (1.0 / D_MODEL)

            @pl.when(i == 0)
            def _():
                loss_ref[0, 0] = 0.0

            loss_ref[0, 0] += 0.5 * jnp.sum(jnp.mean(d * d, axis=-1, keepdims=True))
        at_end()

    tile = lambda w: pl.BlockSpec((tb, w), lambda i: (i, 0))
    whole = lambda shape, **kw: pl.BlockSpec(shape, lambda i: (0,) * len(shape), **kw)
    in_specs = [tile(D_MODEL), whole((1, D_MODEL)), whole((IN_WIDTH, D_MODEL), pipeline_mode=pl.Buffered(1)),
                whole((1, A_WIDTH)), whole((1, A_WIDTH)), whole((4, CHUNK, 2 * CHUNK)), whole((4, CHUNK, CHUNK)), SMEM,
                whole((HEADS, CHUNK, CHUNK)), whole((D_MODEL, D_MODEL)), whole((1, D_MODEL))]
    out_shape = [jax.ShapeDtypeStruct((n_tok, Q_OFF), F32), jax.ShapeDtypeStruct((n_tok, B_WIDTH), BF16),
                 jax.ShapeDtypeStruct((n_tok, 2 * KV_WIDTH), BF16), jax.ShapeDtypeStruct((n_tok, B_WIDTH), F32),
                 jax.ShapeDtypeStruct((n_tok, HEADS * CHUNK), BF16), jax.ShapeDtypeStruct((n_tok, D_MODEL), F32),
                 jax.ShapeDtypeStruct((n_tok, D_MODEL), F32)]
    out_specs = [tile(Q_OFF), tile(B_WIDTH), tile(2 * KV_WIDTH), tile(B_WIDTH), tile(HEADS * CHUNK), tile(D_MODEL),
                 tile(D_MODEL)]
    scratch = [pltpu.VMEM((tb, D_MODEL), BF16), pltpu.VMEM((CHUNK, 2 * KV_WIDTH), BF16)]
    args = [x, g_pre, w_in_t, ln_g, ln_b, wcat, bs, sinks, bias, w_out, g_post]
    if gather is not None:
        wt_shards, wo_shards, _ = gather
        in_specs += [ANY, ANY]
        args += [wt_shards, wo_shards]
        out_shape += [jax.ShapeDtypeStruct((N_CHIPS * wt_shards.shape[1], D_MODEL), BF16),
                      jax.ShapeDtypeStruct((N_CHIPS * wo_shards.shape[1], D_MODEL), BF16)]
        out_specs += [ANY, ANY]
        scratch += _comm_scratch(N_GATHER_SEMS)
    if target is not None:
        in_specs.append(tile(D_MODEL))
        args.append(target)
        out_shape.append(jax.ShapeDtypeStruct((1, 1), F32))
        out_specs.append(SMEM)
    name = "fwd_layer" + ("" if gather is None else "_gather") + ("" if target is None else "_loss")
    return pl.pallas_call(
        body, name=name, grid=(n_tiles,),
        out_shape=tuple(out_shape), in_specs=in_specs, out_specs=tuple(out_specs), scratch_shapes=scratch,
        compiler_params=_cparams(1),
    )(*args)


def _bwd_mix(dout, y, a, q, kv, bz, probs, ln_g, ln_b, wcat, wtcat, bs, w_out, g_post, exchange=None, allgather=None):
    n_tok = y.shape[0]
    tb = TOK_TILE
    n_chunks = tb // CHUNK
    n_tiles = n_tok // tb

    def body(*refs):
        refs = list(refs)
        take = lambda n: [refs.pop(0) for _ in range(n)]
        (dout_ref, y_ref, a_ref, q_ref, kv_ref, halo_ref, bz_ref, probs_ref, lng_ref, lnb_ref, wcat_ref, wtcat_ref, bs_ref,
         wout_ref, gpost_ref) = take(15)
        if exchange is not None:
            dwt_in, dwo_in = take(2)
        if allgather is not None:
            (small_in,) = take(1)
        dproj_ref, dwout_ref, dws_ref, dmix_ref, dlng_ref, dlnb_ref, dgpost_ref, dsink_ref, dbias_ref = take(9)
        if exchange is not None:
            pt_ref, po_ref = take(2)
        if allgather is not None:
            (small_all,) = take(1)
        ycat_s, dy_s, dyc_s, dkv_s, carry_s, dwout_s, stage_s = take(7)
        starts, ends = [], []
        if exchange is not None:
            hosted = _hosted(_exchange_phases((dwt_in, dwo_in), (pt_ref, po_ref), *take(3)), pl.program_id(0), n_tiles)
            starts.append(hosted[0])
            ends.append(hosted[1])
        if allgather is not None:
            hosted = _hosted(_exchange_phases((small_in,), (small_all,), *take(3), by_chip=False), pl.program_id(0),
                             n_tiles)
            starts.append(hosted[0])
            ends.append(hosted[1])
        at_start = lambda: [f() for f in starts]
        at_end = lambda: [f() for f in ends]
        at_start()
        step = pl.program_id(0)
        lo = _lo_mask((CHUNK, CHUNK))
        prev = _ahead() > 0

        @pl.when(step == 0)
        def _():
            dwout_s[...] = jnp.zeros_like(dwout_s)
            dws_ref[...] = jnp.zeros_like(dws_ref)
            dmix_ref[...] = jnp.zeros_like(dmix_ref)
            dlng_ref[...] = jnp.zeros_like(dlng_ref)
            dlnb_ref[...] = jnp.zeros_like(dlnb_ref)
            dgpost_ref[...] = jnp.zeros_like(dgpost_ref)
            dbias_ref[...] = jnp.zeros_like(dbias_ref)
            carry_s[...] = jnp.zeros_like(carry_s)
            for h in range(HEADS):
                dsink_ref[h] = 0.0

        for r0 in range(0, tb, CHUNK):
            rows = slice(r0, r0 + CHUNK)
            yv = y_ref[rows, :]
            dout = dout_ref[rows, :]
            r = lax.rsqrt(jnp.mean(yv * yv, axis=-1, keepdims=True) + NORM_EPS)
            yn = yv * r
            dgpost_ref[...] += jnp.sum(dout * yn, axis=0, keepdims=True)
            dyn = dout * gpost_ref[...]
            dy_s[rows, :] = (r * (dyn - yn * jnp.mean(dyn * yn, axis=-1, keepdims=True))).astype(BF16)
        dyc_s[...] = _dot_nt(dy_s[...], wout_ref[...])
        dkv_s[0:tb, :] = jnp.zeros((tb, 2 * KV_WIDTH), F32)
        dkv_s[tb:tb + CHUNK, :] = carry_s[...]

        def chunk(c, carry):
            r0 = pl.multiple_of(c * CHUNK, CHUNK)
            rows = pl.ds(r0, CHUNK)
            u, gu = _gelu_and_grad(a_ref[rows, 0:A_WIDTH])
            vv, gv = _gelu_and_grad(a_ref[rows, A_WIDTH:2 * A_WIDTH])
            xhat, rs = _layer_norm_stats(vv)
            vnb = (xhat * lng_ref[...] + lnb_ref[...]).astype(BF16)
            d_vn, d_u, d_az = [], [], []
            for p in range(4):
                blk = slice(p * CHUNK, (p + 1) * CHUNK)
                vbd = _blockdiag(vnb[:, blk], lo)
                mixed = _dot(wcat_ref[p], vbd) + bs_ref[p]
                sz, gz = _silu_and_grad(a_ref[rows, 2 * A_WIDTH + p * CHUNK:2 * A_WIDTH + (p + 1) * CHUNK])
                ub = u[:, blk]
                dya = dyc_s[rows, blk]
                um = ub * mixed
                ycat_s[rows, blk] = (um * sz).astype(BF16)
                d_mixed = (dya * ub) * sz
                d_u.append((dya * mixed) * sz)
                d_az.append((dya * um) * gz)
                dmix_ref[:, blk] += d_mixed
                dmbd = _blockdiag(d_mixed.astype(BF16), lo)
                d_vn.append(_dot(wtcat_ref[p], dmbd))
                dws_ref[p] += _dot_nt(dmbd, vnb[:, blk])
            d_vn = jnp.concatenate(d_vn, axis=1)
            dlng_ref[...] += jnp.sum(d_vn * xhat, axis=0, keepdims=True)
            dlnb_ref[...] += jnp.sum(d_vn, axis=0, keepdims=True)
            dxh = d_vn * lng_ref[...]
            d_vv = rs * (dxh - jnp.mean(dxh, axis=-1, keepdims=True)
                         - xhat * jnp.mean(dxh * xhat, axis=-1, keepdims=True))
            dproj_ref[rows, 0:A_WIDTH] = (jnp.concatenate(d_u, axis=1) * gu).astype(BF16)
            dproj_ref[rows, A_WIDTH:2 * A_WIDTH] = (d_vv * gv).astype(BF16)
            dproj_ref[rows, 2 * A_WIDTH:Q_OFF] = jnp.concatenate(d_az, axis=1).astype(BF16)
            ks, vs = _kv_rows(kv_ref, halo_ref, r0, c)
            q_stacks = _query_stacks([q_ref[rows, p * CHUNK:(p + 1) * CHUNK] for p in range(4)], lo)
            probs = {(p, half): probs_ref[rows, (2 * p + half) * CHUNK:(2 * p + half + 1) * CHUNK]
                     for p in range(4) for half in range(2)}
            outs_h, p_stacks = _attention_values(probs, vs, prev)
            dom = {}
            for p in range(4):
                blk = slice(p * CHUNK, (p + 1) * CHUNK)
                sb, gb = _silu_and_grad(bz_ref[rows, blk])
                dyb = dyc_s[rows, B_WIDTH + p * CHUNK:B_WIDTH + (p + 1) * CHUNK]
                o = jnp.where(lo, outs_h[p, 0], outs_h[p, 1])
                ycat_s[rows, B_WIDTH + p * CHUNK:B_WIDTH + (p + 1) * CHUNK] = (o * sb).astype(BF16)
                dproj_ref[rows, BZ_OFF + p * CHUNK:BZ_OFF + (p + 1) * CHUNK] = ((dyb * o) * gb).astype(BF16)
                dom[p] = _masked_halves((dyb * sb).astype(BF16), lo)
            dqs, dk_by_copy, dv_by_copy = {}, [], []
            for sw, members in enumerate(_HEADS_BY_COPY):
                do_stack = jnp.concatenate([dom[p][half] for p, half in members], axis=0)
                dp_stack = _dot_nt(do_stack, vs[sw])
                dsws = []
                for i, (p, half) in enumerate(members):
                    head = 2 * p + half
                    dp = _window_square(dp_stack[i * CHUNK:(i + 1) * CHUNK, :], prev)
                    prob = probs[p, half].astype(F32)
                    delta = jnp.sum(prob * dp, axis=-1, keepdims=True)
                    ds = prob * (dp - delta)
                    dsink_ref[head] += -jnp.sum((1.0 - jnp.sum(prob, axis=-1, keepdims=True)) * delta)
                    dbias_ref[head] += ds
                    dsws.append(_window_keys(ds.astype(BF16), prev))
                ds_stack = jnp.concatenate(dsws, axis=0)
                dq_stack = _dot(ds_stack, ks[sw])
                for i, m in enumerate(members):
                    dqs[m] = dq_stack[i * CHUNK:(i + 1) * CHUNK, :]
                dk_by_copy.append(_dot_tn(ds_stack, q_stacks[sw]))
                dv_by_copy.append(_dot_tn(p_stacks[sw], do_stack))
            for p in range(4):
                dq = jnp.where(lo, dqs[p, 0], dqs[p, 1]) * QK_SCALE
                dproj_ref[rows, Q_OFF + p * CHUNK:Q_OFF + (p + 1) * CHUNK] = dq.astype(BF16)
            both = pl.ds(r0, 2 * CHUNK)
            dkv_s[both, 0:KV_WIDTH] += dk_by_copy[0] + _swap_halves(dk_by_copy[1])
            dkv_s[both, KV_WIDTH:2 * KV_WIDTH] += dv_by_copy[0] + _swap_halves(dv_by_copy[1])
            return carry

        lax.fori_loop(0, n_chunks, chunk, 0, unroll=BWD_CHUNK_UNROLL)
        dwout_s[...] += _dot_tn(ycat_s[...], dy_s[...])
        dproj_ref[:, K_OFF:BZ_OFF] = dkv_s[CHUNK:CHUNK + tb, :].astype(BF16)
        carry_s[...] = dkv_s[0:CHUNK, :]

        @pl.when(step == n_tiles - 1)
        def _():
            row = lax.broadcasted_iota(jnp.int32, (2 * CHUNK, CHUNK), 0)
            col = lax.broadcasted_iota(jnp.int32, (2 * CHUNK, CHUNK), 1)
            causal = col <= jnp.where(row >= CHUNK, row - CHUNK, row)
            for p in range(4):
                dws_ref[p] = jnp.where(causal, dws_ref[p], 0.0)
            rows = stage_s.shape[0]
            for r0 in range(0, D_MODEL, rows):
                stage_s[...] = dwout_s[r0:r0 + rows, :].astype(BF16)
                pltpu.sync_copy(stage_s, dwout_ref.at[r0:r0 + rows, :])

        at_end()

    tile = lambda w: pl.BlockSpec((tb, w), lambda s: (n_tiles - 1 - s, 0))
    whole = lambda shape, **kw: pl.BlockSpec(shape, lambda s: (0,) * len(shape), **kw)
    once = dict(pipeline_mode=pl.Buffered(1))
    in_specs = [tile(D_MODEL), tile(D_MODEL), tile(Q_OFF), tile(B_WIDTH), tile(2 * KV_WIDTH),
                pl.BlockSpec((CHUNK, 2 * KV_WIDTH), lambda s: (jnp.maximum((n_tiles - 1 - s) * n_chunks - 1, 0), 0)),
                tile(B_WIDTH), tile(HEADS * CHUNK), whole((1, A_WIDTH)), whole((1, A_WIDTH)), whole((4, CHUNK, 2 * CHUNK)),
                whole((4, CHUNK, 2 * CHUNK)), whole((4, CHUNK, CHUNK)), whole((D_MODEL, D_MODEL), **once),
                whole((1, D_MODEL))]
    args = [dout, y, a, q, kv, kv, bz, probs, ln_g, ln_b, wcat, wtcat, bs, w_out, g_post]
    small_shapes = [jax.ShapeDtypeStruct((D_MODEL, D_MODEL), BF16), jax.ShapeDtypeStruct((4, 2 * CHUNK, CHUNK), F32),
                    jax.ShapeDtypeStruct((CHUNK, A_WIDTH), F32), jax.ShapeDtypeStruct((1, A_WIDTH), F32),
                    jax.ShapeDtypeStruct((1, A_WIDTH), F32), jax.ShapeDtypeStruct((1, D_MODEL), F32),
                    jax.ShapeDtypeStruct((HEADS,), F32), jax.ShapeDtypeStruct((HEADS, CHUNK, CHUNK), F32)]
    small_specs = [ANY, whole((4, 2 * CHUNK, CHUNK)), whole((CHUNK, A_WIDTH)), whole((1, A_WIDTH)), whole((1, A_WIDTH)),
                   whole((1, D_MODEL)), SMEM, whole((HEADS, CHUNK, CHUNK))]
    scratch = [pltpu.VMEM((tb, D_MODEL), BF16), pltpu.VMEM((tb, D_MODEL), BF16), pltpu.VMEM((tb, D_MODEL), F32),
               pltpu.VMEM((tb + CHUNK, 2 * KV_WIDTH), F32), pltpu.VMEM((CHUNK, 2 * KV_WIDTH), F32),
               pltpu.VMEM((D_MODEL, D_MODEL), F32), pltpu.VMEM((2 * CHUNK, D_MODEL), BF16)]
    out_shape = [jax.ShapeDtypeStruct((n_tok, IN_WIDTH), BF16)] + small_shapes
    out_specs = [tile(IN_WIDTH)] + small_specs
    if exchange is not None:
        dwt, dwo = exchange
        in_specs += [ANY, ANY]
        args += [dwt, dwo]
        out_shape += [jax.ShapeDtypeStruct((N_DEV, dwt.shape[0] // N_CHIPS, D_MODEL), BF16),
                      jax.ShapeDtypeStruct((N_DEV, dwo.shape[0] // N_CHIPS, D_MODEL), BF16)]
        out_specs += [ANY, ANY]
    if allgather is not None:
        in_specs.append(ANY)
        args.append(allgather)
        out_shape.append(jax.ShapeDtypeStruct((N_DEV,) + allgather.shape, allgather.dtype))
        out_specs.append(ANY)
    for hosted in (exchange, allgather):
        if hosted is not None:
            scratch += _comm_scratch(N_EXCHANGE_SEMS)
    return pl.pallas_call(
        body, name="bwd_mix" + ("" if exchange is None and allgather is None else "_exchange"), grid=(n_tiles,),
        out_shape=tuple(out_shape), in_specs=in_specs, out_specs=tuple(out_specs), scratch_shapes=scratch,
        compiler_params=_cparams(1),
    )(*args)


def _bwd_in(dproj, x, dout, g_pre, w_in_t):
    n_tok = x.shape[0]
    tm = TOK_TILE
    n_tiles = n_tok // tm

    def body(dp_ref, x_ref, dout_ref, g_ref, wt_ref, dx_ref, dwt_ref, dg_ref, acc_s):
        i = pl.program_id(0)

        @pl.when(i == 0)
        def _():
            acc_s[...] = jnp.zeros_like(acc_s)
            dg_ref[...] = jnp.zeros_like(dg_ref)

        xf = x_ref[...]
        r = lax.rsqrt(jnp.mean(xf * xf, axis=-1, keepdims=True) + NORM_EPS)
        xn = xf * r
        h = (xn * g_ref[...]).astype(BF16)
        dp = dp_ref[...]
        dh = _dot(dp, wt_ref[...])
        acc_s[...] += _dot_tn(dp, h)
        dg_ref[...] += jnp.sum(dh * xn, axis=0, keepdims=True)
        dhn = dh * g_ref[...]
        dx_ref[...] = dout_ref[...] + r * (dhn - xn * jnp.mean(dhn * xn, axis=-1, keepdims=True))

        @pl.when(i == n_tiles - 1)
        def _():
            dwt_ref[...] = acc_s[...].astype(BF16)

    return pl.pallas_call(
        body, name="bwd_in", grid=(n_tiles,),
        out_shape=(jax.ShapeDtypeStruct((n_tok, D_MODEL), F32), jax.ShapeDtypeStruct((IN_WIDTH, D_MODEL), BF16),
                   jax.ShapeDtypeStruct((1, D_MODEL), F32)),
        in_specs=[pl.BlockSpec((tm, IN_WIDTH), lambda i: (i, 0)), pl.BlockSpec((tm, D_MODEL), lambda i: (i, 0)),
                  pl.BlockSpec((tm, D_MODEL), lambda i: (i, 0)), pl.BlockSpec((1, D_MODEL), lambda i: (0, 0)),
                  pl.BlockSpec((IN_WIDTH, D_MODEL), lambda i: (0, 0), pipeline_mode=pl.Buffered(1))],
        out_specs=(pl.BlockSpec((tm, D_MODEL), lambda i: (i, 0)),
                   pl.BlockSpec((IN_WIDTH, D_MODEL), lambda i: (0, 0), pipeline_mode=pl.Buffered(1)),
                   pl.BlockSpec((1, D_MODEL), lambda i: (0, 0))),
        scratch_shapes=[pltpu.VMEM((IN_WIDTH, D_MODEL), F32)],
        compiler_params=_cparams(1),
    )(dproj, x, dout, g_pre, w_in_t)


def _adam_update(w, g, m, v):
    nm = ADAM_B1 * m + (1.0 - ADAM_B1) * g
    nv = ADAM_B2 * v + (1.0 - ADAM_B2) * (g * g)
    m_hat = nm / (1.0 - ADAM_B1 ** ADAM_STEP)
    v_hat = nv / (1.0 - ADAM_B2 ** ADAM_STEP)
    return -ADAM_LR * (m_hat / (jnp.sqrt(v_hat) + ADAM_EPS) + ADAM_WD * w), nm, nv


def _sum_slots(p_ref):
    tot = p_ref[0].astype(F32)
    for d in range(1, N_DEV):
        tot = tot + p_ref[d].astype(F32)
    return tot


def _adamw_parts(parts, w, m, v, layer, results, name):
    n_layers, n_rows, n_cols = w.shape
    tr = n_rows // 2 if n_rows * n_cols > 512 * 1024 else n_rows

    def body(*refs):
        p_ref, w_ref, m_ref, v_ref = refs[:4]
        g_ref, d_ref, nm_ref, nv_ref = refs[-4:]
        g = _sum_slots(p_ref)
        g_ref[0] = g
        d_ref[0], nm_ref[0], nv_ref[0] = _adam_update(w_ref[0], g, m_ref[0], v_ref[0])

    spec = pl.BlockSpec((1, tr, n_cols), lambda i: (layer, i, 0))
    shape = jax.ShapeDtypeStruct((n_layers, n_rows, n_cols), F32)
    kept = [] if results is None else list(results)
    return pl.pallas_call(
        body, name="adamw_" + name, grid=(n_rows // tr,), out_shape=(shape,) * 4,
        in_specs=[pl.BlockSpec((N_DEV, tr, n_cols), lambda i: (0, i, 0)), spec, spec, spec] + [ANY] * len(kept),
        out_specs=(spec,) * 4, input_output_aliases={4 + j: j for j in range(len(kept))},
        compiler_params=_cparams(1),
    )(parts, w, m, v, *kept)


def _adamw(w, g, m, v, name):
    n_rows, n_cols = w.shape
    tr = 512 if n_rows % 512 == 0 else n_rows

    def body(w_ref, g_ref, m_ref, v_ref, d_ref, nm_ref, nv_ref):
        d_ref[...], nm_ref[...], nv_ref[...] = _adam_update(w_ref[...], g_ref[...], m_ref[...], v_ref[...])

    spec = pl.BlockSpec((tr, n_cols), lambda i: (i, 0))
    shape = jax.ShapeDtypeStruct((n_rows, n_cols), F32)
    return pl.pallas_call(
        body, name="adamw_" + name, grid=(n_rows // tr,), out_shape=(shape, shape, shape),
        in_specs=[spec] * 4, out_specs=(spec, spec, spec), compiler_params=_cparams(1),
    )(w, g, m, v)


_SMALL = ("pre_norm_g", "ln_v_g", "ln_v_b", "b_spatial", "sinks", "rel_bias", "post_norm_g")


def _pack(pieces):
    blocks, first_rows, n = [], [], 0
    for p in pieces:
        flat = p.reshape(-1)
        flat = jnp.concatenate([flat, jnp.zeros(((-flat.shape[0]) % (8 * 128),), F32)]).reshape(-1, 128)
        blocks.append(flat)
        first_rows.append(n)
        n += flat.shape[0]
    return jnp.concatenate(blocks, axis=0), first_rows


def _unpack(block, first_row, shape):
    size = math.prod(shape)
    return block[first_row:first_row + -(-size // 128)].reshape(-1)[:size].reshape(shape)


def kernel(x, pre_norm_g, w_in, ln_v_g, ln_v_b, w_spatial, b_spatial, sinks, rel_bias, w_out, post_norm_g, loss_target, m_pre_norm_g, m_w_in, m_ln_v_g, m_ln_v_b, m_w_spatial, m_b_spatial, m_sinks, m_rel_bias, m_w_out, m_post_norm_g, v_pre_norm_g, v_w_in, v_ln_v_g, v_ln_v_b, v_w_spatial, v_b_spatial, v_sinks, v_rel_bias, v_w_out, v_post_norm_g):
    weights = dict(pre_norm_g=pre_norm_g, w_in=w_in, ln_v_g=ln_v_g, ln_v_b=ln_v_b, w_spatial=w_spatial, b_spatial=b_spatial,
                   sinks=sinks, rel_bias=rel_bias, w_out=w_out, post_norm_g=post_norm_g)
    mom_m = dict(pre_norm_g=m_pre_norm_g, w_in=m_w_in, ln_v_g=m_ln_v_g, ln_v_b=m_ln_v_b, w_spatial=m_w_spatial,
                 b_spatial=m_b_spatial, sinks=m_sinks, rel_bias=m_rel_bias, w_out=m_w_out, post_norm_g=m_post_norm_g)
    mom_v = dict(pre_norm_g=v_pre_norm_g, w_in=v_w_in, ln_v_g=v_ln_v_g, ln_v_b=v_ln_v_b, w_spatial=v_w_spatial,
                 b_spatial=v_b_spatial, sinks=v_sinks, rel_bias=v_rel_bias, w_out=v_w_out, post_norm_g=v_post_norm_g)
    n_seq, seq_len, _ = x.shape
    n_layers = w_in.shape[0]
    x2 = x.reshape(n_seq * seq_len, D_MODEL)
    target2 = loss_target.reshape(n_seq * seq_len, D_MODEL)
    row = lambda p, l: p[l][None]

    wt_shards = jnp.swapaxes(w_in, 1, 2).astype(BF16)
    wo_shards = w_out.astype(BF16)
    bucket = _window_buckets()
    bias = _bias_table(rel_bias, bucket)
    wcat, wtcat, bs = _spatial_tables(w_spatial, jnp.swapaxes(b_spatial, 1, 2))

    wt, wo = [None] * n_layers, [None] * n_layers
    wt[0], wo[0] = _gather_weights(wt_shards, wo_shards, 0)
    xs, saved = [x2], []
    for l in range(n_layers):
        layer_args = (xs[-1], row(pre_norm_g, l), wt[l], row(ln_v_g, l), row(ln_v_b, l), wcat[l], bs[l], sinks[l], bias,
                      wo[l], row(post_norm_g, l), seq_len)
        if l + 1 < n_layers:
            *acts, xn, y, wt[l + 1], wo[l + 1] = _fwd_layer(*layer_args, gather=(wt_shards, wo_shards, l + 1))
            xs.append(xn)
        else:
            *acts, dx, y, loss = _fwd_layer(*layer_args, target=target2)
        saved.append((y, *acts))

    small = [None] * n_layers
    parts = [None] * n_layers
    waiting = None
    for l in reversed(range(n_layers)):
        hosted = {} if waiting is None else dict(exchange=waiting[:2], allgather=waiting[2])
        dproj, *rest = _bwd_mix(dx, *saved[l], row(ln_v_g, l), row(ln_v_b, l), wcat[l], wtcat[l], bs[l], wo[l],
                                row(post_norm_g, l), **hosted)
        dx, dwt, dgpre = _bwd_in(dproj, xs[l], dx, row(pre_norm_g, l), wt[l])
        dwo, dws, dmix, dlng, dlnb, dgpost, dsink, dbias = rest[:8]
        if waiting is not None:
            parts[l + 1] = list(rest[8:])
        waiting = (dwt, dwo, dws.reshape(HEADS * CHUNK, CHUNK).astype(BF16))
        small[l] = dict(pre_norm_g=dgpre[0], ln_v_g=dlng[0], ln_v_b=dlnb[0], dmix=dmix, sinks=dsink, dbias=dbias,
                        post_norm_g=dgpost[0])
    parts[0] = list(_exchange_grads(*waiting))

    db = _b_spatial_grad(jnp.stack([s["dmix"] for s in small]))
    pieces = {n: jnp.stack([s[n] for s in small]) for n in ("pre_norm_g", "ln_v_g", "ln_v_b", "sinks", "post_norm_g")}
    pieces["b_spatial"] = jnp.swapaxes(db[:, :, :HEADS], 1, 2)
    pieces["rel_bias"] = _rel_bias_grad(jnp.stack([s["dbias"] for s in small]), bucket)
    packed, first_rows = _pack([pieces[n] for n in _SMALL] + [loss])
    total = _allreduce_small(packed)
    grad = {n: _unpack(total, r, weights[n].shape) for n, r in zip(_SMALL, first_rows)}
    loss_out = total[first_rows[-1], 0]

    delta, new_m, new_v = {}, {}, {}
    for n in _SMALL:
        w = weights[n]
        two_d = (-1, w.shape[-1])
        d, nm, nv = _adamw(w.reshape(two_d), grad[n].reshape(two_d), mom_m[n].reshape(two_d), mom_v[n].reshape(two_d), n)
        delta[n], new_m[n], new_v[n] = d.reshape(w.shape), nm.reshape(w.shape), nv.reshape(w.shape)

    t3 = lambda p: jnp.swapaxes(p, 1, 2)
    flat3 = lambda p: p.reshape(n_layers, HEADS * CHUNK, CHUNK)
    for n, which, view, back in (("w_in", 0, t3, t3), ("w_out", 1, lambda p: p, lambda p: p),
                                 ("w_spatial", 2, flat3, lambda p: p.reshape(w_spatial.shape))):
        results = None
        for l in reversed(range(n_layers)):
            results = _adamw_parts(parts[l][which], view(weights[n]), view(mom_m[n]), view(mom_v[n]), l, results, n)
        grad[n], delta[n], new_m[n], new_v[n] = (back(r) for r in results)

    names = tuple(weights)
    return (loss_out, dx.reshape(x.shape), *[grad[n] for n in names], *[delta[n] for n in names],
            *[new_m[n] for n in names], *[new_v[n] for n in names])
```

```python
import math

import jax
import jax.numpy as jnp
from jax import lax
from jax.experimental import pallas as pl
from jax.experimental.pallas import tpu as pltpu

F32 = jnp.float32
BF16 = jnp.bfloat16

D_MODEL = 1024
A_WIDTH = 512
B_WIDTH = 512
KV_WIDTH = 128
IN_WIDTH = 3 * A_WIDTH + 2 * B_WIDTH + 2 * KV_WIDTH
CHUNK = 128
HEADS = 8
HEAD_DIM = 64
REL_BUCKETS = 32
NORM_EPS = 1e-6
NEG = -1e30
Q_OFF = 3 * A_WIDTH
K_OFF = Q_OFF + B_WIDTH
BZ_OFF = K_OFF + 2 * KV_WIDTH
QK_SCALE = HEAD_DIM ** -0.5

ADAM_LR = 0.001
ADAM_B1 = 0.9
ADAM_B2 = 0.999
ADAM_EPS = 1e-08
ADAM_WD = 0.01
ADAM_STEP = 10

TOK_TILE = 512
CHUNK_UNROLL = 4
BWD_CHUNK_UNROLL = 4
VMEM_LIMIT_V7X = 60 * 1024 * 1024

N_DEV = 8
N_CHIPS = 4
MESH_ID = pl.DeviceIdType.MESH
ANY = pl.BlockSpec(memory_space=pl.ANY)
SMEM = pl.BlockSpec(memory_space=pltpu.SMEM)


def _cparams(n_axes):
    return pltpu.CompilerParams(dimension_semantics=("arbitrary",) * n_axes, vmem_limit_bytes=VMEM_LIMIT_V7X)


_GELU_C = math.sqrt(2.0 / math.pi)
_GELU_C3 = _GELU_C * 0.044715


def _gelu(x):
    t = jnp.tanh(x * (_GELU_C3 * (x * x) + _GELU_C))
    return x * (0.5 * t + 0.5)


def _gelu_and_grad(x):
    x2 = x * x
    t = jnp.tanh(x * (_GELU_C3 * x2 + _GELU_C))
    cdf = 0.5 * t + 0.5
    d = cdf + (x * (cdf * (1.0 - cdf))) * ((6.0 * _GELU_C3) * x2 + 2.0 * _GELU_C)
    return x * cdf, d


def _sigmoid(x):
    return 0.5 + 0.5 * jnp.tanh(0.5 * x)


def _silu(x):
    return x * _sigmoid(x)


def _silu_and_grad(x):
    s = _sigmoid(x)
    return x * s, s * (1.0 + x * (1.0 - s))


def _dot(a, b):
    return jnp.dot(a, b, preferred_element_type=F32)


def _dot_nt(a, b):
    return lax.dot_general(a, b, (((1,), (1,)), ((), ())), preferred_element_type=F32)


def _dot_tn(a, b):
    return lax.dot_general(a, b, (((0,), (0,)), ((), ())), preferred_element_type=F32)


def _lo_mask(shape):
    return lax.broadcasted_iota(jnp.int32, shape, 1) < HEAD_DIM


def _swap_halves(v):
    return pltpu.roll(v, HEAD_DIM, 1)


def _window_buckets():
    q_loc = jnp.arange(CHUNK)[:, None]
    j_loc = jnp.arange(CHUNK)[None, :]
    d = q_loc - j_loc + jnp.where(j_loc > q_loc, CHUNK, 0)
    max_exact = REL_BUCKETS // 2
    safe = jnp.maximum(d, 1).astype(F32)
    large = max_exact + (jnp.log(safe / max_exact) / math.log(CHUNK / max_exact)
                         * (REL_BUCKETS - max_exact)).astype(jnp.int32)
    large = jnp.minimum(large, REL_BUCKETS - 1)
    return jnp.where(d < max_exact, d, large).astype(jnp.int32)


def _bias_table(rel_bias, bucket):
    def body(rb_ref, bk_ref, out_ref):
        bk = bk_ref[...]
        for h in range(HEADS):
            acc = jnp.zeros(bk.shape, F32)
            for b in range(REL_BUCKETS):
                acc = jnp.where(bk == b, rb_ref[b, h], acc)
            out_ref[h] = acc

    vmem = pl.BlockSpec(memory_space=pltpu.VMEM)
    return pl.pallas_call(
        body, name="bias_table",
        out_shape=jax.ShapeDtypeStruct((HEADS, CHUNK, CHUNK), F32),
        in_specs=[SMEM, vmem], out_specs=vmem,
    )(rel_bias, bucket)


def _rel_bias_grad(dbias, bucket):
    n_layers = dbias.shape[0]

    def body(db_ref, bk_ref, out_ref):
        bk = bk_ref[...]
        for h in range(HEADS):
            tot = db_ref[0, h]
            for l in range(1, n_layers):
                tot = tot + db_ref[l, h]
            for b in range(REL_BUCKETS):
                out_ref[b, h] = jnp.sum(jnp.where(bk == b, tot, 0.0))

    vmem = pl.BlockSpec(memory_space=pltpu.VMEM)
    return pl.pallas_call(
        body, name="rel_bias_grad",
        out_shape=jax.ShapeDtypeStruct((REL_BUCKETS, HEADS), F32),
        in_specs=[vmem] * 2, out_specs=SMEM,
    )(dbias, bucket)


def _spatial_tables(w_spatial, b_spatial_t):
    n_layers = w_spatial.shape[0]

    def body(w_ref, b_ref, wcat_ref, wtcat_ref, bs_ref):
        row = lax.broadcasted_iota(jnp.int32, (CHUNK, CHUNK), 0)
        col = lax.broadcasted_iota(jnp.int32, (CHUNK, CHUNK), 1)
        causal = col <= row
        lo = _lo_mask((CHUNK, CHUNK))
        for p in range(4):
            for half in range(2):
                w = jnp.where(causal, w_ref[0, 2 * p + half], 0.0)
                wcat_ref[0, p, :, half * CHUNK:(half + 1) * CHUNK] = w.astype(BF16)
                wtcat_ref[0, p, :, half * CHUNK:(half + 1) * CHUNK] = w.T.astype(BF16)
            b = b_ref[0]
            bs_ref[0, p] = jnp.where(lo, b[:, 2 * p:2 * p + 1], b[:, 2 * p + 1:2 * p + 2])

    return pl.pallas_call(
        body, name="spatial_tables", grid=(n_layers,),
        out_shape=(jax.ShapeDtypeStruct((n_layers, 4, CHUNK, 2 * CHUNK), BF16),
                   jax.ShapeDtypeStruct((n_layers, 4, CHUNK, 2 * CHUNK), BF16),
                   jax.ShapeDtypeStruct((n_layers, 4, CHUNK, CHUNK), F32)),
        in_specs=[pl.BlockSpec((1, HEADS, CHUNK, CHUNK), lambda l: (l, 0, 0, 0)),
                  pl.BlockSpec((1, CHUNK, HEADS), lambda l: (l, 0, 0))],
        out_specs=(pl.BlockSpec((1, 4, CHUNK, 2 * CHUNK), lambda l: (l, 0, 0, 0)),
                   pl.BlockSpec((1, 4, CHUNK, 2 * CHUNK), lambda l: (l, 0, 0, 0)),
                   pl.BlockSpec((1, 4, CHUNK, CHUNK), lambda l: (l, 0, 0, 0))),
        compiler_params=_cparams(1),
    )(w_spatial, b_spatial_t)


def _b_spatial_grad(dmix):
    n_layers = dmix.shape[0]

    def body(d_ref, out_ref):
        lane = lax.broadcasted_iota(jnp.int32, (CHUNK, CHUNK), 1)
        acc = jnp.zeros((CHUNK, CHUNK), F32)
        for p in range(4):
            t = d_ref[0, :, p * CHUNK:(p + 1) * CHUNK]
            s_lo = jnp.sum(jnp.where(lane < HEAD_DIM, t, 0.0), axis=1, keepdims=True)
            s_hi = jnp.sum(jnp.where(lane < HEAD_DIM, 0.0, t), axis=1, keepdims=True)
            acc = jnp.where(lane == 2 * p, s_lo, acc)
            acc = jnp.where(lane == 2 * p + 1, s_hi, acc)
        out_ref[0] = acc

    return pl.pallas_call(
        body, name="b_spatial_grad", grid=(n_layers,),
        out_shape=jax.ShapeDtypeStruct((n_layers, CHUNK, CHUNK), F32),
        in_specs=[pl.BlockSpec((1, CHUNK, A_WIDTH), lambda l: (l, 0, 0))],
        out_specs=pl.BlockSpec((1, CHUNK, CHUNK), lambda l: (l, 0, 0)),
        compiler_params=_cparams(1),
    )(dmix)


def _place():
    x, y, c = lax.axis_index("x"), lax.axis_index("y"), lax.axis_index("c")
    other_chips = [(1 - x, y), (x, 1 - y), (1 - x, 1 - y)]
    return x, y, c, other_chips


N_GATHER_SEMS = 12


def _gather_phases(shards, fulls, send_sems, recv_sems, local_sems):
    x, y, c, chips = _place()
    sibling = (x, y, 1 - c)
    n_arr = len(shards)

    def half_rows(a, chip, half):
        n = shards[a].shape[0]
        start = (2 * chip[0] + chip[1]) * n + half * (n // 2)
        return fulls[a].at[pl.ds(pl.multiple_of(start, 16), n // 2), :]

    def my_half(a):
        n = shards[a].shape[0]
        return shards[a].at[pl.ds(pl.multiple_of(c * (n // 2), 16), n // 2), :]

    def copy(k, a, src, chip, half, to):
        return pltpu.make_async_remote_copy(
            src_ref=src, dst_ref=half_rows(a, chip, half), send_sem=send_sems.at[n_arr * k + a],
            recv_sem=recv_sems.at[n_arr * k + a], device_id=to, device_id_type=MESH_ID)

    def local(a):
        n = shards[a].shape[0]
        mine = fulls[a].at[pl.ds(pl.multiple_of((2 * x + y) * n, 16), n), :]
        return pltpu.make_async_copy(shards[a], mine, local_sems.at[a])

    def first(k, a):
        return copy(k, a, my_half(a), (x, y), c, (chips[k][0], chips[k][1], c))

    def passed(k, a):
        return copy(3 + k, a, half_rows(a, chips[k], c), chips[k], c, sibling)

    def phase_a():
        for a in range(n_arr):
            local(a).start()
        for k in range(3):
            for a in range(n_arr):
                first(k, a).start()

    def phase_b():
        for k in range(3):
            for a in range(n_arr):
                copy(k, a, my_half(a), chips[k], c, sibling).wait_recv()
                passed(k, a).start()

    def phase_c():
        for k in range(3):
            for a in range(n_arr):
                copy(3 + k, a, my_half(a), chips[k], 1 - c, sibling).wait_recv()
        for k in range(3):
            for a in range(n_arr):
                first(k, a).wait_send()
                passed(k, a).wait_send()
        for a in range(n_arr):
            local(a).wait()

    return phase_a, phase_b, phase_c


N_EXCHANGE_SEMS = 14


def _exchange_phases(partials, parts, send_sems, recv_sems, local_sems, by_chip=True):
    x, y, c, chips = _place()
    me, sibling = (x, y, c), (x, y, 1 - c)
    n_arr = len(partials)

    def block(a, chip):
        if not by_chip:
            return partials[a]
        n = partials[a].shape[0] // N_CHIPS
        return partials[a].at[pl.ds(pl.multiple_of((2 * chip[0] + chip[1]) * n, 16), n), :]

    def slot(a, dev):
        return parts[a].at[4 * dev[0] + 2 * dev[1] + dev[2]]

    def copy(k, a, src, origin, to):
        return pltpu.make_async_remote_copy(
            src_ref=src, dst_ref=slot(a, origin), send_sem=send_sems.at[n_arr * k + a],
            recv_sem=recv_sems.at[n_arr * k + a], device_id=to, device_id_type=MESH_ID)

    def local(a):
        return pltpu.make_async_copy(block(a, (x, y)), slot(a, me), local_sems.at[a])

    def first(k, a):
        if k == 0:
            return copy(0, a, block(a, (x, y)), me, sibling)
        chip = chips[k - 1]
        return copy(k, a, block(a, chip), me, (chip[0], chip[1], c))

    def passed(k, a):
        origin = (chips[k][0], chips[k][1], c)
        return copy(4 + k, a, slot(a, origin), origin, sibling)

    def phase_a():
        for a in range(n_arr):
            local(a).start()
        for k in range(4):
            for a in range(n_arr):
                first(k, a).start()

    def phase_b():
        for k in range(3):
            for a in range(n_arr):
                copy(1 + k, a, block(a, (x, y)), (chips[k][0], chips[k][1], c), me).wait_recv()
                passed(k, a).start()

    def phase_c():
        for a in range(n_arr):
            copy(0, a, block(a, (x, y)), sibling, me).wait_recv()
        for k in range(3):
            for a in range(n_arr):
                copy(4 + k, a, block(a, (x, y)), (chips[k][0], chips[k][1], 1 - c), me).wait_recv()
        for k in range(4):
            for a in range(n_arr):
                first(k, a).wait_send()
        for k in range(3):
            for a in range(n_arr):
                passed(k, a).wait_send()
        for a in range(n_arr):
            local(a).wait()

    return phase_a, phase_b, phase_c


def _comm_scratch(n_sems):
    return [pltpu.SemaphoreType.DMA((n_sems,)), pltpu.SemaphoreType.DMA((n_sems,)), pltpu.SemaphoreType.DMA((2,))]


def _gather_weights(wt_shards, wo_shards, layer):
    wt_rows, wo_rows = wt_shards.shape[1], wo_shards.shape[1]

    def body(wt_ref, wo_ref, wt_full, wo_full, send_sems, recv_sems, local_sems):
        phases = _gather_phases((wt_ref.at[layer], wo_ref.at[layer]), (wt_full, wo_full), send_sems, recv_sems, local_sems)
        for phase in phases:
            phase()

    return pl.pallas_call(
        body, name="gather_weights",
        out_shape=(jax.ShapeDtypeStruct((N_CHIPS * wt_rows, D_MODEL), BF16),
                   jax.ShapeDtypeStruct((N_CHIPS * wo_rows, D_MODEL), BF16)),
        in_specs=[ANY, ANY], out_specs=(ANY, ANY), scratch_shapes=_comm_scratch(N_GATHER_SEMS),
    )(wt_shards, wo_shards)


def _allreduce_small(part):
    n_rows = part.shape[0]

    def body(p_ref, tot_ref, all_ref, send_sems, recv_sems, local_sem):
        x, y, c, chips = _place()
        me, sibling = (x, y, c), (x, y, 1 - c)

        def rows(dev):
            return all_ref.at[pl.ds(pl.multiple_of((4 * dev[0] + 2 * dev[1] + dev[2]) * n_rows, 8), n_rows), :]

        def copy(k, origin, to, src=None):
            return pltpu.make_async_remote_copy(
                src_ref=rows(origin) if src is None else src, dst_ref=rows(origin), send_sem=send_sems.at[k],
                recv_sem=recv_sems.at[k], device_id=to, device_id_type=MESH_ID)

        mine = pltpu.make_async_copy(p_ref, rows(me), local_sem)
        mine.start()
        first = [copy(0, me, sibling, src=p_ref)]
        first += [copy(1 + k, me, (chip[0], chip[1], c), src=p_ref) for k, chip in enumerate(chips)]
        for cp in first:
            cp.start()
        passed = []
        for k, chip in enumerate(chips):
            origin = (chip[0], chip[1], c)
            copy(1 + k, origin, me).wait_recv()
            fwd = copy(4 + k, origin, sibling)
            fwd.start()
            passed.append(fwd)
        copy(0, sibling, me).wait_recv()
        for k, chip in enumerate(chips):
            copy(4 + k, (chip[0], chip[1], 1 - c), me).wait_recv()
        for cp in first + passed:
            cp.wait_send()
        mine.wait()
        tot = all_ref[0:n_rows, :]
        for d in range(1, N_DEV):
            tot = tot + all_ref[d * n_rows:(d + 1) * n_rows, :]
        tot_ref[...] = tot

    vmem = pl.BlockSpec(memory_space=pltpu.VMEM)
    return pl.pallas_call(
        body, name="allreduce_small",
        out_shape=jax.ShapeDtypeStruct((n_rows, 128), F32),
        in_specs=[vmem], out_specs=vmem,
        scratch_shapes=[pltpu.VMEM((N_DEV * n_rows, 128), F32), pltpu.SemaphoreType.DMA((7,)),
                        pltpu.SemaphoreType.DMA((7,)), pltpu.SemaphoreType.DMA],
        compiler_params=pltpu.CompilerParams(vmem_limit_bytes=VMEM_LIMIT_V7X),
    )(part)


def _hosted(phases, step, n_steps, pass_on_step=None):
    phase_a, phase_b, phase_c = phases
    if pass_on_step is None:
        pass_on_step = (3 * n_steps) // 4

    def at_start():
        pl.when(step == 0)(phase_a)

    def at_end():
        pl.when(step == pass_on_step)(phase_b)
        pl.when(step == n_steps - 1)(phase_c)

    return at_start, at_end


def _layer_norm_stats(vv):
    mu = jnp.mean(vv, axis=-1, keepdims=True)
    xc = vv - mu
    rs = lax.rsqrt(jnp.mean(xc * xc, axis=-1, keepdims=True) + NORM_EPS)
    return xc * rs, rs


def _blockdiag(v, lo):
    zero = jnp.zeros_like(v)
    return jnp.concatenate([jnp.where(lo, v, zero), jnp.where(lo, zero, v)], axis=0)


def _softmax_sink(s, sink):
    m = jnp.maximum(jnp.max(s, axis=-1, keepdims=True), sink)
    e = jnp.exp(s - m)
    esink = jnp.exp(sink - m)
    den = jnp.sum(e, axis=-1, keepdims=True) + esink
    return e / den


def _kv_rows(cur_ref, halo_ref, r0, c):
    prev_in_tile = cur_ref[pl.ds(pl.multiple_of(jnp.maximum(r0 - CHUNK, 0), CHUNK), CHUNK), :]
    prev = jnp.where(c == 0, halo_ref[...], prev_in_tile)
    kv2 = jnp.concatenate([prev, cur_ref[pl.ds(r0, CHUNK), :]], axis=0)
    k2, v2 = kv2[:, 0:KV_WIDTH], kv2[:, KV_WIDTH:2 * KV_WIDTH]
    return (k2, _swap_halves(k2)), (v2, _swap_halves(v2))


def _window_square(over_keys, prev):
    return jnp.where(prev, over_keys[:, 0:CHUNK], over_keys[:, CHUNK:2 * CHUNK])


def _window_keys(square, prev):
    zero = jnp.zeros_like(square)
    return jnp.concatenate([jnp.where(prev, square, zero), jnp.where(prev, zero, square)], axis=1)


def _ahead(shape=(CHUNK, CHUNK)):
    return lax.broadcasted_iota(jnp.int32, shape, 1) - lax.broadcasted_iota(jnp.int32, shape, 0)


def _dead_mask(first, ahead):
    return ahead > jnp.where(first, 0, CHUNK)


def _head_of(p, half):
    return 2 * p + half, int(half != p // 2)


_HEADS_BY_COPY = tuple(tuple((p, half) for p in range(4) for half in range(2) if _head_of(p, half)[1] == sw)
                       for sw in range(2))


def _masked_halves(tile, lo):
    zero = jnp.zeros_like(tile)
    return {0: jnp.where(lo, tile, zero), 1: jnp.where(lo, zero, tile)}


def _query_stacks(q_tiles, lo):
    qm = {p: _masked_halves(q_tiles[p] * QK_SCALE, lo) for p in range(4)}
    return [jnp.concatenate([qm[p][half] for p, half in members], axis=0) for members in _HEADS_BY_COPY]


def _attention_probs(q_stacks, ks, bias_ref, sink_ref, dead, prev):
    probs = {}
    for sw, members in enumerate(_HEADS_BY_COPY):
        s_stack = _dot_nt(q_stacks[sw], ks[sw])
        for i, (p, half) in enumerate(members):
            head = 2 * p + half
            s = _window_square(s_stack[i * CHUNK:(i + 1) * CHUNK, :], prev) + bias_ref[head]
            s = jnp.where(dead, NEG, s)
            probs[p, half] = _softmax_sink(s, sink_ref[head]).astype(BF16)
    return probs


def _attention_values(probs, vs, prev):
    outs, p_stacks = {}, []
    for sw, members in enumerate(_HEADS_BY_COPY):
        p_stack = jnp.concatenate([_window_keys(probs[m], prev) for m in members], axis=0)
        r_stack = _dot(p_stack, vs[sw])
        p_stacks.append(p_stack)
        for i, m in enumerate(members):
            outs[m] = r_stack[i * CHUNK:(i + 1) * CHUNK, :]
    return outs, p_stacks


def _fwd_layer(x, g_pre, w_in_t, ln_g, ln_b, wcat, bs, sinks, bias, w_out, g_post, seq_len, gather=None, target=None):
    n_tok = x.shape[0]
    tb = TOK_TILE
    n_chunks = tb // CHUNK
    n_tiles = n_tok // tb
    n_in = 11
    assert gather is None or target is None

    def body(*refs):
        (x_ref, gpre_ref, wt_ref, lng_ref, lnb_ref, wcat_ref, bs_ref, sink_ref, bias_ref, wout_ref,
         gpost_ref) = refs[:n_in]
        at_start = at_end = lambda: None
        if gather is not None:
            (wts_ref, wos_ref, a_ref, q_ref, kv_ref, bz_ref, probs_ref, xn_ref, y_ref, wt_full, wo_full, ycat_s, halo_ref,
             send_sems, recv_sems, local_sems) = refs[n_in:]
            phases = _gather_phases((wts_ref.at[gather[2]], wos_ref.at[gather[2]]), (wt_full, wo_full), send_sems,
                                    recv_sems, local_sems)
            at_start, at_end = _hosted(phases, pl.program_id(0), n_tiles)
        elif target is not None:
            target_ref, a_ref, q_ref, kv_ref, bz_ref, probs_ref, xn_ref, y_ref, loss_ref, ycat_s, halo_ref = refs[n_in:]
        else:
            a_ref, q_ref, kv_ref, bz_ref, probs_ref, xn_ref, y_ref, ycat_s, halo_ref = refs[n_in:]
        at_start()
        i = pl.program_id(0)
        lo = _lo_mask((CHUNK, CHUNK))
        ahead = _ahead()
        prev = ahead > 0

        @pl.when(i == 0)
        def _():
            halo_ref[...] = jnp.zeros_like(halo_ref)

        xf = x_ref[...]
        r1 = lax.rsqrt(jnp.mean(xf * xf, axis=-1, keepdims=True) + NORM_EPS)
        h = ((xf * r1) * gpre_ref[...]).astype(BF16)
        a_ref[...] = _dot_nt(h, wt_ref[0:Q_OFF, :])
        q_ref[...] = _dot_nt(h, wt_ref[Q_OFF:K_OFF, :]).astype(BF16)
        kv_ref[...] = _dot_nt(h, wt_ref[K_OFF:BZ_OFF, :]).astype(BF16)
        bz_ref[...] = _dot_nt(h, wt_ref[BZ_OFF:IN_WIDTH, :])

        def chunk(c, carry):
            r0 = pl.multiple_of(c * CHUNK, CHUNK)
            rows = pl.ds(r0, CHUNK)
            u = _gelu(a_ref[rows, 0:A_WIDTH])
            vv = _gelu(a_ref[rows, A_WIDTH:2 * A_WIDTH])
            xhat, _ = _layer_norm_stats(vv)
            vnb = (xhat * lng_ref[...] + lnb_ref[...]).astype(BF16)
            for p in range(4):
                blk = slice(p * CHUNK, (p + 1) * CHUNK)
                mixed = _dot(wcat_ref[p], _blockdiag(vnb[:, blk], lo)) + bs_ref[p]
                sz = _silu(a_ref[rows, 2 * A_WIDTH + p * CHUNK:2 * A_WIDTH + (p + 1) * CHUNK])
                ycat_s[rows, blk] = ((u[:, blk] * mixed) * sz).astype(BF16)
            ks, vs = _kv_rows(kv_ref, halo_ref, r0, c)
            dead = _dead_mask(lax.rem(i * tb + r0, seq_len) == 0, ahead)
            q_tiles = [q_ref[rows, p * CHUNK:(p + 1) * CHUNK] for p in range(4)]
            probs = _attention_probs(_query_stacks(q_tiles, lo), ks, bias_ref, sink_ref, dead, prev)
            for (p, half), prob in probs.items():
                head = 2 * p + half
                probs_ref[rows, head * CHUNK:(head + 1) * CHUNK] = prob
            outs, _ = _attention_values(probs, vs, prev)
            for p in range(4):
                blk = slice(p * CHUNK, (p + 1) * CHUNK)
                o = jnp.where(lo, outs[p, 0], outs[p, 1])
                ycat_s[rows, B_WIDTH + p * CHUNK:B_WIDTH + (p + 1) * CHUNK] = (o * _silu(bz_ref[rows, blk])).astype(BF16)
            return carry

        lax.fori_loop(0, n_chunks, chunk, 0, unroll=CHUNK_UNROLL)
        halo_ref[...] = kv_ref[tb - CHUNK:tb, :]
        y = _dot(ycat_s[...], wout_ref[...])
        r = lax.rsqrt(jnp.mean(y * y, axis=-1, keepdims=True) + NORM_EPS)
        y_ref[...] = y
        xn = x_ref[...] + (y * r) * gpost_ref[...]
        if target is None:
            xn_ref[...] = xn
        else:
            d = xn - target_ref[...]
            xn_ref[...] = d * (1.0 / D_MODEL)

            @pl.when(i == 0)
            def _():
                loss_ref[0, 0] = 0.0

            loss_ref[0, 0] += 0.5 * jnp.sum(jnp.mean(d * d, axis=-1, keepdims=True))
        at_end()

    tile = lambda w: pl.BlockSpec((tb, w), lambda i: (i, 0))
    whole = lambda shape, **kw: pl.BlockSpec(shape, lambda i: (0,) * len(shape), **kw)
    in_specs = [tile(D_MODEL), whole((1, D_MODEL)), whole((IN_WIDTH, D_MODEL), pipeline_mode=pl.Buffered(1)),
                whole((1, A_WIDTH)), whole((1, A_WIDTH)), whole((4, CHUNK, 2 * CHUNK)), whole((4, CHUNK, CHUNK)), SMEM,
                whole((HEADS, CHUNK, CHUNK)), whole((D_MODEL, D_MODEL)), whole((1, D_MODEL))]
    out_shape = [jax.ShapeDtypeStruct((n_tok, Q_OFF), F32), jax.ShapeDtypeStruct((n_tok, B_WIDTH), BF16),
                 jax.ShapeDtypeStruct((n_tok, 2 * KV_WIDTH), BF16), jax.ShapeDtypeStruct((n_tok, B_WIDTH), F32),
                 jax.ShapeDtypeStruct((n_tok, HEADS * CHUNK), BF16), jax.ShapeDtypeStruct((n_tok, D_MODEL), F32),
                 jax.ShapeDtypeStruct((n_tok, D_MODEL), F32)]
    out_specs = [tile(Q_OFF), tile(B_WIDTH), tile(2 * KV_WIDTH), tile(B_WIDTH), tile(HEADS * CHUNK), tile(D_MODEL),
                 tile(D_MODEL)]
    scratch = [pltpu.VMEM((tb, D_MODEL), BF16), pltpu.VMEM((CHUNK, 2 * KV_WIDTH), BF16)]
    args = [x, g_pre, w_in_t, ln_g, ln_b, wcat, bs, sinks, bias, w_out, g_post]
    if gather is not None:
        wt_shards, wo_shards, _ = gather
        in_specs += [ANY, ANY]
        args += [wt_shards, wo_shards]
        out_shape += [jax.ShapeDtypeStruct((N_CHIPS * wt_shards.shape[1], D_MODEL), BF16),
                      jax.ShapeDtypeStruct((N_CHIPS * wo_shards.shape[1], D_MODEL), BF16)]
        out_specs += [ANY, ANY]
        scratch += _comm_scratch(N_GATHER_SEMS)
    if target is not None:
        in_specs.append(tile(D_MODEL))
        args.append(target)
        out_shape.append(jax.ShapeDtypeStruct((1, 1), F32))
        out_specs.append(SMEM)
    name = "fwd_layer" + ("" if gather is None else "_gather") + ("" if target is None else "_loss")
    return pl.pallas_call(
        body, name=name, grid=(n_tiles,),
        out_shape=tuple(out_shape), in_specs=in_specs, out_specs=tuple(out_specs), scratch_shapes=scratch,
        compiler_params=_cparams(1),
    )(*args)


def _bwd_mix(dout, y, a, q, kv, bz, probs, ln_g, ln_b, wcat, wtcat, bs, w_out, g_post, exchange=None, allgather=None):
    n_tok = y.shape[0]
    tb = TOK_TILE
    n_chunks = tb // CHUNK
    n_tiles = n_tok // tb

    def body(*refs):
        refs = list(refs)
        take = lambda n: [refs.pop(0) for _ in range(n)]
        (dout_ref, y_ref, a_ref, q_ref, kv_ref, halo_ref, bz_ref, probs_ref, lng_ref, lnb_ref, wcat_ref, wtcat_ref, bs_ref,
         wout_ref, gpost_ref) = take(15)
        if exchange is not None:
            dwt_in, dwo_in = take(2)
        if allgather is not None:
            (small_in,) = take(1)
        dproj_ref, dwout_ref, dws_ref, dmix_ref, dlng_ref, dlnb_ref, dgpost_ref, dsink_ref, dbias_ref = take(9)
        if exchange is not None:
            pt_ref, po_ref = take(2)
        if allgather is not None:
            (small_all,) = take(1)
        ycat_s, dy_s, dyc_s, dkv_s, carry_s, dwout_s, stage_s = take(7)
        starts, ends = [], []
        if exchange is not None:
            hosted = _hosted(_exchange_phases((dwt_in, dwo_in), (pt_ref, po_ref), *take(3)), pl.program_id(0), n_tiles)
            starts.append(hosted[0])
            ends.append(hosted[1])
        if allgather is not None:
            hosted = _hosted(_exchange_phases((small_in,), (small_all,), *take(3), by_chip=False), pl.program_id(0),
                             n_tiles)
            starts.append(hosted[0])
            ends.append(hosted[1])
        at_start = lambda: [f() for f in starts]
        at_end = lambda: [f() for f in ends]
        at_start()
        step = pl.program_id(0)
        lo = _lo_mask((CHUNK, CHUNK))
        prev = _ahead() > 0

        @pl.when(step == 0)
        def _():
            dwout_s[...] = jnp.zeros_like(dwout_s)
            dws_ref[...] = jnp.zeros_like(dws_ref)
            dmix_ref[...] = jnp.zeros_like(dmix_ref)
            dlng_ref[...] = jnp.zeros_like(dlng_ref)
            dlnb_ref[...] = jnp.zeros_like(dlnb_ref)
            dgpost_ref[...] = jnp.zeros_like(dgpost_ref)
            dbias_ref[...] = jnp.zeros_like(dbias_ref)
            carry_s[...] = jnp.zeros_like(carry_s)
            for h in range(HEADS):
                dsink_ref[h] = 0.0

        for r0 in range(0, tb, CHUNK):
            rows = slice(r0, r0 + CHUNK)
            yv = y_ref[rows, :]
            dout = dout_ref[rows, :]
            r = lax.rsqrt(jnp.mean(yv * yv, axis=-1, keepdims=True) + NORM_EPS)
            yn = yv * r
            dgpost_ref[...] += jnp.sum(dout * yn, axis=0, keepdims=True)
            dyn = dout * gpost_ref[...]
            dy_s[rows, :] = (r * (dyn - yn * jnp.mean(dyn * yn, axis=-1, keepdims=True))).astype(BF16)
        dyc_s[...] = _dot_nt(dy_s[...], wout_ref[...])
        dkv_s[0:tb, :] = jnp.zeros((tb, 2 * KV_WIDTH), F32)
        dkv_s[tb:tb + CHUNK, :] = carry_s[...]

        def chunk(c, carry):
            r0 = pl.multiple_of(c * CHUNK, CHUNK)
            rows = pl.ds(r0, CHUNK)
            u, gu = _gelu_and_grad(a_ref[rows, 0:A_WIDTH])
            vv, gv = _gelu_and_grad(a_ref[rows, A_WIDTH:2 * A_WIDTH])
            xhat, rs = _layer_norm_stats(vv)
            vnb = (xhat * lng_ref[...] + lnb_ref[...]).astype(BF16)
            d_vn, d_u, d_az = [], [], []
            for p in range(4):
                blk = slice(p * CHUNK, (p + 1) * CHUNK)
                vbd = _blockdiag(vnb[:, blk], lo)
                mixed = _dot(wcat_ref[p], vbd) + bs_ref[p]
                sz, gz = _silu_and_grad(a_ref[rows, 2 * A_WIDTH + p * CHUNK:2 * A_WIDTH + (p + 1) * CHUNK])
                ub = u[:, blk]
                dya = dyc_s[rows, blk]
                um = ub * mixed
                ycat_s[rows, blk] = (um * sz).astype(BF16)
                d_mixed = (dya * ub) * sz
                d_u.append((dya * mixed) * sz)
                d_az.append((dya * um) * gz)
                dmix_ref[:, blk] += d_mixed
                dmbd = _blockdiag(d_mixed.astype(BF16), lo)
                d_vn.append(_dot(wtcat_ref[p], dmbd))
                dws_ref[p] += _dot_nt(dmbd, vnb[:, blk])
            d_vn = jnp.concatenate(d_vn, axis=1)
            dlng_ref[...] += jnp.sum(d_vn * xhat, axis=0, keepdims=True)
            dlnb_ref[...] += jnp.sum(d_vn, axis=0, keepdims=True)
            dxh = d_vn * lng_ref[...]
            d_vv = rs * (dxh - jnp.mean(dxh, axis=-1, keepdims=True)
                         - xhat * jnp.mean(dxh * xhat, axis=-1, keepdims=True))
            dproj_ref[rows, 0:A_WIDTH] = (jnp.concatenate(d_u, axis=1) * gu).astype(BF16)
            dproj_ref[rows, A_WIDTH:2 * A_WIDTH] = (d_vv * gv).astype(BF16)
            dproj_ref[rows, 2 * A_WIDTH:Q_OFF] = jnp.concatenate(d_az, axis=1).astype(BF16)
            ks, vs = _kv_rows(kv_ref, halo_ref, r0, c)
            q_stacks = _query_stacks([q_ref[rows, p * CHUNK:(p + 1) * CHUNK] for p in range(4)], lo)
            probs = {(p, half): probs_ref[rows, (2 * p + half) * CHUNK:(2 * p + half + 1) * CHUNK]
                     for p in range(4) for half in range(2)}
            outs_h, p_stacks = _attention_values(probs, vs, prev)
            dom = {}
            for p in range(4):
                blk = slice(p * CHUNK, (p + 1) * CHUNK)
                sb, gb = _silu_and_grad(bz_ref[rows, blk])
                dyb = dyc_s[rows, B_WIDTH + p * CHUNK:B_WIDTH + (p + 1) * CHUNK]
                o = jnp.where(lo, outs_h[p, 0], outs_h[p, 1])
                ycat_s[rows, B_WIDTH + p * CHUNK:B_WIDTH + (p + 1) * CHUNK] = (o * sb).astype(BF16)
                dproj_ref[rows, BZ_OFF + p * CHUNK:BZ_OFF + (p + 1) * CHUNK] = ((dyb * o) * gb).astype(BF16)
                dom[p] = _masked_halves((dyb * sb).astype(BF16), lo)
            dqs, dk_by_copy, dv_by_copy = {}, [], []
            for sw, members in enumerate(_HEADS_BY_COPY):
                do_stack = jnp.concatenate([dom[p][half] for p, half in members], axis=0)
                dp_stack = _dot_nt(do_stack, vs[sw])
                dsws = []
                for i, (p, half) in enumerate(members):
                    head = 2 * p + half
                    dp = _window_square(dp_stack[i * CHUNK:(i + 1) * CHUNK, :], prev)
                    prob = probs[p, half].astype(F32)
                    delta = jnp.sum(prob * dp, axis=-1, keepdims=True)
                    ds = prob * (dp - delta)
                    dsink_ref[head] += -jnp.sum((1.0 - jnp.sum(prob, axis=-1, keepdims=True)) * delta)
                    dbias_ref[head] += ds
                    dsws.append(_window_keys(ds.astype(BF16), prev))
                ds_stack = jnp.concatenate(dsws, axis=0)
                dq_stack = _dot(ds_stack, ks[sw])
                for i, m in enumerate(members):
                    dqs[m] = dq_stack[i * CHUNK:(i + 1) * CHUNK, :]
                dk_by_copy.append(_dot_tn(ds_stack, q_stacks[sw]))
                dv_by_copy.append(_dot_tn(p_stacks[sw], do_stack))
            for p in range(4):
                dq = jnp.where(lo, dqs[p, 0], dqs[p, 1]) * QK_SCALE
                dproj_ref[rows, Q_OFF + p * CHUNK:Q_OFF + (p + 1) * CHUNK] = dq.astype(BF16)
            both = pl.ds(r0, 2 * CHUNK)
            dkv_s[both, 0:KV_WIDTH] += dk_by_copy[0] + _swap_halves(dk_by_copy[1])
            dkv_s[both, KV_WIDTH:2 * KV_WIDTH] += dv_by_copy[0] + _swap_halves(dv_by_copy[1])
            return carry

        lax.fori_loop(0, n_chunks, chunk, 0, unroll=BWD_CHUNK_UNROLL)
        dwout_s[...] += _dot_tn(ycat_s[...], dy_s[...])
        dproj_ref[:, K_OFF:BZ_OFF] = dkv_s[CHUNK:CHUNK + tb, :].astype(BF16)
        carry_s[...] = dkv_s[0:CHUNK, :]

        @pl.when(step == n_tiles - 1)
        def _():
            row = lax.broadcasted_iota(jnp.int32, (2 * CHUNK, CHUNK), 0)
            col = lax.broadcasted_iota(jnp.int32, (2 * CHUNK, CHUNK), 1)
            causal = col <= jnp.where(row >= CHUNK, row - CHUNK, row)
            for p in range(4):
                dws_ref[p] = jnp.where(causal, dws_ref[p], 0.0)
            rows = stage_s.shape[0]
            for r0 in range(0, D_MODEL, rows):
                stage_s[...] = dwout_s[r0:r0 + rows, :].astype(BF16)
                pltpu.sync_copy(stage_s, dwout_ref.at[r0:r0 + rows, :])

        at_end()

    tile = lambda w: pl.BlockSpec((tb, w), lambda s: (n_tiles - 1 - s, 0))
    whole = lambda shape, **kw: pl.BlockSpec(shape, lambda s: (0,) * len(shape), **kw)
    once = dict(pipeline_mode=pl.Buffered(1))
    in_specs = [tile(D_MODEL), tile(D_MODEL), tile(Q_OFF), tile(B_WIDTH), tile(2 * KV_WIDTH),
                pl.BlockSpec((CHUNK, 2 * KV_WIDTH), lambda s: (jnp.maximum((n_tiles - 1 - s) * n_chunks - 1, 0), 0)),
                tile(B_WIDTH), tile(HEADS * CHUNK), whole((1, A_WIDTH)), whole((1, A_WIDTH)), whole((4, CHUNK, 2 * CHUNK)),
                whole((4, CHUNK, 2 * CHUNK)), whole((4, CHUNK, CHUNK)), whole((D_MODEL, D_MODEL), **once),
                whole((1, D_MODEL))]
    args = [dout, y, a, q, kv, kv, bz, probs, ln_g, ln_b, wcat, wtcat, bs, w_out, g_post]
    small_shapes = [jax.ShapeDtypeStruct((D_MODEL, D_MODEL), BF16), jax.ShapeDtypeStruct((4, 2 * CHUNK, CHUNK), F32),
                    jax.ShapeDtypeStruct((CHUNK, A_WIDTH), F32), jax.ShapeDtypeStruct((1, A_WIDTH), F32),
                    jax.ShapeDtypeStruct((1, A_WIDTH), F32), jax.ShapeDtypeStruct((1, D_MODEL), F32),
                    jax.ShapeDtypeStruct((HEADS,), F32), jax.ShapeDtypeStruct((HEADS, CHUNK, CHUNK), F32)]
    small_specs = [ANY, whole((4, 2 * CHUNK, CHUNK)), whole((CHUNK, A_WIDTH)), whole((1, A_WIDTH)), whole((1, A_WIDTH)),
                   whole((1, D_MODEL)), SMEM, whole((HEADS, CHUNK, CHUNK))]
    scratch = [pltpu.VMEM((tb, D_MODEL), BF16), pltpu.VMEM((tb, D_MODEL), BF16), pltpu.VMEM((tb, D_MODEL), F32),
               pltpu.VMEM((tb + CHUNK, 2 * KV_WIDTH), F32), pltpu.VMEM((CHUNK, 2 * KV_WIDTH), F32),
               pltpu.VMEM((D_MODEL, D_MODEL), F32), pltpu.VMEM((2 * CHUNK, D_MODEL), BF16)]
    out_shape = [jax.ShapeDtypeStruct((n_tok, IN_WIDTH), BF16)] + small_shapes
    out_specs = [tile(IN_WIDTH)] + small_specs
    if exchange is not None:
        dwt, dwo = exchange
        in_specs += [ANY, ANY]
        args += [dwt, dwo]
        out_shape += [jax.ShapeDtypeStruct((N_DEV, dwt.shape[0] // N_CHIPS, D_MODEL), BF16),
                      jax.ShapeDtypeStruct((N_DEV, dwo.shape[0] // N_CHIPS, D_MODEL), BF16)]
        out_specs += [ANY, ANY]
    if allgather is not None:
        in_specs.append(ANY)
        args.append(allgather)
        out_shape.append(jax.ShapeDtypeStruct((N_DEV,) + allgather.shape, allgather.dtype))
        out_specs.append(ANY)
    for hosted in (exchange, allgather):
        if hosted is not None:
            scratch += _comm_scratch(N_EXCHANGE_SEMS)
    return pl.pallas_call(
        body, name="bwd_mix" + ("" if exchange is None and allgather is None else "_exchange"), grid=(n_tiles,),
        out_shape=tuple(out_shape), in_specs=in_specs, out_specs=tuple(out_specs), scratch_shapes=scratch,
        compiler_params=_cparams(1),
    )(*args)


def _bwd_in(dproj, x, dout, g_pre, w_in_t):
    n_tok = x.shape[0]
    tm = TOK_TILE
    n_tiles = n_tok // tm

    def body(dp_ref, x_ref, dout_ref, g_ref, wt_ref, dx_ref, dwt_ref, dg_ref, acc_s):
        i = pl.program_id(0)

        @pl.when(i == 0)
        def _():
            acc_s[...] = jnp.zeros_like(acc_s)
            dg_ref[...] = jnp.zeros_like(dg_ref)

        xf = x_ref[...]
        r = lax.rsqrt(jnp.mean(xf * xf, axis=-1, keepdims=True) + NORM_EPS)
        xn = xf * r
        h = (xn * g_ref[...]).astype(BF16)
        dp = dp_ref[...]
        dh = _dot(dp, wt_ref[...])
        acc_s[...] += _dot_tn(dp, h)
        dg_ref[...] += jnp.sum(dh * xn, axis=0, keepdims=True)
        dhn = dh * g_ref[...]
        dx_ref[...] = dout_ref[...] + r * (dhn - xn * jnp.mean(dhn * xn, axis=-1, keepdims=True))

        @pl.when(i == n_tiles - 1)
        def _():
            dwt_ref[...] = acc_s[...].astype(BF16)

    return pl.pallas_call(
        body, name="bwd_in", grid=(n_tiles,),
        out_shape=(jax.ShapeDtypeStruct((n_tok, D_MODEL), F32), jax.ShapeDtypeStruct((IN_WIDTH, D_MODEL), BF16),
                   jax.ShapeDtypeStruct((1, D_MODEL), F32)),
        in_specs=[pl.BlockSpec((tm, IN_WIDTH), lambda i: (i, 0)), pl.BlockSpec((tm, D_MODEL), lambda i: (i, 0)),
                  pl.BlockSpec((tm, D_MODEL), lambda i: (i, 0)), pl.BlockSpec((1, D_MODEL), lambda i: (0, 0)),
                  pl.BlockSpec((IN_WIDTH, D_MODEL), lambda i: (0, 0), pipeline_mode=pl.Buffered(1))],
        out_specs=(pl.BlockSpec((tm, D_MODEL), lambda i: (i, 0)),
                   pl.BlockSpec((IN_WIDTH, D_MODEL), lambda i: (0, 0), pipeline_mode=pl.Buffered(1)),
                   pl.BlockSpec((1, D_MODEL), lambda i: (0, 0))),
        scratch_shapes=[pltpu.VMEM((IN_WIDTH, D_MODEL), F32)],
        compiler_params=_cparams(1),
    )(dproj, x, dout, g_pre, w_in_t)


def _riders(exchanges):
    args = [a for arrays, _ in exchanges for a in arrays]
    shapes = [jax.ShapeDtypeStruct((N_DEV, a.shape[0] // N_CHIPS if by_chip else a.shape[0]) + a.shape[1:], a.dtype)
              for arrays, by_chip in exchanges for a in arrays]
    scratch = [s for _ in exchanges for s in _comm_scratch(N_EXCHANGE_SEMS)]

    def bind(in_refs, out_refs, sem_refs, step, n_steps, pass_on_step=None):
        in_refs, out_refs, sem_refs = list(in_refs), list(out_refs), list(sem_refs)
        starts, ends = [], []
        for arrays, by_chip in exchanges:
            n = len(arrays)
            phases = _exchange_phases(tuple(in_refs[:n]), tuple(out_refs[:n]), *sem_refs[:3], by_chip=by_chip)
            del in_refs[:n], out_refs[:n], sem_refs[:3]
            at_start, at_end = _hosted(phases, step, n_steps, pass_on_step)
            starts.append(at_start)
            ends.append(at_end)
        return (lambda: [f() for f in starts]), (lambda: [f() for f in ends])

    return args, [ANY] * len(args), shapes, [ANY] * len(shapes), scratch, bind


def _bwd_in_dw(dproj, x, g_pre, exchanges):
    n_tok = x.shape[0]
    tm = TOK_TILE
    n_tiles = n_tok // tm
    r_args, r_in_specs, r_shapes, r_out_specs, r_scratch, bind = _riders(exchanges)

    def body(*refs):
        dp_ref, x_ref, g_ref = refs[:3]
        in_refs = refs[3:3 + len(r_args)]
        dwt_ref = refs[3 + len(r_args)]
        out_refs = refs[4 + len(r_args):4 + 2 * len(r_args)]
        acc_s = refs[4 + 2 * len(r_args)]
        i = pl.program_id(0)
        at_start, at_end = bind(in_refs, out_refs, refs[5 + 2 * len(r_args):], i, n_tiles)
        at_start()

        @pl.when(i == 0)
        def _():
            acc_s[...] = jnp.zeros_like(acc_s)

        xf = x_ref[...]
        r = lax.rsqrt(jnp.mean(xf * xf, axis=-1, keepdims=True) + NORM_EPS)
        acc_s[...] += _dot_tn(dp_ref[...], ((xf * r) * g_ref[...]).astype(BF16))

        @pl.when(i == n_tiles - 1)
        def _():
            dwt_ref[...] = acc_s[...].astype(BF16)

        at_end()

    return pl.pallas_call(
        body, name="bwd_in_dw", grid=(n_tiles,),
        out_shape=(jax.ShapeDtypeStruct((IN_WIDTH, D_MODEL), BF16), *r_shapes),
        in_specs=[pl.BlockSpec((tm, IN_WIDTH), lambda i: (i, 0)), pl.BlockSpec((tm, D_MODEL), lambda i: (i, 0)),
                  pl.BlockSpec((1, D_MODEL), lambda i: (0, 0))] + r_in_specs,
        out_specs=(pl.BlockSpec((IN_WIDTH, D_MODEL), lambda i: (0, 0), pipeline_mode=pl.Buffered(1)), *r_out_specs),
        scratch_shapes=[pltpu.VMEM((IN_WIDTH, D_MODEL), F32)] + r_scratch,
        compiler_params=_cparams(1),
    )(dproj, x, g_pre, *r_args)


def _bwd_in_dx(dproj, x, dout, g_pre, w_in_t, exchanges):
    n_tok = x.shape[0]
    tm = TOK_TILE
    n_tiles = n_tok // tm
    r_args, r_in_specs, r_shapes, r_out_specs, r_scratch, bind = _riders(exchanges)

    def body(*refs):
        dp_ref, x_ref, dout_ref, g_ref, wt_ref = refs[:5]
        in_refs = refs[5:5 + len(r_args)]
        dx_ref, dg_ref = refs[5 + len(r_args):7 + len(r_args)]
        out_refs = refs[7 + len(r_args):7 + 2 * len(r_args)]
        i = pl.program_id(0)
        at_start, at_end = bind(in_refs, out_refs, refs[7 + 2 * len(r_args):], i, n_tiles, pass_on_step=n_tiles - 2)
        at_start()

        @pl.when(i == 0)
        def _():
            dg_ref[...] = jnp.zeros_like(dg_ref)

        xf = x_ref[...]
        r = lax.rsqrt(jnp.mean(xf * xf, axis=-1, keepdims=True) + NORM_EPS)
        xn = xf * r
        dh = _dot(dp_ref[...], wt_ref[...])
        dg_ref[...] += jnp.sum(dh * xn, axis=0, keepdims=True)
        dhn = dh * g_ref[...]
        dx_ref[...] = dout_ref[...] + r * (dhn - xn * jnp.mean(dhn * xn, axis=-1, keepdims=True))
        at_end()

    tile = lambda w: pl.BlockSpec((tm, w), lambda i: (i, 0))
    return pl.pallas_call(
        body, name="bwd_in_dx", grid=(n_tiles,),
        out_shape=(jax.ShapeDtypeStruct((n_tok, D_MODEL), F32), jax.ShapeDtypeStruct((1, D_MODEL), F32), *r_shapes),
        in_specs=[tile(IN_WIDTH), tile(D_MODEL), tile(D_MODEL), pl.BlockSpec((1, D_MODEL), lambda i: (0, 0)),
                  pl.BlockSpec((IN_WIDTH, D_MODEL), lambda i: (0, 0), pipeline_mode=pl.Buffered(1))] + r_in_specs,
        out_specs=(tile(D_MODEL), pl.BlockSpec((1, D_MODEL), lambda i: (0, 0)), *r_out_specs),
        scratch_shapes=r_scratch,
        compiler_params=_cparams(1),
    )(dproj, x, dout, g_pre, w_in_t, *r_args)


def _adam_update(w, g, m, v):
    nm = ADAM_B1 * m + (1.0 - ADAM_B1) * g
    nv = ADAM_B2 * v + (1.0 - ADAM_B2) * (g * g)
    m_hat = nm / (1.0 - ADAM_B1 ** ADAM_STEP)
    v_hat = nv / (1.0 - ADAM_B2 ** ADAM_STEP)
    return -ADAM_LR * (m_hat / (jnp.sqrt(v_hat) + ADAM_EPS) + ADAM_WD * w), nm, nv


def _sum_slots(p_ref):
    tot = p_ref[0].astype(F32)
    for d in range(1, N_DEV):
        tot = tot + p_ref[d].astype(F32)
    return tot


def _adamw_parts(parts, w, m, v, layer, results, name):
    n_layers, n_rows, n_cols = w.shape
    tr = n_rows // 2 if n_rows * n_cols > 512 * 1024 else n_rows

    def body(*refs):
        p_ref, w_ref, m_ref, v_ref = refs[:4]
        g_ref, d_ref, nm_ref, nv_ref = refs[-4:]
        g = _sum_slots(p_ref)
        g_ref[0] = g
        d_ref[0], nm_ref[0], nv_ref[0] = _adam_update(w_ref[0], g, m_ref[0], v_ref[0])

    spec = pl.BlockSpec((1, tr, n_cols), lambda i: (layer, i, 0))
    shape = jax.ShapeDtypeStruct((n_layers, n_rows, n_cols), F32)
    kept = [] if results is None else list(results)
    return pl.pallas_call(
        body, name="adamw_" + name, grid=(n_rows // tr,), out_shape=(shape,) * 4,
        in_specs=[pl.BlockSpec((N_DEV, tr, n_cols), lambda i: (0, i, 0)), spec, spec, spec] + [ANY] * len(kept),
        out_specs=(spec,) * 4, input_output_aliases={4 + j: j for j in range(len(kept))},
        compiler_params=_cparams(1),
    )(parts, w, m, v, *kept)


def _adamw(w, g, m, v, name):
    n_rows, n_cols = w.shape
    tr = 512 if n_rows % 512 == 0 else n_rows

    def body(w_ref, g_ref, m_ref, v_ref, d_ref, nm_ref, nv_ref):
        d_ref[...], nm_ref[...], nv_ref[...] = _adam_update(w_ref[...], g_ref[...], m_ref[...], v_ref[...])

    spec = pl.BlockSpec((tr, n_cols), lambda i: (i, 0))
    shape = jax.ShapeDtypeStruct((n_rows, n_cols), F32)
    return pl.pallas_call(
        body, name="adamw_" + name, grid=(n_rows // tr,), out_shape=(shape, shape, shape),
        in_specs=[spec] * 4, out_specs=(spec, spec, spec), compiler_params=_cparams(1),
    )(w, g, m, v)


_SMALL = ("pre_norm_g", "ln_v_g", "ln_v_b", "b_spatial", "sinks", "rel_bias", "post_norm_g")


def _pack(pieces):
    blocks, first_rows, n = [], [], 0
    for p in pieces:
        flat = p.reshape(-1)
        flat = jnp.concatenate([flat, jnp.zeros(((-flat.shape[0]) % (8 * 128),), F32)]).reshape(-1, 128)
        blocks.append(flat)
        first_rows.append(n)
        n += flat.shape[0]
    return jnp.concatenate(blocks, axis=0), first_rows


def _unpack(block, first_row, shape):
    size = math.prod(shape)
    return block[first_row:first_row + -(-size // 128)].reshape(-1)[:size].reshape(shape)


def kernel(x, pre_norm_g, w_in, ln_v_g, ln_v_b, w_spatial, b_spatial, sinks, rel_bias, w_out, post_norm_g, loss_target, m_pre_norm_g, m_w_in, m_ln_v_g, m_ln_v_b, m_w_spatial, m_b_spatial, m_sinks, m_rel_bias, m_w_out, m_post_norm_g, v_pre_norm_g, v_w_in, v_ln_v_g, v_ln_v_b, v_w_spatial, v_b_spatial, v_sinks, v_rel_bias, v_w_out, v_post_norm_g):
    weights = dict(pre_norm_g=pre_norm_g, w_in=w_in, ln_v_g=ln_v_g, ln_v_b=ln_v_b, w_spatial=w_spatial, b_spatial=b_spatial,
                   sinks=sinks, rel_bias=rel_bias, w_out=w_out, post_norm_g=post_norm_g)
    mom_m = dict(pre_norm_g=m_pre_norm_g, w_in=m_w_in, ln_v_g=m_ln_v_g, ln_v_b=m_ln_v_b, w_spatial=m_w_spatial,
                 b_spatial=m_b_spatial, sinks=m_sinks, rel_bias=m_rel_bias, w_out=m_w_out, post_norm_g=m_post_norm_g)
    mom_v = dict(pre_norm_g=v_pre_norm_g, w_in=v_w_in, ln_v_g=v_ln_v_g, ln_v_b=v_ln_v_b, w_spatial=v_w_spatial,
                 b_spatial=v_b_spatial, sinks=v_sinks, rel_bias=v_rel_bias, w_out=v_w_out, post_norm_g=v_post_norm_g)
    n_seq, seq_len, _ = x.shape
    n_layers = w_in.shape[0]
    x2 = x.reshape(n_seq * seq_len, D_MODEL)
    target2 = loss_target.reshape(n_seq * seq_len, D_MODEL)
    row = lambda p, l: p[l][None]

    wt_shards = jnp.swapaxes(w_in, 1, 2).astype(BF16)
    wo_shards = w_out.astype(BF16)
    bucket = _window_buckets()
    bias = _bias_table(rel_bias, bucket)
    wcat, wtcat, bs = _spatial_tables(w_spatial, jnp.swapaxes(b_spatial, 1, 2))

    wt, wo = [None] * n_layers, [None] * n_layers
    wt[0], wo[0] = _gather_weights(wt_shards, wo_shards, 0)
    xs, saved = [x2], []
    for l in range(n_layers):
        layer_args = (xs[-1], row(pre_norm_g, l), wt[l], row(ln_v_g, l), row(ln_v_b, l), wcat[l], bs[l], sinks[l], bias,
                      wo[l], row(post_norm_g, l), seq_len)
        if l + 1 < n_layers:
            *acts, xn, y, wt[l + 1], wo[l + 1] = _fwd_layer(*layer_args, gather=(wt_shards, wo_shards, l + 1))
            xs.append(xn)
        else:
            *acts, dx, y, loss = _fwd_layer(*layer_args, target=target2)
        saved.append((y, *acts))

    small = [None] * n_layers
    parts = [None] * n_layers
    waiting = None
    for l in reversed(range(n_layers)):
        hosted = {} if waiting is None else dict(exchange=waiting[:2], allgather=waiting[2])
        dproj, *rest = _bwd_mix(dx, *saved[l], row(ln_v_g, l), row(ln_v_b, l), wcat[l], wtcat[l], bs[l], wo[l],
                                row(post_norm_g, l), **hosted)
        dwo, dws, dmix, dlng, dlnb, dgpost, dsink, dbias = rest[:8]
        dws = dws.reshape(HEADS * CHUNK, CHUNK).astype(BF16)
        if waiting is not None:
            parts[l + 1] = list(rest[8:])
        if l > 0:
            dx, dwt, dgpre = _bwd_in(dproj, xs[l], dx, row(pre_norm_g, l), wt[l])
            waiting = (dwt, dwo, dws)
        else:
            dwt, po, ws_all = _bwd_in_dw(dproj, xs[l], row(pre_norm_g, l), [((dwo,), True), ((dws,), False)])
            dx, dgpre, pt = _bwd_in_dx(dproj, xs[l], dx, row(pre_norm_g, l), wt[l], [((dwt,), True)])
            parts[l] = [pt, po, ws_all]
        small[l] = dict(pre_norm_g=dgpre[0], ln_v_g=dlng[0], ln_v_b=dlnb[0], dmix=dmix, sinks=dsink, dbias=dbias,
                        post_norm_g=dgpost[0])

    db = _b_spatial_grad(jnp.stack([s["dmix"] for s in small]))
    pieces = {n: jnp.stack([s[n] for s in small]) for n in ("pre_norm_g", "ln_v_g", "ln_v_b", "sinks", "post_norm_g")}
    pieces["b_spatial"] = jnp.swapaxes(db[:, :, :HEADS], 1, 2)
    pieces["rel_bias"] = _rel_bias_grad(jnp.stack([s["dbias"] for s in small]), bucket)
    packed, first_rows = _pack([pieces[n] for n in _SMALL] + [loss])
    total = _allreduce_small(packed)
    grad = {n: _unpack(total, r, weights[n].shape) for n, r in zip(_SMALL, first_rows)}
    loss_out = total[first_rows[-1], 0]

    delta, new_m, new_v = {}, {}, {}
    for n in _SMALL:
        w = weights[n]
        two_d = (-1, w.shape[-1])
        d, nm, nv = _adamw(w.reshape(two_d), grad[n].reshape(two_d), mom_m[n].reshape(two_d), mom_v[n].reshape(two_d), n)
        delta[n], new_m[n], new_v[n] = d.reshape(w.shape), nm.reshape(w.shape), nv.reshape(w.shape)

    t3 = lambda p: jnp.swapaxes(p, 1, 2)
    flat3 = lambda p: p.reshape(n_layers, HEADS * CHUNK, CHUNK)
    for n, which, view, back in (("w_in", 0, t3, t3), ("w_out", 1, lambda p: p, lambda p: p),
                                 ("w_spatial", 2, flat3, lambda p: p.reshape(w_spatial.shape))):
        results = None
        for l in reversed(range(n_layers)):
            results = _adamw_parts(parts[l][which], view(weights[n]), view(mom_m[n]), view(mom_v[n]), l, results, n)
        grad[n], delta[n], new_m[n], new_v[n] = (back(r) for r in results)

    names = tuple(weights)
    return (loss_out, dx.reshape(x.shape), *[grad[n] for n in names], *[delta[n] for n in names],
            *[new_m[n] for n in names], *[new_v[n] for n in names])
```

```python
import math

import jax
import jax.numpy as jnp
from jax import lax
from jax.experimental import pallas as pl
from jax.experimental.pallas import tpu as pltpu

F32 = jnp.float32
BF16 = jnp.bfloat16

D_MODEL = 1024
A_WIDTH = 512
B_WIDTH = 512
KV_WIDTH = 128
IN_WIDTH = 3 * A_WIDTH + 2 * B_WIDTH + 2 * KV_WIDTH
CHUNK = 128
HEADS = 8
HEAD_DIM = 64
REL_BUCKETS = 32
NORM_EPS = 1e-6
NEG = -1e30
Q_OFF = 3 * A_WIDTH
K_OFF = Q_OFF + B_WIDTH
BZ_OFF = K_OFF + 2 * KV_WIDTH
QK_SCALE = HEAD_DIM ** -0.5

ADAM_LR = 0.001
ADAM_B1 = 0.9
ADAM_B2 = 0.999
ADAM_EPS = 1e-08
ADAM_WD = 0.01
ADAM_STEP = 10

TOK_TILE = 512
CHUNK_UNROLL = 4
BWD_CHUNK_UNROLL = 4
VMEM_LIMIT_V7X = 60 * 1024 * 1024

N_DEV = 8
N_CHIPS = 4
MESH_ID = pl.DeviceIdType.MESH
ANY = pl.BlockSpec(memory_space=pl.ANY)
SMEM = pl.BlockSpec(memory_space=pltpu.SMEM)


def _cparams(n_axes):
    return pltpu.CompilerParams(dimension_semantics=("arbitrary",) * n_axes, vmem_limit_bytes=VMEM_LIMIT_V7X)


_GELU_C = math.sqrt(2.0 / math.pi)
_GELU_C3 = _GELU_C * 0.044715


def _gelu(x):
    t = jnp.tanh(x * (_GELU_C3 * (x * x) + _GELU_C))
    return x * (0.5 * t + 0.5)


def _gelu_and_grad(x):
    x2 = x * x
    t = jnp.tanh(x * (_GELU_C3 * x2 + _GELU_C))
    cdf = 0.5 * t + 0.5
    d = cdf + (x * (cdf * (1.0 - cdf))) * ((6.0 * _GELU_C3) * x2 + 2.0 * _GELU_C)
    return x * cdf, d


def _sigmoid(x):
    return 0.5 + 0.5 * jnp.tanh(0.5 * x)


def _silu(x):
    return x * _sigmoid(x)


def _silu_and_grad(x):
    s = _sigmoid(x)
    return x * s, s * (1.0 + x * (1.0 - s))


def _dot(a, b):
    return jnp.dot(a, b, preferred_element_type=F32)


def _dot_nt(a, b):
    return lax.dot_general(a, b, (((1,), (1,)), ((), ())), preferred_element_type=F32)


def _dot_tn(a, b):
    return lax.dot_general(a, b, (((0,), (0,)), ((), ())), preferred_element_type=F32)


def _lo_mask(shape):
    return lax.broadcasted_iota(jnp.int32, shape, 1) < HEAD_DIM


def _swap_halves(v):
    return pltpu.roll(v, HEAD_DIM, 1)


def _window_buckets():
    q_loc = jnp.arange(CHUNK)[:, None]
    j_loc = jnp.arange(CHUNK)[None, :]
    d = q_loc - j_loc + jnp.where(j_loc > q_loc, CHUNK, 0)
    max_exact = REL_BUCKETS // 2
    safe = jnp.maximum(d, 1).astype(F32)
    large = max_exact + (jnp.log(safe / max_exact) / math.log(CHUNK / max_exact)
                         * (REL_BUCKETS - max_exact)).astype(jnp.int32)
    large = jnp.minimum(large, REL_BUCKETS - 1)
    return jnp.where(d < max_exact, d, large).astype(jnp.int32)


def _bias_table(rel_bias, bucket):
    def body(rb_ref, bk_ref, out_ref):
        bk = bk_ref[...]
        for h in range(HEADS):
            acc = jnp.zeros(bk.shape, F32)
            for b in range(REL_BUCKETS):
                acc = jnp.where(bk == b, rb_ref[b, h], acc)
            out_ref[h] = acc

    vmem = pl.BlockSpec(memory_space=pltpu.VMEM)
    return pl.pallas_call(
        body, name="bias_table",
        out_shape=jax.ShapeDtypeStruct((HEADS, CHUNK, CHUNK), F32),
        in_specs=[SMEM, vmem], out_specs=vmem,
    )(rel_bias, bucket)


def _rel_bias_grad(dbias, bucket):
    n_layers = dbias.shape[0]

    def body(db_ref, bk_ref, out_ref, dsink_ref):
        bk = bk_ref[...]
        for h in range(HEADS):
            tot = jnp.zeros((CHUNK, CHUNK), F32)
            for l in range(n_layers):
                ds = db_ref[l, h]
                dsink_ref[l, h] = -jnp.sum(ds)
                tot = tot + ds
            for b in range(REL_BUCKETS):
                out_ref[b, h] = jnp.sum(jnp.where(bk == b, tot, 0.0))

    vmem = pl.BlockSpec(memory_space=pltpu.VMEM)
    return pl.pallas_call(
        body, name="rel_bias_grad",
        out_shape=(jax.ShapeDtypeStruct((REL_BUCKETS, HEADS), F32), jax.ShapeDtypeStruct((n_layers, HEADS), F32)),
        in_specs=[vmem] * 2, out_specs=(SMEM, SMEM),
    )(dbias, bucket)


def _spatial_tables(w_spatial, b_spatial_t):
    n_layers = w_spatial.shape[0]

    def body(w_ref, b_ref, wcat_ref, wtcat_ref, bs_ref):
        row = lax.broadcasted_iota(jnp.int32, (CHUNK, CHUNK), 0)
        col = lax.broadcasted_iota(jnp.int32, (CHUNK, CHUNK), 1)
        causal = col <= row
        lo = _lo_mask((CHUNK, CHUNK))
        for p in range(4):
            for half in range(2):
                w = jnp.where(causal, w_ref[0, 2 * p + half], 0.0)
                wcat_ref[0, p, :, half * CHUNK:(half + 1) * CHUNK] = w.astype(BF16)
                wtcat_ref[0, p, :, half * CHUNK:(half + 1) * CHUNK] = w.T.astype(BF16)
            b = b_ref[0]
            bs_ref[0, p] = jnp.where(lo, b[:, 2 * p:2 * p + 1], b[:, 2 * p + 1:2 * p + 2])

    return pl.pallas_call(
        body, name="spatial_tables", grid=(n_layers,),
        out_shape=(jax.ShapeDtypeStruct((n_layers, 4, CHUNK, 2 * CHUNK), BF16),
                   jax.ShapeDtypeStruct((n_layers, 4, CHUNK, 2 * CHUNK), BF16),
                   jax.ShapeDtypeStruct((n_layers, 4, CHUNK, CHUNK), F32)),
        in_specs=[pl.BlockSpec((1, HEADS, CHUNK, CHUNK), lambda l: (l, 0, 0, 0)),
                  pl.BlockSpec((1, CHUNK, HEADS), lambda l: (l, 0, 0))],
        out_specs=(pl.BlockSpec((1, 4, CHUNK, 2 * CHUNK), lambda l: (l, 0, 0, 0)),
                   pl.BlockSpec((1, 4, CHUNK, 2 * CHUNK), lambda l: (l, 0, 0, 0)),
                   pl.BlockSpec((1, 4, CHUNK, CHUNK), lambda l: (l, 0, 0, 0))),
        compiler_params=_cparams(1),
    )(w_spatial, b_spatial_t)


def _b_spatial_grad(dmix):
    n_layers = dmix.shape[0]

    def body(d_ref, out_ref):
        lane = lax.broadcasted_iota(jnp.int32, (CHUNK, CHUNK), 1)
        acc = jnp.zeros((CHUNK, CHUNK), F32)
        for p in range(4):
            t = d_ref[0, :, p * CHUNK:(p + 1) * CHUNK]
            s_lo = jnp.sum(jnp.where(lane < HEAD_DIM, t, 0.0), axis=1, keepdims=True)
            s_hi = jnp.sum(jnp.where(lane < HEAD_DIM, 0.0, t), axis=1, keepdims=True)
            acc = jnp.where(lane == 2 * p, s_lo, acc)
            acc = jnp.where(lane == 2 * p + 1, s_hi, acc)
        out_ref[0] = acc

    return pl.pallas_call(
        body, name="b_spatial_grad", grid=(n_layers,),
        out_shape=jax.ShapeDtypeStruct((n_layers, CHUNK, CHUNK), F32),
        in_specs=[pl.BlockSpec((1, CHUNK, A_WIDTH), lambda l: (l, 0, 0))],
        out_specs=pl.BlockSpec((1, CHUNK, CHUNK), lambda l: (l, 0, 0)),
        compiler_params=_cparams(1),
    )(dmix)


def _place():
    x, y, c = lax.axis_index("x"), lax.axis_index("y"), lax.axis_index("c")
    other_chips = [(1 - x, y), (x, 1 - y), (1 - x, 1 - y)]
    return x, y, c, other_chips


N_GATHER_SEMS = 12


def _gather_phases(shards, fulls, send_sems, recv_sems, local_sems):
    x, y, c, chips = _place()
    sibling = (x, y, 1 - c)
    n_arr = len(shards)

    def half_rows(a, chip, half):
        n = shards[a].shape[0]
        start = (2 * chip[0] + chip[1]) * n + half * (n // 2)
        return fulls[a].at[pl.ds(pl.multiple_of(start, 16), n // 2), :]

    def my_half(a):
        n = shards[a].shape[0]
        return shards[a].at[pl.ds(pl.multiple_of(c * (n // 2), 16), n // 2), :]

    def copy(k, a, src, chip, half, to):
        return pltpu.make_async_remote_copy(
            src_ref=src, dst_ref=half_rows(a, chip, half), send_sem=send_sems.at[n_arr * k + a],
            recv_sem=recv_sems.at[n_arr * k + a], device_id=to, device_id_type=MESH_ID)

    def local(a):
        n = shards[a].shape[0]
        mine = fulls[a].at[pl.ds(pl.multiple_of((2 * x + y) * n, 16), n), :]
        return pltpu.make_async_copy(shards[a], mine, local_sems.at[a])

    def first(k, a):
        return copy(k, a, my_half(a), (x, y), c, (chips[k][0], chips[k][1], c))

    def passed(k, a):
        return copy(3 + k, a, half_rows(a, chips[k], c), chips[k], c, sibling)

    def phase_a():
        for a in range(n_arr):
            local(a).start()
        for k in range(3):
            for a in range(n_arr):
                first(k, a).start()

    def phase_b():
        for k in range(3):
            for a in range(n_arr):
                copy(k, a, my_half(a), chips[k], c, sibling).wait_recv()
                passed(k, a).start()

    def phase_c():
        for k in range(3):
            for a in range(n_arr):
                copy(3 + k, a, my_half(a), chips[k], 1 - c, sibling).wait_recv()
        for k in range(3):
            for a in range(n_arr):
                first(k, a).wait_send()
                passed(k, a).wait_send()
        for a in range(n_arr):
            local(a).wait()

    return phase_a, phase_b, phase_c


N_EXCHANGE_SEMS = 14


def _exchange_phases(partials, parts, send_sems, recv_sems, local_sems, by_chip=True):
    x, y, c, chips = _place()
    me, sibling = (x, y, c), (x, y, 1 - c)
    n_arr = len(partials)

    def block(a, chip):
        if not by_chip:
            return partials[a]
        n = partials[a].shape[0] // N_CHIPS
        return partials[a].at[pl.ds(pl.multiple_of((2 * chip[0] + chip[1]) * n, 16), n), :]

    def slot(a, dev):
        return parts[a].at[4 * dev[0] + 2 * dev[1] + dev[2]]

    def copy(k, a, src, origin, to):
        return pltpu.make_async_remote_copy(
            src_ref=src, dst_ref=slot(a, origin), send_sem=send_sems.at[n_arr * k + a],
            recv_sem=recv_sems.at[n_arr * k + a], device_id=to, device_id_type=MESH_ID)

    def local(a):
        return pltpu.make_async_copy(block(a, (x, y)), slot(a, me), local_sems.at[a])

    def first(k, a):
        if k == 0:
            return copy(0, a, block(a, (x, y)), me, sibling)
        chip = chips[k - 1]
        return copy(k, a, block(a, chip), me, (chip[0], chip[1], c))

    def passed(k, a):
        origin = (chips[k][0], chips[k][1], c)
        return copy(4 + k, a, slot(a, origin), origin, sibling)

    def phase_a():
        for a in range(n_arr):
            local(a).start()
        for k in range(4):
            for a in range(n_arr):
                first(k, a).start()

    def phase_b():
        for k in range(3):
            for a in range(n_arr):
                copy(1 + k, a, block(a, (x, y)), (chips[k][0], chips[k][1], c), me).wait_recv()
                passed(k, a).start()

    def phase_c():
        for a in range(n_arr):
            copy(0, a, block(a, (x, y)), sibling, me).wait_recv()
        for k in range(3):
            for a in range(n_arr):
                copy(4 + k, a, block(a, (x, y)), (chips[k][0], chips[k][1], 1 - c), me).wait_recv()
        for k in range(4):
            for a in range(n_arr):
                first(k, a).wait_send()
        for k in range(3):
            for a in range(n_arr):
                passed(k, a).wait_send()
        for a in range(n_arr):
            local(a).wait()

    return phase_a, phase_b, phase_c


def _comm_scratch(n_sems):
    return [pltpu.SemaphoreType.DMA((n_sems,)), pltpu.SemaphoreType.DMA((n_sems,)), pltpu.SemaphoreType.DMA((2,))]


def _gather_weights(wt_shards, wo_shards, layer):
    wt_rows, wo_rows = wt_shards.shape[1], wo_shards.shape[1]

    def body(wt_ref, wo_ref, wt_full, wo_full, send_sems, recv_sems, local_sems):
        phases = _gather_phases((wt_ref.at[layer], wo_ref.at[layer]), (wt_full, wo_full), send_sems, recv_sems, local_sems)
        for phase in phases:
            phase()

    return pl.pallas_call(
        body, name="gather_weights",
        out_shape=(jax.ShapeDtypeStruct((N_CHIPS * wt_rows, D_MODEL), BF16),
                   jax.ShapeDtypeStruct((N_CHIPS * wo_rows, D_MODEL), BF16)),
        in_specs=[ANY, ANY], out_specs=(ANY, ANY), scratch_shapes=_comm_scratch(N_GATHER_SEMS),
    )(wt_shards, wo_shards)


def _allreduce_small(part):
    n_rows = part.shape[0]

    def body(p_ref, tot_ref, all_ref, send_sems, recv_sems, local_sem):
        x, y, c, chips = _place()
        me, sibling = (x, y, c), (x, y, 1 - c)

        def rows(dev):
            return all_ref.at[pl.ds(pl.multiple_of((4 * dev[0] + 2 * dev[1] + dev[2]) * n_rows, 8), n_rows), :]

        def copy(k, origin, to, src=None):
            return pltpu.make_async_remote_copy(
                src_ref=rows(origin) if src is None else src, dst_ref=rows(origin), send_sem=send_sems.at[k],
                recv_sem=recv_sems.at[k], device_id=to, device_id_type=MESH_ID)

        mine = pltpu.make_async_copy(p_ref, rows(me), local_sem)
        mine.start()
        first = [copy(0, me, sibling, src=p_ref)]
        first += [copy(1 + k, me, (chip[0], chip[1], c), src=p_ref) for k, chip in enumerate(chips)]
        for cp in first:
            cp.start()
        passed = []
        for k, chip in enumerate(chips):
            origin = (chip[0], chip[1], c)
            copy(1 + k, origin, me).wait_recv()
            fwd = copy(4 + k, origin, sibling)
            fwd.start()
            passed.append(fwd)
        copy(0, sibling, me).wait_recv()
        for k, chip in enumerate(chips):
            copy(4 + k, (chip[0], chip[1], 1 - c), me).wait_recv()
        for cp in first + passed:
            cp.wait_send()
        mine.wait()
        tot = all_ref[0:n_rows, :]
        for d in range(1, N_DEV):
            tot = tot + all_ref[d * n_rows:(d + 1) * n_rows, :]
        tot_ref[...] = tot

    vmem = pl.BlockSpec(memory_space=pltpu.VMEM)
    return pl.pallas_call(
        body, name="allreduce_small",
        out_shape=jax.ShapeDtypeStruct((n_rows, 128), F32),
        in_specs=[vmem], out_specs=vmem,
        scratch_shapes=[pltpu.VMEM((N_DEV * n_rows, 128), F32), pltpu.SemaphoreType.DMA((7,)),
                        pltpu.SemaphoreType.DMA((7,)), pltpu.SemaphoreType.DMA],
        compiler_params=pltpu.CompilerParams(vmem_limit_bytes=VMEM_LIMIT_V7X),
    )(part)


def _hosted(phases, step, n_steps, pass_on_step=None):
    phase_a, phase_b, phase_c = phases
    if pass_on_step is None:
        pass_on_step = (3 * n_steps) // 4

    def at_start():
        pl.when(step == 0)(phase_a)

    def at_end():
        pl.when(step == pass_on_step)(phase_b)
        pl.when(step == n_steps - 1)(phase_c)

    return at_start, at_end


def _layer_norm_stats(vv):
    mu = jnp.mean(vv, axis=-1, keepdims=True)
    xc = vv - mu
    rs = lax.rsqrt(jnp.mean(xc * xc, axis=-1, keepdims=True) + NORM_EPS)
    return xc * rs, rs


def _blockdiag(v, lo):
    zero = jnp.zeros_like(v)
    return jnp.concatenate([jnp.where(lo, v, zero), jnp.where(lo, zero, v)], axis=0)


def _softmax_sink(s, sink):
    m = jnp.maximum(jnp.max(s, axis=-1, keepdims=True), sink)
    e = jnp.exp(s - m)
    esink = jnp.exp(sink - m)
    den = jnp.sum(e, axis=-1, keepdims=True) + esink
    return e / den


def _kv_rows(cur_ref, halo_ref, r0, c):
    prev_in_tile = cur_ref[pl.ds(pl.multiple_of(jnp.maximum(r0 - CHUNK, 0), CHUNK), CHUNK), :]
    prev = jnp.where(c == 0, halo_ref[...], prev_in_tile)
    kv2 = jnp.concatenate([prev, cur_ref[pl.ds(r0, CHUNK), :]], axis=0)
    k2, v2 = kv2[:, 0:KV_WIDTH], kv2[:, KV_WIDTH:2 * KV_WIDTH]
    return (k2, _swap_halves(k2)), (v2, _swap_halves(v2))


def _window_square(over_keys, prev):
    return jnp.where(prev, over_keys[:, 0:CHUNK], over_keys[:, CHUNK:2 * CHUNK])


def _window_keys(square, prev):
    zero = jnp.zeros_like(square)
    return jnp.concatenate([jnp.where(prev, square, zero), jnp.where(prev, zero, square)], axis=1)


def _ahead(shape=(CHUNK, CHUNK)):
    return lax.broadcasted_iota(jnp.int32, shape, 1) - lax.broadcasted_iota(jnp.int32, shape, 0)


def _dead_mask(first, ahead):
    return ahead > jnp.where(first, 0, CHUNK)


def _head_of(p, half):
    return 2 * p + half, int(half != p // 2)


_HEADS_BY_COPY = tuple(tuple((p, half) for p in range(4) for half in range(2) if _head_of(p, half)[1] == sw)
                       for sw in range(2))


def _masked_halves(tile, lo):
    zero = jnp.zeros_like(tile)
    return {0: jnp.where(lo, tile, zero), 1: jnp.where(lo, zero, tile)}


def _query_stacks(q_tiles, lo):
    qm = {p: _masked_halves(q_tiles[p] * QK_SCALE, lo) for p in range(4)}
    return [jnp.concatenate([qm[p][half] for p, half in members], axis=0) for members in _HEADS_BY_COPY]


def _attention_probs(q_stacks, ks, bias_ref, sink_ref, dead, prev):
    probs = {}
    for sw, members in enumerate(_HEADS_BY_COPY):
        s_stack = _dot_nt(q_stacks[sw], ks[sw])
        for i, (p, half) in enumerate(members):
            head = 2 * p + half
            s = _window_square(s_stack[i * CHUNK:(i + 1) * CHUNK, :], prev) + bias_ref[head]
            s = jnp.where(dead, NEG, s)
            probs[p, half] = _softmax_sink(s, sink_ref[head]).astype(BF16)
    return probs


def _attention_values(probs, vs, prev):
    outs, p_stacks = {}, []
    for sw, members in enumerate(_HEADS_BY_COPY):
        p_stack = jnp.concatenate([_window_keys(probs[m], prev) for m in members], axis=0)
        r_stack = _dot(p_stack, vs[sw])
        p_stacks.append(p_stack)
        for i, m in enumerate(members):
            outs[m] = r_stack[i * CHUNK:(i + 1) * CHUNK, :]
    return outs, p_stacks


def _fwd_layer(x, g_pre, w_in_t, ln_g, ln_b, wcat, bs, sinks, bias, w_out, g_post, seq_len, gather=None, target=None):
    n_tok = x.shape[0]
    tb = TOK_TILE
    n_chunks = tb // CHUNK
    n_tiles = n_tok // tb
    n_in = 11
    assert gather is None or target is None

    def body(*refs):
        (x_ref, gpre_ref, wt_ref, lng_ref, lnb_ref, wcat_ref, bs_ref, sink_ref, bias_ref, wout_ref,
         gpost_ref) = refs[:n_in]
        at_start = at_end = lambda: None
        if gather is not None:
            (wts_ref, wos_ref, a_ref, q_ref, kv_ref, bz_ref, probs_ref, xn_ref, y_ref, wt_full, wo_full, ycat_s, halo_ref,
             send_sems, recv_sems, local_sems) = refs[n_in:]
            phases = _gather_phases((wts_ref.at[gather[2]], wos_ref.at[gather[2]]), (wt_full, wo_full), send_sems,
                                    recv_sems, local_sems)
            at_start, at_end = _hosted(phases, pl.program_id(0), n_tiles)
        elif target is not None:
            target_ref, a_ref, q_ref, kv_ref, bz_ref, probs_ref, xn_ref, y_ref, loss_ref, ycat_s, halo_ref = refs[n_in:]
        else:
            a_ref, q_ref, kv_ref, bz_ref, probs_ref, xn_ref, y_ref, ycat_s, halo_ref = refs[n_in:]
        at_start()
        i = pl.program_id(0)
        lo = _lo_mask((CHUNK, CHUNK))
        ahead = _ahead()
        prev = ahead > 0

        @pl.when(i == 0)
        def _():
            halo_ref[...] = jnp.zeros_like(halo_ref)

        xf = x_ref[...]
        r1 = lax.rsqrt(jnp.mean(xf * xf, axis=-1, keepdims=True) + NORM_EPS)
        h = ((xf * r1) * gpre_ref[...]).astype(BF16)
        a_ref[...] = _dot_nt(h, wt_ref[0:Q_OFF, :])
        q_ref[...] = _dot_nt(h, wt_ref[Q_OFF:K_OFF, :]).astype(BF16)
        kv_ref[...] = _dot_nt(h, wt_ref[K_OFF:BZ_OFF, :]).astype(BF16)
        bz_ref[...] = _dot_nt(h, wt_ref[BZ_OFF:IN_WIDTH, :])

        def chunk(c, carry):
            r0 = pl.multiple_of(c * CHUNK, CHUNK)
            rows = pl.ds(r0, CHUNK)
            u = _gelu(a_ref[rows, 0:A_WIDTH])
            vv = _gelu(a_ref[rows, A_WIDTH:2 * A_WIDTH])
            xhat, _ = _layer_norm_stats(vv)
            vnb = (xhat * lng_ref[...] + lnb_ref[...]).astype(BF16)
            for p in range(4):
                blk = slice(p * CHUNK, (p + 1) * CHUNK)
                mixed = _dot(wcat_ref[p], _blockdiag(vnb[:, blk], lo)) + bs_ref[p]
                sz = _silu(a_ref[rows, 2 * A_WIDTH + p * CHUNK:2 * A_WIDTH + (p + 1) * CHUNK])
                ycat_s[rows, blk] = ((u[:, blk] * mixed) * sz).astype(BF16)
            ks, vs = _kv_rows(kv_ref, halo_ref, r0, c)
            dead = _dead_mask(lax.rem(i * tb + r0, seq_len) == 0, ahead)
            q_tiles = [q_ref[rows, p * CHUNK:(p + 1) * CHUNK] for p in range(4)]
            probs = _attention_probs(_query_stacks(q_tiles, lo), ks, bias_ref, sink_ref, dead, prev)
            for (p, half), prob in probs.items():
                head = 2 * p + half
                probs_ref[rows, head * CHUNK:(head + 1) * CHUNK] = prob
            outs, _ = _attention_values(probs, vs, prev)
            for p in range(4):
                blk = slice(p * CHUNK, (p + 1) * CHUNK)
                o = jnp.where(lo, outs[p, 0], outs[p, 1])
                ycat_s[rows, B_WIDTH + p * CHUNK:B_WIDTH + (p + 1) * CHUNK] = (o * _silu(bz_ref[rows, blk])).astype(BF16)
            return carry

        lax.fori_loop(0, n_chunks, chunk, 0, unroll=CHUNK_UNROLL)
        halo_ref[...] = kv_ref[tb - CHUNK:tb, :]
        y = _dot(ycat_s[...], wout_ref[...])
        r = lax.rsqrt(jnp.mean(y * y, axis=-1, keepdims=True) + NORM_EPS)
        y_ref[...] = y
        xn = x_ref[...] + (y * r) * gpost_ref[...]
        if target is None:
            xn_ref[...] = xn
        else:
            d = xn - target_ref[...]
            xn_ref[...] = d * (1.0 / D_MODEL)

            @pl.when(i == 0)
            def _():
                loss_ref[0, 0] = 0.0

            loss_ref[0, 0] += 0.5 * jnp.sum(jnp.mean(d * d, axis=-1, keepdims=True))
        at_end()

    tile = lambda w: pl.BlockSpec((tb, w), lambda i: (i, 0))
    whole = lambda shape, **kw: pl.BlockSpec(shape, lambda i: (0,) * len(shape), **kw)
    in_specs = [tile(D_MODEL), whole((1, D_MODEL)), whole((IN_WIDTH, D_MODEL), pipeline_mode=pl.Buffered(1)),
                whole((1, A_WIDTH)), whole((1, A_WIDTH)), whole((4, CHUNK, 2 * CHUNK)), whole((4, CHUNK, CHUNK)), SMEM,
                whole((HEADS, CHUNK, CHUNK)), whole((D_MODEL, D_MODEL)), whole((1, D_MODEL))]
    out_shape = [jax.ShapeDtypeStruct((n_tok, Q_OFF), F32), jax.ShapeDtypeStruct((n_tok, B_WIDTH), BF16),
                 jax.ShapeDtypeStruct((n_tok, 2 * KV_WIDTH), BF16), jax.ShapeDtypeStruct((n_tok, B_WIDTH), F32),
                 jax.ShapeDtypeStruct((n_tok, HEADS * CHUNK), BF16), jax.ShapeDtypeStruct((n_tok, D_MODEL), F32),
                 jax.ShapeDtypeStruct((n_tok, D_MODEL), F32)]
    out_specs = [tile(Q_OFF), tile(B_WIDTH), tile(2 * KV_WIDTH), tile(B_WIDTH), tile(HEADS * CHUNK), tile(D_MODEL),
                 tile(D_MODEL)]
    scratch = [pltpu.VMEM((tb, D_MODEL), BF16), pltpu.VMEM((CHUNK, 2 * KV_WIDTH), BF16)]
    args = [x, g_pre, w_in_t, ln_g, ln_b, wcat, bs, sinks, bias, w_out, g_post]
    if gather is not None:
        wt_shards, wo_shards, _ = gather
        in_specs += [ANY, ANY]
        args += [wt_shards, wo_shards]
        out_shape += [jax.ShapeDtypeStruct((N_CHIPS * wt_shards.shape[1], D_MODEL), BF16),
                      jax.ShapeDtypeStruct((N_CHIPS * wo_shards.shape[1], D_MODEL), BF16)]
        out_specs += [ANY, ANY]
        scratch += _comm_scratch(N_GATHER_SEMS)
    if target is not None:
        in_specs.append(tile(D_MODEL))
        args.append(target)
        out_shape.append(jax.ShapeDtypeStruct((1, 1), F32))
        out_specs.append(SMEM)
    name = "fwd_layer" + ("" if gather is None else "_gather") + ("" if target is None else "_loss")
    return pl.pallas_call(
        body, name=name, grid=(n_tiles,),
        out_shape=tuple(out_shape), in_specs=in_specs, out_specs=tuple(out_specs), scratch_shapes=scratch,
        compiler_params=_cparams(1),
    )(*args)


def _bwd_mix(dout, y, a, q, kv, bz, probs, ln_g, ln_b, wcat, wtcat, bs, w_out, g_post, exchange=None, allgather=None):
    n_tok = y.shape[0]
    tb = TOK_TILE
    n_chunks = tb // CHUNK
    n_tiles = n_tok // tb

    def body(*refs):
        refs = list(refs)
        take = lambda n: [refs.pop(0) for _ in range(n)]
        (dout_ref, y_ref, a_ref, q_ref, kv_ref, halo_ref, bz_ref, probs_ref, lng_ref, lnb_ref, wcat_ref, wtcat_ref, bs_ref,
         wout_ref, gpost_ref) = take(15)
        if exchange is not None:
            dwt_in, dwo_in = take(2)
        if allgather is not None:
            (small_in,) = take(1)
        dproj_ref, dwout_ref, dws_ref, dmix_ref, dlng_ref, dlnb_ref, dgpost_ref, dbias_ref = take(8)
        if exchange is not None:
            pt_ref, po_ref = take(2)
        if allgather is not None:
            (small_all,) = take(1)
        ycat_s, dy_s, dyc_s, dkv_s, carry_s, dwout_s, stage_s = take(7)
        starts, ends = [], []
        if exchange is not None:
            hosted = _hosted(_exchange_phases((dwt_in, dwo_in), (pt_ref, po_ref), *take(3)), pl.program_id(0), n_tiles)
            starts.append(hosted[0])
            ends.append(hosted[1])
        if allgather is not None:
            hosted = _hosted(_exchange_phases((small_in,), (small_all,), *take(3), by_chip=False), pl.program_id(0),
                             n_tiles)
            starts.append(hosted[0])
            ends.append(hosted[1])
        at_start = lambda: [f() for f in starts]
        at_end = lambda: [f() for f in ends]
        at_start()
        step = pl.program_id(0)
        lo = _lo_mask((CHUNK, CHUNK))
        prev = _ahead() > 0

        @pl.when(step == 0)
        def _():
            dwout_s[...] = jnp.zeros_like(dwout_s)
            dws_ref[...] = jnp.zeros_like(dws_ref)
            dmix_ref[...] = jnp.zeros_like(dmix_ref)
            dlng_ref[...] = jnp.zeros_like(dlng_ref)
            dlnb_ref[...] = jnp.zeros_like(dlnb_ref)
            dgpost_ref[...] = jnp.zeros_like(dgpost_ref)
            dbias_ref[...] = jnp.zeros_like(dbias_ref)
            carry_s[...] = jnp.zeros_like(carry_s)

        for r0 in range(0, tb, CHUNK):
            rows = slice(r0, r0 + CHUNK)
            yv = y_ref[rows, :]
            dout = dout_ref[rows, :]
            r = lax.rsqrt(jnp.mean(yv * yv, axis=-1, keepdims=True) + NORM_EPS)
            yn = yv * r
            dgpost_ref[...] += jnp.sum(dout * yn, axis=0, keepdims=True)
            dyn = dout * gpost_ref[...]
            dy_s[rows, :] = (r * (dyn - yn * jnp.mean(dyn * yn, axis=-1, keepdims=True))).astype(BF16)
        dyc_s[...] = _dot_nt(dy_s[...], wout_ref[...])
        dkv_s[0:tb, :] = jnp.zeros((tb, 2 * KV_WIDTH), F32)
        dkv_s[tb:tb + CHUNK, :] = carry_s[...]

        def chunk(c, carry):
            r0 = pl.multiple_of(c * CHUNK, CHUNK)
            rows = pl.ds(r0, CHUNK)
            u, gu = _gelu_and_grad(a_ref[rows, 0:A_WIDTH])
            vv, gv = _gelu_and_grad(a_ref[rows, A_WIDTH:2 * A_WIDTH])
            xhat, rs = _layer_norm_stats(vv)
            vnb = (xhat * lng_ref[...] + lnb_ref[...]).astype(BF16)
            d_vn, d_u, d_az = [], [], []
            for p in range(4):
                blk = slice(p * CHUNK, (p + 1) * CHUNK)
                vbd = _blockdiag(vnb[:, blk], lo)
                mixed = _dot(wcat_ref[p], vbd) + bs_ref[p]
                sz, gz = _silu_and_grad(a_ref[rows, 2 * A_WIDTH + p * CHUNK:2 * A_WIDTH + (p + 1) * CHUNK])
                ub = u[:, blk]
                dya = dyc_s[rows, blk]
                um = ub * mixed
                ycat_s[rows, blk] = (um * sz).astype(BF16)
                d_mixed = (dya * ub) * sz
                d_u.append((dya * mixed) * sz)
                d_az.append((dya * um) * gz)
                dmix_ref[:, blk] += d_mixed
                dmbd = _blockdiag(d_mixed.astype(BF16), lo)
                d_vn.append(_dot(wtcat_ref[p], dmbd))
                dws_ref[p] += _dot_nt(dmbd, vnb[:, blk])
            d_vn = jnp.concatenate(d_vn, axis=1)
            dlng_ref[...] += jnp.sum(d_vn * xhat, axis=0, keepdims=True)
            dlnb_ref[...] += jnp.sum(d_vn, axis=0, keepdims=True)
            dxh = d_vn * lng_ref[...]
            d_vv = rs * (dxh - jnp.mean(dxh, axis=-1, keepdims=True)
                         - xhat * jnp.mean(dxh * xhat, axis=-1, keepdims=True))
            dproj_ref[rows, 0:A_WIDTH] = (jnp.concatenate(d_u, axis=1) * gu).astype(BF16)
            dproj_ref[rows, A_WIDTH:2 * A_WIDTH] = (d_vv * gv).astype(BF16)
            dproj_ref[rows, 2 * A_WIDTH:Q_OFF] = jnp.concatenate(d_az, axis=1).astype(BF16)
            ks, vs = _kv_rows(kv_ref, halo_ref, r0, c)
            q_stacks = _query_stacks([q_ref[rows, p * CHUNK:(p + 1) * CHUNK] for p in range(4)], lo)
            probs = {(p, half): probs_ref[rows, (2 * p + half) * CHUNK:(2 * p + half + 1) * CHUNK]
                     for p in range(4) for half in range(2)}
            outs_h, p_stacks = _attention_values(probs, vs, prev)
            dom = {}
            for p in range(4):
                blk = slice(p * CHUNK, (p + 1) * CHUNK)
                sb, gb = _silu_and_grad(bz_ref[rows, blk])
                dyb = dyc_s[rows, B_WIDTH + p * CHUNK:B_WIDTH + (p + 1) * CHUNK]
                o = jnp.where(lo, outs_h[p, 0], outs_h[p, 1])
                ycat_s[rows, B_WIDTH + p * CHUNK:B_WIDTH + (p + 1) * CHUNK] = (o * sb).astype(BF16)
                dproj_ref[rows, BZ_OFF + p * CHUNK:BZ_OFF + (p + 1) * CHUNK] = ((dyb * o) * gb).astype(BF16)
                dom[p] = _masked_halves((dyb * sb).astype(BF16), lo)
            dqs, dk_by_copy, dv_by_copy = {}, [], []
            for sw, members in enumerate(_HEADS_BY_COPY):
                do_stack = jnp.concatenate([dom[p][half] for p, half in members], axis=0)
                dp_stack = _dot_nt(do_stack, vs[sw])
                dsws = []
                for i, (p, half) in enumerate(members):
                    head = 2 * p + half
                    dp = _window_square(dp_stack[i * CHUNK:(i + 1) * CHUNK, :], prev)
                    prob = probs[p, half].astype(F32)
                    delta = jnp.sum(prob * dp, axis=-1, keepdims=True)
                    ds = prob * (dp - delta)
                    dbias_ref[head] += ds
                    dsws.append(_window_keys(ds.astype(BF16), prev))
                ds_stack = jnp.concatenate(dsws, axis=0)
                dq_stack = _dot(ds_stack, ks[sw])
                for i, m in enumerate(members):
                    dqs[m] = dq_stack[i * CHUNK:(i + 1) * CHUNK, :]
                dk_by_copy.append(_dot_tn(ds_stack, q_stacks[sw]))
                dv_by_copy.append(_dot_tn(p_stacks[sw], do_stack))
            for p in range(4):
                dq = jnp.where(lo, dqs[p, 0], dqs[p, 1]) * QK_SCALE
                dproj_ref[rows, Q_OFF + p * CHUNK:Q_OFF + (p + 1) * CHUNK] = dq.astype(BF16)
            both = pl.ds(r0, 2 * CHUNK)
            dkv_s[both, 0:KV_WIDTH] += dk_by_copy[0] + _swap_halves(dk_by_copy[1])
            dkv_s[both, KV_WIDTH:2 * KV_WIDTH] += dv_by_copy[0] + _swap_halves(dv_by_copy[1])
            return carry

        lax.fori_loop(0, n_chunks, chunk, 0, unroll=BWD_CHUNK_UNROLL)
        dwout_s[...] += _dot_tn(ycat_s[...], dy_s[...])
        dproj_ref[:, K_OFF:BZ_OFF] = dkv_s[CHUNK:CHUNK + tb, :].astype(BF16)
        carry_s[...] = dkv_s[0:CHUNK, :]

        @pl.when(step == n_tiles - 1)
        def _():
            row = lax.broadcasted_iota(jnp.int32, (2 * CHUNK, CHUNK), 0)
            col = lax.broadcasted_iota(jnp.int32, (2 * CHUNK, CHUNK), 1)
            causal = col <= jnp.where(row >= CHUNK, row - CHUNK, row)
            for p in range(4):
                dws_ref[p] = jnp.where(causal, dws_ref[p], 0.0)
            rows = stage_s.shape[0]
            for r0 in range(0, D_MODEL, rows):
                stage_s[...] = dwout_s[r0:r0 + rows, :].astype(BF16)
                pltpu.sync_copy(stage_s, dwout_ref.at[r0:r0 + rows, :])

        at_end()

    tile = lambda w: pl.BlockSpec((tb, w), lambda s: (n_tiles - 1 - s, 0))
    whole = lambda shape, **kw: pl.BlockSpec(shape, lambda s: (0,) * len(shape), **kw)
    once = dict(pipeline_mode=pl.Buffered(1))
    in_specs = [tile(D_MODEL), tile(D_MODEL), tile(Q_OFF), tile(B_WIDTH), tile(2 * KV_WIDTH),
                pl.BlockSpec((CHUNK, 2 * KV_WIDTH), lambda s: (jnp.maximum((n_tiles - 1 - s) * n_chunks - 1, 0), 0)),
                tile(B_WIDTH), tile(HEADS * CHUNK), whole((1, A_WIDTH)), whole((1, A_WIDTH)), whole((4, CHUNK, 2 * CHUNK)),
                whole((4, CHUNK, 2 * CHUNK)), whole((4, CHUNK, CHUNK)), whole((D_MODEL, D_MODEL), **once),
                whole((1, D_MODEL))]
    args = [dout, y, a, q, kv, kv, bz, probs, ln_g, ln_b, wcat, wtcat, bs, w_out, g_post]
    small_shapes = [jax.ShapeDtypeStruct((D_MODEL, D_MODEL), BF16), jax.ShapeDtypeStruct((4, 2 * CHUNK, CHUNK), F32),
                    jax.ShapeDtypeStruct((CHUNK, A_WIDTH), F32), jax.ShapeDtypeStruct((1, A_WIDTH), F32),
                    jax.ShapeDtypeStruct((1, A_WIDTH), F32), jax.ShapeDtypeStruct((1, D_MODEL), F32),
                    jax.ShapeDtypeStruct((HEADS, CHUNK, CHUNK), F32)]
    small_specs = [ANY, whole((4, 2 * CHUNK, CHUNK)), whole((CHUNK, A_WIDTH)), whole((1, A_WIDTH)), whole((1, A_WIDTH)),
                   whole((1, D_MODEL)), whole((HEADS, CHUNK, CHUNK))]
    scratch = [pltpu.VMEM((tb, D_MODEL), BF16), pltpu.VMEM((tb, D_MODEL), BF16), pltpu.VMEM((tb, D_MODEL), F32),
               pltpu.VMEM((tb + CHUNK, 2 * KV_WIDTH), F32), pltpu.VMEM((CHUNK, 2 * KV_WIDTH), F32),
               pltpu.VMEM((D_MODEL, D_MODEL), F32), pltpu.VMEM((2 * CHUNK, D_MODEL), BF16)]
    out_shape = [jax.ShapeDtypeStruct((n_tok, IN_WIDTH), BF16)] + small_shapes
    out_specs = [tile(IN_WIDTH)] + small_specs
    if exchange is not None:
        dwt, dwo = exchange
        in_specs += [ANY, ANY]
        args += [dwt, dwo]
        out_shape += [jax.ShapeDtypeStruct((N_DEV, dwt.shape[0] // N_CHIPS, D_MODEL), BF16),
                      jax.ShapeDtypeStruct((N_DEV, dwo.shape[0] // N_CHIPS, D_MODEL), BF16)]
        out_specs += [ANY, ANY]
    if allgather is not None:
        in_specs.append(ANY)
        args.append(allgather)
        out_shape.append(jax.ShapeDtypeStruct((N_DEV,) + allgather.shape, allgather.dtype))
        out_specs.append(ANY)
    for hosted in (exchange, allgather):
        if hosted is not None:
            scratch += _comm_scratch(N_EXCHANGE_SEMS)
    return pl.pallas_call(
        body, name="bwd_mix" + ("" if exchange is None and allgather is None else "_exchange"), grid=(n_tiles,),
        out_shape=tuple(out_shape), in_specs=in_specs, out_specs=tuple(out_specs), scratch_shapes=scratch,
        compiler_params=_cparams(1),
    )(*args)


def _bwd_in(dproj, x, dout, g_pre, w_in_t):
    n_tok = x.shape[0]
    tm = TOK_TILE
    n_tiles = n_tok // tm

    def body(dp_ref, x_ref, dout_ref, g_ref, wt_ref, dx_ref, dwt_ref, dg_ref, acc_s):
        i = pl.program_id(0)

        @pl.when(i == 0)
        def _():
            acc_s[...] = jnp.zeros_like(acc_s)
            dg_ref[...] = jnp.zeros_like(dg_ref)

        xf = x_ref[...]
        r = lax.rsqrt(jnp.mean(xf * xf, axis=-1, keepdims=True) + NORM_EPS)
        xn = xf * r
        h = (xn * g_ref[...]).astype(BF16)
        dp = dp_ref[...]
        dh = _dot(dp, wt_ref[...])
        acc_s[...] += _dot_tn(dp, h)
        dg_ref[...] += jnp.sum(dh * xn, axis=0, keepdims=True)
        dhn = dh * g_ref[...]
        dx_ref[...] = dout_ref[...] + r * (dhn - xn * jnp.mean(dhn * xn, axis=-1, keepdims=True))

        @pl.when(i == n_tiles - 1)
        def _():
            dwt_ref[...] = acc_s[...].astype(BF16)

    return pl.pallas_call(
        body, name="bwd_in", grid=(n_tiles,),
        out_shape=(jax.ShapeDtypeStruct((n_tok, D_MODEL), F32), jax.ShapeDtypeStruct((IN_WIDTH, D_MODEL), BF16),
                   jax.ShapeDtypeStruct((1, D_MODEL), F32)),
        in_specs=[pl.BlockSpec((tm, IN_WIDTH), lambda i: (i, 0)), pl.BlockSpec((tm, D_MODEL), lambda i: (i, 0)),
                  pl.BlockSpec((tm, D_MODEL), lambda i: (i, 0)), pl.BlockSpec((1, D_MODEL), lambda i: (0, 0)),
                  pl.BlockSpec((IN_WIDTH, D_MODEL), lambda i: (0, 0), pipeline_mode=pl.Buffered(1))],
        out_specs=(pl.BlockSpec((tm, D_MODEL), lambda i: (i, 0)),
                   pl.BlockSpec((IN_WIDTH, D_MODEL), lambda i: (0, 0), pipeline_mode=pl.Buffered(1)),
                   pl.BlockSpec((1, D_MODEL), lambda i: (0, 0))),
        scratch_shapes=[pltpu.VMEM((IN_WIDTH, D_MODEL), F32)],
        compiler_params=_cparams(1),
    )(dproj, x, dout, g_pre, w_in_t)


def _riders(exchanges):
    args = [a for arrays, _ in exchanges for a in arrays]
    shapes = [jax.ShapeDtypeStruct((N_DEV, a.shape[0] // N_CHIPS if by_chip else a.shape[0]) + a.shape[1:], a.dtype)
              for arrays, by_chip in exchanges for a in arrays]
    scratch = [s for _ in exchanges for s in _comm_scratch(N_EXCHANGE_SEMS)]

    def bind(in_refs, out_refs, sem_refs, step, n_steps, pass_on_step=None):
        in_refs, out_refs, sem_refs = list(in_refs), list(out_refs), list(sem_refs)
        starts, ends = [], []
        for arrays, by_chip in exchanges:
            n = len(arrays)
            phases = _exchange_phases(tuple(in_refs[:n]), tuple(out_refs[:n]), *sem_refs[:3], by_chip=by_chip)
            del in_refs[:n], out_refs[:n], sem_refs[:3]
            at_start, at_end = _hosted(phases, step, n_steps, pass_on_step)
            starts.append(at_start)
            ends.append(at_end)
        return (lambda: [f() for f in starts]), (lambda: [f() for f in ends])

    return args, [ANY] * len(args), shapes, [ANY] * len(shapes), scratch, bind


def _bwd_in_dw(dproj, x, g_pre, exchanges):
    n_tok = x.shape[0]
    tm = TOK_TILE
    n_tiles = n_tok // tm
    r_args, r_in_specs, r_shapes, r_out_specs, r_scratch, bind = _riders(exchanges)

    def body(*refs):
        dp_ref, x_ref, g_ref = refs[:3]
        in_refs = refs[3:3 + len(r_args)]
        dwt_ref = refs[3 + len(r_args)]
        out_refs = refs[4 + len(r_args):4 + 2 * len(r_args)]
        acc_s = refs[4 + 2 * len(r_args)]
        i = pl.program_id(0)
        at_start, at_end = bind(in_refs, out_refs, refs[5 + 2 * len(r_args):], i, n_tiles)
        at_start()

        @pl.when(i == 0)
        def _():
            acc_s[...] = jnp.zeros_like(acc_s)

        xf = x_ref[...]
        r = lax.rsqrt(jnp.mean(xf * xf, axis=-1, keepdims=True) + NORM_EPS)
        acc_s[...] += _dot_tn(dp_ref[...], ((xf * r) * g_ref[...]).astype(BF16))

        @pl.when(i == n_tiles - 1)
        def _():
            dwt_ref[...] = acc_s[...].astype(BF16)

        at_end()

    return pl.pallas_call(
        body, name="bwd_in_dw", grid=(n_tiles,),
        out_shape=(jax.ShapeDtypeStruct((IN_WIDTH, D_MODEL), BF16), *r_shapes),
        in_specs=[pl.BlockSpec((tm, IN_WIDTH), lambda i: (i, 0)), pl.BlockSpec((tm, D_MODEL), lambda i: (i, 0)),
                  pl.BlockSpec((1, D_MODEL), lambda i: (0, 0))] + r_in_specs,
        out_specs=(pl.BlockSpec((IN_WIDTH, D_MODEL), lambda i: (0, 0), pipeline_mode=pl.Buffered(1)), *r_out_specs),
        scratch_shapes=[pltpu.VMEM((IN_WIDTH, D_MODEL), F32)] + r_scratch,
        compiler_params=_cparams(1),
    )(dproj, x, g_pre, *r_args)


def _bwd_in_dx(dproj, x, dout, g_pre, w_in_t, exchanges):
    n_tok = x.shape[0]
    tm = TOK_TILE
    n_tiles = n_tok // tm
    r_args, r_in_specs, r_shapes, r_out_specs, r_scratch, bind = _riders(exchanges)

    def body(*refs):
        dp_ref, x_ref, dout_ref, g_ref, wt_ref = refs[:5]
        in_refs = refs[5:5 + len(r_args)]
        dx_ref, dg_ref = refs[5 + len(r_args):7 + len(r_args)]
        out_refs = refs[7 + len(r_args):7 + 2 * len(r_args)]
        i = pl.program_id(0)
        at_start, at_end = bind(in_refs, out_refs, refs[7 + 2 * len(r_args):], i, n_tiles, pass_on_step=n_tiles - 2)
        at_start()

        @pl.when(i == 0)
        def _():
            dg_ref[...] = jnp.zeros_like(dg_ref)

        xf = x_ref[...]
        r = lax.rsqrt(jnp.mean(xf * xf, axis=-1, keepdims=True) + NORM_EPS)
        xn = xf * r
        dh = _dot(dp_ref[...], wt_ref[...])
        dg_ref[...] += jnp.sum(dh * xn, axis=0, keepdims=True)
        dhn = dh * g_ref[...]
        dx_ref[...] = dout_ref[...] + r * (dhn - xn * jnp.mean(dhn * xn, axis=-1, keepdims=True))
        at_end()

    tile = lambda w: pl.BlockSpec((tm, w), lambda i: (i, 0))
    return pl.pallas_call(
        body, name="bwd_in_dx", grid=(n_tiles,),
        out_shape=(jax.ShapeDtypeStruct((n_tok, D_MODEL), F32), jax.ShapeDtypeStruct((1, D_MODEL), F32), *r_shapes),
        in_specs=[tile(IN_WIDTH), tile(D_MODEL), tile(D_MODEL), pl.BlockSpec((1, D_MODEL), lambda i: (0, 0)),
                  pl.BlockSpec((IN_WIDTH, D_MODEL), lambda i: (0, 0), pipeline_mode=pl.Buffered(1))] + r_in_specs,
        out_specs=(tile(D_MODEL), pl.BlockSpec((1, D_MODEL), lambda i: (0, 0)), *r_out_specs),
        scratch_shapes=r_scratch,
        compiler_params=_cparams(1),
    )(dproj, x, dout, g_pre, w_in_t, *r_args)


def _adam_update(w, g, m, v):
    nm = ADAM_B1 * m + (1.0 - ADAM_B1) * g
    nv = ADAM_B2 * v + (1.0 - ADAM_B2) * (g * g)
    m_hat = nm / (1.0 - ADAM_B1 ** ADAM_STEP)
    v_hat = nv / (1.0 - ADAM_B2 ** ADAM_STEP)
    return -ADAM_LR * (m_hat / (jnp.sqrt(v_hat) + ADAM_EPS) + ADAM_WD * w), nm, nv


def _sum_slots(p_ref):
    tot = p_ref[0].astype(F32)
    for d in range(1, N_DEV):
        tot = tot + p_ref[d].astype(F32)
    return tot


def _adamw_parts(parts, w, m, v, layer, results, name):
    n_layers, n_rows, n_cols = w.shape
    tr = n_rows // 2 if n_rows * n_cols > 512 * 1024 else n_rows

    def body(*refs):
        p_ref, w_ref, m_ref, v_ref = refs[:4]
        g_ref, d_ref, nm_ref, nv_ref = refs[-4:]
        g = _sum_slots(p_ref)
        g_ref[0] = g
        d_ref[0], nm_ref[0], nv_ref[0] = _adam_update(w_ref[0], g, m_ref[0], v_ref[0])

    spec = pl.BlockSpec((1, tr, n_cols), lambda i: (layer, i, 0))
    shape = jax.ShapeDtypeStruct((n_layers, n_rows, n_cols), F32)
    kept = [] if results is None else list(results)
    return pl.pallas_call(
        body, name="adamw_" + name, grid=(n_rows // tr,), out_shape=(shape,) * 4,
        in_specs=[pl.BlockSpec((N_DEV, tr, n_cols), lambda i: (0, i, 0)), spec, spec, spec] + [ANY] * len(kept),
        out_specs=(spec,) * 4, input_output_aliases={4 + j: j for j in range(len(kept))},
        compiler_params=_cparams(1),
    )(parts, w, m, v, *kept)


def _adamw(w, g, m, v, name):
    n_rows, n_cols = w.shape
    tr = 512 if n_rows % 512 == 0 else n_rows

    def body(w_ref, g_ref, m_ref, v_ref, d_ref, nm_ref, nv_ref):
        d_ref[...], nm_ref[...], nv_ref[...] = _adam_update(w_ref[...], g_ref[...], m_ref[...], v_ref[...])

    spec = pl.BlockSpec((tr, n_cols), lambda i: (i, 0))
    shape = jax.ShapeDtypeStruct((n_rows, n_cols), F32)
    return pl.pallas_call(
        body, name="adamw_" + name, grid=(n_rows // tr,), out_shape=(shape, shape, shape),
        in_specs=[spec] * 4, out_specs=(spec, spec, spec), compiler_params=_cparams(1),
    )(w, g, m, v)


_SMALL = ("pre_norm_g", "ln_v_g", "ln_v_b", "b_spatial", "sinks", "rel_bias", "post_norm_g")


def _pack(pieces):
    blocks, first_rows, n = [], [], 0
    for p in pieces:
        flat = p.reshape(-1)
        flat = jnp.concatenate([flat, jnp.zeros(((-flat.shape[0]) % (8 * 128),), F32)]).reshape(-1, 128)
        blocks.append(flat)
        first_rows.append(n)
        n += flat.shape[0]
    return jnp.concatenate(blocks, axis=0), first_rows


def _unpack(block, first_row, shape):
    size = math.prod(shape)
    return block[first_row:first_row + -(-size // 128)].reshape(-1)[:size].reshape(shape)


def kernel(x, pre_norm_g, w_in, ln_v_g, ln_v_b, w_spatial, b_spatial, sinks, rel_bias, w_out, post_norm_g, loss_target, m_pre_norm_g, m_w_in, m_ln_v_g, m_ln_v_b, m_w_spatial, m_b_spatial, m_sinks, m_rel_bias, m_w_out, m_post_norm_g, v_pre_norm_g, v_w_in, v_ln_v_g, v_ln_v_b, v_w_spatial, v_b_spatial, v_sinks, v_rel_bias, v_w_out, v_post_norm_g):
    weights = dict(pre_norm_g=pre_norm_g, w_in=w_in, ln_v_g=ln_v_g, ln_v_b=ln_v_b, w_spatial=w_spatial, b_spatial=b_spatial,
                   sinks=sinks, rel_bias=rel_bias, w_out=w_out, post_norm_g=post_norm_g)
    mom_m = dict(pre_norm_g=m_pre_norm_g, w_in=m_w_in, ln_v_g=m_ln_v_g, ln_v_b=m_ln_v_b, w_spatial=m_w_spatial,
                 b_spatial=m_b_spatial, sinks=m_sinks, rel_bias=m_rel_bias, w_out=m_w_out, post_norm_g=m_post_norm_g)
    mom_v = dict(pre_norm_g=v_pre_norm_g, w_in=v_w_in, ln_v_g=v_ln_v_g, ln_v_b=v_ln_v_b, w_spatial=v_w_spatial,
                 b_spatial=v_b_spatial, sinks=v_sinks, rel_bias=v_rel_bias, w_out=v_w_out, post_norm_g=v_post_norm_g)
    n_seq, seq_len, _ = x.shape
    n_layers = w_in.shape[0]
    x2 = x.reshape(n_seq * seq_len, D_MODEL)
    target2 = loss_target.reshape(n_seq * seq_len, D_MODEL)
    row = lambda p, l: p[l][None]

    wt_shards = jnp.swapaxes(w_in, 1, 2).astype(BF16)
    wo_shards = w_out.astype(BF16)
    bucket = _window_buckets()
    bias = _bias_table(rel_bias, bucket)
    wcat, wtcat, bs = _spatial_tables(w_spatial, jnp.swapaxes(b_spatial, 1, 2))

    wt, wo = [None] * n_layers, [None] * n_layers
    wt[0], wo[0] = _gather_weights(wt_shards, wo_shards, 0)
    xs, saved = [x2], []
    for l in range(n_layers):
        layer_args = (xs[-1], row(pre_norm_g, l), wt[l], row(ln_v_g, l), row(ln_v_b, l), wcat[l], bs[l], sinks[l], bias,
                      wo[l], row(post_norm_g, l), seq_len)
        if l + 1 < n_layers:
            *acts, xn, y, wt[l + 1], wo[l + 1] = _fwd_layer(*layer_args, gather=(wt_shards, wo_shards, l + 1))
            xs.append(xn)
        else:
            *acts, dx, y, loss = _fwd_layer(*layer_args, target=target2)
        saved.append((y, *acts))

    small = [None] * n_layers
    parts = [None] * n_layers
    waiting = None
    for l in reversed(range(n_layers)):
        hosted = {} if waiting is None else dict(exchange=waiting[:2], allgather=waiting[2])
        dproj, *rest = _bwd_mix(dx, *saved[l], row(ln_v_g, l), row(ln_v_b, l), wcat[l], wtcat[l], bs[l], wo[l],
                                row(post_norm_g, l), **hosted)
        dwo, dws, dmix, dlng, dlnb, dgpost, dbias = rest[:7]
        dws = dws.reshape(HEADS * CHUNK, CHUNK).astype(BF16)
        if waiting is not None:
            parts[l + 1] = list(rest[7:])
        if l > 0:
            dx, dwt, dgpre = _bwd_in(dproj, xs[l], dx, row(pre_norm_g, l), wt[l])
            waiting = (dwt, dwo, dws)
        else:
            dwt, po, ws_all = _bwd_in_dw(dproj, xs[l], row(pre_norm_g, l), [((dwo,), True), ((dws,), False)])
            dx, dgpre, pt = _bwd_in_dx(dproj, xs[l], dx, row(pre_norm_g, l), wt[l], [((dwt,), True)])
            parts[l] = [pt, po, ws_all]
        small[l] = dict(pre_norm_g=dgpre[0], ln_v_g=dlng[0], ln_v_b=dlnb[0], dmix=dmix, dbias=dbias,
                        post_norm_g=dgpost[0])

    db = _b_spatial_grad(jnp.stack([s["dmix"] for s in small]))
    pieces = {n: jnp.stack([s[n] for s in small]) for n in ("pre_norm_g", "ln_v_g", "ln_v_b", "post_norm_g")}
    pieces["b_spatial"] = jnp.swapaxes(db[:, :, :HEADS], 1, 2)
    pieces["rel_bias"], pieces["sinks"] = _rel_bias_grad(jnp.stack([s["dbias"] for s in small]), bucket)
    packed, first_rows = _pack([pieces[n] for n in _SMALL] + [loss])
    total = _allreduce_small(packed)
    grad = {n: _unpack(total, r, weights[n].shape) for n, r in zip(_SMALL, first_rows)}
    loss_out = total[first_rows[-1], 0]

    delta, new_m, new_v = {}, {}, {}
    for n in _SMALL:
        w = weights[n]
        two_d = (-1, w.shape[-1])
        d, nm, nv = _adamw(w.reshape(two_d), grad[n].reshape(two_d), mom_m[n].reshape(two_d), mom_v[n].reshape(two_d), n)
        delta[n], new_m[n], new_v[n] = d.reshape(w.shape), nm.reshape(w.shape), nv.reshape(w.shape)

    t3 = lambda p: jnp.swapaxes(p, 1, 2)
    flat3 = lambda p: p.reshape(n_layers, HEADS * CHUNK, CHUNK)
    for n, which, view, back in (("w_in", 0, t3, t3), ("w_out", 1, lambda p: p, lambda p: p),
                                 ("w_spatial", 2, flat3, lambda p: p.reshape(w_spatial.shape))):
        results = None
        for l in reversed(range(n_layers)):
            results = _adamw_parts(parts[l][which], view(weights[n]), view(mom_m[n]), view(mom_v[n]), l, results, n)
        grad[n], delta[n], new_m[n], new_v[n] = (back(r) for r in results)

    names = tuple(weights)
    return (loss_out, dx.reshape(x.shape), *[grad[n] for n in names], *[delta[n] for n in names],
            *[new_m[n] for n in names], *[new_v[n] for n in names])
```

```python
import math

import jax
import jax.numpy as jnp
from jax import lax
from jax.experimental import pallas as pl
from jax.experimental.pallas import tpu as pltpu

F32 = jnp.float32
BF16 = jnp.bfloat16

D_MODEL = 1024
A_WIDTH = 512
B_WIDTH = 512
KV_WIDTH = 128
IN_WIDTH = 3 * A_WIDTH + 2 * B_WIDTH + 2 * KV_WIDTH
CHUNK = 128
HEADS = 8
HEAD_DIM = 64
REL_BUCKETS = 32
NORM_EPS = 1e-6
NEG = -1e30
Q_OFF = 3 * A_WIDTH
K_OFF = Q_OFF + B_WIDTH
BZ_OFF = K_OFF + 2 * KV_WIDTH
QK_SCALE = HEAD_DIM ** -0.5

ADAM_LR = 0.001
ADAM_B1 = 0.9
ADAM_B2 = 0.999
ADAM_EPS = 1e-08
ADAM_WD = 0.01
ADAM_STEP = 10

TOK_TILE = 512
CHUNK_UNROLL = 4
BWD_CHUNK_UNROLL = 4
ADAMW_BLOCK_ELEMS = 192 * 1024
VMEM_LIMIT_V7X = 60 * 1024 * 1024

N_DEV = 8
N_CHIPS = 4
MESH_ID = pl.DeviceIdType.MESH
ANY = pl.BlockSpec(memory_space=pl.ANY)
SMEM = pl.BlockSpec(memory_space=pltpu.SMEM)


def _cparams(n_axes):
    return pltpu.CompilerParams(dimension_semantics=("arbitrary",) * n_axes, vmem_limit_bytes=VMEM_LIMIT_V7X)


_GELU_C = math.sqrt(2.0 / math.pi)
_GELU_C3 = _GELU_C * 0.044715


def _gelu(x):
    t = jnp.tanh(x * (_GELU_C3 * (x * x) + _GELU_C))
    return x * (0.5 * t + 0.5)


def _gelu_and_grad(x):
    x2 = x * x
    t = jnp.tanh(x * (_GELU_C3 * x2 + _GELU_C))
    cdf = 0.5 * t + 0.5
    d = cdf + (x * (cdf * (1.0 - cdf))) * ((6.0 * _GELU_C3) * x2 + 2.0 * _GELU_C)
    return x * cdf, d


def _sigmoid(x):
    return 0.5 + 0.5 * jnp.tanh(0.5 * x)


def _silu(x):
    return x * _sigmoid(x)


def _silu_and_grad(x):
    s = _sigmoid(x)
    return x * s, s * (1.0 + x * (1.0 - s))


def _dot(a, b):
    return jnp.dot(a, b, preferred_element_type=F32)


def _dot_nt(a, b):
    return lax.dot_general(a, b, (((1,), (1,)), ((), ())), preferred_element_type=F32)


def _dot_tn(a, b):
    return lax.dot_general(a, b, (((0,), (0,)), ((), ())), preferred_element_type=F32)


def _lo_mask(shape):
    return lax.broadcasted_iota(jnp.int32, shape, 1) < HEAD_DIM


def _swap_halves(v):
    return pltpu.roll(v, HEAD_DIM, 1)


def _window_buckets():
    q_loc = jnp.arange(CHUNK)[:, None]
    j_loc = jnp.arange(CHUNK)[None, :]
    d = q_loc - j_loc + jnp.where(j_loc > q_loc, CHUNK, 0)
    max_exact = REL_BUCKETS // 2
    safe = jnp.maximum(d, 1).astype(F32)
    large = max_exact + (jnp.log(safe / max_exact) / math.log(CHUNK / max_exact)
                         * (REL_BUCKETS - max_exact)).astype(jnp.int32)
    large = jnp.minimum(large, REL_BUCKETS - 1)
    return jnp.where(d < max_exact, d, large).astype(jnp.int32)


def _bias_table(rel_bias, bucket):
    def body(rb_ref, bk_ref, out_ref):
        bk = bk_ref[...]
        for h in range(HEADS):
            acc = jnp.zeros(bk.shape, F32)
            for b in range(REL_BUCKETS):
                acc = jnp.where(bk == b, rb_ref[b, h], acc)
            out_ref[h] = acc

    vmem = pl.BlockSpec(memory_space=pltpu.VMEM)
    return pl.pallas_call(
        body, name="bias_table",
        out_shape=jax.ShapeDtypeStruct((HEADS, CHUNK, CHUNK), F32),
        in_specs=[SMEM, vmem], out_specs=vmem,
    )(rel_bias, bucket)


def _rel_bias_grad(dbias, bucket):
    n_layers = dbias.shape[0]

    def body(db_ref, bk_ref, out_ref, dsink_ref):
        bk = bk_ref[...]
        for h in range(HEADS):
            tot = jnp.zeros((CHUNK, CHUNK), F32)
            for l in range(n_layers):
                ds = db_ref[l, h]
                dsink_ref[l, h] = -jnp.sum(ds)
                tot = tot + ds
            for b in range(REL_BUCKETS):
                out_ref[b, h] = jnp.sum(jnp.where(bk == b, tot, 0.0))

    vmem = pl.BlockSpec(memory_space=pltpu.VMEM)
    return pl.pallas_call(
        body, name="rel_bias_grad",
        out_shape=(jax.ShapeDtypeStruct((REL_BUCKETS, HEADS), F32), jax.ShapeDtypeStruct((n_layers, HEADS), F32)),
        in_specs=[vmem] * 2, out_specs=(SMEM, SMEM),
    )(dbias, bucket)


def _spatial_tables(w_spatial, b_spatial_t):
    n_layers = w_spatial.shape[0]

    def body(w_ref, b_ref, wcat_ref, wtcat_ref, bs_ref):
        row = lax.broadcasted_iota(jnp.int32, (CHUNK, CHUNK), 0)
        col = lax.broadcasted_iota(jnp.int32, (CHUNK, CHUNK), 1)
        causal = col <= row
        lo = _lo_mask((CHUNK, CHUNK))
        for p in range(4):
            for half in range(2):
                w = jnp.where(causal, w_ref[0, 2 * p + half], 0.0)
                wcat_ref[0, p, :, half * CHUNK:(half + 1) * CHUNK] = w.astype(BF16)
                wtcat_ref[0, p, :, half * CHUNK:(half + 1) * CHUNK] = w.T.astype(BF16)
            b = b_ref[0]
            bs_ref[0, p] = jnp.where(lo, b[:, 2 * p:2 * p + 1], b[:, 2 * p + 1:2 * p + 2])

    return pl.pallas_call(
        body, name="spatial_tables", grid=(n_layers,),
        out_shape=(jax.ShapeDtypeStruct((n_layers, 4, CHUNK, 2 * CHUNK), BF16),
                   jax.ShapeDtypeStruct((n_layers, 4, CHUNK, 2 * CHUNK), BF16),
                   jax.ShapeDtypeStruct((n_layers, 4, CHUNK, CHUNK), F32)),
        in_specs=[pl.BlockSpec((1, HEADS, CHUNK, CHUNK), lambda l: (l, 0, 0, 0)),
                  pl.BlockSpec((1, CHUNK, HEADS), lambda l: (l, 0, 0))],
        out_specs=(pl.BlockSpec((1, 4, CHUNK, 2 * CHUNK), lambda l: (l, 0, 0, 0)),
                   pl.BlockSpec((1, 4, CHUNK, 2 * CHUNK), lambda l: (l, 0, 0, 0)),
                   pl.BlockSpec((1, 4, CHUNK, CHUNK), lambda l: (l, 0, 0, 0))),
        compiler_params=_cparams(1),
    )(w_spatial, b_spatial_t)


def _b_spatial_grad(dmix):
    n_layers = dmix.shape[0]

    def body(d_ref, out_ref):
        lane = lax.broadcasted_iota(jnp.int32, (CHUNK, CHUNK), 1)
        acc = jnp.zeros((CHUNK, CHUNK), F32)
        for p in range(4):
            t = d_ref[0, :, p * CHUNK:(p + 1) * CHUNK]
            s_lo = jnp.sum(jnp.where(lane < HEAD_DIM, t, 0.0), axis=1, keepdims=True)
            s_hi = jnp.sum(jnp.where(lane < HEAD_DIM, 0.0, t), axis=1, keepdims=True)
            acc = jnp.where(lane == 2 * p, s_lo, acc)
            acc = jnp.where(lane == 2 * p + 1, s_hi, acc)
        out_ref[0] = acc

    return pl.pallas_call(
        body, name="b_spatial_grad", grid=(n_layers,),
        out_shape=jax.ShapeDtypeStruct((n_layers, CHUNK, CHUNK), F32),
        in_specs=[pl.BlockSpec((1, CHUNK, A_WIDTH), lambda l: (l, 0, 0))],
        out_specs=pl.BlockSpec((1, CHUNK, CHUNK), lambda l: (l, 0, 0)),
        compiler_params=_cparams(1),
    )(dmix)


def _place():
    x, y, c = lax.axis_index("x"), lax.axis_index("y"), lax.axis_index("c")
    other_chips = [(1 - x, y), (x, 1 - y), (1 - x, 1 - y)]
    return x, y, c, other_chips


N_GATHER_SEMS = 12


def _gather_phases(shards, fulls, send_sems, recv_sems, local_sems):
    x, y, c, chips = _place()
    sibling = (x, y, 1 - c)
    n_arr = len(shards)

    def half_rows(a, chip, half):
        n = shards[a].shape[0]
        start = (2 * chip[0] + chip[1]) * n + half * (n // 2)
        return fulls[a].at[pl.ds(pl.multiple_of(start, 16), n // 2), :]

    def my_half(a):
        n = shards[a].shape[0]
        return shards[a].at[pl.ds(pl.multiple_of(c * (n // 2), 16), n // 2), :]

    def copy(k, a, src, chip, half, to):
        return pltpu.make_async_remote_copy(
            src_ref=src, dst_ref=half_rows(a, chip, half), send_sem=send_sems.at[n_arr * k + a],
            recv_sem=recv_sems.at[n_arr * k + a], device_id=to, device_id_type=MESH_ID)

    def local(a):
        n = shards[a].shape[0]
        mine = fulls[a].at[pl.ds(pl.multiple_of((2 * x + y) * n, 16), n), :]
        return pltpu.make_async_copy(shards[a], mine, local_sems.at[a])

    def first(k, a):
        return copy(k, a, my_half(a), (x, y), c, (chips[k][0], chips[k][1], c))

    def passed(k, a):
        return copy(3 + k, a, half_rows(a, chips[k], c), chips[k], c, sibling)

    def phase_a():
        for a in range(n_arr):
            local(a).start()
        for k in range(3):
            for a in range(n_arr):
                first(k, a).start()

    def phase_b():
        for k in range(3):
            for a in range(n_arr):
                copy(k, a, my_half(a), chips[k], c, sibling).wait_recv()
                passed(k, a).start()

    def phase_c():
        for k in range(3):
            for a in range(n_arr):
                copy(3 + k, a, my_half(a), chips[k], 1 - c, sibling).wait_recv()
        for k in range(3):
            for a in range(n_arr):
                first(k, a).wait_send()
                passed(k, a).wait_send()
        for a in range(n_arr):
            local(a).wait()

    return phase_a, phase_b, phase_c


N_EXCHANGE_SEMS = 14


def _exchange_phases(partials, parts, send_sems, recv_sems, local_sems, by_chip=True):
    x, y, c, chips = _place()
    me, sibling = (x, y, c), (x, y, 1 - c)
    n_arr = len(partials)

    def block(a, chip):
        if not by_chip:
            return partials[a]
        n = partials[a].shape[0] // N_CHIPS
        return partials[a].at[pl.ds(pl.multiple_of((2 * chip[0] + chip[1]) * n, 16), n), :]

    def slot(a, dev):
        return parts[a].at[4 * dev[0] + 2 * dev[1] + dev[2]]

    def copy(k, a, src, origin, to):
        return pltpu.make_async_remote_copy(
            src_ref=src, dst_ref=slot(a, origin), send_sem=send_sems.at[n_arr * k + a],
            recv_sem=recv_sems.at[n_arr * k + a], device_id=to, device_id_type=MESH_ID)

    def local(a):
        return pltpu.make_async_copy(block(a, (x, y)), slot(a, me), local_sems.at[a])

    def first(k, a):
        if k == 0:
            return copy(0, a, block(a, (x, y)), me, sibling)
        chip = chips[k - 1]
        return copy(k, a, block(a, chip), me, (chip[0], chip[1], c))

    def passed(k, a):
        origin = (chips[k][0], chips[k][1], c)
        return copy(4 + k, a, slot(a, origin), origin, sibling)

    def phase_a():
        for a in range(n_arr):
            local(a).start()
        for k in range(4):
            for a in range(n_arr):
                first(k, a).start()

    def phase_b():
        for k in range(3):
            for a in range(n_arr):
                copy(1 + k, a, block(a, (x, y)), (chips[k][0], chips[k][1], c), me).wait_recv()
                passed(k, a).start()

    def phase_c():
        for a in range(n_arr):
            copy(0, a, block(a, (x, y)), sibling, me).wait_recv()
        for k in range(3):
            for a in range(n_arr):
                copy(4 + k, a, block(a, (x, y)), (chips[k][0], chips[k][1], 1 - c), me).wait_recv()
        for k in range(4):
            for a in range(n_arr):
                first(k, a).wait_send()
        for k in range(3):
            for a in range(n_arr):
                passed(k, a).wait_send()
        for a in range(n_arr):
            local(a).wait()

    return phase_a, phase_b, phase_c


def _comm_scratch(n_sems):
    return [pltpu.SemaphoreType.DMA((n_sems,)), pltpu.SemaphoreType.DMA((n_sems,)), pltpu.SemaphoreType.DMA((2,))]


def _gather_weights(wt_shards, wo_shards, layer):
    wt_rows, wo_rows = wt_shards.shape[1], wo_shards.shape[1]

    def body(wt_ref, wo_ref, wt_full, wo_full, send_sems, recv_sems, local_sems):
        phases = _gather_phases((wt_ref.at[layer], wo_ref.at[layer]), (wt_full, wo_full), send_sems, recv_sems, local_sems)
        for phase in phases:
            phase()

    return pl.pallas_call(
        body, name="gather_weights",
        out_shape=(jax.ShapeDtypeStruct((N_CHIPS * wt_rows, D_MODEL), BF16),
                   jax.ShapeDtypeStruct((N_CHIPS * wo_rows, D_MODEL), BF16)),
        in_specs=[ANY, ANY], out_specs=(ANY, ANY), scratch_shapes=_comm_scratch(N_GATHER_SEMS),
    )(wt_shards, wo_shards)


def _allreduce_small(part):
    n_rows = part.shape[0]

    def body(p_ref, tot_ref, all_ref, send_sems, recv_sems, local_sem):
        x, y, c, chips = _place()
        me, sibling = (x, y, c), (x, y, 1 - c)

        def rows(dev):
            return all_ref.at[pl.ds(pl.multiple_of((4 * dev[0] + 2 * dev[1] + dev[2]) * n_rows, 8), n_rows), :]

        def copy(k, origin, to, src=None):
            return pltpu.make_async_remote_copy(
                src_ref=rows(origin) if src is None else src, dst_ref=rows(origin), send_sem=send_sems.at[k],
                recv_sem=recv_sems.at[k], device_id=to, device_id_type=MESH_ID)

        mine = pltpu.make_async_copy(p_ref, rows(me), local_sem)
        mine.start()
        first = [copy(0, me, sibling, src=p_ref)]
        first += [copy(1 + k, me, (chip[0], chip[1], c), src=p_ref) for k, chip in enumerate(chips)]
        for cp in first:
            cp.start()
        passed = []
        for k, chip in enumerate(chips):
            origin = (chip[0], chip[1], c)
            copy(1 + k, origin, me).wait_recv()
            fwd = copy(4 + k, origin, sibling)
            fwd.start()
            passed.append(fwd)
        copy(0, sibling, me).wait_recv()
        for k, chip in enumerate(chips):
            copy(4 + k, (chip[0], chip[1], 1 - c), me).wait_recv()
        for cp in first + passed:
            cp.wait_send()
        mine.wait()
        tot = all_ref[0:n_rows, :]
        for d in range(1, N_DEV):
            tot = tot + all_ref[d * n_rows:(d + 1) * n_rows, :]
        tot_ref[...] = tot

    vmem = pl.BlockSpec(memory_space=pltpu.VMEM)
    return pl.pallas_call(
        body, name="allreduce_small",
        out_shape=jax.ShapeDtypeStruct((n_rows, 128), F32),
        in_specs=[vmem], out_specs=vmem,
        scratch_shapes=[pltpu.VMEM((N_DEV * n_rows, 128), F32), pltpu.SemaphoreType.DMA((7,)),
                        pltpu.SemaphoreType.DMA((7,)), pltpu.SemaphoreType.DMA],
        compiler_params=pltpu.CompilerParams(vmem_limit_bytes=VMEM_LIMIT_V7X),
    )(part)


def _hosted(phases, step, n_steps, pass_on_step=None):
    phase_a, phase_b, phase_c = phases
    if pass_on_step is None:
        pass_on_step = (3 * n_steps) // 4

    def at_start():
        pl.when(step == 0)(phase_a)

    def at_end():
        pl.when(step == pass_on_step)(phase_b)
        pl.when(step == n_steps - 1)(phase_c)

    return at_start, at_end


def _layer_norm_stats(vv):
    mu = jnp.mean(vv, axis=-1, keepdims=True)
    xc = vv - mu
    rs = lax.rsqrt(jnp.mean(xc * xc, axis=-1, keepdims=True) + NORM_EPS)
    return xc * rs, rs


def _blockdiag(v, lo):
    zero = jnp.zeros_like(v)
    return jnp.concatenate([jnp.where(lo, v, zero), jnp.where(lo, zero, v)], axis=0)


def _softmax_sink(s, sink):
    m = jnp.maximum(jnp.max(s, axis=-1, keepdims=True), sink)
    e = jnp.exp(s - m)
    esink = jnp.exp(sink - m)
    den = jnp.sum(e, axis=-1, keepdims=True) + esink
    return e / den


def _kv_rows(cur_ref, halo_ref, r0, c):
    prev_in_tile = cur_ref[pl.ds(pl.multiple_of(jnp.maximum(r0 - CHUNK, 0), CHUNK), CHUNK), :]
    prev = jnp.where(c == 0, halo_ref[...], prev_in_tile)
    kv2 = jnp.concatenate([prev, cur_ref[pl.ds(r0, CHUNK), :]], axis=0)
    k2, v2 = kv2[:, 0:KV_WIDTH], kv2[:, KV_WIDTH:2 * KV_WIDTH]
    return (k2, _swap_halves(k2)), (v2, _swap_halves(v2))


def _window_square(over_keys, prev):
    return jnp.where(prev, over_keys[:, 0:CHUNK], over_keys[:, CHUNK:2 * CHUNK])


def _window_keys(square, prev):
    zero = jnp.zeros_like(square)
    return jnp.concatenate([jnp.where(prev, square, zero), jnp.where(prev, zero, square)], axis=1)


def _ahead(shape=(CHUNK, CHUNK)):
    return lax.broadcasted_iota(jnp.int32, shape, 1) - lax.broadcasted_iota(jnp.int32, shape, 0)


def _dead_mask(first, ahead):
    return ahead > jnp.where(first, 0, CHUNK)


def _head_of(p, half):
    return 2 * p + half, int(half != p // 2)


_HEADS_BY_COPY = tuple(tuple((p, half) for p in range(4) for half in range(2) if _head_of(p, half)[1] == sw)
                       for sw in range(2))


def _masked_halves(tile, lo):
    zero = jnp.zeros_like(tile)
    return {0: jnp.where(lo, tile, zero), 1: jnp.where(lo, zero, tile)}


def _query_stacks(q_tiles, lo):
    qm = {p: _masked_halves(q_tiles[p] * QK_SCALE, lo) for p in range(4)}
    return [jnp.concatenate([qm[p][half] for p, half in members], axis=0) for members in _HEADS_BY_COPY]


def _attention_probs(q_stacks, ks, bias_ref, sink_ref, dead, prev):
    probs = {}
    for sw, members in enumerate(_HEADS_BY_COPY):
        s_stack = _dot_nt(q_stacks[sw], ks[sw])
        for i, (p, half) in enumerate(members):
            head = 2 * p + half
            s = _window_square(s_stack[i * CHUNK:(i + 1) * CHUNK, :], prev) + bias_ref[head]
            s = jnp.where(dead, NEG, s)
            probs[p, half] = _softmax_sink(s, sink_ref[head]).astype(BF16)
    return probs


def _attention_values(probs, vs, prev):
    outs, p_stacks = {}, []
    for sw, members in enumerate(_HEADS_BY_COPY):
        p_stack = jnp.concatenate([_window_keys(probs[m], prev) for m in members], axis=0)
        r_stack = _dot(p_stack, vs[sw])
        p_stacks.append(p_stack)
        for i, m in enumerate(members):
            outs[m] = r_stack[i * CHUNK:(i + 1) * CHUNK, :]
    return outs, p_stacks


def _fwd_layer(x, g_pre, w_in_t, ln_g, ln_b, wcat, bs, sinks, bias, w_out, g_post, seq_len, gather=None, target=None):
    n_tok = x.shape[0]
    tb = TOK_TILE
    n_chunks = tb // CHUNK
    n_tiles = n_tok // tb
    n_in = 11
    assert gather is None or target is None

    def body(*refs):
        (x_ref, gpre_ref, wt_ref, lng_ref, lnb_ref, wcat_ref, bs_ref, sink_ref, bias_ref, wout_ref,
         gpost_ref) = refs[:n_in]
        at_start = at_end = lambda: None
        if gather is not None:
            (wts_ref, wos_ref, a_ref, q_ref, kv_ref, bz_ref, probs_ref, xn_ref, y_ref, wt_full, wo_full, ycat_s, halo_ref,
             send_sems, recv_sems, local_sems) = refs[n_in:]
            phases = _gather_phases((wts_ref.at[gather[2]], wos_ref.at[gather[2]]), (wt_full, wo_full), send_sems,
                                    recv_sems, local_sems)
            at_start, at_end = _hosted(phases, pl.program_id(0), n_tiles)
        elif target is not None:
            target_ref, a_ref, q_ref, kv_ref, bz_ref, probs_ref, xn_ref, y_ref, loss_ref, ycat_s, halo_ref = refs[n_in:]
        else:
            a_ref, q_ref, kv_ref, bz_ref, probs_ref, xn_ref, y_ref, ycat_s, halo_ref = refs[n_in:]
        at_start()
        i = pl.program_id(0)
        lo = _lo_mask((CHUNK, CHUNK))
        ahead = _ahead()
        prev = ahead > 0

        @pl.when(i == 0)
        def _():
            halo_ref[...] = jnp.zeros_like(halo_ref)

        xf = x_ref[...]
        r1 = lax.rsqrt(jnp.mean(xf * xf, axis=-1, keepdims=True) + NORM_EPS)
        h = ((xf * r1) * gpre_ref[...]).astype(BF16)
        a_ref[...] = _dot_nt(h, wt_ref[0:Q_OFF, :])
        q_ref[...] = _dot_nt(h, wt_ref[Q_OFF:K_OFF, :]).astype(BF16)
        kv_ref[...] = _dot_nt(h, wt_ref[K_OFF:BZ_OFF, :]).astype(BF16)
        bz_ref[...] = _dot_nt(h, wt_ref[BZ_OFF:IN_WIDTH, :])

        def chunk(c, carry):
            r0 = pl.multiple_of(c * CHUNK, CHUNK)
            rows = pl.ds(r0, CHUNK)
            u = _gelu(a_ref[rows, 0:A_WIDTH])
            vv = _gelu(a_ref[rows, A_WIDTH:2 * A_WIDTH])
            xhat, _ = _layer_norm_stats(vv)
            vnb = (xhat * lng_ref[...] + lnb_ref[...]).astype(BF16)
            for p in range(4):
                blk = slice(p * CHUNK, (p + 1) * CHUNK)
                mixed = _dot(wcat_ref[p], _blockdiag(vnb[:, blk], lo)) + bs_ref[p]
                sz = _silu(a_ref[rows, 2 * A_WIDTH + p * CHUNK:2 * A_WIDTH + (p + 1) * CHUNK])
                ycat_s[rows, blk] = ((u[:, blk] * mixed) * sz).astype(BF16)
            ks, vs = _kv_rows(kv_ref, halo_ref, r0, c)
            dead = _dead_mask(lax.rem(i * tb + r0, seq_len) == 0, ahead)
            q_tiles = [q_ref[rows, p * CHUNK:(p + 1) * CHUNK] for p in range(4)]
            probs = _attention_probs(_query_stacks(q_tiles, lo), ks, bias_ref, sink_ref, dead, prev)
            for (p, half), prob in probs.items():
                head = 2 * p + half
                probs_ref[rows, head * CHUNK:(head + 1) * CHUNK] = prob
            outs, _ = _attention_values(probs, vs, prev)
            for p in range(4):
                blk = slice(p * CHUNK, (p + 1) * CHUNK)
                o = jnp.where(lo, outs[p, 0], outs[p, 1])
                ycat_s[rows, B_WIDTH + p * CHUNK:B_WIDTH + (p + 1) * CHUNK] = (o * _silu(bz_ref[rows, blk])).astype(BF16)
            return carry

        lax.fori_loop(0, n_chunks, chunk, 0, unroll=CHUNK_UNROLL)
        halo_ref[...] = kv_ref[tb - CHUNK:tb, :]
        y = _dot(ycat_s[...], wout_ref[...])
        r = lax.rsqrt(jnp.mean(y * y, axis=-1, keepdims=True) + NORM_EPS)
        y_ref[...] = y
        xn = x_ref[...] + (y * r) * gpost_ref[...]
        if target is None:
            xn_ref[...] = xn
        else:
            d = xn - target_ref[...]
            xn_ref[...] = d * (1.0 / D_MODEL)

            @pl.when(i == 0)
            def _():
                loss_ref[0, 0] = 0.0

            loss_ref[0, 0] += 0.5 * jnp.sum(jnp.mean(d * d, axis=-1, keepdims=True))
        at_end()

    tile = lambda w: pl.BlockSpec((tb, w), lambda i: (i, 0))
    whole = lambda shape, **kw: pl.BlockSpec(shape, lambda i: (0,) * len(shape), **kw)
    in_specs = [tile(D_MODEL), whole((1, D_MODEL)), whole((IN_WIDTH, D_MODEL), pipeline_mode=pl.Buffered(1)),
                whole((1, A_WIDTH)), whole((1, A_WIDTH)), whole((4, CHUNK, 2 * CHUNK)), whole((4, CHUNK, CHUNK)), SMEM,
                whole((HEADS, CHUNK, CHUNK)), whole((D_MODEL, D_MODEL)), whole((1, D_MODEL))]
    out_shape = [jax.ShapeDtypeStruct((n_tok, Q_OFF), F32), jax.ShapeDtypeStruct((n_tok, B_WIDTH), BF16),
                 jax.ShapeDtypeStruct((n_tok, 2 * KV_WIDTH), BF16), jax.ShapeDtypeStruct((n_tok, B_WIDTH), F32),
                 jax.ShapeDtypeStruct((n_tok, HEADS * CHUNK), BF16), jax.ShapeDtypeStruct((n_tok, D_MODEL), F32),
                 jax.ShapeDtypeStruct((n_tok, D_MODEL), F32)]
    out_specs = [tile(Q_OFF), tile(B_WIDTH), tile(2 * KV_WIDTH), tile(B_WIDTH), tile(HEADS * CHUNK), tile(D_MODEL),
                 tile(D_MODEL)]
    scratch = [pltpu.VMEM((tb, D_MODEL), BF16), pltpu.VMEM((CHUNK, 2 * KV_WIDTH), BF16)]
    args = [x, g_pre, w_in_t, ln_g, ln_b, wcat, bs, sinks, bias, w_out, g_post]
    if gather is not None:
        wt_shards, wo_shards, _ = gather
        in_specs += [ANY, ANY]
        args += [wt_shards, wo_shards]
        out_shape += [jax.ShapeDtypeStruct((N_CHIPS * wt_shards.shape[1], D_MODEL), BF16),
                      jax.ShapeDtypeStruct((N_CHIPS * wo_shards.shape[1], D_MODEL), BF16)]
        out_specs += [ANY, ANY]
        scratch += _comm_scratch(N_GATHER_SEMS)
    if target is not None:
        in_specs.append(tile(D_MODEL))
        args.append(target)
        out_shape.append(jax.ShapeDtypeStruct((1, 1), F32))
        out_specs.append(SMEM)
    name = "fwd_layer" + ("" if gather is None else "_gather") + ("" if target is None else "_loss")
    return pl.pallas_call(
        body, name=name, grid=(n_tiles,),
        out_shape=tuple(out_shape), in_specs=in_specs, out_specs=tuple(out_specs), scratch_shapes=scratch,
        compiler_params=_cparams(1),
    )(*args)


def _bwd_mix(dout, y, a, q, kv, bz, probs, ln_g, ln_b, wcat, wtcat, bs, w_out, g_post, exchange=None, allgather=None):
    n_tok = y.shape[0]
    tb = TOK_TILE
    n_chunks = tb // CHUNK
    n_tiles = n_tok // tb

    def body(*refs):
        refs = list(refs)
        take = lambda n: [refs.pop(0) for _ in range(n)]
        (dout_ref, y_ref, a_ref, q_ref, kv_ref, halo_ref, bz_ref, probs_ref, lng_ref, lnb_ref, wcat_ref, wtcat_ref, bs_ref,
         wout_ref, gpost_ref) = take(15)
        if exchange is not None:
            dwt_in, dwo_in = take(2)
        if allgather is not None:
            (small_in,) = take(1)
        dproj_ref, dwout_ref, dws_ref, dmix_ref, dlng_ref, dlnb_ref, dgpost_ref, dbias_ref = take(8)
        if exchange is not None:
            pt_ref, po_ref = take(2)
        if allgather is not None:
            (small_all,) = take(1)
        ycat_s, dy_s, dyc_s, dkv_s, carry_s, dwout_s, stage_s = take(7)
        starts, ends = [], []
        if exchange is not None:
            hosted = _hosted(_exchange_phases((dwt_in, dwo_in), (pt_ref, po_ref), *take(3)), pl.program_id(0), n_tiles)
            starts.append(hosted[0])
            ends.append(hosted[1])
        if allgather is not None:
            hosted = _hosted(_exchange_phases((small_in,), (small_all,), *take(3), by_chip=False), pl.program_id(0),
                             n_tiles)
            starts.append(hosted[0])
            ends.append(hosted[1])
        at_start = lambda: [f() for f in starts]
        at_end = lambda: [f() for f in ends]
        at_start()
        step = pl.program_id(0)
        lo = _lo_mask((CHUNK, CHUNK))
        prev = _ahead() > 0

        @pl.when(step == 0)
        def _():
            dwout_s[...] = jnp.zeros_like(dwout_s)
            dws_ref[...] = jnp.zeros_like(dws_ref)
            dmix_ref[...] = jnp.zeros_like(dmix_ref)
            dlng_ref[...] = jnp.zeros_like(dlng_ref)
            dlnb_ref[...] = jnp.zeros_like(dlnb_ref)
            dgpost_ref[...] = jnp.zeros_like(dgpost_ref)
            dbias_ref[...] = jnp.zeros_like(dbias_ref)
            carry_s[...] = jnp.zeros_like(carry_s)

        for r0 in range(0, tb, CHUNK):
            rows = slice(r0, r0 + CHUNK)
            yv = y_ref[rows, :]
            dout = dout_ref[rows, :]
            r = lax.rsqrt(jnp.mean(yv * yv, axis=-1, keepdims=True) + NORM_EPS)
            yn = yv * r
            dgpost_ref[...] += jnp.sum(dout * yn, axis=0, keepdims=True)
            dyn = dout * gpost_ref[...]
            dy_s[rows, :] = (r * (dyn - yn * jnp.mean(dyn * yn, axis=-1, keepdims=True))).astype(BF16)
        dyc_s[...] = _dot_nt(dy_s[...], wout_ref[...])
        dkv_s[0:tb, :] = jnp.zeros((tb, 2 * KV_WIDTH), F32)
        dkv_s[tb:tb + CHUNK, :] = carry_s[...]

        def chunk(c, carry):
            r0 = pl.multiple_of(c * CHUNK, CHUNK)
            rows = pl.ds(r0, CHUNK)
            u, gu = _gelu_and_grad(a_ref[rows, 0:A_WIDTH])
            vv, gv = _gelu_and_grad(a_ref[rows, A_WIDTH:2 * A_WIDTH])
            xhat, rs = _layer_norm_stats(vv)
            vnb = (xhat * lng_ref[...] + lnb_ref[...]).astype(BF16)
            d_vn, d_u, d_az = [], [], []
            for p in range(4):
                blk = slice(p * CHUNK, (p + 1) * CHUNK)
                vbd = _blockdiag(vnb[:, blk], lo)
                mixed = _dot(wcat_ref[p], vbd) + bs_ref[p]
                sz, gz = _silu_and_grad(a_ref[rows, 2 * A_WIDTH + p * CHUNK:2 * A_WIDTH + (p + 1) * CHUNK])
                ub = u[:, blk]
                dya = dyc_s[rows, blk]
                um = ub * mixed
                ycat_s[rows, blk] = (um * sz).astype(BF16)
                d_mixed = (dya * ub) * sz
                d_u.append((dya * mixed) * sz)
                d_az.append((dya * um) * gz)
                dmix_ref[:, blk] += d_mixed
                dmbd = _blockdiag(d_mixed.astype(BF16), lo)
                d_vn.append(_dot(wtcat_ref[p], dmbd))
                dws_ref[p] += _dot_nt(dmbd, vnb[:, blk])
            d_vn = jnp.concatenate(d_vn, axis=1)
            dlng_ref[...] += jnp.sum(d_vn * xhat, axis=0, keepdims=True)
            dlnb_ref[...] += jnp.sum(d_vn, axis=0, keepdims=True)
            dxh = d_vn * lng_ref[...]
            d_vv = rs * (dxh - jnp.mean(dxh, axis=-1, keepdims=True)
                         - xhat * jnp.mean(dxh * xhat, axis=-1, keepdims=True))
            dproj_ref[rows, 0:A_WIDTH] = (jnp.concatenate(d_u, axis=1) * gu).astype(BF16)
            dproj_ref[rows, A_WIDTH:2 * A_WIDTH] = (d_vv * gv).astype(BF16)
            dproj_ref[rows, 2 * A_WIDTH:Q_OFF] = jnp.concatenate(d_az, axis=1).astype(BF16)
            ks, vs = _kv_rows(kv_ref, halo_ref, r0, c)
            q_stacks = _query_stacks([q_ref[rows, p * CHUNK:(p + 1) * CHUNK] for p in range(4)], lo)
            probs = {(p, half): probs_ref[rows, (2 * p + half) * CHUNK:(2 * p + half + 1) * CHUNK]
                     for p in range(4) for half in range(2)}
            outs_h, p_stacks = _attention_values(probs, vs, prev)
            dom = {}
            for p in range(4):
                blk = slice(p * CHUNK, (p + 1) * CHUNK)
                sb, gb = _silu_and_grad(bz_ref[rows, blk])
                dyb = dyc_s[rows, B_WIDTH + p * CHUNK:B_WIDTH + (p + 1) * CHUNK]
                o = jnp.where(lo, outs_h[p, 0], outs_h[p, 1])
                ycat_s[rows, B_WIDTH + p * CHUNK:B_WIDTH + (p + 1) * CHUNK] = (o * sb).astype(BF16)
                dproj_ref[rows, BZ_OFF + p * CHUNK:BZ_OFF + (p + 1) * CHUNK] = ((dyb * o) * gb).astype(BF16)
                dom[p] = _masked_halves((dyb * sb).astype(BF16), lo)
            dqs, dk_by_copy, dv_by_copy = {}, [], []
            for sw, members in enumerate(_HEADS_BY_COPY):
                do_stack = jnp.concatenate([dom[p][half] for p, half in members], axis=0)
                dp_stack = _dot_nt(do_stack, vs[sw])
                dsws = []
                for i, (p, half) in enumerate(members):
                    head = 2 * p + half
                    dp = _window_square(dp_stack[i * CHUNK:(i + 1) * CHUNK, :], prev)
                    prob = probs[p, half].astype(F32)
                    delta = jnp.sum(prob * dp, axis=-1, keepdims=True)
                    ds = prob * (dp - delta)
                    dbias_ref[head] += ds
                    dsws.append(_window_keys(ds.astype(BF16), prev))
                ds_stack = jnp.concatenate(dsws, axis=0)
                dq_stack = _dot(ds_stack, ks[sw])
                for i, m in enumerate(members):
                    dqs[m] = dq_stack[i * CHUNK:(i + 1) * CHUNK, :]
                dk_by_copy.append(_dot_tn(ds_stack, q_stacks[sw]))
                dv_by_copy.append(_dot_tn(p_stacks[sw], do_stack))
            for p in range(4):
                dq = jnp.where(lo, dqs[p, 0], dqs[p, 1]) * QK_SCALE
                dproj_ref[rows, Q_OFF + p * CHUNK:Q_OFF + (p + 1) * CHUNK] = dq.astype(BF16)
            both = pl.ds(r0, 2 * CHUNK)
            dkv_s[both, 0:KV_WIDTH] += dk_by_copy[0] + _swap_halves(dk_by_copy[1])
            dkv_s[both, KV_WIDTH:2 * KV_WIDTH] += dv_by_copy[0] + _swap_halves(dv_by_copy[1])
            return carry

        lax.fori_loop(0, n_chunks, chunk, 0, unroll=BWD_CHUNK_UNROLL)
        dwout_s[...] += _dot_tn(ycat_s[...], dy_s[...])
        dproj_ref[:, K_OFF:BZ_OFF] = dkv_s[CHUNK:CHUNK + tb, :].astype(BF16)
        carry_s[...] = dkv_s[0:CHUNK, :]

        @pl.when(step == n_tiles - 1)
        def _():
            row = lax.broadcasted_iota(jnp.int32, (2 * CHUNK, CHUNK), 0)
            col = lax.broadcasted_iota(jnp.int32, (2 * CHUNK, CHUNK), 1)
            causal = col <= jnp.where(row >= CHUNK, row - CHUNK, row)
            for p in range(4):
                dws_ref[p] = jnp.where(causal, dws_ref[p], 0.0)
            rows = stage_s.shape[0]
            for r0 in range(0, D_MODEL, rows):
                stage_s[...] = dwout_s[r0:r0 + rows, :].astype(BF16)
                pltpu.sync_copy(stage_s, dwout_ref.at[r0:r0 + rows, :])

        at_end()

    tile = lambda w: pl.BlockSpec((tb, w), lambda s: (n_tiles - 1 - s, 0))
    whole = lambda shape, **kw: pl.BlockSpec(shape, lambda s: (0,) * len(shape), **kw)
    once = dict(pipeline_mode=pl.Buffered(1))
    in_specs = [tile(D_MODEL), tile(D_MODEL), tile(Q_OFF), tile(B_WIDTH), tile(2 * KV_WIDTH),
                pl.BlockSpec((CHUNK, 2 * KV_WIDTH), lambda s: (jnp.maximum((n_tiles - 1 - s) * n_chunks - 1, 0), 0)),
                tile(B_WIDTH), tile(HEADS * CHUNK), whole((1, A_WIDTH)), whole((1, A_WIDTH)), whole((4, CHUNK, 2 * CHUNK)),
                whole((4, CHUNK, 2 * CHUNK)), whole((4, CHUNK, CHUNK)), whole((D_MODEL, D_MODEL), **once),
                whole((1, D_MODEL))]
    args = [dout, y, a, q, kv, kv, bz, probs, ln_g, ln_b, wcat, wtcat, bs, w_out, g_post]
    small_shapes = [jax.ShapeDtypeStruct((D_MODEL, D_MODEL), BF16), jax.ShapeDtypeStruct((4, 2 * CHUNK, CHUNK), F32),
                    jax.ShapeDtypeStruct((CHUNK, A_WIDTH), F32), jax.ShapeDtypeStruct((1, A_WIDTH), F32),
                    jax.ShapeDtypeStruct((1, A_WIDTH), F32), jax.ShapeDtypeStruct((1, D_MODEL), F32),
                    jax.ShapeDtypeStruct((HEADS, CHUNK, CHUNK), F32)]
    small_specs = [ANY, whole((4, 2 * CHUNK, CHUNK)), whole((CHUNK, A_WIDTH)), whole((1, A_WIDTH)), whole((1, A_WIDTH)),
                   whole((1, D_MODEL)), whole((HEADS, CHUNK, CHUNK))]
    scratch = [pltpu.VMEM((tb, D_MODEL), BF16), pltpu.VMEM((tb, D_MODEL), BF16), pltpu.VMEM((tb, D_MODEL), F32),
               pltpu.VMEM((tb + CHUNK, 2 * KV_WIDTH), F32), pltpu.VMEM((CHUNK, 2 * KV_WIDTH), F32),
               pltpu.VMEM((D_MODEL, D_MODEL), F32), pltpu.VMEM((2 * CHUNK, D_MODEL), BF16)]
    out_shape = [jax.ShapeDtypeStruct((n_tok, IN_WIDTH), BF16)] + small_shapes
    out_specs = [tile(IN_WIDTH)] + small_specs
    if exchange is not None:
        dwt, dwo = exchange
        in_specs += [ANY, ANY]
        args += [dwt, dwo]
        out_shape += [jax.ShapeDtypeStruct((N_DEV, dwt.shape[0] // N_CHIPS, D_MODEL), BF16),
                      jax.ShapeDtypeStruct((N_DEV, dwo.shape[0] // N_CHIPS, D_MODEL), BF16)]
        out_specs += [ANY, ANY]
    if allgather is not None:
        in_specs.append(ANY)
        args.append(allgather)
        out_shape.append(jax.ShapeDtypeStruct((N_DEV,) + allgather.shape, allgather.dtype))
        out_specs.append(ANY)
    for hosted in (exchange, allgather):
        if hosted is not None:
            scratch += _comm_scratch(N_EXCHANGE_SEMS)
    return pl.pallas_call(
        body, name="bwd_mix" + ("" if exchange is None and allgather is None else "_exchange"), grid=(n_tiles,),
        out_shape=tuple(out_shape), in_specs=in_specs, out_specs=tuple(out_specs), scratch_shapes=scratch,
        compiler_params=_cparams(1),
    )(*args)


def _bwd_in(dproj, x, dout, g_pre, w_in_t):
    n_tok = x.shape[0]
    tm = TOK_TILE
    n_tiles = n_tok // tm

    def body(dp_ref, x_ref, dout_ref, g_ref, wt_ref, dx_ref, dwt_ref, dg_ref, acc_s):
        i = pl.program_id(0)

        @pl.when(i == 0)
        def _():
            acc_s[...] = jnp.zeros_like(acc_s)
            dg_ref[...] = jnp.zeros_like(dg_ref)

        xf = x_ref[...]
        r = lax.rsqrt(jnp.mean(xf * xf, axis=-1, keepdims=True) + NORM_EPS)
        xn = xf * r
        h = (xn * g_ref[...]).astype(BF16)
        dp = dp_ref[...]
        dh = _dot(dp, wt_ref[...])
        acc_s[...] += _dot_tn(dp, h)
        dg_ref[...] += jnp.sum(dh * xn, axis=0, keepdims=True)
        dhn = dh * g_ref[...]
        dx_ref[...] = dout_ref[...] + r * (dhn - xn * jnp.mean(dhn * xn, axis=-1, keepdims=True))

        @pl.when(i == n_tiles - 1)
        def _():
            dwt_ref[...] = acc_s[...].astype(BF16)

    return pl.pallas_call(
        body, name="bwd_in", grid=(n_tiles,),
        out_shape=(jax.ShapeDtypeStruct((n_tok, D_MODEL), F32), jax.ShapeDtypeStruct((IN_WIDTH, D_MODEL), BF16),
                   jax.ShapeDtypeStruct((1, D_MODEL), F32)),
        in_specs=[pl.BlockSpec((tm, IN_WIDTH), lambda i: (i, 0)), pl.BlockSpec((tm, D_MODEL), lambda i: (i, 0)),
                  pl.BlockSpec((tm, D_MODEL), lambda i: (i, 0)), pl.BlockSpec((1, D_MODEL), lambda i: (0, 0)),
                  pl.BlockSpec((IN_WIDTH, D_MODEL), lambda i: (0, 0), pipeline_mode=pl.Buffered(1))],
        out_specs=(pl.BlockSpec((tm, D_MODEL), lambda i: (i, 0)),
                   pl.BlockSpec((IN_WIDTH, D_MODEL), lambda i: (0, 0), pipeline_mode=pl.Buffered(1)),
                   pl.BlockSpec((1, D_MODEL), lambda i: (0, 0))),
        scratch_shapes=[pltpu.VMEM((IN_WIDTH, D_MODEL), F32)],
        compiler_params=_cparams(1),
    )(dproj, x, dout, g_pre, w_in_t)


def _riders(exchanges):
    args = [a for arrays, _ in exchanges for a in arrays]
    shapes = [jax.ShapeDtypeStruct((N_DEV, a.shape[0] // N_CHIPS if by_chip else a.shape[0]) + a.shape[1:], a.dtype)
              for arrays, by_chip in exchanges for a in arrays]
    scratch = [s for _ in exchanges for s in _comm_scratch(N_EXCHANGE_SEMS)]

    def bind(in_refs, out_refs, sem_refs, step, n_steps, pass_on_step=None):
        in_refs, out_refs, sem_refs = list(in_refs), list(out_refs), list(sem_refs)
        starts, ends = [], []
        for arrays, by_chip in exchanges:
            n = len(arrays)
            phases = _exchange_phases(tuple(in_refs[:n]), tuple(out_refs[:n]), *sem_refs[:3], by_chip=by_chip)
            del in_refs[:n], out_refs[:n], sem_refs[:3]
            at_start, at_end = _hosted(phases, step, n_steps, pass_on_step)
            starts.append(at_start)
            ends.append(at_end)
        return (lambda: [f() for f in starts]), (lambda: [f() for f in ends])

    return args, [ANY] * len(args), shapes, [ANY] * len(shapes), scratch, bind


def _bwd_in_dw(dproj, x, g_pre, exchanges):
    n_tok = x.shape[0]
    tm = TOK_TILE
    n_tiles = n_tok // tm
    r_args, r_in_specs, r_shapes, r_out_specs, r_scratch, bind = _riders(exchanges)

    def body(*refs):
        dp_ref, x_ref, g_ref = refs[:3]
        in_refs = refs[3:3 + len(r_args)]
        dwt_ref = refs[3 + len(r_args)]
        out_refs = refs[4 + len(r_args):4 + 2 * len(r_args)]
        acc_s = refs[4 + 2 * len(r_args)]
        i = pl.program_id(0)
        at_start, at_end = bind(in_refs, out_refs, refs[5 + 2 * len(r_args):], i, n_tiles)
        at_start()

        @pl.when(i == 0)
        def _():
            acc_s[...] = jnp.zeros_like(acc_s)

        xf = x_ref[...]
        r = lax.rsqrt(jnp.mean(xf * xf, axis=-1, keepdims=True) + NORM_EPS)
        acc_s[...] += _dot_tn(dp_ref[...], ((xf * r) * g_ref[...]).astype(BF16))

        @pl.when(i == n_tiles - 1)
        def _():
            dwt_ref[...] = acc_s[...].astype(BF16)

        at_end()

    return pl.pallas_call(
        body, name="bwd_in_dw", grid=(n_tiles,),
        out_shape=(jax.ShapeDtypeStruct((IN_WIDTH, D_MODEL), BF16), *r_shapes),
        in_specs=[pl.BlockSpec((tm, IN_WIDTH), lambda i: (i, 0)), pl.BlockSpec((tm, D_MODEL), lambda i: (i, 0)),
                  pl.BlockSpec((1, D_MODEL), lambda i: (0, 0))] + r_in_specs,
        out_specs=(pl.BlockSpec((IN_WIDTH, D_MODEL), lambda i: (0, 0), pipeline_mode=pl.Buffered(1)), *r_out_specs),
        scratch_shapes=[pltpu.VMEM((IN_WIDTH, D_MODEL), F32)] + r_scratch,
        compiler_params=_cparams(1),
    )(dproj, x, g_pre, *r_args)


def _bwd_in_dx(dproj, x, dout, g_pre, w_in_t, exchanges):
    n_tok = x.shape[0]
    tm = TOK_TILE
    n_tiles = n_tok // tm
    r_args, r_in_specs, r_shapes, r_out_specs, r_scratch, bind = _riders(exchanges)

    def body(*refs):
        dp_ref, x_ref, dout_ref, g_ref, wt_ref = refs[:5]
        in_refs = refs[5:5 + len(r_args)]
        dx_ref, dg_ref = refs[5 + len(r_args):7 + len(r_args)]
        out_refs = refs[7 + len(r_args):7 + 2 * len(r_args)]
        i = pl.program_id(0)
        at_start, at_end = bind(in_refs, out_refs, refs[7 + 2 * len(r_args):], i, n_tiles, pass_on_step=n_tiles - 2)
        at_start()

        @pl.when(i == 0)
        def _():
            dg_ref[...] = jnp.zeros_like(dg_ref)

        xf = x_ref[...]
        r = lax.rsqrt(jnp.mean(xf * xf, axis=-1, keepdims=True) + NORM_EPS)
        xn = xf * r
        dh = _dot(dp_ref[...], wt_ref[...])
        dg_ref[...] += jnp.sum(dh * xn, axis=0, keepdims=True)
        dhn = dh * g_ref[...]
        dx_ref[...] = dout_ref[...] + r * (dhn - xn * jnp.mean(dhn * xn, axis=-1, keepdims=True))
        at_end()

    tile = lambda w: pl.BlockSpec((tm, w), lambda i: (i, 0))
    return pl.pallas_call(
        body, name="bwd_in_dx", grid=(n_tiles,),
        out_shape=(jax.ShapeDtypeStruct((n_tok, D_MODEL), F32), jax.ShapeDtypeStruct((1, D_MODEL), F32), *r_shapes),
        in_specs=[tile(IN_WIDTH), tile(D_MODEL), tile(D_MODEL), pl.BlockSpec((1, D_MODEL), lambda i: (0, 0)),
                  pl.BlockSpec((IN_WIDTH, D_MODEL), lambda i: (0, 0), pipeline_mode=pl.Buffered(1))] + r_in_specs,
        out_specs=(tile(D_MODEL), pl.BlockSpec((1, D_MODEL), lambda i: (0, 0)), *r_out_specs),
        scratch_shapes=r_scratch,
        compiler_params=_cparams(1),
    )(dproj, x, dout, g_pre, w_in_t, *r_args)


def _adam_update(w, g, m, v):
    nm = ADAM_B1 * m + (1.0 - ADAM_B1) * g
    nv = ADAM_B2 * v + (1.0 - ADAM_B2) * (g * g)
    m_hat = nm / (1.0 - ADAM_B1 ** ADAM_STEP)
    v_hat = nv / (1.0 - ADAM_B2 ** ADAM_STEP)
    return -ADAM_LR * (m_hat / (jnp.sqrt(v_hat) + ADAM_EPS) + ADAM_WD * w), nm, nv


def _sum_slots(p_ref):
    tot = p_ref[0].astype(F32)
    for d in range(1, N_DEV):
        tot = tot + p_ref[d].astype(F32)
    return tot


def _adamw_parts(parts, w, m, v, name):
    n_layers, n_rows, n_cols = w.shape
    tr = n_rows
    while tr * n_cols > ADAMW_BLOCK_ELEMS:
        tr //= 2
    n_blocks = n_rows // tr

    def body(*refs):
        p_refs = refs[:n_layers]
        w_ref, m_ref, v_ref, g_ref, d_ref, nm_ref, nv_ref = refs[n_layers:]
        for k in range(n_layers):
            @pl.when(pl.program_id(0) == k)
            def _(k=k):
                g = _sum_slots(p_refs[k])
                g_ref[0] = g
                d_ref[0], nm_ref[0], nv_ref[0] = _adam_update(w_ref[0], g, m_ref[0], v_ref[0])

    def part_spec(k):
        return pl.BlockSpec((N_DEV, tr, n_cols), lambda l, i: (0, jnp.clip(i + (l - k) * n_blocks, 0, n_blocks - 1), 0))

    spec = pl.BlockSpec((1, tr, n_cols), lambda l, i: (l, i, 0))
    shape = jax.ShapeDtypeStruct((n_layers, n_rows, n_cols), F32)
    return pl.pallas_call(
        body, name="adamw_" + name, grid=(n_layers, n_blocks), out_shape=(shape,) * 4,
        in_specs=[part_spec(k) for k in range(n_layers)] + [spec, spec, spec], out_specs=(spec,) * 4,
        compiler_params=_cparams(2),
    )(*parts, w, m, v)


def _adamw(w, g, m, v, name):
    n_rows, n_cols = w.shape
    tr = 512 if n_rows % 512 == 0 else n_rows

    def body(w_ref, g_ref, m_ref, v_ref, d_ref, nm_ref, nv_ref):
        d_ref[...], nm_ref[...], nv_ref[...] = _adam_update(w_ref[...], g_ref[...], m_ref[...], v_ref[...])

    spec = pl.BlockSpec((tr, n_cols), lambda i: (i, 0))
    shape = jax.ShapeDtypeStruct((n_rows, n_cols), F32)
    return pl.pallas_call(
        body, name="adamw_" + name, grid=(n_rows // tr,), out_shape=(shape, shape, shape),
        in_specs=[spec] * 4, out_specs=(spec, spec, spec), compiler_params=_cparams(1),
    )(w, g, m, v)


_SMALL = ("pre_norm_g", "ln_v_g", "ln_v_b", "b_spatial", "sinks", "rel_bias", "post_norm_g")


def _pack(pieces):
    blocks, first_rows, n = [], [], 0
    for p in pieces:
        flat = p.reshape(-1)
        flat = jnp.concatenate([flat, jnp.zeros(((-flat.shape[0]) % (8 * 128),), F32)]).reshape(-1, 128)
        blocks.append(flat)
        first_rows.append(n)
        n += flat.shape[0]
    return jnp.concatenate(blocks, axis=0), first_rows


def _unpack(block, first_row, shape):
    size = math.prod(shape)
    return block[first_row:first_row + -(-size // 128)].reshape(-1)[:size].reshape(shape)


def kernel(x, pre_norm_g, w_in, ln_v_g, ln_v_b, w_spatial, b_spatial, sinks, rel_bias, w_out, post_norm_g, loss_target, m_pre_norm_g, m_w_in, m_ln_v_g, m_ln_v_b, m_w_spatial, m_b_spatial, m_sinks, m_rel_bias, m_w_out, m_post_norm_g, v_pre_norm_g, v_w_in, v_ln_v_g, v_ln_v_b, v_w_spatial, v_b_spatial, v_sinks, v_rel_bias, v_w_out, v_post_norm_g):
    weights = dict(pre_norm_g=pre_norm_g, w_in=w_in, ln_v_g=ln_v_g, ln_v_b=ln_v_b, w_spatial=w_spatial, b_spatial=b_spatial,
                   sinks=sinks, rel_bias=rel_bias, w_out=w_out, post_norm_g=post_norm_g)
    mom_m = dict(pre_norm_g=m_pre_norm_g, w_in=m_w_in, ln_v_g=m_ln_v_g, ln_v_b=m_ln_v_b, w_spatial=m_w_spatial,
                 b_spatial=m_b_spatial, sinks=m_sinks, rel_bias=m_rel_bias, w_out=m_w_out, post_norm_g=m_post_norm_g)
    mom_v = dict(pre_norm_g=v_pre_norm_g, w_in=v_w_in, ln_v_g=v_ln_v_g, ln_v_b=v_ln_v_b, w_spatial=v_w_spatial,
                 b_spatial=v_b_spatial, sinks=v_sinks, rel_bias=v_rel_bias, w_out=v_w_out, post_norm_g=v_post_norm_g)
    n_seq, seq_len, _ = x.shape
    n_layers = w_in.shape[0]
    x2 = x.reshape(n_seq * seq_len, D_MODEL)
    target2 = loss_target.reshape(n_seq * seq_len, D_MODEL)
    row = lambda p, l: p[l][None]

    wt_shards = jnp.swapaxes(w_in, 1, 2).astype(BF16)
    wo_shards = w_out.astype(BF16)
    bucket = _window_buckets()
    bias = _bias_table(rel_bias, bucket)
    wcat, wtcat, bs = _spatial_tables(w_spatial, jnp.swapaxes(b_spatial, 1, 2))

    wt, wo = [None] * n_layers, [None] * n_layers
    wt[0], wo[0] = _gather_weights(wt_shards, wo_shards, 0)
    xs, saved = [x2], []
    for l in range(n_layers):
        layer_args = (xs[-1], row(pre_norm_g, l), wt[l], row(ln_v_g, l), row(ln_v_b, l), wcat[l], bs[l], sinks[l], bias,
                      wo[l], row(post_norm_g, l), seq_len)
        if l + 1 < n_layers:
            *acts, xn, y, wt[l + 1], wo[l + 1] = _fwd_layer(*layer_args, gather=(wt_shards, wo_shards, l + 1))
            xs.append(xn)
        else:
            *acts, dx, y, loss = _fwd_layer(*layer_args, target=target2)
        saved.append((y, *acts))

    small = [None] * n_layers
    parts = [None] * n_layers
    waiting = None
    for l in reversed(range(n_layers)):
        hosted = {} if waiting is None else dict(exchange=waiting[:2], allgather=waiting[2])
        dproj, *rest = _bwd_mix(dx, *saved[l], row(ln_v_g, l), row(ln_v_b, l), wcat[l], wtcat[l], bs[l], wo[l],
                                row(post_norm_g, l), **hosted)
        dwo, dws, dmix, dlng, dlnb, dgpost, dbias = rest[:7]
        dws = dws.reshape(HEADS * CHUNK, CHUNK).astype(BF16)
        if waiting is not None:
            parts[l + 1] = list(rest[7:])
        if l > 0:
            dx, dwt, dgpre = _bwd_in(dproj, xs[l], dx, row(pre_norm_g, l), wt[l])
            waiting = (dwt, dwo, dws)
        else:
            dwt, po, ws_all = _bwd_in_dw(dproj, xs[l], row(pre_norm_g, l), [((dwo,), True), ((dws,), False)])
            dx, dgpre, pt = _bwd_in_dx(dproj, xs[l], dx, row(pre_norm_g, l), wt[l], [((dwt,), True)])
            parts[l] = [pt, po, ws_all]
        small[l] = dict(pre_norm_g=dgpre[0], ln_v_g=dlng[0], ln_v_b=dlnb[0], dmix=dmix, dbias=dbias,
                        post_norm_g=dgpost[0])

    db = _b_spatial_grad(jnp.stack([s["dmix"] for s in small]))
    pieces = {n: jnp.stack([s[n] for s in small]) for n in ("pre_norm_g", "ln_v_g", "ln_v_b", "post_norm_g")}
    pieces["b_spatial"] = jnp.swapaxes(db[:, :, :HEADS], 1, 2)
    pieces["rel_bias"], pieces["sinks"] = _rel_bias_grad(jnp.stack([s["dbias"] for s in small]), bucket)
    packed, first_rows = _pack([pieces[n] for n in _SMALL] + [loss])
    total = _allreduce_small(packed)
    grad = {n: _unpack(total, r, weights[n].shape) for n, r in zip(_SMALL, first_rows)}
    loss_out = total[first_rows[-1], 0]

    delta, new_m, new_v = {}, {}, {}
    for n in _SMALL:
        w = weights[n]
        two_d = (-1, w.shape[-1])
        d, nm, nv = _adamw(w.reshape(two_d), grad[n].reshape(two_d), mom_m[n].reshape(two_d), mom_v[n].reshape(two_d), n)
        delta[n], new_m[n], new_v[n] = d.reshape(w.shape), nm.reshape(w.shape), nv.reshape(w.shape)

    t3 = lambda p: jnp.swapaxes(p, 1, 2)
    flat3 = lambda p: p.reshape(n_layers, HEADS * CHUNK, CHUNK)
    for n, which, view, back in (("w_in", 0, t3, t3), ("w_out", 1, lambda p: p, lambda p: p),
                                 ("w_spatial", 2, flat3, lambda p: p.reshape(w_spatial.shape))):
        results = _adamw_parts([parts[l][which] for l in range(n_layers)], view(weights[n]), view(mom_m[n]),
                               view(mom_v[n]), n)
        grad[n], delta[n], new_m[n], new_v[n] = (back(r) for r in results)

    names = tuple(weights)
    return (loss_out, dx.reshape(x.shape), *[grad[n] for n in names], *[delta[n] for n in names],
            *[new_m[n] for n in names], *[new_v[n] for n in names])
```

```python
import math

import jax
import jax.numpy as jnp
from jax import lax
from jax.experimental import pallas as pl
from jax.experimental.pallas import tpu as pltpu

F32 = jnp.float32
BF16 = jnp.bfloat16

D_MODEL = 1024
A_WIDTH = 512
B_WIDTH = 512
KV_WIDTH = 128
IN_WIDTH = 3 * A_WIDTH + 2 * B_WIDTH + 2 * KV_WIDTH
CHUNK = 128
HEADS = 8
HEAD_DIM = 64
REL_BUCKETS = 32
NORM_EPS = 1e-6
NEG = -1e30
Q_OFF = 3 * A_WIDTH
K_OFF = Q_OFF + B_WIDTH
BZ_OFF = K_OFF + 2 * KV_WIDTH
QK_SCALE = HEAD_DIM ** -0.5

ADAM_LR = 0.001
ADAM_B1 = 0.9
ADAM_B2 = 0.999
ADAM_EPS = 1e-08
ADAM_WD = 0.01
ADAM_STEP = 10

TOK_TILE = 512
CHUNK_UNROLL = 4
BWD_CHUNK_UNROLL = 4
ADAMW_BLOCK_ELEMS = 192 * 1024
VMEM_LIMIT_V7X = 60 * 1024 * 1024

N_DEV = 8
N_CHIPS = 4
MESH_ID = pl.DeviceIdType.MESH
ANY = pl.BlockSpec(memory_space=pl.ANY)
SMEM = pl.BlockSpec(memory_space=pltpu.SMEM)


def _cparams(n_axes):
    return pltpu.CompilerParams(dimension_semantics=("arbitrary",) * n_axes, vmem_limit_bytes=VMEM_LIMIT_V7X)


_GELU_C = math.sqrt(2.0 / math.pi)
_GELU_C3 = _GELU_C * 0.044715


def _gelu(x):
    t = jnp.tanh(x * (_GELU_C3 * (x * x) + _GELU_C))
    return x * (0.5 * t + 0.5)


def _gelu_and_grad(x):
    x2 = x * x
    t = jnp.tanh(x * (_GELU_C3 * x2 + _GELU_C))
    cdf = 0.5 * t + 0.5
    d = cdf + (x * (cdf * (1.0 - cdf))) * ((6.0 * _GELU_C3) * x2 + 2.0 * _GELU_C)
    return x * cdf, d


def _sigmoid(x):
    return 0.5 + 0.5 * jnp.tanh(0.5 * x)


def _silu(x):
    return x * _sigmoid(x)


def _silu_and_grad(x):
    s = _sigmoid(x)
    return x * s, s * (1.0 + x * (1.0 - s))


def _dot(a, b):
    return jnp.dot(a, b, preferred_element_type=F32)


def _dot_nt(a, b):
    return lax.dot_general(a, b, (((1,), (1,)), ((), ())), preferred_element_type=F32)


def _dot_tn(a, b):
    return lax.dot_general(a, b, (((0,), (0,)), ((), ())), preferred_element_type=F32)


def _lo_mask(shape):
    return lax.broadcasted_iota(jnp.int32, shape, 1) < HEAD_DIM


def _swap_halves(v):
    return pltpu.roll(v, HEAD_DIM, 1)


def _window_buckets():
    q_loc = jnp.arange(CHUNK)[:, None]
    j_loc = jnp.arange(CHUNK)[None, :]
    d = q_loc - j_loc + jnp.where(j_loc > q_loc, CHUNK, 0)
    max_exact = REL_BUCKETS // 2
    safe = jnp.maximum(d, 1).astype(F32)
    large = max_exact + (jnp.log(safe / max_exact) / math.log(CHUNK / max_exact)
                         * (REL_BUCKETS - max_exact)).astype(jnp.int32)
    large = jnp.minimum(large, REL_BUCKETS - 1)
    return jnp.where(d < max_exact, d, large).astype(jnp.int32)


def _bias_table(rel_bias, bucket):
    def body(rb_ref, bk_ref, out_ref):
        bk = bk_ref[...]
        for h in range(HEADS):
            acc = jnp.zeros(bk.shape, F32)
            for b in range(REL_BUCKETS):
                acc = jnp.where(bk == b, rb_ref[b, h], acc)
            out_ref[h] = acc

    vmem = pl.BlockSpec(memory_space=pltpu.VMEM)
    return pl.pallas_call(
        body, name="bias_table",
        out_shape=jax.ShapeDtypeStruct((HEADS, CHUNK, CHUNK), F32),
        in_specs=[SMEM, vmem], out_specs=vmem,
    )(rel_bias, bucket)


def _rel_bias_grad(dbias, bucket):
    n_layers = dbias.shape[0]

    def body(db_ref, bk_ref, out_ref, dsink_ref):
        bk = bk_ref[...]
        for h in range(HEADS):
            tot = jnp.zeros((CHUNK, CHUNK), F32)
            for l in range(n_layers):
                ds = db_ref[l, h]
                dsink_ref[l, h] = -jnp.sum(ds)
                tot = tot + ds
            for b in range(REL_BUCKETS):
                out_ref[b, h] = jnp.sum(jnp.where(bk == b, tot, 0.0))

    vmem = pl.BlockSpec(memory_space=pltpu.VMEM)
    return pl.pallas_call(
        body, name="rel_bias_grad",
        out_shape=(jax.ShapeDtypeStruct((REL_BUCKETS, HEADS), F32), jax.ShapeDtypeStruct((n_layers, HEADS), F32)),
        in_specs=[vmem] * 2, out_specs=(SMEM, SMEM),
    )(dbias, bucket)


def _spatial_tables(w_spatial, b_spatial_t):
    n_layers = w_spatial.shape[0]

    def body(w_ref, b_ref, wcat_ref, wtcat_ref, bs_ref):
        row = lax.broadcasted_iota(jnp.int32, (CHUNK, CHUNK), 0)
        col = lax.broadcasted_iota(jnp.int32, (CHUNK, CHUNK), 1)
        causal = col <= row
        lo = _lo_mask((CHUNK, CHUNK))
        for p in range(4):
            for half in range(2):
                w = jnp.where(causal, w_ref[0, 2 * p + half], 0.0)
                wcat_ref[0, p, :, half * CHUNK:(half + 1) * CHUNK] = w.astype(BF16)
                wtcat_ref[0, p, :, half * CHUNK:(half + 1) * CHUNK] = w.T.astype(BF16)
            b = b_ref[0]
            bs_ref[0, p] = jnp.where(lo, b[:, 2 * p:2 * p + 1], b[:, 2 * p + 1:2 * p + 2])

    return pl.pallas_call(
        body, name="spatial_tables", grid=(n_layers,),
        out_shape=(jax.ShapeDtypeStruct((n_layers, 4, CHUNK, 2 * CHUNK), BF16),
                   jax.ShapeDtypeStruct((n_layers, 4, CHUNK, 2 * CHUNK), BF16),
                   jax.ShapeDtypeStruct((n_layers, 4, CHUNK, CHUNK), F32)),
        in_specs=[pl.BlockSpec((1, HEADS, CHUNK, CHUNK), lambda l: (l, 0, 0, 0)),
                  pl.BlockSpec((1, CHUNK, HEADS), lambda l: (l, 0, 0))],
        out_specs=(pl.BlockSpec((1, 4, CHUNK, 2 * CHUNK), lambda l: (l, 0, 0, 0)),
                   pl.BlockSpec((1, 4, CHUNK, 2 * CHUNK), lambda l: (l, 0, 0, 0)),
                   pl.BlockSpec((1, 4, CHUNK, CHUNK), lambda l: (l, 0, 0, 0))),
        compiler_params=_cparams(1),
    )(w_spatial, b_spatial_t)


def _b_spatial_grad(dmix):
    n_layers = dmix.shape[0]

    def body(d_ref, out_ref):
        lane = lax.broadcasted_iota(jnp.int32, (CHUNK, CHUNK), 1)
        acc = jnp.zeros((CHUNK, CHUNK), F32)
        for p in range(4):
            t = d_ref[0, :, p * CHUNK:(p + 1) * CHUNK]
            s_lo = jnp.sum(jnp.where(lane < HEAD_DIM, t, 0.0), axis=1, keepdims=True)
            s_hi = jnp.sum(jnp.where(lane < HEAD_DIM, 0.0, t), axis=1, keepdims=True)
            acc = jnp.where(lane == 2 * p, s_lo, acc)
            acc = jnp.where(lane == 2 * p + 1, s_hi, acc)
        out_ref[0] = acc

    return pl.pallas_call(
        body, name="b_spatial_grad", grid=(n_layers,),
        out_shape=jax.ShapeDtypeStruct((n_layers, CHUNK, CHUNK), F32),
        in_specs=[pl.BlockSpec((1, CHUNK, A_WIDTH), lambda l: (l, 0, 0))],
        out_specs=pl.BlockSpec((1, CHUNK, CHUNK), lambda l: (l, 0, 0)),
        compiler_params=_cparams(1),
    )(dmix)


def _place():
    x, y, c = lax.axis_index("x"), lax.axis_index("y"), lax.axis_index("c")
    other_chips = [(1 - x, y), (x, 1 - y), (1 - x, 1 - y)]
    return x, y, c, other_chips


N_GATHER_SEMS = 12


def _gather_phases(shards, fulls, send_sems, recv_sems, local_sems):
    x, y, c, chips = _place()
    sibling = (x, y, 1 - c)
    n_arr = len(shards)

    def half_rows(a, chip, half):
        n = shards[a].shape[0]
        start = (2 * chip[0] + chip[1]) * n + half * (n // 2)
        return fulls[a].at[pl.ds(pl.multiple_of(start, 16), n // 2), :]

    def my_half(a):
        n = shards[a].shape[0]
        return shards[a].at[pl.ds(pl.multiple_of(c * (n // 2), 16), n // 2), :]

    def copy(k, a, src, chip, half, to):
        return pltpu.make_async_remote_copy(
            src_ref=src, dst_ref=half_rows(a, chip, half), send_sem=send_sems.at[n_arr * k + a],
            recv_sem=recv_sems.at[n_arr * k + a], device_id=to, device_id_type=MESH_ID)

    def local(a):
        n = shards[a].shape[0]
        mine = fulls[a].at[pl.ds(pl.multiple_of((2 * x + y) * n, 16), n), :]
        return pltpu.make_async_copy(shards[a], mine, local_sems.at[a])

    def first(k, a):
        return copy(k, a, my_half(a), (x, y), c, (chips[k][0], chips[k][1], c))

    def passed(k, a):
        return copy(3 + k, a, half_rows(a, chips[k], c), chips[k], c, sibling)

    def phase_a():
        for a in range(n_arr):
            local(a).start()
        for k in range(3):
            for a in range(n_arr):
                first(k, a).start()

    def phase_b():
        for k in range(3):
            for a in range(n_arr):
                copy(k, a, my_half(a), chips[k], c, sibling).wait_recv()
                passed(k, a).start()

    def phase_c():
        for k in range(3):
            for a in range(n_arr):
                copy(3 + k, a, my_half(a), chips[k], 1 - c, sibling).wait_recv()
        for k in range(3):
            for a in range(n_arr):
                first(k, a).wait_send()
                passed(k, a).wait_send()
        for a in range(n_arr):
            local(a).wait()

    return phase_a, phase_b, phase_c


N_EXCHANGE_SEMS = 14


def _exchange_phases(partials, parts, send_sems, recv_sems, local_sems, by_chip=True):
    x, y, c, chips = _place()
    me, sibling = (x, y, c), (x, y, 1 - c)
    n_arr = len(partials)

    def block(a, chip):
        if not by_chip:
            return partials[a]
        n = partials[a].shape[0] // N_CHIPS
        return partials[a].at[pl.ds(pl.multiple_of((2 * chip[0] + chip[1]) * n, 16), n), :]

    def slot(a, dev):
        return parts[a].at[4 * dev[0] + 2 * dev[1] + dev[2]]

    def copy(k, a, src, origin, to):
        return pltpu.make_async_remote_copy(
            src_ref=src, dst_ref=slot(a, origin), send_sem=send_sems.at[n_arr * k + a],
            recv_sem=recv_sems.at[n_arr * k + a], device_id=to, device_id_type=MESH_ID)

    def local(a):
        return pltpu.make_async_copy(block(a, (x, y)), slot(a, me), local_sems.at[a])

    def first(k, a):
        if k == 0:
            return copy(0, a, block(a, (x, y)), me, sibling)
        chip = chips[k - 1]
        return copy(k, a, block(a, chip), me, (chip[0], chip[1], c))

    def passed(k, a):
        origin = (chips[k][0], chips[k][1], c)
        return copy(4 + k, a, slot(a, origin), origin, sibling)

    def phase_a():
        for a in range(n_arr):
            local(a).start()
        for k in range(4):
            for a in range(n_arr):
                first(k, a).start()

    def phase_b():
        for k in range(3):
            for a in range(n_arr):
                copy(1 + k, a, block(a, (x, y)), (chips[k][0], chips[k][1], c), me).wait_recv()
                passed(k, a).start()

    def phase_c():
        for a in range(n_arr):
            copy(0, a, block(a, (x, y)), sibling, me).wait_recv()
        for k in range(3):
            for a in range(n_arr):
                copy(4 + k, a, block(a, (x, y)), (chips[k][0], chips[k][1], 1 - c), me).wait_recv()
        for k in range(4):
            for a in range(n_arr):
                first(k, a).wait_send()
        for k in range(3):
            for a in range(n_arr):
                passed(k, a).wait_send()
        for a in range(n_arr):
            local(a).wait()

    return phase_a, phase_b, phase_c


def _comm_scratch(n_sems):
    return [pltpu.SemaphoreType.DMA((n_sems,)), pltpu.SemaphoreType.DMA((n_sems,)), pltpu.SemaphoreType.DMA((2,))]


def _gather_weights(wt_shards, wo_shards, layer):
    wt_rows, wo_rows = wt_shards.shape[1], wo_shards.shape[1]

    def body(wt_ref, wo_ref, wt_full, wo_full, send_sems, recv_sems, local_sems):
        phases = _gather_phases((wt_ref.at[layer], wo_ref.at[layer]), (wt_full, wo_full), send_sems, recv_sems, local_sems)
        for phase in phases:
            phase()

    return pl.pallas_call(
        body, name="gather_weights",
        out_shape=(jax.ShapeDtypeStruct((N_CHIPS * wt_rows, D_MODEL), BF16),
                   jax.ShapeDtypeStruct((N_CHIPS * wo_rows, D_MODEL), BF16)),
        in_specs=[ANY, ANY], out_specs=(ANY, ANY), scratch_shapes=_comm_scratch(N_GATHER_SEMS),
    )(wt_shards, wo_shards)


def _allreduce_small(part):
    n_rows = part.shape[0]

    def body(p_ref, tot_ref, all_ref, send_sems, recv_sems, local_sem):
        x, y, c, chips = _place()
        me, sibling = (x, y, c), (x, y, 1 - c)

        def rows(dev):
            return all_ref.at[pl.ds(pl.multiple_of((4 * dev[0] + 2 * dev[1] + dev[2]) * n_rows, 8), n_rows), :]

        def copy(k, origin, to, src=None):
            return pltpu.make_async_remote_copy(
                src_ref=rows(origin) if src is None else src, dst_ref=rows(origin), send_sem=send_sems.at[k],
                recv_sem=recv_sems.at[k], device_id=to, device_id_type=MESH_ID)

        mine = pltpu.make_async_copy(p_ref, rows(me), local_sem)
        mine.start()
        first = [copy(0, me, sibling, src=p_ref)]
        first += [copy(1 + k, me, (chip[0], chip[1], c), src=p_ref) for k, chip in enumerate(chips)]
        for cp in first:
            cp.start()
        passed = []
        for k, chip in enumerate(chips):
            origin = (chip[0], chip[1], c)
            copy(1 + k, origin, me).wait_recv()
            fwd = copy(4 + k, origin, sibling)
            fwd.start()
            passed.append(fwd)
        copy(0, sibling, me).wait_recv()
        for k, chip in enumerate(chips):
            copy(4 + k, (chip[0], chip[1], 1 - c), me).wait_recv()
        for cp in first + passed:
            cp.wait_send()
        mine.wait()
        tot = all_ref[0:n_rows, :]
        for d in range(1, N_DEV):
            tot = tot + all_ref[d * n_rows:(d + 1) * n_rows, :]
        tot_ref[...] = tot

    vmem = pl.BlockSpec(memory_space=pltpu.VMEM)
    return pl.pallas_call(
        body, name="allreduce_small",
        out_shape=jax.ShapeDtypeStruct((n_rows, 128), F32),
        in_specs=[vmem], out_specs=vmem,
        scratch_shapes=[pltpu.VMEM((N_DEV * n_rows, 128), F32), pltpu.SemaphoreType.DMA((7,)),
                        pltpu.SemaphoreType.DMA((7,)), pltpu.SemaphoreType.DMA],
        compiler_params=pltpu.CompilerParams(vmem_limit_bytes=VMEM_LIMIT_V7X),
    )(part)


def _hosted(phases, step, n_steps, pass_on_step=None):
    phase_a, phase_b, phase_c = phases
    if pass_on_step is None:
        pass_on_step = (3 * n_steps) // 4

    def at_start():
        pl.when(step == 0)(phase_a)

    def at_end():
        pl.when(step == pass_on_step)(phase_b)
        pl.when(step == n_steps - 1)(phase_c)

    return at_start, at_end


def _layer_norm_stats(vv):
    mu = jnp.mean(vv, axis=-1, keepdims=True)
    xc = vv - mu
    rs = lax.rsqrt(jnp.mean(xc * xc, axis=-1, keepdims=True) + NORM_EPS)
    return xc * rs, rs


def _blockdiag(v, lo):
    zero = jnp.zeros_like(v)
    return jnp.concatenate([jnp.where(lo, v, zero), jnp.where(lo, zero, v)], axis=0)


def _softmax_sink(s, sink):
    m = jnp.maximum(jnp.max(s, axis=-1, keepdims=True), sink)
    e = jnp.exp(s - m)
    esink = jnp.exp(sink - m)
    den = jnp.sum(e, axis=-1, keepdims=True) + esink
    return e / den


def _kv_rows(cur_ref, halo_ref, r0, c):
    prev_in_tile = cur_ref[pl.ds(pl.multiple_of(jnp.maximum(r0 - CHUNK, 0), CHUNK), CHUNK), :]
    prev = jnp.where(c == 0, halo_ref[...], prev_in_tile)
    kv2 = jnp.concatenate([prev, cur_ref[pl.ds(r0, CHUNK), :]], axis=0)
    k2, v2 = kv2[:, 0:KV_WIDTH], kv2[:, KV_WIDTH:2 * KV_WIDTH]
    return (k2, _swap_halves(k2)), (v2, _swap_halves(v2))


def _window_square(over_keys, prev):
    return jnp.where(prev, over_keys[:, 0:CHUNK], over_keys[:, CHUNK:2 * CHUNK])


def _window_keys(square, prev):
    zero = jnp.zeros_like(square)
    return jnp.concatenate([jnp.where(prev, square, zero), jnp.where(prev, zero, square)], axis=1)


def _ahead(shape=(CHUNK, CHUNK)):
    return lax.broadcasted_iota(jnp.int32, shape, 1) - lax.broadcasted_iota(jnp.int32, shape, 0)


def _dead_mask(first, ahead):
    return ahead > jnp.where(first, 0, CHUNK)


def _head_of(p, half):
    return 2 * p + half, int(half != p // 2)


_HEADS_BY_COPY = tuple(tuple((p, half) for p in range(4) for half in range(2) if _head_of(p, half)[1] == sw)
                       for sw in range(2))


def _masked_halves(tile, lo):
    zero = jnp.zeros_like(tile)
    return {0: jnp.where(lo, tile, zero), 1: jnp.where(lo, zero, tile)}


def _query_stacks(q_tiles, lo):
    qm = {p: _masked_halves(q_tiles[p] * QK_SCALE, lo) for p in range(4)}
    return [jnp.concatenate([qm[p][half] for p, half in members], axis=0) for members in _HEADS_BY_COPY]


def _attention_probs(q_stacks, ks, bias_ref, sink_ref, dead, prev):
    probs = {}
    for sw, members in enumerate(_HEADS_BY_COPY):
        s_stack = _dot_nt(q_stacks[sw], ks[sw])
        for i, (p, half) in enumerate(members):
            head = 2 * p + half
            s = _window_square(s_stack[i * CHUNK:(i + 1) * CHUNK, :], prev) + bias_ref[head]
            s = jnp.where(dead, NEG, s)
            probs[p, half] = _softmax_sink(s, sink_ref[head]).astype(BF16)
    return probs


def _attention_values(probs, vs, prev):
    outs, p_stacks = {}, []
    for sw, members in enumerate(_HEADS_BY_COPY):
        p_stack = jnp.concatenate([_window_keys(probs[m], prev) for m in members], axis=0)
        r_stack = _dot(p_stack, vs[sw])
        p_stacks.append(p_stack)
        for i, m in enumerate(members):
            outs[m] = r_stack[i * CHUNK:(i + 1) * CHUNK, :]
    return outs, p_stacks


def _fwd_layer(x, g_pre, w_in_t, ln_g, ln_b, wcat, bs, sinks, bias, w_out, g_post, seq_len, gather=None, target=None):
    n_tok = x.shape[0]
    tb = TOK_TILE
    n_chunks = tb // CHUNK
    n_tiles = n_tok // tb
    n_in = 11
    assert gather is None or target is None

    def body(*refs):
        (x_ref, gpre_ref, wt_ref, lng_ref, lnb_ref, wcat_ref, bs_ref, sink_ref, bias_ref, wout_ref,
         gpost_ref) = refs[:n_in]
        at_start = at_end = lambda: None
        if gather is not None:
            (wts_ref, wos_ref, a_ref, q_ref, kv_ref, bz_ref, probs_ref, xn_ref, y_ref, wt_full, wo_full, ycat_s, halo_ref,
             send_sems, recv_sems, local_sems) = refs[n_in:]
            phases = _gather_phases((wts_ref.at[gather[2]], wos_ref.at[gather[2]]), (wt_full, wo_full), send_sems,
                                    recv_sems, local_sems)
            at_start, at_end = _hosted(phases, pl.program_id(0), n_tiles)
        elif target is not None:
            target_ref, a_ref, q_ref, kv_ref, bz_ref, probs_ref, xn_ref, y_ref, loss_ref, ycat_s, halo_ref = refs[n_in:]
        else:
            a_ref, q_ref, kv_ref, bz_ref, probs_ref, xn_ref, y_ref, ycat_s, halo_ref = refs[n_in:]
        at_start()
        i = pl.program_id(0)
        lo = _lo_mask((CHUNK, CHUNK))
        ahead = _ahead()
        prev = ahead > 0

        @pl.when(i == 0)
        def _():
            halo_ref[...] = jnp.zeros_like(halo_ref)

        xf = x_ref[...]
        r1 = lax.rsqrt(jnp.mean(xf * xf, axis=-1, keepdims=True) + NORM_EPS)
        h = ((xf * r1) * gpre_ref[...]).astype(BF16)
        a_ref[...] = _dot_nt(h, wt_ref[0:Q_OFF, :])
        q_ref[...] = _dot_nt(h, wt_ref[Q_OFF:K_OFF, :]).astype(BF16)
        kv_ref[...] = _dot_nt(h, wt_ref[K_OFF:BZ_OFF, :]).astype(BF16)
        bz_ref[...] = _dot_nt(h, wt_ref[BZ_OFF:IN_WIDTH, :])

        def chunk(c, carry):
            r0 = pl.multiple_of(c * CHUNK, CHUNK)
            rows = pl.ds(r0, CHUNK)
            u = _gelu(a_ref[rows, 0:A_WIDTH])
            vv = _gelu(a_ref[rows, A_WIDTH:2 * A_WIDTH])
            xhat, _ = _layer_norm_stats(vv)
            vnb = (xhat * lng_ref[...] + lnb_ref[...]).astype(BF16)
            for p in range(4):
                blk = slice(p * CHUNK, (p + 1) * CHUNK)
                mixed = _dot(wcat_ref[p], _blockdiag(vnb[:, blk], lo)) + bs_ref[p]
                sz = _silu(a_ref[rows, 2 * A_WIDTH + p * CHUNK:2 * A_WIDTH + (p + 1) * CHUNK])
                ycat_s[rows, blk] = ((u[:, blk] * mixed) * sz).astype(BF16)
            ks, vs = _kv_rows(kv_ref, halo_ref, r0, c)
            dead = _dead_mask(lax.rem(i * tb + r0, seq_len) == 0, ahead)
            q_tiles = [q_ref[rows, p * CHUNK:(p + 1) * CHUNK] for p in range(4)]
            probs = _attention_probs(_query_stacks(q_tiles, lo), ks, bias_ref, sink_ref, dead, prev)
            for (p, half), prob in probs.items():
                head = 2 * p + half
                probs_ref[rows, head * CHUNK:(head + 1) * CHUNK] = prob
            outs, _ = _attention_values(probs, vs, prev)
            for p in range(4):
                blk = slice(p * CHUNK, (p + 1) * CHUNK)
                o = jnp.where(lo, outs[p, 0], outs[p, 1])
                ycat_s[rows, B_WIDTH + p * CHUNK:B_WIDTH + (p + 1) * CHUNK] = (o * _silu(bz_ref[rows, blk])).astype(BF16)
            return carry

        lax.fori_loop(0, n_chunks, chunk, 0, unroll=CHUNK_UNROLL)
        halo_ref[...] = kv_ref[tb - CHUNK:tb, :]
        y = _dot(ycat_s[...], wout_ref[...])
        r = lax.rsqrt(jnp.mean(y * y, axis=-1, keepdims=True) + NORM_EPS)
        y_ref[...] = y
        xn = x_ref[...] + (y * r) * gpost_ref[...]
        if target is None:
            xn_ref[...] = xn
        else:
            d = xn - target_ref[...]
            xn_ref[...] = d * (1.0 / D_MODEL)

            @pl.when(i == 0)
            def _():
                loss_ref[0, 0] = 0.0

            loss_ref[0, 0] += 0.5 * jnp.sum(jnp.mean(d * d, axis=-1, keepdims=True))
        at_end()

    tile = lambda w: pl.BlockSpec((tb, w), lambda i: (i, 0))
    whole = lambda shape, **kw: pl.BlockSpec(shape, lambda i: (0,) * len(shape), **kw)
    in_specs = [tile(D_MODEL), whole((1, D_MODEL)), whole((IN_WIDTH, D_MODEL), pipeline_mode=pl.Buffered(1)),
                whole((1, A_WIDTH)), whole((1, A_WIDTH)), whole((4, CHUNK, 2 * CHUNK)), whole((4, CHUNK, CHUNK)), SMEM,
                whole((HEADS, CHUNK, CHUNK)), whole((D_MODEL, D_MODEL)), whole((1, D_MODEL))]
    out_shape = [jax.ShapeDtypeStruct((n_tok, Q_OFF), F32), jax.ShapeDtypeStruct((n_tok, B_WIDTH), BF16),
                 jax.ShapeDtypeStruct((n_tok, 2 * KV_WIDTH), BF16), jax.ShapeDtypeStruct((n_tok, B_WIDTH), F32),
                 jax.ShapeDtypeStruct((n_tok, HEADS * CHUNK), BF16), jax.ShapeDtypeStruct((n_tok, D_MODEL), F32),
                 jax.ShapeDtypeStruct((n_tok, D_MODEL), F32)]
    out_specs = [tile(Q_OFF), tile(B_WIDTH), tile(2 * KV_WIDTH), tile(B_WIDTH), tile(HEADS * CHUNK), tile(D_MODEL),
                 tile(D_MODEL)]
    scratch = [pltpu.VMEM((tb, D_MODEL), BF16), pltpu.VMEM((CHUNK, 2 * KV_WIDTH), BF16)]
    args = [x, g_pre, w_in_t, ln_g, ln_b, wcat, bs, sinks, bias, w_out, g_post]
    if gather is not None:
        wt_shards, wo_shards, _ = gather
        in_specs += [ANY, ANY]
        args += [wt_shards, wo_shards]
        out_shape += [jax.ShapeDtypeStruct((N_CHIPS * wt_shards.shape[1], D_MODEL), BF16),
                      jax.ShapeDtypeStruct((N_CHIPS * wo_shards.shape[1], D_MODEL), BF16)]
        out_specs += [ANY, ANY]
        scratch += _comm_scratch(N_GATHER_SEMS)
    if target is not None:
        in_specs.append(tile(D_MODEL))
        args.append(target)
        out_shape.append(jax.ShapeDtypeStruct((1, 1), F32))
        out_specs.append(SMEM)
    name = "fwd_layer" + ("" if gather is None else "_gather") + ("" if target is None else "_loss")
    return pl.pallas_call(
        body, name=name, grid=(n_tiles,),
        out_shape=tuple(out_shape), in_specs=in_specs, out_specs=tuple(out_specs), scratch_shapes=scratch,
        compiler_params=_cparams(1),
    )(*args)


def _bwd_mix(dout, y, a, q, kv, bz, probs, ln_g, ln_b, wcat, wtcat, bs, w_out, g_post, exchange=None, allgather=None):
    n_tok = y.shape[0]
    tb = TOK_TILE
    n_chunks = tb // CHUNK
    n_tiles = n_tok // tb

    def body(*refs):
        refs = list(refs)
        take = lambda n: [refs.pop(0) for _ in range(n)]
        (dout_ref, y_ref, a_ref, q_ref, kv_ref, halo_ref, bz_ref, probs_ref, lng_ref, lnb_ref, wcat_ref, wtcat_ref, bs_ref,
         wout_ref, gpost_ref) = take(15)
        if exchange is not None:
            dwt_in, dwo_in = take(2)
        if allgather is not None:
            (small_in,) = take(1)
        dproj_ref, dwout_ref, dws_ref, dmix_ref, dlng_ref, dlnb_ref, dgpost_ref, dbias_ref = take(8)
        if exchange is not None:
            pt_ref, po_ref = take(2)
        if allgather is not None:
            (small_all,) = take(1)
        ycat_s, dy_s, dyc_s, dkv_s, carry_s, dwout_s, stage_s = take(7)
        starts, ends = [], []
        if exchange is not None:
            hosted = _hosted(_exchange_phases((dwt_in, dwo_in), (pt_ref, po_ref), *take(3)), pl.program_id(0), n_tiles)
            starts.append(hosted[0])
            ends.append(hosted[1])
        if allgather is not None:
            hosted = _hosted(_exchange_phases((small_in,), (small_all,), *take(3), by_chip=False), pl.program_id(0),
                             n_tiles)
            starts.append(hosted[0])
            ends.append(hosted[1])
        at_start = lambda: [f() for f in starts]
        at_end = lambda: [f() for f in ends]
        at_start()
        step = pl.program_id(0)
        lo = _lo_mask((CHUNK, CHUNK))
        prev = _ahead() > 0

        @pl.when(step == 0)
        def _():
            dwout_s[...] = jnp.zeros_like(dwout_s)
            dws_ref[...] = jnp.zeros_like(dws_ref)
            dmix_ref[...] = jnp.zeros_like(dmix_ref)
            dlng_ref[...] = jnp.zeros_like(dlng_ref)
            dlnb_ref[...] = jnp.zeros_like(dlnb_ref)
            dgpost_ref[...] = jnp.zeros_like(dgpost_ref)
            dbias_ref[...] = jnp.zeros_like(dbias_ref)
            carry_s[...] = jnp.zeros_like(carry_s)

        for r0 in range(0, tb, CHUNK):
            rows = slice(r0, r0 + CHUNK)
            yv = y_ref[rows, :]
            dout = dout_ref[rows, :]
            r = lax.rsqrt(jnp.mean(yv * yv, axis=-1, keepdims=True) + NORM_EPS)
            yn = yv * r
            dgpost_ref[...] += jnp.sum(dout * yn, axis=0, keepdims=True)
            dyn = dout * gpost_ref[...]
            dy_s[rows, :] = (r * (dyn - yn * jnp.mean(dyn * yn, axis=-1, keepdims=True))).astype(BF16)
        dyc_s[...] = _dot_nt(dy_s[...], wout_ref[...])
        dkv_s[0:tb, :] = jnp.zeros((tb, 2 * KV_WIDTH), F32)
        dkv_s[tb:tb + CHUNK, :] = carry_s[...]

        def chunk(c, carry):
            r0 = pl.multiple_of(c * CHUNK, CHUNK)
            rows = pl.ds(r0, CHUNK)
            u, gu = _gelu_and_grad(a_ref[rows, 0:A_WIDTH])
            vv, gv = _gelu_and_grad(a_ref[rows, A_WIDTH:2 * A_WIDTH])
            xhat, rs = _layer_norm_stats(vv)
            vnb = (xhat * lng_ref[...] + lnb_ref[...]).astype(BF16)
            d_vn, d_u, d_az = [], [], []
            for p in range(4):
                blk = slice(p * CHUNK, (p + 1) * CHUNK)
                vbd = _blockdiag(vnb[:, blk], lo)
                mixed = _dot(wcat_ref[p], vbd) + bs_ref[p]
                sz, gz = _silu_and_grad(a_ref[rows, 2 * A_WIDTH + p * CHUNK:2 * A_WIDTH + (p + 1) * CHUNK])
                ub = u[:, blk]
                dya = dyc_s[rows, blk]
                um = ub * mixed
                ycat_s[rows, blk] = (um * sz).astype(BF16)
                d_mixed = (dya * ub) * sz
                d_u.append((dya * mixed) * sz)
                d_az.append((dya * um) * gz)
                dmix_ref[:, blk] += d_mixed
                dmbd = _blockdiag(d_mixed.astype(BF16), lo)
                d_vn.append(_dot(wtcat_ref[p], dmbd))
                dws_ref[p] += _dot_nt(dmbd, vnb[:, blk])
            d_vn = jnp.concatenate(d_vn, axis=1)
            dlng_ref[...] += jnp.sum(d_vn * xhat, axis=0, keepdims=True)
            dlnb_ref[...] += jnp.sum(d_vn, axis=0, keepdims=True)
            dxh = d_vn * lng_ref[...]
            d_vv = rs * (dxh - jnp.mean(dxh, axis=-1, keepdims=True)
                         - xhat * jnp.mean(dxh * xhat, axis=-1, keepdims=True))
            dproj_ref[rows, 0:A_WIDTH] = (jnp.concatenate(d_u, axis=1) * gu).astype(BF16)
            dproj_ref[rows, A_WIDTH:2 * A_WIDTH] = (d_vv * gv).astype(BF16)
            dproj_ref[rows, 2 * A_WIDTH:Q_OFF] = jnp.concatenate(d_az, axis=1).astype(BF16)
            ks, vs = _kv_rows(kv_ref, halo_ref, r0, c)
            q_stacks = _query_stacks([q_ref[rows, p * CHUNK:(p + 1) * CHUNK] for p in range(4)], lo)
            probs = {(p, half): probs_ref[rows, (2 * p + half) * CHUNK:(2 * p + half + 1) * CHUNK]
                     for p in range(4) for half in range(2)}
            outs_h, p_stacks = _attention_values(probs, vs, prev)
            dom = {}
            for p in range(4):
                blk = slice(p * CHUNK, (p + 1) * CHUNK)
                sb, gb = _silu_and_grad(bz_ref[rows, blk])
                dyb = dyc_s[rows, B_WIDTH + p * CHUNK:B_WIDTH + (p + 1) * CHUNK]
                o = jnp.where(lo, outs_h[p, 0], outs_h[p, 1])
                ycat_s[rows, B_WIDTH + p * CHUNK:B_WIDTH + (p + 1) * CHUNK] = (o * sb).astype(BF16)
                dproj_ref[rows, BZ_OFF + p * CHUNK:BZ_OFF + (p + 1) * CHUNK] = ((dyb * o) * gb).astype(BF16)
                dom[p] = _masked_halves((dyb * sb).astype(BF16), lo)
            dqs, dk_by_copy, dv_by_copy = {}, [], []
            for sw, members in enumerate(_HEADS_BY_COPY):
                do_stack = jnp.concatenate([dom[p][half] for p, half in members], axis=0)
                dp_stack = _dot_nt(do_stack, vs[sw])
                dsws = []
                for i, (p, half) in enumerate(members):
                    head = 2 * p + half
                    dp = _window_square(dp_stack[i * CHUNK:(i + 1) * CHUNK, :], prev)
                    prob = probs[p, half].astype(F32)
                    delta = jnp.sum(prob * dp, axis=-1, keepdims=True)
                    ds = prob * (dp - delta)
                    dbias_ref[head] += ds
                    dsws.append(_window_keys(ds.astype(BF16), prev))
                ds_stack = jnp.concatenate(dsws, axis=0)
                dq_stack = _dot(ds_stack, ks[sw])
                for i, m in enumerate(members):
                    dqs[m] = dq_stack[i * CHUNK:(i + 1) * CHUNK, :]
                dk_by_copy.append(_dot_tn(ds_stack, q_stacks[sw]))
                dv_by_copy.append(_dot_tn(p_stacks[sw], do_stack))
            for p in range(4):
                dq = jnp.where(lo, dqs[p, 0], dqs[p, 1]) * QK_SCALE
                dproj_ref[rows, Q_OFF + p * CHUNK:Q_OFF + (p + 1) * CHUNK] = dq.astype(BF16)
            both = pl.ds(r0, 2 * CHUNK)
            dkv_s[both, 0:KV_WIDTH] += dk_by_copy[0] + _swap_halves(dk_by_copy[1])
            dkv_s[both, KV_WIDTH:2 * KV_WIDTH] += dv_by_copy[0] + _swap_halves(dv_by_copy[1])
            return carry

        lax.fori_loop(0, n_chunks, chunk, 0, unroll=BWD_CHUNK_UNROLL)
        dwout_s[...] += _dot_tn(ycat_s[...], dy_s[...])
        dproj_ref[:, K_OFF:BZ_OFF] = dkv_s[CHUNK:CHUNK + tb, :].astype(BF16)
        carry_s[...] = dkv_s[0:CHUNK, :]

        @pl.when(step == n_tiles - 1)
        def _():
            row = lax.broadcasted_iota(jnp.int32, (2 * CHUNK, CHUNK), 0)
            col = lax.broadcasted_iota(jnp.int32, (2 * CHUNK, CHUNK), 1)
            causal = col <= jnp.where(row >= CHUNK, row - CHUNK, row)
            for p in range(4):
                dws_ref[p] = jnp.where(causal, dws_ref[p], 0.0)
            rows = stage_s.shape[0]
            for r0 in range(0, D_MODEL, rows):
                stage_s[...] = dwout_s[r0:r0 + rows, :].astype(BF16)
                pltpu.sync_copy(stage_s, dwout_ref.at[r0:r0 + rows, :])

        at_end()

    tile = lambda w: pl.BlockSpec((tb, w), lambda s: (n_tiles - 1 - s, 0))
    whole = lambda shape, **kw: pl.BlockSpec(shape, lambda s: (0,) * len(shape), **kw)
    once = dict(pipeline_mode=pl.Buffered(1))
    in_specs = [tile(D_MODEL), tile(D_MODEL), tile(Q_OFF), tile(B_WIDTH), tile(2 * KV_WIDTH),
                pl.BlockSpec((CHUNK, 2 * KV_WIDTH), lambda s: (jnp.maximum((n_tiles - 1 - s) * n_chunks - 1, 0), 0)),
                tile(B_WIDTH), tile(HEADS * CHUNK), whole((1, A_WIDTH)), whole((1, A_WIDTH)), whole((4, CHUNK, 2 * CHUNK)),
                whole((4, CHUNK, 2 * CHUNK)), whole((4, CHUNK, CHUNK)), whole((D_MODEL, D_MODEL), **once),
                whole((1, D_MODEL))]
    args = [dout, y, a, q, kv, kv, bz, probs, ln_g, ln_b, wcat, wtcat, bs, w_out, g_post]
    small_shapes = [jax.ShapeDtypeStruct((D_MODEL, D_MODEL), BF16), jax.ShapeDtypeStruct((4, 2 * CHUNK, CHUNK), F32),
                    jax.ShapeDtypeStruct((CHUNK, A_WIDTH), F32), jax.ShapeDtypeStruct((1, A_WIDTH), F32),
                    jax.ShapeDtypeStruct((1, A_WIDTH), F32), jax.ShapeDtypeStruct((1, D_MODEL), F32),
                    jax.ShapeDtypeStruct((HEADS, CHUNK, CHUNK), F32)]
    small_specs = [ANY, whole((4, 2 * CHUNK, CHUNK)), whole((CHUNK, A_WIDTH)), whole((1, A_WIDTH)), whole((1, A_WIDTH)),
                   whole((1, D_MODEL)), whole((HEADS, CHUNK, CHUNK))]
    scratch = [pltpu.VMEM((tb, D_MODEL), BF16), pltpu.VMEM((tb, D_MODEL), BF16), pltpu.VMEM((tb, D_MODEL), F32),
               pltpu.VMEM((tb + CHUNK, 2 * KV_WIDTH), F32), pltpu.VMEM((CHUNK, 2 * KV_WIDTH), F32),
               pltpu.VMEM((D_MODEL, D_MODEL), F32), pltpu.VMEM((2 * CHUNK, D_MODEL), BF16)]
    out_shape = [jax.ShapeDtypeStruct((n_tok, IN_WIDTH), BF16)] + small_shapes
    out_specs = [tile(IN_WIDTH)] + small_specs
    if exchange is not None:
        dwt, dwo = exchange
        in_specs += [ANY, ANY]
        args += [dwt, dwo]
        out_shape += [jax.ShapeDtypeStruct((N_DEV, dwt.shape[0] // N_CHIPS, D_MODEL), BF16),
                      jax.ShapeDtypeStruct((N_DEV, dwo.shape[0] // N_CHIPS, D_MODEL), BF16)]
        out_specs += [ANY, ANY]
    if allgather is not None:
        in_specs.append(ANY)
        args.append(allgather)
        out_shape.append(jax.ShapeDtypeStruct((N_DEV,) + allgather.shape, allgather.dtype))
        out_specs.append(ANY)
    for hosted in (exchange, allgather):
        if hosted is not None:
            scratch += _comm_scratch(N_EXCHANGE_SEMS)
    return pl.pallas_call(
        body, name="bwd_mix" + ("" if exchange is None and allgather is None else "_exchange"), grid=(n_tiles,),
        out_shape=tuple(out_shape), in_specs=in_specs, out_specs=tuple(out_specs), scratch_shapes=scratch,
        compiler_params=_cparams(1),
    )(*args)


def _bwd_in(dproj, x, dout, g_pre, w_in_t):
    n_tok = x.shape[0]
    tm = TOK_TILE
    n_tiles = n_tok // tm

    def body(dp_ref, x_ref, dout_ref, g_ref, wt_ref, dx_ref, dwt_ref, dg_ref, acc_s):
        i = pl.program_id(0)

        @pl.when(i == 0)
        def _():
            acc_s[...] = jnp.zeros_like(acc_s)
            dg_ref[...] = jnp.zeros_like(dg_ref)

        xf = x_ref[...]
        r = lax.rsqrt(jnp.mean(xf * xf, axis=-1, keepdims=True) + NORM_EPS)
        xn = xf * r
        h = (xn * g_ref[...]).astype(BF16)
        dp = dp_ref[...]
        dh = _dot(dp, wt_ref[...])
        acc_s[...] += _dot_tn(dp, h)
        dg_ref[...] += jnp.sum(dh * xn, axis=0, keepdims=True)
        dhn = dh * g_ref[...]
        dx_ref[...] = dout_ref[...] + r * (dhn - xn * jnp.mean(dhn * xn, axis=-1, keepdims=True))

        @pl.when(i == n_tiles - 1)
        def _():
            dwt_ref[...] = acc_s[...].astype(BF16)

    return pl.pallas_call(
        body, name="bwd_in", grid=(n_tiles,),
        out_shape=(jax.ShapeDtypeStruct((n_tok, D_MODEL), F32), jax.ShapeDtypeStruct((IN_WIDTH, D_MODEL), BF16),
                   jax.ShapeDtypeStruct((1, D_MODEL), F32)),
        in_specs=[pl.BlockSpec((tm, IN_WIDTH), lambda i: (i, 0)), pl.BlockSpec((tm, D_MODEL), lambda i: (i, 0)),
                  pl.BlockSpec((tm, D_MODEL), lambda i: (i, 0)), pl.BlockSpec((1, D_MODEL), lambda i: (0, 0)),
                  pl.BlockSpec((IN_WIDTH, D_MODEL), lambda i: (0, 0), pipeline_mode=pl.Buffered(1))],
        out_specs=(pl.BlockSpec((tm, D_MODEL), lambda i: (i, 0)),
                   pl.BlockSpec((IN_WIDTH, D_MODEL), lambda i: (0, 0), pipeline_mode=pl.Buffered(1)),
                   pl.BlockSpec((1, D_MODEL), lambda i: (0, 0))),
        scratch_shapes=[pltpu.VMEM((IN_WIDTH, D_MODEL), F32)],
        compiler_params=_cparams(1),
    )(dproj, x, dout, g_pre, w_in_t)


def _riders(exchanges):
    args = [a for arrays, _ in exchanges for a in arrays]
    shapes = [jax.ShapeDtypeStruct((N_DEV, a.shape[0] // N_CHIPS if by_chip else a.shape[0]) + a.shape[1:], a.dtype)
              for arrays, by_chip in exchanges for a in arrays]
    scratch = [s for _ in exchanges for s in _comm_scratch(N_EXCHANGE_SEMS)]

    def bind(in_refs, out_refs, sem_refs, step, n_steps, pass_on_step=None):
        in_refs, out_refs, sem_refs = list(in_refs), list(out_refs), list(sem_refs)
        starts, ends = [], []
        for arrays, by_chip in exchanges:
            n = len(arrays)
            phases = _exchange_phases(tuple(in_refs[:n]), tuple(out_refs[:n]), *sem_refs[:3], by_chip=by_chip)
            del in_refs[:n], out_refs[:n], sem_refs[:3]
            at_start, at_end = _hosted(phases, step, n_steps, pass_on_step)
            starts.append(at_start)
            ends.append(at_end)
        return (lambda: [f() for f in starts]), (lambda: [f() for f in ends])

    return args, [ANY] * len(args), shapes, [ANY] * len(shapes), scratch, bind


def _bwd_in_dw(dproj, x, g_pre, exchanges):
    n_tok = x.shape[0]
    tm = TOK_TILE
    n_tiles = n_tok // tm
    r_args, r_in_specs, r_shapes, r_out_specs, r_scratch, bind = _riders(exchanges)

    def body(*refs):
        dp_ref, x_ref, g_ref = refs[:3]
        in_refs = refs[3:3 + len(r_args)]
        dwt_ref = refs[3 + len(r_args)]
        out_refs = refs[4 + len(r_args):4 + 2 * len(r_args)]
        acc_s = refs[4 + 2 * len(r_args)]
        i = pl.program_id(0)
        at_start, at_end = bind(in_refs, out_refs, refs[5 + 2 * len(r_args):], i, n_tiles)
        at_start()

        @pl.when(i == 0)
        def _():
            acc_s[...] = jnp.zeros_like(acc_s)

        xf = x_ref[...]
        r = lax.rsqrt(jnp.mean(xf * xf, axis=-1, keepdims=True) + NORM_EPS)
        acc_s[...] += _dot_tn(dp_ref[...], ((xf * r) * g_ref[...]).astype(BF16))

        @pl.when(i == n_tiles - 1)
        def _():
            dwt_ref[...] = acc_s[...].astype(BF16)

        at_end()

    return pl.pallas_call(
        body, name="bwd_in_dw", grid=(n_tiles,),
        out_shape=(jax.ShapeDtypeStruct((IN_WIDTH, D_MODEL), BF16), *r_shapes),
        in_specs=[pl.BlockSpec((tm, IN_WIDTH), lambda i: (i, 0)), pl.BlockSpec((tm, D_MODEL), lambda i: (i, 0)),
                  pl.BlockSpec((1, D_MODEL), lambda i: (0, 0))] + r_in_specs,
        out_specs=(pl.BlockSpec((IN_WIDTH, D_MODEL), lambda i: (0, 0), pipeline_mode=pl.Buffered(1)), *r_out_specs),
        scratch_shapes=[pltpu.VMEM((IN_WIDTH, D_MODEL), F32)] + r_scratch,
        compiler_params=_cparams(1),
    )(dproj, x, g_pre, *r_args)


def _bwd_in_dx(dproj, x, dout, g_pre, w_in_t, exchanges):
    n_tok = x.shape[0]
    tm = TOK_TILE
    n_tiles = n_tok // tm
    r_args, r_in_specs, r_shapes, r_out_specs, r_scratch, bind = _riders(exchanges)

    def body(*refs):
        dp_ref, x_ref, dout_ref, g_ref, wt_ref = refs[:5]
        in_refs = refs[5:5 + len(r_args)]
        dx_ref, dg_ref = refs[5 + len(r_args):7 + len(r_args)]
        out_refs = refs[7 + len(r_args):7 + 2 * len(r_args)]
        i = pl.program_id(0)
        at_start, at_end = bind(in_refs, out_refs, refs[7 + 2 * len(r_args):], i, n_tiles, pass_on_step=n_tiles - 2)
        at_start()

        @pl.when(i == 0)
        def _():
            dg_ref[...] = jnp.zeros_like(dg_ref)

        xf = x_ref[...]
        r = lax.rsqrt(jnp.mean(xf * xf, axis=-1, keepdims=True) + NORM_EPS)
        xn = xf * r
        dh = _dot(dp_ref[...], wt_ref[...])
        dg_ref[...] += jnp.sum(dh * xn, axis=0, keepdims=True)
        dhn = dh * g_ref[...]
        dx_ref[...] = dout_ref[...] + r * (dhn - xn * jnp.mean(dhn * xn, axis=-1, keepdims=True))
        at_end()

    tile = lambda w: pl.BlockSpec((tm, w), lambda i: (i, 0))
    return pl.pallas_call(
        body, name="bwd_in_dx", grid=(n_tiles,),
        out_shape=(jax.ShapeDtypeStruct((n_tok, D_MODEL), F32), jax.ShapeDtypeStruct((1, D_MODEL), F32), *r_shapes),
        in_specs=[tile(IN_WIDTH), tile(D_MODEL), tile(D_MODEL), pl.BlockSpec((1, D_MODEL), lambda i: (0, 0)),
                  pl.BlockSpec((IN_WIDTH, D_MODEL), lambda i: (0, 0), pipeline_mode=pl.Buffered(1))] + r_in_specs,
        out_specs=(tile(D_MODEL), pl.BlockSpec((1, D_MODEL), lambda i: (0, 0)), *r_out_specs),
        scratch_shapes=r_scratch,
        compiler_params=_cparams(1),
    )(dproj, x, dout, g_pre, w_in_t, *r_args)


def _adam_update(w, g, m, v):
    nm = ADAM_B1 * m + (1.0 - ADAM_B1) * g
    nv = ADAM_B2 * v + (1.0 - ADAM_B2) * (g * g)
    m_hat = nm / (1.0 - ADAM_B1 ** ADAM_STEP)
    v_hat = nv / (1.0 - ADAM_B2 ** ADAM_STEP)
    return -ADAM_LR * (m_hat / (jnp.sqrt(v_hat) + ADAM_EPS) + ADAM_WD * w), nm, nv


def _sum_slots(p_ref):
    tot = p_ref[0].astype(F32)
    for d in range(1, N_DEV):
        tot = tot + p_ref[d].astype(F32)
    return tot


def _adamw_parts(parts, w, m, v, name):
    n_layers, n_rows, n_cols = w.shape
    tr = n_rows
    while tr * n_cols > ADAMW_BLOCK_ELEMS:
        tr //= 2
    n_blocks = n_rows // tr

    def body(*refs):
        p_refs = refs[:n_layers]
        w_ref, m_ref, v_ref, g_ref, d_ref, nm_ref, nv_ref = refs[n_layers:]
        for k in range(n_layers):
            @pl.when(pl.program_id(0) == k)
            def _(k=k):
                g = _sum_slots(p_refs[k])
                g_ref[0] = g
                d_ref[0], nm_ref[0], nv_ref[0] = _adam_update(w_ref[0], g, m_ref[0], v_ref[0])

    def part_spec(k):
        return pl.BlockSpec((N_DEV, tr, n_cols), lambda l, i: (0, jnp.clip(i + (l - k) * n_blocks, 0, n_blocks - 1), 0))

    spec = pl.BlockSpec((1, tr, n_cols), lambda l, i: (l, i, 0))
    shape = jax.ShapeDtypeStruct((n_layers, n_rows, n_cols), F32)
    return pl.pallas_call(
        body, name="adamw_" + name, grid=(n_layers, n_blocks), out_shape=(shape,) * 4,
        in_specs=[part_spec(k) for k in range(n_layers)] + [spec, spec, spec], out_specs=(spec,) * 4,
        compiler_params=_cparams(2),
    )(*parts, w, m, v)


def _adamw_small(quads):
    n = len(quads)

    def body(*refs):
        ins, outs = refs[:4 * n], refs[4 * n:]
        for j in range(n):
            w_ref, g_ref, m_ref, v_ref = ins[4 * j:4 * j + 4]
            d_ref, nm_ref, nv_ref = outs[3 * j:3 * j + 3]
            d_ref[...], nm_ref[...], nv_ref[...] = _adam_update(w_ref[...], g_ref[...], m_ref[...], v_ref[...])

    vmem = pl.BlockSpec(memory_space=pltpu.VMEM)
    shapes = tuple(jax.ShapeDtypeStruct(w.shape, F32) for w, *_ in quads for _ in range(3))
    results = pl.pallas_call(
        body, name="adamw_small", out_shape=shapes, in_specs=[vmem] * (4 * n), out_specs=(vmem,) * (3 * n),
    )(*[a for quad in quads for a in quad])
    return [results[3 * j:3 * j + 3] for j in range(n)]


_SMALL = ("pre_norm_g", "ln_v_g", "ln_v_b", "b_spatial", "sinks", "rel_bias", "post_norm_g")


def _pack(pieces):
    blocks, first_rows, n = [], [], 0
    for p in pieces:
        flat = p.reshape(-1)
        flat = jnp.concatenate([flat, jnp.zeros(((-flat.shape[0]) % (8 * 128),), F32)]).reshape(-1, 128)
        blocks.append(flat)
        first_rows.append(n)
        n += flat.shape[0]
    return jnp.concatenate(blocks, axis=0), first_rows


def _unpack(block, first_row, shape):
    size = math.prod(shape)
    return block[first_row:first_row + -(-size // 128)].reshape(-1)[:size].reshape(shape)


def kernel(x, pre_norm_g, w_in, ln_v_g, ln_v_b, w_spatial, b_spatial, sinks, rel_bias, w_out, post_norm_g, loss_target, m_pre_norm_g, m_w_in, m_ln_v_g, m_ln_v_b, m_w_spatial, m_b_spatial, m_sinks, m_rel_bias, m_w_out, m_post_norm_g, v_pre_norm_g, v_w_in, v_ln_v_g, v_ln_v_b, v_w_spatial, v_b_spatial, v_sinks, v_rel_bias, v_w_out, v_post_norm_g):
    weights = dict(pre_norm_g=pre_norm_g, w_in=w_in, ln_v_g=ln_v_g, ln_v_b=ln_v_b, w_spatial=w_spatial, b_spatial=b_spatial,
                   sinks=sinks, rel_bias=rel_bias, w_out=w_out, post_norm_g=post_norm_g)
    mom_m = dict(pre_norm_g=m_pre_norm_g, w_in=m_w_in, ln_v_g=m_ln_v_g, ln_v_b=m_ln_v_b, w_spatial=m_w_spatial,
                 b_spatial=m_b_spatial, sinks=m_sinks, rel_bias=m_rel_bias, w_out=m_w_out, post_norm_g=m_post_norm_g)
    mom_v = dict(pre_norm_g=v_pre_norm_g, w_in=v_w_in, ln_v_g=v_ln_v_g, ln_v_b=v_ln_v_b, w_spatial=v_w_spatial,
                 b_spatial=v_b_spatial, sinks=v_sinks, rel_bias=v_rel_bias, w_out=v_w_out, post_norm_g=v_post_norm_g)
    n_seq, seq_len, _ = x.shape
    n_layers = w_in.shape[0]
    x2 = x.reshape(n_seq * seq_len, D_MODEL)
    target2 = loss_target.reshape(n_seq * seq_len, D_MODEL)
    row = lambda p, l: p[l][None]

    wt_shards = jnp.swapaxes(w_in, 1, 2).astype(BF16)
    wo_shards = w_out.astype(BF16)
    bucket = _window_buckets()
    bias = _bias_table(rel_bias, bucket)
    wcat, wtcat, bs = _spatial_tables(w_spatial, jnp.swapaxes(b_spatial, 1, 2))

    wt, wo = [None] * n_layers, [None] * n_layers
    wt[0], wo[0] = _gather_weights(wt_shards, wo_shards, 0)
    xs, saved = [x2], []
    for l in range(n_layers):
        layer_args = (xs[-1], row(pre_norm_g, l), wt[l], row(ln_v_g, l), row(ln_v_b, l), wcat[l], bs[l], sinks[l], bias,
                      wo[l], row(post_norm_g, l), seq_len)
        if l + 1 < n_layers:
            *acts, xn, y, wt[l + 1], wo[l + 1] = _fwd_layer(*layer_args, gather=(wt_shards, wo_shards, l + 1))
            xs.append(xn)
        else:
            *acts, dx, y, loss = _fwd_layer(*layer_args, target=target2)
        saved.append((y, *acts))

    small = [None] * n_layers
    parts = [None] * n_layers
    waiting = None
    for l in reversed(range(n_layers)):
        hosted = {} if waiting is None else dict(exchange=waiting[:2], allgather=waiting[2])
        dproj, *rest = _bwd_mix(dx, *saved[l], row(ln_v_g, l), row(ln_v_b, l), wcat[l], wtcat[l], bs[l], wo[l],
                                row(post_norm_g, l), **hosted)
        dwo, dws, dmix, dlng, dlnb, dgpost, dbias = rest[:7]
        dws = dws.reshape(HEADS * CHUNK, CHUNK).astype(BF16)
        if waiting is not None:
            parts[l + 1] = list(rest[7:])
        if l > 0:
            dx, dwt, dgpre = _bwd_in(dproj, xs[l], dx, row(pre_norm_g, l), wt[l])
            waiting = (dwt, dwo, dws)
        else:
            dwt, po, ws_all = _bwd_in_dw(dproj, xs[l], row(pre_norm_g, l), [((dwo,), True), ((dws,), False)])
            dx, dgpre, pt = _bwd_in_dx(dproj, xs[l], dx, row(pre_norm_g, l), wt[l], [((dwt,), True)])
            parts[l] = [pt, po, ws_all]
        small[l] = dict(pre_norm_g=dgpre[0], ln_v_g=dlng[0], ln_v_b=dlnb[0], dmix=dmix, dbias=dbias,
                        post_norm_g=dgpost[0])

    db = _b_spatial_grad(jnp.stack([s["dmix"] for s in small]))
    pieces = {n: jnp.stack([s[n] for s in small]) for n in ("pre_norm_g", "ln_v_g", "ln_v_b", "post_norm_g")}
    pieces["b_spatial"] = jnp.swapaxes(db[:, :, :HEADS], 1, 2)
    pieces["rel_bias"], pieces["sinks"] = _rel_bias_grad(jnp.stack([s["dbias"] for s in small]), bucket)
    packed, first_rows = _pack([pieces[n] for n in _SMALL] + [loss])
    total = _allreduce_small(packed)
    grad = {n: _unpack(total, r, weights[n].shape) for n, r in zip(_SMALL, first_rows)}
    loss_out = total[first_rows[-1], 0]

    delta, new_m, new_v = {}, {}, {}
    two_d = lambda p: p.reshape(-1, p.shape[-1])
    updates = _adamw_small([tuple(two_d(p[n]) for p in (weights, grad, mom_m, mom_v)) for n in _SMALL])
    for n, update in zip(_SMALL, updates):
        delta[n], new_m[n], new_v[n] = (u.reshape(weights[n].shape) for u in update)

    t3 = lambda p: jnp.swapaxes(p, 1, 2)
    flat3 = lambda p: p.reshape(n_layers, HEADS * CHUNK, CHUNK)
    for n, which, view, back in (("w_in", 0, t3, t3), ("w_out", 1, lambda p: p, lambda p: p),
                                 ("w_spatial", 2, flat3, lambda p: p.reshape(w_spatial.shape))):
        results = _adamw_parts([parts[l][which] for l in range(n_layers)], view(weights[n]), view(mom_m[n]),
                               view(mom_v[n]), n)
        grad[n], delta[n], new_m[n], new_v[n] = (back(r) for r in results)

    names = tuple(weights)
    return (loss_out, dx.reshape(x.shape), *[grad[n] for n in names], *[delta[n] for n in names],
            *[new_m[n] for n in names], *[new_v[n] for n in names])
```

```python
import math

import jax
import jax.numpy as jnp
from jax import lax
from jax.experimental import pallas as pl
from jax.experimental.pallas import tpu as pltpu

F32 = jnp.float32
BF16 = jnp.bfloat16

D_MODEL = 1024
A_WIDTH = 512
B_WIDTH = 512
KV_WIDTH = 128
IN_WIDTH = 3 * A_WIDTH + 2 * B_WIDTH + 2 * KV_WIDTH
CHUNK = 128
HEADS = 8
HEAD_DIM = 64
REL_BUCKETS = 32
NORM_EPS = 1e-6
NEG = -1e30
Q_OFF = 3 * A_WIDTH
K_OFF = Q_OFF + B_WIDTH
BZ_OFF = K_OFF + 2 * KV_WIDTH
QK_SCALE = HEAD_DIM ** -0.5

ADAM_LR = 0.001
ADAM_B1 = 0.9
ADAM_B2 = 0.999
ADAM_EPS = 1e-08
ADAM_WD = 0.01
ADAM_STEP = 10

TOK_TILE = 512
CHUNK_UNROLL = 4
BWD_CHUNK_UNROLL = 4
ADAMW_BLOCK_ELEMS = 192 * 1024
VMEM_LIMIT_V7X = 60 * 1024 * 1024

N_DEV = 8
N_CHIPS = 4
MESH_ID = pl.DeviceIdType.MESH
ANY = pl.BlockSpec(memory_space=pl.ANY)
SMEM = pl.BlockSpec(memory_space=pltpu.SMEM)


def _cparams(n_axes):
    return pltpu.CompilerParams(dimension_semantics=("arbitrary",) * n_axes, vmem_limit_bytes=VMEM_LIMIT_V7X)


_GELU_C = math.sqrt(2.0 / math.pi)
_GELU_C3 = _GELU_C * 0.044715


def _gelu(x):
    t = jnp.tanh(x * (_GELU_C3 * (x * x) + _GELU_C))
    return x * (0.5 * t + 0.5)


def _gelu_and_grad(x):
    x2 = x * x
    t = jnp.tanh(x * (_GELU_C3 * x2 + _GELU_C))
    cdf = 0.5 * t + 0.5
    d = cdf + (x * (cdf * (1.0 - cdf))) * ((6.0 * _GELU_C3) * x2 + 2.0 * _GELU_C)
    return x * cdf, d


def _sigmoid(x):
    return 0.5 + 0.5 * jnp.tanh(0.5 * x)


def _silu(x):
    return x * _sigmoid(x)


def _silu_and_grad(x):
    s = _sigmoid(x)
    return x * s, s * (1.0 + x * (1.0 - s))


def _dot(a, b):
    return jnp.dot(a, b, preferred_element_type=F32)


def _dot_nt(a, b):
    return lax.dot_general(a, b, (((1,), (1,)), ((), ())), preferred_element_type=F32)


def _dot_tn(a, b):
    return lax.dot_general(a, b, (((0,), (0,)), ((), ())), preferred_element_type=F32)


def _lo_mask(shape):
    return lax.broadcasted_iota(jnp.int32, shape, 1) < HEAD_DIM


def _swap_halves(v):
    return pltpu.roll(v, HEAD_DIM, 1)


def _window_buckets():
    q_loc = jnp.arange(CHUNK)[:, None]
    j_loc = jnp.arange(CHUNK)[None, :]
    d = q_loc - j_loc + jnp.where(j_loc > q_loc, CHUNK, 0)
    max_exact = REL_BUCKETS // 2
    safe = jnp.maximum(d, 1).astype(F32)
    large = max_exact + (jnp.log(safe / max_exact) / math.log(CHUNK / max_exact)
                         * (REL_BUCKETS - max_exact)).astype(jnp.int32)
    large = jnp.minimum(large, REL_BUCKETS - 1)
    return jnp.where(d < max_exact, d, large).astype(jnp.int32)


def _bias_table(rel_bias, bucket):
    def body(rb_ref, bk_ref, out_ref):
        bk = bk_ref[...]
        for h in range(HEADS):
            acc = jnp.zeros(bk.shape, F32)
            for b in range(REL_BUCKETS):
                acc = jnp.where(bk == b, rb_ref[b, h], acc)
            out_ref[h] = acc

    vmem = pl.BlockSpec(memory_space=pltpu.VMEM)
    return pl.pallas_call(
        body, name="bias_table",
        out_shape=jax.ShapeDtypeStruct((HEADS, CHUNK, CHUNK), F32),
        in_specs=[SMEM, vmem], out_specs=vmem,
    )(rel_bias, bucket)


def _rel_bias_grad(dbias, bucket):
    n_layers = len(dbias)

    def body(*refs):
        db_refs, (bk_ref, out_ref, dsink_ref) = refs[:n_layers], refs[n_layers:]
        bk = bk_ref[...]
        for h in range(HEADS):
            tot = jnp.zeros((CHUNK, CHUNK), F32)
            for l in range(n_layers):
                ds = db_refs[l][h]
                dsink_ref[l, h] = -jnp.sum(ds)
                tot = tot + ds
            for b in range(REL_BUCKETS):
                out_ref[b, h] = jnp.sum(jnp.where(bk == b, tot, 0.0))

    vmem = pl.BlockSpec(memory_space=pltpu.VMEM)
    return pl.pallas_call(
        body, name="rel_bias_grad",
        out_shape=(jax.ShapeDtypeStruct((REL_BUCKETS, HEADS), F32), jax.ShapeDtypeStruct((n_layers, HEADS), F32)),
        in_specs=[vmem] * (n_layers + 1), out_specs=(SMEM, SMEM),
    )(*dbias, bucket)


def _spatial_tables(w_spatial, b_spatial_t):
    n_layers = w_spatial.shape[0]

    def body(w_ref, b_ref, wcat_ref, wtcat_ref, bs_ref):
        row = lax.broadcasted_iota(jnp.int32, (CHUNK, CHUNK), 0)
        col = lax.broadcasted_iota(jnp.int32, (CHUNK, CHUNK), 1)
        causal = col <= row
        lo = _lo_mask((CHUNK, CHUNK))
        for p in range(4):
            for half in range(2):
                w = jnp.where(causal, w_ref[0, 2 * p + half], 0.0)
                wcat_ref[0, p, :, half * CHUNK:(half + 1) * CHUNK] = w.astype(BF16)
                wtcat_ref[0, p, :, half * CHUNK:(half + 1) * CHUNK] = w.T.astype(BF16)
            b = b_ref[0]
            bs_ref[0, p] = jnp.where(lo, b[:, 2 * p:2 * p + 1], b[:, 2 * p + 1:2 * p + 2])

    return pl.pallas_call(
        body, name="spatial_tables", grid=(n_layers,),
        out_shape=(jax.ShapeDtypeStruct((n_layers, 4, CHUNK, 2 * CHUNK), BF16),
                   jax.ShapeDtypeStruct((n_layers, 4, CHUNK, 2 * CHUNK), BF16),
                   jax.ShapeDtypeStruct((n_layers, 4, CHUNK, CHUNK), F32)),
        in_specs=[pl.BlockSpec((1, HEADS, CHUNK, CHUNK), lambda l: (l, 0, 0, 0)),
                  pl.BlockSpec((1, CHUNK, HEADS), lambda l: (l, 0, 0))],
        out_specs=(pl.BlockSpec((1, 4, CHUNK, 2 * CHUNK), lambda l: (l, 0, 0, 0)),
                   pl.BlockSpec((1, 4, CHUNK, 2 * CHUNK), lambda l: (l, 0, 0, 0)),
                   pl.BlockSpec((1, 4, CHUNK, CHUNK), lambda l: (l, 0, 0, 0))),
        compiler_params=_cparams(1),
    )(w_spatial, b_spatial_t)


def _b_spatial_grad(dmix):
    n_layers = len(dmix)

    def body(*refs):
        d_refs, out_ref = refs[:n_layers], refs[n_layers]
        lane = lax.broadcasted_iota(jnp.int32, (CHUNK, CHUNK), 1)
        for l in range(n_layers):
            acc = jnp.zeros((CHUNK, CHUNK), F32)
            for p in range(4):
                t = d_refs[l][:, p * CHUNK:(p + 1) * CHUNK]
                s_lo = jnp.sum(jnp.where(lane < HEAD_DIM, t, 0.0), axis=1, keepdims=True)
                s_hi = jnp.sum(jnp.where(lane < HEAD_DIM, 0.0, t), axis=1, keepdims=True)
                acc = jnp.where(lane == 2 * p, s_lo, acc)
                acc = jnp.where(lane == 2 * p + 1, s_hi, acc)
            out_ref[l] = acc

    vmem = pl.BlockSpec(memory_space=pltpu.VMEM)
    return pl.pallas_call(
        body, name="b_spatial_grad",
        out_shape=jax.ShapeDtypeStruct((n_layers, CHUNK, CHUNK), F32),
        in_specs=[vmem] * n_layers, out_specs=vmem,
    )(*dmix)


def _place():
    x, y, c = lax.axis_index("x"), lax.axis_index("y"), lax.axis_index("c")
    other_chips = [(1 - x, y), (x, 1 - y), (1 - x, 1 - y)]
    return x, y, c, other_chips


N_GATHER_SEMS = 12


def _gather_phases(shards, fulls, send_sems, recv_sems, local_sems):
    x, y, c, chips = _place()
    sibling = (x, y, 1 - c)
    n_arr = len(shards)

    def half_rows(a, chip, half):
        n = shards[a].shape[0]
        start = (2 * chip[0] + chip[1]) * n + half * (n // 2)
        return fulls[a].at[pl.ds(pl.multiple_of(start, 16), n // 2), :]

    def my_half(a):
        n = shards[a].shape[0]
        return shards[a].at[pl.ds(pl.multiple_of(c * (n // 2), 16), n // 2), :]

    def copy(k, a, src, chip, half, to):
        return pltpu.make_async_remote_copy(
            src_ref=src, dst_ref=half_rows(a, chip, half), send_sem=send_sems.at[n_arr * k + a],
            recv_sem=recv_sems.at[n_arr * k + a], device_id=to, device_id_type=MESH_ID)

    def local(a):
        n = shards[a].shape[0]
        mine = fulls[a].at[pl.ds(pl.multiple_of((2 * x + y) * n, 16), n), :]
        return pltpu.make_async_copy(shards[a], mine, local_sems.at[a])

    def first(k, a):
        return copy(k, a, my_half(a), (x, y), c, (chips[k][0], chips[k][1], c))

    def passed(k, a):
        return copy(3 + k, a, half_rows(a, chips[k], c), chips[k], c, sibling)

    def phase_a():
        for a in range(n_arr):
            local(a).start()
        for k in range(3):
            for a in range(n_arr):
                first(k, a).start()

    def phase_b():
        for k in range(3):
            for a in range(n_arr):
                copy(k, a, my_half(a), chips[k], c, sibling).wait_recv()
                passed(k, a).start()

    def phase_c():
        for k in range(3):
            for a in range(n_arr):
                copy(3 + k, a, my_half(a), chips[k], 1 - c, sibling).wait_recv()
        for k in range(3):
            for a in range(n_arr):
                first(k, a).wait_send()
                passed(k, a).wait_send()
        for a in range(n_arr):
            local(a).wait()

    return phase_a, phase_b, phase_c


N_EXCHANGE_SEMS = 14


def _exchange_phases(partials, parts, send_sems, recv_sems, local_sems, by_chip=True):
    x, y, c, chips = _place()
    me, sibling = (x, y, c), (x, y, 1 - c)
    n_arr = len(partials)

    def block(a, chip):
        if not by_chip:
            return partials[a]
        n = partials[a].shape[0] // N_CHIPS
        return partials[a].at[pl.ds(pl.multiple_of((2 * chip[0] + chip[1]) * n, 16), n), :]

    def slot(a, dev):
        return parts[a].at[4 * dev[0] + 2 * dev[1] + dev[2]]

    def copy(k, a, src, origin, to):
        return pltpu.make_async_remote_copy(
            src_ref=src, dst_ref=slot(a, origin), send_sem=send_sems.at[n_arr * k + a],
            recv_sem=recv_sems.at[n_arr * k + a], device_id=to, device_id_type=MESH_ID)

    def local(a):
        return pltpu.make_async_copy(block(a, (x, y)), slot(a, me), local_sems.at[a])

    def first(k, a):
        if k == 0:
            return copy(0, a, block(a, (x, y)), me, sibling)
        chip = chips[k - 1]
        return copy(k, a, block(a, chip), me, (chip[0], chip[1], c))

    def passed(k, a):
        origin = (chips[k][0], chips[k][1], c)
        return copy(4 + k, a, slot(a, origin), origin, sibling)

    def phase_a():
        for a in range(n_arr):
            local(a).start()
        for k in range(4):
            for a in range(n_arr):
                first(k, a).start()

    def phase_b():
        for k in range(3):
            for a in range(n_arr):
                copy(1 + k, a, block(a, (x, y)), (chips[k][0], chips[k][1], c), me).wait_recv()
                passed(k, a).start()

    def phase_c():
        for a in range(n_arr):
            copy(0, a, block(a, (x, y)), sibling, me).wait_recv()
        for k in range(3):
            for a in range(n_arr):
                copy(4 + k, a, block(a, (x, y)), (chips[k][0], chips[k][1], 1 - c), me).wait_recv()
        for k in range(4):
            for a in range(n_arr):
                first(k, a).wait_send()
        for k in range(3):
            for a in range(n_arr):
                passed(k, a).wait_send()
        for a in range(n_arr):
            local(a).wait()

    return phase_a, phase_b, phase_c


def _comm_scratch(n_sems):
    return [pltpu.SemaphoreType.DMA((n_sems,)), pltpu.SemaphoreType.DMA((n_sems,)), pltpu.SemaphoreType.DMA((2,))]


def _gather_weights(wt_shards, wo_shards, layer):
    wt_rows, wo_rows = wt_shards.shape[1], wo_shards.shape[1]

    def body(wt_ref, wo_ref, wt_full, wo_full, send_sems, recv_sems, local_sems):
        phases = _gather_phases((wt_ref.at[layer], wo_ref.at[layer]), (wt_full, wo_full), send_sems, recv_sems, local_sems)
        for phase in phases:
            phase()

    return pl.pallas_call(
        body, name="gather_weights",
        out_shape=(jax.ShapeDtypeStruct((N_CHIPS * wt_rows, D_MODEL), BF16),
                   jax.ShapeDtypeStruct((N_CHIPS * wo_rows, D_MODEL), BF16)),
        in_specs=[ANY, ANY], out_specs=(ANY, ANY), scratch_shapes=_comm_scratch(N_GATHER_SEMS),
    )(wt_shards, wo_shards)


def _allreduce_small(part):
    n_rows = part.shape[0]

    def body(p_ref, tot_ref, all_ref, send_sems, recv_sems, local_sem):
        x, y, c, chips = _place()
        me, sibling = (x, y, c), (x, y, 1 - c)

        def rows(dev):
            return all_ref.at[pl.ds(pl.multiple_of((4 * dev[0] + 2 * dev[1] + dev[2]) * n_rows, 8), n_rows), :]

        def copy(k, origin, to, src=None):
            return pltpu.make_async_remote_copy(
                src_ref=rows(origin) if src is None else src, dst_ref=rows(origin), send_sem=send_sems.at[k],
                recv_sem=recv_sems.at[k], device_id=to, device_id_type=MESH_ID)

        mine = pltpu.make_async_copy(p_ref, rows(me), local_sem)
        mine.start()
        first = [copy(0, me, sibling, src=p_ref)]
        first += [copy(1 + k, me, (chip[0], chip[1], c), src=p_ref) for k, chip in enumerate(chips)]
        for cp in first:
            cp.start()
        passed = []
        for k, chip in enumerate(chips):
            origin = (chip[0], chip[1], c)
            copy(1 + k, origin, me).wait_recv()
            fwd = copy(4 + k, origin, sibling)
            fwd.start()
            passed.append(fwd)
        copy(0, sibling, me).wait_recv()
        for k, chip in enumerate(chips):
            copy(4 + k, (chip[0], chip[1], 1 - c), me).wait_recv()
        for cp in first + passed:
            cp.wait_send()
        mine.wait()
        tot = all_ref[0:n_rows, :]
        for d in range(1, N_DEV):
            tot = tot + all_ref[d * n_rows:(d + 1) * n_rows, :]
        tot_ref[...] = tot

    vmem = pl.BlockSpec(memory_space=pltpu.VMEM)
    return pl.pallas_call(
        body, name="allreduce_small",
        out_shape=jax.ShapeDtypeStruct((n_rows, 128), F32),
        in_specs=[vmem], out_specs=vmem,
        scratch_shapes=[pltpu.VMEM((N_DEV * n_rows, 128), F32), pltpu.SemaphoreType.DMA((7,)),
                        pltpu.SemaphoreType.DMA((7,)), pltpu.SemaphoreType.DMA],
        compiler_params=pltpu.CompilerParams(vmem_limit_bytes=VMEM_LIMIT_V7X),
    )(part)


def _hosted(phases, step, n_steps, pass_on_step=None):
    phase_a, phase_b, phase_c = phases
    if pass_on_step is None:
        pass_on_step = (3 * n_steps) // 4

    def at_start():
        pl.when(step == 0)(phase_a)

    def at_end():
        pl.when(step == pass_on_step)(phase_b)
        pl.when(step == n_steps - 1)(phase_c)

    return at_start, at_end


def _layer_norm_stats(vv):
    mu = jnp.mean(vv, axis=-1, keepdims=True)
    xc = vv - mu
    rs = lax.rsqrt(jnp.mean(xc * xc, axis=-1, keepdims=True) + NORM_EPS)
    return xc * rs, rs


def _blockdiag(v, lo):
    zero = jnp.zeros_like(v)
    return jnp.concatenate([jnp.where(lo, v, zero), jnp.where(lo, zero, v)], axis=0)


def _softmax_sink(s, sink):
    m = jnp.maximum(jnp.max(s, axis=-1, keepdims=True), sink)
    e = jnp.exp(s - m)
    esink = jnp.exp(sink - m)
    den = jnp.sum(e, axis=-1, keepdims=True) + esink
    return e / den


def _kv_rows(cur_ref, halo_ref, r0, c):
    prev_in_tile = cur_ref[pl.ds(pl.multiple_of(jnp.maximum(r0 - CHUNK, 0), CHUNK), CHUNK), :]
    prev = jnp.where(c == 0, halo_ref[...], prev_in_tile)
    kv2 = jnp.concatenate([prev, cur_ref[pl.ds(r0, CHUNK), :]], axis=0)
    k2, v2 = kv2[:, 0:KV_WIDTH], kv2[:, KV_WIDTH:2 * KV_WIDTH]
    return (k2, _swap_halves(k2)), (v2, _swap_halves(v2))


def _window_square(over_keys, prev):
    return jnp.where(prev, over_keys[:, 0:CHUNK], over_keys[:, CHUNK:2 * CHUNK])


def _window_keys(square, prev):
    zero = jnp.zeros_like(square)
    return jnp.concatenate([jnp.where(prev, square, zero), jnp.where(prev, zero, square)], axis=1)


def _ahead(shape=(CHUNK, CHUNK)):
    return lax.broadcasted_iota(jnp.int32, shape, 1) - lax.broadcasted_iota(jnp.int32, shape, 0)


def _dead_mask(first, ahead):
    return ahead > jnp.where(first, 0, CHUNK)


def _head_of(p, half):
    return 2 * p + half, int(half != p // 2)


_HEADS_BY_COPY = tuple(tuple((p, half) for p in range(4) for half in range(2) if _head_of(p, half)[1] == sw)
                       for sw in range(2))


def _masked_halves(tile, lo):
    zero = jnp.zeros_like(tile)
    return {0: jnp.where(lo, tile, zero), 1: jnp.where(lo, zero, tile)}


def _query_stacks(q_tiles, lo):
    qm = {p: _masked_halves(q_tiles[p] * QK_SCALE, lo) for p in range(4)}
    return [jnp.concatenate([qm[p][half] for p, half in members], axis=0) for members in _HEADS_BY_COPY]


def _attention_probs(q_stacks, ks, bias_ref, sink_ref, dead, prev):
    probs = {}
    for sw, members in enumerate(_HEADS_BY_COPY):
        s_stack = _dot_nt(q_stacks[sw], ks[sw])
        for i, (p, half) in enumerate(members):
            head = 2 * p + half
            s = _window_square(s_stack[i * CHUNK:(i + 1) * CHUNK, :], prev) + bias_ref[head]
            s = jnp.where(dead, NEG, s)
            probs[p, half] = _softmax_sink(s, sink_ref[head]).astype(BF16)
    return probs


def _attention_values(probs, vs, prev):
    outs, p_stacks = {}, []
    for sw, members in enumerate(_HEADS_BY_COPY):
        p_stack = jnp.concatenate([_window_keys(probs[m], prev) for m in members], axis=0)
        r_stack = _dot(p_stack, vs[sw])
        p_stacks.append(p_stack)
        for i, m in enumerate(members):
            outs[m] = r_stack[i * CHUNK:(i + 1) * CHUNK, :]
    return outs, p_stacks


def _fwd_layer(x, g_pre, w_in_t, ln_g, ln_b, wcat, bs, sinks, bias, w_out, g_post, seq_len, gather=None, target=None):
    n_tok = x.shape[0]
    tb = TOK_TILE
    n_chunks = tb // CHUNK
    n_tiles = n_tok // tb
    n_in = 11
    assert gather is None or target is None

    def body(*refs):
        (x_ref, gpre_ref, wt_ref, lng_ref, lnb_ref, wcat_ref, bs_ref, sink_ref, bias_ref, wout_ref,
         gpost_ref) = refs[:n_in]
        at_start = at_end = lambda: None
        if gather is not None:
            (wts_ref, wos_ref, a_ref, q_ref, kv_ref, bz_ref, probs_ref, xn_ref, y_ref, wt_full, wo_full, ycat_s, halo_ref,
             send_sems, recv_sems, local_sems) = refs[n_in:]
            phases = _gather_phases((wts_ref.at[gather[2]], wos_ref.at[gather[2]]), (wt_full, wo_full), send_sems,
                                    recv_sems, local_sems)
            at_start, at_end = _hosted(phases, pl.program_id(0), n_tiles)
        elif target is not None:
            target_ref, a_ref, q_ref, kv_ref, bz_ref, probs_ref, xn_ref, y_ref, loss_ref, ycat_s, halo_ref = refs[n_in:]
        else:
            a_ref, q_ref, kv_ref, bz_ref, probs_ref, xn_ref, y_ref, ycat_s, halo_ref = refs[n_in:]
        at_start()
        i = pl.program_id(0)
        lo = _lo_mask((CHUNK, CHUNK))
        ahead = _ahead()
        prev = ahead > 0

        @pl.when(i == 0)
        def _():
            halo_ref[...] = jnp.zeros_like(halo_ref)

        xf = x_ref[...]
        r1 = lax.rsqrt(jnp.mean(xf * xf, axis=-1, keepdims=True) + NORM_EPS)
        h = ((xf * r1) * gpre_ref[...]).astype(BF16)
        a_ref[...] = _dot_nt(h, wt_ref[0:Q_OFF, :])
        q_ref[...] = _dot_nt(h, wt_ref[Q_OFF:K_OFF, :]).astype(BF16)
        kv_ref[...] = _dot_nt(h, wt_ref[K_OFF:BZ_OFF, :]).astype(BF16)
        bz_ref[...] = _dot_nt(h, wt_ref[BZ_OFF:IN_WIDTH, :])

        def chunk(c, carry):
            r0 = pl.multiple_of(c * CHUNK, CHUNK)
            rows = pl.ds(r0, CHUNK)
            u = _gelu(a_ref[rows, 0:A_WIDTH])
            vv = _gelu(a_ref[rows, A_WIDTH:2 * A_WIDTH])
            xhat, _ = _layer_norm_stats(vv)
            vnb = (xhat * lng_ref[...] + lnb_ref[...]).astype(BF16)
            for p in range(4):
                blk = slice(p * CHUNK, (p + 1) * CHUNK)
                mixed = _dot(wcat_ref[p], _blockdiag(vnb[:, blk], lo)) + bs_ref[p]
                sz = _silu(a_ref[rows, 2 * A_WIDTH + p * CHUNK:2 * A_WIDTH + (p + 1) * CHUNK])
                ycat_s[rows, blk] = ((u[:, blk] * mixed) * sz).astype(BF16)
            ks, vs = _kv_rows(kv_ref, halo_ref, r0, c)
            dead = _dead_mask(lax.rem(i * tb + r0, seq_len) == 0, ahead)
            q_tiles = [q_ref[rows, p * CHUNK:(p + 1) * CHUNK] for p in range(4)]
            probs = _attention_probs(_query_stacks(q_tiles, lo), ks, bias_ref, sink_ref, dead, prev)
            for (p, half), prob in probs.items():
                head = 2 * p + half
                probs_ref[rows, head * CHUNK:(head + 1) * CHUNK] = prob
            outs, _ = _attention_values(probs, vs, prev)
            for p in range(4):
                blk = slice(p * CHUNK, (p + 1) * CHUNK)
                o = jnp.where(lo, outs[p, 0], outs[p, 1])
                ycat_s[rows, B_WIDTH + p * CHUNK:B_WIDTH + (p + 1) * CHUNK] = (o * _silu(bz_ref[rows, blk])).astype(BF16)
            return carry

        lax.fori_loop(0, n_chunks, chunk, 0, unroll=CHUNK_UNROLL)
        halo_ref[...] = kv_ref[tb - CHUNK:tb, :]
        y = _dot(ycat_s[...], wout_ref[...])
        r = lax.rsqrt(jnp.mean(y * y, axis=-1, keepdims=True) + NORM_EPS)
        y_ref[...] = y
        xn = x_ref[...] + (y * r) * gpost_ref[...]
        if target is None:
            xn_ref[...] = xn
        else:
            d = xn - target_ref[...]
            xn_ref[...] = d * (1.0 / D_MODEL)

            @pl.when(i == 0)
            def _():
                loss_ref[0, 0] = 0.0

            loss_ref[0, 0] += 0.5 * jnp.sum(jnp.mean(d * d, axis=-1, keepdims=True))
        at_end()

    tile = lambda w: pl.BlockSpec((tb, w), lambda i: (i, 0))
    whole = lambda shape, **kw: pl.BlockSpec(shape, lambda i: (0,) * len(shape), **kw)
    in_specs = [tile(D_MODEL), whole((1, D_MODEL)), whole((IN_WIDTH, D_MODEL), pipeline_mode=pl.Buffered(1)),
                whole((1, A_WIDTH)), whole((1, A_WIDTH)), whole((4, CHUNK, 2 * CHUNK)), whole((4, CHUNK, CHUNK)), SMEM,
                whole((HEADS, CHUNK, CHUNK)), whole((D_MODEL, D_MODEL)), whole((1, D_MODEL))]
    out_shape = [jax.ShapeDtypeStruct((n_tok, Q_OFF), F32), jax.ShapeDtypeStruct((n_tok, B_WIDTH), BF16),
                 jax.ShapeDtypeStruct((n_tok, 2 * KV_WIDTH), BF16), jax.ShapeDtypeStruct((n_tok, B_WIDTH), F32),
                 jax.ShapeDtypeStruct((n_tok, HEADS * CHUNK), BF16), jax.ShapeDtypeStruct((n_tok, D_MODEL), F32),
                 jax.ShapeDtypeStruct((n_tok, D_MODEL), F32)]
    out_specs = [tile(Q_OFF), tile(B_WIDTH), tile(2 * KV_WIDTH), tile(B_WIDTH), tile(HEADS * CHUNK), tile(D_MODEL),
                 tile(D_MODEL)]
    scratch = [pltpu.VMEM((tb, D_MODEL), BF16), pltpu.VMEM((CHUNK, 2 * KV_WIDTH), BF16)]
    args = [x, g_pre, w_in_t, ln_g, ln_b, wcat, bs, sinks, bias, w_out, g_post]
    if gather is not None:
        wt_shards, wo_shards, _ = gather
        in_specs += [ANY, ANY]
        args += [wt_shards, wo_shards]
        out_shape += [jax.ShapeDtypeStruct((N_CHIPS * wt_shards.shape[1], D_MODEL), BF16),
                      jax.ShapeDtypeStruct((N_CHIPS * wo_shards.shape[1], D_MODEL), BF16)]
        out_specs += [ANY, ANY]
        scratch += _comm_scratch(N_GATHER_SEMS)
    if target is not None:
        in_specs.append(tile(D_MODEL))
        args.append(target)
        out_shape.append(jax.ShapeDtypeStruct((1, 1), F32))
        out_specs.append(SMEM)
    name = "fwd_layer" + ("" if gather is None else "_gather") + ("" if target is None else "_loss")
    return pl.pallas_call(
        body, name=name, grid=(n_tiles,),
        out_shape=tuple(out_shape), in_specs=in_specs, out_specs=tuple(out_specs), scratch_shapes=scratch,
        compiler_params=_cparams(1),
    )(*args)


def _bwd_mix(dout, y, a, q, kv, bz, probs, ln_g, ln_b, wcat, wtcat, bs, w_out, g_post, exchange=None, allgather=None):
    n_tok = y.shape[0]
    tb = TOK_TILE
    n_chunks = tb // CHUNK
    n_tiles = n_tok // tb

    def body(*refs):
        refs = list(refs)
        take = lambda n: [refs.pop(0) for _ in range(n)]
        (dout_ref, y_ref, a_ref, q_ref, kv_ref, halo_ref, bz_ref, probs_ref, lng_ref, lnb_ref, wcat_ref, wtcat_ref, bs_ref,
         wout_ref, gpost_ref) = take(15)
        if exchange is not None:
            dwt_in, dwo_in = take(2)
        if allgather is not None:
            (small_in,) = take(1)
        dproj_ref, dwout_ref, dws_ref, dmix_ref, dlng_ref, dlnb_ref, dgpost_ref, dbias_ref = take(8)
        if exchange is not None:
            pt_ref, po_ref = take(2)
        if allgather is not None:
            (small_all,) = take(1)
        ycat_s, dy_s, dyc_s, dkv_s, carry_s, dwout_s, stage_s, dws_s = take(8)
        starts, ends = [], []
        if exchange is not None:
            hosted = _hosted(_exchange_phases((dwt_in, dwo_in), (pt_ref, po_ref), *take(3)), pl.program_id(0), n_tiles)
            starts.append(hosted[0])
            ends.append(hosted[1])
        if allgather is not None:
            hosted = _hosted(_exchange_phases((small_in,), (small_all,), *take(3), by_chip=False), pl.program_id(0),
                             n_tiles)
            starts.append(hosted[0])
            ends.append(hosted[1])
        at_start = lambda: [f() for f in starts]
        at_end = lambda: [f() for f in ends]
        at_start()
        step = pl.program_id(0)
        lo = _lo_mask((CHUNK, CHUNK))
        prev = _ahead() > 0

        @pl.when(step == 0)
        def _():
            dwout_s[...] = jnp.zeros_like(dwout_s)
            dws_s[...] = jnp.zeros_like(dws_s)
            dmix_ref[...] = jnp.zeros_like(dmix_ref)
            dlng_ref[...] = jnp.zeros_like(dlng_ref)
            dlnb_ref[...] = jnp.zeros_like(dlnb_ref)
            dgpost_ref[...] = jnp.zeros_like(dgpost_ref)
            dbias_ref[...] = jnp.zeros_like(dbias_ref)
            carry_s[...] = jnp.zeros_like(carry_s)

        for r0 in range(0, tb, CHUNK):
            rows = slice(r0, r0 + CHUNK)
            yv = y_ref[rows, :]
            dout = dout_ref[rows, :]
            r = lax.rsqrt(jnp.mean(yv * yv, axis=-1, keepdims=True) + NORM_EPS)
            yn = yv * r
            dgpost_ref[...] += jnp.sum(dout * yn, axis=0, keepdims=True)
            dyn = dout * gpost_ref[...]
            dy_s[rows, :] = (r * (dyn - yn * jnp.mean(dyn * yn, axis=-1, keepdims=True))).astype(BF16)
        dyc_s[...] = _dot_nt(dy_s[...], wout_ref[...])
        dkv_s[0:tb, :] = jnp.zeros((tb, 2 * KV_WIDTH), F32)
        dkv_s[tb:tb + CHUNK, :] = carry_s[...]

        def chunk(c, carry):
            r0 = pl.multiple_of(c * CHUNK, CHUNK)
            rows = pl.ds(r0, CHUNK)
            u, gu = _gelu_and_grad(a_ref[rows, 0:A_WIDTH])
            vv, gv = _gelu_and_grad(a_ref[rows, A_WIDTH:2 * A_WIDTH])
            xhat, rs = _layer_norm_stats(vv)
            vnb = (xhat * lng_ref[...] + lnb_ref[...]).astype(BF16)
            d_vn, d_u, d_az = [], [], []
            for p in range(4):
                blk = slice(p * CHUNK, (p + 1) * CHUNK)
                vbd = _blockdiag(vnb[:, blk], lo)
                mixed = _dot(wcat_ref[p], vbd) + bs_ref[p]
                sz, gz = _silu_and_grad(a_ref[rows, 2 * A_WIDTH + p * CHUNK:2 * A_WIDTH + (p + 1) * CHUNK])
                ub = u[:, blk]
                dya = dyc_s[rows, blk]
                um = ub * mixed
                ycat_s[rows, blk] = (um * sz).astype(BF16)
                d_mixed = (dya * ub) * sz
                d_u.append((dya * mixed) * sz)
                d_az.append((dya * um) * gz)
                dmix_ref[:, blk] += d_mixed
                dmbd = _blockdiag(d_mixed.astype(BF16), lo)
                d_vn.append(_dot(wtcat_ref[p], dmbd))
                dws_s[p] += _dot_nt(dmbd, vnb[:, blk])
            d_vn = jnp.concatenate(d_vn, axis=1)
            dlng_ref[...] += jnp.sum(d_vn * xhat, axis=0, keepdims=True)
            dlnb_ref[...] += jnp.sum(d_vn, axis=0, keepdims=True)
            dxh = d_vn * lng_ref[...]
            d_vv = rs * (dxh - jnp.mean(dxh, axis=-1, keepdims=True)
                         - xhat * jnp.mean(dxh * xhat, axis=-1, keepdims=True))
            dproj_ref[rows, 0:A_WIDTH] = (jnp.concatenate(d_u, axis=1) * gu).astype(BF16)
            dproj_ref[rows, A_WIDTH:2 * A_WIDTH] = (d_vv * gv).astype(BF16)
            dproj_ref[rows, 2 * A_WIDTH:Q_OFF] = jnp.concatenate(d_az, axis=1).astype(BF16)
            ks, vs = _kv_rows(kv_ref, halo_ref, r0, c)
            q_stacks = _query_stacks([q_ref[rows, p * CHUNK:(p + 1) * CHUNK] for p in range(4)], lo)
            probs = {(p, half): probs_ref[rows, (2 * p + half) * CHUNK:(2 * p + half + 1) * CHUNK]
                     for p in range(4) for half in range(2)}
            outs_h, p_stacks = _attention_values(probs, vs, prev)
            dom = {}
            for p in range(4):
                blk = slice(p * CHUNK, (p + 1) * CHUNK)
                sb, gb = _silu_and_grad(bz_ref[rows, blk])
                dyb = dyc_s[rows, B_WIDTH + p * CHUNK:B_WIDTH + (p + 1) * CHUNK]
                o = jnp.where(lo, outs_h[p, 0], outs_h[p, 1])
                ycat_s[rows, B_WIDTH + p * CHUNK:B_WIDTH + (p + 1) * CHUNK] = (o * sb).astype(BF16)
                dproj_ref[rows, BZ_OFF + p * CHUNK:BZ_OFF + (p + 1) * CHUNK] = ((dyb * o) * gb).astype(BF16)
                dom[p] = _masked_halves((dyb * sb).astype(BF16), lo)
            dqs, dk_by_copy, dv_by_copy = {}, [], []
            for sw, members in enumerate(_HEADS_BY_COPY):
                do_stack = jnp.concatenate([dom[p][half] for p, half in members], axis=0)
                dp_stack = _dot_nt(do_stack, vs[sw])
                dsws = []
                for i, (p, half) in enumerate(members):
                    head = 2 * p + half
                    dp = _window_square(dp_stack[i * CHUNK:(i + 1) * CHUNK, :], prev)
                    prob = probs[p, half].astype(F32)
                    delta = jnp.sum(prob * dp, axis=-1, keepdims=True)
                    ds = prob * (dp - delta)
                    dbias_ref[head] += ds
                    dsws.append(_window_keys(ds.astype(BF16), prev))
                ds_stack = jnp.concatenate(dsws, axis=0)
                dq_stack = _dot(ds_stack, ks[sw])
                for i, m in enumerate(members):
                    dqs[m] = dq_stack[i * CHUNK:(i + 1) * CHUNK, :]
                dk_by_copy.append(_dot_tn(ds_stack, q_stacks[sw]))
                dv_by_copy.append(_dot_tn(p_stacks[sw], do_stack))
            for p in range(4):
                dq = jnp.where(lo, dqs[p, 0], dqs[p, 1]) * QK_SCALE
                dproj_ref[rows, Q_OFF + p * CHUNK:Q_OFF + (p + 1) * CHUNK] = dq.astype(BF16)
            both = pl.ds(r0, 2 * CHUNK)
            dkv_s[both, 0:KV_WIDTH] += dk_by_copy[0] + _swap_halves(dk_by_copy[1])
            dkv_s[both, KV_WIDTH:2 * KV_WIDTH] += dv_by_copy[0] + _swap_halves(dv_by_copy[1])
            return carry

        lax.fori_loop(0, n_chunks, chunk, 0, unroll=BWD_CHUNK_UNROLL)
        dwout_s[...] += _dot_tn(ycat_s[...], dy_s[...])
        dproj_ref[:, K_OFF:BZ_OFF] = dkv_s[CHUNK:CHUNK + tb, :].astype(BF16)
        carry_s[...] = dkv_s[0:CHUNK, :]

        @pl.when(step == n_tiles - 1)
        def _():
            row = lax.broadcasted_iota(jnp.int32, (2 * CHUNK, CHUNK), 0)
            col = lax.broadcasted_iota(jnp.int32, (2 * CHUNK, CHUNK), 1)
            causal = col <= jnp.where(row >= CHUNK, row - CHUNK, row)
            for p in range(4):
                dws_ref[p] = jnp.where(causal, dws_s[p], 0.0).astype(BF16)
            rows = stage_s.shape[0]
            for r0 in range(0, D_MODEL, rows):
                stage_s[...] = dwout_s[r0:r0 + rows, :].astype(BF16)
                pltpu.sync_copy(stage_s, dwout_ref.at[r0:r0 + rows, :])

        at_end()

    tile = lambda w: pl.BlockSpec((tb, w), lambda s: (n_tiles - 1 - s, 0))
    whole = lambda shape, **kw: pl.BlockSpec(shape, lambda s: (0,) * len(shape), **kw)
    once = dict(pipeline_mode=pl.Buffered(1))
    in_specs = [tile(D_MODEL), tile(D_MODEL), tile(Q_OFF), tile(B_WIDTH), tile(2 * KV_WIDTH),
                pl.BlockSpec((CHUNK, 2 * KV_WIDTH), lambda s: (jnp.maximum((n_tiles - 1 - s) * n_chunks - 1, 0), 0)),
                tile(B_WIDTH), tile(HEADS * CHUNK), whole((1, A_WIDTH)), whole((1, A_WIDTH)), whole((4, CHUNK, 2 * CHUNK)),
                whole((4, CHUNK, 2 * CHUNK)), whole((4, CHUNK, CHUNK)), whole((D_MODEL, D_MODEL), **once),
                whole((1, D_MODEL))]
    args = [dout, y, a, q, kv, kv, bz, probs, ln_g, ln_b, wcat, wtcat, bs, w_out, g_post]
    small_shapes = [jax.ShapeDtypeStruct((D_MODEL, D_MODEL), BF16), jax.ShapeDtypeStruct((4, 2 * CHUNK, CHUNK), BF16),
                    jax.ShapeDtypeStruct((CHUNK, A_WIDTH), F32), jax.ShapeDtypeStruct((1, A_WIDTH), F32),
                    jax.ShapeDtypeStruct((1, A_WIDTH), F32), jax.ShapeDtypeStruct((1, D_MODEL), F32),
                    jax.ShapeDtypeStruct((HEADS, CHUNK, CHUNK), F32)]
    small_specs = [ANY, whole((4, 2 * CHUNK, CHUNK)), whole((CHUNK, A_WIDTH)), whole((1, A_WIDTH)), whole((1, A_WIDTH)),
                   whole((1, D_MODEL)), whole((HEADS, CHUNK, CHUNK))]
    scratch = [pltpu.VMEM((tb, D_MODEL), BF16), pltpu.VMEM((tb, D_MODEL), BF16), pltpu.VMEM((tb, D_MODEL), F32),
               pltpu.VMEM((tb + CHUNK, 2 * KV_WIDTH), F32), pltpu.VMEM((CHUNK, 2 * KV_WIDTH), F32),
               pltpu.VMEM((D_MODEL, D_MODEL), F32), pltpu.VMEM((2 * CHUNK, D_MODEL), BF16),
               pltpu.VMEM((4, 2 * CHUNK, CHUNK), F32)]
    out_shape = [jax.ShapeDtypeStruct((n_tok, IN_WIDTH), BF16)] + small_shapes
    out_specs = [tile(IN_WIDTH)] + small_specs
    if exchange is not None:
        dwt, dwo = exchange
        in_specs += [ANY, ANY]
        args += [dwt, dwo]
        out_shape += [jax.ShapeDtypeStruct((N_DEV, dwt.shape[0] // N_CHIPS, D_MODEL), BF16),
                      jax.ShapeDtypeStruct((N_DEV, dwo.shape[0] // N_CHIPS, D_MODEL), BF16)]
        out_specs += [ANY, ANY]
    if allgather is not None:
        in_specs.append(ANY)
        args.append(allgather)
        out_shape.append(jax.ShapeDtypeStruct((N_DEV,) + allgather.shape, allgather.dtype))
        out_specs.append(ANY)
    for hosted in (exchange, allgather):
        if hosted is not None:
            scratch += _comm_scratch(N_EXCHANGE_SEMS)
    return pl.pallas_call(
        body, name="bwd_mix" + ("" if exchange is None and allgather is None else "_exchange"), grid=(n_tiles,),
        out_shape=tuple(out_shape), in_specs=in_specs, out_specs=tuple(out_specs), scratch_shapes=scratch,
        compiler_params=_cparams(1),
    )(*args)


def _bwd_in(dproj, x, dout, g_pre, w_in_t):
    n_tok = x.shape[0]
    tm = TOK_TILE
    n_tiles = n_tok // tm

    def body(dp_ref, x_ref, dout_ref, g_ref, wt_ref, dx_ref, dwt_ref, dg_ref, acc_s):
        i = pl.program_id(0)

        @pl.when(i == 0)
        def _():
            acc_s[...] = jnp.zeros_like(acc_s)
            dg_ref[...] = jnp.zeros_like(dg_ref)

        xf = x_ref[...]
        r = lax.rsqrt(jnp.mean(xf * xf, axis=-1, keepdims=True) + NORM_EPS)
        xn = xf * r
        h = (xn * g_ref[...]).astype(BF16)
        dp = dp_ref[...]
        dh = _dot(dp, wt_ref[...])
        acc_s[...] += _dot_tn(dp, h)
        dg_ref[...] += jnp.sum(dh * xn, axis=0, keepdims=True)
        dhn = dh * g_ref[...]
        dx_ref[...] = dout_ref[...] + r * (dhn - xn * jnp.mean(dhn * xn, axis=-1, keepdims=True))

        @pl.when(i == n_tiles - 1)
        def _():
            dwt_ref[...] = acc_s[...].astype(BF16)

    return pl.pallas_call(
        body, name="bwd_in", grid=(n_tiles,),
        out_shape=(jax.ShapeDtypeStruct((n_tok, D_MODEL), F32), jax.ShapeDtypeStruct((IN_WIDTH, D_MODEL), BF16),
                   jax.ShapeDtypeStruct((1, D_MODEL), F32)),
        in_specs=[pl.BlockSpec((tm, IN_WIDTH), lambda i: (i, 0)), pl.BlockSpec((tm, D_MODEL), lambda i: (i, 0)),
                  pl.BlockSpec((tm, D_MODEL), lambda i: (i, 0)), pl.BlockSpec((1, D_MODEL), lambda i: (0, 0)),
                  pl.BlockSpec((IN_WIDTH, D_MODEL), lambda i: (0, 0), pipeline_mode=pl.Buffered(1))],
        out_specs=(pl.BlockSpec((tm, D_MODEL), lambda i: (i, 0)),
                   pl.BlockSpec((IN_WIDTH, D_MODEL), lambda i: (0, 0), pipeline_mode=pl.Buffered(1)),
                   pl.BlockSpec((1, D_MODEL), lambda i: (0, 0))),
        scratch_shapes=[pltpu.VMEM((IN_WIDTH, D_MODEL), F32)],
        compiler_params=_cparams(1),
    )(dproj, x, dout, g_pre, w_in_t)


def _riders(exchanges):
    args = [a for arrays, _ in exchanges for a in arrays]
    shapes = [jax.ShapeDtypeStruct((N_DEV, a.shape[0] // N_CHIPS if by_chip else a.shape[0]) + a.shape[1:], a.dtype)
              for arrays, by_chip in exchanges for a in arrays]
    scratch = [s for _ in exchanges for s in _comm_scratch(N_EXCHANGE_SEMS)]

    def bind(in_refs, out_refs, sem_refs, step, n_steps, pass_on_step=None):
        in_refs, out_refs, sem_refs = list(in_refs), list(out_refs), list(sem_refs)
        starts, ends = [], []
        for arrays, by_chip in exchanges:
            n = len(arrays)
            phases = _exchange_phases(tuple(in_refs[:n]), tuple(out_refs[:n]), *sem_refs[:3], by_chip=by_chip)
            del in_refs[:n], out_refs[:n], sem_refs[:3]
            at_start, at_end = _hosted(phases, step, n_steps, pass_on_step)
            starts.append(at_start)
            ends.append(at_end)
        return (lambda: [f() for f in starts]), (lambda: [f() for f in ends])

    return args, [ANY] * len(args), shapes, [ANY] * len(shapes), scratch, bind


def _bwd_in_dw(dproj, x, g_pre, exchanges):
    n_tok = x.shape[0]
    tm = TOK_TILE
    n_tiles = n_tok // tm
    r_args, r_in_specs, r_shapes, r_out_specs, r_scratch, bind = _riders(exchanges)

    def body(*refs):
        dp_ref, x_ref, g_ref = refs[:3]
        in_refs = refs[3:3 + len(r_args)]
        dwt_ref = refs[3 + len(r_args)]
        out_refs = refs[4 + len(r_args):4 + 2 * len(r_args)]
        acc_s = refs[4 + 2 * len(r_args)]
        i = pl.program_id(0)
        at_start, at_end = bind(in_refs, out_refs, refs[5 + 2 * len(r_args):], i, n_tiles)
        at_start()

        @pl.when(i == 0)
        def _():
            acc_s[...] = jnp.zeros_like(acc_s)

        xf = x_ref[...]
        r = lax.rsqrt(jnp.mean(xf * xf, axis=-1, keepdims=True) + NORM_EPS)
        acc_s[...] += _dot_tn(dp_ref[...], ((xf * r) * g_ref[...]).astype(BF16))

        @pl.when(i == n_tiles - 1)
        def _():
            dwt_ref[...] = acc_s[...].astype(BF16)

        at_end()

    return pl.pallas_call(
        body, name="bwd_in_dw", grid=(n_tiles,),
        out_shape=(jax.ShapeDtypeStruct((IN_WIDTH, D_MODEL), BF16), *r_shapes),
        in_specs=[pl.BlockSpec((tm, IN_WIDTH), lambda i: (i, 0)), pl.BlockSpec((tm, D_MODEL), lambda i: (i, 0)),
                  pl.BlockSpec((1, D_MODEL), lambda i: (0, 0))] + r_in_specs,
        out_specs=(pl.BlockSpec((IN_WIDTH, D_MODEL), lambda i: (0, 0), pipeline_mode=pl.Buffered(1)), *r_out_specs),
        scratch_shapes=[pltpu.VMEM((IN_WIDTH, D_MODEL), F32)] + r_scratch,
        compiler_params=_cparams(1),
    )(dproj, x, g_pre, *r_args)


def _bwd_in_dx(dproj, x, dout, g_pre, w_in_t, exchanges):
    n_tok = x.shape[0]
    tm = TOK_TILE
    n_tiles = n_tok // tm
    r_args, r_in_specs, r_shapes, r_out_specs, r_scratch, bind = _riders(exchanges)

    def body(*refs):
        dp_ref, x_ref, dout_ref, g_ref, wt_ref = refs[:5]
        in_refs = refs[5:5 + len(r_args)]
        dx_ref, dg_ref = refs[5 + len(r_args):7 + len(r_args)]
        out_refs = refs[7 + len(r_args):7 + 2 * len(r_args)]
        i = pl.program_id(0)
        at_start, at_end = bind(in_refs, out_refs, refs[7 + 2 * len(r_args):], i, n_tiles, pass_on_step=n_tiles - 2)
        at_start()

        @pl.when(i == 0)
        def _():
            dg_ref[...] = jnp.zeros_like(dg_ref)

        xf = x_ref[...]
        r = lax.rsqrt(jnp.mean(xf * xf, axis=-1, keepdims=True) + NORM_EPS)
        xn = xf * r
        dh = _dot(dp_ref[...], wt_ref[...])
        dg_ref[...] += jnp.sum(dh * xn, axis=0, keepdims=True)
        dhn = dh * g_ref[...]
        dx_ref[...] = dout_ref[...] + r * (dhn - xn * jnp.mean(dhn * xn, axis=-1, keepdims=True))
        at_end()

    tile = lambda w: pl.BlockSpec((tm, w), lambda i: (i, 0))
    return pl.pallas_call(
        body, name="bwd_in_dx", grid=(n_tiles,),
        out_shape=(jax.ShapeDtypeStruct((n_tok, D_MODEL), F32), jax.ShapeDtypeStruct((1, D_MODEL), F32), *r_shapes),
        in_specs=[tile(IN_WIDTH), tile(D_MODEL), tile(D_MODEL), pl.BlockSpec((1, D_MODEL), lambda i: (0, 0)),
                  pl.BlockSpec((IN_WIDTH, D_MODEL), lambda i: (0, 0), pipeline_mode=pl.Buffered(1))] + r_in_specs,
        out_specs=(tile(D_MODEL), pl.BlockSpec((1, D_MODEL), lambda i: (0, 0)), *r_out_specs),
        scratch_shapes=r_scratch,
        compiler_params=_cparams(1),
    )(dproj, x, dout, g_pre, w_in_t, *r_args)


def _adam_update(w, g, m, v):
    nm = ADAM_B1 * m + (1.0 - ADAM_B1) * g
    nv = ADAM_B2 * v + (1.0 - ADAM_B2) * (g * g)
    m_hat = nm / (1.0 - ADAM_B1 ** ADAM_STEP)
    v_hat = nv / (1.0 - ADAM_B2 ** ADAM_STEP)
    return -ADAM_LR * (m_hat / (jnp.sqrt(v_hat) + ADAM_EPS) + ADAM_WD * w), nm, nv


def _sum_slots(p_ref):
    tot = p_ref[0].astype(F32)
    for d in range(1, N_DEV):
        tot = tot + p_ref[d].astype(F32)
    return tot


def _adamw_parts(parts, w, m, v, name):
    n_layers, n_rows, n_cols = w.shape
    tr = n_rows
    while tr * n_cols > ADAMW_BLOCK_ELEMS:
        tr //= 2
    n_blocks = n_rows // tr

    def body(*refs):
        p_refs = refs[:n_layers]
        w_ref, m_ref, v_ref, g_ref, d_ref, nm_ref, nv_ref = refs[n_layers:]
        for k in range(n_layers):
            @pl.when(pl.program_id(0) == k)
            def _(k=k):
                g = _sum_slots(p_refs[k])
                g_ref[0] = g
                d_ref[0], nm_ref[0], nv_ref[0] = _adam_update(w_ref[0], g, m_ref[0], v_ref[0])

    def part_spec(k):
        return pl.BlockSpec((N_DEV, tr, n_cols), lambda l, i: (0, jnp.clip(i + (l - k) * n_blocks, 0, n_blocks - 1), 0))

    spec = pl.BlockSpec((1, tr, n_cols), lambda l, i: (l, i, 0))
    shape = jax.ShapeDtypeStruct((n_layers, n_rows, n_cols), F32)
    return pl.pallas_call(
        body, name="adamw_" + name, grid=(n_layers, n_blocks), out_shape=(shape,) * 4,
        in_specs=[part_spec(k) for k in range(n_layers)] + [spec, spec, spec], out_specs=(spec,) * 4,
        compiler_params=_cparams(2),
    )(*parts, w, m, v)


def _adamw_small(quads):
    n = len(quads)

    def body(*refs):
        ins, outs = refs[:4 * n], refs[4 * n:]
        for j in range(n):
            w_ref, g_ref, m_ref, v_ref = ins[4 * j:4 * j + 4]
            d_ref, nm_ref, nv_ref = outs[3 * j:3 * j + 3]
            d_ref[...], nm_ref[...], nv_ref[...] = _adam_update(w_ref[...], g_ref[...], m_ref[...], v_ref[...])

    vmem = pl.BlockSpec(memory_space=pltpu.VMEM)
    shapes = tuple(jax.ShapeDtypeStruct(w.shape, F32) for w, *_ in quads for _ in range(3))
    results = pl.pallas_call(
        body, name="adamw_small", out_shape=shapes, in_specs=[vmem] * (4 * n), out_specs=(vmem,) * (3 * n),
    )(*[a for quad in quads for a in quad])
    return [results[3 * j:3 * j + 3] for j in range(n)]


_SMALL = ("pre_norm_g", "ln_v_g", "ln_v_b", "b_spatial", "sinks", "rel_bias", "post_norm_g")


def _pack(pieces):
    blocks, first_rows, n = [], [], 0
    for p in pieces:
        flat = p.reshape(-1)
        flat = jnp.concatenate([flat, jnp.zeros(((-flat.shape[0]) % (8 * 128),), F32)]).reshape(-1, 128)
        blocks.append(flat)
        first_rows.append(n)
        n += flat.shape[0]
    return jnp.concatenate(blocks, axis=0), first_rows


def _unpack(block, first_row, shape):
    size = math.prod(shape)
    return block[first_row:first_row + -(-size // 128)].reshape(-1)[:size].reshape(shape)


def kernel(x, pre_norm_g, w_in, ln_v_g, ln_v_b, w_spatial, b_spatial, sinks, rel_bias, w_out, post_norm_g, loss_target, m_pre_norm_g, m_w_in, m_ln_v_g, m_ln_v_b, m_w_spatial, m_b_spatial, m_sinks, m_rel_bias, m_w_out, m_post_norm_g, v_pre_norm_g, v_w_in, v_ln_v_g, v_ln_v_b, v_w_spatial, v_b_spatial, v_sinks, v_rel_bias, v_w_out, v_post_norm_g):
    weights = dict(pre_norm_g=pre_norm_g, w_in=w_in, ln_v_g=ln_v_g, ln_v_b=ln_v_b, w_spatial=w_spatial, b_spatial=b_spatial,
                   sinks=sinks, rel_bias=rel_bias, w_out=w_out, post_norm_g=post_norm_g)
    mom_m = dict(pre_norm_g=m_pre_norm_g, w_in=m_w_in, ln_v_g=m_ln_v_g, ln_v_b=m_ln_v_b, w_spatial=m_w_spatial,
                 b_spatial=m_b_spatial, sinks=m_sinks, rel_bias=m_rel_bias, w_out=m_w_out, post_norm_g=m_post_norm_g)
    mom_v = dict(pre_norm_g=v_pre_norm_g, w_in=v_w_in, ln_v_g=v_ln_v_g, ln_v_b=v_ln_v_b, w_spatial=v_w_spatial,
                 b_spatial=v_b_spatial, sinks=v_sinks, rel_bias=v_rel_bias, w_out=v_w_out, post_norm_g=v_post_norm_g)
    n_seq, seq_len, _ = x.shape
    n_layers = w_in.shape[0]
    x2 = x.reshape(n_seq * seq_len, D_MODEL)
    target2 = loss_target.reshape(n_seq * seq_len, D_MODEL)
    row = lambda p, l: p[l][None]

    wt_shards = jnp.swapaxes(w_in, 1, 2).astype(BF16)
    wo_shards = w_out.astype(BF16)
    bucket = _window_buckets()
    bias = _bias_table(rel_bias, bucket)
    wcat, wtcat, bs = _spatial_tables(w_spatial, jnp.swapaxes(b_spatial, 1, 2))

    wt, wo = [None] * n_layers, [None] * n_layers
    wt[0], wo[0] = _gather_weights(wt_shards, wo_shards, 0)
    xs, saved = [x2], []
    for l in range(n_layers):
        layer_args = (xs[-1], row(pre_norm_g, l), wt[l], row(ln_v_g, l), row(ln_v_b, l), wcat[l], bs[l], sinks[l], bias,
                      wo[l], row(post_norm_g, l), seq_len)
        if l + 1 < n_layers:
            *acts, xn, y, wt[l + 1], wo[l + 1] = _fwd_layer(*layer_args, gather=(wt_shards, wo_shards, l + 1))
            xs.append(xn)
        else:
            *acts, dx, y, loss = _fwd_layer(*layer_args, target=target2)
        saved.append((y, *acts))

    small = [None] * n_layers
    parts = [None] * n_layers
    waiting = None
    for l in reversed(range(n_layers)):
        hosted = {} if waiting is None else dict(exchange=waiting[:2], allgather=waiting[2])
        dproj, *rest = _bwd_mix(dx, *saved[l], row(ln_v_g, l), row(ln_v_b, l), wcat[l], wtcat[l], bs[l], wo[l],
                                row(post_norm_g, l), **hosted)
        dwo, dws, dmix, dlng, dlnb, dgpost, dbias = rest[:7]
        dws = dws.reshape(HEADS * CHUNK, CHUNK)
        if waiting is not None:
            parts[l + 1] = list(rest[7:])
        if l > 0:
            dx, dwt, dgpre = _bwd_in(dproj, xs[l], dx, row(pre_norm_g, l), wt[l])
            waiting = (dwt, dwo, dws)
        else:
            dwt, po, ws_all = _bwd_in_dw(dproj, xs[l], row(pre_norm_g, l), [((dwo,), True), ((dws,), False)])
            dx, dgpre, pt = _bwd_in_dx(dproj, xs[l], dx, row(pre_norm_g, l), wt[l], [((dwt,), True)])
            parts[l] = [pt, po, ws_all]
        small[l] = dict(pre_norm_g=dgpre[0], ln_v_g=dlng[0], ln_v_b=dlnb[0], dmix=dmix, dbias=dbias,
                        post_norm_g=dgpost[0])

    db = _b_spatial_grad([s["dmix"] for s in small])
    pieces = {n: jnp.stack([s[n] for s in small]) for n in ("pre_norm_g", "ln_v_g", "ln_v_b", "post_norm_g")}
    pieces["b_spatial"] = jnp.swapaxes(db[:, :, :HEADS], 1, 2)
    pieces["rel_bias"], pieces["sinks"] = _rel_bias_grad([s["dbias"] for s in small], bucket)
    packed, first_rows = _pack([pieces[n] for n in _SMALL] + [loss])
    total = _allreduce_small(packed)
    grad = {n: _unpack(total, r, weights[n].shape) for n, r in zip(_SMALL, first_rows)}
    loss_out = total[first_rows[-1], 0]

    delta, new_m, new_v = {}, {}, {}
    two_d = lambda p: p.reshape(-1, p.shape[-1])
    updates = _adamw_small([tuple(two_d(p[n]) for p in (weights, grad, mom_m, mom_v)) for n in _SMALL])
    for n, update in zip(_SMALL, updates):
        delta[n], new_m[n], new_v[n] = (u.reshape(weights[n].shape) for u in update)

    t3 = lambda p: jnp.swapaxes(p, 1, 2)
    flat3 = lambda p: p.reshape(n_layers, HEADS * CHUNK, CHUNK)
    for n, which, view, back in (("w_in", 0, t3, t3), ("w_out", 1, lambda p: p, lambda p: p),
                                 ("w_spatial", 2, flat3, lambda p: p.reshape(w_spatial.shape))):
        results = _adamw_parts([parts[l][which] for l in range(n_layers)], view(weights[n]), view(mom_m[n]),
                               view(mom_v[n]), n)
        grad[n], delta[n], new_m[n], new_v[n] = (back(r) for r in results)

    names = tuple(weights)
    return (loss_out, dx.reshape(x.shape), *[grad[n] for n in names], *[delta[n] for n in names],
            *[new_m[n] for n in names], *[new_v[n] for n in names])
```

```python
import math

import jax
import jax.numpy as jnp
from jax import lax
from jax.experimental import pallas as pl
from jax.experimental.pallas import tpu as pltpu

F32 = jnp.float32
BF16 = jnp.bfloat16

D_MODEL = 1024
A_WIDTH = 512
B_WIDTH = 512
KV_WIDTH = 128
IN_WIDTH = 3 * A_WIDTH + 2 * B_WIDTH + 2 * KV_WIDTH
CHUNK = 128
HEADS = 8
HEAD_DIM = 64
REL_BUCKETS = 32
NORM_EPS = 1e-6
NEG = -1e30
Q_OFF = 3 * A_WIDTH
K_OFF = Q_OFF + B_WIDTH
BZ_OFF = K_OFF + 2 * KV_WIDTH
QK_SCALE = HEAD_DIM ** -0.5

ADAM_LR = 0.001
ADAM_B1 = 0.9
ADAM_B2 = 0.999
ADAM_EPS = 1e-08
ADAM_WD = 0.01
ADAM_STEP = 10

TOK_TILE = 512
CHUNK_UNROLL = 4
BWD_CHUNK_UNROLL = 4
ADAMW_BLOCK_ELEMS = 192 * 1024
VMEM_LIMIT_V7X = 60 * 1024 * 1024

N_DEV = 8
N_CHIPS = 4
MESH_ID = pl.DeviceIdType.MESH
ANY = pl.BlockSpec(memory_space=pl.ANY)
SMEM = pl.BlockSpec(memory_space=pltpu.SMEM)


def _cparams(n_axes):
    return pltpu.CompilerParams(dimension_semantics=("arbitrary",) * n_axes, vmem_limit_bytes=VMEM_LIMIT_V7X)


_GELU_C = math.sqrt(2.0 / math.pi)
_GELU_C3 = _GELU_C * 0.044715


def _gelu(x):
    t = jnp.tanh(x * (_GELU_C3 * (x * x) + _GELU_C))
    return x * (0.5 * t + 0.5)


def _gelu_and_grad(x):
    x2 = x * x
    t = jnp.tanh(x * (_GELU_C3 * x2 + _GELU_C))
    cdf = 0.5 * t + 0.5
    d = cdf + (x * (cdf * (1.0 - cdf))) * ((6.0 * _GELU_C3) * x2 + 2.0 * _GELU_C)
    return x * cdf, d


def _sigmoid(x):
    return 0.5 + 0.5 * jnp.tanh(0.5 * x)


def _silu(x):
    return x * _sigmoid(x)


def _silu_and_grad(x):
    s = _sigmoid(x)
    return x * s, s * (1.0 + x * (1.0 - s))


def _dot(a, b):
    return jnp.dot(a, b, preferred_element_type=F32)


def _dot_nt(a, b):
    return lax.dot_general(a, b, (((1,), (1,)), ((), ())), preferred_element_type=F32)


def _dot_tn(a, b):
    return lax.dot_general(a, b, (((0,), (0,)), ((), ())), preferred_element_type=F32)


def _lo_mask(shape):
    return lax.broadcasted_iota(jnp.int32, shape, 1) < HEAD_DIM


def _swap_halves(v):
    return pltpu.roll(v, HEAD_DIM, 1)


def _window_buckets():
    q_loc = jnp.arange(CHUNK)[:, None]
    j_loc = jnp.arange(CHUNK)[None, :]
    d = q_loc - j_loc + jnp.where(j_loc > q_loc, CHUNK, 0)
    max_exact = REL_BUCKETS // 2
    safe = jnp.maximum(d, 1).astype(F32)
    large = max_exact + (jnp.log(safe / max_exact) / math.log(CHUNK / max_exact)
                         * (REL_BUCKETS - max_exact)).astype(jnp.int32)
    large = jnp.minimum(large, REL_BUCKETS - 1)
    return jnp.where(d < max_exact, d, large).astype(jnp.int32)


def _bias_table(rel_bias, bucket):
    def body(rb_ref, bk_ref, out_ref):
        bk = bk_ref[...]
        for h in range(HEADS):
            acc = jnp.zeros(bk.shape, F32)
            for b in range(REL_BUCKETS):
                acc = jnp.where(bk == b, rb_ref[b, h], acc)
            out_ref[h] = acc

    vmem = pl.BlockSpec(memory_space=pltpu.VMEM)
    return pl.pallas_call(
        body, name="bias_table",
        out_shape=jax.ShapeDtypeStruct((HEADS, CHUNK, CHUNK), F32),
        in_specs=[SMEM, vmem], out_specs=vmem,
    )(rel_bias, bucket)


def _rel_bias_grad(dbias, bucket):
    n_layers = len(dbias)

    def body(*refs):
        db_refs, (bk_ref, out_ref, dsink_ref) = refs[:n_layers], refs[n_layers:]
        bk = bk_ref[...]
        for h in range(HEADS):
            tot = jnp.zeros((CHUNK, CHUNK), F32)
            for l in range(n_layers):
                ds = db_refs[l][h]
                dsink_ref[l, h] = -jnp.sum(ds)
                tot = tot + ds
            for b in range(REL_BUCKETS):
                out_ref[b, h] = jnp.sum(jnp.where(bk == b, tot, 0.0))

    vmem = pl.BlockSpec(memory_space=pltpu.VMEM)
    return pl.pallas_call(
        body, name="rel_bias_grad",
        out_shape=(jax.ShapeDtypeStruct((REL_BUCKETS, HEADS), F32), jax.ShapeDtypeStruct((n_layers, HEADS), F32)),
        in_specs=[vmem] * (n_layers + 1), out_specs=(SMEM, SMEM),
    )(*dbias, bucket)


def _spatial_tables(w_spatial, b_spatial_t):
    n_layers = w_spatial.shape[0]

    def body(w_ref, b_ref, wcat_ref, wtcat_ref, bs_ref):
        row = lax.broadcasted_iota(jnp.int32, (CHUNK, CHUNK), 0)
        col = lax.broadcasted_iota(jnp.int32, (CHUNK, CHUNK), 1)
        causal = col <= row
        lo = _lo_mask((CHUNK, CHUNK))
        for p in range(4):
            for half in range(2):
                w = jnp.where(causal, w_ref[0, 2 * p + half], 0.0)
                wcat_ref[0, p, :, half * CHUNK:(half + 1) * CHUNK] = w.astype(BF16)
                wtcat_ref[0, p, :, half * CHUNK:(half + 1) * CHUNK] = w.T.astype(BF16)
            b = b_ref[0]
            bs_ref[0, p] = jnp.where(lo, b[:, 2 * p:2 * p + 1], b[:, 2 * p + 1:2 * p + 2])

    return pl.pallas_call(
        body, name="spatial_tables", grid=(n_layers,),
        out_shape=(jax.ShapeDtypeStruct((n_layers, 4, CHUNK, 2 * CHUNK), BF16),
                   jax.ShapeDtypeStruct((n_layers, 4, CHUNK, 2 * CHUNK), BF16),
                   jax.ShapeDtypeStruct((n_layers, 4, CHUNK, CHUNK), F32)),
        in_specs=[pl.BlockSpec((1, HEADS, CHUNK, CHUNK), lambda l: (l, 0, 0, 0)),
                  pl.BlockSpec((1, CHUNK, HEADS), lambda l: (l, 0, 0))],
        out_specs=(pl.BlockSpec((1, 4, CHUNK, 2 * CHUNK), lambda l: (l, 0, 0, 0)),
                   pl.BlockSpec((1, 4, CHUNK, 2 * CHUNK), lambda l: (l, 0, 0, 0)),
                   pl.BlockSpec((1, 4, CHUNK, CHUNK), lambda l: (l, 0, 0, 0))),
        compiler_params=_cparams(1),
    )(w_spatial, b_spatial_t)


def _b_spatial_grad(dmix):
    n_layers = len(dmix)

    def body(*refs):
        d_refs, out_ref = refs[:n_layers], refs[n_layers]
        lane = lax.broadcasted_iota(jnp.int32, (CHUNK, CHUNK), 1)
        for l in range(n_layers):
            acc = jnp.zeros((CHUNK, CHUNK), F32)
            for p in range(4):
                t = d_refs[l][:, p * CHUNK:(p + 1) * CHUNK]
                s_lo = jnp.sum(jnp.where(lane < HEAD_DIM, t, 0.0), axis=1, keepdims=True)
                s_hi = jnp.sum(jnp.where(lane < HEAD_DIM, 0.0, t), axis=1, keepdims=True)
                acc = jnp.where(lane == 2 * p, s_lo, acc)
                acc = jnp.where(lane == 2 * p + 1, s_hi, acc)
            out_ref[l] = acc

    vmem = pl.BlockSpec(memory_space=pltpu.VMEM)
    return pl.pallas_call(
        body, name="b_spatial_grad",
        out_shape=jax.ShapeDtypeStruct((n_layers, CHUNK, CHUNK), F32),
        in_specs=[vmem] * n_layers, out_specs=vmem,
    )(*dmix)


def _place():
    x, y, c = lax.axis_index("x"), lax.axis_index("y"), lax.axis_index("c")
    other_chips = [(1 - x, y), (x, 1 - y), (1 - x, 1 - y)]
    return x, y, c, other_chips


N_GATHER_SEMS = 12


def _gather_phases(shards, fulls, send_sems, recv_sems, local_sems):
    x, y, c, chips = _place()
    sibling = (x, y, 1 - c)
    n_arr = len(shards)

    def half_rows(a, chip, half):
        n = shards[a].shape[0]
        start = (2 * chip[0] + chip[1]) * n + half * (n // 2)
        return fulls[a].at[pl.ds(pl.multiple_of(start, 16), n // 2), :]

    def my_half(a):
        n = shards[a].shape[0]
        return shards[a].at[pl.ds(pl.multiple_of(c * (n // 2), 16), n // 2), :]

    def copy(k, a, src, chip, half, to):
        return pltpu.make_async_remote_copy(
            src_ref=src, dst_ref=half_rows(a, chip, half), send_sem=send_sems.at[n_arr * k + a],
            recv_sem=recv_sems.at[n_arr * k + a], device_id=to, device_id_type=MESH_ID)

    def local(a):
        n = shards[a].shape[0]
        mine = fulls[a].at[pl.ds(pl.multiple_of((2 * x + y) * n, 16), n), :]
        return pltpu.make_async_copy(shards[a], mine, local_sems.at[a])

    def first(k, a):
        return copy(k, a, my_half(a), (x, y), c, (chips[k][0], chips[k][1], c))

    def passed(k, a):
        return copy(3 + k, a, half_rows(a, chips[k], c), chips[k], c, sibling)

    def phase_a():
        for a in range(n_arr):
            local(a).start()
        for k in range(3):
            for a in range(n_arr):
                first(k, a).start()

    def phase_b():
        for k in range(3):
            for a in range(n_arr):
                copy(k, a, my_half(a), chips[k], c, sibling).wait_recv()
                passed(k, a).start()

    def phase_c():
        for k in range(3):
            for a in range(n_arr):
                copy(3 + k, a, my_half(a), chips[k], 1 - c, sibling).wait_recv()
        for k in range(3):
            for a in range(n_arr):
                first(k, a).wait_send()
                passed(k, a).wait_send()
        for a in range(n_arr):
            local(a).wait()

    return phase_a, phase_b, phase_c


N_EXCHANGE_SEMS = 14


def _exchange_phases(partials, parts, send_sems, recv_sems, local_sems, by_chip=True):
    x, y, c, chips = _place()
    me, sibling = (x, y, c), (x, y, 1 - c)
    n_arr = len(partials)

    def block(a, chip):
        if not by_chip:
            return partials[a]
        n = partials[a].shape[0] // N_CHIPS
        return partials[a].at[pl.ds(pl.multiple_of((2 * chip[0] + chip[1]) * n, 16), n), :]

    def slot(a, dev):
        return parts[a].at[4 * dev[0] + 2 * dev[1] + dev[2]]

    def copy(k, a, src, origin, to):
        return pltpu.make_async_remote_copy(
            src_ref=src, dst_ref=slot(a, origin), send_sem=send_sems.at[n_arr * k + a],
            recv_sem=recv_sems.at[n_arr * k + a], device_id=to, device_id_type=MESH_ID)

    def local(a):
        return pltpu.make_async_copy(block(a, (x, y)), slot(a, me), local_sems.at[a])

    def first(k, a):
        if k == 0:
            return copy(0, a, block(a, (x, y)), me, sibling)
        chip = chips[k - 1]
        return copy(k, a, block(a, chip), me, (chip[0], chip[1], c))

    def passed(k, a):
        origin = (chips[k][0], chips[k][1], c)
        return copy(4 + k, a, slot(a, origin), origin, sibling)

    def phase_a():
        for a in range(n_arr):
            local(a).start()
        for k in range(4):
            for a in range(n_arr):
                first(k, a).start()

    def phase_b():
        for k in range(3):
            for a in range(n_arr):
                copy(1 + k, a, block(a, (x, y)), (chips[k][0], chips[k][1], c), me).wait_recv()
                passed(k, a).start()

    def phase_c():
        for a in range(n_arr):
            copy(0, a, block(a, (x, y)), sibling, me).wait_recv()
        for k in range(3):
            for a in range(n_arr):
                copy(4 + k, a, block(a, (x, y)), (chips[k][0], chips[k][1], 1 - c), me).wait_recv()
        for k in range(4):
            for a in range(n_arr):
                first(k, a).wait_send()
        for k in range(3):
            for a in range(n_arr):
                passed(k, a).wait_send()
        for a in range(n_arr):
            local(a).wait()

    return phase_a, phase_b, phase_c


def _comm_scratch(n_sems):
    return [pltpu.SemaphoreType.DMA((n_sems,)), pltpu.SemaphoreType.DMA((n_sems,)), pltpu.SemaphoreType.DMA((2,))]


def _gather_weights(wt_shards, wo_shards, layer):
    wt_rows, wo_rows = wt_shards.shape[1], wo_shards.shape[1]

    def body(wt_ref, wo_ref, wt_full, wo_full, send_sems, recv_sems, local_sems):
        phases = _gather_phases((wt_ref.at[layer], wo_ref.at[layer]), (wt_full, wo_full), send_sems, recv_sems, local_sems)
        for phase in phases:
            phase()

    return pl.pallas_call(
        body, name="gather_weights",
        out_shape=(jax.ShapeDtypeStruct((N_CHIPS * wt_rows, D_MODEL), BF16),
                   jax.ShapeDtypeStruct((N_CHIPS * wo_rows, D_MODEL), BF16)),
        in_specs=[ANY, ANY], out_specs=(ANY, ANY), scratch_shapes=_comm_scratch(N_GATHER_SEMS),
    )(wt_shards, wo_shards)


def _allreduce_small(part):
    n_rows = part.shape[0]

    def body(p_ref, tot_ref, all_ref, send_sems, recv_sems, local_sem):
        x, y, c, chips = _place()
        me, sibling = (x, y, c), (x, y, 1 - c)

        def rows(dev):
            return all_ref.at[pl.ds(pl.multiple_of((4 * dev[0] + 2 * dev[1] + dev[2]) * n_rows, 8), n_rows), :]

        def copy(k, origin, to, src=None):
            return pltpu.make_async_remote_copy(
                src_ref=rows(origin) if src is None else src, dst_ref=rows(origin), send_sem=send_sems.at[k],
                recv_sem=recv_sems.at[k], device_id=to, device_id_type=MESH_ID)

        mine = pltpu.make_async_copy(p_ref, rows(me), local_sem)
        mine.start()
        first = [copy(0, me, sibling, src=p_ref)]
        first += [copy(1 + k, me, (chip[0], chip[1], c), src=p_ref) for k, chip in enumerate(chips)]
        for cp in first:
            cp.start()
        passed = []
        for k, chip in enumerate(chips):
            origin = (chip[0], chip[1], c)
            copy(1 + k, origin, me).wait_recv()
            fwd = copy(4 + k, origin, sibling)
            fwd.start()
            passed.append(fwd)
        copy(0, sibling, me).wait_recv()
        for k, chip in enumerate(chips):
            copy(4 + k, (chip[0], chip[1], 1 - c), me).wait_recv()
        for cp in first + passed:
            cp.wait_send()
        mine.wait()
        tot = all_ref[0:n_rows, :]
        for d in range(1, N_DEV):
            tot = tot + all_ref[d * n_rows:(d + 1) * n_rows, :]
        tot_ref[...] = tot

    vmem = pl.BlockSpec(memory_space=pltpu.VMEM)
    return pl.pallas_call(
        body, name="allreduce_small",
        out_shape=jax.ShapeDtypeStruct((n_rows, 128), F32),
        in_specs=[vmem], out_specs=vmem,
        scratch_shapes=[pltpu.VMEM((N_DEV * n_rows, 128), F32), pltpu.SemaphoreType.DMA((7,)),
                        pltpu.SemaphoreType.DMA((7,)), pltpu.SemaphoreType.DMA],
        compiler_params=pltpu.CompilerParams(vmem_limit_bytes=VMEM_LIMIT_V7X),
    )(part)


def _hosted(phases, step, n_steps, pass_on_step=None):
    phase_a, phase_b, phase_c = phases
    if pass_on_step is None:
        pass_on_step = (3 * n_steps) // 4

    def at_start():
        pl.when(step == 0)(phase_a)

    def at_end():
        pl.when(step == pass_on_step)(phase_b)
        pl.when(step == n_steps - 1)(phase_c)

    return at_start, at_end


def _layer_norm_stats(vv):
    mu = jnp.mean(vv, axis=-1, keepdims=True)
    xc = vv - mu
    rs = lax.rsqrt(jnp.mean(xc * xc, axis=-1, keepdims=True) + NORM_EPS)
    return xc * rs, mu, rs


STAT_POST_RMS, STAT_LN_MEAN, STAT_LN_RSTD = (slice(k * CHUNK, (k + 1) * CHUNK) for k in range(3))
N_STATS = 3


def _over_lanes(stat, width):
    return jnp.concatenate([stat] * (width // CHUNK), axis=1)


def _blockdiag(v, lo):
    zero = jnp.zeros_like(v)
    return jnp.concatenate([jnp.where(lo, v, zero), jnp.where(lo, zero, v)], axis=0)


def _softmax_sink(s, sink):
    m = jnp.maximum(jnp.max(s, axis=-1, keepdims=True), sink)
    e = jnp.exp(s - m)
    esink = jnp.exp(sink - m)
    den = jnp.sum(e, axis=-1, keepdims=True) + esink
    return e / den


def _kv_rows(cur_ref, halo_ref, r0, c):
    prev_in_tile = cur_ref[pl.ds(pl.multiple_of(jnp.maximum(r0 - CHUNK, 0), CHUNK), CHUNK), :]
    prev = jnp.where(c == 0, halo_ref[...], prev_in_tile)
    kv2 = jnp.concatenate([prev, cur_ref[pl.ds(r0, CHUNK), :]], axis=0)
    k2, v2 = kv2[:, 0:KV_WIDTH], kv2[:, KV_WIDTH:2 * KV_WIDTH]
    return (k2, _swap_halves(k2)), (v2, _swap_halves(v2))


def _window_square(over_keys, prev):
    return jnp.where(prev, over_keys[:, 0:CHUNK], over_keys[:, CHUNK:2 * CHUNK])


def _window_keys(square, prev):
    zero = jnp.zeros_like(square)
    return jnp.concatenate([jnp.where(prev, square, zero), jnp.where(prev, zero, square)], axis=1)


def _ahead(shape=(CHUNK, CHUNK)):
    return lax.broadcasted_iota(jnp.int32, shape, 1) - lax.broadcasted_iota(jnp.int32, shape, 0)


def _dead_mask(first, ahead):
    return ahead > jnp.where(first, 0, CHUNK)


def _head_of(p, half):
    return 2 * p + half, int(half != p // 2)


_HEADS_BY_COPY = tuple(tuple((p, half) for p in range(4) for half in range(2) if _head_of(p, half)[1] == sw)
                       for sw in range(2))


def _masked_halves(tile, lo):
    zero = jnp.zeros_like(tile)
    return {0: jnp.where(lo, tile, zero), 1: jnp.where(lo, zero, tile)}


def _query_stacks(q_tiles, lo):
    qm = {p: _masked_halves(q_tiles[p] * QK_SCALE, lo) for p in range(4)}
    return [jnp.concatenate([qm[p][half] for p, half in members], axis=0) for members in _HEADS_BY_COPY]


def _attention_probs(q_stacks, ks, bias_ref, sink_ref, dead, prev):
    probs = {}
    for sw, members in enumerate(_HEADS_BY_COPY):
        s_stack = _dot_nt(q_stacks[sw], ks[sw])
        for i, (p, half) in enumerate(members):
            head = 2 * p + half
            s = _window_square(s_stack[i * CHUNK:(i + 1) * CHUNK, :], prev) + bias_ref[head]
            s = jnp.where(dead, NEG, s)
            probs[p, half] = _softmax_sink(s, sink_ref[head]).astype(BF16)
    return probs


def _attention_values(probs, vs, prev):
    outs, p_stacks = {}, []
    for sw, members in enumerate(_HEADS_BY_COPY):
        p_stack = jnp.concatenate([_window_keys(probs[m], prev) for m in members], axis=0)
        r_stack = _dot(p_stack, vs[sw])
        p_stacks.append(p_stack)
        for i, m in enumerate(members):
            outs[m] = r_stack[i * CHUNK:(i + 1) * CHUNK, :]
    return outs, p_stacks


def _fwd_layer(x, g_pre, w_in_t, ln_g, ln_b, wcat, bs, sinks, bias, w_out, g_post, seq_len, gather=None, target=None):
    n_tok = x.shape[0]
    tb = TOK_TILE
    n_chunks = tb // CHUNK
    n_tiles = n_tok // tb
    n_in = 11
    assert gather is None or target is None

    def body(*refs):
        (x_ref, gpre_ref, wt_ref, lng_ref, lnb_ref, wcat_ref, bs_ref, sink_ref, bias_ref, wout_ref,
         gpost_ref) = refs[:n_in]
        at_start = at_end = lambda: None
        if gather is not None:
            (wts_ref, wos_ref, a_ref, q_ref, kv_ref, bz_ref, probs_ref, stats_ref, xn_ref, y_ref, wt_full, wo_full, ycat_s,
             halo_ref, send_sems, recv_sems, local_sems) = refs[n_in:]
            phases = _gather_phases((wts_ref.at[gather[2]], wos_ref.at[gather[2]]), (wt_full, wo_full), send_sems,
                                    recv_sems, local_sems)
            at_start, at_end = _hosted(phases, pl.program_id(0), n_tiles)
        elif target is not None:
            (target_ref, a_ref, q_ref, kv_ref, bz_ref, probs_ref, stats_ref, xn_ref, y_ref, loss_ref, ycat_s,
             halo_ref) = refs[n_in:]
        else:
            a_ref, q_ref, kv_ref, bz_ref, probs_ref, stats_ref, xn_ref, y_ref, ycat_s, halo_ref = refs[n_in:]
        at_start()
        i = pl.program_id(0)
        lo = _lo_mask((CHUNK, CHUNK))
        ahead = _ahead()
        prev = ahead > 0

        @pl.when(i == 0)
        def _():
            halo_ref[...] = jnp.zeros_like(halo_ref)

        xf = x_ref[...]
        r1 = lax.rsqrt(jnp.mean(xf * xf, axis=-1, keepdims=True) + NORM_EPS)
        h = ((xf * r1) * gpre_ref[...]).astype(BF16)
        a_ref[...] = _dot_nt(h, wt_ref[0:Q_OFF, :])
        q_ref[...] = _dot_nt(h, wt_ref[Q_OFF:K_OFF, :]).astype(BF16)
        kv_ref[...] = _dot_nt(h, wt_ref[K_OFF:BZ_OFF, :]).astype(BF16)
        bz_ref[...] = _dot_nt(h, wt_ref[BZ_OFF:IN_WIDTH, :])

        def chunk(c, carry):
            r0 = pl.multiple_of(c * CHUNK, CHUNK)
            rows = pl.ds(r0, CHUNK)
            u = _gelu(a_ref[rows, 0:A_WIDTH])
            vv = _gelu(a_ref[rows, A_WIDTH:2 * A_WIDTH])
            xhat, mu, rs = _layer_norm_stats(vv)
            stats_ref[rows, STAT_LN_MEAN] = jnp.broadcast_to(mu, (CHUNK, CHUNK))
            stats_ref[rows, STAT_LN_RSTD] = jnp.broadcast_to(rs, (CHUNK, CHUNK))
            vnb = (xhat * lng_ref[...] + lnb_ref[...]).astype(BF16)
            for p in range(4):
                blk = slice(p * CHUNK, (p + 1) * CHUNK)
                mixed = _dot(wcat_ref[p], _blockdiag(vnb[:, blk], lo)) + bs_ref[p]
                sz = _silu(a_ref[rows, 2 * A_WIDTH + p * CHUNK:2 * A_WIDTH + (p + 1) * CHUNK])
                ycat_s[rows, blk] = ((u[:, blk] * mixed) * sz).astype(BF16)
            ks, vs = _kv_rows(kv_ref, halo_ref, r0, c)
            dead = _dead_mask(lax.rem(i * tb + r0, seq_len) == 0, ahead)
            q_tiles = [q_ref[rows, p * CHUNK:(p + 1) * CHUNK] for p in range(4)]
            probs = _attention_probs(_query_stacks(q_tiles, lo), ks, bias_ref, sink_ref, dead, prev)
            for (p, half), prob in probs.items():
                head = 2 * p + half
                probs_ref[rows, head * CHUNK:(head + 1) * CHUNK] = prob
            outs, _ = _attention_values(probs, vs, prev)
            for p in range(4):
                blk = slice(p * CHUNK, (p + 1) * CHUNK)
                o = jnp.where(lo, outs[p, 0], outs[p, 1])
                ycat_s[rows, B_WIDTH + p * CHUNK:B_WIDTH + (p + 1) * CHUNK] = (o * _silu(bz_ref[rows, blk])).astype(BF16)
            return carry

        lax.fori_loop(0, n_chunks, chunk, 0, unroll=CHUNK_UNROLL)
        halo_ref[...] = kv_ref[tb - CHUNK:tb, :]
        y = _dot(ycat_s[...], wout_ref[...])
        r = lax.rsqrt(jnp.mean(y * y, axis=-1, keepdims=True) + NORM_EPS)
        y_ref[...] = y
        stats_ref[:, STAT_POST_RMS] = jnp.broadcast_to(r, (tb, CHUNK))
        xn = x_ref[...] + (y * r) * gpost_ref[...]
        if target is None:
            xn_ref[...] = xn
        else:
            d = xn - target_ref[...]
            xn_ref[...] = d * (1.0 / D_MODEL)

            @pl.when(i == 0)
            def _():
                loss_ref[0, 0] = 0.0

            loss_ref[0, 0] += 0.5 * jnp.sum(jnp.mean(d * d, axis=-1, keepdims=True))
        at_end()

    tile = lambda w: pl.BlockSpec((tb, w), lambda i: (i, 0))
    whole = lambda shape, **kw: pl.BlockSpec(shape, lambda i: (0,) * len(shape), **kw)
    in_specs = [tile(D_MODEL), whole((1, D_MODEL)), whole((IN_WIDTH, D_MODEL), pipeline_mode=pl.Buffered(1)),
                whole((1, A_WIDTH)), whole((1, A_WIDTH)), whole((4, CHUNK, 2 * CHUNK)), whole((4, CHUNK, CHUNK)), SMEM,
                whole((HEADS, CHUNK, CHUNK)), whole((D_MODEL, D_MODEL)), whole((1, D_MODEL))]
    out_shape = [jax.ShapeDtypeStruct((n_tok, Q_OFF), F32), jax.ShapeDtypeStruct((n_tok, B_WIDTH), BF16),
                 jax.ShapeDtypeStruct((n_tok, 2 * KV_WIDTH), BF16), jax.ShapeDtypeStruct((n_tok, B_WIDTH), F32),
                 jax.ShapeDtypeStruct((n_tok, HEADS * CHUNK), BF16), jax.ShapeDtypeStruct((n_tok, N_STATS * CHUNK), F32),
                 jax.ShapeDtypeStruct((n_tok, D_MODEL), F32), jax.ShapeDtypeStruct((n_tok, D_MODEL), F32)]
    out_specs = [tile(Q_OFF), tile(B_WIDTH), tile(2 * KV_WIDTH), tile(B_WIDTH), tile(HEADS * CHUNK), tile(N_STATS * CHUNK),
                 tile(D_MODEL), tile(D_MODEL)]
    scratch = [pltpu.VMEM((tb, D_MODEL), BF16), pltpu.VMEM((CHUNK, 2 * KV_WIDTH), BF16)]
    args = [x, g_pre, w_in_t, ln_g, ln_b, wcat, bs, sinks, bias, w_out, g_post]
    if gather is not None:
        wt_shards, wo_shards, _ = gather
        in_specs += [ANY, ANY]
        args += [wt_shards, wo_shards]
        out_shape += [jax.ShapeDtypeStruct((N_CHIPS * wt_shards.shape[1], D_MODEL), BF16),
                      jax.ShapeDtypeStruct((N_CHIPS * wo_shards.shape[1], D_MODEL), BF16)]
        out_specs += [ANY, ANY]
        scratch += _comm_scratch(N_GATHER_SEMS)
    if target is not None:
        in_specs.append(tile(D_MODEL))
        args.append(target)
        out_shape.append(jax.ShapeDtypeStruct((1, 1), F32))
        out_specs.append(SMEM)
    name = "fwd_layer" + ("" if gather is None else "_gather") + ("" if target is None else "_loss")
    return pl.pallas_call(
        body, name=name, grid=(n_tiles,),
        out_shape=tuple(out_shape), in_specs=in_specs, out_specs=tuple(out_specs), scratch_shapes=scratch,
        compiler_params=_cparams(1),
    )(*args)


def _bwd_mix(dout, y, a, q, kv, bz, probs, stats, ln_g, ln_b, wcat, wtcat, bs, w_out, g_post, exchange=None, allgather=None):
    n_tok = y.shape[0]
    tb = TOK_TILE
    n_chunks = tb // CHUNK
    n_tiles = n_tok // tb

    def body(*refs):
        refs = list(refs)
        take = lambda n: [refs.pop(0) for _ in range(n)]
        (dout_ref, y_ref, a_ref, q_ref, kv_ref, halo_ref, bz_ref, probs_ref, stats_ref, lng_ref, lnb_ref, wcat_ref,
         wtcat_ref, bs_ref, wout_ref, gpost_ref) = take(16)
        if exchange is not None:
            dwt_in, dwo_in = take(2)
        if allgather is not None:
            (small_in,) = take(1)
        dproj_ref, dwout_ref, dws_ref, dmix_ref, dlng_ref, dlnb_ref, dgpost_ref, dbias_ref = take(8)
        if exchange is not None:
            pt_ref, po_ref = take(2)
        if allgather is not None:
            (small_all,) = take(1)
        ycat_s, dy_s, dyc_s, dkv_s, carry_s, dwout_s, stage_s, dws_s = take(8)
        starts, ends = [], []
        if exchange is not None:
            hosted = _hosted(_exchange_phases((dwt_in, dwo_in), (pt_ref, po_ref), *take(3)), pl.program_id(0), n_tiles)
            starts.append(hosted[0])
            ends.append(hosted[1])
        if allgather is not None:
            hosted = _hosted(_exchange_phases((small_in,), (small_all,), *take(3), by_chip=False), pl.program_id(0),
                             n_tiles)
            starts.append(hosted[0])
            ends.append(hosted[1])
        at_start = lambda: [f() for f in starts]
        at_end = lambda: [f() for f in ends]
        at_start()
        step = pl.program_id(0)
        lo = _lo_mask((CHUNK, CHUNK))
        prev = _ahead() > 0

        @pl.when(step == 0)
        def _():
            dwout_s[...] = jnp.zeros_like(dwout_s)
            dws_s[...] = jnp.zeros_like(dws_s)
            dmix_ref[...] = jnp.zeros_like(dmix_ref)
            dlng_ref[...] = jnp.zeros_like(dlng_ref)
            dlnb_ref[...] = jnp.zeros_like(dlnb_ref)
            dgpost_ref[...] = jnp.zeros_like(dgpost_ref)
            dbias_ref[...] = jnp.zeros_like(dbias_ref)
            carry_s[...] = jnp.zeros_like(carry_s)

        for r0 in range(0, tb, CHUNK):
            rows = slice(r0, r0 + CHUNK)
            yv = y_ref[rows, :]
            dout = dout_ref[rows, :]
            r = _over_lanes(stats_ref[rows, STAT_POST_RMS], D_MODEL)
            yn = yv * r
            dgpost_ref[...] += jnp.sum(dout * yn, axis=0, keepdims=True)
            dyn = dout * gpost_ref[...]
            dy_s[rows, :] = (r * (dyn - yn * jnp.mean(dyn * yn, axis=-1, keepdims=True))).astype(BF16)
        dyc_s[...] = _dot_nt(dy_s[...], wout_ref[...])
        dkv_s[0:tb, :] = jnp.zeros((tb, 2 * KV_WIDTH), F32)
        dkv_s[tb:tb + CHUNK, :] = carry_s[...]

        def chunk(c, carry):
            r0 = pl.multiple_of(c * CHUNK, CHUNK)
            rows = pl.ds(r0, CHUNK)
            u, gu = _gelu_and_grad(a_ref[rows, 0:A_WIDTH])
            vv, gv = _gelu_and_grad(a_ref[rows, A_WIDTH:2 * A_WIDTH])
            rs = _over_lanes(stats_ref[rows, STAT_LN_RSTD], A_WIDTH)
            xhat = (vv - _over_lanes(stats_ref[rows, STAT_LN_MEAN], A_WIDTH)) * rs
            vnb = (xhat * lng_ref[...] + lnb_ref[...]).astype(BF16)
            d_vn, d_u, d_az = [], [], []
            for p in range(4):
                blk = slice(p * CHUNK, (p + 1) * CHUNK)
                vbd = _blockdiag(vnb[:, blk], lo)
                mixed = _dot(wcat_ref[p], vbd) + bs_ref[p]
                sz, gz = _silu_and_grad(a_ref[rows, 2 * A_WIDTH + p * CHUNK:2 * A_WIDTH + (p + 1) * CHUNK])
                ub = u[:, blk]
                dya = dyc_s[rows, blk]
                um = ub * mixed
                ycat_s[rows, blk] = (um * sz).astype(BF16)
                d_mixed = (dya * ub) * sz
                d_u.append((dya * mixed) * sz)
                d_az.append((dya * um) * gz)
                dmix_ref[:, blk] += d_mixed
                dmbd = _blockdiag(d_mixed.astype(BF16), lo)
                d_vn.append(_dot(wtcat_ref[p], dmbd))
                dws_s[p] += _dot_nt(dmbd, vnb[:, blk])
            d_vn = jnp.concatenate(d_vn, axis=1)
            dlng_ref[...] += jnp.sum(d_vn * xhat, axis=0, keepdims=True)
            dlnb_ref[...] += jnp.sum(d_vn, axis=0, keepdims=True)
            dxh = d_vn * lng_ref[...]
            d_vv = rs * (dxh - jnp.mean(dxh, axis=-1, keepdims=True)
                         - xhat * jnp.mean(dxh * xhat, axis=-1, keepdims=True))
            dproj_ref[rows, 0:A_WIDTH] = (jnp.concatenate(d_u, axis=1) * gu).astype(BF16)
            dproj_ref[rows, A_WIDTH:2 * A_WIDTH] = (d_vv * gv).astype(BF16)
            dproj_ref[rows, 2 * A_WIDTH:Q_OFF] = jnp.concatenate(d_az, axis=1).astype(BF16)
            ks, vs = _kv_rows(kv_ref, halo_ref, r0, c)
            q_stacks = _query_stacks([q_ref[rows, p * CHUNK:(p + 1) * CHUNK] for p in range(4)], lo)
            probs = {(p, half): probs_ref[rows, (2 * p + half) * CHUNK:(2 * p + half + 1) * CHUNK]
                     for p in range(4) for half in range(2)}
            outs_h, p_stacks = _attention_values(probs, vs, prev)
            dom = {}
            for p in range(4):
                blk = slice(p * CHUNK, (p + 1) * CHUNK)
                sb, gb = _silu_and_grad(bz_ref[rows, blk])
                dyb = dyc_s[rows, B_WIDTH + p * CHUNK:B_WIDTH + (p + 1) * CHUNK]
                o = jnp.where(lo, outs_h[p, 0], outs_h[p, 1])
                ycat_s[rows, B_WIDTH + p * CHUNK:B_WIDTH + (p + 1) * CHUNK] = (o * sb).astype(BF16)
                dproj_ref[rows, BZ_OFF + p * CHUNK:BZ_OFF + (p + 1) * CHUNK] = ((dyb * o) * gb).astype(BF16)
                dom[p] = _masked_halves((dyb * sb).astype(BF16), lo)
            dqs, dk_by_copy, dv_by_copy = {}, [], []
            for sw, members in enumerate(_HEADS_BY_COPY):
                do_stack = jnp.concatenate([dom[p][half] for p, half in members], axis=0)
                dp_stack = _dot_nt(do_stack, vs[sw])
                dsws = []
                for i, (p, half) in enumerate(members):
                    head = 2 * p + half
                    dp = _window_square(dp_stack[i * CHUNK:(i + 1) * CHUNK, :], prev)
                    prob = probs[p, half].astype(F32)
                    delta = jnp.sum(prob * dp, axis=-1, keepdims=True)
                    ds = prob * (dp - delta)
                    dbias_ref[head] += ds
                    dsws.append(_window_keys(ds.astype(BF16), prev))
                ds_stack = jnp.concatenate(dsws, axis=0)
                dq_stack = _dot(ds_stack, ks[sw])
                for i, m in enumerate(members):
                    dqs[m] = dq_stack[i * CHUNK:(i + 1) * CHUNK, :]
                dk_by_copy.append(_dot_tn(ds_stack, q_stacks[sw]))
                dv_by_copy.append(_dot_tn(p_stacks[sw], do_stack))
            for p in range(4):
                dq = jnp.where(lo, dqs[p, 0], dqs[p, 1]) * QK_SCALE
                dproj_ref[rows, Q_OFF + p * CHUNK:Q_OFF + (p + 1) * CHUNK] = dq.astype(BF16)
            both = pl.ds(r0, 2 * CHUNK)
            dkv_s[both, 0:KV_WIDTH] += dk_by_copy[0] + _swap_halves(dk_by_copy[1])
            dkv_s[both, KV_WIDTH:2 * KV_WIDTH] += dv_by_copy[0] + _swap_halves(dv_by_copy[1])
            return carry

        lax.fori_loop(0, n_chunks, chunk, 0, unroll=BWD_CHUNK_UNROLL)
        dwout_s[...] += _dot_tn(ycat_s[...], dy_s[...])
        dproj_ref[:, K_OFF:BZ_OFF] = dkv_s[CHUNK:CHUNK + tb, :].astype(BF16)
        carry_s[...] = dkv_s[0:CHUNK, :]

        @pl.when(step == n_tiles - 1)
        def _():
            row = lax.broadcasted_iota(jnp.int32, (2 * CHUNK, CHUNK), 0)
            col = lax.broadcasted_iota(jnp.int32, (2 * CHUNK, CHUNK), 1)
            causal = col <= jnp.where(row >= CHUNK, row - CHUNK, row)
            for p in range(4):
                dws_ref[p] = jnp.where(causal, dws_s[p], 0.0).astype(BF16)
            rows = stage_s.shape[0]
            for r0 in range(0, D_MODEL, rows):
                stage_s[...] = dwout_s[r0:r0 + rows, :].astype(BF16)
                pltpu.sync_copy(stage_s, dwout_ref.at[r0:r0 + rows, :])

        at_end()

    tile = lambda w: pl.BlockSpec((tb, w), lambda s: (n_tiles - 1 - s, 0))
    whole = lambda shape, **kw: pl.BlockSpec(shape, lambda s: (0,) * len(shape), **kw)
    once = dict(pipeline_mode=pl.Buffered(1))
    in_specs = [tile(D_MODEL), tile(D_MODEL), tile(Q_OFF), tile(B_WIDTH), tile(2 * KV_WIDTH),
                pl.BlockSpec((CHUNK, 2 * KV_WIDTH), lambda s: (jnp.maximum((n_tiles - 1 - s) * n_chunks - 1, 0), 0)),
                tile(B_WIDTH), tile(HEADS * CHUNK), tile(N_STATS * CHUNK), whole((1, A_WIDTH)), whole((1, A_WIDTH)),
                whole((4, CHUNK, 2 * CHUNK)),
                whole((4, CHUNK, 2 * CHUNK)), whole((4, CHUNK, CHUNK)), whole((D_MODEL, D_MODEL), **once),
                whole((1, D_MODEL))]
    args = [dout, y, a, q, kv, kv, bz, probs, stats, ln_g, ln_b, wcat, wtcat, bs, w_out, g_post]
    small_shapes = [jax.ShapeDtypeStruct((D_MODEL, D_MODEL), BF16), jax.ShapeDtypeStruct((4, 2 * CHUNK, CHUNK), BF16),
                    jax.ShapeDtypeStruct((CHUNK, A_WIDTH), F32), jax.ShapeDtypeStruct((1, A_WIDTH), F32),
                    jax.ShapeDtypeStruct((1, A_WIDTH), F32), jax.ShapeDtypeStruct((1, D_MODEL), F32),
                    jax.ShapeDtypeStruct((HEADS, CHUNK, CHUNK), F32)]
    small_specs = [ANY, whole((4, 2 * CHUNK, CHUNK)), whole((CHUNK, A_WIDTH)), whole((1, A_WIDTH)), whole((1, A_WIDTH)),
                   whole((1, D_MODEL)), whole((HEADS, CHUNK, CHUNK))]
    scratch = [pltpu.VMEM((tb, D_MODEL), BF16), pltpu.VMEM((tb, D_MODEL), BF16), pltpu.VMEM((tb, D_MODEL), F32),
               pltpu.VMEM((tb + CHUNK, 2 * KV_WIDTH), F32), pltpu.VMEM((CHUNK, 2 * KV_WIDTH), F32),
               pltpu.VMEM((D_MODEL, D_MODEL), F32), pltpu.VMEM((2 * CHUNK, D_MODEL), BF16),
               pltpu.VMEM((4, 2 * CHUNK, CHUNK), F32)]
    out_shape = [jax.ShapeDtypeStruct((n_tok, IN_WIDTH), BF16)] + small_shapes
    out_specs = [tile(IN_WIDTH)] + small_specs
    if exchange is not None:
        dwt, dwo = exchange
        in_specs += [ANY, ANY]
        args += [dwt, dwo]
        out_shape += [jax.ShapeDtypeStruct((N_DEV, dwt.shape[0] // N_CHIPS, D_MODEL), BF16),
                      jax.ShapeDtypeStruct((N_DEV, dwo.shape[0] // N_CHIPS, D_MODEL), BF16)]
        out_specs += [ANY, ANY]
    if allgather is not None:
        in_specs.append(ANY)
        args.append(allgather)
        out_shape.append(jax.ShapeDtypeStruct((N_DEV,) + allgather.shape, allgather.dtype))
        out_specs.append(ANY)
    for hosted in (exchange, allgather):
        if hosted is not None:
            scratch += _comm_scratch(N_EXCHANGE_SEMS)
    return pl.pallas_call(
        body, name="bwd_mix" + ("" if exchange is None and allgather is None else "_exchange"), grid=(n_tiles,),
        out_shape=tuple(out_shape), in_specs=in_specs, out_specs=tuple(out_specs), scratch_shapes=scratch,
        compiler_params=_cparams(1),
    )(*args)


def _bwd_in(dproj, x, dout, g_pre, w_in_t):
    n_tok = x.shape[0]
    tm = TOK_TILE
    n_tiles = n_tok // tm

    def body(dp_ref, x_ref, dout_ref, g_ref, wt_ref, dx_ref, dwt_ref, dg_ref, acc_s):
        i = pl.program_id(0)

        @pl.when(i == 0)
        def _():
            acc_s[...] = jnp.zeros_like(acc_s)
            dg_ref[...] = jnp.zeros_like(dg_ref)

        xf = x_ref[...]
        r = lax.rsqrt(jnp.mean(xf * xf, axis=-1, keepdims=True) + NORM_EPS)
        xn = xf * r
        h = (xn * g_ref[...]).astype(BF16)
        dp = dp_ref[...]
        dh = _dot(dp, wt_ref[...])
        acc_s[...] += _dot_tn(dp, h)
        dg_ref[...] += jnp.sum(dh * xn, axis=0, keepdims=True)
        dhn = dh * g_ref[...]
        dx_ref[...] = dout_ref[...] + r * (dhn - xn * jnp.mean(dhn * xn, axis=-1, keepdims=True))

        @pl.when(i == n_tiles - 1)
        def _():
            dwt_ref[...] = acc_s[...].astype(BF16)

    return pl.pallas_call(
        body, name="bwd_in", grid=(n_tiles,),
        out_shape=(jax.ShapeDtypeStruct((n_tok, D_MODEL), F32), jax.ShapeDtypeStruct((IN_WIDTH, D_MODEL), BF16),
                   jax.ShapeDtypeStruct((1, D_MODEL), F32)),
        in_specs=[pl.BlockSpec((tm, IN_WIDTH), lambda i: (i, 0)), pl.BlockSpec((tm, D_MODEL), lambda i: (i, 0)),
                  pl.BlockSpec((tm, D_MODEL), lambda i: (i, 0)), pl.BlockSpec((1, D_MODEL), lambda i: (0, 0)),
                  pl.BlockSpec((IN_WIDTH, D_MODEL), lambda i: (0, 0), pipeline_mode=pl.Buffered(1))],
        out_specs=(pl.BlockSpec((tm, D_MODEL), lambda i: (i, 0)),
                   pl.BlockSpec((IN_WIDTH, D_MODEL), lambda i: (0, 0), pipeline_mode=pl.Buffered(1)),
                   pl.BlockSpec((1, D_MODEL), lambda i: (0, 0))),
        scratch_shapes=[pltpu.VMEM((IN_WIDTH, D_MODEL), F32)],
        compiler_params=_cparams(1),
    )(dproj, x, dout, g_pre, w_in_t)


def _riders(exchanges):
    args = [a for arrays, _ in exchanges for a in arrays]
    shapes = [jax.ShapeDtypeStruct((N_DEV, a.shape[0] // N_CHIPS if by_chip else a.shape[0]) + a.shape[1:], a.dtype)
              for arrays, by_chip in exchanges for a in arrays]
    scratch = [s for _ in exchanges for s in _comm_scratch(N_EXCHANGE_SEMS)]

    def bind(in_refs, out_refs, sem_refs, step, n_steps, pass_on_step=None):
        in_refs, out_refs, sem_refs = list(in_refs), list(out_refs), list(sem_refs)
        starts, ends = [], []
        for arrays, by_chip in exchanges:
            n = len(arrays)
            phases = _exchange_phases(tuple(in_refs[:n]), tuple(out_refs[:n]), *sem_refs[:3], by_chip=by_chip)
            del in_refs[:n], out_refs[:n], sem_refs[:3]
            at_start, at_end = _hosted(phases, step, n_steps, pass_on_step)
            starts.append(at_start)
            ends.append(at_end)
        return (lambda: [f() for f in starts]), (lambda: [f() for f in ends])

    return args, [ANY] * len(args), shapes, [ANY] * len(shapes), scratch, bind


def _bwd_in_dw(dproj, x, g_pre, exchanges):
    n_tok = x.shape[0]
    tm = TOK_TILE
    n_tiles = n_tok // tm
    r_args, r_in_specs, r_shapes, r_out_specs, r_scratch, bind = _riders(exchanges)

    def body(*refs):
        dp_ref, x_ref, g_ref = refs[:3]
        in_refs = refs[3:3 + len(r_args)]
        dwt_ref = refs[3 + len(r_args)]
        out_refs = refs[4 + len(r_args):4 + 2 * len(r_args)]
        acc_s = refs[4 + 2 * len(r_args)]
        i = pl.program_id(0)
        at_start, at_end = bind(in_refs, out_refs, refs[5 + 2 * len(r_args):], i, n_tiles)
        at_start()

        @pl.when(i == 0)
        def _():
            acc_s[...] = jnp.zeros_like(acc_s)

        xf = x_ref[...]
        r = lax.rsqrt(jnp.mean(xf * xf, axis=-1, keepdims=True) + NORM_EPS)
        acc_s[...] += _dot_tn(dp_ref[...], ((xf * r) * g_ref[...]).astype(BF16))

        @pl.when(i == n_tiles - 1)
        def _():
            dwt_ref[...] = acc_s[...].astype(BF16)

        at_end()

    return pl.pallas_call(
        body, name="bwd_in_dw", grid=(n_tiles,),
        out_shape=(jax.ShapeDtypeStruct((IN_WIDTH, D_MODEL), BF16), *r_shapes),
        in_specs=[pl.BlockSpec((tm, IN_WIDTH), lambda i: (i, 0)), pl.BlockSpec((tm, D_MODEL), lambda i: (i, 0)),
                  pl.BlockSpec((1, D_MODEL), lambda i: (0, 0))] + r_in_specs,
        out_specs=(pl.BlockSpec((IN_WIDTH, D_MODEL), lambda i: (0, 0), pipeline_mode=pl.Buffered(1)), *r_out_specs),
        scratch_shapes=[pltpu.VMEM((IN_WIDTH, D_MODEL), F32)] + r_scratch,
        compiler_params=_cparams(1),
    )(dproj, x, g_pre, *r_args)


def _bwd_in_dx(dproj, x, dout, g_pre, w_in_t, exchanges):
    n_tok = x.shape[0]
    tm = TOK_TILE
    n_tiles = n_tok // tm
    r_args, r_in_specs, r_shapes, r_out_specs, r_scratch, bind = _riders(exchanges)

    def body(*refs):
        dp_ref, x_ref, dout_ref, g_ref, wt_ref = refs[:5]
        in_refs = refs[5:5 + len(r_args)]
        dx_ref, dg_ref = refs[5 + len(r_args):7 + len(r_args)]
        out_refs = refs[7 + len(r_args):7 + 2 * len(r_args)]
        i = pl.program_id(0)
        at_start, at_end = bind(in_refs, out_refs, refs[7 + 2 * len(r_args):], i, n_tiles, pass_on_step=n_tiles - 2)
        at_start()

        @pl.when(i == 0)
        def _():
            dg_ref[...] = jnp.zeros_like(dg_ref)

        xf = x_ref[...]
        r = lax.rsqrt(jnp.mean(xf * xf, axis=-1, keepdims=True) + NORM_EPS)
        xn = xf * r
        dh = _dot(dp_ref[...], wt_ref[...])
        dg_ref[...] += jnp.sum(dh * xn, axis=0, keepdims=True)
        dhn = dh * g_ref[...]
        dx_ref[...] = dout_ref[...] + r * (dhn - xn * jnp.mean(dhn * xn, axis=-1, keepdims=True))
        at_end()

    tile = lambda w: pl.BlockSpec((tm, w), lambda i: (i, 0))
    return pl.pallas_call(
        body, name="bwd_in_dx", grid=(n_tiles,),
        out_shape=(jax.ShapeDtypeStruct((n_tok, D_MODEL), F32), jax.ShapeDtypeStruct((1, D_MODEL), F32), *r_shapes),
        in_specs=[tile(IN_WIDTH), tile(D_MODEL), tile(D_MODEL), pl.BlockSpec((1, D_MODEL), lambda i: (0, 0)),
                  pl.BlockSpec((IN_WIDTH, D_MODEL), lambda i: (0, 0), pipeline_mode=pl.Buffered(1))] + r_in_specs,
        out_specs=(tile(D_MODEL), pl.BlockSpec((1, D_MODEL), lambda i: (0, 0)), *r_out_specs),
        scratch_shapes=r_scratch,
        compiler_params=_cparams(1),
    )(dproj, x, dout, g_pre, w_in_t, *r_args)


def _adam_update(w, g, m, v):
    nm = ADAM_B1 * m + (1.0 - ADAM_B1) * g
    nv = ADAM_B2 * v + (1.0 - ADAM_B2) * (g * g)
    m_hat = nm / (1.0 - ADAM_B1 ** ADAM_STEP)
    v_hat = nv / (1.0 - ADAM_B2 ** ADAM_STEP)
    return -ADAM_LR * (m_hat / (jnp.sqrt(v_hat) + ADAM_EPS) + ADAM_WD * w), nm, nv


def _sum_slots(p_ref):
    tot = p_ref[0].astype(F32)
    for d in range(1, N_DEV):
        tot = tot + p_ref[d].astype(F32)
    return tot


def _adamw_parts(parts, w, m, v, name):
    n_layers, n_rows, n_cols = w.shape
    tr = n_rows
    while tr * n_cols > ADAMW_BLOCK_ELEMS:
        tr //= 2
    n_blocks = n_rows // tr

    def body(*refs):
        p_refs = refs[:n_layers]
        w_ref, m_ref, v_ref, g_ref, d_ref, nm_ref, nv_ref = refs[n_layers:]
        for k in range(n_layers):
            @pl.when(pl.program_id(0) == k)
            def _(k=k):
                g = _sum_slots(p_refs[k])
                g_ref[0] = g
                d_ref[0], nm_ref[0], nv_ref[0] = _adam_update(w_ref[0], g, m_ref[0], v_ref[0])

    def part_spec(k):
        return pl.BlockSpec((N_DEV, tr, n_cols), lambda l, i: (0, jnp.clip(i + (l - k) * n_blocks, 0, n_blocks - 1), 0))

    spec = pl.BlockSpec((1, tr, n_cols), lambda l, i: (l, i, 0))
    shape = jax.ShapeDtypeStruct((n_layers, n_rows, n_cols), F32)
    return pl.pallas_call(
        body, name="adamw_" + name, grid=(n_layers, n_blocks), out_shape=(shape,) * 4,
        in_specs=[part_spec(k) for k in range(n_layers)] + [spec, spec, spec], out_specs=(spec,) * 4,
        compiler_params=_cparams(2),
    )(*parts, w, m, v)


def _adamw_small(quads):
    n = len(quads)

    def body(*refs):
        ins, outs = refs[:4 * n], refs[4 * n:]
        for j in range(n):
            w_ref, g_ref, m_ref, v_ref = ins[4 * j:4 * j + 4]
            d_ref, nm_ref, nv_ref = outs[3 * j:3 * j + 3]
            d_ref[...], nm_ref[...], nv_ref[...] = _adam_update(w_ref[...], g_ref[...], m_ref[...], v_ref[...])

    vmem = pl.BlockSpec(memory_space=pltpu.VMEM)
    shapes = tuple(jax.ShapeDtypeStruct(w.shape, F32) for w, *_ in quads for _ in range(3))
    results = pl.pallas_call(
        body, name="adamw_small", out_shape=shapes, in_specs=[vmem] * (4 * n), out_specs=(vmem,) * (3 * n),
    )(*[a for quad in quads for a in quad])
    return [results[3 * j:3 * j + 3] for j in range(n)]


_SMALL = ("pre_norm_g", "ln_v_g", "ln_v_b", "b_spatial", "sinks", "rel_bias", "post_norm_g")


def _pack(pieces):
    blocks, first_rows, n = [], [], 0
    for p in pieces:
        flat = p.reshape(-1)
        flat = jnp.concatenate([flat, jnp.zeros(((-flat.shape[0]) % (8 * 128),), F32)]).reshape(-1, 128)
        blocks.append(flat)
        first_rows.append(n)
        n += flat.shape[0]
    return jnp.concatenate(blocks, axis=0), first_rows


def _unpack(block, first_row, shape):
    size = math.prod(shape)
    return block[first_row:first_row + -(-size // 128)].reshape(-1)[:size].reshape(shape)


def kernel(x, pre_norm_g, w_in, ln_v_g, ln_v_b, w_spatial, b_spatial, sinks, rel_bias, w_out, post_norm_g, loss_target, m_pre_norm_g, m_w_in, m_ln_v_g, m_ln_v_b, m_w_spatial, m_b_spatial, m_sinks, m_rel_bias, m_w_out, m_post_norm_g, v_pre_norm_g, v_w_in, v_ln_v_g, v_ln_v_b, v_w_spatial, v_b_spatial, v_sinks, v_rel_bias, v_w_out, v_post_norm_g):
    weights = dict(pre_norm_g=pre_norm_g, w_in=w_in, ln_v_g=ln_v_g, ln_v_b=ln_v_b, w_spatial=w_spatial, b_spatial=b_spatial,
                   sinks=sinks, rel_bias=rel_bias, w_out=w_out, post_norm_g=post_norm_g)
    mom_m = dict(pre_norm_g=m_pre_norm_g, w_in=m_w_in, ln_v_g=m_ln_v_g, ln_v_b=m_ln_v_b, w_spatial=m_w_spatial,
                 b_spatial=m_b_spatial, sinks=m_sinks, rel_bias=m_rel_bias, w_out=m_w_out, post_norm_g=m_post_norm_g)
    mom_v = dict(pre_norm_g=v_pre_norm_g, w_in=v_w_in, ln_v_g=v_ln_v_g, ln_v_b=v_ln_v_b, w_spatial=v_w_spatial,
                 b_spatial=v_b_spatial, sinks=v_sinks, rel_bias=v_rel_bias, w_out=v_w_out, post_norm_g=v_post_norm_g)
    n_seq, seq_len, _ = x.shape
    n_layers = w_in.shape[0]
    x2 = x.reshape(n_seq * seq_len, D_MODEL)
    target2 = loss_target.reshape(n_seq * seq_len, D_MODEL)
    row = lambda p, l: p[l][None]

    wt_shards = jnp.swapaxes(w_in, 1, 2).astype(BF16)
    wo_shards = w_out.astype(BF16)
    bucket = _window_buckets()
    bias = _bias_table(rel_bias, bucket)
    wcat, wtcat, bs = _spatial_tables(w_spatial, jnp.swapaxes(b_spatial, 1, 2))

    wt, wo = [None] * n_layers, [None] * n_layers
    wt[0], wo[0] = _gather_weights(wt_shards, wo_shards, 0)
    xs, saved = [x2], []
    for l in range(n_layers):
        layer_args = (xs[-1], row(pre_norm_g, l), wt[l], row(ln_v_g, l), row(ln_v_b, l), wcat[l], bs[l], sinks[l], bias,
                      wo[l], row(post_norm_g, l), seq_len)
        if l + 1 < n_layers:
            *acts, xn, y, wt[l + 1], wo[l + 1] = _fwd_layer(*layer_args, gather=(wt_shards, wo_shards, l + 1))
            xs.append(xn)
        else:
            *acts, dx, y, loss = _fwd_layer(*layer_args, target=target2)
        saved.append((y, *acts))

    small = [None] * n_layers
    parts = [None] * n_layers
    waiting = None
    for l in reversed(range(n_layers)):
        hosted = {} if waiting is None else dict(exchange=waiting[:2], allgather=waiting[2])
        dproj, *rest = _bwd_mix(dx, *saved[l], row(ln_v_g, l), row(ln_v_b, l), wcat[l], wtcat[l], bs[l], wo[l],
                                row(post_norm_g, l), **hosted)
        dwo, dws, dmix, dlng, dlnb, dgpost, dbias = rest[:7]
        dws = dws.reshape(HEADS * CHUNK, CHUNK)
        if waiting is not None:
            parts[l + 1] = list(rest[7:])
        if l > 0:
            dx, dwt, dgpre = _bwd_in(dproj, xs[l], dx, row(pre_norm_g, l), wt[l])
            waiting = (dwt, dwo, dws)
        else:
            dwt, po, ws_all = _bwd_in_dw(dproj, xs[l], row(pre_norm_g, l), [((dwo,), True), ((dws,), False)])
            dx, dgpre, pt = _bwd_in_dx(dproj, xs[l], dx, row(pre_norm_g, l), wt[l], [((dwt,), True)])
            parts[l] = [pt, po, ws_all]
        small[l] = dict(pre_norm_g=dgpre[0], ln_v_g=dlng[0], ln_v_b=dlnb[0], dmix=dmix, dbias=dbias,
                        post_norm_g=dgpost[0])

    db = _b_spatial_grad([s["dmix"] for s in small])
    pieces = {n: jnp.stack([s[n] for s in small]) for n in ("pre_norm_g", "ln_v_g", "ln_v_b", "post_norm_g")}
    pieces["b_spatial"] = jnp.swapaxes(db[:, :, :HEADS], 1, 2)
    pieces["rel_bias"], pieces["sinks"] = _rel_bias_grad([s["dbias"] for s in small], bucket)
    packed, first_rows = _pack([pieces[n] for n in _SMALL] + [loss])
    total = _allreduce_small(packed)
    grad = {n: _unpack(total, r, weights[n].shape) for n, r in zip(_SMALL, first_rows)}
    loss_out = total[first_rows[-1], 0]

    delta, new_m, new_v = {}, {}, {}
    two_d = lambda p: p.reshape(-1, p.shape[-1])
    updates = _adamw_small([tuple(two_d(p[n]) for p in (weights, grad, mom_m, mom_v)) for n in _SMALL])
    for n, update in zip(_SMALL, updates):
        delta[n], new_m[n], new_v[n] = (u.reshape(weights[n].shape) for u in update)

    t3 = lambda p: jnp.swapaxes(p, 1, 2)
    flat3 = lambda p: p.reshape(n_layers, HEADS * CHUNK, CHUNK)
    for n, which, view, back in (("w_in", 0, t3, t3), ("w_out", 1, lambda p: p, lambda p: p),
                                 ("w_spatial", 2, flat3, lambda p: p.reshape(w_spatial.shape))):
        results = _adamw_parts([parts[l][which] for l in range(n_layers)], view(weights[n]), view(mom_m[n]),
                               view(mom_v[n]), n)
        grad[n], delta[n], new_m[n], new_v[n] = (back(r) for r in results)

    names = tuple(weights)
    return (loss_out, dx.reshape(x.shape), *[grad[n] for n in names], *[delta[n] for n in names],
            *[new_m[n] for n in names], *[new_v[n] for n in names])
```

```python
import math

import jax
import jax.numpy as jnp
from jax import lax
from jax.experimental import pallas as pl
from jax.experimental.pallas import tpu as pltpu

F32 = jnp.float32
BF16 = jnp.bfloat16

D_MODEL = 1024
A_WIDTH = 512
B_WIDTH = 512
KV_WIDTH = 128
IN_WIDTH = 3 * A_WIDTH + 2 * B_WIDTH + 2 * KV_WIDTH
CHUNK = 128
HEADS = 8
HEAD_DIM = 64
REL_BUCKETS = 32
NORM_EPS = 1e-6
NEG = -1e30
Q_OFF = 3 * A_WIDTH
K_OFF = Q_OFF + B_WIDTH
BZ_OFF = K_OFF + 2 * KV_WIDTH
QK_SCALE = HEAD_DIM ** -0.5

ADAM_LR = 0.001
ADAM_B1 = 0.9
ADAM_B2 = 0.999
ADAM_EPS = 1e-08
ADAM_WD = 0.01
ADAM_STEP = 10

TOK_TILE = 512
CHUNK_UNROLL = 4
BWD_CHUNK_UNROLL = 4
ADAMW_BLOCK_ELEMS = 192 * 1024
VMEM_LIMIT_V7X = 60 * 1024 * 1024

N_DEV = 8
N_CHIPS = 4
MESH_ID = pl.DeviceIdType.MESH
ANY = pl.BlockSpec(memory_space=pl.ANY)
SMEM = pl.BlockSpec(memory_space=pltpu.SMEM)


def _cparams(n_axes):
    return pltpu.CompilerParams(dimension_semantics=("arbitrary",) * n_axes, vmem_limit_bytes=VMEM_LIMIT_V7X)


_GELU_C = math.sqrt(2.0 / math.pi)
_GELU_C3 = _GELU_C * 0.044715


def _gelu(x):
    t = jnp.tanh(x * (_GELU_C3 * (x * x) + _GELU_C))
    return x * (0.5 * t + 0.5)


def _gelu_and_grad(x):
    x2 = x * x
    t = jnp.tanh(x * (_GELU_C3 * x2 + _GELU_C))
    cdf = 0.5 * t + 0.5
    d = cdf + (x * (cdf * (1.0 - cdf))) * ((6.0 * _GELU_C3) * x2 + 2.0 * _GELU_C)
    return x * cdf, d


def _sigmoid(x):
    return 0.5 + 0.5 * jnp.tanh(0.5 * x)


def _silu(x):
    return x * _sigmoid(x)


def _silu_and_grad(x):
    s = _sigmoid(x)
    return x * s, s * (1.0 + x * (1.0 - s))


def _dot(a, b):
    return jnp.dot(a, b, preferred_element_type=F32)


def _dot_nt(a, b):
    return lax.dot_general(a, b, (((1,), (1,)), ((), ())), preferred_element_type=F32)


def _dot_tn(a, b):
    return lax.dot_general(a, b, (((0,), (0,)), ((), ())), preferred_element_type=F32)


def _lo_mask(shape):
    return lax.broadcasted_iota(jnp.int32, shape, 1) < HEAD_DIM


def _swap_halves(v):
    return pltpu.roll(v, HEAD_DIM, 1)


def _window_buckets():
    q_loc = jnp.arange(CHUNK)[:, None]
    j_loc = jnp.arange(CHUNK)[None, :]
    d = q_loc - j_loc + jnp.where(j_loc > q_loc, CHUNK, 0)
    max_exact = REL_BUCKETS // 2
    safe = jnp.maximum(d, 1).astype(F32)
    large = max_exact + (jnp.log(safe / max_exact) / math.log(CHUNK / max_exact)
                         * (REL_BUCKETS - max_exact)).astype(jnp.int32)
    large = jnp.minimum(large, REL_BUCKETS - 1)
    return jnp.where(d < max_exact, d, large).astype(jnp.int32)


def _bias_table(rel_bias, bucket):
    def body(rb_ref, bk_ref, out_ref):
        bk = bk_ref[...]
        for h in range(HEADS):
            acc = jnp.zeros(bk.shape, F32)
            for b in range(REL_BUCKETS):
                acc = jnp.where(bk == b, rb_ref[b, h], acc)
            out_ref[h] = acc

    vmem = pl.BlockSpec(memory_space=pltpu.VMEM)
    return pl.pallas_call(
        body, name="bias_table",
        out_shape=jax.ShapeDtypeStruct((HEADS, CHUNK, CHUNK), F32),
        in_specs=[SMEM, vmem], out_specs=vmem,
    )(rel_bias, bucket)


def _rel_bias_grad(dbias, bucket):
    n_layers = len(dbias)

    def body(*refs):
        db_refs, (bk_ref, out_ref, dsink_ref) = refs[:n_layers], refs[n_layers:]
        bk = bk_ref[...]
        for h in range(HEADS):
            tot = jnp.zeros((CHUNK, CHUNK), F32)
            for l in range(n_layers):
                ds = db_refs[l][h]
                dsink_ref[l, h] = -jnp.sum(ds)
                tot = tot + ds
            for b in range(REL_BUCKETS):
                out_ref[b, h] = jnp.sum(jnp.where(bk == b, tot, 0.0))

    vmem = pl.BlockSpec(memory_space=pltpu.VMEM)
    return pl.pallas_call(
        body, name="rel_bias_grad",
        out_shape=(jax.ShapeDtypeStruct((REL_BUCKETS, HEADS), F32), jax.ShapeDtypeStruct((n_layers, HEADS), F32)),
        in_specs=[vmem] * (n_layers + 1), out_specs=(SMEM, SMEM),
    )(*dbias, bucket)


def _spatial_tables(w_spatial, b_spatial_t):
    n_layers = w_spatial.shape[0]

    def body(w_ref, b_ref, wcat_ref, wtcat_ref, bs_ref):
        row = lax.broadcasted_iota(jnp.int32, (CHUNK, CHUNK), 0)
        col = lax.broadcasted_iota(jnp.int32, (CHUNK, CHUNK), 1)
        causal = col <= row
        lo = _lo_mask((CHUNK, CHUNK))
        for p in range(4):
            for half in range(2):
                w = jnp.where(causal, w_ref[0, 2 * p + half], 0.0)
                wcat_ref[0, p, :, half * CHUNK:(half + 1) * CHUNK] = w.astype(BF16)
                wtcat_ref[0, p, :, half * CHUNK:(half + 1) * CHUNK] = w.T.astype(BF16)
            b = b_ref[0]
            bs_ref[0, p] = jnp.where(lo, b[:, 2 * p:2 * p + 1], b[:, 2 * p + 1:2 * p + 2])

    return pl.pallas_call(
        body, name="spatial_tables", grid=(n_layers,),
        out_shape=(jax.ShapeDtypeStruct((n_layers, 4, CHUNK, 2 * CHUNK), BF16),
                   jax.ShapeDtypeStruct((n_layers, 4, CHUNK, 2 * CHUNK), BF16),
                   jax.ShapeDtypeStruct((n_layers, 4, CHUNK, CHUNK), F32)),
        in_specs=[pl.BlockSpec((1, HEADS, CHUNK, CHUNK), lambda l: (l, 0, 0, 0)),
                  pl.BlockSpec((1, CHUNK, HEADS), lambda l: (l, 0, 0))],
        out_specs=(pl.BlockSpec((1, 4, CHUNK, 2 * CHUNK), lambda l: (l, 0, 0, 0)),
                   pl.BlockSpec((1, 4, CHUNK, 2 * CHUNK), lambda l: (l, 0, 0, 0)),
                   pl.BlockSpec((1, 4, CHUNK, CHUNK), lambda l: (l, 0, 0, 0))),
        compiler_params=_cparams(1),
    )(w_spatial, b_spatial_t)


def _b_spatial_grad(dmix):
    n_layers = len(dmix)

    def body(*refs):
        d_refs, out_ref = refs[:n_layers], refs[n_layers]
        lane = lax.broadcasted_iota(jnp.int32, (CHUNK, CHUNK), 1)
        for l in range(n_layers):
            acc = jnp.zeros((CHUNK, CHUNK), F32)
            for p in range(4):
                t = d_refs[l][:, p * CHUNK:(p + 1) * CHUNK]
                s_lo = jnp.sum(jnp.where(lane < HEAD_DIM, t, 0.0), axis=1, keepdims=True)
                s_hi = jnp.sum(jnp.where(lane < HEAD_DIM, 0.0, t), axis=1, keepdims=True)
                acc = jnp.where(lane == 2 * p, s_lo, acc)
                acc = jnp.where(lane == 2 * p + 1, s_hi, acc)
            out_ref[l] = acc

    vmem = pl.BlockSpec(memory_space=pltpu.VMEM)
    return pl.pallas_call(
        body, name="b_spatial_grad",
        out_shape=jax.ShapeDtypeStruct((n_layers, CHUNK, CHUNK), F32),
        in_specs=[vmem] * n_layers, out_specs=vmem,
    )(*dmix)


def _place():
    x, y, c = lax.axis_index("x"), lax.axis_index("y"), lax.axis_index("c")
    other_chips = [(1 - x, y), (x, 1 - y), (1 - x, 1 - y)]
    return x, y, c, other_chips


N_GATHER_SEMS = 12


def _gather_phases(shards, fulls, send_sems, recv_sems, local_sems):
    x, y, c, chips = _place()
    sibling = (x, y, 1 - c)
    n_arr = len(shards)

    def half_rows(a, chip, half):
        n = shards[a].shape[0]
        start = (2 * chip[0] + chip[1]) * n + half * (n // 2)
        return fulls[a].at[pl.ds(pl.multiple_of(start, 16), n // 2), :]

    def my_half(a):
        n = shards[a].shape[0]
        return shards[a].at[pl.ds(pl.multiple_of(c * (n // 2), 16), n // 2), :]

    def copy(k, a, src, chip, half, to):
        return pltpu.make_async_remote_copy(
            src_ref=src, dst_ref=half_rows(a, chip, half), send_sem=send_sems.at[n_arr * k + a],
            recv_sem=recv_sems.at[n_arr * k + a], device_id=to, device_id_type=MESH_ID)

    def local(a):
        n = shards[a].shape[0]
        mine = fulls[a].at[pl.ds(pl.multiple_of((2 * x + y) * n, 16), n), :]
        return pltpu.make_async_copy(shards[a], mine, local_sems.at[a])

    def first(k, a):
        return copy(k, a, my_half(a), (x, y), c, (chips[k][0], chips[k][1], c))

    def passed(k, a):
        return copy(3 + k, a, half_rows(a, chips[k], c), chips[k], c, sibling)

    def phase_a():
        for a in range(n_arr):
            local(a).start()
        for k in range(3):
            for a in range(n_arr):
                first(k, a).start()

    def phase_b():
        for k in range(3):
            for a in range(n_arr):
                copy(k, a, my_half(a), chips[k], c, sibling).wait_recv()
                passed(k, a).start()

    def phase_c():
        for k in range(3):
            for a in range(n_arr):
                copy(3 + k, a, my_half(a), chips[k], 1 - c, sibling).wait_recv()
        for k in range(3):
            for a in range(n_arr):
                first(k, a).wait_send()
                passed(k, a).wait_send()
        for a in range(n_arr):
            local(a).wait()

    return phase_a, phase_b, phase_c


N_EXCHANGE_SEMS = 14


def _exchange_phases(partials, parts, send_sems, recv_sems, local_sems, by_chip=True):
    x, y, c, chips = _place()
    me, sibling = (x, y, c), (x, y, 1 - c)
    n_arr = len(partials)

    def block(a, chip):
        if not by_chip:
            return partials[a]
        n = partials[a].shape[0] // N_CHIPS
        return partials[a].at[pl.ds(pl.multiple_of((2 * chip[0] + chip[1]) * n, 16), n), :]

    def slot(a, dev):
        return parts[a].at[4 * dev[0] + 2 * dev[1] + dev[2]]

    def copy(k, a, src, origin, to):
        return pltpu.make_async_remote_copy(
            src_ref=src, dst_ref=slot(a, origin), send_sem=send_sems.at[n_arr * k + a],
            recv_sem=recv_sems.at[n_arr * k + a], device_id=to, device_id_type=MESH_ID)

    def local(a):
        return pltpu.make_async_copy(block(a, (x, y)), slot(a, me), local_sems.at[a])

    def first(k, a):
        if k == 0:
            return copy(0, a, block(a, (x, y)), me, sibling)
        chip = chips[k - 1]
        return copy(k, a, block(a, chip), me, (chip[0], chip[1], c))

    def passed(k, a):
        origin = (chips[k][0], chips[k][1], c)
        return copy(4 + k, a, slot(a, origin), origin, sibling)

    def phase_a():
        for a in range(n_arr):
            local(a).start()
        for k in range(4):
            for a in range(n_arr):
                first(k, a).start()

    def phase_b():
        for k in range(3):
            for a in range(n_arr):
                copy(1 + k, a, block(a, (x, y)), (chips[k][0], chips[k][1], c), me).wait_recv()
                passed(k, a).start()

    def phase_c():
        for a in range(n_arr):
            copy(0, a, block(a, (x, y)), sibling, me).wait_recv()
        for k in range(3):
            for a in range(n_arr):
                copy(4 + k, a, block(a, (x, y)), (chips[k][0], chips[k][1], 1 - c), me).wait_recv()
        for k in range(4):
            for a in range(n_arr):
                first(k, a).wait_send()
        for k in range(3):
            for a in range(n_arr):
                passed(k, a).wait_send()
        for a in range(n_arr):
            local(a).wait()

    return phase_a, phase_b, phase_c


def _comm_scratch(n_sems):
    return [pltpu.SemaphoreType.DMA((n_sems,)), pltpu.SemaphoreType.DMA((n_sems,)), pltpu.SemaphoreType.DMA((2,))]


def _gather_weights(wt_shards, wo_shards, layer):
    wt_rows, wo_rows = wt_shards.shape[1], wo_shards.shape[1]

    def body(wt_ref, wo_ref, wt_full, wo_full, send_sems, recv_sems, local_sems):
        phases = _gather_phases((wt_ref.at[layer], wo_ref.at[layer]), (wt_full, wo_full), send_sems, recv_sems, local_sems)
        for phase in phases:
            phase()

    return pl.pallas_call(
        body, name="gather_weights",
        out_shape=(jax.ShapeDtypeStruct((N_CHIPS * wt_rows, D_MODEL), BF16),
                   jax.ShapeDtypeStruct((N_CHIPS * wo_rows, D_MODEL), BF16)),
        in_specs=[ANY, ANY], out_specs=(ANY, ANY), scratch_shapes=_comm_scratch(N_GATHER_SEMS),
    )(wt_shards, wo_shards)


def _allreduce_small(part):
    n_rows = part.shape[0]

    def body(p_ref, tot_ref, all_ref, send_sems, recv_sems, local_sem):
        x, y, c, chips = _place()
        me, sibling = (x, y, c), (x, y, 1 - c)

        def rows(dev):
            return all_ref.at[pl.ds(pl.multiple_of((4 * dev[0] + 2 * dev[1] + dev[2]) * n_rows, 8), n_rows), :]

        def copy(k, origin, to, src=None):
            return pltpu.make_async_remote_copy(
                src_ref=rows(origin) if src is None else src, dst_ref=rows(origin), send_sem=send_sems.at[k],
                recv_sem=recv_sems.at[k], device_id=to, device_id_type=MESH_ID)

        mine = pltpu.make_async_copy(p_ref, rows(me), local_sem)
        mine.start()
        first = [copy(0, me, sibling, src=p_ref)]
        first += [copy(1 + k, me, (chip[0], chip[1], c), src=p_ref) for k, chip in enumerate(chips)]
        for cp in first:
            cp.start()
        passed = []
        for k, chip in enumerate(chips):
            origin = (chip[0], chip[1], c)
            copy(1 + k, origin, me).wait_recv()
            fwd = copy(4 + k, origin, sibling)
            fwd.start()
            passed.append(fwd)
        copy(0, sibling, me).wait_recv()
        for k, chip in enumerate(chips):
            copy(4 + k, (chip[0], chip[1], 1 - c), me).wait_recv()
        for cp in first + passed:
            cp.wait_send()
        mine.wait()
        tot = all_ref[0:n_rows, :]
        for d in range(1, N_DEV):
            tot = tot + all_ref[d * n_rows:(d + 1) * n_rows, :]
        tot_ref[...] = tot

    vmem = pl.BlockSpec(memory_space=pltpu.VMEM)
    return pl.pallas_call(
        body, name="allreduce_small",
        out_shape=jax.ShapeDtypeStruct((n_rows, 128), F32),
        in_specs=[vmem], out_specs=vmem,
        scratch_shapes=[pltpu.VMEM((N_DEV * n_rows, 128), F32), pltpu.SemaphoreType.DMA((7,)),
                        pltpu.SemaphoreType.DMA((7,)), pltpu.SemaphoreType.DMA],
        compiler_params=pltpu.CompilerParams(vmem_limit_bytes=VMEM_LIMIT_V7X),
    )(part)


def _hosted(phases, step, n_steps, pass_on_step=None):
    phase_a, phase_b, phase_c = phases
    if pass_on_step is None:
        pass_on_step = (3 * n_steps) // 4

    def at_start():
        pl.when(step == 0)(phase_a)

    def at_end():
        pl.when(step == pass_on_step)(phase_b)
        pl.when(step == n_steps - 1)(phase_c)

    return at_start, at_end


def _layer_norm_stats(vv):
    mu = jnp.mean(vv, axis=-1, keepdims=True)
    xc = vv - mu
    rs = lax.rsqrt(jnp.mean(xc * xc, axis=-1, keepdims=True) + NORM_EPS)
    return xc * rs, mu, rs


STAT_POST_RMS, STAT_LN_MEAN, STAT_LN_RSTD = (slice(k * CHUNK, (k + 1) * CHUNK) for k in range(3))
N_STATS = 3


def _over_lanes(stat, width):
    return jnp.concatenate([stat] * (width // CHUNK), axis=1)


def _blockdiag(v, lo):
    zero = jnp.zeros_like(v)
    return jnp.concatenate([jnp.where(lo, v, zero), jnp.where(lo, zero, v)], axis=0)


def _softmax_sink(s, sink):
    m = jnp.maximum(jnp.max(s, axis=-1, keepdims=True), sink)
    e = jnp.exp(s - m)
    esink = jnp.exp(sink - m)
    den = jnp.sum(e, axis=-1, keepdims=True) + esink
    return e / den


def _kv_rows(cur_ref, halo_ref, r0, c):
    prev_in_tile = cur_ref[pl.ds(pl.multiple_of(jnp.maximum(r0 - CHUNK, 0), CHUNK), CHUNK), :]
    prev = jnp.where(c == 0, halo_ref[...], prev_in_tile)
    kv2 = jnp.concatenate([prev, cur_ref[pl.ds(r0, CHUNK), :]], axis=0)
    k2, v2 = kv2[:, 0:KV_WIDTH], kv2[:, KV_WIDTH:2 * KV_WIDTH]
    return (k2, _swap_halves(k2)), (v2, _swap_halves(v2))


def _window_square(over_keys, prev):
    return jnp.where(prev, over_keys[:, 0:CHUNK], over_keys[:, CHUNK:2 * CHUNK])


def _window_keys(square, prev):
    zero = jnp.zeros_like(square)
    return jnp.concatenate([jnp.where(prev, square, zero), jnp.where(prev, zero, square)], axis=1)


def _ahead(shape=(CHUNK, CHUNK)):
    return lax.broadcasted_iota(jnp.int32, shape, 1) - lax.broadcasted_iota(jnp.int32, shape, 0)


def _dead_mask(first, ahead):
    return ahead > jnp.where(first, 0, CHUNK)


def _head_of(p, half):
    return 2 * p + half, int(half != p // 2)


_HEADS_BY_COPY = tuple(tuple((p, half) for p in range(4) for half in range(2) if _head_of(p, half)[1] == sw)
                       for sw in range(2))


def _masked_halves(tile, lo):
    zero = jnp.zeros_like(tile)
    return {0: jnp.where(lo, tile, zero), 1: jnp.where(lo, zero, tile)}


def _query_stacks(q_tiles, lo):
    qm = {p: _masked_halves(q_tiles[p] * QK_SCALE, lo) for p in range(4)}
    return [jnp.concatenate([qm[p][half] for p, half in members], axis=0) for members in _HEADS_BY_COPY]


def _attention_probs(q_stacks, ks, bias_ref, sink_ref, dead, prev):
    probs = {}
    for sw, members in enumerate(_HEADS_BY_COPY):
        s_stack = _dot_nt(q_stacks[sw], ks[sw])
        for i, (p, half) in enumerate(members):
            head = 2 * p + half
            s = _window_square(s_stack[i * CHUNK:(i + 1) * CHUNK, :], prev) + bias_ref[head]
            s = jnp.where(dead, NEG, s)
            probs[p, half] = _softmax_sink(s, sink_ref[head]).astype(BF16)
    return probs


def _attention_values(probs, vs, prev):
    outs, p_stacks = {}, []
    for sw, members in enumerate(_HEADS_BY_COPY):
        p_stack = jnp.concatenate([_window_keys(probs[m], prev) for m in members], axis=0)
        r_stack = _dot(p_stack, vs[sw])
        p_stacks.append(p_stack)
        for i, m in enumerate(members):
            outs[m] = r_stack[i * CHUNK:(i + 1) * CHUNK, :]
    return outs, p_stacks


def _fwd_layer(x, g_pre, w_in_t, ln_g, ln_b, wcat, bs, sinks, bias, w_out, g_post, seq_len, gather=None, target=None):
    n_tok = x.shape[0]
    tb = TOK_TILE
    n_chunks = tb // CHUNK
    n_tiles = n_tok // tb
    n_in = 11
    assert gather is None or target is None

    def body(*refs):
        (x_ref, gpre_ref, wt_ref, lng_ref, lnb_ref, wcat_ref, bs_ref, sink_ref, bias_ref, wout_ref,
         gpost_ref) = refs[:n_in]
        at_start = at_end = lambda: None
        if gather is not None:
            (wts_ref, wos_ref, a_ref, q_ref, kv_ref, bz_ref, probs_ref, stats_ref, mixed_ref, xn_ref, y_ref, wt_full, wo_full,
             ycat_s, halo_ref, send_sems, recv_sems, local_sems) = refs[n_in:]
            phases = _gather_phases((wts_ref.at[gather[2]], wos_ref.at[gather[2]]), (wt_full, wo_full), send_sems,
                                    recv_sems, local_sems)
            at_start, at_end = _hosted(phases, pl.program_id(0), n_tiles)
        elif target is not None:
            (target_ref, a_ref, q_ref, kv_ref, bz_ref, probs_ref, stats_ref, mixed_ref, xn_ref, y_ref, loss_ref, ycat_s,
             halo_ref) = refs[n_in:]
        else:
            a_ref, q_ref, kv_ref, bz_ref, probs_ref, stats_ref, mixed_ref, xn_ref, y_ref, ycat_s, halo_ref = refs[n_in:]
        at_start()
        i = pl.program_id(0)
        lo = _lo_mask((CHUNK, CHUNK))
        ahead = _ahead()
        prev = ahead > 0

        @pl.when(i == 0)
        def _():
            halo_ref[...] = jnp.zeros_like(halo_ref)

        xf = x_ref[...]
        r1 = lax.rsqrt(jnp.mean(xf * xf, axis=-1, keepdims=True) + NORM_EPS)
        h = ((xf * r1) * gpre_ref[...]).astype(BF16)
        a_ref[...] = _dot_nt(h, wt_ref[0:Q_OFF, :])
        q_ref[...] = _dot_nt(h, wt_ref[Q_OFF:K_OFF, :]).astype(BF16)
        kv_ref[...] = _dot_nt(h, wt_ref[K_OFF:BZ_OFF, :]).astype(BF16)
        bz_ref[...] = _dot_nt(h, wt_ref[BZ_OFF:IN_WIDTH, :])

        def chunk(c, carry):
            r0 = pl.multiple_of(c * CHUNK, CHUNK)
            rows = pl.ds(r0, CHUNK)
            u = _gelu(a_ref[rows, 0:A_WIDTH])
            vv = _gelu(a_ref[rows, A_WIDTH:2 * A_WIDTH])
            xhat, mu, rs = _layer_norm_stats(vv)
            stats_ref[rows, STAT_LN_MEAN] = jnp.broadcast_to(mu, (CHUNK, CHUNK))
            stats_ref[rows, STAT_LN_RSTD] = jnp.broadcast_to(rs, (CHUNK, CHUNK))
            vnb = (xhat * lng_ref[...] + lnb_ref[...]).astype(BF16)
            for p in range(4):
                blk = slice(p * CHUNK, (p + 1) * CHUNK)
                mixed = _dot(wcat_ref[p], _blockdiag(vnb[:, blk], lo)) + bs_ref[p]
                mixed_ref[rows, blk] = mixed
                sz = _silu(a_ref[rows, 2 * A_WIDTH + p * CHUNK:2 * A_WIDTH + (p + 1) * CHUNK])
                ycat_s[rows, blk] = ((u[:, blk] * mixed) * sz).astype(BF16)
            ks, vs = _kv_rows(kv_ref, halo_ref, r0, c)
            dead = _dead_mask(lax.rem(i * tb + r0, seq_len) == 0, ahead)
            q_tiles = [q_ref[rows, p * CHUNK:(p + 1) * CHUNK] for p in range(4)]
            probs = _attention_probs(_query_stacks(q_tiles, lo), ks, bias_ref, sink_ref, dead, prev)
            for (p, half), prob in probs.items():
                head = 2 * p + half
                probs_ref[rows, head * CHUNK:(head + 1) * CHUNK] = prob
            outs, _ = _attention_values(probs, vs, prev)
            for p in range(4):
                blk = slice(p * CHUNK, (p + 1) * CHUNK)
                o = jnp.where(lo, outs[p, 0], outs[p, 1])
                ycat_s[rows, B_WIDTH + p * CHUNK:B_WIDTH + (p + 1) * CHUNK] = (o * _silu(bz_ref[rows, blk])).astype(BF16)
            return carry

        lax.fori_loop(0, n_chunks, chunk, 0, unroll=CHUNK_UNROLL)
        halo_ref[...] = kv_ref[tb - CHUNK:tb, :]
        y = _dot(ycat_s[...], wout_ref[...])
        r = lax.rsqrt(jnp.mean(y * y, axis=-1, keepdims=True) + NORM_EPS)
        y_ref[...] = y
        stats_ref[:, STAT_POST_RMS] = jnp.broadcast_to(r, (tb, CHUNK))
        xn = x_ref[...] + (y * r) * gpost_ref[...]
        if target is None:
            xn_ref[...] = xn
        else:
            d = xn - target_ref[...]
            xn_ref[...] = d * (1.0 / D_MODEL)

            @pl.when(i == 0)
            def _():
                loss_ref[0, 0] = 0.0

            loss_ref[0, 0] += 0.5 * jnp.sum(jnp.mean(d * d, axis=-1, keepdims=True))
        at_end()

    tile = lambda w: pl.BlockSpec((tb, w), lambda i: (i, 0))
    whole = lambda shape, **kw: pl.BlockSpec(shape, lambda i: (0,) * len(shape), **kw)
    in_specs = [tile(D_MODEL), whole((1, D_MODEL)), whole((IN_WIDTH, D_MODEL), pipeline_mode=pl.Buffered(1)),
                whole((1, A_WIDTH)), whole((1, A_WIDTH)), whole((4, CHUNK, 2 * CHUNK)), whole((4, CHUNK, CHUNK)), SMEM,
                whole((HEADS, CHUNK, CHUNK)), whole((D_MODEL, D_MODEL)), whole((1, D_MODEL))]
    out_shape = [jax.ShapeDtypeStruct((n_tok, Q_OFF), F32), jax.ShapeDtypeStruct((n_tok, B_WIDTH), BF16),
                 jax.ShapeDtypeStruct((n_tok, 2 * KV_WIDTH), BF16), jax.ShapeDtypeStruct((n_tok, B_WIDTH), F32),
                 jax.ShapeDtypeStruct((n_tok, HEADS * CHUNK), BF16), jax.ShapeDtypeStruct((n_tok, N_STATS * CHUNK), F32),
                 jax.ShapeDtypeStruct((n_tok, A_WIDTH), F32),
                 jax.ShapeDtypeStruct((n_tok, D_MODEL), F32), jax.ShapeDtypeStruct((n_tok, D_MODEL), F32)]
    out_specs = [tile(Q_OFF), tile(B_WIDTH), tile(2 * KV_WIDTH), tile(B_WIDTH), tile(HEADS * CHUNK), tile(N_STATS * CHUNK),
                 tile(A_WIDTH), tile(D_MODEL), tile(D_MODEL)]
    scratch = [pltpu.VMEM((tb, D_MODEL), BF16), pltpu.VMEM((CHUNK, 2 * KV_WIDTH), BF16)]
    args = [x, g_pre, w_in_t, ln_g, ln_b, wcat, bs, sinks, bias, w_out, g_post]
    if gather is not None:
        wt_shards, wo_shards, _ = gather
        in_specs += [ANY, ANY]
        args += [wt_shards, wo_shards]
        out_shape += [jax.ShapeDtypeStruct((N_CHIPS * wt_shards.shape[1], D_MODEL), BF16),
                      jax.ShapeDtypeStruct((N_CHIPS * wo_shards.shape[1], D_MODEL), BF16)]
        out_specs += [ANY, ANY]
        scratch += _comm_scratch(N_GATHER_SEMS)
    if target is not None:
        in_specs.append(tile(D_MODEL))
        args.append(target)
        out_shape.append(jax.ShapeDtypeStruct((1, 1), F32))
        out_specs.append(SMEM)
    name = "fwd_layer" + ("" if gather is None else "_gather") + ("" if target is None else "_loss")
    return pl.pallas_call(
        body, name=name, grid=(n_tiles,),
        out_shape=tuple(out_shape), in_specs=in_specs, out_specs=tuple(out_specs), scratch_shapes=scratch,
        compiler_params=_cparams(1),
    )(*args)


def _bwd_mix(dout, y, a, q, kv, bz, probs, stats, mixed, ln_g, ln_b, wtcat, w_out, g_post, exchange=None, allgather=None):
    n_tok = y.shape[0]
    tb = TOK_TILE
    n_chunks = tb // CHUNK
    n_tiles = n_tok // tb

    def body(*refs):
        refs = list(refs)
        take = lambda n: [refs.pop(0) for _ in range(n)]
        (dout_ref, y_ref, a_ref, q_ref, kv_ref, halo_ref, bz_ref, probs_ref, stats_ref, mixed_ref, lng_ref, lnb_ref,
         wtcat_ref, wout_ref, gpost_ref) = take(15)
        if exchange is not None:
            dwt_in, dwo_in = take(2)
        if allgather is not None:
            (small_in,) = take(1)
        dproj_ref, dwout_ref, dws_ref, dmix_ref, dlng_ref, dlnb_ref, dgpost_ref, dbias_ref = take(8)
        if exchange is not None:
            pt_ref, po_ref = take(2)
        if allgather is not None:
            (small_all,) = take(1)
        ycat_s, dy_s, dyc_s, dkv_s, carry_s, dwout_s, stage_s, dws_s = take(8)
        starts, ends = [], []
        if exchange is not None:
            hosted = _hosted(_exchange_phases((dwt_in, dwo_in), (pt_ref, po_ref), *take(3)), pl.program_id(0), n_tiles)
            starts.append(hosted[0])
            ends.append(hosted[1])
        if allgather is not None:
            hosted = _hosted(_exchange_phases((small_in,), (small_all,), *take(3), by_chip=False), pl.program_id(0),
                             n_tiles)
            starts.append(hosted[0])
            ends.append(hosted[1])
        at_start = lambda: [f() for f in starts]
        at_end = lambda: [f() for f in ends]
        at_start()
        step = pl.program_id(0)
        lo = _lo_mask((CHUNK, CHUNK))
        prev = _ahead() > 0

        @pl.when(step == 0)
        def _():
            dwout_s[...] = jnp.zeros_like(dwout_s)
            dws_s[...] = jnp.zeros_like(dws_s)
            dmix_ref[...] = jnp.zeros_like(dmix_ref)
            dlng_ref[...] = jnp.zeros_like(dlng_ref)
            dlnb_ref[...] = jnp.zeros_like(dlnb_ref)
            dgpost_ref[...] = jnp.zeros_like(dgpost_ref)
            dbias_ref[...] = jnp.zeros_like(dbias_ref)
            carry_s[...] = jnp.zeros_like(carry_s)

        for r0 in range(0, tb, CHUNK):
            rows = slice(r0, r0 + CHUNK)
            yv = y_ref[rows, :]
            dout = dout_ref[rows, :]
            r = _over_lanes(stats_ref[rows, STAT_POST_RMS], D_MODEL)
            yn = yv * r
            dgpost_ref[...] += jnp.sum(dout * yn, axis=0, keepdims=True)
            dyn = dout * gpost_ref[...]
            dy_s[rows, :] = (r * (dyn - yn * jnp.mean(dyn * yn, axis=-1, keepdims=True))).astype(BF16)
        dyc_s[...] = _dot_nt(dy_s[...], wout_ref[...])
        dkv_s[0:tb, :] = jnp.zeros((tb, 2 * KV_WIDTH), F32)
        dkv_s[tb:tb + CHUNK, :] = carry_s[...]

        def chunk(c, carry):
            r0 = pl.multiple_of(c * CHUNK, CHUNK)
            rows = pl.ds(r0, CHUNK)
            u, gu = _gelu_and_grad(a_ref[rows, 0:A_WIDTH])
            vv, gv = _gelu_and_grad(a_ref[rows, A_WIDTH:2 * A_WIDTH])
            rs = _over_lanes(stats_ref[rows, STAT_LN_RSTD], A_WIDTH)
            xhat = (vv - _over_lanes(stats_ref[rows, STAT_LN_MEAN], A_WIDTH)) * rs
            vnb = (xhat * lng_ref[...] + lnb_ref[...]).astype(BF16)
            d_vn, d_u, d_az = [], [], []
            for p in range(4):
                blk = slice(p * CHUNK, (p + 1) * CHUNK)
                mixed = mixed_ref[rows, blk]
                sz, gz = _silu_and_grad(a_ref[rows, 2 * A_WIDTH + p * CHUNK:2 * A_WIDTH + (p + 1) * CHUNK])
                ub = u[:, blk]
                dya = dyc_s[rows, blk]
                um = ub * mixed
                ycat_s[rows, blk] = (um * sz).astype(BF16)
                d_mixed = (dya * ub) * sz
                d_u.append((dya * mixed) * sz)
                d_az.append((dya * um) * gz)
                dmix_ref[:, blk] += d_mixed
                dmbd = _blockdiag(d_mixed.astype(BF16), lo)
                d_vn.append(_dot(wtcat_ref[p], dmbd))
                dws_s[p] += _dot_nt(dmbd, vnb[:, blk])
            d_vn = jnp.concatenate(d_vn, axis=1)
            dlng_ref[...] += jnp.sum(d_vn * xhat, axis=0, keepdims=True)
            dlnb_ref[...] += jnp.sum(d_vn, axis=0, keepdims=True)
            dxh = d_vn * lng_ref[...]
            d_vv = rs * (dxh - jnp.mean(dxh, axis=-1, keepdims=True)
                         - xhat * jnp.mean(dxh * xhat, axis=-1, keepdims=True))
            dproj_ref[rows, 0:A_WIDTH] = (jnp.concatenate(d_u, axis=1) * gu).astype(BF16)
            dproj_ref[rows, A_WIDTH:2 * A_WIDTH] = (d_vv * gv).astype(BF16)
            dproj_ref[rows, 2 * A_WIDTH:Q_OFF] = jnp.concatenate(d_az, axis=1).astype(BF16)
            ks, vs = _kv_rows(kv_ref, halo_ref, r0, c)
            q_stacks = _query_stacks([q_ref[rows, p * CHUNK:(p + 1) * CHUNK] for p in range(4)], lo)
            probs = {(p, half): probs_ref[rows, (2 * p + half) * CHUNK:(2 * p + half + 1) * CHUNK]
                     for p in range(4) for half in range(2)}
            outs_h, p_stacks = _attention_values(probs, vs, prev)
            dom = {}
            for p in range(4):
                blk = slice(p * CHUNK, (p + 1) * CHUNK)
                sb, gb = _silu_and_grad(bz_ref[rows, blk])
                dyb = dyc_s[rows, B_WIDTH + p * CHUNK:B_WIDTH + (p + 1) * CHUNK]
                o = jnp.where(lo, outs_h[p, 0], outs_h[p, 1])
                ycat_s[rows, B_WIDTH + p * CHUNK:B_WIDTH + (p + 1) * CHUNK] = (o * sb).astype(BF16)
                dproj_ref[rows, BZ_OFF + p * CHUNK:BZ_OFF + (p + 1) * CHUNK] = ((dyb * o) * gb).astype(BF16)
                dom[p] = _masked_halves((dyb * sb).astype(BF16), lo)
            dqs, dk_by_copy, dv_by_copy = {}, [], []
            for sw, members in enumerate(_HEADS_BY_COPY):
                do_stack = jnp.concatenate([dom[p][half] for p, half in members], axis=0)
                dp_stack = _dot_nt(do_stack, vs[sw])
                dsws = []
                for i, (p, half) in enumerate(members):
                    head = 2 * p + half
                    dp = _window_square(dp_stack[i * CHUNK:(i + 1) * CHUNK, :], prev)
                    prob = probs[p, half].astype(F32)
                    delta = jnp.sum(prob * dp, axis=-1, keepdims=True)
                    ds = prob * (dp - delta)
                    dbias_ref[head] += ds
                    dsws.append(_window_keys(ds.astype(BF16), prev))
                ds_stack = jnp.concatenate(dsws, axis=0)
                dq_stack = _dot(ds_stack, ks[sw])
                for i, m in enumerate(members):
                    dqs[m] = dq_stack[i * CHUNK:(i + 1) * CHUNK, :]
                dk_by_copy.append(_dot_tn(ds_stack, q_stacks[sw]))
                dv_by_copy.append(_dot_tn(p_stacks[sw], do_stack))
            for p in range(4):
                dq = jnp.where(lo, dqs[p, 0], dqs[p, 1]) * QK_SCALE
                dproj_ref[rows, Q_OFF + p * CHUNK:Q_OFF + (p + 1) * CHUNK] = dq.astype(BF16)
            both = pl.ds(r0, 2 * CHUNK)
            dkv_s[both, 0:KV_WIDTH] += dk_by_copy[0] + _swap_halves(dk_by_copy[1])
            dkv_s[both, KV_WIDTH:2 * KV_WIDTH] += dv_by_copy[0] + _swap_halves(dv_by_copy[1])
            return carry

        lax.fori_loop(0, n_chunks, chunk, 0, unroll=BWD_CHUNK_UNROLL)
        dwout_s[...] += _dot_tn(ycat_s[...], dy_s[...])
        dproj_ref[:, K_OFF:BZ_OFF] = dkv_s[CHUNK:CHUNK + tb, :].astype(BF16)
        carry_s[...] = dkv_s[0:CHUNK, :]

        @pl.when(step == n_tiles - 1)
        def _():
            row = lax.broadcasted_iota(jnp.int32, (2 * CHUNK, CHUNK), 0)
            col = lax.broadcasted_iota(jnp.int32, (2 * CHUNK, CHUNK), 1)
            causal = col <= jnp.where(row >= CHUNK, row - CHUNK, row)
            for p in range(4):
                dws_ref[p] = jnp.where(causal, dws_s[p], 0.0).astype(BF16)
            rows = stage_s.shape[0]
            for r0 in range(0, D_MODEL, rows):
                stage_s[...] = dwout_s[r0:r0 + rows, :].astype(BF16)
                pltpu.sync_copy(stage_s, dwout_ref.at[r0:r0 + rows, :])

        at_end()

    tile = lambda w: pl.BlockSpec((tb, w), lambda s: (n_tiles - 1 - s, 0))
    whole = lambda shape, **kw: pl.BlockSpec(shape, lambda s: (0,) * len(shape), **kw)
    once = dict(pipeline_mode=pl.Buffered(1))
    in_specs = [tile(D_MODEL), tile(D_MODEL), tile(Q_OFF), tile(B_WIDTH), tile(2 * KV_WIDTH),
                pl.BlockSpec((CHUNK, 2 * KV_WIDTH), lambda s: (jnp.maximum((n_tiles - 1 - s) * n_chunks - 1, 0), 0)),
                tile(B_WIDTH), tile(HEADS * CHUNK), tile(N_STATS * CHUNK), tile(A_WIDTH), whole((1, A_WIDTH)), whole((1, A_WIDTH)),
                whole((4, CHUNK, 2 * CHUNK)), whole((D_MODEL, D_MODEL), **once), whole((1, D_MODEL))]
    args = [dout, y, a, q, kv, kv, bz, probs, stats, mixed, ln_g, ln_b, wtcat, w_out, g_post]
    small_shapes = [jax.ShapeDtypeStruct((D_MODEL, D_MODEL), BF16), jax.ShapeDtypeStruct((4, 2 * CHUNK, CHUNK), BF16),
                    jax.ShapeDtypeStruct((CHUNK, A_WIDTH), F32), jax.ShapeDtypeStruct((1, A_WIDTH), F32),
                    jax.ShapeDtypeStruct((1, A_WIDTH), F32), jax.ShapeDtypeStruct((1, D_MODEL), F32),
                    jax.ShapeDtypeStruct((HEADS, CHUNK, CHUNK), F32)]
    small_specs = [ANY, whole((4, 2 * CHUNK, CHUNK)), whole((CHUNK, A_WIDTH)), whole((1, A_WIDTH)), whole((1, A_WIDTH)),
                   whole((1, D_MODEL)), whole((HEADS, CHUNK, CHUNK))]
    scratch = [pltpu.VMEM((tb, D_MODEL), BF16), pltpu.VMEM((tb, D_MODEL), BF16), pltpu.VMEM((tb, D_MODEL), F32),
               pltpu.VMEM((tb + CHUNK, 2 * KV_WIDTH), F32), pltpu.VMEM((CHUNK, 2 * KV_WIDTH), F32),
               pltpu.VMEM((D_MODEL, D_MODEL), F32), pltpu.VMEM((2 * CHUNK, D_MODEL), BF16),
               pltpu.VMEM((4, 2 * CHUNK, CHUNK), F32)]
    out_shape = [jax.ShapeDtypeStruct((n_tok, IN_WIDTH), BF16)] + small_shapes
    out_specs = [tile(IN_WIDTH)] + small_specs
    if exchange is not None:
        dwt, dwo = exchange
        in_specs += [ANY, ANY]
        args += [dwt, dwo]
        out_shape += [jax.ShapeDtypeStruct((N_DEV, dwt.shape[0] // N_CHIPS, D_MODEL), BF16),
                      jax.ShapeDtypeStruct((N_DEV, dwo.shape[0] // N_CHIPS, D_MODEL), BF16)]
        out_specs += [ANY, ANY]
    if allgather is not None:
        in_specs.append(ANY)
        args.append(allgather)
        out_shape.append(jax.ShapeDtypeStruct((N_DEV,) + allgather.shape, allgather.dtype))
        out_specs.append(ANY)
    for hosted in (exchange, allgather):
        if hosted is not None:
            scratch += _comm_scratch(N_EXCHANGE_SEMS)
    return pl.pallas_call(
        body, name="bwd_mix" + ("" if exchange is None and allgather is None else "_exchange"), grid=(n_tiles,),
        out_shape=tuple(out_shape), in_specs=in_specs, out_specs=tuple(out_specs), scratch_shapes=scratch,
        compiler_params=_cparams(1),
    )(*args)


def _bwd_in(dproj, x, dout, g_pre, w_in_t):
    n_tok = x.shape[0]
    tm = TOK_TILE
    n_tiles = n_tok // tm

    def body(dp_ref, x_ref, dout_ref, g_ref, wt_ref, dx_ref, dwt_ref, dg_ref, acc_s):
        i = pl.program_id(0)

        @pl.when(i == 0)
        def _():
            acc_s[...] = jnp.zeros_like(acc_s)
            dg_ref[...] = jnp.zeros_like(dg_ref)

        xf = x_ref[...]
        r = lax.rsqrt(jnp.mean(xf * xf, axis=-1, keepdims=True) + NORM_EPS)
        xn = xf * r
        h = (xn * g_ref[...]).astype(BF16)
        dp = dp_ref[...]
        dh = _dot(dp, wt_ref[...])
        acc_s[...] += _dot_tn(dp, h)
        dg_ref[...] += jnp.sum(dh * xn, axis=0, keepdims=True)
        dhn = dh * g_ref[...]
        dx_ref[...] = dout_ref[...] + r * (dhn - xn * jnp.mean(dhn * xn, axis=-1, keepdims=True))

        @pl.when(i == n_tiles - 1)
        def _():
            dwt_ref[...] = acc_s[...].astype(BF16)

    return pl.pallas_call(
        body, name="bwd_in", grid=(n_tiles,),
        out_shape=(jax.ShapeDtypeStruct((n_tok, D_MODEL), F32), jax.ShapeDtypeStruct((IN_WIDTH, D_MODEL), BF16),
                   jax.ShapeDtypeStruct((1, D_MODEL), F32)),
        in_specs=[pl.BlockSpec((tm, IN_WIDTH), lambda i: (i, 0)), pl.BlockSpec((tm, D_MODEL), lambda i: (i, 0)),
                  pl.BlockSpec((tm, D_MODEL), lambda i: (i, 0)), pl.BlockSpec((1, D_MODEL), lambda i: (0, 0)),
                  pl.BlockSpec((IN_WIDTH, D_MODEL), lambda i: (0, 0), pipeline_mode=pl.Buffered(1))],
        out_specs=(pl.BlockSpec((tm, D_MODEL), lambda i: (i, 0)),
                   pl.BlockSpec((IN_WIDTH, D_MODEL), lambda i: (0, 0), pipeline_mode=pl.Buffered(1)),
                   pl.BlockSpec((1, D_MODEL), lambda i: (0, 0))),
        scratch_shapes=[pltpu.VMEM((IN_WIDTH, D_MODEL), F32)],
        compiler_params=_cparams(1),
    )(dproj, x, dout, g_pre, w_in_t)


def _riders(exchanges):
    args = [a for arrays, _ in exchanges for a in arrays]
    shapes = [jax.ShapeDtypeStruct((N_DEV, a.shape[0] // N_CHIPS if by_chip else a.shape[0]) + a.shape[1:], a.dtype)
              for arrays, by_chip in exchanges for a in arrays]
    scratch = [s for _ in exchanges for s in _comm_scratch(N_EXCHANGE_SEMS)]

    def bind(in_refs, out_refs, sem_refs, step, n_steps, pass_on_step=None):
        in_refs, out_refs, sem_refs = list(in_refs), list(out_refs), list(sem_refs)
        starts, ends = [], []
        for arrays, by_chip in exchanges:
            n = len(arrays)
            phases = _exchange_phases(tuple(in_refs[:n]), tuple(out_refs[:n]), *sem_refs[:3], by_chip=by_chip)
            del in_refs[:n], out_refs[:n], sem_refs[:3]
            at_start, at_end = _hosted(phases, step, n_steps, pass_on_step)
            starts.append(at_start)
            ends.append(at_end)
        return (lambda: [f() for f in starts]), (lambda: [f() for f in ends])

    return args, [ANY] * len(args), shapes, [ANY] * len(shapes), scratch, bind


def _bwd_in_dw(dproj, x, g_pre, exchanges):
    n_tok = x.shape[0]
    tm = TOK_TILE
    n_tiles = n_tok // tm
    r_args, r_in_specs, r_shapes, r_out_specs, r_scratch, bind = _riders(exchanges)

    def body(*refs):
        dp_ref, x_ref, g_ref = refs[:3]
        in_refs = refs[3:3 + len(r_args)]
        dwt_ref = refs[3 + len(r_args)]
        out_refs = refs[4 + len(r_args):4 + 2 * len(r_args)]
        acc_s = refs[4 + 2 * len(r_args)]
        i = pl.program_id(0)
        at_start, at_end = bind(in_refs, out_refs, refs[5 + 2 * len(r_args):], i, n_tiles)
        at_start()

        @pl.when(i == 0)
        def _():
            acc_s[...] = jnp.zeros_like(acc_s)

        xf = x_ref[...]
        r = lax.rsqrt(jnp.mean(xf * xf, axis=-1, keepdims=True) + NORM_EPS)
        acc_s[...] += _dot_tn(dp_ref[...], ((xf * r) * g_ref[...]).astype(BF16))

        @pl.when(i == n_tiles - 1)
        def _():
            dwt_ref[...] = acc_s[...].astype(BF16)

        at_end()

    return pl.pallas_call(
        body, name="bwd_in_dw", grid=(n_tiles,),
        out_shape=(jax.ShapeDtypeStruct((IN_WIDTH, D_MODEL), BF16), *r_shapes),
        in_specs=[pl.BlockSpec((tm, IN_WIDTH), lambda i: (i, 0)), pl.BlockSpec((tm, D_MODEL), lambda i: (i, 0)),
                  pl.BlockSpec((1, D_MODEL), lambda i: (0, 0))] + r_in_specs,
        out_specs=(pl.BlockSpec((IN_WIDTH, D_MODEL), lambda i: (0, 0), pipeline_mode=pl.Buffered(1)), *r_out_specs),
        scratch_shapes=[pltpu.VMEM((IN_WIDTH, D_MODEL), F32)] + r_scratch,
        compiler_params=_cparams(1),
    )(dproj, x, g_pre, *r_args)


def _bwd_in_dx(dproj, x, dout, g_pre, w_in_t, exchanges):
    n_tok = x.shape[0]
    tm = TOK_TILE
    n_tiles = n_tok // tm
    r_args, r_in_specs, r_shapes, r_out_specs, r_scratch, bind = _riders(exchanges)

    def body(*refs):
        dp_ref, x_ref, dout_ref, g_ref, wt_ref = refs[:5]
        in_refs = refs[5:5 + len(r_args)]
        dx_ref, dg_ref = refs[5 + len(r_args):7 + len(r_args)]
        out_refs = refs[7 + len(r_args):7 + 2 * len(r_args)]
        i = pl.program_id(0)
        at_start, at_end = bind(in_refs, out_refs, refs[7 + 2 * len(r_args):], i, n_tiles, pass_on_step=n_tiles - 2)
        at_start()

        @pl.when(i == 0)
        def _():
            dg_ref[...] = jnp.zeros_like(dg_ref)

        xf = x_ref[...]
        r = lax.rsqrt(jnp.mean(xf * xf, axis=-1, keepdims=True) + NORM_EPS)
        xn = xf * r
        dh = _dot(dp_ref[...], wt_ref[...])
        dg_ref[...] += jnp.sum(dh * xn, axis=0, keepdims=True)
        dhn = dh * g_ref[...]
        dx_ref[...] = dout_ref[...] + r * (dhn - xn * jnp.mean(dhn * xn, axis=-1, keepdims=True))
        at_end()

    tile = lambda w: pl.BlockSpec((tm, w), lambda i: (i, 0))
    return pl.pallas_call(
        body, name="bwd_in_dx", grid=(n_tiles,),
        out_shape=(jax.ShapeDtypeStruct((n_tok, D_MODEL), F32), jax.ShapeDtypeStruct((1, D_MODEL), F32), *r_shapes),
        in_specs=[tile(IN_WIDTH), tile(D_MODEL), tile(D_MODEL), pl.BlockSpec((1, D_MODEL), lambda i: (0, 0)),
                  pl.BlockSpec((IN_WIDTH, D_MODEL), lambda i: (0, 0), pipeline_mode=pl.Buffered(1))] + r_in_specs,
        out_specs=(tile(D_MODEL), pl.BlockSpec((1, D_MODEL), lambda i: (0, 0)), *r_out_specs),
        scratch_shapes=r_scratch,
        compiler_params=_cparams(1),
    )(dproj, x, dout, g_pre, w_in_t, *r_args)


def _adam_update(w, g, m, v):
    nm = ADAM_B1 * m + (1.0 - ADAM_B1) * g
    nv = ADAM_B2 * v + (1.0 - ADAM_B2) * (g * g)
    m_hat = nm / (1.0 - ADAM_B1 ** ADAM_STEP)
    v_hat = nv / (1.0 - ADAM_B2 ** ADAM_STEP)
    return -ADAM_LR * (m_hat / (jnp.sqrt(v_hat) + ADAM_EPS) + ADAM_WD * w), nm, nv


def _sum_slots(p_ref):
    tot = p_ref[0].astype(F32)
    for d in range(1, N_DEV):
        tot = tot + p_ref[d].astype(F32)
    return tot


def _adamw_parts(parts, w, m, v, name):
    n_layers, n_rows, n_cols = w.shape
    tr = n_rows
    while tr * n_cols > ADAMW_BLOCK_ELEMS:
        tr //= 2
    n_blocks = n_rows // tr

    def body(*refs):
        p_refs = refs[:n_layers]
        w_ref, m_ref, v_ref, g_ref, d_ref, nm_ref, nv_ref = refs[n_layers:]
        for k in range(n_layers):
            @pl.when(pl.program_id(0) == k)
            def _(k=k):
                g = _sum_slots(p_refs[k])
                g_ref[0] = g
                d_ref[0], nm_ref[0], nv_ref[0] = _adam_update(w_ref[0], g, m_ref[0], v_ref[0])

    def part_spec(k):
        return pl.BlockSpec((N_DEV, tr, n_cols), lambda l, i: (0, jnp.clip(i + (l - k) * n_blocks, 0, n_blocks - 1), 0))

    spec = pl.BlockSpec((1, tr, n_cols), lambda l, i: (l, i, 0))
    shape = jax.ShapeDtypeStruct((n_layers, n_rows, n_cols), F32)
    return pl.pallas_call(
        body, name="adamw_" + name, grid=(n_layers, n_blocks), out_shape=(shape,) * 4,
        in_specs=[part_spec(k) for k in range(n_layers)] + [spec, spec, spec], out_specs=(spec,) * 4,
        compiler_params=_cparams(2),
    )(*parts, w, m, v)


def _adamw_small(quads):
    n = len(quads)

    def body(*refs):
        ins, outs = refs[:4 * n], refs[4 * n:]
        for j in range(n):
            w_ref, g_ref, m_ref, v_ref = ins[4 * j:4 * j + 4]
            d_ref, nm_ref, nv_ref = outs[3 * j:3 * j + 3]
            d_ref[...], nm_ref[...], nv_ref[...] = _adam_update(w_ref[...], g_ref[...], m_ref[...], v_ref[...])

    vmem = pl.BlockSpec(memory_space=pltpu.VMEM)
    shapes = tuple(jax.ShapeDtypeStruct(w.shape, F32) for w, *_ in quads for _ in range(3))
    results = pl.pallas_call(
        body, name="adamw_small", out_shape=shapes, in_specs=[vmem] * (4 * n), out_specs=(vmem,) * (3 * n),
    )(*[a for quad in quads for a in quad])
    return [results[3 * j:3 * j + 3] for j in range(n)]


_SMALL = ("pre_norm_g", "ln_v_g", "ln_v_b", "b_spatial", "sinks", "rel_bias", "post_norm_g")


def _pack(pieces):
    blocks, first_rows, n = [], [], 0
    for p in pieces:
        flat = p.reshape(-1)
        flat = jnp.concatenate([flat, jnp.zeros(((-flat.shape[0]) % (8 * 128),), F32)]).reshape(-1, 128)
        blocks.append(flat)
        first_rows.append(n)
        n += flat.shape[0]
    return jnp.concatenate(blocks, axis=0), first_rows


def _unpack(block, first_row, shape):
    size = math.prod(shape)
    return block[first_row:first_row + -(-size // 128)].reshape(-1)[:size].reshape(shape)


def kernel(x, pre_norm_g, w_in, ln_v_g, ln_v_b, w_spatial, b_spatial, sinks, rel_bias, w_out, post_norm_g, loss_target, m_pre_norm_g, m_w_in, m_ln_v_g, m_ln_v_b, m_w_spatial, m_b_spatial, m_sinks, m_rel_bias, m_w_out, m_post_norm_g, v_pre_norm_g, v_w_in, v_ln_v_g, v_ln_v_b, v_w_spatial, v_b_spatial, v_sinks, v_rel_bias, v_w_out, v_post_norm_g):
    weights = dict(pre_norm_g=pre_norm_g, w_in=w_in, ln_v_g=ln_v_g, ln_v_b=ln_v_b, w_spatial=w_spatial, b_spatial=b_spatial,
                   sinks=sinks, rel_bias=rel_bias, w_out=w_out, post_norm_g=post_norm_g)
    mom_m = dict(pre_norm_g=m_pre_norm_g, w_in=m_w_in, ln_v_g=m_ln_v_g, ln_v_b=m_ln_v_b, w_spatial=m_w_spatial,
                 b_spatial=m_b_spatial, sinks=m_sinks, rel_bias=m_rel_bias, w_out=m_w_out, post_norm_g=m_post_norm_g)
    mom_v = dict(pre_norm_g=v_pre_norm_g, w_in=v_w_in, ln_v_g=v_ln_v_g, ln_v_b=v_ln_v_b, w_spatial=v_w_spatial,
                 b_spatial=v_b_spatial, sinks=v_sinks, rel_bias=v_rel_bias, w_out=v_w_out, post_norm_g=v_post_norm_g)
    n_seq, seq_len, _ = x.shape
    n_layers = w_in.shape[0]
    x2 = x.reshape(n_seq * seq_len, D_MODEL)
    target2 = loss_target.reshape(n_seq * seq_len, D_MODEL)
    row = lambda p, l: p[l][None]

    wt_shards = jnp.swapaxes(w_in, 1, 2).astype(BF16)
    wo_shards = w_out.astype(BF16)
    bucket = _window_buckets()
    bias = _bias_table(rel_bias, bucket)
    wcat, wtcat, bs = _spatial_tables(w_spatial, jnp.swapaxes(b_spatial, 1, 2))

    wt, wo = [None] * n_layers, [None] * n_layers
    wt[0], wo[0] = _gather_weights(wt_shards, wo_shards, 0)
    xs, saved = [x2], []
    for l in range(n_layers):
        layer_args = (xs[-1], row(pre_norm_g, l), wt[l], row(ln_v_g, l), row(ln_v_b, l), wcat[l], bs[l], sinks[l], bias,
                      wo[l], row(post_norm_g, l), seq_len)
        if l + 1 < n_layers:
            *acts, xn, y, wt[l + 1], wo[l + 1] = _fwd_layer(*layer_args, gather=(wt_shards, wo_shards, l + 1))
            xs.append(xn)
        else:
            *acts, dx, y, loss = _fwd_layer(*layer_args, target=target2)
        saved.append((y, *acts))

    small = [None] * n_layers
    parts = [None] * n_layers
    waiting = None
    for l in reversed(range(n_layers)):
        hosted = {} if waiting is None else dict(exchange=waiting[:2], allgather=waiting[2])
        dproj, *rest = _bwd_mix(dx, *saved[l], row(ln_v_g, l), row(ln_v_b, l), wtcat[l], wo[l], row(post_norm_g, l), **hosted)
        dwo, dws, dmix, dlng, dlnb, dgpost, dbias = rest[:7]
        dws = dws.reshape(HEADS * CHUNK, CHUNK)
        if waiting is not None:
            parts[l + 1] = list(rest[7:])
        if l > 0:
            dx, dwt, dgpre = _bwd_in(dproj, xs[l], dx, row(pre_norm_g, l), wt[l])
            waiting = (dwt, dwo, dws)
        else:
            dwt, po, ws_all = _bwd_in_dw(dproj, xs[l], row(pre_norm_g, l), [((dwo,), True), ((dws,), False)])
            dx, dgpre, pt = _bwd_in_dx(dproj, xs[l], dx, row(pre_norm_g, l), wt[l], [((dwt,), True)])
            parts[l] = [pt, po, ws_all]
        small[l] = dict(pre_norm_g=dgpre[0], ln_v_g=dlng[0], ln_v_b=dlnb[0], dmix=dmix, dbias=dbias,
                        post_norm_g=dgpost[0])

    db = _b_spatial_grad([s["dmix"] for s in small])
    pieces = {n: jnp.stack([s[n] for s in small]) for n in ("pre_norm_g", "ln_v_g", "ln_v_b", "post_norm_g")}
    pieces["b_spatial"] = jnp.swapaxes(db[:, :, :HEADS], 1, 2)
    pieces["rel_bias"], pieces["sinks"] = _rel_bias_grad([s["dbias"] for s in small], bucket)
    packed, first_rows = _pack([pieces[n] for n in _SMALL] + [loss])
    total = _allreduce_small(packed)
    grad = {n: _unpack(total, r, weights[n].shape) for n, r in zip(_SMALL, first_rows)}
    loss_out = total[first_rows[-1], 0]

    delta, new_m, new_v = {}, {}, {}
    two_d = lambda p: p.reshape(-1, p.shape[-1])
    updates = _adamw_small([tuple(two_d(p[n]) for p in (weights, grad, mom_m, mom_v)) for n in _SMALL])
    for n, update in zip(_SMALL, updates):
        delta[n], new_m[n], new_v[n] = (u.reshape(weights[n].shape) for u in update)

    t3 = lambda p: jnp.swapaxes(p, 1, 2)
    flat3 = lambda p: p.reshape(n_layers, HEADS * CHUNK, CHUNK)
    for n, which, view, back in (("w_in", 0, t3, t3), ("w_out", 1, lambda p: p, lambda p: p),
                                 ("w_spatial", 2, flat3, lambda p: p.reshape(w_spatial.shape))):
        results = _adamw_parts([parts[l][which] for l in range(n_layers)], view(weights[n]), view(mom_m[n]),
                               view(mom_v[n]), n)
        grad[n], delta[n], new_m[n], new_v[n] = (back(r) for r in results)

    names = tuple(weights)
    return (loss_out, dx.reshape(x.shape), *[grad[n] for n in names], *[delta[n] for n in names],
            *[new_m[n] for n in names], *[new_v[n] for n in names])
```

```python
import math

import jax
import jax.numpy as jnp
from jax import lax
from jax.experimental import pallas as pl
from jax.experimental.pallas import tpu as pltpu

F32 = jnp.float32
BF16 = jnp.bfloat16

D_MODEL = 1024
A_WIDTH = 512
B_WIDTH = 512
KV_WIDTH = 128
IN_WIDTH = 3 * A_WIDTH + 2 * B_WIDTH + 2 * KV_WIDTH
CHUNK = 128
HEADS = 8
HEAD_DIM = 64
REL_BUCKETS = 32
NORM_EPS = 1e-6
NEG = -1e30
Q_OFF = 3 * A_WIDTH
K_OFF = Q_OFF + B_WIDTH
BZ_OFF = K_OFF + 2 * KV_WIDTH
QK_SCALE = HEAD_DIM ** -0.5

ADAM_LR = 0.001
ADAM_B1 = 0.9
ADAM_B2 = 0.999
ADAM_EPS = 1e-08
ADAM_WD = 0.01
ADAM_STEP = 10

TOK_TILE = 512
CHUNK_UNROLL = 4
BWD_CHUNK_UNROLL = 4
ADAMW_BLOCK_ELEMS = 192 * 1024
VMEM_LIMIT_V7X = 60 * 1024 * 1024

N_DEV = 8
N_CHIPS = 4
MESH_ID = pl.DeviceIdType.MESH
ANY = pl.BlockSpec(memory_space=pl.ANY)
SMEM = pl.BlockSpec(memory_space=pltpu.SMEM)


def _cparams(n_axes):
    return pltpu.CompilerParams(dimension_semantics=("arbitrary",) * n_axes, vmem_limit_bytes=VMEM_LIMIT_V7X)


_GELU_C = math.sqrt(2.0 / math.pi)
_GELU_C3 = _GELU_C * 0.044715


def _gelu_and_grad(x):
    x2 = x * x
    t = jnp.tanh(x * (_GELU_C3 * x2 + _GELU_C))
    cdf = 0.5 * t + 0.5
    d = cdf + (x * (cdf * (1.0 - cdf))) * ((6.0 * _GELU_C3) * x2 + 2.0 * _GELU_C)
    return x * cdf, d


def _sigmoid(x):
    return 0.5 + 0.5 * jnp.tanh(0.5 * x)


def _silu(x):
    return x * _sigmoid(x)


def _silu_and_grad(x):
    s = _sigmoid(x)
    return x * s, s * (1.0 + x * (1.0 - s))


def _dot(a, b):
    return jnp.dot(a, b, preferred_element_type=F32)


def _dot_nt(a, b):
    return lax.dot_general(a, b, (((1,), (1,)), ((), ())), preferred_element_type=F32)


def _dot_tn(a, b):
    return lax.dot_general(a, b, (((0,), (0,)), ((), ())), preferred_element_type=F32)


def _lo_mask(shape):
    return lax.broadcasted_iota(jnp.int32, shape, 1) < HEAD_DIM


def _swap_halves(v):
    return pltpu.roll(v, HEAD_DIM, 1)


def _window_buckets():
    q_loc = jnp.arange(CHUNK)[:, None]
    j_loc = jnp.arange(CHUNK)[None, :]
    d = q_loc - j_loc + jnp.where(j_loc > q_loc, CHUNK, 0)
    max_exact = REL_BUCKETS // 2
    safe = jnp.maximum(d, 1).astype(F32)
    large = max_exact + (jnp.log(safe / max_exact) / math.log(CHUNK / max_exact)
                         * (REL_BUCKETS - max_exact)).astype(jnp.int32)
    large = jnp.minimum(large, REL_BUCKETS - 1)
    return jnp.where(d < max_exact, d, large).astype(jnp.int32)


def _bias_table(rel_bias, bucket):
    def body(rb_ref, bk_ref, out_ref):
        bk = bk_ref[...]
        for h in range(HEADS):
            acc = jnp.zeros(bk.shape, F32)
            for b in range(REL_BUCKETS):
                acc = jnp.where(bk == b, rb_ref[b, h], acc)
            out_ref[h] = acc

    vmem = pl.BlockSpec(memory_space=pltpu.VMEM)
    return pl.pallas_call(
        body, name="bias_table",
        out_shape=jax.ShapeDtypeStruct((HEADS, CHUNK, CHUNK), F32),
        in_specs=[SMEM, vmem], out_specs=vmem,
    )(rel_bias, bucket)


def _rel_bias_grad(dbias, bucket):
    n_layers = len(dbias)

    def body(*refs):
        db_refs, (bk_ref, out_ref, dsink_ref) = refs[:n_layers], refs[n_layers:]
        bk = bk_ref[...]
        for h in range(HEADS):
            tot = jnp.zeros((CHUNK, CHUNK), F32)
            for l in range(n_layers):
                ds = db_refs[l][h]
                dsink_ref[l, h] = -jnp.sum(ds)
                tot = tot + ds
            for b in range(REL_BUCKETS):
                out_ref[b, h] = jnp.sum(jnp.where(bk == b, tot, 0.0))

    vmem = pl.BlockSpec(memory_space=pltpu.VMEM)
    return pl.pallas_call(
        body, name="rel_bias_grad",
        out_shape=(jax.ShapeDtypeStruct((REL_BUCKETS, HEADS), F32), jax.ShapeDtypeStruct((n_layers, HEADS), F32)),
        in_specs=[vmem] * (n_layers + 1), out_specs=(SMEM, SMEM),
    )(*dbias, bucket)


def _spatial_tables(w_spatial, b_spatial_t):
    n_layers = w_spatial.shape[0]

    def body(w_ref, b_ref, wcat_ref, wtcat_ref, bs_ref):
        row = lax.broadcasted_iota(jnp.int32, (CHUNK, CHUNK), 0)
        col = lax.broadcasted_iota(jnp.int32, (CHUNK, CHUNK), 1)
        causal = col <= row
        lo = _lo_mask((CHUNK, CHUNK))
        for p in range(4):
            for half in range(2):
                w = jnp.where(causal, w_ref[0, 2 * p + half], 0.0)
                wcat_ref[0, p, :, half * CHUNK:(half + 1) * CHUNK] = w.astype(BF16)
                wtcat_ref[0, p, :, half * CHUNK:(half + 1) * CHUNK] = w.T.astype(BF16)
            b = b_ref[0]
            bs_ref[0, p] = jnp.where(lo, b[:, 2 * p:2 * p + 1], b[:, 2 * p + 1:2 * p + 2])

    return pl.pallas_call(
        body, name="spatial_tables", grid=(n_layers,),
        out_shape=(jax.ShapeDtypeStruct((n_layers, 4, CHUNK, 2 * CHUNK), BF16),
                   jax.ShapeDtypeStruct((n_layers, 4, CHUNK, 2 * CHUNK), BF16),
                   jax.ShapeDtypeStruct((n_layers, 4, CHUNK, CHUNK), F32)),
        in_specs=[pl.BlockSpec((1, HEADS, CHUNK, CHUNK), lambda l: (l, 0, 0, 0)),
                  pl.BlockSpec((1, CHUNK, HEADS), lambda l: (l, 0, 0))],
        out_specs=(pl.BlockSpec((1, 4, CHUNK, 2 * CHUNK), lambda l: (l, 0, 0, 0)),
                   pl.BlockSpec((1, 4, CHUNK, 2 * CHUNK), lambda l: (l, 0, 0, 0)),
                   pl.BlockSpec((1, 4, CHUNK, CHUNK), lambda l: (l, 0, 0, 0))),
        compiler_params=_cparams(1),
    )(w_spatial, b_spatial_t)


def _b_spatial_grad(dmix):
    n_layers = len(dmix)

    def body(*refs):
        d_refs, out_ref = refs[:n_layers], refs[n_layers]
        lane = lax.broadcasted_iota(jnp.int32, (CHUNK, CHUNK), 1)
        for l in range(n_layers):
            acc = jnp.zeros((CHUNK, CHUNK), F32)
            for p in range(4):
                t = d_refs[l][:, p * CHUNK:(p + 1) * CHUNK]
                s_lo = jnp.sum(jnp.where(lane < HEAD_DIM, t, 0.0), axis=1, keepdims=True)
                s_hi = jnp.sum(jnp.where(lane < HEAD_DIM, 0.0, t), axis=1, keepdims=True)
                acc = jnp.where(lane == 2 * p, s_lo, acc)
                acc = jnp.where(lane == 2 * p + 1, s_hi, acc)
            out_ref[l] = acc

    vmem = pl.BlockSpec(memory_space=pltpu.VMEM)
    return pl.pallas_call(
        body, name="b_spatial_grad",
        out_shape=jax.ShapeDtypeStruct((n_layers, CHUNK, CHUNK), F32),
        in_specs=[vmem] * n_layers, out_specs=vmem,
    )(*dmix)


def _place():
    x, y, c = lax.axis_index("x"), lax.axis_index("y"), lax.axis_index("c")
    other_chips = [(1 - x, y), (x, 1 - y), (1 - x, 1 - y)]
    return x, y, c, other_chips


N_GATHER_SEMS = 12


def _gather_phases(shards, fulls, send_sems, recv_sems, local_sems):
    x, y, c, chips = _place()
    sibling = (x, y, 1 - c)
    n_arr = len(shards)

    def half_rows(a, chip, half):
        n = shards[a].shape[0]
        start = (2 * chip[0] + chip[1]) * n + half * (n // 2)
        return fulls[a].at[pl.ds(pl.multiple_of(start, 16), n // 2), :]

    def my_half(a):
        n = shards[a].shape[0]
        return shards[a].at[pl.ds(pl.multiple_of(c * (n // 2), 16), n // 2), :]

    def copy(k, a, src, chip, half, to):
        return pltpu.make_async_remote_copy(
            src_ref=src, dst_ref=half_rows(a, chip, half), send_sem=send_sems.at[n_arr * k + a],
            recv_sem=recv_sems.at[n_arr * k + a], device_id=to, device_id_type=MESH_ID)

    def local(a):
        n = shards[a].shape[0]
        mine = fulls[a].at[pl.ds(pl.multiple_of((2 * x + y) * n, 16), n), :]
        return pltpu.make_async_copy(shards[a], mine, local_sems.at[a])

    def first(k, a):
        return copy(k, a, my_half(a), (x, y), c, (chips[k][0], chips[k][1], c))

    def passed(k, a):
        return copy(3 + k, a, half_rows(a, chips[k], c), chips[k], c, sibling)

    def phase_a():
        for a in range(n_arr):
            local(a).start()
        for k in range(3):
            for a in range(n_arr):
                first(k, a).start()

    def phase_b():
        for k in range(3):
            for a in range(n_arr):
                copy(k, a, my_half(a), chips[k], c, sibling).wait_recv()
                passed(k, a).start()

    def phase_c():
        for k in range(3):
            for a in range(n_arr):
                copy(3 + k, a, my_half(a), chips[k], 1 - c, sibling).wait_recv()
        for k in range(3):
            for a in range(n_arr):
                first(k, a).wait_send()
                passed(k, a).wait_send()
        for a in range(n_arr):
            local(a).wait()

    return phase_a, phase_b, phase_c


N_EXCHANGE_SEMS = 14


def _exchange_phases(partials, parts, send_sems, recv_sems, local_sems, by_chip=True):
    x, y, c, chips = _place()
    me, sibling = (x, y, c), (x, y, 1 - c)
    n_arr = len(partials)

    def block(a, chip):
        if not by_chip:
            return partials[a]
        n = partials[a].shape[0] // N_CHIPS
        return partials[a].at[pl.ds(pl.multiple_of((2 * chip[0] + chip[1]) * n, 16), n), :]

    def slot(a, dev):
        return parts[a].at[4 * dev[0] + 2 * dev[1] + dev[2]]

    def copy(k, a, src, origin, to):
        return pltpu.make_async_remote_copy(
            src_ref=src, dst_ref=slot(a, origin), send_sem=send_sems.at[n_arr * k + a],
            recv_sem=recv_sems.at[n_arr * k + a], device_id=to, device_id_type=MESH_ID)

    def local(a):
        return pltpu.make_async_copy(block(a, (x, y)), slot(a, me), local_sems.at[a])

    def first(k, a):
        if k == 0:
            return copy(0, a, block(a, (x, y)), me, sibling)
        chip = chips[k - 1]
        return copy(k, a, block(a, chip), me, (chip[0], chip[1], c))

    def passed(k, a):
        origin = (chips[k][0], chips[k][1], c)
        return copy(4 + k, a, slot(a, origin), origin, sibling)

    def phase_a():
        for a in range(n_arr):
            local(a).start()
        for k in range(4):
            for a in range(n_arr):
                first(k, a).start()

    def phase_b():
        for k in range(3):
            for a in range(n_arr):
                copy(1 + k, a, block(a, (x, y)), (chips[k][0], chips[k][1], c), me).wait_recv()
                passed(k, a).start()

    def phase_c():
        for a in range(n_arr):
            copy(0, a, block(a, (x, y)), sibling, me).wait_recv()
        for k in range(3):
            for a in range(n_arr):
                copy(4 + k, a, block(a, (x, y)), (chips[k][0], chips[k][1], 1 - c), me).wait_recv()
        for k in range(4):
            for a in range(n_arr):
                first(k, a).wait_send()
        for k in range(3):
            for a in range(n_arr):
                passed(k, a).wait_send()
        for a in range(n_arr):
            local(a).wait()

    return phase_a, phase_b, phase_c


def _comm_scratch(n_sems):
    return [pltpu.SemaphoreType.DMA((n_sems,)), pltpu.SemaphoreType.DMA((n_sems,)), pltpu.SemaphoreType.DMA((2,))]


def _gather_weights(wt_shards, wo_shards, layer):
    wt_rows, wo_rows = wt_shards.shape[1], wo_shards.shape[1]

    def body(wt_ref, wo_ref, wt_full, wo_full, send_sems, recv_sems, local_sems):
        phases = _gather_phases((wt_ref.at[layer], wo_ref.at[layer]), (wt_full, wo_full), send_sems, recv_sems, local_sems)
        for phase in phases:
            phase()

    return pl.pallas_call(
        body, name="gather_weights",
        out_shape=(jax.ShapeDtypeStruct((N_CHIPS * wt_rows, D_MODEL), BF16),
                   jax.ShapeDtypeStruct((N_CHIPS * wo_rows, D_MODEL), BF16)),
        in_specs=[ANY, ANY], out_specs=(ANY, ANY), scratch_shapes=_comm_scratch(N_GATHER_SEMS),
    )(wt_shards, wo_shards)


def _allreduce_small(part):
    n_rows = part.shape[0]

    def body(p_ref, tot_ref, all_ref, send_sems, recv_sems, local_sem):
        x, y, c, chips = _place()
        me, sibling = (x, y, c), (x, y, 1 - c)

        def rows(dev):
            return all_ref.at[pl.ds(pl.multiple_of((4 * dev[0] + 2 * dev[1] + dev[2]) * n_rows, 8), n_rows), :]

        def copy(k, origin, to, src=None):
            return pltpu.make_async_remote_copy(
                src_ref=rows(origin) if src is None else src, dst_ref=rows(origin), send_sem=send_sems.at[k],
                recv_sem=recv_sems.at[k], device_id=to, device_id_type=MESH_ID)

        mine = pltpu.make_async_copy(p_ref, rows(me), local_sem)
        mine.start()
        first = [copy(0, me, sibling, src=p_ref)]
        first += [copy(1 + k, me, (chip[0], chip[1], c), src=p_ref) for k, chip in enumerate(chips)]
        for cp in first:
            cp.start()
        passed = []
        for k, chip in enumerate(chips):
            origin = (chip[0], chip[1], c)
            copy(1 + k, origin, me).wait_recv()
            fwd = copy(4 + k, origin, sibling)
            fwd.start()
            passed.append(fwd)
        copy(0, sibling, me).wait_recv()
        for k, chip in enumerate(chips):
            copy(4 + k, (chip[0], chip[1], 1 - c), me).wait_recv()
        for cp in first + passed:
            cp.wait_send()
        mine.wait()
        tot = all_ref[0:n_rows, :]
        for d in range(1, N_DEV):
            tot = tot + all_ref[d * n_rows:(d + 1) * n_rows, :]
        tot_ref[...] = tot

    vmem = pl.BlockSpec(memory_space=pltpu.VMEM)
    return pl.pallas_call(
        body, name="allreduce_small",
        out_shape=jax.ShapeDtypeStruct((n_rows, 128), F32),
        in_specs=[vmem], out_specs=vmem,
        scratch_shapes=[pltpu.VMEM((N_DEV * n_rows, 128), F32), pltpu.SemaphoreType.DMA((7,)),
                        pltpu.SemaphoreType.DMA((7,)), pltpu.SemaphoreType.DMA],
        compiler_params=pltpu.CompilerParams(vmem_limit_bytes=VMEM_LIMIT_V7X),
    )(part)


def _hosted(phases, step, n_steps, pass_on_step=None):
    phase_a, phase_b, phase_c = phases
    if pass_on_step is None:
        pass_on_step = (3 * n_steps) // 4

    def at_start():
        pl.when(step == 0)(phase_a)

    def at_end():
        pl.when(step == pass_on_step)(phase_b)
        pl.when(step == n_steps - 1)(phase_c)

    return at_start, at_end


def _layer_norm_stats(vv):
    mu = jnp.mean(vv, axis=-1, keepdims=True)
    xc = vv - mu
    rs = lax.rsqrt(jnp.mean(xc * xc, axis=-1, keepdims=True) + NORM_EPS)
    return xc * rs, mu, rs


STAT_POST_RMS, STAT_LN_MEAN, STAT_LN_RSTD = (slice(k * CHUNK, (k + 1) * CHUNK) for k in range(3))
N_STATS = 3


def _over_lanes(stat, width):
    return jnp.concatenate([stat] * (width // CHUNK), axis=1)


def _blockdiag(v, lo):
    zero = jnp.zeros_like(v)
    return jnp.concatenate([jnp.where(lo, v, zero), jnp.where(lo, zero, v)], axis=0)


def _softmax_sink(s, sink):
    m = jnp.maximum(jnp.max(s, axis=-1, keepdims=True), sink)
    e = jnp.exp(s - m)
    esink = jnp.exp(sink - m)
    den = jnp.sum(e, axis=-1, keepdims=True) + esink
    return e / den


def _kv_rows(cur_ref, halo_ref, r0, c):
    prev_in_tile = cur_ref[pl.ds(pl.multiple_of(jnp.maximum(r0 - CHUNK, 0), CHUNK), CHUNK), :]
    prev = jnp.where(c == 0, halo_ref[...], prev_in_tile)
    kv2 = jnp.concatenate([prev, cur_ref[pl.ds(r0, CHUNK), :]], axis=0)
    k2, v2 = kv2[:, 0:KV_WIDTH], kv2[:, KV_WIDTH:2 * KV_WIDTH]
    return (k2, _swap_halves(k2)), (v2, _swap_halves(v2))


def _window_square(over_keys, prev):
    return jnp.where(prev, over_keys[:, 0:CHUNK], over_keys[:, CHUNK:2 * CHUNK])


def _window_keys(square, prev):
    zero = jnp.zeros_like(square)
    return jnp.concatenate([jnp.where(prev, square, zero), jnp.where(prev, zero, square)], axis=1)


def _ahead(shape=(CHUNK, CHUNK)):
    return lax.broadcasted_iota(jnp.int32, shape, 1) - lax.broadcasted_iota(jnp.int32, shape, 0)


def _dead_mask(first, ahead):
    return ahead > jnp.where(first, 0, CHUNK)


def _head_of(p, half):
    return 2 * p + half, int(half != p // 2)


_HEADS_BY_COPY = tuple(tuple((p, half) for p in range(4) for half in range(2) if _head_of(p, half)[1] == sw)
                       for sw in range(2))


def _masked_halves(tile, lo):
    zero = jnp.zeros_like(tile)
    return {0: jnp.where(lo, tile, zero), 1: jnp.where(lo, zero, tile)}


def _query_stacks(q_tiles, lo):
    qm = {p: _masked_halves(q_tiles[p] * QK_SCALE, lo) for p in range(4)}
    return [jnp.concatenate([qm[p][half] for p, half in members], axis=0) for members in _HEADS_BY_COPY]


def _attention_probs(q_stacks, ks, bias_ref, sink_ref, dead, prev):
    probs = {}
    for sw, members in enumerate(_HEADS_BY_COPY):
        s_stack = _dot_nt(q_stacks[sw], ks[sw])
        for i, (p, half) in enumerate(members):
            head = 2 * p + half
            s = _window_square(s_stack[i * CHUNK:(i + 1) * CHUNK, :], prev) + bias_ref[head]
            s = jnp.where(dead, NEG, s)
            probs[p, half] = _softmax_sink(s, sink_ref[head]).astype(BF16)
    return probs


def _attention_values(probs, vs, prev):
    outs, p_stacks = {}, []
    for sw, members in enumerate(_HEADS_BY_COPY):
        p_stack = jnp.concatenate([_window_keys(probs[m], prev) for m in members], axis=0)
        r_stack = _dot(p_stack, vs[sw])
        p_stacks.append(p_stack)
        for i, m in enumerate(members):
            outs[m] = r_stack[i * CHUNK:(i + 1) * CHUNK, :]
    return outs, p_stacks


def _fwd_layer(x, g_pre, w_in_t, ln_g, ln_b, wcat, bs, sinks, bias, w_out, g_post, seq_len, gather=None, target=None):
    n_tok = x.shape[0]
    tb = TOK_TILE
    n_chunks = tb // CHUNK
    n_tiles = n_tok // tb
    n_in = 11
    assert gather is None or target is None

    def body(*refs):
        (x_ref, gpre_ref, wt_ref, lng_ref, lnb_ref, wcat_ref, bs_ref, sink_ref, bias_ref, wout_ref,
         gpost_ref) = refs[:n_in]
        at_start = at_end = lambda: None
        if gather is not None:
            (wts_ref, wos_ref, a_ref, q_ref, kv_ref, bz_ref, probs_ref, stats_ref, mixed_ref, act_ref, xn_ref, y_ref, wt_full,
             wo_full, ycat_s, halo_ref, send_sems, recv_sems, local_sems) = refs[n_in:]
            phases = _gather_phases((wts_ref.at[gather[2]], wos_ref.at[gather[2]]), (wt_full, wo_full), send_sems,
                                    recv_sems, local_sems)
            at_start, at_end = _hosted(phases, pl.program_id(0), n_tiles)
        elif target is not None:
            (target_ref, a_ref, q_ref, kv_ref, bz_ref, probs_ref, stats_ref, mixed_ref, act_ref, xn_ref, y_ref, loss_ref,
             ycat_s, halo_ref) = refs[n_in:]
        else:
            (a_ref, q_ref, kv_ref, bz_ref, probs_ref, stats_ref, mixed_ref, act_ref, xn_ref, y_ref, ycat_s,
             halo_ref) = refs[n_in:]
        at_start()
        i = pl.program_id(0)
        lo = _lo_mask((CHUNK, CHUNK))
        ahead = _ahead()
        prev = ahead > 0

        @pl.when(i == 0)
        def _():
            halo_ref[...] = jnp.zeros_like(halo_ref)

        xf = x_ref[...]
        r1 = lax.rsqrt(jnp.mean(xf * xf, axis=-1, keepdims=True) + NORM_EPS)
        h = ((xf * r1) * gpre_ref[...]).astype(BF16)
        a_ref[...] = _dot_nt(h, wt_ref[0:Q_OFF, :])
        q_ref[...] = _dot_nt(h, wt_ref[Q_OFF:K_OFF, :]).astype(BF16)
        kv_ref[...] = _dot_nt(h, wt_ref[K_OFF:BZ_OFF, :]).astype(BF16)
        bz_ref[...] = _dot_nt(h, wt_ref[BZ_OFF:IN_WIDTH, :])

        def chunk(c, carry):
            r0 = pl.multiple_of(c * CHUNK, CHUNK)
            rows = pl.ds(r0, CHUNK)
            u, gu = _gelu_and_grad(a_ref[rows, 0:A_WIDTH])
            vv, gv = _gelu_and_grad(a_ref[rows, A_WIDTH:2 * A_WIDTH])
            for k, act in enumerate((u, gu, vv, gv)):
                act_ref[rows, k * A_WIDTH:(k + 1) * A_WIDTH] = act
            xhat, mu, rs = _layer_norm_stats(vv)
            stats_ref[rows, STAT_LN_MEAN] = jnp.broadcast_to(mu, (CHUNK, CHUNK))
            stats_ref[rows, STAT_LN_RSTD] = jnp.broadcast_to(rs, (CHUNK, CHUNK))
            vnb = (xhat * lng_ref[...] + lnb_ref[...]).astype(BF16)
            for p in range(4):
                blk = slice(p * CHUNK, (p + 1) * CHUNK)
                mixed = _dot(wcat_ref[p], _blockdiag(vnb[:, blk], lo)) + bs_ref[p]
                mixed_ref[rows, blk] = mixed
                sz = _silu(a_ref[rows, 2 * A_WIDTH + p * CHUNK:2 * A_WIDTH + (p + 1) * CHUNK])
                ycat_s[rows, blk] = ((u[:, blk] * mixed) * sz).astype(BF16)
            ks, vs = _kv_rows(kv_ref, halo_ref, r0, c)
            dead = _dead_mask(lax.rem(i * tb + r0, seq_len) == 0, ahead)
            q_tiles = [q_ref[rows, p * CHUNK:(p + 1) * CHUNK] for p in range(4)]
            probs = _attention_probs(_query_stacks(q_tiles, lo), ks, bias_ref, sink_ref, dead, prev)
            for (p, half), prob in probs.items():
                head = 2 * p + half
                probs_ref[rows, head * CHUNK:(head + 1) * CHUNK] = prob
            outs, _ = _attention_values(probs, vs, prev)
            for p in range(4):
                blk = slice(p * CHUNK, (p + 1) * CHUNK)
                o = jnp.where(lo, outs[p, 0], outs[p, 1])
                ycat_s[rows, B_WIDTH + p * CHUNK:B_WIDTH + (p + 1) * CHUNK] = (o * _silu(bz_ref[rows, blk])).astype(BF16)
            return carry

        lax.fori_loop(0, n_chunks, chunk, 0, unroll=CHUNK_UNROLL)
        halo_ref[...] = kv_ref[tb - CHUNK:tb, :]
        y = _dot(ycat_s[...], wout_ref[...])
        r = lax.rsqrt(jnp.mean(y * y, axis=-1, keepdims=True) + NORM_EPS)
        y_ref[...] = y
        stats_ref[:, STAT_POST_RMS] = jnp.broadcast_to(r, (tb, CHUNK))
        xn = x_ref[...] + (y * r) * gpost_ref[...]
        if target is None:
            xn_ref[...] = xn
        else:
            d = xn - target_ref[...]
            xn_ref[...] = d * (1.0 / D_MODEL)

            @pl.when(i == 0)
            def _():
                loss_ref[0, 0] = 0.0

            loss_ref[0, 0] += 0.5 * jnp.sum(jnp.mean(d * d, axis=-1, keepdims=True))
        at_end()

    tile = lambda w: pl.BlockSpec((tb, w), lambda i: (i, 0))
    whole = lambda shape, **kw: pl.BlockSpec(shape, lambda i: (0,) * len(shape), **kw)
    in_specs = [tile(D_MODEL), whole((1, D_MODEL)), whole((IN_WIDTH, D_MODEL), pipeline_mode=pl.Buffered(1)),
                whole((1, A_WIDTH)), whole((1, A_WIDTH)), whole((4, CHUNK, 2 * CHUNK)), whole((4, CHUNK, CHUNK)), SMEM,
                whole((HEADS, CHUNK, CHUNK)), whole((D_MODEL, D_MODEL)), whole((1, D_MODEL))]
    out_shape = [jax.ShapeDtypeStruct((n_tok, Q_OFF), F32), jax.ShapeDtypeStruct((n_tok, B_WIDTH), BF16),
                 jax.ShapeDtypeStruct((n_tok, 2 * KV_WIDTH), BF16), jax.ShapeDtypeStruct((n_tok, B_WIDTH), F32),
                 jax.ShapeDtypeStruct((n_tok, HEADS * CHUNK), BF16), jax.ShapeDtypeStruct((n_tok, N_STATS * CHUNK), F32),
                 jax.ShapeDtypeStruct((n_tok, A_WIDTH), F32), jax.ShapeDtypeStruct((n_tok, 4 * A_WIDTH), F32),
                 jax.ShapeDtypeStruct((n_tok, D_MODEL), F32), jax.ShapeDtypeStruct((n_tok, D_MODEL), F32)]
    out_specs = [tile(Q_OFF), tile(B_WIDTH), tile(2 * KV_WIDTH), tile(B_WIDTH), tile(HEADS * CHUNK), tile(N_STATS * CHUNK),
                 tile(A_WIDTH), tile(4 * A_WIDTH), tile(D_MODEL), tile(D_MODEL)]
    scratch = [pltpu.VMEM((tb, D_MODEL), BF16), pltpu.VMEM((CHUNK, 2 * KV_WIDTH), BF16)]
    args = [x, g_pre, w_in_t, ln_g, ln_b, wcat, bs, sinks, bias, w_out, g_post]
    if gather is not None:
        wt_shards, wo_shards, _ = gather
        in_specs += [ANY, ANY]
        args += [wt_shards, wo_shards]
        out_shape += [jax.ShapeDtypeStruct((N_CHIPS * wt_shards.shape[1], D_MODEL), BF16),
                      jax.ShapeDtypeStruct((N_CHIPS * wo_shards.shape[1], D_MODEL), BF16)]
        out_specs += [ANY, ANY]
        scratch += _comm_scratch(N_GATHER_SEMS)
    if target is not None:
        in_specs.append(tile(D_MODEL))
        args.append(target)
        out_shape.append(jax.ShapeDtypeStruct((1, 1), F32))
        out_specs.append(SMEM)
    name = "fwd_layer" + ("" if gather is None else "_gather") + ("" if target is None else "_loss")
    return pl.pallas_call(
        body, name=name, grid=(n_tiles,),
        out_shape=tuple(out_shape), in_specs=in_specs, out_specs=tuple(out_specs), scratch_shapes=scratch,
        compiler_params=_cparams(1),
    )(*args)


def _bwd_mix(dout, y, a, q, kv, bz, probs, stats, mixed, act, ln_g, ln_b, wtcat, w_out, g_post, exchange=None,
             allgather=None):
    n_tok = y.shape[0]
    tb = TOK_TILE
    n_chunks = tb // CHUNK
    n_tiles = n_tok // tb

    def body(*refs):
        refs = list(refs)
        take = lambda n: [refs.pop(0) for _ in range(n)]
        (dout_ref, y_ref, az_ref, q_ref, kv_ref, halo_ref, bz_ref, probs_ref, stats_ref, mixed_ref, act_ref, lng_ref,
         lnb_ref, wtcat_ref, wout_ref, gpost_ref) = take(16)
        if exchange is not None:
            dwt_in, dwo_in = take(2)
        if allgather is not None:
            (small_in,) = take(1)
        dproj_ref, dwout_ref, dws_ref, dmix_ref, dlng_ref, dlnb_ref, dgpost_ref, dbias_ref = take(8)
        if exchange is not None:
            pt_ref, po_ref = take(2)
        if allgather is not None:
            (small_all,) = take(1)
        ycat_s, dy_s, dyc_s, dkv_s, carry_s, dwout_s, stage_s, dws_s = take(8)
        starts, ends = [], []
        if exchange is not None:
            hosted = _hosted(_exchange_phases((dwt_in, dwo_in), (pt_ref, po_ref), *take(3)), pl.program_id(0), n_tiles)
            starts.append(hosted[0])
            ends.append(hosted[1])
        if allgather is not None:
            hosted = _hosted(_exchange_phases((small_in,), (small_all,), *take(3), by_chip=False), pl.program_id(0),
                             n_tiles)
            starts.append(hosted[0])
            ends.append(hosted[1])
        at_start = lambda: [f() for f in starts]
        at_end = lambda: [f() for f in ends]
        at_start()
        step = pl.program_id(0)
        lo = _lo_mask((CHUNK, CHUNK))
        prev = _ahead() > 0

        @pl.when(step == 0)
        def _():
            dwout_s[...] = jnp.zeros_like(dwout_s)
            dws_s[...] = jnp.zeros_like(dws_s)
            dmix_ref[...] = jnp.zeros_like(dmix_ref)
            dlng_ref[...] = jnp.zeros_like(dlng_ref)
            dlnb_ref[...] = jnp.zeros_like(dlnb_ref)
            dgpost_ref[...] = jnp.zeros_like(dgpost_ref)
            dbias_ref[...] = jnp.zeros_like(dbias_ref)
            carry_s[...] = jnp.zeros_like(carry_s)

        for r0 in range(0, tb, CHUNK):
            rows = slice(r0, r0 + CHUNK)
            yv = y_ref[rows, :]
            dout = dout_ref[rows, :]
            r = _over_lanes(stats_ref[rows, STAT_POST_RMS], D_MODEL)
            yn = yv * r
            dgpost_ref[...] += jnp.sum(dout * yn, axis=0, keepdims=True)
            dyn = dout * gpost_ref[...]
            dy_s[rows, :] = (r * (dyn - yn * jnp.mean(dyn * yn, axis=-1, keepdims=True))).astype(BF16)
        dyc_s[...] = _dot_nt(dy_s[...], wout_ref[...])
        dkv_s[0:tb, :] = jnp.zeros((tb, 2 * KV_WIDTH), F32)
        dkv_s[tb:tb + CHUNK, :] = carry_s[...]

        def chunk(c, carry):
            r0 = pl.multiple_of(c * CHUNK, CHUNK)
            rows = pl.ds(r0, CHUNK)
            u, gu, vv, gv = (act_ref[rows, k * A_WIDTH:(k + 1) * A_WIDTH] for k in range(4))
            rs = _over_lanes(stats_ref[rows, STAT_LN_RSTD], A_WIDTH)
            xhat = (vv - _over_lanes(stats_ref[rows, STAT_LN_MEAN], A_WIDTH)) * rs
            vnb = (xhat * lng_ref[...] + lnb_ref[...]).astype(BF16)
            d_vn, d_u, d_az = [], [], []
            for p in range(4):
                blk = slice(p * CHUNK, (p + 1) * CHUNK)
                mixed = mixed_ref[rows, blk]
                sz, gz = _silu_and_grad(az_ref[rows, blk])
                ub = u[:, blk]
                dya = dyc_s[rows, blk]
                um = ub * mixed
                ycat_s[rows, blk] = (um * sz).astype(BF16)
                d_mixed = (dya * ub) * sz
                d_u.append((dya * mixed) * sz)
                d_az.append((dya * um) * gz)
                dmix_ref[:, blk] += d_mixed
                dmbd = _blockdiag(d_mixed.astype(BF16), lo)
                d_vn.append(_dot(wtcat_ref[p], dmbd))
                dws_s[p] += _dot_nt(dmbd, vnb[:, blk])
            d_vn = jnp.concatenate(d_vn, axis=1)
            dlng_ref[...] += jnp.sum(d_vn * xhat, axis=0, keepdims=True)
            dlnb_ref[...] += jnp.sum(d_vn, axis=0, keepdims=True)
            dxh = d_vn * lng_ref[...]
            d_vv = rs * (dxh - jnp.mean(dxh, axis=-1, keepdims=True)
                         - xhat * jnp.mean(dxh * xhat, axis=-1, keepdims=True))
            dproj_ref[rows, 0:A_WIDTH] = (jnp.concatenate(d_u, axis=1) * gu).astype(BF16)
            dproj_ref[rows, A_WIDTH:2 * A_WIDTH] = (d_vv * gv).astype(BF16)
            dproj_ref[rows, 2 * A_WIDTH:Q_OFF] = jnp.concatenate(d_az, axis=1).astype(BF16)
            ks, vs = _kv_rows(kv_ref, halo_ref, r0, c)
            q_stacks = _query_stacks([q_ref[rows, p * CHUNK:(p + 1) * CHUNK] for p in range(4)], lo)
            probs = {(p, half): probs_ref[rows, (2 * p + half) * CHUNK:(2 * p + half + 1) * CHUNK]
                     for p in range(4) for half in range(2)}
            outs_h, p_stacks = _attention_values(probs, vs, prev)
            dom = {}
            for p in range(4):
                blk = slice(p * CHUNK, (p + 1) * CHUNK)
                sb, gb = _silu_and_grad(bz_ref[rows, blk])
                dyb = dyc_s[rows, B_WIDTH + p * CHUNK:B_WIDTH + (p + 1) * CHUNK]
                o = jnp.where(lo, outs_h[p, 0], outs_h[p, 1])
                ycat_s[rows, B_WIDTH + p * CHUNK:B_WIDTH + (p + 1) * CHUNK] = (o * sb).astype(BF16)
                dproj_ref[rows, BZ_OFF + p * CHUNK:BZ_OFF + (p + 1) * CHUNK] = ((dyb * o) * gb).astype(BF16)
                dom[p] = _masked_halves((dyb * sb).astype(BF16), lo)
            dqs, dk_by_copy, dv_by_copy = {}, [], []
            for sw, members in enumerate(_HEADS_BY_COPY):
                do_stack = jnp.concatenate([dom[p][half] for p, half in members], axis=0)
                dp_stack = _dot_nt(do_stack, vs[sw])
                dsws = []
                for i, (p, half) in enumerate(members):
                    head = 2 * p + half
                    dp = _window_square(dp_stack[i * CHUNK:(i + 1) * CHUNK, :], prev)
                    prob = probs[p, half].astype(F32)
                    delta = jnp.sum(prob * dp, axis=-1, keepdims=True)
                    ds = prob * (dp - delta)
                    dbias_ref[head] += ds
                    dsws.append(_window_keys(ds.astype(BF16), prev))
                ds_stack = jnp.concatenate(dsws, axis=0)
                dq_stack = _dot(ds_stack, ks[sw])
                for i, m in enumerate(members):
                    dqs[m] = dq_stack[i * CHUNK:(i + 1) * CHUNK, :]
                dk_by_copy.append(_dot_tn(ds_stack, q_stacks[sw]))
                dv_by_copy.append(_dot_tn(p_stacks[sw], do_stack))
            for p in range(4):
                dq = jnp.where(lo, dqs[p, 0], dqs[p, 1]) * QK_SCALE
                dproj_ref[rows, Q_OFF + p * CHUNK:Q_OFF + (p + 1) * CHUNK] = dq.astype(BF16)
            both = pl.ds(r0, 2 * CHUNK)
            dkv_s[both, 0:KV_WIDTH] += dk_by_copy[0] + _swap_halves(dk_by_copy[1])
            dkv_s[both, KV_WIDTH:2 * KV_WIDTH] += dv_by_copy[0] + _swap_halves(dv_by_copy[1])
            return carry

        lax.fori_loop(0, n_chunks, chunk, 0, unroll=BWD_CHUNK_UNROLL)
        dwout_s[...] += _dot_tn(ycat_s[...], dy_s[...])
        dproj_ref[:, K_OFF:BZ_OFF] = dkv_s[CHUNK:CHUNK + tb, :].astype(BF16)
        carry_s[...] = dkv_s[0:CHUNK, :]

        @pl.when(step == n_tiles - 1)
        def _():
            row = lax.broadcasted_iota(jnp.int32, (2 * CHUNK, CHUNK), 0)
            col = lax.broadcasted_iota(jnp.int32, (2 * CHUNK, CHUNK), 1)
            causal = col <= jnp.where(row >= CHUNK, row - CHUNK, row)
            for p in range(4):
                dws_ref[p] = jnp.where(causal, dws_s[p], 0.0).astype(BF16)
            rows = stage_s.shape[0]
            for r0 in range(0, D_MODEL, rows):
                stage_s[...] = dwout_s[r0:r0 + rows, :].astype(BF16)
                pltpu.sync_copy(stage_s, dwout_ref.at[r0:r0 + rows, :])

        at_end()

    tile = lambda w: pl.BlockSpec((tb, w), lambda s: (n_tiles - 1 - s, 0))
    whole = lambda shape, **kw: pl.BlockSpec(shape, lambda s: (0,) * len(shape), **kw)
    once = dict(pipeline_mode=pl.Buffered(1))
    gate_cols = pl.BlockSpec((tb, A_WIDTH), lambda s: (n_tiles - 1 - s, 2))
    in_specs = [tile(D_MODEL), tile(D_MODEL), gate_cols, tile(B_WIDTH), tile(2 * KV_WIDTH),
                pl.BlockSpec((CHUNK, 2 * KV_WIDTH), lambda s: (jnp.maximum((n_tiles - 1 - s) * n_chunks - 1, 0), 0)),
                tile(B_WIDTH), tile(HEADS * CHUNK), tile(N_STATS * CHUNK), tile(A_WIDTH), tile(4 * A_WIDTH),
                whole((1, A_WIDTH)), whole((1, A_WIDTH)), whole((4, CHUNK, 2 * CHUNK)), whole((D_MODEL, D_MODEL), **once),
                whole((1, D_MODEL))]
    args = [dout, y, a, q, kv, kv, bz, probs, stats, mixed, act, ln_g, ln_b, wtcat, w_out, g_post]
    small_shapes = [jax.ShapeDtypeStruct((D_MODEL, D_MODEL), BF16), jax.ShapeDtypeStruct((4, 2 * CHUNK, CHUNK), BF16),
                    jax.ShapeDtypeStruct((CHUNK, A_WIDTH), F32), jax.ShapeDtypeStruct((1, A_WIDTH), F32),
                    jax.ShapeDtypeStruct((1, A_WIDTH), F32), jax.ShapeDtypeStruct((1, D_MODEL), F32),
                    jax.ShapeDtypeStruct((HEADS, CHUNK, CHUNK), F32)]
    small_specs = [ANY, whole((4, 2 * CHUNK, CHUNK)), whole((CHUNK, A_WIDTH)), whole((1, A_WIDTH)), whole((1, A_WIDTH)),
                   whole((1, D_MODEL)), whole((HEADS, CHUNK, CHUNK))]
    scratch = [pltpu.VMEM((tb, D_MODEL), BF16), pltpu.VMEM((tb, D_MODEL), BF16), pltpu.VMEM((tb, D_MODEL), F32),
               pltpu.VMEM((tb + CHUNK, 2 * KV_WIDTH), F32), pltpu.VMEM((CHUNK, 2 * KV_WIDTH), F32),
               pltpu.VMEM((D_MODEL, D_MODEL), F32), pltpu.VMEM((2 * CHUNK, D_MODEL), BF16),
               pltpu.VMEM((4, 2 * CHUNK, CHUNK), F32)]
    out_shape = [jax.ShapeDtypeStruct((n_tok, IN_WIDTH), BF16)] + small_shapes
    out_specs = [tile(IN_WIDTH)] + small_specs
    if exchange is not None:
        dwt, dwo = exchange
        in_specs += [ANY, ANY]
        args += [dwt, dwo]
        out_shape += [jax.ShapeDtypeStruct((N_DEV, dwt.shape[0] // N_CHIPS, D_MODEL), BF16),
                      jax.ShapeDtypeStruct((N_DEV, dwo.shape[0] // N_CHIPS, D_MODEL), BF16)]
        out_specs += [ANY, ANY]
    if allgather is not None:
        in_specs.append(ANY)
        args.append(allgather)
        out_shape.append(jax.ShapeDtypeStruct((N_DEV,) + allgather.shape, allgather.dtype))
        out_specs.append(ANY)
    for hosted in (exchange, allgather):
        if hosted is not None:
            scratch += _comm_scratch(N_EXCHANGE_SEMS)
    return pl.pallas_call(
        body, name="bwd_mix" + ("" if exchange is None and allgather is None else "_exchange"), grid=(n_tiles,),
        out_shape=tuple(out_shape), in_specs=in_specs, out_specs=tuple(out_specs), scratch_shapes=scratch,
        compiler_params=_cparams(1),
    )(*args)


def _bwd_in(dproj, x, dout, g_pre, w_in_t):
    n_tok = x.shape[0]
    tm = TOK_TILE
    n_tiles = n_tok // tm

    def body(dp_ref, x_ref, dout_ref, g_ref, wt_ref, dx_ref, dwt_ref, dg_ref, acc_s):
        i = pl.program_id(0)

        @pl.when(i == 0)
        def _():
            acc_s[...] = jnp.zeros_like(acc_s)
            dg_ref[...] = jnp.zeros_like(dg_ref)

        xf = x_ref[...]
        r = lax.rsqrt(jnp.mean(xf * xf, axis=-1, keepdims=True) + NORM_EPS)
        xn = xf * r
        h = (xn * g_ref[...]).astype(BF16)
        dp = dp_ref[...]
        dh = _dot(dp, wt_ref[...])
        acc_s[...] += _dot_tn(dp, h)
        dg_ref[...] += jnp.sum(dh * xn, axis=0, keepdims=True)
        dhn = dh * g_ref[...]
        dx_ref[...] = dout_ref[...] + r * (dhn - xn * jnp.mean(dhn * xn, axis=-1, keepdims=True))

        @pl.when(i == n_tiles - 1)
        def _():
            dwt_ref[...] = acc_s[...].astype(BF16)

    return pl.pallas_call(
        body, name="bwd_in", grid=(n_tiles,),
        out_shape=(jax.ShapeDtypeStruct((n_tok, D_MODEL), F32), jax.ShapeDtypeStruct((IN_WIDTH, D_MODEL), BF16),
                   jax.ShapeDtypeStruct((1, D_MODEL), F32)),
        in_specs=[pl.BlockSpec((tm, IN_WIDTH), lambda i: (i, 0)), pl.BlockSpec((tm, D_MODEL), lambda i: (i, 0)),
                  pl.BlockSpec((tm, D_MODEL), lambda i: (i, 0)), pl.BlockSpec((1, D_MODEL), lambda i: (0, 0)),
                  pl.BlockSpec((IN_WIDTH, D_MODEL), lambda i: (0, 0), pipeline_mode=pl.Buffered(1))],
        out_specs=(pl.BlockSpec((tm, D_MODEL), lambda i: (i, 0)),
                   pl.BlockSpec((IN_WIDTH, D_MODEL), lambda i: (0, 0), pipeline_mode=pl.Buffered(1)),
                   pl.BlockSpec((1, D_MODEL), lambda i: (0, 0))),
        scratch_shapes=[pltpu.VMEM((IN_WIDTH, D_MODEL), F32)],
        compiler_params=_cparams(1),
    )(dproj, x, dout, g_pre, w_in_t)


def _riders(exchanges):
    args = [a for arrays, _ in exchanges for a in arrays]
    shapes = [jax.ShapeDtypeStruct((N_DEV, a.shape[0] // N_CHIPS if by_chip else a.shape[0]) + a.shape[1:], a.dtype)
              for arrays, by_chip in exchanges for a in arrays]
    scratch = [s for _ in exchanges for s in _comm_scratch(N_EXCHANGE_SEMS)]

    def bind(in_refs, out_refs, sem_refs, step, n_steps, pass_on_step=None):
        in_refs, out_refs, sem_refs = list(in_refs), list(out_refs), list(sem_refs)
        starts, ends = [], []
        for arrays, by_chip in exchanges:
            n = len(arrays)
            phases = _exchange_phases(tuple(in_refs[:n]), tuple(out_refs[:n]), *sem_refs[:3], by_chip=by_chip)
            del in_refs[:n], out_refs[:n], sem_refs[:3]
            at_start, at_end = _hosted(phases, step, n_steps, pass_on_step)
            starts.append(at_start)
            ends.append(at_end)
        return (lambda: [f() for f in starts]), (lambda: [f() for f in ends])

    return args, [ANY] * len(args), shapes, [ANY] * len(shapes), scratch, bind


def _bwd_in_dw(dproj, x, g_pre, exchanges):
    n_tok = x.shape[0]
    tm = TOK_TILE
    n_tiles = n_tok // tm
    r_args, r_in_specs, r_shapes, r_out_specs, r_scratch, bind = _riders(exchanges)

    def body(*refs):
        dp_ref, x_ref, g_ref = refs[:3]
        in_refs = refs[3:3 + len(r_args)]
        dwt_ref = refs[3 + len(r_args)]
        out_refs = refs[4 + len(r_args):4 + 2 * len(r_args)]
        acc_s = refs[4 + 2 * len(r_args)]
        i = pl.program_id(0)
        at_start, at_end = bind(in_refs, out_refs, refs[5 + 2 * len(r_args):], i, n_tiles)
        at_start()

        @pl.when(i == 0)
        def _():
            acc_s[...] = jnp.zeros_like(acc_s)

        xf = x_ref[...]
        r = lax.rsqrt(jnp.mean(xf * xf, axis=-1, keepdims=True) + NORM_EPS)
        acc_s[...] += _dot_tn(dp_ref[...], ((xf * r) * g_ref[...]).astype(BF16))

        @pl.when(i == n_tiles - 1)
        def _():
            dwt_ref[...] = acc_s[...].astype(BF16)

        at_end()

    return pl.pallas_call(
        body, name="bwd_in_dw", grid=(n_tiles,),
        out_shape=(jax.ShapeDtypeStruct((IN_WIDTH, D_MODEL), BF16), *r_shapes),
        in_specs=[pl.BlockSpec((tm, IN_WIDTH), lambda i: (i, 0)), pl.BlockSpec((tm, D_MODEL), lambda i: (i, 0)),
                  pl.BlockSpec((1, D_MODEL), lambda i: (0, 0))] + r_in_specs,
        out_specs=(pl.BlockSpec((IN_WIDTH, D_MODEL), lambda i: (0, 0), pipeline_mode=pl.Buffered(1)), *r_out_specs),
        scratch_shapes=[pltpu.VMEM((IN_WIDTH, D_MODEL), F32)] + r_scratch,
        compiler_params=_cparams(1),
    )(dproj, x, g_pre, *r_args)


def _bwd_in_dx(dproj, x, dout, g_pre, w_in_t, exchanges):
    n_tok = x.shape[0]
    tm = TOK_TILE
    n_tiles = n_tok // tm
    r_args, r_in_specs, r_shapes, r_out_specs, r_scratch, bind = _riders(exchanges)

    def body(*refs):
        dp_ref, x_ref, dout_ref, g_ref, wt_ref = refs[:5]
        in_refs = refs[5:5 + len(r_args)]
        dx_ref, dg_ref = refs[5 + len(r_args):7 + len(r_args)]
        out_refs = refs[7 + len(r_args):7 + 2 * len(r_args)]
        i = pl.program_id(0)
        at_start, at_end = bind(in_refs, out_refs, refs[7 + 2 * len(r_args):], i, n_tiles, pass_on_step=n_tiles - 2)
        at_start()

        @pl.when(i == 0)
        def _():
            dg_ref[...] = jnp.zeros_like(dg_ref)

        xf = x_ref[...]
        r = lax.rsqrt(jnp.mean(xf * xf, axis=-1, keepdims=True) + NORM_EPS)
        xn = xf * r
        dh = _dot(dp_ref[...], wt_ref[...])
        dg_ref[...] += jnp.sum(dh * xn, axis=0, keepdims=True)
        dhn = dh * g_ref[...]
        dx_ref[...] = dout_ref[...] + r * (dhn - xn * jnp.mean(dhn * xn, axis=-1, keepdims=True))
        at_end()

    tile = lambda w: pl.BlockSpec((tm, w), lambda i: (i, 0))
    return pl.pallas_call(
        body, name="bwd_in_dx", grid=(n_tiles,),
        out_shape=(jax.ShapeDtypeStruct((n_tok, D_MODEL), F32), jax.ShapeDtypeStruct((1, D_MODEL), F32), *r_shapes),
        in_specs=[tile(IN_WIDTH), tile(D_MODEL), tile(D_MODEL), pl.BlockSpec((1, D_MODEL), lambda i: (0, 0)),
                  pl.BlockSpec((IN_WIDTH, D_MODEL), lambda i: (0, 0), pipeline_mode=pl.Buffered(1))] + r_in_specs,
        out_specs=(tile(D_MODEL), pl.BlockSpec((1, D_MODEL), lambda i: (0, 0)), *r_out_specs),
        scratch_shapes=r_scratch,
        compiler_params=_cparams(1),
    )(dproj, x, dout, g_pre, w_in_t, *r_args)


def _adam_update(w, g, m, v):
    nm = ADAM_B1 * m + (1.0 - ADAM_B1) * g
    nv = ADAM_B2 * v + (1.0 - ADAM_B2) * (g * g)
    m_hat = nm / (1.0 - ADAM_B1 ** ADAM_STEP)
    v_hat = nv / (1.0 - ADAM_B2 ** ADAM_STEP)
    return -ADAM_LR * (m_hat / (jnp.sqrt(v_hat) + ADAM_EPS) + ADAM_WD * w), nm, nv


def _sum_slots(p_ref):
    tot = p_ref[0].astype(F32)
    for d in range(1, N_DEV):
        tot = tot + p_ref[d].astype(F32)
    return tot


def _adamw_parts(parts, w, m, v, name):
    n_layers, n_rows, n_cols = w.shape
    tr = n_rows
    while tr * n_cols > ADAMW_BLOCK_ELEMS:
        tr //= 2
    n_blocks = n_rows // tr

    def body(*refs):
        p_refs = refs[:n_layers]
        w_ref, m_ref, v_ref, g_ref, d_ref, nm_ref, nv_ref = refs[n_layers:]
        for k in range(n_layers):
            @pl.when(pl.program_id(0) == k)
            def _(k=k):
                g = _sum_slots(p_refs[k])
                g_ref[0] = g
                d_ref[0], nm_ref[0], nv_ref[0] = _adam_update(w_ref[0], g, m_ref[0], v_ref[0])

    def part_spec(k):
        return pl.BlockSpec((N_DEV, tr, n_cols), lambda l, i: (0, jnp.clip(i + (l - k) * n_blocks, 0, n_blocks - 1), 0))

    spec = pl.BlockSpec((1, tr, n_cols), lambda l, i: (l, i, 0))
    shape = jax.ShapeDtypeStruct((n_layers, n_rows, n_cols), F32)
    return pl.pallas_call(
        body, name="adamw_" + name, grid=(n_layers, n_blocks), out_shape=(shape,) * 4,
        in_specs=[part_spec(k) for k in range(n_layers)] + [spec, spec, spec], out_specs=(spec,) * 4,
        compiler_params=_cparams(2),
    )(*parts, w, m, v)


def _adamw_small(quads):
    n = len(quads)

    def body(*refs):
        ins, outs = refs[:4 * n], refs[4 * n:]
        for j in range(n):
            w_ref, g_ref, m_ref, v_ref = ins[4 * j:4 * j + 4]
            d_ref, nm_ref, nv_ref = outs[3 * j:3 * j + 3]
            d_ref[...], nm_ref[...], nv_ref[...] = _adam_update(w_ref[...], g_ref[...], m_ref[...], v_ref[...])

    vmem = pl.BlockSpec(memory_space=pltpu.VMEM)
    shapes = tuple(jax.ShapeDtypeStruct(w.shape, F32) for w, *_ in quads for _ in range(3))
    results = pl.pallas_call(
        body, name="adamw_small", out_shape=shapes, in_specs=[vmem] * (4 * n), out_specs=(vmem,) * (3 * n),
    )(*[a for quad in quads for a in quad])
    return [results[3 * j:3 * j + 3] for j in range(n)]


_SMALL = ("pre_norm_g", "ln_v_g", "ln_v_b", "b_spatial", "sinks", "rel_bias", "post_norm_g")


def _pack(pieces):
    blocks, first_rows, n = [], [], 0
    for p in pieces:
        flat = p.reshape(-1)
        flat = jnp.concatenate([flat, jnp.zeros(((-flat.shape[0]) % (8 * 128),), F32)]).reshape(-1, 128)
        blocks.append(flat)
        first_rows.append(n)
        n += flat.shape[0]
    return jnp.concatenate(blocks, axis=0), first_rows


def _unpack(block, first_row, shape):
    size = math.prod(shape)
    return block[first_row:first_row + -(-size // 128)].reshape(-1)[:size].reshape(shape)


def kernel(x, pre_norm_g, w_in, ln_v_g, ln_v_b, w_spatial, b_spatial, sinks, rel_bias, w_out, post_norm_g, loss_target, m_pre_norm_g, m_w_in, m_ln_v_g, m_ln_v_b, m_w_spatial, m_b_spatial, m_sinks, m_rel_bias, m_w_out, m_post_norm_g, v_pre_norm_g, v_w_in, v_ln_v_g, v_ln_v_b, v_w_spatial, v_b_spatial, v_sinks, v_rel_bias, v_w_out, v_post_norm_g):
    weights = dict(pre_norm_g=pre_norm_g, w_in=w_in, ln_v_g=ln_v_g, ln_v_b=ln_v_b, w_spatial=w_spatial, b_spatial=b_spatial,
                   sinks=sinks, rel_bias=rel_bias, w_out=w_out, post_norm_g=post_norm_g)
    mom_m = dict(pre_norm_g=m_pre_norm_g, w_in=m_w_in, ln_v_g=m_ln_v_g, ln_v_b=m_ln_v_b, w_spatial=m_w_spatial,
                 b_spatial=m_b_spatial, sinks=m_sinks, rel_bias=m_rel_bias, w_out=m_w_out, post_norm_g=m_post_norm_g)
    mom_v = dict(pre_norm_g=v_pre_norm_g, w_in=v_w_in, ln_v_g=v_ln_v_g, ln_v_b=v_ln_v_b, w_spatial=v_w_spatial,
                 b_spatial=v_b_spatial, sinks=v_sinks, rel_bias=v_rel_bias, w_out=v_w_out, post_norm_g=v_post_norm_g)
    n_seq, seq_len, _ = x.shape
    n_layers = w_in.shape[0]
    x2 = x.reshape(n_seq * seq_len, D_MODEL)
    target2 = loss_target.reshape(n_seq * seq_len, D_MODEL)
    row = lambda p, l: p[l][None]

    wt_shards = jnp.swapaxes(w_in, 1, 2).astype(BF16)
    wo_shards = w_out.astype(BF16)
    bucket = _window_buckets()
    bias = _bias_table(rel_bias, bucket)
    wcat, wtcat, bs = _spatial_tables(w_spatial, jnp.swapaxes(b_spatial, 1, 2))

    wt, wo = [None] * n_layers, [None] * n_layers
    wt[0], wo[0] = _gather_weights(wt_shards, wo_shards, 0)
    xs, saved = [x2], []
    for l in range(n_layers):
        layer_args = (xs[-1], row(pre_norm_g, l), wt[l], row(ln_v_g, l), row(ln_v_b, l), wcat[l], bs[l], sinks[l], bias,
                      wo[l], row(post_norm_g, l), seq_len)
        if l + 1 < n_layers:
            *acts, xn, y, wt[l + 1], wo[l + 1] = _fwd_layer(*layer_args, gather=(wt_shards, wo_shards, l + 1))
            xs.append(xn)
        else:
            *acts, dx, y, loss = _fwd_layer(*layer_args, target=target2)
        saved.append((y, *acts))

    small = [None] * n_layers
    parts = [None] * n_layers
    waiting = None
    for l in reversed(range(n_layers)):
        hosted = {} if waiting is None else dict(exchange=waiting[:2], allgather=waiting[2])
        dproj, *rest = _bwd_mix(dx, *saved[l], row(ln_v_g, l), row(ln_v_b, l), wtcat[l], wo[l], row(post_norm_g, l), **hosted)
        dwo, dws, dmix, dlng, dlnb, dgpost, dbias = rest[:7]
        dws = dws.reshape(HEADS * CHUNK, CHUNK)
        if waiting is not None:
            parts[l + 1] = list(rest[7:])
        if l > 0:
            dx, dwt, dgpre = _bwd_in(dproj, xs[l], dx, row(pre_norm_g, l), wt[l])
            waiting = (dwt, dwo, dws)
        else:
            dwt, po, ws_all = _bwd_in_dw(dproj, xs[l], row(pre_norm_g, l), [((dwo,), True), ((dws,), False)])
            dx, dgpre, pt = _bwd_in_dx(dproj, xs[l], dx, row(pre_norm_g, l), wt[l], [((dwt,), True)])
            parts[l] = [pt, po, ws_all]
        small[l] = dict(pre_norm_g=dgpre[0], ln_v_g=dlng[0], ln_v_b=dlnb[0], dmix=dmix, dbias=dbias,
                        post_norm_g=dgpost[0])

    db = _b_spatial_grad([s["dmix"] for s in small])
    pieces = {n: jnp.stack([s[n] for s in small]) for n in ("pre_norm_g", "ln_v_g", "ln_v_b", "post_norm_g")}
    pieces["b_spatial"] = jnp.swapaxes(db[:, :, :HEADS], 1, 2)
    pieces["rel_bias"], pieces["sinks"] = _rel_bias_grad([s["dbias"] for s in small], bucket)
    packed, first_rows = _pack([pieces[n] for n in _SMALL] + [loss])
    total = _allreduce_small(packed)
    grad = {n: _unpack(total, r, weights[n].shape) for n, r in zip(_SMALL, first_rows)}
    loss_out = total[first_rows[-1], 0]

    delta, new_m, new_v = {}, {}, {}
    two_d = lambda p: p.reshape(-1, p.shape[-1])
    updates = _adamw_small([tuple(two_d(p[n]) for p in (weights, grad, mom_m, mom_v)) for n in _SMALL])
    for n, update in zip(_SMALL, updates):
        delta[n], new_m[n], new_v[n] = (u.reshape(weights[n].shape) for u in update)

    t3 = lambda p: jnp.swapaxes(p, 1, 2)
    flat3 = lambda p: p.reshape(n_layers, HEADS * CHUNK, CHUNK)
    for n, which, view, back in (("w_in", 0, t3, t3), ("w_out", 1, lambda p: p, lambda p: p),
                                 ("w_spatial", 2, flat3, lambda p: p.reshape(w_spatial.shape))):
        results = _adamw_parts([parts[l][which] for l in range(n_layers)], view(weights[n]), view(mom_m[n]),
                               view(mom_v[n]), n)
        grad[n], delta[n], new_m[n], new_v[n] = (back(r) for r in results)

    names = tuple(weights)
    return (loss_out, dx.reshape(x.shape), *[grad[n] for n in names], *[delta[n] for n in names],
            *[new_m[n] for n in names], *[new_v[n] for n in names])
```

```python
import math

import jax
import jax.numpy as jnp
from jax import lax
from jax.experimental import pallas as pl
from jax.experimental.pallas import tpu as pltpu

F32 = jnp.float32
BF16 = jnp.bfloat16

D_MODEL = 1024
A_WIDTH = 512
B_WIDTH = 512
KV_WIDTH = 128
IN_WIDTH = 3 * A_WIDTH + 2 * B_WIDTH + 2 * KV_WIDTH
CHUNK = 128
HEADS = 8
HEAD_DIM = 64
REL_BUCKETS = 32
NORM_EPS = 1e-6
NEG = -1e30
Q_OFF = 3 * A_WIDTH
K_OFF = Q_OFF + B_WIDTH
BZ_OFF = K_OFF + 2 * KV_WIDTH
QK_SCALE = HEAD_DIM ** -0.5

ADAM_LR = 0.001
ADAM_B1 = 0.9
ADAM_B2 = 0.999
ADAM_EPS = 1e-08
ADAM_WD = 0.01
ADAM_STEP = 10

TOK_TILE = 512
CHUNK_UNROLL = 4
BWD_CHUNK_UNROLL = 4
ADAMW_BLOCK_ELEMS = 192 * 1024
VMEM_LIMIT_V7X = 60 * 1024 * 1024

N_DEV = 8
N_CHIPS = 4
MESH_ID = pl.DeviceIdType.MESH
ANY = pl.BlockSpec(memory_space=pl.ANY)
SMEM = pl.BlockSpec(memory_space=pltpu.SMEM)


def _cparams(n_axes):
    return pltpu.CompilerParams(dimension_semantics=("arbitrary",) * n_axes, vmem_limit_bytes=VMEM_LIMIT_V7X)


_GELU_C = math.sqrt(2.0 / math.pi)
_GELU_C3 = _GELU_C * 0.044715


def _gelu_and_grad(x):
    x2 = x * x
    t = jnp.tanh(x * (_GELU_C3 * x2 + _GELU_C))
    cdf = 0.5 * t + 0.5
    d = cdf + (x * (cdf * (1.0 - cdf))) * ((6.0 * _GELU_C3) * x2 + 2.0 * _GELU_C)
    return x * cdf, d


def _sigmoid(x):
    return 0.5 + 0.5 * jnp.tanh(0.5 * x)


def _silu(x):
    return x * _sigmoid(x)


def _silu_and_grad(x):
    s = _sigmoid(x)
    return x * s, s * (1.0 + x * (1.0 - s))


def _dot(a, b):
    return jnp.dot(a, b, preferred_element_type=F32)


def _dot_nt(a, b):
    return lax.dot_general(a, b, (((1,), (1,)), ((), ())), preferred_element_type=F32)


def _dot_tn(a, b):
    return lax.dot_general(a, b, (((0,), (0,)), ((), ())), preferred_element_type=F32)


def _lo_mask(shape):
    return lax.broadcasted_iota(jnp.int32, shape, 1) < HEAD_DIM


def _swap_halves(v):
    return pltpu.roll(v, HEAD_DIM, 1)


def _window_buckets():
    q_loc = jnp.arange(CHUNK)[:, None]
    j_loc = jnp.arange(CHUNK)[None, :]
    d = q_loc - j_loc + jnp.where(j_loc > q_loc, CHUNK, 0)
    max_exact = REL_BUCKETS // 2
    safe = jnp.maximum(d, 1).astype(F32)
    large = max_exact + (jnp.log(safe / max_exact) / math.log(CHUNK / max_exact)
                         * (REL_BUCKETS - max_exact)).astype(jnp.int32)
    large = jnp.minimum(large, REL_BUCKETS - 1)
    return jnp.where(d < max_exact, d, large).astype(jnp.int32)


def _bias_table(rel_bias, bucket):
    def body(rb_ref, bk_ref, out_ref):
        bk = bk_ref[...]
        for h in range(HEADS):
            acc = jnp.zeros(bk.shape, F32)
            for b in range(REL_BUCKETS):
                acc = jnp.where(bk == b, rb_ref[b, h], acc)
            out_ref[h] = acc

    vmem = pl.BlockSpec(memory_space=pltpu.VMEM)
    return pl.pallas_call(
        body, name="bias_table",
        out_shape=jax.ShapeDtypeStruct((HEADS, CHUNK, CHUNK), F32),
        in_specs=[SMEM, vmem], out_specs=vmem,
    )(rel_bias, bucket)


def _rel_bias_grad(dbias, bucket):
    n_layers = len(dbias)

    def body(*refs):
        db_refs, (bk_ref, out_ref, dsink_ref) = refs[:n_layers], refs[n_layers:]
        bk = bk_ref[...]
        for h in range(HEADS):
            tot = jnp.zeros((CHUNK, CHUNK), F32)
            for l in range(n_layers):
                ds = db_refs[l][h]
                dsink_ref[l, h] = -jnp.sum(ds)
                tot = tot + ds
            for b in range(REL_BUCKETS):
                out_ref[b, h] = jnp.sum(jnp.where(bk == b, tot, 0.0))

    vmem = pl.BlockSpec(memory_space=pltpu.VMEM)
    return pl.pallas_call(
        body, name="rel_bias_grad",
        out_shape=(jax.ShapeDtypeStruct((REL_BUCKETS, HEADS), F32), jax.ShapeDtypeStruct((n_layers, HEADS), F32)),
        in_specs=[vmem] * (n_layers + 1), out_specs=(SMEM, SMEM),
    )(*dbias, bucket)


def _spatial_tables(w_spatial, b_spatial_t):
    n_layers = w_spatial.shape[0]

    def body(w_ref, b_ref, wcat_ref, wtcat_ref, bs_ref):
        row = lax.broadcasted_iota(jnp.int32, (CHUNK, CHUNK), 0)
        col = lax.broadcasted_iota(jnp.int32, (CHUNK, CHUNK), 1)
        causal = col <= row
        lo = _lo_mask((CHUNK, CHUNK))
        for p in range(4):
            for half in range(2):
                w = jnp.where(causal, w_ref[0, 2 * p + half], 0.0)
                wcat_ref[0, p, :, half * CHUNK:(half + 1) * CHUNK] = w.astype(BF16)
                wtcat_ref[0, p, :, half * CHUNK:(half + 1) * CHUNK] = w.T.astype(BF16)
            b = b_ref[0]
            bs_ref[0, p] = jnp.where(lo, b[:, 2 * p:2 * p + 1], b[:, 2 * p + 1:2 * p + 2])

    return pl.pallas_call(
        body, name="spatial_tables", grid=(n_layers,),
        out_shape=(jax.ShapeDtypeStruct((n_layers, 4, CHUNK, 2 * CHUNK), BF16),
                   jax.ShapeDtypeStruct((n_layers, 4, CHUNK, 2 * CHUNK), BF16),
                   jax.ShapeDtypeStruct((n_layers, 4, CHUNK, CHUNK), F32)),
        in_specs=[pl.BlockSpec((1, HEADS, CHUNK, CHUNK), lambda l: (l, 0, 0, 0)),
                  pl.BlockSpec((1, CHUNK, HEADS), lambda l: (l, 0, 0))],
        out_specs=(pl.BlockSpec((1, 4, CHUNK, 2 * CHUNK), lambda l: (l, 0, 0, 0)),
                   pl.BlockSpec((1, 4, CHUNK, 2 * CHUNK), lambda l: (l, 0, 0, 0)),
                   pl.BlockSpec((1, 4, CHUNK, CHUNK), lambda l: (l, 0, 0, 0))),
        compiler_params=_cparams(1),
    )(w_spatial, b_spatial_t)


def _b_spatial_grad(dmix):
    n_layers = len(dmix)

    def body(*refs):
        d_refs, out_ref = refs[:n_layers], refs[n_layers]
        lane = lax.broadcasted_iota(jnp.int32, (CHUNK, CHUNK), 1)
        for l in range(n_layers):
            acc = jnp.zeros((CHUNK, CHUNK), F32)
            for p in range(4):
                t = d_refs[l][:, p * CHUNK:(p + 1) * CHUNK]
                s_lo = jnp.sum(jnp.where(lane < HEAD_DIM, t, 0.0), axis=1, keepdims=True)
                s_hi = jnp.sum(jnp.where(lane < HEAD_DIM, 0.0, t), axis=1, keepdims=True)
                acc = jnp.where(lane == 2 * p, s_lo, acc)
                acc = jnp.where(lane == 2 * p + 1, s_hi, acc)
            out_ref[l] = acc

    vmem = pl.BlockSpec(memory_space=pltpu.VMEM)
    return pl.pallas_call(
        body, name="b_spatial_grad",
        out_shape=jax.ShapeDtypeStruct((n_layers, CHUNK, CHUNK), F32),
        in_specs=[vmem] * n_layers, out_specs=vmem,
    )(*dmix)


def _place():
    x, y, c = lax.axis_index("x"), lax.axis_index("y"), lax.axis_index("c")
    other_chips = [(1 - x, y), (x, 1 - y), (1 - x, 1 - y)]
    return x, y, c, other_chips


N_GATHER_SEMS = 12


def _gather_phases(shards, fulls, send_sems, recv_sems, local_sems):
    x, y, c, chips = _place()
    sibling = (x, y, 1 - c)
    n_arr = len(shards)

    def half_rows(a, chip, half):
        n = shards[a].shape[0]
        start = (2 * chip[0] + chip[1]) * n + half * (n // 2)
        return fulls[a].at[pl.ds(pl.multiple_of(start, 16), n // 2), :]

    def my_half(a):
        n = shards[a].shape[0]
        return shards[a].at[pl.ds(pl.multiple_of(c * (n // 2), 16), n // 2), :]

    def copy(k, a, src, chip, half, to):
        return pltpu.make_async_remote_copy(
            src_ref=src, dst_ref=half_rows(a, chip, half), send_sem=send_sems.at[n_arr * k + a],
            recv_sem=recv_sems.at[n_arr * k + a], device_id=to, device_id_type=MESH_ID)

    def local(a):
        n = shards[a].shape[0]
        mine = fulls[a].at[pl.ds(pl.multiple_of((2 * x + y) * n, 16), n), :]
        return pltpu.make_async_copy(shards[a], mine, local_sems.at[a])

    def first(k, a):
        return copy(k, a, my_half(a), (x, y), c, (chips[k][0], chips[k][1], c))

    def passed(k, a):
        return copy(3 + k, a, half_rows(a, chips[k], c), chips[k], c, sibling)

    def phase_a():
        for a in range(n_arr):
            local(a).start()
        for k in range(3):
            for a in range(n_arr):
                first(k, a).start()

    def phase_b():
        for k in range(3):
            for a in range(n_arr):
                copy(k, a, my_half(a), chips[k], c, sibling).wait_recv()
                passed(k, a).start()

    def phase_c():
        for k in range(3):
            for a in range(n_arr):
                copy(3 + k, a, my_half(a), chips[k], 1 - c, sibling).wait_recv()
        for k in range(3):
            for a in range(n_arr):
                first(k, a).wait_send()
                passed(k, a).wait_send()
        for a in range(n_arr):
            local(a).wait()

    return phase_a, phase_b, phase_c


N_EXCHANGE_SEMS = 14


def _exchange_phases(partials, parts, send_sems, recv_sems, local_sems, by_chip=True):
    x, y, c, chips = _place()
    me, sibling = (x, y, c), (x, y, 1 - c)
    n_arr = len(partials)

    def block(a, chip):
        if not by_chip:
            return partials[a]
        n = partials[a].shape[0] // N_CHIPS
        return partials[a].at[pl.ds(pl.multiple_of((2 * chip[0] + chip[1]) * n, 16), n), :]

    def slot(a, dev):
        return parts[a].at[4 * dev[0] + 2 * dev[1] + dev[2]]

    def copy(k, a, src, origin, to):
        return pltpu.make_async_remote_copy(
            src_ref=src, dst_ref=slot(a, origin), send_sem=send_sems.at[n_arr * k + a],
            recv_sem=recv_sems.at[n_arr * k + a], device_id=to, device_id_type=MESH_ID)

    def local(a):
        return pltpu.make_async_copy(block(a, (x, y)), slot(a, me), local_sems.at[a])

    def first(k, a):
        if k == 0:
            return copy(0, a, block(a, (x, y)), me, sibling)
        chip = chips[k - 1]
        return copy(k, a, block(a, chip), me, (chip[0], chip[1], c))

    def passed(k, a):
        origin = (chips[k][0], chips[k][1], c)
        return copy(4 + k, a, slot(a, origin), origin, sibling)

    def phase_a():
        for a in range(n_arr):
            local(a).start()
        for k in range(4):
            for a in range(n_arr):
                first(k, a).start()

    def phase_b():
        for k in range(3):
            for a in range(n_arr):
                copy(1 + k, a, block(a, (x, y)), (chips[k][0], chips[k][1], c), me).wait_recv()
                passed(k, a).start()

    def phase_c():
        for a in range(n_arr):
            copy(0, a, block(a, (x, y)), sibling, me).wait_recv()
        for k in range(3):
            for a in range(n_arr):
                copy(4 + k, a, block(a, (x, y)), (chips[k][0], chips[k][1], 1 - c), me).wait_recv()
        for k in range(4):
            for a in range(n_arr):
                first(k, a).wait_send()
        for k in range(3):
            for a in range(n_arr):
                passed(k, a).wait_send()
        for a in range(n_arr):
            local(a).wait()

    return phase_a, phase_b, phase_c


def _comm_scratch(n_sems):
    return [pltpu.SemaphoreType.DMA((n_sems,)), pltpu.SemaphoreType.DMA((n_sems,)), pltpu.SemaphoreType.DMA((2,))]


def _gather_weights(wt_shards, wo_shards, layer):
    wt_rows, wo_rows = wt_shards.shape[1], wo_shards.shape[1]

    def body(wt_ref, wo_ref, wt_full, wo_full, send_sems, recv_sems, local_sems):
        phases = _gather_phases((wt_ref.at[layer], wo_ref.at[layer]), (wt_full, wo_full), send_sems, recv_sems, local_sems)
        for phase in phases:
            phase()

    return pl.pallas_call(
        body, name="gather_weights",
        out_shape=(jax.ShapeDtypeStruct((N_CHIPS * wt_rows, D_MODEL), BF16),
                   jax.ShapeDtypeStruct((N_CHIPS * wo_rows, D_MODEL), BF16)),
        in_specs=[ANY, ANY], out_specs=(ANY, ANY), scratch_shapes=_comm_scratch(N_GATHER_SEMS),
    )(wt_shards, wo_shards)


def _allreduce_small(part):
    n_rows = part.shape[0]

    def body(p_ref, tot_ref, all_ref, send_sems, recv_sems, local_sem):
        x, y, c, chips = _place()
        me, sibling = (x, y, c), (x, y, 1 - c)

        def rows(dev):
            return all_ref.at[pl.ds(pl.multiple_of((4 * dev[0] + 2 * dev[1] + dev[2]) * n_rows, 8), n_rows), :]

        def copy(k, origin, to, src=None):
            return pltpu.make_async_remote_copy(
                src_ref=rows(origin) if src is None else src, dst_ref=rows(origin), send_sem=send_sems.at[k],
                recv_sem=recv_sems.at[k], device_id=to, device_id_type=MESH_ID)

        mine = pltpu.make_async_copy(p_ref, rows(me), local_sem)
        mine.start()
        first = [copy(0, me, sibling, src=p_ref)]
        first += [copy(1 + k, me, (chip[0], chip[1], c), src=p_ref) for k, chip in enumerate(chips)]
        for cp in first:
            cp.start()
        passed = []
        for k, chip in enumerate(chips):
            origin = (chip[0], chip[1], c)
            copy(1 + k, origin, me).wait_recv()
            fwd = copy(4 + k, origin, sibling)
            fwd.start()
            passed.append(fwd)
        copy(0, sibling, me).wait_recv()
        for k, chip in enumerate(chips):
            copy(4 + k, (chip[0], chip[1], 1 - c), me).wait_recv()
        for cp in first + passed:
            cp.wait_send()
        mine.wait()
        tot = all_ref[0:n_rows, :]
        for d in range(1, N_DEV):
            tot = tot + all_ref[d * n_rows:(d + 1) * n_rows, :]
        tot_ref[...] = tot

    vmem = pl.BlockSpec(memory_space=pltpu.VMEM)
    return pl.pallas_call(
        body, name="allreduce_small",
        out_shape=jax.ShapeDtypeStruct((n_rows, 128), F32),
        in_specs=[vmem], out_specs=vmem,
        scratch_shapes=[pltpu.VMEM((N_DEV * n_rows, 128), F32), pltpu.SemaphoreType.DMA((7,)),
                        pltpu.SemaphoreType.DMA((7,)), pltpu.SemaphoreType.DMA],
        compiler_params=pltpu.CompilerParams(vmem_limit_bytes=VMEM_LIMIT_V7X),
    )(part)


def _hosted(phases, step, n_steps, pass_on_step=None):
    phase_a, phase_b, phase_c = phases
    if pass_on_step is None:
        pass_on_step = (3 * n_steps) // 4

    def at_start():
        pl.when(step == 0)(phase_a)

    def at_end():
        pl.when(step == pass_on_step)(phase_b)
        pl.when(step == n_steps - 1)(phase_c)

    return at_start, at_end


def _layer_norm_stats(vv):
    mu = jnp.mean(vv, axis=-1, keepdims=True)
    xc = vv - mu
    rs = lax.rsqrt(jnp.mean(xc * xc, axis=-1, keepdims=True) + NORM_EPS)
    return xc * rs, mu, rs


STAT_POST_RMS, STAT_LN_MEAN, STAT_LN_RSTD = (slice(k * CHUNK, (k + 1) * CHUNK) for k in range(3))
N_STATS = 3


def _over_lanes(stat, width):
    return jnp.concatenate([stat] * (width // CHUNK), axis=1)


def _blockdiag(v, lo):
    zero = jnp.zeros_like(v)
    return jnp.concatenate([jnp.where(lo, v, zero), jnp.where(lo, zero, v)], axis=0)


def _softmax_sink(s, sink):
    m = jnp.maximum(jnp.max(s, axis=-1, keepdims=True), sink)
    e = jnp.exp(s - m)
    esink = jnp.exp(sink - m)
    den = jnp.sum(e, axis=-1, keepdims=True) + esink
    return e / den


def _kv_rows(cur_ref, halo_ref, r0, c):
    prev_in_tile = cur_ref[pl.ds(pl.multiple_of(jnp.maximum(r0 - CHUNK, 0), CHUNK), CHUNK), :]
    prev = jnp.where(c == 0, halo_ref[...], prev_in_tile)
    kv2 = jnp.concatenate([prev, cur_ref[pl.ds(r0, CHUNK), :]], axis=0)
    k2, v2 = kv2[:, 0:KV_WIDTH], kv2[:, KV_WIDTH:2 * KV_WIDTH]
    return (k2, _swap_halves(k2)), (v2, _swap_halves(v2))


def _window_square(over_keys, prev):
    return jnp.where(prev, over_keys[:, 0:CHUNK], over_keys[:, CHUNK:2 * CHUNK])


def _window_keys(square, prev):
    zero = jnp.zeros_like(square)
    return jnp.concatenate([jnp.where(prev, square, zero), jnp.where(prev, zero, square)], axis=1)


def _ahead(shape=(CHUNK, CHUNK)):
    return lax.broadcasted_iota(jnp.int32, shape, 1) - lax.broadcasted_iota(jnp.int32, shape, 0)


def _dead_mask(first, ahead):
    return ahead > jnp.where(first, 0, CHUNK)


def _head_of(p, half):
    return 2 * p + half, int(half != p // 2)


_HEADS_BY_COPY = tuple(tuple((p, half) for p in range(4) for half in range(2) if _head_of(p, half)[1] == sw)
                       for sw in range(2))


def _masked_halves(tile, lo):
    zero = jnp.zeros_like(tile)
    return {0: jnp.where(lo, tile, zero), 1: jnp.where(lo, zero, tile)}


def _query_stacks(q_tiles, lo):
    qm = {p: _masked_halves(q_tiles[p] * QK_SCALE, lo) for p in range(4)}
    return [jnp.concatenate([qm[p][half] for p, half in members], axis=0) for members in _HEADS_BY_COPY]


def _attention_probs(q_stacks, ks, bias_ref, sink_ref, dead, prev):
    probs = {}
    for sw, members in enumerate(_HEADS_BY_COPY):
        s_stack = _dot_nt(q_stacks[sw], ks[sw])
        for i, (p, half) in enumerate(members):
            head = 2 * p + half
            s = _window_square(s_stack[i * CHUNK:(i + 1) * CHUNK, :], prev) + bias_ref[head]
            s = jnp.where(dead, NEG, s)
            probs[p, half] = _softmax_sink(s, sink_ref[head]).astype(BF16)
    return probs


def _attention_values(probs, vs, prev):
    outs, p_stacks = {}, []
    for sw, members in enumerate(_HEADS_BY_COPY):
        p_stack = jnp.concatenate([_window_keys(probs[m], prev) for m in members], axis=0)
        r_stack = _dot(p_stack, vs[sw])
        p_stacks.append(p_stack)
        for i, m in enumerate(members):
            outs[m] = r_stack[i * CHUNK:(i + 1) * CHUNK, :]
    return outs, p_stacks


def _fwd_layer(x, g_pre, w_in_t, ln_g, ln_b, wcat, bs, sinks, bias, w_out, g_post, seq_len, gather=None, target=None):
    n_tok = x.shape[0]
    tb = TOK_TILE
    n_chunks = tb // CHUNK
    n_tiles = n_tok // tb
    n_in = 11
    assert gather is None or target is None

    def body(*refs):
        (x_ref, gpre_ref, wt_ref, lng_ref, lnb_ref, wcat_ref, bs_ref, sink_ref, bias_ref, wout_ref,
         gpost_ref) = refs[:n_in]
        at_start = at_end = lambda: None
        if gather is not None:
            (wts_ref, wos_ref, a_ref, q_ref, kv_ref, bz_ref, probs_ref, stats_ref, mixed_ref, act_ref, att_ref, xn_ref, y_ref,
             wt_full, wo_full, ycat_s, halo_ref, send_sems, recv_sems, local_sems) = refs[n_in:]
            phases = _gather_phases((wts_ref.at[gather[2]], wos_ref.at[gather[2]]), (wt_full, wo_full), send_sems,
                                    recv_sems, local_sems)
            at_start, at_end = _hosted(phases, pl.program_id(0), n_tiles)
        elif target is not None:
            (target_ref, a_ref, q_ref, kv_ref, bz_ref, probs_ref, stats_ref, mixed_ref, act_ref, att_ref, xn_ref, y_ref,
             loss_ref, ycat_s, halo_ref) = refs[n_in:]
        else:
            (a_ref, q_ref, kv_ref, bz_ref, probs_ref, stats_ref, mixed_ref, act_ref, att_ref, xn_ref, y_ref, ycat_s,
             halo_ref) = refs[n_in:]
        at_start()
        i = pl.program_id(0)
        lo = _lo_mask((CHUNK, CHUNK))
        ahead = _ahead()
        prev = ahead > 0

        @pl.when(i == 0)
        def _():
            halo_ref[...] = jnp.zeros_like(halo_ref)

        xf = x_ref[...]
        r1 = lax.rsqrt(jnp.mean(xf * xf, axis=-1, keepdims=True) + NORM_EPS)
        h = ((xf * r1) * gpre_ref[...]).astype(BF16)
        a_ref[...] = _dot_nt(h, wt_ref[0:Q_OFF, :])
        q_ref[...] = _dot_nt(h, wt_ref[Q_OFF:K_OFF, :]).astype(BF16)
        kv_ref[...] = _dot_nt(h, wt_ref[K_OFF:BZ_OFF, :]).astype(BF16)
        bz_ref[...] = _dot_nt(h, wt_ref[BZ_OFF:IN_WIDTH, :])

        def chunk(c, carry):
            r0 = pl.multiple_of(c * CHUNK, CHUNK)
            rows = pl.ds(r0, CHUNK)
            u, gu = _gelu_and_grad(a_ref[rows, 0:A_WIDTH])
            vv, gv = _gelu_and_grad(a_ref[rows, A_WIDTH:2 * A_WIDTH])
            for k, act in enumerate((u, gu, vv, gv)):
                act_ref[rows, k * A_WIDTH:(k + 1) * A_WIDTH] = act
            xhat, mu, rs = _layer_norm_stats(vv)
            stats_ref[rows, STAT_LN_MEAN] = jnp.broadcast_to(mu, (CHUNK, CHUNK))
            stats_ref[rows, STAT_LN_RSTD] = jnp.broadcast_to(rs, (CHUNK, CHUNK))
            vnb = (xhat * lng_ref[...] + lnb_ref[...]).astype(BF16)
            for p in range(4):
                blk = slice(p * CHUNK, (p + 1) * CHUNK)
                mixed = _dot(wcat_ref[p], _blockdiag(vnb[:, blk], lo)) + bs_ref[p]
                mixed_ref[rows, blk] = mixed
                sz = _silu(a_ref[rows, 2 * A_WIDTH + p * CHUNK:2 * A_WIDTH + (p + 1) * CHUNK])
                ycat_s[rows, blk] = ((u[:, blk] * mixed) * sz).astype(BF16)
            ks, vs = _kv_rows(kv_ref, halo_ref, r0, c)
            dead = _dead_mask(lax.rem(i * tb + r0, seq_len) == 0, ahead)
            q_tiles = [q_ref[rows, p * CHUNK:(p + 1) * CHUNK] for p in range(4)]
            probs = _attention_probs(_query_stacks(q_tiles, lo), ks, bias_ref, sink_ref, dead, prev)
            for (p, half), prob in probs.items():
                head = 2 * p + half
                probs_ref[rows, head * CHUNK:(head + 1) * CHUNK] = prob
            outs, _ = _attention_values(probs, vs, prev)
            for p in range(4):
                blk = slice(p * CHUNK, (p + 1) * CHUNK)
                o = jnp.where(lo, outs[p, 0], outs[p, 1])
                att_ref[rows, blk] = o
                ycat_s[rows, B_WIDTH + p * CHUNK:B_WIDTH + (p + 1) * CHUNK] = (o * _silu(bz_ref[rows, blk])).astype(BF16)
            return carry

        lax.fori_loop(0, n_chunks, chunk, 0, unroll=CHUNK_UNROLL)
        halo_ref[...] = kv_ref[tb - CHUNK:tb, :]
        y = _dot(ycat_s[...], wout_ref[...])
        r = lax.rsqrt(jnp.mean(y * y, axis=-1, keepdims=True) + NORM_EPS)
        y_ref[...] = y
        stats_ref[:, STAT_POST_RMS] = jnp.broadcast_to(r, (tb, CHUNK))
        xn = x_ref[...] + (y * r) * gpost_ref[...]
        if target is None:
            xn_ref[...] = xn
        else:
            d = xn - target_ref[...]
            xn_ref[...] = d * (1.0 / D_MODEL)

            @pl.when(i == 0)
            def _():
                loss_ref[0, 0] = 0.0

            loss_ref[0, 0] += 0.5 * jnp.sum(jnp.mean(d * d, axis=-1, keepdims=True))
        at_end()

    tile = lambda w: pl.BlockSpec((tb, w), lambda i: (i, 0))
    whole = lambda shape, **kw: pl.BlockSpec(shape, lambda i: (0,) * len(shape), **kw)
    in_specs = [tile(D_MODEL), whole((1, D_MODEL)), whole((IN_WIDTH, D_MODEL), pipeline_mode=pl.Buffered(1)),
                whole((1, A_WIDTH)), whole((1, A_WIDTH)), whole((4, CHUNK, 2 * CHUNK)), whole((4, CHUNK, CHUNK)), SMEM,
                whole((HEADS, CHUNK, CHUNK)), whole((D_MODEL, D_MODEL)), whole((1, D_MODEL))]
    out_shape = [jax.ShapeDtypeStruct((n_tok, Q_OFF), F32), jax.ShapeDtypeStruct((n_tok, B_WIDTH), BF16),
                 jax.ShapeDtypeStruct((n_tok, 2 * KV_WIDTH), BF16), jax.ShapeDtypeStruct((n_tok, B_WIDTH), F32),
                 jax.ShapeDtypeStruct((n_tok, HEADS * CHUNK), BF16), jax.ShapeDtypeStruct((n_tok, N_STATS * CHUNK), F32),
                 jax.ShapeDtypeStruct((n_tok, A_WIDTH), F32), jax.ShapeDtypeStruct((n_tok, 4 * A_WIDTH), F32),
                 jax.ShapeDtypeStruct((n_tok, B_WIDTH), F32),
                 jax.ShapeDtypeStruct((n_tok, D_MODEL), F32), jax.ShapeDtypeStruct((n_tok, D_MODEL), F32)]
    out_specs = [tile(Q_OFF), tile(B_WIDTH), tile(2 * KV_WIDTH), tile(B_WIDTH), tile(HEADS * CHUNK), tile(N_STATS * CHUNK),
                 tile(A_WIDTH), tile(4 * A_WIDTH), tile(B_WIDTH), tile(D_MODEL), tile(D_MODEL)]
    scratch = [pltpu.VMEM((tb, D_MODEL), BF16), pltpu.VMEM((CHUNK, 2 * KV_WIDTH), BF16)]
    args = [x, g_pre, w_in_t, ln_g, ln_b, wcat, bs, sinks, bias, w_out, g_post]
    if gather is not None:
        wt_shards, wo_shards, _ = gather
        in_specs += [ANY, ANY]
        args += [wt_shards, wo_shards]
        out_shape += [jax.ShapeDtypeStruct((N_CHIPS * wt_shards.shape[1], D_MODEL), BF16),
                      jax.ShapeDtypeStruct((N_CHIPS * wo_shards.shape[1], D_MODEL), BF16)]
        out_specs += [ANY, ANY]
        scratch += _comm_scratch(N_GATHER_SEMS)
    if target is not None:
        in_specs.append(tile(D_MODEL))
        args.append(target)
        out_shape.append(jax.ShapeDtypeStruct((1, 1), F32))
        out_specs.append(SMEM)
    name = "fwd_layer" + ("" if gather is None else "_gather") + ("" if target is None else "_loss")
    return pl.pallas_call(
        body, name=name, grid=(n_tiles,),
        out_shape=tuple(out_shape), in_specs=in_specs, out_specs=tuple(out_specs), scratch_shapes=scratch,
        compiler_params=_cparams(1),
    )(*args)


def _bwd_mix(dout, y, a, q, kv, bz, probs, stats, mixed, act, att, ln_g, ln_b, wtcat, w_out, g_post, exchange=None,
             allgather=None):
    n_tok = y.shape[0]
    tb = TOK_TILE
    n_chunks = tb // CHUNK
    n_tiles = n_tok // tb

    def body(*refs):
        refs = list(refs)
        take = lambda n: [refs.pop(0) for _ in range(n)]
        (dout_ref, y_ref, az_ref, q_ref, kv_ref, halo_ref, bz_ref, probs_ref, stats_ref, mixed_ref, act_ref, att_ref,
         lng_ref, lnb_ref, wtcat_ref, wout_ref, gpost_ref) = take(17)
        if exchange is not None:
            dwt_in, dwo_in = take(2)
        if allgather is not None:
            (small_in,) = take(1)
        dproj_ref, dwout_ref, dws_ref, dmix_ref, dlng_ref, dlnb_ref, dgpost_ref, dbias_ref = take(8)
        if exchange is not None:
            pt_ref, po_ref = take(2)
        if allgather is not None:
            (small_all,) = take(1)
        ycat_s, dy_s, dyc_s, dkv_s, carry_s, dwout_s, stage_s, dws_s = take(8)
        starts, ends = [], []
        if exchange is not None:
            hosted = _hosted(_exchange_phases((dwt_in, dwo_in), (pt_ref, po_ref), *take(3)), pl.program_id(0), n_tiles)
            starts.append(hosted[0])
            ends.append(hosted[1])
        if allgather is not None:
            hosted = _hosted(_exchange_phases((small_in,), (small_all,), *take(3), by_chip=False), pl.program_id(0),
                             n_tiles)
            starts.append(hosted[0])
            ends.append(hosted[1])
        at_start = lambda: [f() for f in starts]
        at_end = lambda: [f() for f in ends]
        at_start()
        step = pl.program_id(0)
        lo = _lo_mask((CHUNK, CHUNK))
        prev = _ahead() > 0

        @pl.when(step == 0)
        def _():
            dwout_s[...] = jnp.zeros_like(dwout_s)
            dws_s[...] = jnp.zeros_like(dws_s)
            dmix_ref[...] = jnp.zeros_like(dmix_ref)
            dlng_ref[...] = jnp.zeros_like(dlng_ref)
            dlnb_ref[...] = jnp.zeros_like(dlnb_ref)
            dgpost_ref[...] = jnp.zeros_like(dgpost_ref)
            dbias_ref[...] = jnp.zeros_like(dbias_ref)
            carry_s[...] = jnp.zeros_like(carry_s)

        for r0 in range(0, tb, CHUNK):
            rows = slice(r0, r0 + CHUNK)
            yv = y_ref[rows, :]
            dout = dout_ref[rows, :]
            r = _over_lanes(stats_ref[rows, STAT_POST_RMS], D_MODEL)
            yn = yv * r
            dgpost_ref[...] += jnp.sum(dout * yn, axis=0, keepdims=True)
            dyn = dout * gpost_ref[...]
            dy_s[rows, :] = (r * (dyn - yn * jnp.mean(dyn * yn, axis=-1, keepdims=True))).astype(BF16)
        dyc_s[...] = _dot_nt(dy_s[...], wout_ref[...])
        dkv_s[0:tb, :] = jnp.zeros((tb, 2 * KV_WIDTH), F32)
        dkv_s[tb:tb + CHUNK, :] = carry_s[...]

        def chunk(c, carry):
            r0 = pl.multiple_of(c * CHUNK, CHUNK)
            rows = pl.ds(r0, CHUNK)
            u, gu, vv, gv = (act_ref[rows, k * A_WIDTH:(k + 1) * A_WIDTH] for k in range(4))
            rs = _over_lanes(stats_ref[rows, STAT_LN_RSTD], A_WIDTH)
            xhat = (vv - _over_lanes(stats_ref[rows, STAT_LN_MEAN], A_WIDTH)) * rs
            vnb = (xhat * lng_ref[...] + lnb_ref[...]).astype(BF16)
            d_vn, d_u, d_az = [], [], []
            for p in range(4):
                blk = slice(p * CHUNK, (p + 1) * CHUNK)
                mixed = mixed_ref[rows, blk]
                sz, gz = _silu_and_grad(az_ref[rows, blk])
                ub = u[:, blk]
                dya = dyc_s[rows, blk]
                um = ub * mixed
                ycat_s[rows, blk] = (um * sz).astype(BF16)
                d_mixed = (dya * ub) * sz
                d_u.append((dya * mixed) * sz)
                d_az.append((dya * um) * gz)
                dmix_ref[:, blk] += d_mixed
                dmbd = _blockdiag(d_mixed.astype(BF16), lo)
                d_vn.append(_dot(wtcat_ref[p], dmbd))
                dws_s[p] += _dot_nt(dmbd, vnb[:, blk])
            d_vn = jnp.concatenate(d_vn, axis=1)
            dlng_ref[...] += jnp.sum(d_vn * xhat, axis=0, keepdims=True)
            dlnb_ref[...] += jnp.sum(d_vn, axis=0, keepdims=True)
            dxh = d_vn * lng_ref[...]
            d_vv = rs * (dxh - jnp.mean(dxh, axis=-1, keepdims=True)
                         - xhat * jnp.mean(dxh * xhat, axis=-1, keepdims=True))
            dproj_ref[rows, 0:A_WIDTH] = (jnp.concatenate(d_u, axis=1) * gu).astype(BF16)
            dproj_ref[rows, A_WIDTH:2 * A_WIDTH] = (d_vv * gv).astype(BF16)
            dproj_ref[rows, 2 * A_WIDTH:Q_OFF] = jnp.concatenate(d_az, axis=1).astype(BF16)
            ks, vs = _kv_rows(kv_ref, halo_ref, r0, c)
            q_stacks = _query_stacks([q_ref[rows, p * CHUNK:(p + 1) * CHUNK] for p in range(4)], lo)
            probs = {(p, half): probs_ref[rows, (2 * p + half) * CHUNK:(2 * p + half + 1) * CHUNK]
                     for p in range(4) for half in range(2)}
            p_stacks = [jnp.concatenate([_window_keys(probs[m], prev) for m in members], axis=0)
                        for members in _HEADS_BY_COPY]
            dom = {}
            for p in range(4):
                blk = slice(p * CHUNK, (p + 1) * CHUNK)
                sb, gb = _silu_and_grad(bz_ref[rows, blk])
                dyb = dyc_s[rows, B_WIDTH + p * CHUNK:B_WIDTH + (p + 1) * CHUNK]
                o = att_ref[rows, blk]
                ycat_s[rows, B_WIDTH + p * CHUNK:B_WIDTH + (p + 1) * CHUNK] = (o * sb).astype(BF16)
                dproj_ref[rows, BZ_OFF + p * CHUNK:BZ_OFF + (p + 1) * CHUNK] = ((dyb * o) * gb).astype(BF16)
                dom[p] = _masked_halves((dyb * sb).astype(BF16), lo)
            dqs, dk_by_copy, dv_by_copy = {}, [], []
            for sw, members in enumerate(_HEADS_BY_COPY):
                do_stack = jnp.concatenate([dom[p][half] for p, half in members], axis=0)
                dp_stack = _dot_nt(do_stack, vs[sw])
                dsws = []
                for i, (p, half) in enumerate(members):
                    head = 2 * p + half
                    dp = _window_square(dp_stack[i * CHUNK:(i + 1) * CHUNK, :], prev)
                    prob = probs[p, half].astype(F32)
                    delta = jnp.sum(prob * dp, axis=-1, keepdims=True)
                    ds = prob * (dp - delta)
                    dbias_ref[head] += ds
                    dsws.append(_window_keys(ds.astype(BF16), prev))
                ds_stack = jnp.concatenate(dsws, axis=0)
                dq_stack = _dot(ds_stack, ks[sw])
                for i, m in enumerate(members):
                    dqs[m] = dq_stack[i * CHUNK:(i + 1) * CHUNK, :]
                dk_by_copy.append(_dot_tn(ds_stack, q_stacks[sw]))
                dv_by_copy.append(_dot_tn(p_stacks[sw], do_stack))
            for p in range(4):
                dq = jnp.where(lo, dqs[p, 0], dqs[p, 1]) * QK_SCALE
                dproj_ref[rows, Q_OFF + p * CHUNK:Q_OFF + (p + 1) * CHUNK] = dq.astype(BF16)
            both = pl.ds(r0, 2 * CHUNK)
            dkv_s[both, 0:KV_WIDTH] += dk_by_copy[0] + _swap_halves(dk_by_copy[1])
            dkv_s[both, KV_WIDTH:2 * KV_WIDTH] += dv_by_copy[0] + _swap_halves(dv_by_copy[1])
            return carry

        lax.fori_loop(0, n_chunks, chunk, 0, unroll=BWD_CHUNK_UNROLL)
        dwout_s[...] += _dot_tn(ycat_s[...], dy_s[...])
        dproj_ref[:, K_OFF:BZ_OFF] = dkv_s[CHUNK:CHUNK + tb, :].astype(BF16)
        carry_s[...] = dkv_s[0:CHUNK, :]

        @pl.when(step == n_tiles - 1)
        def _():
            row = lax.broadcasted_iota(jnp.int32, (2 * CHUNK, CHUNK), 0)
            col = lax.broadcasted_iota(jnp.int32, (2 * CHUNK, CHUNK), 1)
            causal = col <= jnp.where(row >= CHUNK, row - CHUNK, row)
            for p in range(4):
                dws_ref[p] = jnp.where(causal, dws_s[p], 0.0).astype(BF16)
            rows = stage_s.shape[0]
            for r0 in range(0, D_MODEL, rows):
                stage_s[...] = dwout_s[r0:r0 + rows, :].astype(BF16)
                pltpu.sync_copy(stage_s, dwout_ref.at[r0:r0 + rows, :])

        at_end()

    tile = lambda w: pl.BlockSpec((tb, w), lambda s: (n_tiles - 1 - s, 0))
    whole = lambda shape, **kw: pl.BlockSpec(shape, lambda s: (0,) * len(shape), **kw)
    once = dict(pipeline_mode=pl.Buffered(1))
    gate_cols = pl.BlockSpec((tb, A_WIDTH), lambda s: (n_tiles - 1 - s, 2))
    in_specs = [tile(D_MODEL), tile(D_MODEL), gate_cols, tile(B_WIDTH), tile(2 * KV_WIDTH),
                pl.BlockSpec((CHUNK, 2 * KV_WIDTH), lambda s: (jnp.maximum((n_tiles - 1 - s) * n_chunks - 1, 0), 0)),
                tile(B_WIDTH), tile(HEADS * CHUNK), tile(N_STATS * CHUNK), tile(A_WIDTH), tile(4 * A_WIDTH), tile(B_WIDTH),
                whole((1, A_WIDTH)), whole((1, A_WIDTH)), whole((4, CHUNK, 2 * CHUNK)), whole((D_MODEL, D_MODEL), **once),
                whole((1, D_MODEL))]
    args = [dout, y, a, q, kv, kv, bz, probs, stats, mixed, act, att, ln_g, ln_b, wtcat, w_out, g_post]
    small_shapes = [jax.ShapeDtypeStruct((D_MODEL, D_MODEL), BF16), jax.ShapeDtypeStruct((4, 2 * CHUNK, CHUNK), BF16),
                    jax.ShapeDtypeStruct((CHUNK, A_WIDTH), F32), jax.ShapeDtypeStruct((1, A_WIDTH), F32),
                    jax.ShapeDtypeStruct((1, A_WIDTH), F32), jax.ShapeDtypeStruct((1, D_MODEL), F32),
                    jax.ShapeDtypeStruct((HEADS, CHUNK, CHUNK), F32)]
    small_specs = [ANY, whole((4, 2 * CHUNK, CHUNK)), whole((CHUNK, A_WIDTH)), whole((1, A_WIDTH)), whole((1, A_WIDTH)),
                   whole((1, D_MODEL)), whole((HEADS, CHUNK, CHUNK))]
    scratch = [pltpu.VMEM((tb, D_MODEL), BF16), pltpu.VMEM((tb, D_MODEL), BF16), pltpu.VMEM((tb, D_MODEL), F32),
               pltpu.VMEM((tb + CHUNK, 2 * KV_WIDTH), F32), pltpu.VMEM((CHUNK, 2 * KV_WIDTH), F32),
               pltpu.VMEM((D_MODEL, D_MODEL), F32), pltpu.VMEM((2 * CHUNK, D_MODEL), BF16),
               pltpu.VMEM((4, 2 * CHUNK, CHUNK), F32)]
    out_shape = [jax.ShapeDtypeStruct((n_tok, IN_WIDTH), BF16)] + small_shapes
    out_specs = [tile(IN_WIDTH)] + small_specs
    if exchange is not None:
        dwt, dwo = exchange
        in_specs += [ANY, ANY]
        args += [dwt, dwo]
        out_shape += [jax.ShapeDtypeStruct((N_DEV, dwt.shape[0] // N_CHIPS, D_MODEL), BF16),
                      jax.ShapeDtypeStruct((N_DEV, dwo.shape[0] // N_CHIPS, D_MODEL), BF16)]
        out_specs += [ANY, ANY]
    if allgather is not None:
        in_specs.append(ANY)
        args.append(allgather)
        out_shape.append(jax.ShapeDtypeStruct((N_DEV,) + allgather.shape, allgather.dtype))
        out_specs.append(ANY)
    for hosted in (exchange, allgather):
        if hosted is not None:
            scratch += _comm_scratch(N_EXCHANGE_SEMS)
    return pl.pallas_call(
        body, name="bwd_mix" + ("" if exchange is None and allgather is None else "_exchange"), grid=(n_tiles,),
        out_shape=tuple(out_shape), in_specs=in_specs, out_specs=tuple(out_specs), scratch_shapes=scratch,
        compiler_params=_cparams(1),
    )(*args)


def _bwd_in(dproj, x, dout, g_pre, w_in_t):
    n_tok = x.shape[0]
    tm = TOK_TILE
    n_tiles = n_tok // tm

    def body(dp_ref, x_ref, dout_ref, g_ref, wt_ref, dx_ref, dwt_ref, dg_ref, acc_s):
        i = pl.program_id(0)

        @pl.when(i == 0)
        def _():
            acc_s[...] = jnp.zeros_like(acc_s)
            dg_ref[...] = jnp.zeros_like(dg_ref)

        xf = x_ref[...]
        r = lax.rsqrt(jnp.mean(xf * xf, axis=-1, keepdims=True) + NORM_EPS)
        xn = xf * r
        h = (xn * g_ref[...]).astype(BF16)
        dp = dp_ref[...]
        dh = _dot(dp, wt_ref[...])
        acc_s[...] += _dot_tn(dp, h)
        dg_ref[...] += jnp.sum(dh * xn, axis=0, keepdims=True)
        dhn = dh * g_ref[...]
        dx_ref[...] = dout_ref[...] + r * (dhn - xn * jnp.mean(dhn * xn, axis=-1, keepdims=True))

        @pl.when(i == n_tiles - 1)
        def _():
            dwt_ref[...] = acc_s[...].astype(BF16)

    return pl.pallas_call(
        body, name="bwd_in", grid=(n_tiles,),
        out_shape=(jax.ShapeDtypeStruct((n_tok, D_MODEL), F32), jax.ShapeDtypeStruct((IN_WIDTH, D_MODEL), BF16),
                   jax.ShapeDtypeStruct((1, D_MODEL), F32)),
        in_specs=[pl.BlockSpec((tm, IN_WIDTH), lambda i: (i, 0)), pl.BlockSpec((tm, D_MODEL), lambda i: (i, 0)),
                  pl.BlockSpec((tm, D_MODEL), lambda i: (i, 0)), pl.BlockSpec((1, D_MODEL), lambda i: (0, 0)),
                  pl.BlockSpec((IN_WIDTH, D_MODEL), lambda i: (0, 0), pipeline_mode=pl.Buffered(1))],
        out_specs=(pl.BlockSpec((tm, D_MODEL), lambda i: (i, 0)),
                   pl.BlockSpec((IN_WIDTH, D_MODEL), lambda i: (0, 0), pipeline_mode=pl.Buffered(1)),
                   pl.BlockSpec((1, D_MODEL), lambda i: (0, 0))),
        scratch_shapes=[pltpu.VMEM((IN_WIDTH, D_MODEL), F32)],
        compiler_params=_cparams(1),
    )(dproj, x, dout, g_pre, w_in_t)


def _riders(exchanges):
    args = [a for arrays, _ in exchanges for a in arrays]
    shapes = [jax.ShapeDtypeStruct((N_DEV, a.shape[0] // N_CHIPS if by_chip else a.shape[0]) + a.shape[1:], a.dtype)
              for arrays, by_chip in exchanges for a in arrays]
    scratch = [s for _ in exchanges for s in _comm_scratch(N_EXCHANGE_SEMS)]

    def bind(in_refs, out_refs, sem_refs, step, n_steps, pass_on_step=None):
        in_refs, out_refs, sem_refs = list(in_refs), list(out_refs), list(sem_refs)
        starts, ends = [], []
        for arrays, by_chip in exchanges:
            n = len(arrays)
            phases = _exchange_phases(tuple(in_refs[:n]), tuple(out_refs[:n]), *sem_refs[:3], by_chip=by_chip)
            del in_refs[:n], out_refs[:n], sem_refs[:3]
            at_start, at_end = _hosted(phases, step, n_steps, pass_on_step)
            starts.append(at_start)
            ends.append(at_end)
        return (lambda: [f() for f in starts]), (lambda: [f() for f in ends])

    return args, [ANY] * len(args), shapes, [ANY] * len(shapes), scratch, bind


def _bwd_in_dw(dproj, x, g_pre, exchanges):
    n_tok = x.shape[0]
    tm = TOK_TILE
    n_tiles = n_tok // tm
    r_args, r_in_specs, r_shapes, r_out_specs, r_scratch, bind = _riders(exchanges)

    def body(*refs):
        dp_ref, x_ref, g_ref = refs[:3]
        in_refs = refs[3:3 + len(r_args)]
        dwt_ref = refs[3 + len(r_args)]
        out_refs = refs[4 + len(r_args):4 + 2 * len(r_args)]
        acc_s = refs[4 + 2 * len(r_args)]
        i = pl.program_id(0)
        at_start, at_end = bind(in_refs, out_refs, refs[5 + 2 * len(r_args):], i, n_tiles)
        at_start()

        @pl.when(i == 0)
        def _():
            acc_s[...] = jnp.zeros_like(acc_s)

        xf = x_ref[...]
        r = lax.rsqrt(jnp.mean(xf * xf, axis=-1, keepdims=True) + NORM_EPS)
        acc_s[...] += _dot_tn(dp_ref[...], ((xf * r) * g_ref[...]).astype(BF16))

        @pl.when(i == n_tiles - 1)
        def _():
            dwt_ref[...] = acc_s[...].astype(BF16)

        at_end()

    return pl.pallas_call(
        body, name="bwd_in_dw", grid=(n_tiles,),
        out_shape=(jax.ShapeDtypeStruct((IN_WIDTH, D_MODEL), BF16), *r_shapes),
        in_specs=[pl.BlockSpec((tm, IN_WIDTH), lambda i: (i, 0)), pl.BlockSpec((tm, D_MODEL), lambda i: (i, 0)),
                  pl.BlockSpec((1, D_MODEL), lambda i: (0, 0))] + r_in_specs,
        out_specs=(pl.BlockSpec((IN_WIDTH, D_MODEL), lambda i: (0, 0), pipeline_mode=pl.Buffered(1)), *r_out_specs),
        scratch_shapes=[pltpu.VMEM((IN_WIDTH, D_MODEL), F32)] + r_scratch,
        compiler_params=_cparams(1),
    )(dproj, x, g_pre, *r_args)


def _bwd_in_dx(dproj, x, dout, g_pre, w_in_t, exchanges):
    n_tok = x.shape[0]
    tm = TOK_TILE
    n_tiles = n_tok // tm
    r_args, r_in_specs, r_shapes, r_out_specs, r_scratch, bind = _riders(exchanges)

    def body(*refs):
        dp_ref, x_ref, dout_ref, g_ref, wt_ref = refs[:5]
        in_refs = refs[5:5 + len(r_args)]
        dx_ref, dg_ref = refs[5 + len(r_args):7 + len(r_args)]
        out_refs = refs[7 + len(r_args):7 + 2 * len(r_args)]
        i = pl.program_id(0)
        at_start, at_end = bind(in_refs, out_refs, refs[7 + 2 * len(r_args):], i, n_tiles, pass_on_step=n_tiles - 2)
        at_start()

        @pl.when(i == 0)
        def _():
            dg_ref[...] = jnp.zeros_like(dg_ref)

        xf = x_ref[...]
        r = lax.rsqrt(jnp.mean(xf * xf, axis=-1, keepdims=True) + NORM_EPS)
        xn = xf * r
        dh = _dot(dp_ref[...], wt_ref[...])
        dg_ref[...] += jnp.sum(dh * xn, axis=0, keepdims=True)
        dhn = dh * g_ref[...]
        dx_ref[...] = dout_ref[...] + r * (dhn - xn * jnp.mean(dhn * xn, axis=-1, keepdims=True))
        at_end()

    tile = lambda w: pl.BlockSpec((tm, w), lambda i: (i, 0))
    return pl.pallas_call(
        body, name="bwd_in_dx", grid=(n_tiles,),
        out_shape=(jax.ShapeDtypeStruct((n_tok, D_MODEL), F32), jax.ShapeDtypeStruct((1, D_MODEL), F32), *r_shapes),
        in_specs=[tile(IN_WIDTH), tile(D_MODEL), tile(D_MODEL), pl.BlockSpec((1, D_MODEL), lambda i: (0, 0)),
                  pl.BlockSpec((IN_WIDTH, D_MODEL), lambda i: (0, 0), pipeline_mode=pl.Buffered(1))] + r_in_specs,
        out_specs=(tile(D_MODEL), pl.BlockSpec((1, D_MODEL), lambda i: (0, 0)), *r_out_specs),
        scratch_shapes=r_scratch,
        compiler_params=_cparams(1),
    )(dproj, x, dout, g_pre, w_in_t, *r_args)


def _adam_update(w, g, m, v):
    nm = ADAM_B1 * m + (1.0 - ADAM_B1) * g
    nv = ADAM_B2 * v + (1.0 - ADAM_B2) * (g * g)
    m_hat = nm / (1.0 - ADAM_B1 ** ADAM_STEP)
    v_hat = nv / (1.0 - ADAM_B2 ** ADAM_STEP)
    return -ADAM_LR * (m_hat / (jnp.sqrt(v_hat) + ADAM_EPS) + ADAM_WD * w), nm, nv


def _sum_slots(p_ref):
    tot = p_ref[0].astype(F32)
    for d in range(1, N_DEV):
        tot = tot + p_ref[d].astype(F32)
    return tot


def _adamw_parts(parts, w, m, v, name):
    n_layers, n_rows, n_cols = w.shape
    tr = n_rows
    while tr * n_cols > ADAMW_BLOCK_ELEMS:
        tr //= 2
    n_blocks = n_rows // tr

    def body(*refs):
        p_refs = refs[:n_layers]
        w_ref, m_ref, v_ref, g_ref, d_ref, nm_ref, nv_ref = refs[n_layers:]
        for k in range(n_layers):
            @pl.when(pl.program_id(0) == k)
            def _(k=k):
                g = _sum_slots(p_refs[k])
                g_ref[0] = g
                d_ref[0], nm_ref[0], nv_ref[0] = _adam_update(w_ref[0], g, m_ref[0], v_ref[0])

    def part_spec(k):
        return pl.BlockSpec((N_DEV, tr, n_cols), lambda l, i: (0, jnp.clip(i + (l - k) * n_blocks, 0, n_blocks - 1), 0))

    spec = pl.BlockSpec((1, tr, n_cols), lambda l, i: (l, i, 0))
    shape = jax.ShapeDtypeStruct((n_layers, n_rows, n_cols), F32)
    return pl.pallas_call(
        body, name="adamw_" + name, grid=(n_layers, n_blocks), out_shape=(shape,) * 4,
        in_specs=[part_spec(k) for k in range(n_layers)] + [spec, spec, spec], out_specs=(spec,) * 4,
        compiler_params=_cparams(2),
    )(*parts, w, m, v)


def _adamw_small(quads):
    n = len(quads)

    def body(*refs):
        ins, outs = refs[:4 * n], refs[4 * n:]
        for j in range(n):
            w_ref, g_ref, m_ref, v_ref = ins[4 * j:4 * j + 4]
            d_ref, nm_ref, nv_ref = outs[3 * j:3 * j + 3]
            d_ref[...], nm_ref[...], nv_ref[...] = _adam_update(w_ref[...], g_ref[...], m_ref[...], v_ref[...])

    vmem = pl.BlockSpec(memory_space=pltpu.VMEM)
    shapes = tuple(jax.ShapeDtypeStruct(w.shape, F32) for w, *_ in quads for _ in range(3))
    results = pl.pallas_call(
        body, name="adamw_small", out_shape=shapes, in_specs=[vmem] * (4 * n), out_specs=(vmem,) * (3 * n),
    )(*[a for quad in quads for a in quad])
    return [results[3 * j:3 * j + 3] for j in range(n)]


_SMALL = ("pre_norm_g", "ln_v_g", "ln_v_b", "b_spatial", "sinks", "rel_bias", "post_norm_g")


def _pack(pieces):
    blocks, first_rows, n = [], [], 0
    for p in pieces:
        flat = p.reshape(-1)
        flat = jnp.concatenate([flat, jnp.zeros(((-flat.shape[0]) % (8 * 128),), F32)]).reshape(-1, 128)
        blocks.append(flat)
        first_rows.append(n)
        n += flat.shape[0]
    return jnp.concatenate(blocks, axis=0), first_rows


def _unpack(block, first_row, shape):
    size = math.prod(shape)
    return block[first_row:first_row + -(-size // 128)].reshape(-1)[:size].reshape(shape)


def kernel(x, pre_norm_g, w_in, ln_v_g, ln_v_b, w_spatial, b_spatial, sinks, rel_bias, w_out, post_norm_g, loss_target, m_pre_norm_g, m_w_in, m_ln_v_g, m_ln_v_b, m_w_spatial, m_b_spatial, m_sinks, m_rel_bias, m_w_out, m_post_norm_g, v_pre_norm_g, v_w_in, v_ln_v_g, v_ln_v_b, v_w_spatial, v_b_spatial, v_sinks, v_rel_bias, v_w_out, v_post_norm_g):
    weights = dict(pre_norm_g=pre_norm_g, w_in=w_in, ln_v_g=ln_v_g, ln_v_b=ln_v_b, w_spatial=w_spatial, b_spatial=b_spatial,
                   sinks=sinks, rel_bias=rel_bias, w_out=w_out, post_norm_g=post_norm_g)
    mom_m = dict(pre_norm_g=m_pre_norm_g, w_in=m_w_in, ln_v_g=m_ln_v_g, ln_v_b=m_ln_v_b, w_spatial=m_w_spatial,
                 b_spatial=m_b_spatial, sinks=m_sinks, rel_bias=m_rel_bias, w_out=m_w_out, post_norm_g=m_post_norm_g)
    mom_v = dict(pre_norm_g=v_pre_norm_g, w_in=v_w_in, ln_v_g=v_ln_v_g, ln_v_b=v_ln_v_b, w_spatial=v_w_spatial,
                 b_spatial=v_b_spatial, sinks=v_sinks, rel_bias=v_rel_bias, w_out=v_w_out, post_norm_g=v_post_norm_g)
    n_seq, seq_len, _ = x.shape
    n_layers = w_in.shape[0]
    x2 = x.reshape(n_seq * seq_len, D_MODEL)
    target2 = loss_target.reshape(n_seq * seq_len, D_MODEL)
    row = lambda p, l: p[l][None]

    wt_shards = jnp.swapaxes(w_in, 1, 2).astype(BF16)
    wo_shards = w_out.astype(BF16)
    bucket = _window_buckets()
    bias = _bias_table(rel_bias, bucket)
    wcat, wtcat, bs = _spatial_tables(w_spatial, jnp.swapaxes(b_spatial, 1, 2))

    wt, wo = [None] * n_layers, [None] * n_layers
    wt[0], wo[0] = _gather_weights(wt_shards, wo_shards, 0)
    xs, saved = [x2], []
    for l in range(n_layers):
        layer_args = (xs[-1], row(pre_norm_g, l), wt[l], row(ln_v_g, l), row(ln_v_b, l), wcat[l], bs[l], sinks[l], bias,
                      wo[l], row(post_norm_g, l), seq_len)
        if l + 1 < n_layers:
            *acts, xn, y, wt[l + 1], wo[l + 1] = _fwd_layer(*layer_args, gather=(wt_shards, wo_shards, l + 1))
            xs.append(xn)
        else:
            *acts, dx, y, loss = _fwd_layer(*layer_args, target=target2)
        saved.append((y, *acts))

    small = [None] * n_layers
    parts = [None] * n_layers
    waiting = None
    for l in reversed(range(n_layers)):
        hosted = {} if waiting is None else dict(exchange=waiting[:2], allgather=waiting[2])
        dproj, *rest = _bwd_mix(dx, *saved[l], row(ln_v_g, l), row(ln_v_b, l), wtcat[l], wo[l], row(post_norm_g, l), **hosted)
        dwo, dws, dmix, dlng, dlnb, dgpost, dbias = rest[:7]
        dws = dws.reshape(HEADS * CHUNK, CHUNK)
        if waiting is not None:
            parts[l + 1] = list(rest[7:])
        if l > 0:
            dx, dwt, dgpre = _bwd_in(dproj, xs[l], dx, row(pre_norm_g, l), wt[l])
            waiting = (dwt, dwo, dws)
        else:
            dwt, po, ws_all = _bwd_in_dw(dproj, xs[l], row(pre_norm_g, l), [((dwo,), True), ((dws,), False)])
            dx, dgpre, pt = _bwd_in_dx(dproj, xs[l], dx, row(pre_norm_g, l), wt[l], [((dwt,), True)])
            parts[l] = [pt, po, ws_all]
        small[l] = dict(pre_norm_g=dgpre[0], ln_v_g=dlng[0], ln_v_b=dlnb[0], dmix=dmix, dbias=dbias,
                        post_norm_g=dgpost[0])

    db = _b_spatial_grad([s["dmix"] for s in small])
    pieces = {n: jnp.stack([s[n] for s in small]) for n in ("pre_norm_g", "ln_v_g", "ln_v_b", "post_norm_g")}
    pieces["b_spatial"] = jnp.swapaxes(db[:, :, :HEADS], 1, 2)
    pieces["rel_bias"], pieces["sinks"] = _rel_bias_grad([s["dbias"] for s in small], bucket)
    packed, first_rows = _pack([pieces[n] for n in _SMALL] + [loss])
    total = _allreduce_small(packed)
    grad = {n: _unpack(total, r, weights[n].shape) for n, r in zip(_SMALL, first_rows)}
    loss_out = total[first_rows[-1], 0]

    delta, new_m, new_v = {}, {}, {}
    two_d = lambda p: p.reshape(-1, p.shape[-1])
    updates = _adamw_small([tuple(two_d(p[n]) for p in (weights, grad, mom_m, mom_v)) for n in _SMALL])
    for n, update in zip(_SMALL, updates):
        delta[n], new_m[n], new_v[n] = (u.reshape(weights[n].shape) for u in update)

    t3 = lambda p: jnp.swapaxes(p, 1, 2)
    flat3 = lambda p: p.reshape(n_layers, HEADS * CHUNK, CHUNK)
    for n, which, view, back in (("w_in", 0, t3, t3), ("w_out", 1, lambda p: p, lambda p: p),
                                 ("w_spatial", 2, flat3, lambda p: p.reshape(w_spatial.shape))):
        results = _adamw_parts([parts[l][which] for l in range(n_layers)], view(weights[n]), view(mom_m[n]),
                               view(mom_v[n]), n)
        grad[n], delta[n], new_m[n], new_v[n] = (back(r) for r in results)

    names = tuple(weights)
    return (loss_out, dx.reshape(x.shape), *[grad[n] for n in names], *[delta[n] for n in names],
            *[new_m[n] for n in names], *[new_v[n] for n in names])
```

```python
import math

import jax
import jax.numpy as jnp
from jax import lax
from jax.experimental import pallas as pl
from jax.experimental.pallas import tpu as pltpu

F32 = jnp.float32
BF16 = jnp.bfloat16

D_MODEL = 1024
A_WIDTH = 512
B_WIDTH = 512
KV_WIDTH = 128
IN_WIDTH = 3 * A_WIDTH + 2 * B_WIDTH + 2 * KV_WIDTH
CHUNK = 128
HEADS = 8
HEAD_DIM = 64
REL_BUCKETS = 32
NORM_EPS = 1e-6
NEG = -1e30
Q_OFF = 3 * A_WIDTH
K_OFF = Q_OFF + B_WIDTH
BZ_OFF = K_OFF + 2 * KV_WIDTH
QK_SCALE = HEAD_DIM ** -0.5

ADAM_LR = 0.001
ADAM_B1 = 0.9
ADAM_B2 = 0.999
ADAM_EPS = 1e-08
ADAM_WD = 0.01
ADAM_STEP = 10

TOK_TILE = 512
CHUNK_UNROLL = 4
BWD_CHUNK_UNROLL = 4
ADAMW_BLOCK_ELEMS = 192 * 1024
VMEM_LIMIT_V7X = 60 * 1024 * 1024

N_DEV = 8
N_CHIPS = 4
MESH_ID = pl.DeviceIdType.MESH
ANY = pl.BlockSpec(memory_space=pl.ANY)
SMEM = pl.BlockSpec(memory_space=pltpu.SMEM)


def _cparams(n_axes):
    return pltpu.CompilerParams(dimension_semantics=("arbitrary",) * n_axes, vmem_limit_bytes=VMEM_LIMIT_V7X)


_GELU_C = math.sqrt(2.0 / math.pi)
_GELU_C3 = _GELU_C * 0.044715


def _gelu_and_grad(x):
    x2 = x * x
    t = jnp.tanh(x * (_GELU_C3 * x2 + _GELU_C))
    cdf = 0.5 * t + 0.5
    d = cdf + (x * (cdf * (1.0 - cdf))) * ((6.0 * _GELU_C3) * x2 + 2.0 * _GELU_C)
    return x * cdf, d


def _sigmoid(x):
    return 0.5 + 0.5 * jnp.tanh(0.5 * x)


def _silu(x):
    return x * _sigmoid(x)


def _silu_and_grad(x):
    s = _sigmoid(x)
    return x * s, s * (1.0 + x * (1.0 - s))


def _dot(a, b):
    return jnp.dot(a, b, preferred_element_type=F32)


def _dot_nt(a, b):
    return lax.dot_general(a, b, (((1,), (1,)), ((), ())), preferred_element_type=F32)


def _dot_tn(a, b):
    return lax.dot_general(a, b, (((0,), (0,)), ((), ())), preferred_element_type=F32)


def _lo_mask(shape):
    return lax.broadcasted_iota(jnp.int32, shape, 1) < HEAD_DIM


def _swap_halves(v):
    return pltpu.roll(v, HEAD_DIM, 1)


def _window_buckets():
    q_loc = jnp.arange(CHUNK)[:, None]
    j_loc = jnp.arange(CHUNK)[None, :]
    d = q_loc - j_loc + jnp.where(j_loc > q_loc, CHUNK, 0)
    max_exact = REL_BUCKETS // 2
    safe = jnp.maximum(d, 1).astype(F32)
    large = max_exact + (jnp.log(safe / max_exact) / math.log(CHUNK / max_exact)
                         * (REL_BUCKETS - max_exact)).astype(jnp.int32)
    large = jnp.minimum(large, REL_BUCKETS - 1)
    return jnp.where(d < max_exact, d, large).astype(jnp.int32)


def _bias_table(rel_bias, bucket):
    def body(rb_ref, bk_ref, out_ref):
        bk = bk_ref[...]
        for h in range(HEADS):
            acc = jnp.zeros(bk.shape, F32)
            for b in range(REL_BUCKETS):
                acc = jnp.where(bk == b, rb_ref[b, h], acc)
            out_ref[h] = acc

    vmem = pl.BlockSpec(memory_space=pltpu.VMEM)
    return pl.pallas_call(
        body, name="bias_table",
        out_shape=jax.ShapeDtypeStruct((HEADS, CHUNK, CHUNK), F32),
        in_specs=[SMEM, vmem], out_specs=vmem,
    )(rel_bias, bucket)


def _rel_bias_grad(dbias, bucket):
    n_layers = len(dbias)

    def body(*refs):
        db_refs, (bk_ref, out_ref, dsink_ref) = refs[:n_layers], refs[n_layers:]
        bk = bk_ref[...]
        for h in range(HEADS):
            tot = jnp.zeros((CHUNK, CHUNK), F32)
            for l in range(n_layers):
                ds = db_refs[l][h]
                dsink_ref[l, h] = -jnp.sum(ds)
                tot = tot + ds
            for b in range(REL_BUCKETS):
                out_ref[b, h] = jnp.sum(jnp.where(bk == b, tot, 0.0))

    vmem = pl.BlockSpec(memory_space=pltpu.VMEM)
    return pl.pallas_call(
        body, name="rel_bias_grad",
        out_shape=(jax.ShapeDtypeStruct((REL_BUCKETS, HEADS), F32), jax.ShapeDtypeStruct((n_layers, HEADS), F32)),
        in_specs=[vmem] * (n_layers + 1), out_specs=(SMEM, SMEM),
    )(*dbias, bucket)


def _spatial_tables(w_spatial, b_spatial_t):
    n_layers = w_spatial.shape[0]

    def body(w_ref, b_ref, wcat_ref, wtcat_ref, bs_ref):
        row = lax.broadcasted_iota(jnp.int32, (CHUNK, CHUNK), 0)
        col = lax.broadcasted_iota(jnp.int32, (CHUNK, CHUNK), 1)
        causal = col <= row
        lo = _lo_mask((CHUNK, CHUNK))
        for p in range(4):
            for half in range(2):
                w = jnp.where(causal, w_ref[0, 2 * p + half], 0.0)
                wcat_ref[0, p, :, half * CHUNK:(half + 1) * CHUNK] = w.astype(BF16)
                wtcat_ref[0, p, :, half * CHUNK:(half + 1) * CHUNK] = w.T.astype(BF16)
            b = b_ref[0]
            bs_ref[0, p] = jnp.where(lo, b[:, 2 * p:2 * p + 1], b[:, 2 * p + 1:2 * p + 2])

    return pl.pallas_call(
        body, name="spatial_tables", grid=(n_layers,),
        out_shape=(jax.ShapeDtypeStruct((n_layers, 4, CHUNK, 2 * CHUNK), BF16),
                   jax.ShapeDtypeStruct((n_layers, 4, CHUNK, 2 * CHUNK), BF16),
                   jax.ShapeDtypeStruct((n_layers, 4, CHUNK, CHUNK), F32)),
        in_specs=[pl.BlockSpec((1, HEADS, CHUNK, CHUNK), lambda l: (l, 0, 0, 0)),
                  pl.BlockSpec((1, CHUNK, HEADS), lambda l: (l, 0, 0))],
        out_specs=(pl.BlockSpec((1, 4, CHUNK, 2 * CHUNK), lambda l: (l, 0, 0, 0)),
                   pl.BlockSpec((1, 4, CHUNK, 2 * CHUNK), lambda l: (l, 0, 0, 0)),
                   pl.BlockSpec((1, 4, CHUNK, CHUNK), lambda l: (l, 0, 0, 0))),
        compiler_params=_cparams(1),
    )(w_spatial, b_spatial_t)


def _b_spatial_grad(dmix):
    n_layers = len(dmix)

    def body(*refs):
        d_refs, out_ref = refs[:n_layers], refs[n_layers]
        lane = lax.broadcasted_iota(jnp.int32, (CHUNK, CHUNK), 1)
        for l in range(n_layers):
            acc = jnp.zeros((CHUNK, CHUNK), F32)
            for p in range(4):
                t = d_refs[l][:, p * CHUNK:(p + 1) * CHUNK]
                s_lo = jnp.sum(jnp.where(lane < HEAD_DIM, t, 0.0), axis=1, keepdims=True)
                s_hi = jnp.sum(jnp.where(lane < HEAD_DIM, 0.0, t), axis=1, keepdims=True)
                acc = jnp.where(lane == 2 * p, s_lo, acc)
                acc = jnp.where(lane == 2 * p + 1, s_hi, acc)
            out_ref[l] = acc

    vmem = pl.BlockSpec(memory_space=pltpu.VMEM)
    return pl.pallas_call(
        body, name="b_spatial_grad",
        out_shape=jax.ShapeDtypeStruct((n_layers, CHUNK, CHUNK), F32),
        in_specs=[vmem] * n_layers, out_specs=vmem,
    )(*dmix)


def _place():
    x, y, c = lax.axis_index("x"), lax.axis_index("y"), lax.axis_index("c")
    other_chips = [(1 - x, y), (x, 1 - y), (1 - x, 1 - y)]
    return x, y, c, other_chips


N_GATHER_SEMS = 12


def _gather_phases(shards, fulls, send_sems, recv_sems, local_sems):
    x, y, c, chips = _place()
    sibling = (x, y, 1 - c)
    n_arr = len(shards)

    def half_rows(a, chip, half):
        n = shards[a].shape[0]
        start = (2 * chip[0] + chip[1]) * n + half * (n // 2)
        return fulls[a].at[pl.ds(pl.multiple_of(start, 16), n // 2), :]

    def my_half(a):
        n = shards[a].shape[0]
        return shards[a].at[pl.ds(pl.multiple_of(c * (n // 2), 16), n // 2), :]

    def copy(k, a, src, chip, half, to):
        return pltpu.make_async_remote_copy(
            src_ref=src, dst_ref=half_rows(a, chip, half), send_sem=send_sems.at[n_arr * k + a],
            recv_sem=recv_sems.at[n_arr * k + a], device_id=to, device_id_type=MESH_ID)

    def local(a):
        n = shards[a].shape[0]
        mine = fulls[a].at[pl.ds(pl.multiple_of((2 * x + y) * n, 16), n), :]
        return pltpu.make_async_copy(shards[a], mine, local_sems.at[a])

    def first(k, a):
        return copy(k, a, my_half(a), (x, y), c, (chips[k][0], chips[k][1], c))

    def passed(k, a):
        return copy(3 + k, a, half_rows(a, chips[k], c), chips[k], c, sibling)

    def phase_a():
        for a in range(n_arr):
            local(a).start()
        for k in range(3):
            for a in range(n_arr):
                first(k, a).start()

    def phase_b():
        for k in range(3):
            for a in range(n_arr):
                copy(k, a, my_half(a), chips[k], c, sibling).wait_recv()
                passed(k, a).start()

    def phase_c():
        for k in range(3):
            for a in range(n_arr):
                copy(3 + k, a, my_half(a), chips[k], 1 - c, sibling).wait_recv()
        for k in range(3):
            for a in range(n_arr):
                first(k, a).wait_send()
                passed(k, a).wait_send()
        for a in range(n_arr):
            local(a).wait()

    return phase_a, phase_b, phase_c


N_EXCHANGE_SEMS = 14


def _exchange_phases(partials, parts, send_sems, recv_sems, local_sems, by_chip=True):
    x, y, c, chips = _place()
    me, sibling = (x, y, c), (x, y, 1 - c)
    n_arr = len(partials)

    def block(a, chip):
        if not by_chip:
            return partials[a]
        n = partials[a].shape[0] // N_CHIPS
        return partials[a].at[pl.ds(pl.multiple_of((2 * chip[0] + chip[1]) * n, 16), n), :]

    def slot(a, dev):
        return parts[a].at[4 * dev[0] + 2 * dev[1] + dev[2]]

    def copy(k, a, src, origin, to):
        return pltpu.make_async_remote_copy(
            src_ref=src, dst_ref=slot(a, origin), send_sem=send_sems.at[n_arr * k + a],
            recv_sem=recv_sems.at[n_arr * k + a], device_id=to, device_id_type=MESH_ID)

    def local(a):
        return pltpu.make_async_copy(block(a, (x, y)), slot(a, me), local_sems.at[a])

    def first(k, a):
        if k == 0:
            return copy(0, a, block(a, (x, y)), me, sibling)
        chip = chips[k - 1]
        return copy(k, a, block(a, chip), me, (chip[0], chip[1], c))

    def passed(k, a):
        origin = (chips[k][0], chips[k][1], c)
        return copy(4 + k, a, slot(a, origin), origin, sibling)

    def phase_a():
        for a in range(n_arr):
            local(a).start()
        for k in range(4):
            for a in range(n_arr):
                first(k, a).start()

    def phase_b():
        for k in range(3):
            for a in range(n_arr):
                copy(1 + k, a, block(a, (x, y)), (chips[k][0], chips[k][1], c), me).wait_recv()
                passed(k, a).start()

    def phase_c():
        for a in range(n_arr):
            copy(0, a, block(a, (x, y)), sibling, me).wait_recv()
        for k in range(3):
            for a in range(n_arr):
                copy(4 + k, a, block(a, (x, y)), (chips[k][0], chips[k][1], 1 - c), me).wait_recv()
        for k in range(4):
            for a in range(n_arr):
                first(k, a).wait_send()
        for k in range(3):
            for a in range(n_arr):
                passed(k, a).wait_send()
        for a in range(n_arr):
            local(a).wait()

    return phase_a, phase_b, phase_c


def _comm_scratch(n_sems):
    return [pltpu.SemaphoreType.DMA((n_sems,)), pltpu.SemaphoreType.DMA((n_sems,)), pltpu.SemaphoreType.DMA((2,))]


def _gather_weights(wt_shards, wo_shards, layer):
    wt_rows, wo_rows = wt_shards.shape[1], wo_shards.shape[1]

    def body(wt_ref, wo_ref, wt_full, wo_full, send_sems, recv_sems, local_sems):
        phases = _gather_phases((wt_ref.at[layer], wo_ref.at[layer]), (wt_full, wo_full), send_sems, recv_sems, local_sems)
        for phase in phases:
            phase()

    return pl.pallas_call(
        body, name="gather_weights",
        out_shape=(jax.ShapeDtypeStruct((N_CHIPS * wt_rows, D_MODEL), BF16),
                   jax.ShapeDtypeStruct((N_CHIPS * wo_rows, D_MODEL), BF16)),
        in_specs=[ANY, ANY], out_specs=(ANY, ANY), scratch_shapes=_comm_scratch(N_GATHER_SEMS),
    )(wt_shards, wo_shards)


def _allreduce_small(part):
    n_rows = part.shape[0]

    def body(p_ref, tot_ref, all_ref, send_sems, recv_sems, local_sem):
        x, y, c, chips = _place()
        me, sibling = (x, y, c), (x, y, 1 - c)

        def rows(dev):
            return all_ref.at[pl.ds(pl.multiple_of((4 * dev[0] + 2 * dev[1] + dev[2]) * n_rows, 8), n_rows), :]

        def copy(k, origin, to, src=None):
            return pltpu.make_async_remote_copy(
                src_ref=rows(origin) if src is None else src, dst_ref=rows(origin), send_sem=send_sems.at[k],
                recv_sem=recv_sems.at[k], device_id=to, device_id_type=MESH_ID)

        mine = pltpu.make_async_copy(p_ref, rows(me), local_sem)
        mine.start()
        first = [copy(0, me, sibling, src=p_ref)]
        first += [copy(1 + k, me, (chip[0], chip[1], c), src=p_ref) for k, chip in enumerate(chips)]
        for cp in first:
            cp.start()
        passed = []
        for k, chip in enumerate(chips):
            origin = (chip[0], chip[1], c)
            copy(1 + k, origin, me).wait_recv()
            fwd = copy(4 + k, origin, sibling)
            fwd.start()
            passed.append(fwd)
        copy(0, sibling, me).wait_recv()
        for k, chip in enumerate(chips):
            copy(4 + k, (chip[0], chip[1], 1 - c), me).wait_recv()
        for cp in first + passed:
            cp.wait_send()
        mine.wait()
        tot = all_ref[0:n_rows, :]
        for d in range(1, N_DEV):
            tot = tot + all_ref[d * n_rows:(d + 1) * n_rows, :]
        tot_ref[...] = tot

    vmem = pl.BlockSpec(memory_space=pltpu.VMEM)
    return pl.pallas_call(
        body, name="allreduce_small",
        out_shape=jax.ShapeDtypeStruct((n_rows, 128), F32),
        in_specs=[vmem], out_specs=vmem,
        scratch_shapes=[pltpu.VMEM((N_DEV * n_rows, 128), F32), pltpu.SemaphoreType.DMA((7,)),
                        pltpu.SemaphoreType.DMA((7,)), pltpu.SemaphoreType.DMA],
        compiler_params=pltpu.CompilerParams(vmem_limit_bytes=VMEM_LIMIT_V7X),
    )(part)


def _hosted(phases, step, n_steps, pass_on_step=None):
    phase_a, phase_b, phase_c = phases
    if pass_on_step is None:
        pass_on_step = (3 * n_steps) // 4

    def at_start():
        pl.when(step == 0)(phase_a)

    def at_end():
        pl.when(step == pass_on_step)(phase_b)
        pl.when(step == n_steps - 1)(phase_c)

    return at_start, at_end


def _layer_norm_stats(vv):
    mu = jnp.mean(vv, axis=-1, keepdims=True)
    xc = vv - mu
    rs = lax.rsqrt(jnp.mean(xc * xc, axis=-1, keepdims=True) + NORM_EPS)
    return xc * rs, mu, rs


STAT_POST_RMS, STAT_LN_MEAN, STAT_LN_RSTD = (slice(k * CHUNK, (k + 1) * CHUNK) for k in range(3))
N_STATS = 3


def _over_lanes(stat, width):
    return jnp.concatenate([stat] * (width // CHUNK), axis=1)


def _blockdiag(v, lo):
    zero = jnp.zeros_like(v)
    return jnp.concatenate([jnp.where(lo, v, zero), jnp.where(lo, zero, v)], axis=0)


def _softmax_sink(s, sink):
    m = jnp.maximum(jnp.max(s, axis=-1, keepdims=True), sink)
    e = jnp.exp(s - m)
    esink = jnp.exp(sink - m)
    den = jnp.sum(e, axis=-1, keepdims=True) + esink
    return e / den


def _kv_rows(cur_ref, halo_ref, r0, c):
    prev_in_tile = cur_ref[pl.ds(pl.multiple_of(jnp.maximum(r0 - CHUNK, 0), CHUNK), CHUNK), :]
    prev = jnp.where(c == 0, halo_ref[...], prev_in_tile)
    kv2 = jnp.concatenate([prev, cur_ref[pl.ds(r0, CHUNK), :]], axis=0)
    k2, v2 = kv2[:, 0:KV_WIDTH], kv2[:, KV_WIDTH:2 * KV_WIDTH]
    return (k2, _swap_halves(k2)), (v2, _swap_halves(v2))


def _window_square(over_keys, prev):
    return jnp.where(prev, over_keys[:, 0:CHUNK], over_keys[:, CHUNK:2 * CHUNK])


def _window_keys(square, prev):
    zero = jnp.zeros_like(square)
    return jnp.concatenate([jnp.where(prev, square, zero), jnp.where(prev, zero, square)], axis=1)


def _ahead(shape=(CHUNK, CHUNK)):
    return lax.broadcasted_iota(jnp.int32, shape, 1) - lax.broadcasted_iota(jnp.int32, shape, 0)


def _dead_mask(first, ahead):
    return ahead > jnp.where(first, 0, CHUNK)


def _head_of(p, half):
    return 2 * p + half, int(half != p // 2)


_HEADS_BY_COPY = tuple(tuple((p, half) for p in range(4) for half in range(2) if _head_of(p, half)[1] == sw)
                       for sw in range(2))


def _masked_halves(tile, lo):
    zero = jnp.zeros_like(tile)
    return {0: jnp.where(lo, tile, zero), 1: jnp.where(lo, zero, tile)}


def _query_stacks(q_tiles, lo):
    qm = {p: _masked_halves(q_tiles[p] * QK_SCALE, lo) for p in range(4)}
    return [jnp.concatenate([qm[p][half] for p, half in members], axis=0) for members in _HEADS_BY_COPY]


def _attention_probs(q_stacks, ks, bias_ref, sink_ref, dead, prev):
    probs = {}
    for sw, members in enumerate(_HEADS_BY_COPY):
        s_stack = _dot_nt(q_stacks[sw], ks[sw])
        for i, (p, half) in enumerate(members):
            head = 2 * p + half
            s = _window_square(s_stack[i * CHUNK:(i + 1) * CHUNK, :], prev) + bias_ref[head]
            s = jnp.where(dead, NEG, s)
            probs[p, half] = _softmax_sink(s, sink_ref[head]).astype(BF16)
    return probs


def _attention_values(probs, vs, prev):
    outs, p_stacks = {}, []
    for sw, members in enumerate(_HEADS_BY_COPY):
        p_stack = jnp.concatenate([_window_keys(probs[m], prev) for m in members], axis=0)
        r_stack = _dot(p_stack, vs[sw])
        p_stacks.append(p_stack)
        for i, m in enumerate(members):
            outs[m] = r_stack[i * CHUNK:(i + 1) * CHUNK, :]
    return outs, p_stacks


def _fwd_layer(x, g_pre, w_in_t, ln_g, ln_b, wcat, bs, sinks, bias, w_out, g_post, seq_len, gather=None, target=None):
    n_tok = x.shape[0]
    tb = TOK_TILE
    n_chunks = tb // CHUNK
    n_tiles = n_tok // tb
    n_in = 11
    assert gather is None or target is None

    def body(*refs):
        (x_ref, gpre_ref, wt_ref, lng_ref, lnb_ref, wcat_ref, bs_ref, sink_ref, bias_ref, wout_ref,
         gpost_ref) = refs[:n_in]
        at_start = at_end = lambda: None
        if gather is not None:
            (wts_ref, wos_ref, a_ref, q_ref, kv_ref, bz_ref, probs_ref, stats_ref, mixed_ref, act_ref, att_ref, xn_ref, y_ref,
             wt_full, wo_full, ycat_s, halo_ref, send_sems, recv_sems, local_sems) = refs[n_in:]
            phases = _gather_phases((wts_ref.at[gather[2]], wos_ref.at[gather[2]]), (wt_full, wo_full), send_sems,
                                    recv_sems, local_sems)
            at_start, at_end = _hosted(phases, pl.program_id(0), n_tiles)
        elif target is not None:
            (target_ref, a_ref, q_ref, kv_ref, bz_ref, probs_ref, stats_ref, mixed_ref, act_ref, att_ref, xn_ref, y_ref,
             loss_ref, ycat_s, halo_ref) = refs[n_in:]
        else:
            (a_ref, q_ref, kv_ref, bz_ref, probs_ref, stats_ref, mixed_ref, act_ref, att_ref, xn_ref, y_ref, ycat_s,
             halo_ref) = refs[n_in:]
        at_start()
        i = pl.program_id(0)
        lo = _lo_mask((CHUNK, CHUNK))
        ahead = _ahead()
        prev = ahead > 0

        @pl.when(i == 0)
        def _():
            halo_ref[...] = jnp.zeros_like(halo_ref)

        xf = x_ref[...]
        r1 = lax.rsqrt(jnp.mean(xf * xf, axis=-1, keepdims=True) + NORM_EPS)
        h = ((xf * r1) * gpre_ref[...]).astype(BF16)
        a_ref[...] = _dot_nt(h, wt_ref[0:Q_OFF, :])
        q_ref[...] = _dot_nt(h, wt_ref[Q_OFF:K_OFF, :]).astype(BF16)
        kv_ref[...] = _dot_nt(h, wt_ref[K_OFF:BZ_OFF, :]).astype(BF16)
        bz_ref[...] = _dot_nt(h, wt_ref[BZ_OFF:IN_WIDTH, :])

        def chunk(c, carry):
            r0 = pl.multiple_of(c * CHUNK, CHUNK)
            rows = pl.ds(r0, CHUNK)
            u, gu = _gelu_and_grad(a_ref[rows, 0:A_WIDTH])
            vv, gv = _gelu_and_grad(a_ref[rows, A_WIDTH:2 * A_WIDTH])
            for k, act in enumerate((u, gu, vv, gv)):
                act_ref[rows, k * A_WIDTH:(k + 1) * A_WIDTH] = act
            xhat, mu, rs = _layer_norm_stats(vv)
            stats_ref[rows, STAT_LN_MEAN] = jnp.broadcast_to(mu, (CHUNK, CHUNK))
            stats_ref[rows, STAT_LN_RSTD] = jnp.broadcast_to(rs, (CHUNK, CHUNK))
            vnb = (xhat * lng_ref[...] + lnb_ref[...]).astype(BF16)
            for p in range(4):
                blk = slice(p * CHUNK, (p + 1) * CHUNK)
                mixed = _dot(wcat_ref[p], _blockdiag(vnb[:, blk], lo)) + bs_ref[p]
                mixed_ref[rows, blk] = mixed
                sz = _silu(a_ref[rows, 2 * A_WIDTH + p * CHUNK:2 * A_WIDTH + (p + 1) * CHUNK])
                ycat_s[rows, blk] = ((u[:, blk] * mixed) * sz).astype(BF16)
            ks, vs = _kv_rows(kv_ref, halo_ref, r0, c)
            dead = _dead_mask(lax.rem(i * tb + r0, seq_len) == 0, ahead)
            q_tiles = [q_ref[rows, p * CHUNK:(p + 1) * CHUNK] for p in range(4)]
            probs = _attention_probs(_query_stacks(q_tiles, lo), ks, bias_ref, sink_ref, dead, prev)
            for (p, half), prob in probs.items():
                head = 2 * p + half
                probs_ref[rows, head * CHUNK:(head + 1) * CHUNK] = prob
            outs, _ = _attention_values(probs, vs, prev)
            for p in range(4):
                blk = slice(p * CHUNK, (p + 1) * CHUNK)
                o = jnp.where(lo, outs[p, 0], outs[p, 1])
                att_ref[rows, blk] = o
                ycat_s[rows, B_WIDTH + p * CHUNK:B_WIDTH + (p + 1) * CHUNK] = (o * _silu(bz_ref[rows, blk])).astype(BF16)
            return carry

        lax.fori_loop(0, n_chunks, chunk, 0, unroll=CHUNK_UNROLL)
        halo_ref[...] = kv_ref[tb - CHUNK:tb, :]
        y = _dot(ycat_s[...], wout_ref[...])
        r = lax.rsqrt(jnp.mean(y * y, axis=-1, keepdims=True) + NORM_EPS)
        yn = y * r
        y_ref[...] = yn
        stats_ref[:, STAT_POST_RMS] = jnp.broadcast_to(r, (tb, CHUNK))
        xn = x_ref[...] + yn * gpost_ref[...]
        if target is None:
            xn_ref[...] = xn
        else:
            d = xn - target_ref[...]
            xn_ref[...] = d * (1.0 / D_MODEL)

            @pl.when(i == 0)
            def _():
                loss_ref[0, 0] = 0.0

            loss_ref[0, 0] += 0.5 * jnp.sum(jnp.mean(d * d, axis=-1, keepdims=True))
        at_end()

    tile = lambda w: pl.BlockSpec((tb, w), lambda i: (i, 0))
    whole = lambda shape, **kw: pl.BlockSpec(shape, lambda i: (0,) * len(shape), **kw)
    in_specs = [tile(D_MODEL), whole((1, D_MODEL)), whole((IN_WIDTH, D_MODEL), pipeline_mode=pl.Buffered(1)),
                whole((1, A_WIDTH)), whole((1, A_WIDTH)), whole((4, CHUNK, 2 * CHUNK)), whole((4, CHUNK, CHUNK)), SMEM,
                whole((HEADS, CHUNK, CHUNK)), whole((D_MODEL, D_MODEL)), whole((1, D_MODEL))]
    out_shape = [jax.ShapeDtypeStruct((n_tok, Q_OFF), F32), jax.ShapeDtypeStruct((n_tok, B_WIDTH), BF16),
                 jax.ShapeDtypeStruct((n_tok, 2 * KV_WIDTH), BF16), jax.ShapeDtypeStruct((n_tok, B_WIDTH), F32),
                 jax.ShapeDtypeStruct((n_tok, HEADS * CHUNK), BF16), jax.ShapeDtypeStruct((n_tok, N_STATS * CHUNK), F32),
                 jax.ShapeDtypeStruct((n_tok, A_WIDTH), F32), jax.ShapeDtypeStruct((n_tok, 4 * A_WIDTH), F32),
                 jax.ShapeDtypeStruct((n_tok, B_WIDTH), F32),
                 jax.ShapeDtypeStruct((n_tok, D_MODEL), F32), jax.ShapeDtypeStruct((n_tok, D_MODEL), F32)]
    out_specs = [tile(Q_OFF), tile(B_WIDTH), tile(2 * KV_WIDTH), tile(B_WIDTH), tile(HEADS * CHUNK), tile(N_STATS * CHUNK),
                 tile(A_WIDTH), tile(4 * A_WIDTH), tile(B_WIDTH), tile(D_MODEL), tile(D_MODEL)]
    scratch = [pltpu.VMEM((tb, D_MODEL), BF16), pltpu.VMEM((CHUNK, 2 * KV_WIDTH), BF16)]
    args = [x, g_pre, w_in_t, ln_g, ln_b, wcat, bs, sinks, bias, w_out, g_post]
    if gather is not None:
        wt_shards, wo_shards, _ = gather
        in_specs += [ANY, ANY]
        args += [wt_shards, wo_shards]
        out_shape += [jax.ShapeDtypeStruct((N_CHIPS * wt_shards.shape[1], D_MODEL), BF16),
                      jax.ShapeDtypeStruct((N_CHIPS * wo_shards.shape[1], D_MODEL), BF16)]
        out_specs += [ANY, ANY]
        scratch += _comm_scratch(N_GATHER_SEMS)
    if target is not None:
        in_specs.append(tile(D_MODEL))
        args.append(target)
        out_shape.append(jax.ShapeDtypeStruct((1, 1), F32))
        out_specs.append(SMEM)
    name = "fwd_layer" + ("" if gather is None else "_gather") + ("" if target is None else "_loss")
    return pl.pallas_call(
        body, name=name, grid=(n_tiles,),
        out_shape=tuple(out_shape), in_specs=in_specs, out_specs=tuple(out_specs), scratch_shapes=scratch,
        compiler_params=_cparams(1),
    )(*args)


def _bwd_mix(dout, y, a, q, kv, bz, probs, stats, mixed, act, att, ln_g, ln_b, wtcat, w_out, g_post, exchange=None,
             allgather=None):
    n_tok = y.shape[0]
    tb = TOK_TILE
    n_chunks = tb // CHUNK
    n_tiles = n_tok // tb

    def body(*refs):
        refs = list(refs)
        take = lambda n: [refs.pop(0) for _ in range(n)]
        (dout_ref, y_ref, az_ref, q_ref, kv_ref, halo_ref, bz_ref, probs_ref, stats_ref, mixed_ref, act_ref, att_ref,
         lng_ref, lnb_ref, wtcat_ref, wout_ref, gpost_ref) = take(17)
        if exchange is not None:
            dwt_in, dwo_in = take(2)
        if allgather is not None:
            (small_in,) = take(1)
        dproj_ref, dwout_ref, dws_ref, dmix_ref, dlng_ref, dlnb_ref, dgpost_ref, dbias_ref = take(8)
        if exchange is not None:
            pt_ref, po_ref = take(2)
        if allgather is not None:
            (small_all,) = take(1)
        ycat_s, dy_s, dyc_s, dkv_s, carry_s, dwout_s, stage_s, dws_s = take(8)
        starts, ends = [], []
        if exchange is not None:
            hosted = _hosted(_exchange_phases((dwt_in, dwo_in), (pt_ref, po_ref), *take(3)), pl.program_id(0), n_tiles)
            starts.append(hosted[0])
            ends.append(hosted[1])
        if allgather is not None:
            hosted = _hosted(_exchange_phases((small_in,), (small_all,), *take(3), by_chip=False), pl.program_id(0),
                             n_tiles)
            starts.append(hosted[0])
            ends.append(hosted[1])
        at_start = lambda: [f() for f in starts]
        at_end = lambda: [f() for f in ends]
        at_start()
        step = pl.program_id(0)
        lo = _lo_mask((CHUNK, CHUNK))
        prev = _ahead() > 0

        @pl.when(step == 0)
        def _():
            dwout_s[...] = jnp.zeros_like(dwout_s)
            dws_s[...] = jnp.zeros_like(dws_s)
            dmix_ref[...] = jnp.zeros_like(dmix_ref)
            dlng_ref[...] = jnp.zeros_like(dlng_ref)
            dlnb_ref[...] = jnp.zeros_like(dlnb_ref)
            dgpost_ref[...] = jnp.zeros_like(dgpost_ref)
            dbias_ref[...] = jnp.zeros_like(dbias_ref)
            carry_s[...] = jnp.zeros_like(carry_s)

        for r0 in range(0, tb, CHUNK):
            rows = slice(r0, r0 + CHUNK)
            yn = y_ref[rows, :]
            dout = dout_ref[rows, :]
            r = _over_lanes(stats_ref[rows, STAT_POST_RMS], D_MODEL)
            dgpost_ref[...] += jnp.sum(dout * yn, axis=0, keepdims=True)
            dyn = dout * gpost_ref[...]
            dy_s[rows, :] = (r * (dyn - yn * jnp.mean(dyn * yn, axis=-1, keepdims=True))).astype(BF16)
        dyc_s[...] = _dot_nt(dy_s[...], wout_ref[...])
        dkv_s[0:tb, :] = jnp.zeros((tb, 2 * KV_WIDTH), F32)
        dkv_s[tb:tb + CHUNK, :] = carry_s[...]

        def chunk(c, carry):
            r0 = pl.multiple_of(c * CHUNK, CHUNK)
            rows = pl.ds(r0, CHUNK)
            u, gu, vv, gv = (act_ref[rows, k * A_WIDTH:(k + 1) * A_WIDTH] for k in range(4))
            rs = _over_lanes(stats_ref[rows, STAT_LN_RSTD], A_WIDTH)
            xhat = (vv - _over_lanes(stats_ref[rows, STAT_LN_MEAN], A_WIDTH)) * rs
            vnb = (xhat * lng_ref[...] + lnb_ref[...]).astype(BF16)
            d_vn, d_u, d_az = [], [], []
            for p in range(4):
                blk = slice(p * CHUNK, (p + 1) * CHUNK)
                mixed = mixed_ref[rows, blk]
                sz, gz = _silu_and_grad(az_ref[rows, blk])
                ub = u[:, blk]
                dya = dyc_s[rows, blk]
                um = ub * mixed
                ycat_s[rows, blk] = (um * sz).astype(BF16)
                d_mixed = (dya * ub) * sz
                d_u.append((dya * mixed) * sz)
                d_az.append((dya * um) * gz)
                dmix_ref[:, blk] += d_mixed
                dmbd = _blockdiag(d_mixed.astype(BF16), lo)
                d_vn.append(_dot(wtcat_ref[p], dmbd))
                dws_s[p] += _dot_nt(dmbd, vnb[:, blk])
            d_vn = jnp.concatenate(d_vn, axis=1)
            dlng_ref[...] += jnp.sum(d_vn * xhat, axis=0, keepdims=True)
            dlnb_ref[...] += jnp.sum(d_vn, axis=0, keepdims=True)
            dxh = d_vn * lng_ref[...]
            d_vv = rs * (dxh - jnp.mean(dxh, axis=-1, keepdims=True)
                         - xhat * jnp.mean(dxh * xhat, axis=-1, keepdims=True))
            dproj_ref[rows, 0:A_WIDTH] = (jnp.concatenate(d_u, axis=1) * gu).astype(BF16)
            dproj_ref[rows, A_WIDTH:2 * A_WIDTH] = (d_vv * gv).astype(BF16)
            dproj_ref[rows, 2 * A_WIDTH:Q_OFF] = jnp.concatenate(d_az, axis=1).astype(BF16)
            ks, vs = _kv_rows(kv_ref, halo_ref, r0, c)
            q_stacks = _query_stacks([q_ref[rows, p * CHUNK:(p + 1) * CHUNK] for p in range(4)], lo)
            probs = {(p, half): probs_ref[rows, (2 * p + half) * CHUNK:(2 * p + half + 1) * CHUNK]
                     for p in range(4) for half in range(2)}
            p_stacks = [jnp.concatenate([_window_keys(probs[m], prev) for m in members], axis=0)
                        for members in _HEADS_BY_COPY]
            dom = {}
            for p in range(4):
                blk = slice(p * CHUNK, (p + 1) * CHUNK)
                sb, gb = _silu_and_grad(bz_ref[rows, blk])
                dyb = dyc_s[rows, B_WIDTH + p * CHUNK:B_WIDTH + (p + 1) * CHUNK]
                o = att_ref[rows, blk]
                ycat_s[rows, B_WIDTH + p * CHUNK:B_WIDTH + (p + 1) * CHUNK] = (o * sb).astype(BF16)
                dproj_ref[rows, BZ_OFF + p * CHUNK:BZ_OFF + (p + 1) * CHUNK] = ((dyb * o) * gb).astype(BF16)
                dom[p] = _masked_halves((dyb * sb).astype(BF16), lo)
            dqs, dk_by_copy, dv_by_copy = {}, [], []
            for sw, members in enumerate(_HEADS_BY_COPY):
                do_stack = jnp.concatenate([dom[p][half] for p, half in members], axis=0)
                dp_stack = _dot_nt(do_stack, vs[sw])
                dsws = []
                for i, (p, half) in enumerate(members):
                    head = 2 * p + half
                    dp = _window_square(dp_stack[i * CHUNK:(i + 1) * CHUNK, :], prev)
                    prob = probs[p, half].astype(F32)
                    delta = jnp.sum(prob * dp, axis=-1, keepdims=True)
                    ds = prob * (dp - delta)
                    dbias_ref[head] += ds
                    dsws.append(_window_keys(ds.astype(BF16), prev))
                ds_stack = jnp.concatenate(dsws, axis=0)
                dq_stack = _dot(ds_stack, ks[sw])
                for i, m in enumerate(members):
                    dqs[m] = dq_stack[i * CHUNK:(i + 1) * CHUNK, :]
                dk_by_copy.append(_dot_tn(ds_stack, q_stacks[sw]))
                dv_by_copy.append(_dot_tn(p_stacks[sw], do_stack))
            for p in range(4):
                dq = jnp.where(lo, dqs[p, 0], dqs[p, 1]) * QK_SCALE
                dproj_ref[rows, Q_OFF + p * CHUNK:Q_OFF + (p + 1) * CHUNK] = dq.astype(BF16)
            both = pl.ds(r0, 2 * CHUNK)
            dkv_s[both, 0:KV_WIDTH] += dk_by_copy[0] + _swap_halves(dk_by_copy[1])
            dkv_s[both, KV_WIDTH:2 * KV_WIDTH] += dv_by_copy[0] + _swap_halves(dv_by_copy[1])
            return carry

        lax.fori_loop(0, n_chunks, chunk, 0, unroll=BWD_CHUNK_UNROLL)
        dwout_s[...] += _dot_tn(ycat_s[...], dy_s[...])
        dproj_ref[:, K_OFF:BZ_OFF] = dkv_s[CHUNK:CHUNK + tb, :].astype(BF16)
        carry_s[...] = dkv_s[0:CHUNK, :]

        @pl.when(step == n_tiles - 1)
        def _():
            row = lax.broadcasted_iota(jnp.int32, (2 * CHUNK, CHUNK), 0)
            col = lax.broadcasted_iota(jnp.int32, (2 * CHUNK, CHUNK), 1)
            causal = col <= jnp.where(row >= CHUNK, row - CHUNK, row)
            for p in range(4):
                dws_ref[p] = jnp.where(causal, dws_s[p], 0.0).astype(BF16)
            rows = stage_s.shape[0]
            for r0 in range(0, D_MODEL, rows):
                stage_s[...] = dwout_s[r0:r0 + rows, :].astype(BF16)
                pltpu.sync_copy(stage_s, dwout_ref.at[r0:r0 + rows, :])

        at_end()

    tile = lambda w: pl.BlockSpec((tb, w), lambda s: (n_tiles - 1 - s, 0))
    whole = lambda shape, **kw: pl.BlockSpec(shape, lambda s: (0,) * len(shape), **kw)
    once = dict(pipeline_mode=pl.Buffered(1))
    gate_cols = pl.BlockSpec((tb, A_WIDTH), lambda s: (n_tiles - 1 - s, 2))
    in_specs = [tile(D_MODEL), tile(D_MODEL), gate_cols, tile(B_WIDTH), tile(2 * KV_WIDTH),
                pl.BlockSpec((CHUNK, 2 * KV_WIDTH), lambda s: (jnp.maximum((n_tiles - 1 - s) * n_chunks - 1, 0), 0)),
                tile(B_WIDTH), tile(HEADS * CHUNK), tile(N_STATS * CHUNK), tile(A_WIDTH), tile(4 * A_WIDTH), tile(B_WIDTH),
                whole((1, A_WIDTH)), whole((1, A_WIDTH)), whole((4, CHUNK, 2 * CHUNK)), whole((D_MODEL, D_MODEL), **once),
                whole((1, D_MODEL))]
    args = [dout, y, a, q, kv, kv, bz, probs, stats, mixed, act, att, ln_g, ln_b, wtcat, w_out, g_post]
    small_shapes = [jax.ShapeDtypeStruct((D_MODEL, D_MODEL), BF16), jax.ShapeDtypeStruct((4, 2 * CHUNK, CHUNK), BF16),
                    jax.ShapeDtypeStruct((CHUNK, A_WIDTH), F32), jax.ShapeDtypeStruct((1, A_WIDTH), F32),
                    jax.ShapeDtypeStruct((1, A_WIDTH), F32), jax.ShapeDtypeStruct((1, D_MODEL), F32),
                    jax.ShapeDtypeStruct((HEADS, CHUNK, CHUNK), F32)]
    small_specs = [ANY, whole((4, 2 * CHUNK, CHUNK)), whole((CHUNK, A_WIDTH)), whole((1, A_WIDTH)), whole((1, A_WIDTH)),
                   whole((1, D_MODEL)), whole((HEADS, CHUNK, CHUNK))]
    scratch = [pltpu.VMEM((tb, D_MODEL), BF16), pltpu.VMEM((tb, D_MODEL), BF16), pltpu.VMEM((tb, D_MODEL), F32),
               pltpu.VMEM((tb + CHUNK, 2 * KV_WIDTH), F32), pltpu.VMEM((CHUNK, 2 * KV_WIDTH), F32),
               pltpu.VMEM((D_MODEL, D_MODEL), F32), pltpu.VMEM((2 * CHUNK, D_MODEL), BF16),
               pltpu.VMEM((4, 2 * CHUNK, CHUNK), F32)]
    out_shape = [jax.ShapeDtypeStruct((n_tok, IN_WIDTH), BF16)] + small_shapes
    out_specs = [tile(IN_WIDTH)] + small_specs
    if exchange is not None:
        dwt, dwo = exchange
        in_specs += [ANY, ANY]
        args += [dwt, dwo]
        out_shape += [jax.ShapeDtypeStruct((N_DEV, dwt.shape[0] // N_CHIPS, D_MODEL), BF16),
                      jax.ShapeDtypeStruct((N_DEV, dwo.shape[0] // N_CHIPS, D_MODEL), BF16)]
        out_specs += [ANY, ANY]
    if allgather is not None:
        in_specs.append(ANY)
        args.append(allgather)
        out_shape.append(jax.ShapeDtypeStruct((N_DEV,) + allgather.shape, allgather.dtype))
        out_specs.append(ANY)
    for hosted in (exchange, allgather):
        if hosted is not None:
            scratch += _comm_scratch(N_EXCHANGE_SEMS)
    return pl.pallas_call(
        body, name="bwd_mix" + ("" if exchange is None and allgather is None else "_exchange"), grid=(n_tiles,),
        out_shape=tuple(out_shape), in_specs=in_specs, out_specs=tuple(out_specs), scratch_shapes=scratch,
        compiler_params=_cparams(1),
    )(*args)


def _bwd_in(dproj, x, dout, g_pre, w_in_t):
    n_tok = x.shape[0]
    tm = TOK_TILE
    n_tiles = n_tok // tm

    def body(dp_ref, x_ref, dout_ref, g_ref, wt_ref, dx_ref, dwt_ref, dg_ref, acc_s):
        i = pl.program_id(0)

        @pl.when(i == 0)
        def _():
            acc_s[...] = jnp.zeros_like(acc_s)
            dg_ref[...] = jnp.zeros_like(dg_ref)

        xf = x_ref[...]
        r = lax.rsqrt(jnp.mean(xf * xf, axis=-1, keepdims=True) + NORM_EPS)
        xn = xf * r
        h = (xn * g_ref[...]).astype(BF16)
        dp = dp_ref[...]
        dh = _dot(dp, wt_ref[...])
        acc_s[...] += _dot_tn(dp, h)
        dg_ref[...] += jnp.sum(dh * xn, axis=0, keepdims=True)
        dhn = dh * g_ref[...]
        dx_ref[...] = dout_ref[...] + r * (dhn - xn * jnp.mean(dhn * xn, axis=-1, keepdims=True))

        @pl.when(i == n_tiles - 1)
        def _():
            dwt_ref[...] = acc_s[...].astype(BF16)

    return pl.pallas_call(
        body, name="bwd_in", grid=(n_tiles,),
        out_shape=(jax.ShapeDtypeStruct((n_tok, D_MODEL), F32), jax.ShapeDtypeStruct((IN_WIDTH, D_MODEL), BF16),
                   jax.ShapeDtypeStruct((1, D_MODEL), F32)),
        in_specs=[pl.BlockSpec((tm, IN_WIDTH), lambda i: (i, 0)), pl.BlockSpec((tm, D_MODEL), lambda i: (i, 0)),
                  pl.BlockSpec((tm, D_MODEL), lambda i: (i, 0)), pl.BlockSpec((1, D_MODEL), lambda i: (0, 0)),
                  pl.BlockSpec((IN_WIDTH, D_MODEL), lambda i: (0, 0), pipeline_mode=pl.Buffered(1))],
        out_specs=(pl.BlockSpec((tm, D_MODEL), lambda i: (i, 0)),
                   pl.BlockSpec((IN_WIDTH, D_MODEL), lambda i: (0, 0), pipeline_mode=pl.Buffered(1)),
                   pl.BlockSpec((1, D_MODEL), lambda i: (0, 0))),
        scratch_shapes=[pltpu.VMEM((IN_WIDTH, D_MODEL), F32)],
        compiler_params=_cparams(1),
    )(dproj, x, dout, g_pre, w_in_t)


def _riders(exchanges):
    args = [a for arrays, _ in exchanges for a in arrays]
    shapes = [jax.ShapeDtypeStruct((N_DEV, a.shape[0] // N_CHIPS if by_chip else a.shape[0]) + a.shape[1:], a.dtype)
              for arrays, by_chip in exchanges for a in arrays]
    scratch = [s for _ in exchanges for s in _comm_scratch(N_EXCHANGE_SEMS)]

    def bind(in_refs, out_refs, sem_refs, step, n_steps, pass_on_step=None):
        in_refs, out_refs, sem_refs = list(in_refs), list(out_refs), list(sem_refs)
        starts, ends = [], []
        for arrays, by_chip in exchanges:
            n = len(arrays)
            phases = _exchange_phases(tuple(in_refs[:n]), tuple(out_refs[:n]), *sem_refs[:3], by_chip=by_chip)
            del in_refs[:n], out_refs[:n], sem_refs[:3]
            at_start, at_end = _hosted(phases, step, n_steps, pass_on_step)
            starts.append(at_start)
            ends.append(at_end)
        return (lambda: [f() for f in starts]), (lambda: [f() for f in ends])

    return args, [ANY] * len(args), shapes, [ANY] * len(shapes), scratch, bind


def _bwd_in_dw(dproj, x, g_pre, exchanges):
    n_tok = x.shape[0]
    tm = TOK_TILE
    n_tiles = n_tok // tm
    r_args, r_in_specs, r_shapes, r_out_specs, r_scratch, bind = _riders(exchanges)

    def body(*refs):
        dp_ref, x_ref, g_ref = refs[:3]
        in_refs = refs[3:3 + len(r_args)]
        dwt_ref = refs[3 + len(r_args)]
        out_refs = refs[4 + len(r_args):4 + 2 * len(r_args)]
        acc_s = refs[4 + 2 * len(r_args)]
        i = pl.program_id(0)
        at_start, at_end = bind(in_refs, out_refs, refs[5 + 2 * len(r_args):], i, n_tiles)
        at_start()

        @pl.when(i == 0)
        def _():
            acc_s[...] = jnp.zeros_like(acc_s)

        xf = x_ref[...]
        r = lax.rsqrt(jnp.mean(xf * xf, axis=-1, keepdims=True) + NORM_EPS)
        acc_s[...] += _dot_tn(dp_ref[...], ((xf * r) * g_ref[...]).astype(BF16))

        @pl.when(i == n_tiles - 1)
        def _():
            dwt_ref[...] = acc_s[...].astype(BF16)

        at_end()

    return pl.pallas_call(
        body, name="bwd_in_dw", grid=(n_tiles,),
        out_shape=(jax.ShapeDtypeStruct((IN_WIDTH, D_MODEL), BF16), *r_shapes),
        in_specs=[pl.BlockSpec((tm, IN_WIDTH), lambda i: (i, 0)), pl.BlockSpec((tm, D_MODEL), lambda i: (i, 0)),
                  pl.BlockSpec((1, D_MODEL), lambda i: (0, 0))] + r_in_specs,
        out_specs=(pl.BlockSpec((IN_WIDTH, D_MODEL), lambda i: (0, 0), pipeline_mode=pl.Buffered(1)), *r_out_specs),
        scratch_shapes=[pltpu.VMEM((IN_WIDTH, D_MODEL), F32)] + r_scratch,
        compiler_params=_cparams(1),
    )(dproj, x, g_pre, *r_args)


def _bwd_in_dx(dproj, x, dout, g_pre, w_in_t, exchanges):
    n_tok = x.shape[0]
    tm = TOK_TILE
    n_tiles = n_tok // tm
    r_args, r_in_specs, r_shapes, r_out_specs, r_scratch, bind = _riders(exchanges)

    def body(*refs):
        dp_ref, x_ref, dout_ref, g_ref, wt_ref = refs[:5]
        in_refs = refs[5:5 + len(r_args)]
        dx_ref, dg_ref = refs[5 + len(r_args):7 + len(r_args)]
        out_refs = refs[7 + len(r_args):7 + 2 * len(r_args)]
        i = pl.program_id(0)
        at_start, at_end = bind(in_refs, out_refs, refs[7 + 2 * len(r_args):], i, n_tiles, pass_on_step=n_tiles - 2)
        at_start()

        @pl.when(i == 0)
        def _():
            dg_ref[...] = jnp.zeros_like(dg_ref)

        xf = x_ref[...]
        r = lax.rsqrt(jnp.mean(xf * xf, axis=-1, keepdims=True) + NORM_EPS)
        xn = xf * r
        dh = _dot(dp_ref[...], wt_ref[...])
        dg_ref[...] += jnp.sum(dh * xn, axis=0, keepdims=True)
        dhn = dh * g_ref[...]
        dx_ref[...] = dout_ref[...] + r * (dhn - xn * jnp.mean(dhn * xn, axis=-1, keepdims=True))
        at_end()

    tile = lambda w: pl.BlockSpec((tm, w), lambda i: (i, 0))
    return pl.pallas_call(
        body, name="bwd_in_dx", grid=(n_tiles,),
        out_shape=(jax.ShapeDtypeStruct((n_tok, D_MODEL), F32), jax.ShapeDtypeStruct((1, D_MODEL), F32), *r_shapes),
        in_specs=[tile(IN_WIDTH), tile(D_MODEL), tile(D_MODEL), pl.BlockSpec((1, D_MODEL), lambda i: (0, 0)),
                  pl.BlockSpec((IN_WIDTH, D_MODEL), lambda i: (0, 0), pipeline_mode=pl.Buffered(1))] + r_in_specs,
        out_specs=(tile(D_MODEL), pl.BlockSpec((1, D_MODEL), lambda i: (0, 0)), *r_out_specs),
        scratch_shapes=r_scratch,
        compiler_params=_cparams(1),
    )(dproj, x, dout, g_pre, w_in_t, *r_args)


def _adam_update(w, g, m, v):
    nm = ADAM_B1 * m + (1.0 - ADAM_B1) * g
    nv = ADAM_B2 * v + (1.0 - ADAM_B2) * (g * g)
    m_hat = nm / (1.0 - ADAM_B1 ** ADAM_STEP)
    v_hat = nv / (1.0 - ADAM_B2 ** ADAM_STEP)
    return -ADAM_LR * (m_hat / (jnp.sqrt(v_hat) + ADAM_EPS) + ADAM_WD * w), nm, nv


def _sum_slots(p_ref):
    tot = p_ref[0].astype(F32)
    for d in range(1, N_DEV):
        tot = tot + p_ref[d].astype(F32)
    return tot


def _adamw_parts(parts, w, m, v, name):
    n_layers, n_rows, n_cols = w.shape
    tr = n_rows
    while tr * n_cols > ADAMW_BLOCK_ELEMS:
        tr //= 2
    n_blocks = n_rows // tr

    def body(*refs):
        p_refs = refs[:n_layers]
        w_ref, m_ref, v_ref, g_ref, d_ref, nm_ref, nv_ref = refs[n_layers:]
        for k in range(n_layers):
            @pl.when(pl.program_id(0) == k)
            def _(k=k):
                g = _sum_slots(p_refs[k])
                g_ref[0] = g
                d_ref[0], nm_ref[0], nv_ref[0] = _adam_update(w_ref[0], g, m_ref[0], v_ref[0])

    def part_spec(k):
        return pl.BlockSpec((N_DEV, tr, n_cols), lambda l, i: (0, jnp.clip(i + (l - k) * n_blocks, 0, n_blocks - 1), 0))

    spec = pl.BlockSpec((1, tr, n_cols), lambda l, i: (l, i, 0))
    shape = jax.ShapeDtypeStruct((n_layers, n_rows, n_cols), F32)
    return pl.pallas_call(
        body, name="adamw_" + name, grid=(n_layers, n_blocks), out_shape=(shape,) * 4,
        in_specs=[part_spec(k) for k in range(n_layers)] + [spec, spec, spec], out_specs=(spec,) * 4,
        compiler_params=_cparams(2),
    )(*parts, w, m, v)


def _adamw_small(quads):
    n = len(quads)

    def body(*refs):
        ins, outs = refs[:4 * n], refs[4 * n:]
        for j in range(n):
            w_ref, g_ref, m_ref, v_ref = ins[4 * j:4 * j + 4]
            d_ref, nm_ref, nv_ref = outs[3 * j:3 * j + 3]
            d_ref[...], nm_ref[...], nv_ref[...] = _adam_update(w_ref[...], g_ref[...], m_ref[...], v_ref[...])

    vmem = pl.BlockSpec(memory_space=pltpu.VMEM)
    shapes = tuple(jax.ShapeDtypeStruct(w.shape, F32) for w, *_ in quads for _ in range(3))
    results = pl.pallas_call(
        body, name="adamw_small", out_shape=shapes, in_specs=[vmem] * (4 * n), out_specs=(vmem,) * (3 * n),
    )(*[a for quad in quads for a in quad])
    return [results[3 * j:3 * j + 3] for j in range(n)]


_SMALL = ("pre_norm_g", "ln_v_g", "ln_v_b", "b_spatial", "sinks", "rel_bias", "post_norm_g")


def _pack(pieces):
    blocks, first_rows, n = [], [], 0
    for p in pieces:
        flat = p.reshape(-1)
        flat = jnp.concatenate([flat, jnp.zeros(((-flat.shape[0]) % (8 * 128),), F32)]).reshape(-1, 128)
        blocks.append(flat)
        first_rows.append(n)
        n += flat.shape[0]
    return jnp.concatenate(blocks, axis=0), first_rows


def _unpack(block, first_row, shape):
    size = math.prod(shape)
    return block[first_row:first_row + -(-size // 128)].reshape(-1)[:size].reshape(shape)


def kernel(x, pre_norm_g, w_in, ln_v_g, ln_v_b, w_spatial, b_spatial, sinks, rel_bias, w_out, post_norm_g, loss_target, m_pre_norm_g, m_w_in, m_ln_v_g, m_ln_v_b, m_w_spatial, m_b_spatial, m_sinks, m_rel_bias, m_w_out, m_post_norm_g, v_pre_norm_g, v_w_in, v_ln_v_g, v_ln_v_b, v_w_spatial, v_b_spatial, v_sinks, v_rel_bias, v_w_out, v_post_norm_g):
    weights = dict(pre_norm_g=pre_norm_g, w_in=w_in, ln_v_g=ln_v_g, ln_v_b=ln_v_b, w_spatial=w_spatial, b_spatial=b_spatial,
                   sinks=sinks, rel_bias=rel_bias, w_out=w_out, post_norm_g=post_norm_g)
    mom_m = dict(pre_norm_g=m_pre_norm_g, w_in=m_w_in, ln_v_g=m_ln_v_g, ln_v_b=m_ln_v_b, w_spatial=m_w_spatial,
                 b_spatial=m_b_spatial, sinks=m_sinks, rel_bias=m_rel_bias, w_out=m_w_out, post_norm_g=m_post_norm_g)
    mom_v = dict(pre_norm_g=v_pre_norm_g, w_in=v_w_in, ln_v_g=v_ln_v_g, ln_v_b=v_ln_v_b, w_spatial=v_w_spatial,
                 b_spatial=v_b_spatial, sinks=v_sinks, rel_bias=v_rel_bias, w_out=v_w_out, post_norm_g=v_post_norm_g)
    n_seq, seq_len, _ = x.shape
    n_layers = w_in.shape[0]
    x2 = x.reshape(n_seq * seq_len, D_MODEL)
    target2 = loss_target.reshape(n_seq * seq_len, D_MODEL)
    row = lambda p, l: p[l][None]

    wt_shards = jnp.swapaxes(w_in, 1, 2).astype(BF16)
    wo_shards = w_out.astype(BF16)
    bucket = _window_buckets()
    bias = _bias_table(rel_bias, bucket)
    wcat, wtcat, bs = _spatial_tables(w_spatial, jnp.swapaxes(b_spatial, 1, 2))

    wt, wo = [None] * n_layers, [None] * n_layers
    wt[0], wo[0] = _gather_weights(wt_shards, wo_shards, 0)
    xs, saved = [x2], []
    for l in range(n_layers):
        layer_args = (xs[-1], row(pre_norm_g, l), wt[l], row(ln_v_g, l), row(ln_v_b, l), wcat[l], bs[l], sinks[l], bias,
                      wo[l], row(post_norm_g, l), seq_len)
        if l + 1 < n_layers:
            *acts, xn, y, wt[l + 1], wo[l + 1] = _fwd_layer(*layer_args, gather=(wt_shards, wo_shards, l + 1))
            xs.append(xn)
        else:
            *acts, dx, y, loss = _fwd_layer(*layer_args, target=target2)
        saved.append((y, *acts))

    small = [None] * n_layers
    parts = [None] * n_layers
    waiting = None
    for l in reversed(range(n_layers)):
        hosted = {} if waiting is None else dict(exchange=waiting[:2], allgather=waiting[2])
        dproj, *rest = _bwd_mix(dx, *saved[l], row(ln_v_g, l), row(ln_v_b, l), wtcat[l], wo[l], row(post_norm_g, l), **hosted)
        dwo, dws, dmix, dlng, dlnb, dgpost, dbias = rest[:7]
        dws = dws.reshape(HEADS * CHUNK, CHUNK)
        if waiting is not None:
            parts[l + 1] = list(rest[7:])
        if l > 0:
            dx, dwt, dgpre = _bwd_in(dproj, xs[l], dx, row(pre_norm_g, l), wt[l])
            waiting = (dwt, dwo, dws)
        else:
            dwt, po, ws_all = _bwd_in_dw(dproj, xs[l], row(pre_norm_g, l), [((dwo,), True), ((dws,), False)])
            dx, dgpre, pt = _bwd_in_dx(dproj, xs[l], dx, row(pre_norm_g, l), wt[l], [((dwt,), True)])
            parts[l] = [pt, po, ws_all]
        small[l] = dict(pre_norm_g=dgpre[0], ln_v_g=dlng[0], ln_v_b=dlnb[0], dmix=dmix, dbias=dbias,
                        post_norm_g=dgpost[0])

    db = _b_spatial_grad([s["dmix"] for s in small])
    pieces = {n: jnp.stack([s[n] for s in small]) for n in ("pre_norm_g", "ln_v_g", "ln_v_b", "post_norm_g")}
    pieces["b_spatial"] = jnp.swapaxes(db[:, :, :HEADS], 1, 2)
    pieces["rel_bias"], pieces["sinks"] = _rel_bias_grad([s["dbias"] for s in small], bucket)
    packed, first_rows = _pack([pieces[n] for n in _SMALL] + [loss])
    total = _allreduce_small(packed)
    grad = {n: _unpack(total, r, weights[n].shape) for n, r in zip(_SMALL, first_rows)}
    loss_out = total[first_rows[-1], 0]

    delta, new_m, new_v = {}, {}, {}
    two_d = lambda p: p.reshape(-1, p.shape[-1])
    updates = _adamw_small([tuple(two_d(p[n]) for p in (weights, grad, mom_m, mom_v)) for n in _SMALL])
    for n, update in zip(_SMALL, updates):
        delta[n], new_m[n], new_v[n] = (u.reshape(weights[n].shape) for u in update)

    t3 = lambda p: jnp.swapaxes(p, 1, 2)
    flat3 = lambda p: p.reshape(n_layers, HEADS * CHUNK, CHUNK)
    for n, which, view, back in (("w_in", 0, t3, t3), ("w_out", 1, lambda p: p, lambda p: p),
                                 ("w_spatial", 2, flat3, lambda p: p.reshape(w_spatial.shape))):
        results = _adamw_parts([parts[l][which] for l in range(n_layers)], view(weights[n]), view(mom_m[n]),
                               view(mom_v[n]), n)
        grad[n], delta[n], new_m[n], new_v[n] = (back(r) for r in results)

    names = tuple(weights)
    return (loss_out, dx.reshape(x.shape), *[grad[n] for n in names], *[delta[n] for n in names],
            *[new_m[n] for n in names], *[new_v[n] for n in names])
```

```python
import math

import jax
import jax.numpy as jnp
from jax import lax
from jax.experimental import pallas as pl
from jax.experimental.pallas import tpu as pltpu

F32 = jnp.float32
BF16 = jnp.bfloat16

D_MODEL = 1024
A_WIDTH = 512
B_WIDTH = 512
KV_WIDTH = 128
IN_WIDTH = 3 * A_WIDTH + 2 * B_WIDTH + 2 * KV_WIDTH
CHUNK = 128
HEADS = 8
HEAD_DIM = 64
REL_BUCKETS = 32
NORM_EPS = 1e-6
NEG = -1e30
Q_OFF = 3 * A_WIDTH
K_OFF = Q_OFF + B_WIDTH
BZ_OFF = K_OFF + 2 * KV_WIDTH
QK_SCALE = HEAD_DIM ** -0.5

ADAM_LR = 0.001
ADAM_B1 = 0.9
ADAM_B2 = 0.999
ADAM_EPS = 1e-08
ADAM_WD = 0.01
ADAM_STEP = 10

TOK_TILE = 512
CHUNK_UNROLL = 4
BWD_CHUNK_UNROLL = 4
ADAMW_BLOCK_ELEMS = 192 * 1024
VMEM_LIMIT_V7X = 60 * 1024 * 1024

N_DEV = 8
N_CHIPS = 4
MESH_ID = pl.DeviceIdType.MESH
ANY = pl.BlockSpec(memory_space=pl.ANY)
SMEM = pl.BlockSpec(memory_space=pltpu.SMEM)


def _cparams(n_axes):
    return pltpu.CompilerParams(dimension_semantics=("arbitrary",) * n_axes, vmem_limit_bytes=VMEM_LIMIT_V7X)


_GELU_C = math.sqrt(2.0 / math.pi)
_GELU_C3 = _GELU_C * 0.044715


def _gelu_and_grad(x):
    x2 = x * x
    t = jnp.tanh(x * (_GELU_C3 * x2 + _GELU_C))
    cdf = 0.5 * t + 0.5
    d = cdf + (x * (cdf * (1.0 - cdf))) * ((6.0 * _GELU_C3) * x2 + 2.0 * _GELU_C)
    return x * cdf, d


def _sigmoid(x):
    return 0.5 + 0.5 * jnp.tanh(0.5 * x)


def _silu(x):
    return x * _sigmoid(x)


def _silu_and_grad(x):
    s = _sigmoid(x)
    return x * s, s * (1.0 + x * (1.0 - s))


def _dot(a, b):
    return jnp.dot(a, b, preferred_element_type=F32)


def _dot_nt(a, b):
    return lax.dot_general(a, b, (((1,), (1,)), ((), ())), preferred_element_type=F32)


def _dot_tn(a, b):
    return lax.dot_general(a, b, (((0,), (0,)), ((), ())), preferred_element_type=F32)


def _lo_mask(shape):
    return lax.broadcasted_iota(jnp.int32, shape, 1) < HEAD_DIM


def _swap_halves(v):
    return pltpu.roll(v, HEAD_DIM, 1)


def _window_buckets():
    q_loc = jnp.arange(CHUNK)[:, None]
    j_loc = jnp.arange(CHUNK)[None, :]
    d = q_loc - j_loc + jnp.where(j_loc > q_loc, CHUNK, 0)
    max_exact = REL_BUCKETS // 2
    safe = jnp.maximum(d, 1).astype(F32)
    large = max_exact + (jnp.log(safe / max_exact) / math.log(CHUNK / max_exact)
                         * (REL_BUCKETS - max_exact)).astype(jnp.int32)
    large = jnp.minimum(large, REL_BUCKETS - 1)
    return jnp.where(d < max_exact, d, large).astype(jnp.int32)


def _bias_table(rel_bias, bucket):
    def body(rb_ref, bk_ref, out_ref):
        bk = bk_ref[...]
        for h in range(HEADS):
            acc = jnp.zeros(bk.shape, F32)
            for b in range(REL_BUCKETS):
                acc = jnp.where(bk == b, rb_ref[b, h], acc)
            out_ref[h] = acc

    vmem = pl.BlockSpec(memory_space=pltpu.VMEM)
    return pl.pallas_call(
        body, name="bias_table",
        out_shape=jax.ShapeDtypeStruct((HEADS, CHUNK, CHUNK), F32),
        in_specs=[SMEM, vmem], out_specs=vmem,
    )(rel_bias, bucket)


def _rel_bias_grad(dbias, bucket):
    n_layers = len(dbias)

    def body(*refs):
        db_refs, (bk_ref, out_ref, dsink_ref) = refs[:n_layers], refs[n_layers:]
        bk = bk_ref[...]
        for h in range(HEADS):
            tot = jnp.zeros((CHUNK, CHUNK), F32)
            for l in range(n_layers):
                ds = db_refs[l][h]
                dsink_ref[l, h] = -jnp.sum(ds)
                tot = tot + ds
            for b in range(REL_BUCKETS):
                out_ref[b, h] = jnp.sum(jnp.where(bk == b, tot, 0.0))

    vmem = pl.BlockSpec(memory_space=pltpu.VMEM)
    return pl.pallas_call(
        body, name="rel_bias_grad",
        out_shape=(jax.ShapeDtypeStruct((REL_BUCKETS, HEADS), F32), jax.ShapeDtypeStruct((n_layers, HEADS), F32)),
        in_specs=[vmem] * (n_layers + 1), out_specs=(SMEM, SMEM),
    )(*dbias, bucket)


def _spatial_tables(w_spatial, b_spatial_t):
    n_layers = w_spatial.shape[0]

    def body(w_ref, b_ref, wcat_ref, wtcat_ref, bs_ref):
        row = lax.broadcasted_iota(jnp.int32, (CHUNK, CHUNK), 0)
        col = lax.broadcasted_iota(jnp.int32, (CHUNK, CHUNK), 1)
        causal = col <= row
        lo = _lo_mask((CHUNK, CHUNK))
        for p in range(4):
            for half in range(2):
                w = jnp.where(causal, w_ref[0, 2 * p + half], 0.0)
                wcat_ref[0, p, :, half * CHUNK:(half + 1) * CHUNK] = w.astype(BF16)
                wtcat_ref[0, p, :, half * CHUNK:(half + 1) * CHUNK] = w.T.astype(BF16)
            b = b_ref[0]
            bs_ref[0, p] = jnp.where(lo, b[:, 2 * p:2 * p + 1], b[:, 2 * p + 1:2 * p + 2])

    return pl.pallas_call(
        body, name="spatial_tables", grid=(n_layers,),
        out_shape=(jax.ShapeDtypeStruct((n_layers, 4, CHUNK, 2 * CHUNK), BF16),
                   jax.ShapeDtypeStruct((n_layers, 4, CHUNK, 2 * CHUNK), BF16),
                   jax.ShapeDtypeStruct((n_layers, 4, CHUNK, CHUNK), F32)),
        in_specs=[pl.BlockSpec((1, HEADS, CHUNK, CHUNK), lambda l: (l, 0, 0, 0)),
                  pl.BlockSpec((1, CHUNK, HEADS), lambda l: (l, 0, 0))],
        out_specs=(pl.BlockSpec((1, 4, CHUNK, 2 * CHUNK), lambda l: (l, 0, 0, 0)),
                   pl.BlockSpec((1, 4, CHUNK, 2 * CHUNK), lambda l: (l, 0, 0, 0)),
                   pl.BlockSpec((1, 4, CHUNK, CHUNK), lambda l: (l, 0, 0, 0))),
        compiler_params=_cparams(1),
    )(w_spatial, b_spatial_t)


def _b_spatial_grad(dmix):
    n_layers = len(dmix)

    def body(*refs):
        d_refs, out_ref = refs[:n_layers], refs[n_layers]
        lane = lax.broadcasted_iota(jnp.int32, (CHUNK, CHUNK), 1)
        for l in range(n_layers):
            acc = jnp.zeros((CHUNK, CHUNK), F32)
            for p in range(4):
                t = d_refs[l][:, p * CHUNK:(p + 1) * CHUNK]
                s_lo = jnp.sum(jnp.where(lane < HEAD_DIM, t, 0.0), axis=1, keepdims=True)
                s_hi = jnp.sum(jnp.where(lane < HEAD_DIM, 0.0, t), axis=1, keepdims=True)
                acc = jnp.where(lane == 2 * p, s_lo, acc)
                acc = jnp.where(lane == 2 * p + 1, s_hi, acc)
            out_ref[l] = acc

    vmem = pl.BlockSpec(memory_space=pltpu.VMEM)
    return pl.pallas_call(
        body, name="b_spatial_grad",
        out_shape=jax.ShapeDtypeStruct((n_layers, CHUNK, CHUNK), F32),
        in_specs=[vmem] * n_layers, out_specs=vmem,
    )(*dmix)


def _place():
    x, y, c = lax.axis_index("x"), lax.axis_index("y"), lax.axis_index("c")
    other_chips = [(1 - x, y), (x, 1 - y), (1 - x, 1 - y)]
    return x, y, c, other_chips


N_GATHER_SEMS = 12


def _gather_phases(shards, fulls, send_sems, recv_sems, local_sems):
    x, y, c, chips = _place()
    sibling = (x, y, 1 - c)
    n_arr = len(shards)

    def half_rows(a, chip, half):
        n = shards[a].shape[0]
        start = (2 * chip[0] + chip[1]) * n + half * (n // 2)
        return fulls[a].at[pl.ds(pl.multiple_of(start, 16), n // 2), :]

    def my_half(a):
        n = shards[a].shape[0]
        return shards[a].at[pl.ds(pl.multiple_of(c * (n // 2), 16), n // 2), :]

    def copy(k, a, src, chip, half, to):
        return pltpu.make_async_remote_copy(
            src_ref=src, dst_ref=half_rows(a, chip, half), send_sem=send_sems.at[n_arr * k + a],
            recv_sem=recv_sems.at[n_arr * k + a], device_id=to, device_id_type=MESH_ID)

    def local(a):
        n = shards[a].shape[0]
        mine = fulls[a].at[pl.ds(pl.multiple_of((2 * x + y) * n, 16), n), :]
        return pltpu.make_async_copy(shards[a], mine, local_sems.at[a])

    def first(k, a):
        return copy(k, a, my_half(a), (x, y), c, (chips[k][0], chips[k][1], c))

    def passed(k, a):
        return copy(3 + k, a, half_rows(a, chips[k], c), chips[k], c, sibling)

    def phase_a():
        for a in range(n_arr):
            local(a).start()
        for k in range(3):
            for a in range(n_arr):
                first(k, a).start()

    def phase_b():
        for k in range(3):
            for a in range(n_arr):
                copy(k, a, my_half(a), chips[k], c, sibling).wait_recv()
                passed(k, a).start()

    def phase_c():
        for k in range(3):
            for a in range(n_arr):
                copy(3 + k, a, my_half(a), chips[k], 1 - c, sibling).wait_recv()
        for k in range(3):
            for a in range(n_arr):
                first(k, a).wait_send()
                passed(k, a).wait_send()
        for a in range(n_arr):
            local(a).wait()

    return phase_a, phase_b, phase_c


N_EXCHANGE_SEMS = 14


def _exchange_phases(partials, parts, send_sems, recv_sems, local_sems, by_chip=True):
    x, y, c, chips = _place()
    me, sibling = (x, y, c), (x, y, 1 - c)
    n_arr = len(partials)

    def block(a, chip):
        if not by_chip:
            return partials[a]
        n = partials[a].shape[0] // N_CHIPS
        return partials[a].at[pl.ds(pl.multiple_of((2 * chip[0] + chip[1]) * n, 16), n), :]

    def slot(a, dev):
        return parts[a].at[4 * dev[0] + 2 * dev[1] + dev[2]]

    def copy(k, a, src, origin, to):
        return pltpu.make_async_remote_copy(
            src_ref=src, dst_ref=slot(a, origin), send_sem=send_sems.at[n_arr * k + a],
            recv_sem=recv_sems.at[n_arr * k + a], device_id=to, device_id_type=MESH_ID)

    def local(a):
        return pltpu.make_async_copy(block(a, (x, y)), slot(a, me), local_sems.at[a])

    def first(k, a):
        if k == 0:
            return copy(0, a, block(a, (x, y)), me, sibling)
        chip = chips[k - 1]
        return copy(k, a, block(a, chip), me, (chip[0], chip[1], c))

    def passed(k, a):
        origin = (chips[k][0], chips[k][1], c)
        return copy(4 + k, a, slot(a, origin), origin, sibling)

    def phase_a():
        for a in range(n_arr):
            local(a).start()
        for k in range(4):
            for a in range(n_arr):
                first(k, a).start()

    def phase_b():
        for k in range(3):
            for a in range(n_arr):
                copy(1 + k, a, block(a, (x, y)), (chips[k][0], chips[k][1], c), me).wait_recv()
                passed(k, a).start()

    def phase_c():
        for a in range(n_arr):
            copy(0, a, block(a, (x, y)), sibling, me).wait_recv()
        for k in range(3):
            for a in range(n_arr):
                copy(4 + k, a, block(a, (x, y)), (chips[k][0], chips[k][1], 1 - c), me).wait_recv()
        for k in range(4):
            for a in range(n_arr):
                first(k, a).wait_send()
        for k in range(3):
            for a in range(n_arr):
                passed(k, a).wait_send()
        for a in range(n_arr):
            local(a).wait()

    return phase_a, phase_b, phase_c


def _comm_scratch(n_sems):
    return [pltpu.SemaphoreType.DMA((n_sems,)), pltpu.SemaphoreType.DMA((n_sems,)), pltpu.SemaphoreType.DMA((2,))]


def _gather_weights(wt_shards, wo_shards, layer):
    wt_rows, wo_rows = wt_shards.shape[1], wo_shards.shape[1]

    def body(wt_ref, wo_ref, wt_full, wo_full, send_sems, recv_sems, local_sems):
        phases = _gather_phases((wt_ref.at[layer], wo_ref.at[layer]), (wt_full, wo_full), send_sems, recv_sems, local_sems)
        for phase in phases:
            phase()

    return pl.pallas_call(
        body, name="gather_weights",
        out_shape=(jax.ShapeDtypeStruct((N_CHIPS * wt_rows, D_MODEL), BF16),
                   jax.ShapeDtypeStruct((N_CHIPS * wo_rows, D_MODEL), BF16)),
        in_specs=[ANY, ANY], out_specs=(ANY, ANY), scratch_shapes=_comm_scratch(N_GATHER_SEMS),
    )(wt_shards, wo_shards)


def _allreduce_small(part):
    n_rows = part.shape[0]

    def body(p_ref, tot_ref, all_ref, send_sems, recv_sems, local_sem):
        x, y, c, chips = _place()
        me, sibling = (x, y, c), (x, y, 1 - c)

        def rows(dev):
            return all_ref.at[pl.ds(pl.multiple_of((4 * dev[0] + 2 * dev[1] + dev[2]) * n_rows, 8), n_rows), :]

        def copy(k, origin, to, src=None):
            return pltpu.make_async_remote_copy(
                src_ref=rows(origin) if src is None else src, dst_ref=rows(origin), send_sem=send_sems.at[k],
                recv_sem=recv_sems.at[k], device_id=to, device_id_type=MESH_ID)

        mine = pltpu.make_async_copy(p_ref, rows(me), local_sem)
        mine.start()
        first = [copy(0, me, sibling, src=p_ref)]
        first += [copy(1 + k, me, (chip[0], chip[1], c), src=p_ref) for k, chip in enumerate(chips)]
        for cp in first:
            cp.start()
        passed = []
        for k, chip in enumerate(chips):
            origin = (chip[0], chip[1], c)
            copy(1 + k, origin, me).wait_recv()
            fwd = copy(4 + k, origin, sibling)
            fwd.start()
            passed.append(fwd)
        copy(0, sibling, me).wait_recv()
        for k, chip in enumerate(chips):
            copy(4 + k, (chip[0], chip[1], 1 - c), me).wait_recv()
        for cp in first + passed:
            cp.wait_send()
        mine.wait()
        tot = all_ref[0:n_rows, :]
        for d in range(1, N_DEV):
            tot = tot + all_ref[d * n_rows:(d + 1) * n_rows, :]
        tot_ref[...] = tot

    vmem = pl.BlockSpec(memory_space=pltpu.VMEM)
    return pl.pallas_call(
        body, name="allreduce_small",
        out_shape=jax.ShapeDtypeStruct((n_rows, 128), F32),
        in_specs=[vmem], out_specs=vmem,
        scratch_shapes=[pltpu.VMEM((N_DEV * n_rows, 128), F32), pltpu.SemaphoreType.DMA((7,)),
                        pltpu.SemaphoreType.DMA((7,)), pltpu.SemaphoreType.DMA],
        compiler_params=pltpu.CompilerParams(vmem_limit_bytes=VMEM_LIMIT_V7X),
    )(part)


def _hosted(phases, step, n_steps, pass_on_step=None):
    phase_a, phase_b, phase_c = phases
    if pass_on_step is None:
        pass_on_step = (3 * n_steps) // 4

    def at_start():
        pl.when(step == 0)(phase_a)

    def at_end():
        pl.when(step == pass_on_step)(phase_b)
        pl.when(step == n_steps - 1)(phase_c)

    return at_start, at_end


def _layer_norm_stats(vv):
    mu = jnp.mean(vv, axis=-1, keepdims=True)
    xc = vv - mu
    rs = lax.rsqrt(jnp.mean(xc * xc, axis=-1, keepdims=True) + NORM_EPS)
    return xc * rs, rs


STAT_POST_RMS, STAT_LN_RSTD = (slice(k * CHUNK, (k + 1) * CHUNK) for k in range(2))
N_STATS = 2


def _over_lanes(stat, width):
    return jnp.concatenate([stat] * (width // CHUNK), axis=1)


def _blockdiag(v, lo):
    zero = jnp.zeros_like(v)
    return jnp.concatenate([jnp.where(lo, v, zero), jnp.where(lo, zero, v)], axis=0)


def _softmax_sink(s, sink):
    m = jnp.maximum(jnp.max(s, axis=-1, keepdims=True), sink)
    e = jnp.exp(s - m)
    esink = jnp.exp(sink - m)
    den = jnp.sum(e, axis=-1, keepdims=True) + esink
    return e / den


def _kv_rows(cur_ref, halo_ref, r0, c):
    prev_in_tile = cur_ref[pl.ds(pl.multiple_of(jnp.maximum(r0 - CHUNK, 0), CHUNK), CHUNK), :]
    prev = jnp.where(c == 0, halo_ref[...], prev_in_tile)
    kv2 = jnp.concatenate([prev, cur_ref[pl.ds(r0, CHUNK), :]], axis=0)
    k2, v2 = kv2[:, 0:KV_WIDTH], kv2[:, KV_WIDTH:2 * KV_WIDTH]
    return (k2, _swap_halves(k2)), (v2, _swap_halves(v2))


def _window_square(over_keys, prev):
    return jnp.where(prev, over_keys[:, 0:CHUNK], over_keys[:, CHUNK:2 * CHUNK])


def _window_keys(square, prev):
    zero = jnp.zeros_like(square)
    return jnp.concatenate([jnp.where(prev, square, zero), jnp.where(prev, zero, square)], axis=1)


def _ahead(shape=(CHUNK, CHUNK)):
    return lax.broadcasted_iota(jnp.int32, shape, 1) - lax.broadcasted_iota(jnp.int32, shape, 0)


def _dead_mask(first, ahead):
    return ahead > jnp.where(first, 0, CHUNK)


def _head_of(p, half):
    return 2 * p + half, int(half != p // 2)


_HEADS_BY_COPY = tuple(tuple((p, half) for p in range(4) for half in range(2) if _head_of(p, half)[1] == sw)
                       for sw in range(2))


def _masked_halves(tile, lo):
    zero = jnp.zeros_like(tile)
    return {0: jnp.where(lo, tile, zero), 1: jnp.where(lo, zero, tile)}


def _query_stacks(q_tiles, lo):
    qm = {p: _masked_halves(q_tiles[p] * QK_SCALE, lo) for p in range(4)}
    return [jnp.concatenate([qm[p][half] for p, half in members], axis=0) for members in _HEADS_BY_COPY]


def _attention_probs(q_stacks, ks, bias_ref, sink_ref, dead, prev):
    probs = {}
    for sw, members in enumerate(_HEADS_BY_COPY):
        s_stack = _dot_nt(q_stacks[sw], ks[sw])
        for i, (p, half) in enumerate(members):
            head = 2 * p + half
            s = _window_square(s_stack[i * CHUNK:(i + 1) * CHUNK, :], prev) + bias_ref[head]
            s = jnp.where(dead, NEG, s)
            probs[p, half] = _softmax_sink(s, sink_ref[head]).astype(BF16)
    return probs


def _attention_values(probs, vs, prev):
    outs, p_stacks = {}, []
    for sw, members in enumerate(_HEADS_BY_COPY):
        p_stack = jnp.concatenate([_window_keys(probs[m], prev) for m in members], axis=0)
        r_stack = _dot(p_stack, vs[sw])
        p_stacks.append(p_stack)
        for i, m in enumerate(members):
            outs[m] = r_stack[i * CHUNK:(i + 1) * CHUNK, :]
    return outs, p_stacks


def _fwd_layer(x, g_pre, w_in_t, ln_g, ln_b, wcat, bs, sinks, bias, w_out, g_post, seq_len, gather=None, target=None):
    n_tok = x.shape[0]
    tb = TOK_TILE
    n_chunks = tb // CHUNK
    n_tiles = n_tok // tb
    n_in = 11
    assert gather is None or target is None

    def body(*refs):
        (x_ref, gpre_ref, wt_ref, lng_ref, lnb_ref, wcat_ref, bs_ref, sink_ref, bias_ref, wout_ref,
         gpost_ref) = refs[:n_in]
        at_start = at_end = lambda: None
        if gather is not None:
            (wts_ref, wos_ref, a_ref, q_ref, kv_ref, bz_ref, probs_ref, stats_ref, mixed_ref, act_ref, att_ref, xn_ref, y_ref,
             wt_full, wo_full, ycat_s, halo_ref, send_sems, recv_sems, local_sems) = refs[n_in:]
            phases = _gather_phases((wts_ref.at[gather[2]], wos_ref.at[gather[2]]), (wt_full, wo_full), send_sems,
                                    recv_sems, local_sems)
            at_start, at_end = _hosted(phases, pl.program_id(0), n_tiles)
        elif target is not None:
            (target_ref, a_ref, q_ref, kv_ref, bz_ref, probs_ref, stats_ref, mixed_ref, act_ref, att_ref, xn_ref, y_ref,
             loss_ref, ycat_s, halo_ref) = refs[n_in:]
        else:
            (a_ref, q_ref, kv_ref, bz_ref, probs_ref, stats_ref, mixed_ref, act_ref, att_ref, xn_ref, y_ref, ycat_s,
             halo_ref) = refs[n_in:]
        at_start()
        i = pl.program_id(0)
        lo = _lo_mask((CHUNK, CHUNK))
        ahead = _ahead()
        prev = ahead > 0

        @pl.when(i == 0)
        def _():
            halo_ref[...] = jnp.zeros_like(halo_ref)

        xf = x_ref[...]
        r1 = lax.rsqrt(jnp.mean(xf * xf, axis=-1, keepdims=True) + NORM_EPS)
        h = ((xf * r1) * gpre_ref[...]).astype(BF16)
        a_ref[...] = _dot_nt(h, wt_ref[0:Q_OFF, :])
        q_ref[...] = _dot_nt(h, wt_ref[Q_OFF:K_OFF, :]).astype(BF16)
        kv_ref[...] = _dot_nt(h, wt_ref[K_OFF:BZ_OFF, :]).astype(BF16)
        bz_ref[...] = _dot_nt(h, wt_ref[BZ_OFF:IN_WIDTH, :])

        def chunk(c, carry):
            r0 = pl.multiple_of(c * CHUNK, CHUNK)
            rows = pl.ds(r0, CHUNK)
            u, gu = _gelu_and_grad(a_ref[rows, 0:A_WIDTH])
            vv, gv = _gelu_and_grad(a_ref[rows, A_WIDTH:2 * A_WIDTH])
            xhat, rs = _layer_norm_stats(vv)
            for k, act in enumerate((u, gu, xhat, gv)):
                act_ref[rows, k * A_WIDTH:(k + 1) * A_WIDTH] = act
            stats_ref[rows, STAT_LN_RSTD] = jnp.broadcast_to(rs, (CHUNK, CHUNK))
            vnb = (xhat * lng_ref[...] + lnb_ref[...]).astype(BF16)
            for p in range(4):
                blk = slice(p * CHUNK, (p + 1) * CHUNK)
                mixed = _dot(wcat_ref[p], _blockdiag(vnb[:, blk], lo)) + bs_ref[p]
                mixed_ref[rows, blk] = mixed
                sz = _silu(a_ref[rows, 2 * A_WIDTH + p * CHUNK:2 * A_WIDTH + (p + 1) * CHUNK])
                ycat_s[rows, blk] = ((u[:, blk] * mixed) * sz).astype(BF16)
            ks, vs = _kv_rows(kv_ref, halo_ref, r0, c)
            dead = _dead_mask(lax.rem(i * tb + r0, seq_len) == 0, ahead)
            q_tiles = [q_ref[rows, p * CHUNK:(p + 1) * CHUNK] for p in range(4)]
            probs = _attention_probs(_query_stacks(q_tiles, lo), ks, bias_ref, sink_ref, dead, prev)
            for (p, half), prob in probs.items():
                head = 2 * p + half
                probs_ref[rows, head * CHUNK:(head + 1) * CHUNK] = prob
            outs, _ = _attention_values(probs, vs, prev)
            for p in range(4):
                blk = slice(p * CHUNK, (p + 1) * CHUNK)
                o = jnp.where(lo, outs[p, 0], outs[p, 1])
                att_ref[rows, blk] = o
                ycat_s[rows, B_WIDTH + p * CHUNK:B_WIDTH + (p + 1) * CHUNK] = (o * _silu(bz_ref[rows, blk])).astype(BF16)
            return carry

        lax.fori_loop(0, n_chunks, chunk, 0, unroll=CHUNK_UNROLL)
        halo_ref[...] = kv_ref[tb - CHUNK:tb, :]
        y = _dot(ycat_s[...], wout_ref[...])
        r = lax.rsqrt(jnp.mean(y * y, axis=-1, keepdims=True) + NORM_EPS)
        yn = y * r
        y_ref[...] = yn
        stats_ref[:, STAT_POST_RMS] = jnp.broadcast_to(r, (tb, CHUNK))
        xn = x_ref[...] + yn * gpost_ref[...]
        if target is None:
            xn_ref[...] = xn
        else:
            d = xn - target_ref[...]
            xn_ref[...] = d * (1.0 / D_MODEL)

            @pl.when(i == 0)
            def _():
                loss_ref[0, 0] = 0.0

            loss_ref[0, 0] += 0.5 * jnp.sum(jnp.mean(d * d, axis=-1, keepdims=True))
        at_end()

    tile = lambda w: pl.BlockSpec((tb, w), lambda i: (i, 0))
    whole = lambda shape, **kw: pl.BlockSpec(shape, lambda i: (0,) * len(shape), **kw)
    in_specs = [tile(D_MODEL), whole((1, D_MODEL)), whole((IN_WIDTH, D_MODEL), pipeline_mode=pl.Buffered(1)),
                whole((1, A_WIDTH)), whole((1, A_WIDTH)), whole((4, CHUNK, 2 * CHUNK)), whole((4, CHUNK, CHUNK)), SMEM,
                whole((HEADS, CHUNK, CHUNK)), whole((D_MODEL, D_MODEL)), whole((1, D_MODEL))]
    out_shape = [jax.ShapeDtypeStruct((n_tok, Q_OFF), F32), jax.ShapeDtypeStruct((n_tok, B_WIDTH), BF16),
                 jax.ShapeDtypeStruct((n_tok, 2 * KV_WIDTH), BF16), jax.ShapeDtypeStruct((n_tok, B_WIDTH), F32),
                 jax.ShapeDtypeStruct((n_tok, HEADS * CHUNK), BF16), jax.ShapeDtypeStruct((n_tok, N_STATS * CHUNK), F32),
                 jax.ShapeDtypeStruct((n_tok, A_WIDTH), F32), jax.ShapeDtypeStruct((n_tok, 4 * A_WIDTH), F32),
                 jax.ShapeDtypeStruct((n_tok, B_WIDTH), F32),
                 jax.ShapeDtypeStruct((n_tok, D_MODEL), F32), jax.ShapeDtypeStruct((n_tok, D_MODEL), F32)]
    out_specs = [tile(Q_OFF), tile(B_WIDTH), tile(2 * KV_WIDTH), tile(B_WIDTH), tile(HEADS * CHUNK), tile(N_STATS * CHUNK),
                 tile(A_WIDTH), tile(4 * A_WIDTH), tile(B_WIDTH), tile(D_MODEL), tile(D_MODEL)]
    scratch = [pltpu.VMEM((tb, D_MODEL), BF16), pltpu.VMEM((CHUNK, 2 * KV_WIDTH), BF16)]
    args = [x, g_pre, w_in_t, ln_g, ln_b, wcat, bs, sinks, bias, w_out, g_post]
    if gather is not None:
        wt_shards, wo_shards, _ = gather
        in_specs += [ANY, ANY]
        args += [wt_shards, wo_shards]
        out_shape += [jax.ShapeDtypeStruct((N_CHIPS * wt_shards.shape[1], D_MODEL), BF16),
                      jax.ShapeDtypeStruct((N_CHIPS * wo_shards.shape[1], D_MODEL), BF16)]
        out_specs += [ANY, ANY]
        scratch += _comm_scratch(N_GATHER_SEMS)
    if target is not None:
        in_specs.append(tile(D_MODEL))
        args.append(target)
        out_shape.append(jax.ShapeDtypeStruct((1, 1), F32))
        out_specs.append(SMEM)
    name = "fwd_layer" + ("" if gather is None else "_gather") + ("" if target is None else "_loss")
    return pl.pallas_call(
        body, name=name, grid=(n_tiles,),
        out_shape=tuple(out_shape), in_specs=in_specs, out_specs=tuple(out_specs), scratch_shapes=scratch,
        compiler_params=_cparams(1),
    )(*args)


def _bwd_mix(dout, y, a, q, kv, bz, probs, stats, mixed, act, att, ln_g, ln_b, wtcat, w_out, g_post, exchange=None,
             allgather=None):
    n_tok = y.shape[0]
    tb = TOK_TILE
    n_chunks = tb // CHUNK
    n_tiles = n_tok // tb

    def body(*refs):
        refs = list(refs)
        take = lambda n: [refs.pop(0) for _ in range(n)]
        (dout_ref, y_ref, az_ref, q_ref, kv_ref, halo_ref, bz_ref, probs_ref, stats_ref, mixed_ref, act_ref, att_ref,
         lng_ref, lnb_ref, wtcat_ref, wout_ref, gpost_ref) = take(17)
        if exchange is not None:
            dwt_in, dwo_in = take(2)
        if allgather is not None:
            (small_in,) = take(1)
        dproj_ref, dwout_ref, dws_ref, dmix_ref, dlng_ref, dlnb_ref, dgpost_ref, dbias_ref = take(8)
        if exchange is not None:
            pt_ref, po_ref = take(2)
        if allgather is not None:
            (small_all,) = take(1)
        ycat_s, dy_s, dyc_s, dkv_s, carry_s, dwout_s, stage_s, dws_s = take(8)
        starts, ends = [], []
        if exchange is not None:
            hosted = _hosted(_exchange_phases((dwt_in, dwo_in), (pt_ref, po_ref), *take(3)), pl.program_id(0), n_tiles)
            starts.append(hosted[0])
            ends.append(hosted[1])
        if allgather is not None:
            hosted = _hosted(_exchange_phases((small_in,), (small_all,), *take(3), by_chip=False), pl.program_id(0),
                             n_tiles)
            starts.append(hosted[0])
            ends.append(hosted[1])
        at_start = lambda: [f() for f in starts]
        at_end = lambda: [f() for f in ends]
        at_start()
        step = pl.program_id(0)
        lo = _lo_mask((CHUNK, CHUNK))
        prev = _ahead() > 0

        @pl.when(step == 0)
        def _():
            dwout_s[...] = jnp.zeros_like(dwout_s)
            dws_s[...] = jnp.zeros_like(dws_s)
            dmix_ref[...] = jnp.zeros_like(dmix_ref)
            dlng_ref[...] = jnp.zeros_like(dlng_ref)
            dlnb_ref[...] = jnp.zeros_like(dlnb_ref)
            dgpost_ref[...] = jnp.zeros_like(dgpost_ref)
            dbias_ref[...] = jnp.zeros_like(dbias_ref)
            carry_s[...] = jnp.zeros_like(carry_s)

        for r0 in range(0, tb, CHUNK):
            rows = slice(r0, r0 + CHUNK)
            yn = y_ref[rows, :]
            dout = dout_ref[rows, :]
            r = _over_lanes(stats_ref[rows, STAT_POST_RMS], D_MODEL)
            dgpost_ref[...] += jnp.sum(dout * yn, axis=0, keepdims=True)
            dyn = dout * gpost_ref[...]
            dy_s[rows, :] = (r * (dyn - yn * jnp.mean(dyn * yn, axis=-1, keepdims=True))).astype(BF16)
        dyc_s[...] = _dot_nt(dy_s[...], wout_ref[...])
        dkv_s[0:tb, :] = jnp.zeros((tb, 2 * KV_WIDTH), F32)
        dkv_s[tb:tb + CHUNK, :] = carry_s[...]

        def chunk(c, carry):
            r0 = pl.multiple_of(c * CHUNK, CHUNK)
            rows = pl.ds(r0, CHUNK)
            u, gu, xhat, gv = (act_ref[rows, k * A_WIDTH:(k + 1) * A_WIDTH] for k in range(4))
            rs = _over_lanes(stats_ref[rows, STAT_LN_RSTD], A_WIDTH)
            vnb = (xhat * lng_ref[...] + lnb_ref[...]).astype(BF16)
            d_vn, d_u, d_az = [], [], []
            for p in range(4):
                blk = slice(p * CHUNK, (p + 1) * CHUNK)
                mixed = mixed_ref[rows, blk]
                sz, gz = _silu_and_grad(az_ref[rows, blk])
                ub = u[:, blk]
                dya = dyc_s[rows, blk]
                um = ub * mixed
                ycat_s[rows, blk] = (um * sz).astype(BF16)
                d_mixed = (dya * ub) * sz
                d_u.append((dya * mixed) * sz)
                d_az.append((dya * um) * gz)
                dmix_ref[:, blk] += d_mixed
                dmbd = _blockdiag(d_mixed.astype(BF16), lo)
                d_vn.append(_dot(wtcat_ref[p], dmbd))
                dws_s[p] += _dot_nt(dmbd, vnb[:, blk])
            d_vn = jnp.concatenate(d_vn, axis=1)
            dlng_ref[...] += jnp.sum(d_vn * xhat, axis=0, keepdims=True)
            dlnb_ref[...] += jnp.sum(d_vn, axis=0, keepdims=True)
            dxh = d_vn * lng_ref[...]
            d_vv = rs * (dxh - jnp.mean(dxh, axis=-1, keepdims=True)
                         - xhat * jnp.mean(dxh * xhat, axis=-1, keepdims=True))
            dproj_ref[rows, 0:A_WIDTH] = (jnp.concatenate(d_u, axis=1) * gu).astype(BF16)
            dproj_ref[rows, A_WIDTH:2 * A_WIDTH] = (d_vv * gv).astype(BF16)
            dproj_ref[rows, 2 * A_WIDTH:Q_OFF] = jnp.concatenate(d_az, axis=1).astype(BF16)
            ks, vs = _kv_rows(kv_ref, halo_ref, r0, c)
            q_stacks = _query_stacks([q_ref[rows, p * CHUNK:(p + 1) * CHUNK] for p in range(4)], lo)
            probs = {(p, half): probs_ref[rows, (2 * p + half) * CHUNK:(2 * p + half + 1) * CHUNK]
                     for p in range(4) for half in range(2)}
            p_stacks = [jnp.concatenate([_window_keys(probs[m], prev) for m in members], axis=0)
                        for members in _HEADS_BY_COPY]
            dom = {}
            for p in range(4):
                blk = slice(p * CHUNK, (p + 1) * CHUNK)
                sb, gb = _silu_and_grad(bz_ref[rows, blk])
                dyb = dyc_s[rows, B_WIDTH + p * CHUNK:B_WIDTH + (p + 1) * CHUNK]
                o = att_ref[rows, blk]
                ycat_s[rows, B_WIDTH + p * CHUNK:B_WIDTH + (p + 1) * CHUNK] = (o * sb).astype(BF16)
                dproj_ref[rows, BZ_OFF + p * CHUNK:BZ_OFF + (p + 1) * CHUNK] = ((dyb * o) * gb).astype(BF16)
                dom[p] = _masked_halves((dyb * sb).astype(BF16), lo)
            dqs, dk_by_copy, dv_by_copy = {}, [], []
            for sw, members in enumerate(_HEADS_BY_COPY):
                do_stack = jnp.concatenate([dom[p][half] for p, half in members], axis=0)
                dp_stack = _dot_nt(do_stack, vs[sw])
                dsws = []
                for i, (p, half) in enumerate(members):
                    head = 2 * p + half
                    dp = _window_square(dp_stack[i * CHUNK:(i + 1) * CHUNK, :], prev)
                    prob = probs[p, half].astype(F32)
                    delta = jnp.sum(prob * dp, axis=-1, keepdims=True)
                    ds = prob * (dp - delta)
                    dbias_ref[head] += ds
                    dsws.append(_window_keys(ds.astype(BF16), prev))
                ds_stack = jnp.concatenate(dsws, axis=0)
                dq_stack = _dot(ds_stack, ks[sw])
                for i, m in enumerate(members):
                    dqs[m] = dq_stack[i * CHUNK:(i + 1) * CHUNK, :]
                dk_by_copy.append(_dot_tn(ds_stack, q_stacks[sw]))
                dv_by_copy.append(_dot_tn(p_stacks[sw], do_stack))
            for p in range(4):
                dq = jnp.where(lo, dqs[p, 0], dqs[p, 1]) * QK_SCALE
                dproj_ref[rows, Q_OFF + p * CHUNK:Q_OFF + (p + 1) * CHUNK] = dq.astype(BF16)
            both = pl.ds(r0, 2 * CHUNK)
            dkv_s[both, 0:KV_WIDTH] += dk_by_copy[0] + _swap_halves(dk_by_copy[1])
            dkv_s[both, KV_WIDTH:2 * KV_WIDTH] += dv_by_copy[0] + _swap_halves(dv_by_copy[1])
            return carry

        lax.fori_loop(0, n_chunks, chunk, 0, unroll=BWD_CHUNK_UNROLL)
        dwout_s[...] += _dot_tn(ycat_s[...], dy_s[...])
        dproj_ref[:, K_OFF:BZ_OFF] = dkv_s[CHUNK:CHUNK + tb, :].astype(BF16)
        carry_s[...] = dkv_s[0:CHUNK, :]

        @pl.when(step == n_tiles - 1)
        def _():
            row = lax.broadcasted_iota(jnp.int32, (2 * CHUNK, CHUNK), 0)
            col = lax.broadcasted_iota(jnp.int32, (2 * CHUNK, CHUNK), 1)
            causal = col <= jnp.where(row >= CHUNK, row - CHUNK, row)
            for p in range(4):
                dws_ref[p] = jnp.where(causal, dws_s[p], 0.0).astype(BF16)
            rows = stage_s.shape[0]
            for r0 in range(0, D_MODEL, rows):
                stage_s[...] = dwout_s[r0:r0 + rows, :].astype(BF16)
                pltpu.sync_copy(stage_s, dwout_ref.at[r0:r0 + rows, :])

        at_end()

    tile = lambda w: pl.BlockSpec((tb, w), lambda s: (n_tiles - 1 - s, 0))
    whole = lambda shape, **kw: pl.BlockSpec(shape, lambda s: (0,) * len(shape), **kw)
    once = dict(pipeline_mode=pl.Buffered(1))
    gate_cols = pl.BlockSpec((tb, A_WIDTH), lambda s: (n_tiles - 1 - s, 2))
    in_specs = [tile(D_MODEL), tile(D_MODEL), gate_cols, tile(B_WIDTH), tile(2 * KV_WIDTH),
                pl.BlockSpec((CHUNK, 2 * KV_WIDTH), lambda s: (jnp.maximum((n_tiles - 1 - s) * n_chunks - 1, 0), 0)),
                tile(B_WIDTH), tile(HEADS * CHUNK), tile(N_STATS * CHUNK), tile(A_WIDTH), tile(4 * A_WIDTH), tile(B_WIDTH),
                whole((1, A_WIDTH)), whole((1, A_WIDTH)), whole((4, CHUNK, 2 * CHUNK)), whole((D_MODEL, D_MODEL), **once),
                whole((1, D_MODEL))]
    args = [dout, y, a, q, kv, kv, bz, probs, stats, mixed, act, att, ln_g, ln_b, wtcat, w_out, g_post]
    small_shapes = [jax.ShapeDtypeStruct((D_MODEL, D_MODEL), BF16), jax.ShapeDtypeStruct((4, 2 * CHUNK, CHUNK), BF16),
                    jax.ShapeDtypeStruct((CHUNK, A_WIDTH), F32), jax.ShapeDtypeStruct((1, A_WIDTH), F32),
                    jax.ShapeDtypeStruct((1, A_WIDTH), F32), jax.ShapeDtypeStruct((1, D_MODEL), F32),
                    jax.ShapeDtypeStruct((HEADS, CHUNK, CHUNK), F32)]
    small_specs = [ANY, whole((4, 2 * CHUNK, CHUNK)), whole((CHUNK, A_WIDTH)), whole((1, A_WIDTH)), whole((1, A_WIDTH)),
                   whole((1, D_MODEL)), whole((HEADS, CHUNK, CHUNK))]
    scratch = [pltpu.VMEM((tb, D_MODEL), BF16), pltpu.VMEM((tb, D_MODEL), BF16), pltpu.VMEM((tb, D_MODEL), F32),
               pltpu.VMEM((tb + CHUNK, 2 * KV_WIDTH), F32), pltpu.VMEM((CHUNK, 2 * KV_WIDTH), F32),
               pltpu.VMEM((D_MODEL, D_MODEL), F32), pltpu.VMEM((2 * CHUNK, D_MODEL), BF16),
               pltpu.VMEM((4, 2 * CHUNK, CHUNK), F32)]
    out_shape = [jax.ShapeDtypeStruct((n_tok, IN_WIDTH), BF16)] + small_shapes
    out_specs = [tile(IN_WIDTH)] + small_specs
    if exchange is not None:
        dwt, dwo = exchange
        in_specs += [ANY, ANY]
        args += [dwt, dwo]
        out_shape += [jax.ShapeDtypeStruct((N_DEV, dwt.shape[0] // N_CHIPS, D_MODEL), BF16),
                      jax.ShapeDtypeStruct((N_DEV, dwo.shape[0] // N_CHIPS, D_MODEL), BF16)]
        out_specs += [ANY, ANY]
    if allgather is not None:
        in_specs.append(ANY)
        args.append(allgather)
        out_shape.append(jax.ShapeDtypeStruct((N_DEV,) + allgather.shape, allgather.dtype))
        out_specs.append(ANY)
    for hosted in (exchange, allgather):
        if hosted is not None:
            scratch += _comm_scratch(N_EXCHANGE_SEMS)
    return pl.pallas_call(
        body, name="bwd_mix" + ("" if exchange is None and allgather is None else "_exchange"), grid=(n_tiles,),
        out_shape=tuple(out_shape), in_specs=in_specs, out_specs=tuple(out_specs), scratch_shapes=scratch,
        compiler_params=_cparams(1),
    )(*args)


def _bwd_in(dproj, x, dout, g_pre, w_in_t):
    n_tok = x.shape[0]
    tm = TOK_TILE
    n_tiles = n_tok // tm

    def body(dp_ref, x_ref, dout_ref, g_ref, wt_ref, dx_ref, dwt_ref, dg_ref, acc_s):
        i = pl.program_id(0)

        @pl.when(i == 0)
        def _():
            acc_s[...] = jnp.zeros_like(acc_s)
            dg_ref[...] = jnp.zeros_like(dg_ref)

        xf = x_ref[...]
        r = lax.rsqrt(jnp.mean(xf * xf, axis=-1, keepdims=True) + NORM_EPS)
        xn = xf * r
        h = (xn * g_ref[...]).astype(BF16)
        dp = dp_ref[...]
        dh = _dot(dp, wt_ref[...])
        acc_s[...] += _dot_tn(dp, h)
        dg_ref[...] += jnp.sum(dh * xn, axis=0, keepdims=True)
        dhn = dh * g_ref[...]
        dx_ref[...] = dout_ref[...] + r * (dhn - xn * jnp.mean(dhn * xn, axis=-1, keepdims=True))

        @pl.when(i == n_tiles - 1)
        def _():
            dwt_ref[...] = acc_s[...].astype(BF16)

    return pl.pallas_call(
        body, name="bwd_in", grid=(n_tiles,),
        out_shape=(jax.ShapeDtypeStruct((n_tok, D_MODEL), F32), jax.ShapeDtypeStruct((IN_WIDTH, D_MODEL), BF16),
                   jax.ShapeDtypeStruct((1, D_MODEL), F32)),
        in_specs=[pl.BlockSpec((tm, IN_WIDTH), lambda i: (i, 0)), pl.BlockSpec((tm, D_MODEL), lambda i: (i, 0)),
                  pl.BlockSpec((tm, D_MODEL), lambda i: (i, 0)), pl.BlockSpec((1, D_MODEL), lambda i: (0, 0)),
                  pl.BlockSpec((IN_WIDTH, D_MODEL), lambda i: (0, 0), pipeline_mode=pl.Buffered(1))],
        out_specs=(pl.BlockSpec((tm, D_MODEL), lambda i: (i, 0)),
                   pl.BlockSpec((IN_WIDTH, D_MODEL), lambda i: (0, 0), pipeline_mode=pl.Buffered(1)),
                   pl.BlockSpec((1, D_MODEL), lambda i: (0, 0))),
        scratch_shapes=[pltpu.VMEM((IN_WIDTH, D_MODEL), F32)],
        compiler_params=_cparams(1),
    )(dproj, x, dout, g_pre, w_in_t)


def _riders(exchanges):
    args = [a for arrays, _ in exchanges for a in arrays]
    shapes = [jax.ShapeDtypeStruct((N_DEV, a.shape[0] // N_CHIPS if by_chip else a.shape[0]) + a.shape[1:], a.dtype)
              for arrays, by_chip in exchanges for a in arrays]
    scratch = [s for _ in exchanges for s in _comm_scratch(N_EXCHANGE_SEMS)]

    def bind(in_refs, out_refs, sem_refs, step, n_steps, pass_on_step=None):
        in_refs, out_refs, sem_refs = list(in_refs), list(out_refs), list(sem_refs)
        starts, ends = [], []
        for arrays, by_chip in exchanges:
            n = len(arrays)
            phases = _exchange_phases(tuple(in_refs[:n]), tuple(out_refs[:n]), *sem_refs[:3], by_chip=by_chip)
            del in_refs[:n], out_refs[:n], sem_refs[:3]
            at_start, at_end = _hosted(phases, step, n_steps, pass_on_step)
            starts.append(at_start)
            ends.append(at_end)
        return (lambda: [f() for f in starts]), (lambda: [f() for f in ends])

    return args, [ANY] * len(args), shapes, [ANY] * len(shapes), scratch, bind


def _bwd_in_dw(dproj, x, g_pre, exchanges):
    n_tok = x.shape[0]
    tm = TOK_TILE
    n_tiles = n_tok // tm
    r_args, r_in_specs, r_shapes, r_out_specs, r_scratch, bind = _riders(exchanges)

    def body(*refs):
        dp_ref, x_ref, g_ref = refs[:3]
        in_refs = refs[3:3 + len(r_args)]
        dwt_ref = refs[3 + len(r_args)]
        out_refs = refs[4 + len(r_args):4 + 2 * len(r_args)]
        acc_s = refs[4 + 2 * len(r_args)]
        i = pl.program_id(0)
        at_start, at_end = bind(in_refs, out_refs, refs[5 + 2 * len(r_args):], i, n_tiles)
        at_start()

        @pl.when(i == 0)
        def _():
            acc_s[...] = jnp.zeros_like(acc_s)

        xf = x_ref[...]
        r = lax.rsqrt(jnp.mean(xf * xf, axis=-1, keepdims=True) + NORM_EPS)
        acc_s[...] += _dot_tn(dp_ref[...], ((xf * r) * g_ref[...]).astype(BF16))

        @pl.when(i == n_tiles - 1)
        def _():
            dwt_ref[...] = acc_s[...].astype(BF16)

        at_end()

    return pl.pallas_call(
        body, name="bwd_in_dw", grid=(n_tiles,),
        out_shape=(jax.ShapeDtypeStruct((IN_WIDTH, D_MODEL), BF16), *r_shapes),
        in_specs=[pl.BlockSpec((tm, IN_WIDTH), lambda i: (i, 0)), pl.BlockSpec((tm, D_MODEL), lambda i: (i, 0)),
                  pl.BlockSpec((1, D_MODEL), lambda i: (0, 0))] + r_in_specs,
        out_specs=(pl.BlockSpec((IN_WIDTH, D_MODEL), lambda i: (0, 0), pipeline_mode=pl.Buffered(1)), *r_out_specs),
        scratch_shapes=[pltpu.VMEM((IN_WIDTH, D_MODEL), F32)] + r_scratch,
        compiler_params=_cparams(1),
    )(dproj, x, g_pre, *r_args)


def _bwd_in_dx(dproj, x, dout, g_pre, w_in_t, exchanges):
    n_tok = x.shape[0]
    tm = TOK_TILE
    n_tiles = n_tok // tm
    r_args, r_in_specs, r_shapes, r_out_specs, r_scratch, bind = _riders(exchanges)

    def body(*refs):
        dp_ref, x_ref, dout_ref, g_ref, wt_ref = refs[:5]
        in_refs = refs[5:5 + len(r_args)]
        dx_ref, dg_ref = refs[5 + len(r_args):7 + len(r_args)]
        out_refs = refs[7 + len(r_args):7 + 2 * len(r_args)]
        i = pl.program_id(0)
        at_start, at_end = bind(in_refs, out_refs, refs[7 + 2 * len(r_args):], i, n_tiles, pass_on_step=n_tiles - 2)
        at_start()

        @pl.when(i == 0)
        def _():
            dg_ref[...] = jnp.zeros_like(dg_ref)

        xf = x_ref[...]
        r = lax.rsqrt(jnp.mean(xf * xf, axis=-1, keepdims=True) + NORM_EPS)
        xn = xf * r
        dh = _dot(dp_ref[...], wt_ref[...])
        dg_ref[...] += jnp.sum(dh * xn, axis=0, keepdims=True)
        dhn = dh * g_ref[...]
        dx_ref[...] = dout_ref[...] + r * (dhn - xn * jnp.mean(dhn * xn, axis=-1, keepdims=True))
        at_end()

    tile = lambda w: pl.BlockSpec((tm, w), lambda i: (i, 0))
    return pl.pallas_call(
        body, name="bwd_in_dx", grid=(n_tiles,),
        out_shape=(jax.ShapeDtypeStruct((n_tok, D_MODEL), F32), jax.ShapeDtypeStruct((1, D_MODEL), F32), *r_shapes),
        in_specs=[tile(IN_WIDTH), tile(D_MODEL), tile(D_MODEL), pl.BlockSpec((1, D_MODEL), lambda i: (0, 0)),
                  pl.BlockSpec((IN_WIDTH, D_MODEL), lambda i: (0, 0), pipeline_mode=pl.Buffered(1))] + r_in_specs,
        out_specs=(tile(D_MODEL), pl.BlockSpec((1, D_MODEL), lambda i: (0, 0)), *r_out_specs),
        scratch_shapes=r_scratch,
        compiler_params=_cparams(1),
    )(dproj, x, dout, g_pre, w_in_t, *r_args)


def _adam_update(w, g, m, v):
    nm = ADAM_B1 * m + (1.0 - ADAM_B1) * g
    nv = ADAM_B2 * v + (1.0 - ADAM_B2) * (g * g)
    m_hat = nm / (1.0 - ADAM_B1 ** ADAM_STEP)
    v_hat = nv / (1.0 - ADAM_B2 ** ADAM_STEP)
    return -ADAM_LR * (m_hat / (jnp.sqrt(v_hat) + ADAM_EPS) + ADAM_WD * w), nm, nv


def _sum_slots(p_ref):
    tot = p_ref[0].astype(F32)
    for d in range(1, N_DEV):
        tot = tot + p_ref[d].astype(F32)
    return tot


def _adamw_parts(parts, w, m, v, name):
    n_layers, n_rows, n_cols = w.shape
    tr = n_rows
    while tr * n_cols > ADAMW_BLOCK_ELEMS:
        tr //= 2
    n_blocks = n_rows // tr

    def body(*refs):
        p_refs = refs[:n_layers]
        w_ref, m_ref, v_ref, g_ref, d_ref, nm_ref, nv_ref = refs[n_layers:]
        for k in range(n_layers):
            @pl.when(pl.program_id(0) == k)
            def _(k=k):
                g = _sum_slots(p_refs[k])
                g_ref[0] = g
                d_ref[0], nm_ref[0], nv_ref[0] = _adam_update(w_ref[0], g, m_ref[0], v_ref[0])

    def part_spec(k):
        return pl.BlockSpec((N_DEV, tr, n_cols), lambda l, i: (0, jnp.clip(i + (l - k) * n_blocks, 0, n_blocks - 1), 0))

    spec = pl.BlockSpec((1, tr, n_cols), lambda l, i: (l, i, 0))
    shape = jax.ShapeDtypeStruct((n_layers, n_rows, n_cols), F32)
    return pl.pallas_call(
        body, name="adamw_" + name, grid=(n_layers, n_blocks), out_shape=(shape,) * 4,
        in_specs=[part_spec(k) for k in range(n_layers)] + [spec, spec, spec], out_specs=(spec,) * 4,
        compiler_params=_cparams(2),
    )(*parts, w, m, v)


def _adamw_small(quads):
    n = len(quads)

    def body(*refs):
        ins, outs = refs[:4 * n], refs[4 * n:]
        for j in range(n):
            w_ref, g_ref, m_ref, v_ref = ins[4 * j:4 * j + 4]
            d_ref, nm_ref, nv_ref = outs[3 * j:3 * j + 3]
            d_ref[...], nm_ref[...], nv_ref[...] = _adam_update(w_ref[...], g_ref[...], m_ref[...], v_ref[...])

    vmem = pl.BlockSpec(memory_space=pltpu.VMEM)
    shapes = tuple(jax.ShapeDtypeStruct(w.shape, F32) for w, *_ in quads for _ in range(3))
    results = pl.pallas_call(
        body, name="adamw_small", out_shape=shapes, in_specs=[vmem] * (4 * n), out_specs=(vmem,) * (3 * n),
    )(*[a for quad in quads for a in quad])
    return [results[3 * j:3 * j + 3] for j in range(n)]


_SMALL = ("pre_norm_g", "ln_v_g", "ln_v_b", "b_spatial", "sinks", "rel_bias", "post_norm_g")


def _pack(pieces):
    blocks, first_rows, n = [], [], 0
    for p in pieces:
        flat = p.reshape(-1)
        flat = jnp.concatenate([flat, jnp.zeros(((-flat.shape[0]) % (8 * 128),), F32)]).reshape(-1, 128)
        blocks.append(flat)
        first_rows.append(n)
        n += flat.shape[0]
    return jnp.concatenate(blocks, axis=0), first_rows


def _unpack(block, first_row, shape):
    size = math.prod(shape)
    return block[first_row:first_row + -(-size // 128)].reshape(-1)[:size].reshape(shape)


def kernel(x, pre_norm_g, w_in, ln_v_g, ln_v_b, w_spatial, b_spatial, sinks, rel_bias, w_out, post_norm_g, loss_target, m_pre_norm_g, m_w_in, m_ln_v_g, m_ln_v_b, m_w_spatial, m_b_spatial, m_sinks, m_rel_bias, m_w_out, m_post_norm_g, v_pre_norm_g, v_w_in, v_ln_v_g, v_ln_v_b, v_w_spatial, v_b_spatial, v_sinks, v_rel_bias, v_w_out, v_post_norm_g):
    weights = dict(pre_norm_g=pre_norm_g, w_in=w_in, ln_v_g=ln_v_g, ln_v_b=ln_v_b, w_spatial=w_spatial, b_spatial=b_spatial,
                   sinks=sinks, rel_bias=rel_bias, w_out=w_out, post_norm_g=post_norm_g)
    mom_m = dict(pre_norm_g=m_pre_norm_g, w_in=m_w_in, ln_v_g=m_ln_v_g, ln_v_b=m_ln_v_b, w_spatial=m_w_spatial,
                 b_spatial=m_b_spatial, sinks=m_sinks, rel_bias=m_rel_bias, w_out=m_w_out, post_norm_g=m_post_norm_g)
    mom_v = dict(pre_norm_g=v_pre_norm_g, w_in=v_w_in, ln_v_g=v_ln_v_g, ln_v_b=v_ln_v_b, w_spatial=v_w_spatial,
                 b_spatial=v_b_spatial, sinks=v_sinks, rel_bias=v_rel_bias, w_out=v_w_out, post_norm_g=v_post_norm_g)
    n_seq, seq_len, _ = x.shape
    n_layers = w_in.shape[0]
    x2 = x.reshape(n_seq * seq_len, D_MODEL)
    target2 = loss_target.reshape(n_seq * seq_len, D_MODEL)
    row = lambda p, l: p[l][None]

    wt_shards = jnp.swapaxes(w_in, 1, 2).astype(BF16)
    wo_shards = w_out.astype(BF16)
    bucket = _window_buckets()
    bias = _bias_table(rel_bias, bucket)
    wcat, wtcat, bs = _spatial_tables(w_spatial, jnp.swapaxes(b_spatial, 1, 2))

    wt, wo = [None] * n_layers, [None] * n_layers
    wt[0], wo[0] = _gather_weights(wt_shards, wo_shards, 0)
    xs, saved = [x2], []
    for l in range(n_layers):
        layer_args = (xs[-1], row(pre_norm_g, l), wt[l], row(ln_v_g, l), row(ln_v_b, l), wcat[l], bs[l], sinks[l], bias,
                      wo[l], row(post_norm_g, l), seq_len)
        if l + 1 < n_layers:
            *acts, xn, y, wt[l + 1], wo[l + 1] = _fwd_layer(*layer_args, gather=(wt_shards, wo_shards, l + 1))
            xs.append(xn)
        else:
            *acts, dx, y, loss = _fwd_layer(*layer_args, target=target2)
        saved.append((y, *acts))

    small = [None] * n_layers
    parts = [None] * n_layers
    waiting = None
    for l in reversed(range(n_layers)):
        hosted = {} if waiting is None else dict(exchange=waiting[:2], allgather=waiting[2])
        dproj, *rest = _bwd_mix(dx, *saved[l], row(ln_v_g, l), row(ln_v_b, l), wtcat[l], wo[l], row(post_norm_g, l), **hosted)
        dwo, dws, dmix, dlng, dlnb, dgpost, dbias = rest[:7]
        dws = dws.reshape(HEADS * CHUNK, CHUNK)
        if waiting is not None:
            parts[l + 1] = list(rest[7:])
        if l > 0:
            dx, dwt, dgpre = _bwd_in(dproj, xs[l], dx, row(pre_norm_g, l), wt[l])
            waiting = (dwt, dwo, dws)
        else:
            dwt, po, ws_all = _bwd_in_dw(dproj, xs[l], row(pre_norm_g, l), [((dwo,), True), ((dws,), False)])
            dx, dgpre, pt = _bwd_in_dx(dproj, xs[l], dx, row(pre_norm_g, l), wt[l], [((dwt,), True)])
            parts[l] = [pt, po, ws_all]
        small[l] = dict(pre_norm_g=dgpre[0], ln_v_g=dlng[0], ln_v_b=dlnb[0], dmix=dmix, dbias=dbias,
                        post_norm_g=dgpost[0])

    db = _b_spatial_grad([s["dmix"] for s in small])
    pieces = {n: jnp.stack([s[n] for s in small]) for n in ("pre_norm_g", "ln_v_g", "ln_v_b", "post_norm_g")}
    pieces["b_spatial"] = jnp.swapaxes(db[:, :, :HEADS], 1, 2)
    pieces["rel_bias"], pieces["sinks"] = _rel_bias_grad([s["dbias"] for s in small], bucket)
    packed, first_rows = _pack([pieces[n] for n in _SMALL] + [loss])
    total = _allreduce_small(packed)
    grad = {n: _unpack(total, r, weights[n].shape) for n, r in zip(_SMALL, first_rows)}
    loss_out = total[first_rows[-1], 0]

    delta, new_m, new_v = {}, {}, {}
    two_d = lambda p: p.reshape(-1, p.shape[-1])
    updates = _adamw_small([tuple(two_d(p[n]) for p in (weights, grad, mom_m, mom_v)) for n in _SMALL])
    for n, update in zip(_SMALL, updates):
        delta[n], new_m[n], new_v[n] = (u.reshape(weights[n].shape) for u in update)

    t3 = lambda p: jnp.swapaxes(p, 1, 2)
    flat3 = lambda p: p.reshape(n_layers, HEADS * CHUNK, CHUNK)
    for n, which, view, back in (("w_in", 0, t3, t3), ("w_out", 1, lambda p: p, lambda p: p),
                                 ("w_spatial", 2, flat3, lambda p: p.reshape(w_spatial.shape))):
        results = _adamw_parts([parts[l][which] for l in range(n_layers)], view(weights[n]), view(mom_m[n]),
                               view(mom_v[n]), n)
        grad[n], delta[n], new_m[n], new_v[n] = (back(r) for r in results)

    names = tuple(weights)
    return (loss_out, dx.reshape(x.shape), *[grad[n] for n in names], *[delta[n] for n in names],
            *[new_m[n] for n in names], *[new_v[n] for n in names])
```

```python
import math

import jax
import jax.numpy as jnp
from jax import lax
from jax.experimental import pallas as pl
from jax.experimental.pallas import tpu as pltpu

F32 = jnp.float32
BF16 = jnp.bfloat16

D_MODEL = 1024
A_WIDTH = 512
B_WIDTH = 512
KV_WIDTH = 128
IN_WIDTH = 3 * A_WIDTH + 2 * B_WIDTH + 2 * KV_WIDTH
CHUNK = 128
HEADS = 8
HEAD_DIM = 64
REL_BUCKETS = 32
NORM_EPS = 1e-6
NEG = -1e30
Q_OFF = 3 * A_WIDTH
K_OFF = Q_OFF + B_WIDTH
BZ_OFF = K_OFF + 2 * KV_WIDTH
QK_SCALE = HEAD_DIM ** -0.5

ADAM_LR = 0.001
ADAM_B1 = 0.9
ADAM_B2 = 0.999
ADAM_EPS = 1e-08
ADAM_WD = 0.01
ADAM_STEP = 10

TOK_TILE = 512
CHUNK_UNROLL = 4
BWD_CHUNK_UNROLL = 4
ADAMW_BLOCK_ELEMS = 192 * 1024
N_RING = 3
VMEM_LIMIT_V7X = 60 * 1024 * 1024

N_DEV = 8
N_CHIPS = 4
MESH_ID = pl.DeviceIdType.MESH
ANY = pl.BlockSpec(memory_space=pl.ANY)
SMEM = pl.BlockSpec(memory_space=pltpu.SMEM)


def _cparams(n_axes):
    return pltpu.CompilerParams(dimension_semantics=("arbitrary",) * n_axes, vmem_limit_bytes=VMEM_LIMIT_V7X)


_GELU_C = math.sqrt(2.0 / math.pi)
_GELU_C3 = _GELU_C * 0.044715


def _gelu_and_grad(x):
    x2 = x * x
    t = jnp.tanh(x * (_GELU_C3 * x2 + _GELU_C))
    cdf = 0.5 * t + 0.5
    d = cdf + (x * (cdf * (1.0 - cdf))) * ((6.0 * _GELU_C3) * x2 + 2.0 * _GELU_C)
    return x * cdf, d


def _sigmoid(x):
    return 0.5 + 0.5 * jnp.tanh(0.5 * x)


def _silu(x):
    return x * _sigmoid(x)


def _silu_and_grad(x):
    s = _sigmoid(x)
    return x * s, s * (1.0 + x * (1.0 - s))


def _dot(a, b):
    return jnp.dot(a, b, preferred_element_type=F32)


def _dot_nt(a, b):
    return lax.dot_general(a, b, (((1,), (1,)), ((), ())), preferred_element_type=F32)


def _dot_tn(a, b):
    return lax.dot_general(a, b, (((0,), (0,)), ((), ())), preferred_element_type=F32)


def _lo_mask(shape):
    return lax.broadcasted_iota(jnp.int32, shape, 1) < HEAD_DIM


def _swap_halves(v):
    return pltpu.roll(v, HEAD_DIM, 1)


def _window_buckets():
    q_loc = jnp.arange(CHUNK)[:, None]
    j_loc = jnp.arange(CHUNK)[None, :]
    d = q_loc - j_loc + jnp.where(j_loc > q_loc, CHUNK, 0)
    max_exact = REL_BUCKETS // 2
    safe = jnp.maximum(d, 1).astype(F32)
    large = max_exact + (jnp.log(safe / max_exact) / math.log(CHUNK / max_exact)
                         * (REL_BUCKETS - max_exact)).astype(jnp.int32)
    large = jnp.minimum(large, REL_BUCKETS - 1)
    return jnp.where(d < max_exact, d, large).astype(jnp.int32)


def _bias_table(rel_bias, bucket):
    def body(rb_ref, bk_ref, out_ref):
        bk = bk_ref[...]
        for h in range(HEADS):
            acc = jnp.zeros(bk.shape, F32)
            for b in range(REL_BUCKETS):
                acc = jnp.where(bk == b, rb_ref[b, h], acc)
            out_ref[h] = acc

    vmem = pl.BlockSpec(memory_space=pltpu.VMEM)
    return pl.pallas_call(
        body, name="bias_table",
        out_shape=jax.ShapeDtypeStruct((HEADS, CHUNK, CHUNK), F32),
        in_specs=[SMEM, vmem], out_specs=vmem,
    )(rel_bias, bucket)


def _rel_bias_grad(dbias, bucket):
    n_layers = len(dbias)

    def body(*refs):
        db_refs, (bk_ref, out_ref, dsink_ref) = refs[:n_layers], refs[n_layers:]
        bk = bk_ref[...]
        for h in range(HEADS):
            tot = jnp.zeros((CHUNK, CHUNK), F32)
            for l in range(n_layers):
                ds = db_refs[l][h]
                dsink_ref[l, h] = -jnp.sum(ds)
                tot = tot + ds
            for b in range(REL_BUCKETS):
                out_ref[b, h] = jnp.sum(jnp.where(bk == b, tot, 0.0))

    vmem = pl.BlockSpec(memory_space=pltpu.VMEM)
    return pl.pallas_call(
        body, name="rel_bias_grad",
        out_shape=(jax.ShapeDtypeStruct((REL_BUCKETS, HEADS), F32), jax.ShapeDtypeStruct((n_layers, HEADS), F32)),
        in_specs=[vmem] * (n_layers + 1), out_specs=(SMEM, SMEM),
    )(*dbias, bucket)


def _spatial_tables(w_spatial, b_spatial_t):
    n_layers = w_spatial.shape[0]

    def body(w_ref, b_ref, wcat_ref, wtcat_ref, bs_ref):
        row = lax.broadcasted_iota(jnp.int32, (CHUNK, CHUNK), 0)
        col = lax.broadcasted_iota(jnp.int32, (CHUNK, CHUNK), 1)
        causal = col <= row
        lo = _lo_mask((CHUNK, CHUNK))
        for p in range(4):
            for half in range(2):
                w = jnp.where(causal, w_ref[0, 2 * p + half], 0.0)
                wcat_ref[0, p, :, half * CHUNK:(half + 1) * CHUNK] = w.astype(BF16)
                wtcat_ref[0, p, :, half * CHUNK:(half + 1) * CHUNK] = w.T.astype(BF16)
            b = b_ref[0]
            bs_ref[0, p] = jnp.where(lo, b[:, 2 * p:2 * p + 1], b[:, 2 * p + 1:2 * p + 2])

    return pl.pallas_call(
        body, name="spatial_tables", grid=(n_layers,),
        out_shape=(jax.ShapeDtypeStruct((n_layers, 4, CHUNK, 2 * CHUNK), BF16),
                   jax.ShapeDtypeStruct((n_layers, 4, CHUNK, 2 * CHUNK), BF16),
                   jax.ShapeDtypeStruct((n_layers, 4, CHUNK, CHUNK), F32)),
        in_specs=[pl.BlockSpec((1, HEADS, CHUNK, CHUNK), lambda l: (l, 0, 0, 0)),
                  pl.BlockSpec((1, CHUNK, HEADS), lambda l: (l, 0, 0))],
        out_specs=(pl.BlockSpec((1, 4, CHUNK, 2 * CHUNK), lambda l: (l, 0, 0, 0)),
                   pl.BlockSpec((1, 4, CHUNK, 2 * CHUNK), lambda l: (l, 0, 0, 0)),
                   pl.BlockSpec((1, 4, CHUNK, CHUNK), lambda l: (l, 0, 0, 0))),
        compiler_params=_cparams(1),
    )(w_spatial, b_spatial_t)


def _b_spatial_grad(dmix):
    n_layers = len(dmix)

    def body(*refs):
        d_refs, out_ref = refs[:n_layers], refs[n_layers]
        lane = lax.broadcasted_iota(jnp.int32, (CHUNK, CHUNK), 1)
        for l in range(n_layers):
            acc = jnp.zeros((CHUNK, CHUNK), F32)
            for p in range(4):
                t = d_refs[l][:, p * CHUNK:(p + 1) * CHUNK]
                s_lo = jnp.sum(jnp.where(lane < HEAD_DIM, t, 0.0), axis=1, keepdims=True)
                s_hi = jnp.sum(jnp.where(lane < HEAD_DIM, 0.0, t), axis=1, keepdims=True)
                acc = jnp.where(lane == 2 * p, s_lo, acc)
                acc = jnp.where(lane == 2 * p + 1, s_hi, acc)
            out_ref[l] = acc

    vmem = pl.BlockSpec(memory_space=pltpu.VMEM)
    return pl.pallas_call(
        body, name="b_spatial_grad",
        out_shape=jax.ShapeDtypeStruct((n_layers, CHUNK, CHUNK), F32),
        in_specs=[vmem] * n_layers, out_specs=vmem,
    )(*dmix)


def _place():
    x, y, c = lax.axis_index("x"), lax.axis_index("y"), lax.axis_index("c")
    other_chips = [(1 - x, y), (x, 1 - y), (1 - x, 1 - y)]
    return x, y, c, other_chips


N_GATHER_SEMS = 12


def _gather_phases(shards, fulls, send_sems, recv_sems, local_sems):
    x, y, c, chips = _place()
    sibling = (x, y, 1 - c)
    n_arr = len(shards)

    def half_rows(a, chip, half):
        n = shards[a].shape[0]
        start = (2 * chip[0] + chip[1]) * n + half * (n // 2)
        return fulls[a].at[pl.ds(pl.multiple_of(start, 16), n // 2), :]

    def my_half(a):
        n = shards[a].shape[0]
        return shards[a].at[pl.ds(pl.multiple_of(c * (n // 2), 16), n // 2), :]

    def copy(k, a, src, chip, half, to):
        return pltpu.make_async_remote_copy(
            src_ref=src, dst_ref=half_rows(a, chip, half), send_sem=send_sems.at[n_arr * k + a],
            recv_sem=recv_sems.at[n_arr * k + a], device_id=to, device_id_type=MESH_ID)

    def local(a):
        n = shards[a].shape[0]
        mine = fulls[a].at[pl.ds(pl.multiple_of((2 * x + y) * n, 16), n), :]
        return pltpu.make_async_copy(shards[a], mine, local_sems.at[a])

    def first(k, a):
        return copy(k, a, my_half(a), (x, y), c, (chips[k][0], chips[k][1], c))

    def passed(k, a):
        return copy(3 + k, a, half_rows(a, chips[k], c), chips[k], c, sibling)

    def phase_a():
        for a in range(n_arr):
            local(a).start()
        for k in range(3):
            for a in range(n_arr):
                first(k, a).start()

    def phase_b():
        for k in range(3):
            for a in range(n_arr):
                copy(k, a, my_half(a), chips[k], c, sibling).wait_recv()
                passed(k, a).start()

    def phase_c():
        for k in range(3):
            for a in range(n_arr):
                copy(3 + k, a, my_half(a), chips[k], 1 - c, sibling).wait_recv()
        for k in range(3):
            for a in range(n_arr):
                first(k, a).wait_send()
                passed(k, a).wait_send()
        for a in range(n_arr):
            local(a).wait()

    return phase_a, phase_b, phase_c


N_EXCHANGE_SEMS = 14


def _exchange_phases(partials, parts, send_sems, recv_sems, local_sems, by_chip=True):
    x, y, c, chips = _place()
    me, sibling = (x, y, c), (x, y, 1 - c)
    n_arr = len(partials)

    def block(a, chip):
        if not by_chip:
            return partials[a]
        n = partials[a].shape[0] // N_CHIPS
        return partials[a].at[pl.ds(pl.multiple_of((2 * chip[0] + chip[1]) * n, 16), n), :]

    def slot(a, dev):
        return parts[a].at[4 * dev[0] + 2 * dev[1] + dev[2]]

    def copy(k, a, src, origin, to):
        return pltpu.make_async_remote_copy(
            src_ref=src, dst_ref=slot(a, origin), send_sem=send_sems.at[n_arr * k + a],
            recv_sem=recv_sems.at[n_arr * k + a], device_id=to, device_id_type=MESH_ID)

    def local(a):
        return pltpu.make_async_copy(block(a, (x, y)), slot(a, me), local_sems.at[a])

    def first(k, a):
        if k == 0:
            return copy(0, a, block(a, (x, y)), me, sibling)
        chip = chips[k - 1]
        return copy(k, a, block(a, chip), me, (chip[0], chip[1], c))

    def passed(k, a):
        origin = (chips[k][0], chips[k][1], c)
        return copy(4 + k, a, slot(a, origin), origin, sibling)

    def phase_a():
        for a in range(n_arr):
            local(a).start()
        for k in range(4):
            for a in range(n_arr):
                first(k, a).start()

    def phase_b():
        for k in range(3):
            for a in range(n_arr):
                copy(1 + k, a, block(a, (x, y)), (chips[k][0], chips[k][1], c), me).wait_recv()
                passed(k, a).start()

    def phase_c():
        for a in range(n_arr):
            copy(0, a, block(a, (x, y)), sibling, me).wait_recv()
        for k in range(3):
            for a in range(n_arr):
                copy(4 + k, a, block(a, (x, y)), (chips[k][0], chips[k][1], 1 - c), me).wait_recv()
        for k in range(4):
            for a in range(n_arr):
                first(k, a).wait_send()
        for k in range(3):
            for a in range(n_arr):
                passed(k, a).wait_send()
        for a in range(n_arr):
            local(a).wait()

    return phase_a, phase_b, phase_c


def _comm_scratch(n_sems):
    return [pltpu.SemaphoreType.DMA((n_sems,)), pltpu.SemaphoreType.DMA((n_sems,)), pltpu.SemaphoreType.DMA((2,))]


def _gather_weights(wt_shards, wo_shards, layer):
    wt_rows, wo_rows = wt_shards.shape[1], wo_shards.shape[1]

    def body(wt_ref, wo_ref, wt_full, wo_full, send_sems, recv_sems, local_sems):
        phases = _gather_phases((wt_ref.at[layer], wo_ref.at[layer]), (wt_full, wo_full), send_sems, recv_sems, local_sems)
        for phase in phases:
            phase()

    return pl.pallas_call(
        body, name="gather_weights",
        out_shape=(jax.ShapeDtypeStruct((N_CHIPS * wt_rows, D_MODEL), BF16),
                   jax.ShapeDtypeStruct((N_CHIPS * wo_rows, D_MODEL), BF16)),
        in_specs=[ANY, ANY], out_specs=(ANY, ANY), scratch_shapes=_comm_scratch(N_GATHER_SEMS),
    )(wt_shards, wo_shards)


def _allreduce_small(part):
    n_rows = part.shape[0]

    def body(p_ref, tot_ref, all_ref, send_sems, recv_sems, local_sem):
        x, y, c, chips = _place()
        me, sibling = (x, y, c), (x, y, 1 - c)

        def rows(dev):
            return all_ref.at[pl.ds(pl.multiple_of((4 * dev[0] + 2 * dev[1] + dev[2]) * n_rows, 8), n_rows), :]

        def copy(k, origin, to, src=None):
            return pltpu.make_async_remote_copy(
                src_ref=rows(origin) if src is None else src, dst_ref=rows(origin), send_sem=send_sems.at[k],
                recv_sem=recv_sems.at[k], device_id=to, device_id_type=MESH_ID)

        mine = pltpu.make_async_copy(p_ref, rows(me), local_sem)
        mine.start()
        first = [copy(0, me, sibling, src=p_ref)]
        first += [copy(1 + k, me, (chip[0], chip[1], c), src=p_ref) for k, chip in enumerate(chips)]
        for cp in first:
            cp.start()
        passed = []
        for k, chip in enumerate(chips):
            origin = (chip[0], chip[1], c)
            copy(1 + k, origin, me).wait_recv()
            fwd = copy(4 + k, origin, sibling)
            fwd.start()
            passed.append(fwd)
        copy(0, sibling, me).wait_recv()
        for k, chip in enumerate(chips):
            copy(4 + k, (chip[0], chip[1], 1 - c), me).wait_recv()
        for cp in first + passed:
            cp.wait_send()
        mine.wait()
        tot = all_ref[0:n_rows, :]
        for d in range(1, N_DEV):
            tot = tot + all_ref[d * n_rows:(d + 1) * n_rows, :]
        tot_ref[...] = tot

    vmem = pl.BlockSpec(memory_space=pltpu.VMEM)
    return pl.pallas_call(
        body, name="allreduce_small",
        out_shape=jax.ShapeDtypeStruct((n_rows, 128), F32),
        in_specs=[vmem], out_specs=vmem,
        scratch_shapes=[pltpu.VMEM((N_DEV * n_rows, 128), F32), pltpu.SemaphoreType.DMA((7,)),
                        pltpu.SemaphoreType.DMA((7,)), pltpu.SemaphoreType.DMA],
        compiler_params=pltpu.CompilerParams(vmem_limit_bytes=VMEM_LIMIT_V7X),
    )(part)


def _hosted(phases, step, n_steps, pass_on_step=None):
    phase_a, phase_b, phase_c = phases
    if pass_on_step is None:
        pass_on_step = (3 * n_steps) // 4

    def at_start():
        pl.when(step == 0)(phase_a)

    def at_end():
        pl.when(step == pass_on_step)(phase_b)
        pl.when(step == n_steps - 1)(phase_c)

    return at_start, at_end


def _layer_norm_stats(vv):
    mu = jnp.mean(vv, axis=-1, keepdims=True)
    xc = vv - mu
    rs = lax.rsqrt(jnp.mean(xc * xc, axis=-1, keepdims=True) + NORM_EPS)
    return xc * rs, rs


STAT_POST_RMS, STAT_LN_RSTD = (slice(k * CHUNK, (k + 1) * CHUNK) for k in range(2))
N_STATS = 2


def _over_lanes(stat, width):
    return jnp.concatenate([stat] * (width // CHUNK), axis=1)


def _blockdiag(v, lo):
    zero = jnp.zeros_like(v)
    return jnp.concatenate([jnp.where(lo, v, zero), jnp.where(lo, zero, v)], axis=0)


def _softmax_sink(s, sink):
    m = jnp.maximum(jnp.max(s, axis=-1, keepdims=True), sink)
    e = jnp.exp(s - m)
    esink = jnp.exp(sink - m)
    den = jnp.sum(e, axis=-1, keepdims=True) + esink
    return e / den


def _kv_rows(cur_ref, halo_ref, r0, c):
    prev_in_tile = cur_ref[pl.ds(pl.multiple_of(jnp.maximum(r0 - CHUNK, 0), CHUNK), CHUNK), :]
    prev = jnp.where(c == 0, halo_ref[...], prev_in_tile)
    kv2 = jnp.concatenate([prev, cur_ref[pl.ds(r0, CHUNK), :]], axis=0)
    k2, v2 = kv2[:, 0:KV_WIDTH], kv2[:, KV_WIDTH:2 * KV_WIDTH]
    return (k2, _swap_halves(k2)), (v2, _swap_halves(v2))


def _window_square(over_keys, prev):
    return jnp.where(prev, over_keys[:, 0:CHUNK], over_keys[:, CHUNK:2 * CHUNK])


def _window_keys(square, prev):
    zero = jnp.zeros_like(square)
    return jnp.concatenate([jnp.where(prev, square, zero), jnp.where(prev, zero, square)], axis=1)


def _ahead(shape=(CHUNK, CHUNK)):
    return lax.broadcasted_iota(jnp.int32, shape, 1) - lax.broadcasted_iota(jnp.int32, shape, 0)


def _dead_mask(first, ahead):
    return ahead > jnp.where(first, 0, CHUNK)


def _head_of(p, half):
    return 2 * p + half, int(half != p // 2)


_HEADS_BY_COPY = tuple(tuple((p, half) for p in range(4) for half in range(2) if _head_of(p, half)[1] == sw)
                       for sw in range(2))


def _masked_halves(tile, lo):
    zero = jnp.zeros_like(tile)
    return {0: jnp.where(lo, tile, zero), 1: jnp.where(lo, zero, tile)}


def _query_stacks(q_tiles, lo):
    qm = {p: _masked_halves(q_tiles[p] * QK_SCALE, lo) for p in range(4)}
    return [jnp.concatenate([qm[p][half] for p, half in members], axis=0) for members in _HEADS_BY_COPY]


def _attention_probs(q_stacks, ks, bias_ref, sink_ref, dead, prev):
    probs = {}
    for sw, members in enumerate(_HEADS_BY_COPY):
        s_stack = _dot_nt(q_stacks[sw], ks[sw])
        for i, (p, half) in enumerate(members):
            head = 2 * p + half
            s = _window_square(s_stack[i * CHUNK:(i + 1) * CHUNK, :], prev) + bias_ref[head]
            s = jnp.where(dead, NEG, s)
            probs[p, half] = _softmax_sink(s, sink_ref[head]).astype(BF16)
    return probs


def _attention_values(probs, vs, prev):
    outs, p_stacks = {}, []
    for sw, members in enumerate(_HEADS_BY_COPY):
        p_stack = jnp.concatenate([_window_keys(probs[m], prev) for m in members], axis=0)
        r_stack = _dot(p_stack, vs[sw])
        p_stacks.append(p_stack)
        for i, m in enumerate(members):
            outs[m] = r_stack[i * CHUNK:(i + 1) * CHUNK, :]
    return outs, p_stacks


def _fwd_layer(x, g_pre, w_in_t, ln_g, ln_b, wcat, bs, sinks, bias, w_out, g_post, seq_len, gather=None, target=None):
    n_tok = x.shape[0]
    tb = TOK_TILE
    n_chunks = tb // CHUNK
    n_tiles = n_tok // tb
    n_in = 11
    assert gather is None or target is None

    def body(*refs):
        (x_ref, gpre_ref, wt_ref, lng_ref, lnb_ref, wcat_ref, bs_ref, sink_ref, bias_ref, wout_ref,
         gpost_ref) = refs[:n_in]
        at_start = at_end = lambda: None
        if gather is not None:
            (wts_ref, wos_ref, a_ref, q_ref, kv_ref, bz_ref, probs_ref, stats_ref, mixed_ref, act_ref, att_ref, xn_ref, y_ref,
             wt_full, wo_full, ycat_s, halo_ref, send_sems, recv_sems, local_sems) = refs[n_in:]
            phases = _gather_phases((wts_ref.at[gather[2]], wos_ref.at[gather[2]]), (wt_full, wo_full), send_sems,
                                    recv_sems, local_sems)
            at_start, at_end = _hosted(phases, pl.program_id(0), n_tiles)
        elif target is not None:
            (target_ref, a_ref, q_ref, kv_ref, bz_ref, probs_ref, stats_ref, mixed_ref, act_ref, att_ref, xn_ref, y_ref,
             loss_ref, ycat_s, halo_ref) = refs[n_in:]
        else:
            (a_ref, q_ref, kv_ref, bz_ref, probs_ref, stats_ref, mixed_ref, act_ref, att_ref, xn_ref, y_ref, ycat_s,
             halo_ref) = refs[n_in:]
        at_start()
        i = pl.program_id(0)
        lo = _lo_mask((CHUNK, CHUNK))
        ahead = _ahead()
        prev = ahead > 0

        @pl.when(i == 0)
        def _():
            halo_ref[...] = jnp.zeros_like(halo_ref)

        xf = x_ref[...]
        r1 = lax.rsqrt(jnp.mean(xf * xf, axis=-1, keepdims=True) + NORM_EPS)
        h = ((xf * r1) * gpre_ref[...]).astype(BF16)
        a_ref[...] = _dot_nt(h, wt_ref[0:Q_OFF, :])
        q_ref[...] = _dot_nt(h, wt_ref[Q_OFF:K_OFF, :]).astype(BF16)
        kv_ref[...] = _dot_nt(h, wt_ref[K_OFF:BZ_OFF, :]).astype(BF16)
        bz_ref[...] = _dot_nt(h, wt_ref[BZ_OFF:IN_WIDTH, :])

        def chunk(c, carry):
            r0 = pl.multiple_of(c * CHUNK, CHUNK)
            rows = pl.ds(r0, CHUNK)
            u, gu = _gelu_and_grad(a_ref[rows, 0:A_WIDTH])
            vv, gv = _gelu_and_grad(a_ref[rows, A_WIDTH:2 * A_WIDTH])
            xhat, rs = _layer_norm_stats(vv)
            for k, act in enumerate((u, gu, xhat, gv)):
                act_ref[rows, k * A_WIDTH:(k + 1) * A_WIDTH] = act
            stats_ref[rows, STAT_LN_RSTD] = jnp.broadcast_to(rs, (CHUNK, CHUNK))
            vnb = (xhat * lng_ref[...] + lnb_ref[...]).astype(BF16)
            for p in range(4):
                blk = slice(p * CHUNK, (p + 1) * CHUNK)
                mixed = _dot(wcat_ref[p], _blockdiag(vnb[:, blk], lo)) + bs_ref[p]
                mixed_ref[rows, blk] = mixed
                sz = _silu(a_ref[rows, 2 * A_WIDTH + p * CHUNK:2 * A_WIDTH + (p + 1) * CHUNK])
                ycat_s[rows, blk] = ((u[:, blk] * mixed) * sz).astype(BF16)
            ks, vs = _kv_rows(kv_ref, halo_ref, r0, c)
            dead = _dead_mask(lax.rem(i * tb + r0, seq_len) == 0, ahead)
            q_tiles = [q_ref[rows, p * CHUNK:(p + 1) * CHUNK] for p in range(4)]
            probs = _attention_probs(_query_stacks(q_tiles, lo), ks, bias_ref, sink_ref, dead, prev)
            for (p, half), prob in probs.items():
                head = 2 * p + half
                probs_ref[rows, head * CHUNK:(head + 1) * CHUNK] = prob
            outs, _ = _attention_values(probs, vs, prev)
            for p in range(4):
                blk = slice(p * CHUNK, (p + 1) * CHUNK)
                o = jnp.where(lo, outs[p, 0], outs[p, 1])
                att_ref[rows, blk] = o
                ycat_s[rows, B_WIDTH + p * CHUNK:B_WIDTH + (p + 1) * CHUNK] = (o * _silu(bz_ref[rows, blk])).astype(BF16)
            return carry

        lax.fori_loop(0, n_chunks, chunk, 0, unroll=CHUNK_UNROLL)
        halo_ref[...] = kv_ref[tb - CHUNK:tb, :]
        y = _dot(ycat_s[...], wout_ref[...])
        r = lax.rsqrt(jnp.mean(y * y, axis=-1, keepdims=True) + NORM_EPS)
        yn = y * r
        y_ref[...] = yn
        stats_ref[:, STAT_POST_RMS] = jnp.broadcast_to(r, (tb, CHUNK))
        xn = x_ref[...] + yn * gpost_ref[...]
        if target is None:
            xn_ref[...] = xn
        else:
            d = xn - target_ref[...]
            xn_ref[...] = d * (1.0 / D_MODEL)

            @pl.when(i == 0)
            def _():
                loss_ref[0, 0] = 0.0

            loss_ref[0, 0] += 0.5 * jnp.sum(jnp.mean(d * d, axis=-1, keepdims=True))
        at_end()

    tile = lambda w: pl.BlockSpec((tb, w), lambda i: (i, 0))
    whole = lambda shape, **kw: pl.BlockSpec(shape, lambda i: (0,) * len(shape), **kw)
    in_specs = [tile(D_MODEL), whole((1, D_MODEL)), whole((IN_WIDTH, D_MODEL), pipeline_mode=pl.Buffered(1)),
                whole((1, A_WIDTH)), whole((1, A_WIDTH)), whole((4, CHUNK, 2 * CHUNK)), whole((4, CHUNK, CHUNK)), SMEM,
                whole((HEADS, CHUNK, CHUNK)), whole((D_MODEL, D_MODEL)), whole((1, D_MODEL))]
    out_shape = [jax.ShapeDtypeStruct((n_tok, Q_OFF), F32), jax.ShapeDtypeStruct((n_tok, B_WIDTH), BF16),
                 jax.ShapeDtypeStruct((n_tok, 2 * KV_WIDTH), BF16), jax.ShapeDtypeStruct((n_tok, B_WIDTH), F32),
                 jax.ShapeDtypeStruct((n_tok, HEADS * CHUNK), BF16), jax.ShapeDtypeStruct((n_tok, N_STATS * CHUNK), F32),
                 jax.ShapeDtypeStruct((n_tok, A_WIDTH), F32), jax.ShapeDtypeStruct((n_tok, 4 * A_WIDTH), F32),
                 jax.ShapeDtypeStruct((n_tok, B_WIDTH), F32),
                 jax.ShapeDtypeStruct((n_tok, D_MODEL), F32), jax.ShapeDtypeStruct((n_tok, D_MODEL), F32)]
    out_specs = [tile(Q_OFF), tile(B_WIDTH), tile(2 * KV_WIDTH), tile(B_WIDTH), tile(HEADS * CHUNK), tile(N_STATS * CHUNK),
                 tile(A_WIDTH), tile(4 * A_WIDTH), tile(B_WIDTH), tile(D_MODEL), tile(D_MODEL)]
    scratch = [pltpu.VMEM((tb, D_MODEL), BF16), pltpu.VMEM((CHUNK, 2 * KV_WIDTH), BF16)]
    args = [x, g_pre, w_in_t, ln_g, ln_b, wcat, bs, sinks, bias, w_out, g_post]
    if gather is not None:
        wt_shards, wo_shards, _ = gather
        in_specs += [ANY, ANY]
        args += [wt_shards, wo_shards]
        out_shape += [jax.ShapeDtypeStruct((N_CHIPS * wt_shards.shape[1], D_MODEL), BF16),
                      jax.ShapeDtypeStruct((N_CHIPS * wo_shards.shape[1], D_MODEL), BF16)]
        out_specs += [ANY, ANY]
        scratch += _comm_scratch(N_GATHER_SEMS)
    if target is not None:
        in_specs.append(tile(D_MODEL))
        args.append(target)
        out_shape.append(jax.ShapeDtypeStruct((1, 1), F32))
        out_specs.append(SMEM)
    name = "fwd_layer" + ("" if gather is None else "_gather") + ("" if target is None else "_loss")
    return pl.pallas_call(
        body, name=name, grid=(n_tiles,),
        out_shape=tuple(out_shape), in_specs=in_specs, out_specs=tuple(out_specs), scratch_shapes=scratch,
        compiler_params=_cparams(1),
    )(*args)


def _bwd_mix(dout, y, a, q, kv, bz, probs, stats, mixed, act, att, ln_g, ln_b, wtcat, w_out, g_post, exchange=None,
             allgather=None):
    n_tok = y.shape[0]
    tb = TOK_TILE
    n_chunks = tb // CHUNK
    n_tiles = n_tok // tb

    def body(*refs):
        refs = list(refs)
        take = lambda n: [refs.pop(0) for _ in range(n)]
        (dout_ref, y_ref, az_ref, q_ref, kv_ref, halo_ref, bz_ref, probs_ref, stats_ref, mixed_ref, act_hbm, att_ref,
         lng_ref, lnb_ref, wtcat_ref, wout_ref, gpost_ref) = take(17)
        if exchange is not None:
            dwt_in, dwo_in = take(2)
        if allgather is not None:
            (small_in,) = take(1)
        dproj_ref, dwout_ref, dws_ref, dmix_ref, dlng_ref, dlnb_ref, dgpost_ref, dbias_ref = take(8)
        if exchange is not None:
            pt_ref, po_ref = take(2)
        if allgather is not None:
            (small_all,) = take(1)
        ycat_s, dy_s, dyc_s, dkv_s, carry_s, dwout_s, stage_s, dws_s, act_ring, act_sems = take(10)
        starts, ends = [], []
        if exchange is not None:
            hosted = _hosted(_exchange_phases((dwt_in, dwo_in), (pt_ref, po_ref), *take(3)), pl.program_id(0), n_tiles)
            starts.append(hosted[0])
            ends.append(hosted[1])
        if allgather is not None:
            hosted = _hosted(_exchange_phases((small_in,), (small_all,), *take(3), by_chip=False), pl.program_id(0),
                             n_tiles)
            starts.append(hosted[0])
            ends.append(hosted[1])
        at_start = lambda: [f() for f in starts]
        at_end = lambda: [f() for f in ends]
        at_start()
        step = pl.program_id(0)
        lo = _lo_mask((CHUNK, CHUNK))
        prev = _ahead() > 0

        def act_copy(at_step):
            slot = lax.rem(at_step, N_RING)
            first = pl.multiple_of((n_tiles - 1 - at_step) * tb, tb)
            return pltpu.make_async_copy(act_hbm.at[pl.ds(first, tb), :], act_ring.at[slot], act_sems.at[slot])

        @pl.when(step == 0)
        def _():
            for ahead in range(N_RING - 1):
                act_copy(step + ahead).start()

        @pl.when(step + N_RING - 1 < n_tiles)
        def _():
            act_copy(step + N_RING - 1).start()

        act_copy(step).wait()
        act_ref = act_ring.at[lax.rem(step, N_RING)]

        @pl.when(step == 0)
        def _():
            dwout_s[...] = jnp.zeros_like(dwout_s)
            dws_s[...] = jnp.zeros_like(dws_s)
            dmix_ref[...] = jnp.zeros_like(dmix_ref)
            dlng_ref[...] = jnp.zeros_like(dlng_ref)
            dlnb_ref[...] = jnp.zeros_like(dlnb_ref)
            dgpost_ref[...] = jnp.zeros_like(dgpost_ref)
            dbias_ref[...] = jnp.zeros_like(dbias_ref)
            carry_s[...] = jnp.zeros_like(carry_s)

        for r0 in range(0, tb, CHUNK):
            rows = slice(r0, r0 + CHUNK)
            yn = y_ref[rows, :]
            dout = dout_ref[rows, :]
            r = _over_lanes(stats_ref[rows, STAT_POST_RMS], D_MODEL)
            dgpost_ref[...] += jnp.sum(dout * yn, axis=0, keepdims=True)
            dyn = dout * gpost_ref[...]
            dy_s[rows, :] = (r * (dyn - yn * jnp.mean(dyn * yn, axis=-1, keepdims=True))).astype(BF16)
        dyc_s[...] = _dot_nt(dy_s[...], wout_ref[...])
        dkv_s[0:tb, :] = jnp.zeros((tb, 2 * KV_WIDTH), F32)
        dkv_s[tb:tb + CHUNK, :] = carry_s[...]

        def chunk(c, carry):
            r0 = pl.multiple_of(c * CHUNK, CHUNK)
            rows = pl.ds(r0, CHUNK)
            u, gu, xhat, gv = (act_ref[rows, k * A_WIDTH:(k + 1) * A_WIDTH] for k in range(4))
            rs = _over_lanes(stats_ref[rows, STAT_LN_RSTD], A_WIDTH)
            vnb = (xhat * lng_ref[...] + lnb_ref[...]).astype(BF16)
            d_vn, d_u, d_az = [], [], []
            for p in range(4):
                blk = slice(p * CHUNK, (p + 1) * CHUNK)
                mixed = mixed_ref[rows, blk]
                sz, gz = _silu_and_grad(az_ref[rows, blk])
                ub = u[:, blk]
                dya = dyc_s[rows, blk]
                um = ub * mixed
                ycat_s[rows, blk] = (um * sz).astype(BF16)
                d_mixed = (dya * ub) * sz
                d_u.append((dya * mixed) * sz)
                d_az.append((dya * um) * gz)
                dmix_ref[:, blk] += d_mixed
                dmbd = _blockdiag(d_mixed.astype(BF16), lo)
                d_vn.append(_dot(wtcat_ref[p], dmbd))
                dws_s[p] += _dot_nt(dmbd, vnb[:, blk])
            d_vn = jnp.concatenate(d_vn, axis=1)
            dlng_ref[...] += jnp.sum(d_vn * xhat, axis=0, keepdims=True)
            dlnb_ref[...] += jnp.sum(d_vn, axis=0, keepdims=True)
            dxh = d_vn * lng_ref[...]
            d_vv = rs * (dxh - jnp.mean(dxh, axis=-1, keepdims=True)
                         - xhat * jnp.mean(dxh * xhat, axis=-1, keepdims=True))
            dproj_ref[rows, 0:A_WIDTH] = (jnp.concatenate(d_u, axis=1) * gu).astype(BF16)
            dproj_ref[rows, A_WIDTH:2 * A_WIDTH] = (d_vv * gv).astype(BF16)
            dproj_ref[rows, 2 * A_WIDTH:Q_OFF] = jnp.concatenate(d_az, axis=1).astype(BF16)
            ks, vs = _kv_rows(kv_ref, halo_ref, r0, c)
            q_stacks = _query_stacks([q_ref[rows, p * CHUNK:(p + 1) * CHUNK] for p in range(4)], lo)
            probs = {(p, half): probs_ref[rows, (2 * p + half) * CHUNK:(2 * p + half + 1) * CHUNK]
                     for p in range(4) for half in range(2)}
            p_stacks = [jnp.concatenate([_window_keys(probs[m], prev) for m in members], axis=0)
                        for members in _HEADS_BY_COPY]
            dom = {}
            for p in range(4):
                blk = slice(p * CHUNK, (p + 1) * CHUNK)
                sb, gb = _silu_and_grad(bz_ref[rows, blk])
                dyb = dyc_s[rows, B_WIDTH + p * CHUNK:B_WIDTH + (p + 1) * CHUNK]
                o = att_ref[rows, blk]
                ycat_s[rows, B_WIDTH + p * CHUNK:B_WIDTH + (p + 1) * CHUNK] = (o * sb).astype(BF16)
                dproj_ref[rows, BZ_OFF + p * CHUNK:BZ_OFF + (p + 1) * CHUNK] = ((dyb * o) * gb).astype(BF16)
                dom[p] = _masked_halves((dyb * sb).astype(BF16), lo)
            dqs, dk_by_copy, dv_by_copy = {}, [], []
            for sw, members in enumerate(_HEADS_BY_COPY):
                do_stack = jnp.concatenate([dom[p][half] for p, half in members], axis=0)
                dp_stack = _dot_nt(do_stack, vs[sw])
                dsws = []
                for i, (p, half) in enumerate(members):
                    head = 2 * p + half
                    dp = _window_square(dp_stack[i * CHUNK:(i + 1) * CHUNK, :], prev)
                    prob = probs[p, half].astype(F32)
                    delta = jnp.sum(prob * dp, axis=-1, keepdims=True)
                    ds = prob * (dp - delta)
                    dbias_ref[head] += ds
                    dsws.append(_window_keys(ds.astype(BF16), prev))
                ds_stack = jnp.concatenate(dsws, axis=0)
                dq_stack = _dot(ds_stack, ks[sw])
                for i, m in enumerate(members):
                    dqs[m] = dq_stack[i * CHUNK:(i + 1) * CHUNK, :]
                dk_by_copy.append(_dot_tn(ds_stack, q_stacks[sw]))
                dv_by_copy.append(_dot_tn(p_stacks[sw], do_stack))
            for p in range(4):
                dq = jnp.where(lo, dqs[p, 0], dqs[p, 1]) * QK_SCALE
                dproj_ref[rows, Q_OFF + p * CHUNK:Q_OFF + (p + 1) * CHUNK] = dq.astype(BF16)
            both = pl.ds(r0, 2 * CHUNK)
            dkv_s[both, 0:KV_WIDTH] += dk_by_copy[0] + _swap_halves(dk_by_copy[1])
            dkv_s[both, KV_WIDTH:2 * KV_WIDTH] += dv_by_copy[0] + _swap_halves(dv_by_copy[1])
            return carry

        lax.fori_loop(0, n_chunks, chunk, 0, unroll=BWD_CHUNK_UNROLL)
        dwout_s[...] += _dot_tn(ycat_s[...], dy_s[...])
        dproj_ref[:, K_OFF:BZ_OFF] = dkv_s[CHUNK:CHUNK + tb, :].astype(BF16)
        carry_s[...] = dkv_s[0:CHUNK, :]

        @pl.when(step == n_tiles - 1)
        def _():
            row = lax.broadcasted_iota(jnp.int32, (2 * CHUNK, CHUNK), 0)
            col = lax.broadcasted_iota(jnp.int32, (2 * CHUNK, CHUNK), 1)
            causal = col <= jnp.where(row >= CHUNK, row - CHUNK, row)
            for p in range(4):
                dws_ref[p] = jnp.where(causal, dws_s[p], 0.0).astype(BF16)
            rows = stage_s.shape[0]
            for r0 in range(0, D_MODEL, rows):
                stage_s[...] = dwout_s[r0:r0 + rows, :].astype(BF16)
                pltpu.sync_copy(stage_s, dwout_ref.at[r0:r0 + rows, :])

        at_end()

    tile = lambda w: pl.BlockSpec((tb, w), lambda s: (n_tiles - 1 - s, 0))
    whole = lambda shape, **kw: pl.BlockSpec(shape, lambda s: (0,) * len(shape), **kw)
    once = dict(pipeline_mode=pl.Buffered(1))
    gate_cols = pl.BlockSpec((tb, A_WIDTH), lambda s: (n_tiles - 1 - s, 2))
    in_specs = [tile(D_MODEL), tile(D_MODEL), gate_cols, tile(B_WIDTH), tile(2 * KV_WIDTH),
                pl.BlockSpec((CHUNK, 2 * KV_WIDTH), lambda s: (jnp.maximum((n_tiles - 1 - s) * n_chunks - 1, 0), 0)),
                tile(B_WIDTH), tile(HEADS * CHUNK), tile(N_STATS * CHUNK), tile(A_WIDTH), ANY, tile(B_WIDTH),
                whole((1, A_WIDTH)), whole((1, A_WIDTH)), whole((4, CHUNK, 2 * CHUNK)), whole((D_MODEL, D_MODEL), **once),
                whole((1, D_MODEL))]
    args = [dout, y, a, q, kv, kv, bz, probs, stats, mixed, act, att, ln_g, ln_b, wtcat, w_out, g_post]
    small_shapes = [jax.ShapeDtypeStruct((D_MODEL, D_MODEL), BF16), jax.ShapeDtypeStruct((4, 2 * CHUNK, CHUNK), BF16),
                    jax.ShapeDtypeStruct((CHUNK, A_WIDTH), F32), jax.ShapeDtypeStruct((1, A_WIDTH), F32),
                    jax.ShapeDtypeStruct((1, A_WIDTH), F32), jax.ShapeDtypeStruct((1, D_MODEL), F32),
                    jax.ShapeDtypeStruct((HEADS, CHUNK, CHUNK), F32)]
    small_specs = [ANY, whole((4, 2 * CHUNK, CHUNK)), whole((CHUNK, A_WIDTH)), whole((1, A_WIDTH)), whole((1, A_WIDTH)),
                   whole((1, D_MODEL)), whole((HEADS, CHUNK, CHUNK))]
    scratch = [pltpu.VMEM((tb, D_MODEL), BF16), pltpu.VMEM((tb, D_MODEL), BF16), pltpu.VMEM((tb, D_MODEL), F32),
               pltpu.VMEM((tb + CHUNK, 2 * KV_WIDTH), F32), pltpu.VMEM((CHUNK, 2 * KV_WIDTH), F32),
               pltpu.VMEM((D_MODEL, D_MODEL), F32), pltpu.VMEM((2 * CHUNK, D_MODEL), BF16),
               pltpu.VMEM((4, 2 * CHUNK, CHUNK), F32), pltpu.VMEM((N_RING, tb, 4 * A_WIDTH), F32),
               pltpu.SemaphoreType.DMA((N_RING,))]
    out_shape = [jax.ShapeDtypeStruct((n_tok, IN_WIDTH), BF16)] + small_shapes
    out_specs = [tile(IN_WIDTH)] + small_specs
    if exchange is not None:
        dwt, dwo = exchange
        in_specs += [ANY, ANY]
        args += [dwt, dwo]
        out_shape += [jax.ShapeDtypeStruct((N_DEV, dwt.shape[0] // N_CHIPS, D_MODEL), BF16),
                      jax.ShapeDtypeStruct((N_DEV, dwo.shape[0] // N_CHIPS, D_MODEL), BF16)]
        out_specs += [ANY, ANY]
    if allgather is not None:
        in_specs.append(ANY)
        args.append(allgather)
        out_shape.append(jax.ShapeDtypeStruct((N_DEV,) + allgather.shape, allgather.dtype))
        out_specs.append(ANY)
    for hosted in (exchange, allgather):
        if hosted is not None:
            scratch += _comm_scratch(N_EXCHANGE_SEMS)
    return pl.pallas_call(
        body, name="bwd_mix" + ("" if exchange is None and allgather is None else "_exchange"), grid=(n_tiles,),
        out_shape=tuple(out_shape), in_specs=in_specs, out_specs=tuple(out_specs), scratch_shapes=scratch,
        compiler_params=_cparams(1),
    )(*args)


def _bwd_in(dproj, x, dout, g_pre, w_in_t):
    n_tok = x.shape[0]
    tm = TOK_TILE
    n_tiles = n_tok // tm

    def body(dp_ref, x_ref, dout_ref, g_ref, wt_ref, dx_ref, dwt_ref, dg_ref, acc_s):
        i = pl.program_id(0)

        @pl.when(i == 0)
        def _():
            acc_s[...] = jnp.zeros_like(acc_s)
            dg_ref[...] = jnp.zeros_like(dg_ref)

        xf = x_ref[...]
        r = lax.rsqrt(jnp.mean(xf * xf, axis=-1, keepdims=True) + NORM_EPS)
        xn = xf * r
        h = (xn * g_ref[...]).astype(BF16)
        dp = dp_ref[...]
        dh = _dot(dp, wt_ref[...])
        acc_s[...] += _dot_tn(dp, h)
        dg_ref[...] += jnp.sum(dh * xn, axis=0, keepdims=True)
        dhn = dh * g_ref[...]
        dx_ref[...] = dout_ref[...] + r * (dhn - xn * jnp.mean(dhn * xn, axis=-1, keepdims=True))

        @pl.when(i == n_tiles - 1)
        def _():
            dwt_ref[...] = acc_s[...].astype(BF16)

    return pl.pallas_call(
        body, name="bwd_in", grid=(n_tiles,),
        out_shape=(jax.ShapeDtypeStruct((n_tok, D_MODEL), F32), jax.ShapeDtypeStruct((IN_WIDTH, D_MODEL), BF16),
                   jax.ShapeDtypeStruct((1, D_MODEL), F32)),
        in_specs=[pl.BlockSpec((tm, IN_WIDTH), lambda i: (i, 0)), pl.BlockSpec((tm, D_MODEL), lambda i: (i, 0)),
                  pl.BlockSpec((tm, D_MODEL), lambda i: (i, 0)), pl.BlockSpec((1, D_MODEL), lambda i: (0, 0)),
                  pl.BlockSpec((IN_WIDTH, D_MODEL), lambda i: (0, 0), pipeline_mode=pl.Buffered(1))],
        out_specs=(pl.BlockSpec((tm, D_MODEL), lambda i: (i, 0)),
                   pl.BlockSpec((IN_WIDTH, D_MODEL), lambda i: (0, 0), pipeline_mode=pl.Buffered(1)),
                   pl.BlockSpec((1, D_MODEL), lambda i: (0, 0))),
        scratch_shapes=[pltpu.VMEM((IN_WIDTH, D_MODEL), F32)],
        compiler_params=_cparams(1),
    )(dproj, x, dout, g_pre, w_in_t)


def _riders(exchanges):
    args = [a for arrays, _ in exchanges for a in arrays]
    shapes = [jax.ShapeDtypeStruct((N_DEV, a.shape[0] // N_CHIPS if by_chip else a.shape[0]) + a.shape[1:], a.dtype)
              for arrays, by_chip in exchanges for a in arrays]
    scratch = [s for _ in exchanges for s in _comm_scratch(N_EXCHANGE_SEMS)]

    def bind(in_refs, out_refs, sem_refs, step, n_steps, pass_on_step=None):
        in_refs, out_refs, sem_refs = list(in_refs), list(out_refs), list(sem_refs)
        starts, ends = [], []
        for arrays, by_chip in exchanges:
            n = len(arrays)
            phases = _exchange_phases(tuple(in_refs[:n]), tuple(out_refs[:n]), *sem_refs[:3], by_chip=by_chip)
            del in_refs[:n], out_refs[:n], sem_refs[:3]
            at_start, at_end = _hosted(phases, step, n_steps, pass_on_step)
            starts.append(at_start)
            ends.append(at_end)
        return (lambda: [f() for f in starts]), (lambda: [f() for f in ends])

    return args, [ANY] * len(args), shapes, [ANY] * len(shapes), scratch, bind


def _bwd_in_dw(dproj, x, g_pre, exchanges):
    n_tok = x.shape[0]
    tm = TOK_TILE
    n_tiles = n_tok // tm
    r_args, r_in_specs, r_shapes, r_out_specs, r_scratch, bind = _riders(exchanges)

    def body(*refs):
        dp_ref, x_ref, g_ref = refs[:3]
        in_refs = refs[3:3 + len(r_args)]
        dwt_ref = refs[3 + len(r_args)]
        out_refs = refs[4 + len(r_args):4 + 2 * len(r_args)]
        acc_s = refs[4 + 2 * len(r_args)]
        i = pl.program_id(0)
        at_start, at_end = bind(in_refs, out_refs, refs[5 + 2 * len(r_args):], i, n_tiles)
        at_start()

        @pl.when(i == 0)
        def _():
            acc_s[...] = jnp.zeros_like(acc_s)

        xf = x_ref[...]
        r = lax.rsqrt(jnp.mean(xf * xf, axis=-1, keepdims=True) + NORM_EPS)
        acc_s[...] += _dot_tn(dp_ref[...], ((xf * r) * g_ref[...]).astype(BF16))

        @pl.when(i == n_tiles - 1)
        def _():
            dwt_ref[...] = acc_s[...].astype(BF16)

        at_end()

    return pl.pallas_call(
        body, name="bwd_in_dw", grid=(n_tiles,),
        out_shape=(jax.ShapeDtypeStruct((IN_WIDTH, D_MODEL), BF16), *r_shapes),
        in_specs=[pl.BlockSpec((tm, IN_WIDTH), lambda i: (i, 0)), pl.BlockSpec((tm, D_MODEL), lambda i: (i, 0)),
                  pl.BlockSpec((1, D_MODEL), lambda i: (0, 0))] + r_in_specs,
        out_specs=(pl.BlockSpec((IN_WIDTH, D_MODEL), lambda i: (0, 0), pipeline_mode=pl.Buffered(1)), *r_out_specs),
        scratch_shapes=[pltpu.VMEM((IN_WIDTH, D_MODEL), F32)] + r_scratch,
        compiler_params=_cparams(1),
    )(dproj, x, g_pre, *r_args)


def _bwd_in_dx(dproj, x, dout, g_pre, w_in_t, exchanges):
    n_tok = x.shape[0]
    tm = TOK_TILE
    n_tiles = n_tok // tm
    r_args, r_in_specs, r_shapes, r_out_specs, r_scratch, bind = _riders(exchanges)

    def body(*refs):
        dp_ref, x_ref, dout_ref, g_ref, wt_ref = refs[:5]
        in_refs = refs[5:5 + len(r_args)]
        dx_ref, dg_ref = refs[5 + len(r_args):7 + len(r_args)]
        out_refs = refs[7 + len(r_args):7 + 2 * len(r_args)]
        i = pl.program_id(0)
        at_start, at_end = bind(in_refs, out_refs, refs[7 + 2 * len(r_args):], i, n_tiles, pass_on_step=n_tiles - 2)
        at_start()

        @pl.when(i == 0)
        def _():
            dg_ref[...] = jnp.zeros_like(dg_ref)

        xf = x_ref[...]
        r = lax.rsqrt(jnp.mean(xf * xf, axis=-1, keepdims=True) + NORM_EPS)
        xn = xf * r
        dh = _dot(dp_ref[...], wt_ref[...])
        dg_ref[...] += jnp.sum(dh * xn, axis=0, keepdims=True)
        dhn = dh * g_ref[...]
        dx_ref[...] = dout_ref[...] + r * (dhn - xn * jnp.mean(dhn * xn, axis=-1, keepdims=True))
        at_end()

    tile = lambda w: pl.BlockSpec((tm, w), lambda i: (i, 0))
    return pl.pallas_call(
        body, name="bwd_in_dx", grid=(n_tiles,),
        out_shape=(jax.ShapeDtypeStruct((n_tok, D_MODEL), F32), jax.ShapeDtypeStruct((1, D_MODEL), F32), *r_shapes),
        in_specs=[tile(IN_WIDTH), tile(D_MODEL), tile(D_MODEL), pl.BlockSpec((1, D_MODEL), lambda i: (0, 0)),
                  pl.BlockSpec((IN_WIDTH, D_MODEL), lambda i: (0, 0), pipeline_mode=pl.Buffered(1))] + r_in_specs,
        out_specs=(tile(D_MODEL), pl.BlockSpec((1, D_MODEL), lambda i: (0, 0)), *r_out_specs),
        scratch_shapes=r_scratch,
        compiler_params=_cparams(1),
    )(dproj, x, dout, g_pre, w_in_t, *r_args)


def _adam_update(w, g, m, v):
    nm = ADAM_B1 * m + (1.0 - ADAM_B1) * g
    nv = ADAM_B2 * v + (1.0 - ADAM_B2) * (g * g)
    m_hat = nm / (1.0 - ADAM_B1 ** ADAM_STEP)
    v_hat = nv / (1.0 - ADAM_B2 ** ADAM_STEP)
    return -ADAM_LR * (m_hat / (jnp.sqrt(v_hat) + ADAM_EPS) + ADAM_WD * w), nm, nv


def _sum_slots(p_ref):
    tot = p_ref[0].astype(F32)
    for d in range(1, N_DEV):
        tot = tot + p_ref[d].astype(F32)
    return tot


def _adamw_parts(parts, w, m, v, name):
    n_layers, n_rows, n_cols = w.shape
    tr = n_rows
    while tr * n_cols > ADAMW_BLOCK_ELEMS:
        tr //= 2
    n_blocks = n_rows // tr

    def body(*refs):
        p_refs = refs[:n_layers]
        w_ref, m_ref, v_ref, g_ref, d_ref, nm_ref, nv_ref = refs[n_layers:]
        for k in range(n_layers):
            @pl.when(pl.program_id(0) == k)
            def _(k=k):
                g = _sum_slots(p_refs[k])
                g_ref[0] = g
                d_ref[0], nm_ref[0], nv_ref[0] = _adam_update(w_ref[0], g, m_ref[0], v_ref[0])

    def part_spec(k):
        return pl.BlockSpec((N_DEV, tr, n_cols), lambda l, i: (0, jnp.clip(i + (l - k) * n_blocks, 0, n_blocks - 1), 0))

    spec = pl.BlockSpec((1, tr, n_cols), lambda l, i: (l, i, 0))
    shape = jax.ShapeDtypeStruct((n_layers, n_rows, n_cols), F32)
    return pl.pallas_call(
        body, name="adamw_" + name, grid=(n_layers, n_blocks), out_shape=(shape,) * 4,
        in_specs=[part_spec(k) for k in range(n_layers)] + [spec, spec, spec], out_specs=(spec,) * 4,
        compiler_params=_cparams(2),
    )(*parts, w, m, v)


def _adamw_small(quads):
    n = len(quads)

    def body(*refs):
        ins, outs = refs[:4 * n], refs[4 * n:]
        for j in range(n):
            w_ref, g_ref, m_ref, v_ref = ins[4 * j:4 * j + 4]
            d_ref, nm_ref, nv_ref = outs[3 * j:3 * j + 3]
            d_ref[...], nm_ref[...], nv_ref[...] = _adam_update(w_ref[...], g_ref[...], m_ref[...], v_ref[...])

    vmem = pl.BlockSpec(memory_space=pltpu.VMEM)
    shapes = tuple(jax.ShapeDtypeStruct(w.shape, F32) for w, *_ in quads for _ in range(3))
    results = pl.pallas_call(
        body, name="adamw_small", out_shape=shapes, in_specs=[vmem] * (4 * n), out_specs=(vmem,) * (3 * n),
    )(*[a for quad in quads for a in quad])
    return [results[3 * j:3 * j + 3] for j in range(n)]


_SMALL = ("pre_norm_g", "ln_v_g", "ln_v_b", "b_spatial", "sinks", "rel_bias", "post_norm_g")


def _pack(pieces):
    blocks, first_rows, n = [], [], 0
    for p in pieces:
        flat = p.reshape(-1)
        flat = jnp.concatenate([flat, jnp.zeros(((-flat.shape[0]) % (8 * 128),), F32)]).reshape(-1, 128)
        blocks.append(flat)
        first_rows.append(n)
        n += flat.shape[0]
    return jnp.concatenate(blocks, axis=0), first_rows


def _unpack(block, first_row, shape):
    size = math.prod(shape)
    return block[first_row:first_row + -(-size // 128)].reshape(-1)[:size].reshape(shape)


def kernel(x, pre_norm_g, w_in, ln_v_g, ln_v_b, w_spatial, b_spatial, sinks, rel_bias, w_out, post_norm_g, loss_target, m_pre_norm_g, m_w_in, m_ln_v_g, m_ln_v_b, m_w_spatial, m_b_spatial, m_sinks, m_rel_bias, m_w_out, m_post_norm_g, v_pre_norm_g, v_w_in, v_ln_v_g, v_ln_v_b, v_w_spatial, v_b_spatial, v_sinks, v_rel_bias, v_w_out, v_post_norm_g):
    weights = dict(pre_norm_g=pre_norm_g, w_in=w_in, ln_v_g=ln_v_g, ln_v_b=ln_v_b, w_spatial=w_spatial, b_spatial=b_spatial,
                   sinks=sinks, rel_bias=rel_bias, w_out=w_out, post_norm_g=post_norm_g)
    mom_m = dict(pre_norm_g=m_pre_norm_g, w_in=m_w_in, ln_v_g=m_ln_v_g, ln_v_b=m_ln_v_b, w_spatial=m_w_spatial,
                 b_spatial=m_b_spatial, sinks=m_sinks, rel_bias=m_rel_bias, w_out=m_w_out, post_norm_g=m_post_norm_g)
    mom_v = dict(pre_norm_g=v_pre_norm_g, w_in=v_w_in, ln_v_g=v_ln_v_g, ln_v_b=v_ln_v_b, w_spatial=v_w_spatial,
                 b_spatial=v_b_spatial, sinks=v_sinks, rel_bias=v_rel_bias, w_out=v_w_out, post_norm_g=v_post_norm_g)
    n_seq, seq_len, _ = x.shape
    n_layers = w_in.shape[0]
    x2 = x.reshape(n_seq * seq_len, D_MODEL)
    target2 = loss_target.reshape(n_seq * seq_len, D_MODEL)
    row = lambda p, l: p[l][None]

    wt_shards = jnp.swapaxes(w_in, 1, 2).astype(BF16)
    wo_shards = w_out.astype(BF16)
    bucket = _window_buckets()
    bias = _bias_table(rel_bias, bucket)
    wcat, wtcat, bs = _spatial_tables(w_spatial, jnp.swapaxes(b_spatial, 1, 2))

    wt, wo = [None] * n_layers, [None] * n_layers
    wt[0], wo[0] = _gather_weights(wt_shards, wo_shards, 0)
    xs, saved = [x2], []
    for l in range(n_layers):
        layer_args = (xs[-1], row(pre_norm_g, l), wt[l], row(ln_v_g, l), row(ln_v_b, l), wcat[l], bs[l], sinks[l], bias,
                      wo[l], row(post_norm_g, l), seq_len)
        if l + 1 < n_layers:
            *acts, xn, y, wt[l + 1], wo[l + 1] = _fwd_layer(*layer_args, gather=(wt_shards, wo_shards, l + 1))
            xs.append(xn)
        else:
            *acts, dx, y, loss = _fwd_layer(*layer_args, target=target2)
        saved.append((y, *acts))

    small = [None] * n_layers
    parts = [None] * n_layers
    waiting = None
    for l in reversed(range(n_layers)):
        hosted = {} if waiting is None else dict(exchange=waiting[:2], allgather=waiting[2])
        dproj, *rest = _bwd_mix(dx, *saved[l], row(ln_v_g, l), row(ln_v_b, l), wtcat[l], wo[l], row(post_norm_g, l), **hosted)
        dwo, dws, dmix, dlng, dlnb, dgpost, dbias = rest[:7]
        dws = dws.reshape(HEADS * CHUNK, CHUNK)
        if waiting is not None:
            parts[l + 1] = list(rest[7:])
        if l > 0:
            dx, dwt, dgpre = _bwd_in(dproj, xs[l], dx, row(pre_norm_g, l), wt[l])
            waiting = (dwt, dwo, dws)
        else:
            dwt, po, ws_all = _bwd_in_dw(dproj, xs[l], row(pre_norm_g, l), [((dwo,), True), ((dws,), False)])
            dx, dgpre, pt = _bwd_in_dx(dproj, xs[l], dx, row(pre_norm_g, l), wt[l], [((dwt,), True)])
            parts[l] = [pt, po, ws_all]
        small[l] = dict(pre_norm_g=dgpre[0], ln_v_g=dlng[0], ln_v_b=dlnb[0], dmix=dmix, dbias=dbias,
                        post_norm_g=dgpost[0])

    db = _b_spatial_grad([s["dmix"] for s in small])
    pieces = {n: jnp.stack([s[n] for s in small]) for n in ("pre_norm_g", "ln_v_g", "ln_v_b", "post_norm_g")}
    pieces["b_spatial"] = jnp.swapaxes(db[:, :, :HEADS], 1, 2)
    pieces["rel_bias"], pieces["sinks"] = _rel_bias_grad([s["dbias"] for s in small], bucket)
    packed, first_rows = _pack([pieces[n] for n in _SMALL] + [loss])
    total = _allreduce_small(packed)
    grad = {n: _unpack(total, r, weights[n].shape) for n, r in zip(_SMALL, first_rows)}
    loss_out = total[first_rows[-1], 0]

    delta, new_m, new_v = {}, {}, {}
    two_d = lambda p: p.reshape(-1, p.shape[-1])
    updates = _adamw_small([tuple(two_d(p[n]) for p in (weights, grad, mom_m, mom_v)) for n in _SMALL])
    for n, update in zip(_SMALL, updates):
        delta[n], new_m[n], new_v[n] = (u.reshape(weights[n].shape) for u in update)

    t3 = lambda p: jnp.swapaxes(p, 1, 2)
    flat3 = lambda p: p.reshape(n_layers, HEADS * CHUNK, CHUNK)
    for n, which, view, back in (("w_in", 0, t3, t3), ("w_out", 1, lambda p: p, lambda p: p),
                                 ("w_spatial", 2, flat3, lambda p: p.reshape(w_spatial.shape))):
        results = _adamw_parts([parts[l][which] for l in range(n_layers)], view(weights[n]), view(mom_m[n]),
                               view(mom_v[n]), n)
        grad[n], delta[n], new_m[n], new_v[n] = (back(r) for r in results)

    names = tuple(weights)
    return (loss_out, dx.reshape(x.shape), *[grad[n] for n in names], *[delta[n] for n in names],
            *[new_m[n] for n in names], *[new_v[n] for n in names])
```

```python
import math

import jax
import jax.numpy as jnp
from jax import lax
from jax.experimental import pallas as pl
from jax.experimental.pallas import tpu as pltpu

F32 = jnp.float32
BF16 = jnp.bfloat16

D_MODEL = 1024
A_WIDTH = 512
B_WIDTH = 512
KV_WIDTH = 128
IN_WIDTH = 3 * A_WIDTH + 2 * B_WIDTH + 2 * KV_WIDTH
CHUNK = 128
HEADS = 8
HEAD_DIM = 64
REL_BUCKETS = 32
NORM_EPS = 1e-6
NEG = -1e30
Q_OFF = 3 * A_WIDTH
K_OFF = Q_OFF + B_WIDTH
BZ_OFF = K_OFF + 2 * KV_WIDTH
QK_SCALE = HEAD_DIM ** -0.5

ADAM_LR = 0.001
ADAM_B1 = 0.9
ADAM_B2 = 0.999
ADAM_EPS = 1e-08
ADAM_WD = 0.01
ADAM_STEP = 10

TOK_TILE = 512
CHUNK_UNROLL = 4
BWD_CHUNK_UNROLL = 4
ADAMW_BLOCK_ELEMS = 192 * 1024
VMEM_LIMIT_V7X = 60 * 1024 * 1024

N_DEV = 8
N_CHIPS = 4
MESH_ID = pl.DeviceIdType.MESH
ANY = pl.BlockSpec(memory_space=pl.ANY)
SMEM = pl.BlockSpec(memory_space=pltpu.SMEM)


def _cparams(n_axes):
    return pltpu.CompilerParams(dimension_semantics=("arbitrary",) * n_axes, vmem_limit_bytes=VMEM_LIMIT_V7X)


_GELU_C = math.sqrt(2.0 / math.pi)
_GELU_C3 = _GELU_C * 0.044715


def _gelu_and_grad(x):
    x2 = x * x
    t = jnp.tanh(x * (_GELU_C3 * x2 + _GELU_C))
    cdf = 0.5 * t + 0.5
    d = cdf + (x * (cdf * (1.0 - cdf))) * ((6.0 * _GELU_C3) * x2 + 2.0 * _GELU_C)
    return x * cdf, d


def _sigmoid(x):
    return 0.5 + 0.5 * jnp.tanh(0.5 * x)


def _silu(x):
    return x * _sigmoid(x)


def _silu_and_grad(x):
    s = _sigmoid(x)
    return x * s, s * (1.0 + x * (1.0 - s))


def _dot(a, b):
    return jnp.dot(a, b, preferred_element_type=F32)


def _dot_nt(a, b):
    return lax.dot_general(a, b, (((1,), (1,)), ((), ())), preferred_element_type=F32)


def _dot_tn(a, b):
    return lax.dot_general(a, b, (((0,), (0,)), ((), ())), preferred_element_type=F32)


def _lo_mask(shape):
    return lax.broadcasted_iota(jnp.int32, shape, 1) < HEAD_DIM


def _swap_halves(v):
    return pltpu.roll(v, HEAD_DIM, 1)


def _window_buckets():
    q_loc = jnp.arange(CHUNK)[:, None]
    j_loc = jnp.arange(CHUNK)[None, :]
    d = q_loc - j_loc + jnp.where(j_loc > q_loc, CHUNK, 0)
    max_exact = REL_BUCKETS // 2
    safe = jnp.maximum(d, 1).astype(F32)
    large = max_exact + (jnp.log(safe / max_exact) / math.log(CHUNK / max_exact)
                         * (REL_BUCKETS - max_exact)).astype(jnp.int32)
    large = jnp.minimum(large, REL_BUCKETS - 1)
    return jnp.where(d < max_exact, d, large).astype(jnp.int32)


def _bias_table(rel_bias, bucket):
    def body(rb_ref, bk_ref, out_ref):
        bk = bk_ref[...]
        for h in range(HEADS):
            acc = jnp.zeros(bk.shape, F32)
            for b in range(REL_BUCKETS):
                acc = jnp.where(bk == b, rb_ref[b, h], acc)
            out_ref[h] = acc

    vmem = pl.BlockSpec(memory_space=pltpu.VMEM)
    return pl.pallas_call(
        body, name="bias_table",
        out_shape=jax.ShapeDtypeStruct((HEADS, CHUNK, CHUNK), F32),
        in_specs=[SMEM, vmem], out_specs=vmem,
    )(rel_bias, bucket)


def _rel_bias_grad(dbias, bucket):
    n_layers = len(dbias)

    def body(*refs):
        db_refs, (bk_ref, out_ref, dsink_ref) = refs[:n_layers], refs[n_layers:]
        bk = bk_ref[...]
        for h in range(HEADS):
            tot = jnp.zeros((CHUNK, CHUNK), F32)
            for l in range(n_layers):
                ds = db_refs[l][h]
                dsink_ref[l, h] = -jnp.sum(ds)
                tot = tot + ds
            for b in range(REL_BUCKETS):
                out_ref[b, h] = jnp.sum(jnp.where(bk == b, tot, 0.0))

    vmem = pl.BlockSpec(memory_space=pltpu.VMEM)
    return pl.pallas_call(
        body, name="rel_bias_grad",
        out_shape=(jax.ShapeDtypeStruct((REL_BUCKETS, HEADS), F32), jax.ShapeDtypeStruct((n_layers, HEADS), F32)),
        in_specs=[vmem] * (n_layers + 1), out_specs=(SMEM, SMEM),
    )(*dbias, bucket)


def _spatial_tables(w_spatial, b_spatial_t):
    n_layers = w_spatial.shape[0]

    def body(w_ref, b_ref, wcat_ref, wtcat_ref, bs_ref):
        row = lax.broadcasted_iota(jnp.int32, (CHUNK, CHUNK), 0)
        col = lax.broadcasted_iota(jnp.int32, (CHUNK, CHUNK), 1)
        causal = col <= row
        lo = _lo_mask((CHUNK, CHUNK))
        for p in range(4):
            for half in range(2):
                w = jnp.where(causal, w_ref[0, 2 * p + half], 0.0)
                wcat_ref[0, p, :, half * CHUNK:(half + 1) * CHUNK] = w.astype(BF16)
                wtcat_ref[0, p, :, half * CHUNK:(half + 1) * CHUNK] = w.T.astype(BF16)
            b = b_ref[0]
            bs_ref[0, p] = jnp.where(lo, b[:, 2 * p:2 * p + 1], b[:, 2 * p + 1:2 * p + 2])

    return pl.pallas_call(
        body, name="spatial_tables", grid=(n_layers,),
        out_shape=(jax.ShapeDtypeStruct((n_layers, 4, CHUNK, 2 * CHUNK), BF16),
                   jax.ShapeDtypeStruct((n_layers, 4, CHUNK, 2 * CHUNK), BF16),
                   jax.ShapeDtypeStruct((n_layers, 4, CHUNK, CHUNK), F32)),
        in_specs=[pl.BlockSpec((1, HEADS, CHUNK, CHUNK), lambda l: (l, 0, 0, 0)),
                  pl.BlockSpec((1, CHUNK, HEADS), lambda l: (l, 0, 0))],
        out_specs=(pl.BlockSpec((1, 4, CHUNK, 2 * CHUNK), lambda l: (l, 0, 0, 0)),
                   pl.BlockSpec((1, 4, CHUNK, 2 * CHUNK), lambda l: (l, 0, 0, 0)),
                   pl.BlockSpec((1, 4, CHUNK, CHUNK), lambda l: (l, 0, 0, 0))),
        compiler_params=_cparams(1),
    )(w_spatial, b_spatial_t)


def _b_spatial_grad(dmix):
    n_layers = len(dmix)

    def body(*refs):
        d_refs, out_ref = refs[:n_layers], refs[n_layers]
        lane = lax.broadcasted_iota(jnp.int32, (CHUNK, CHUNK), 1)
        for l in range(n_layers):
            acc = jnp.zeros((CHUNK, CHUNK), F32)
            for p in range(4):
                t = d_refs[l][:, p * CHUNK:(p + 1) * CHUNK]
                s_lo = jnp.sum(jnp.where(lane < HEAD_DIM, t, 0.0), axis=1, keepdims=True)
                s_hi = jnp.sum(jnp.where(lane < HEAD_DIM, 0.0, t), axis=1, keepdims=True)
                acc = jnp.where(lane == 2 * p, s_lo, acc)
                acc = jnp.where(lane == 2 * p + 1, s_hi, acc)
            out_ref[l] = acc

    vmem = pl.BlockSpec(memory_space=pltpu.VMEM)
    return pl.pallas_call(
        body, name="b_spatial_grad",
        out_shape=jax.ShapeDtypeStruct((n_layers, CHUNK, CHUNK), F32),
        in_specs=[vmem] * n_layers, out_specs=vmem,
    )(*dmix)


def _place():
    x, y, c = lax.axis_index("x"), lax.axis_index("y"), lax.axis_index("c")
    other_chips = [(1 - x, y), (x, 1 - y), (1 - x, 1 - y)]
    return x, y, c, other_chips


N_GATHER_SEMS = 12


def _gather_phases(shards, fulls, send_sems, recv_sems, local_sems):
    x, y, c, chips = _place()
    sibling = (x, y, 1 - c)
    n_arr = len(shards)

    def half_rows(a, chip, half):
        n = shards[a].shape[0]
        start = (2 * chip[0] + chip[1]) * n + half * (n // 2)
        return fulls[a].at[pl.ds(pl.multiple_of(start, 16), n // 2), :]

    def my_half(a):
        n = shards[a].shape[0]
        return shards[a].at[pl.ds(pl.multiple_of(c * (n // 2), 16), n // 2), :]

    def copy(k, a, src, chip, half, to):
        return pltpu.make_async_remote_copy(
            src_ref=src, dst_ref=half_rows(a, chip, half), send_sem=send_sems.at[n_arr * k + a],
            recv_sem=recv_sems.at[n_arr * k + a], device_id=to, device_id_type=MESH_ID)

    def local(a):
        n = shards[a].shape[0]
        mine = fulls[a].at[pl.ds(pl.multiple_of((2 * x + y) * n, 16), n), :]
        return pltpu.make_async_copy(shards[a], mine, local_sems.at[a])

    def first(k, a):
        return copy(k, a, my_half(a), (x, y), c, (chips[k][0], chips[k][1], c))

    def passed(k, a):
        return copy(3 + k, a, half_rows(a, chips[k], c), chips[k], c, sibling)

    def phase_a():
        for a in range(n_arr):
            local(a).start()
        for k in range(3):
            for a in range(n_arr):
                first(k, a).start()

    def phase_b():
        for k in range(3):
            for a in range(n_arr):
                copy(k, a, my_half(a), chips[k], c, sibling).wait_recv()
                passed(k, a).start()

    def phase_c():
        for k in range(3):
            for a in range(n_arr):
                copy(3 + k, a, my_half(a), chips[k], 1 - c, sibling).wait_recv()
        for k in range(3):
            for a in range(n_arr):
                first(k, a).wait_send()
                passed(k, a).wait_send()
        for a in range(n_arr):
            local(a).wait()

    return phase_a, phase_b, phase_c


N_EXCHANGE_SEMS = 14


def _exchange_phases(partials, parts, send_sems, recv_sems, local_sems, by_chip=True):
    x, y, c, chips = _place()
    me, sibling = (x, y, c), (x, y, 1 - c)
    n_arr = len(partials)

    def block(a, chip):
        if not by_chip:
            return partials[a]
        n = partials[a].shape[0] // N_CHIPS
        return partials[a].at[pl.ds(pl.multiple_of((2 * chip[0] + chip[1]) * n, 16), n), :]

    def slot(a, dev):
        return parts[a].at[4 * dev[0] + 2 * dev[1] + dev[2]]

    def copy(k, a, src, origin, to):
        return pltpu.make_async_remote_copy(
            src_ref=src, dst_ref=slot(a, origin), send_sem=send_sems.at[n_arr * k + a],
            recv_sem=recv_sems.at[n_arr * k + a], device_id=to, device_id_type=MESH_ID)

    def local(a):
        return pltpu.make_async_copy(block(a, (x, y)), slot(a, me), local_sems.at[a])

    def first(k, a):
        if k == 0:
            return copy(0, a, block(a, (x, y)), me, sibling)
        chip = chips[k - 1]
        return copy(k, a, block(a, chip), me, (chip[0], chip[1], c))

    def passed(k, a):
        origin = (chips[k][0], chips[k][1], c)
        return copy(4 + k, a, slot(a, origin), origin, sibling)

    def phase_a():
        for a in range(n_arr):
            local(a).start()
        for k in range(4):
            for a in range(n_arr):
                first(k, a).start()

    def phase_b():
        for k in range(3):
            for a in range(n_arr):
                copy(1 + k, a, block(a, (x, y)), (chips[k][0], chips[k][1], c), me).wait_recv()
                passed(k, a).start()

    def phase_c():
        for a in range(n_arr):
            copy(0, a, block(a, (x, y)), sibling, me).wait_recv()
        for k in range(3):
            for a in range(n_arr):
                copy(4 + k, a, block(a, (x, y)), (chips[k][0], chips[k][1], 1 - c), me).wait_recv()
        for k in range(4):
            for a in range(n_arr):
                first(k, a).wait_send()
        for k in range(3):
            for a in range(n_arr):
                passed(k, a).wait_send()
        for a in range(n_arr):
            local(a).wait()

    return phase_a, phase_b, phase_c


def _comm_scratch(n_sems):
    return [pltpu.SemaphoreType.DMA((n_sems,)), pltpu.SemaphoreType.DMA((n_sems,)), pltpu.SemaphoreType.DMA((2,))]


def _gather_weights(wt_shards, wo_shards, layer):
    wt_rows, wo_rows = wt_shards.shape[1], wo_shards.shape[1]

    def body(wt_ref, wo_ref, wt_full, wo_full, send_sems, recv_sems, local_sems):
        phases = _gather_phases((wt_ref.at[layer], wo_ref.at[layer]), (wt_full, wo_full), send_sems, recv_sems, local_sems)
        for phase in phases:
            phase()

    return pl.pallas_call(
        body, name="gather_weights",
        out_shape=(jax.ShapeDtypeStruct((N_CHIPS * wt_rows, D_MODEL), BF16),
                   jax.ShapeDtypeStruct((N_CHIPS * wo_rows, D_MODEL), BF16)),
        in_specs=[ANY, ANY], out_specs=(ANY, ANY), scratch_shapes=_comm_scratch(N_GATHER_SEMS),
    )(wt_shards, wo_shards)


def _allreduce_small(part):
    n_rows = part.shape[0]

    def body(p_ref, tot_ref, all_ref, send_sems, recv_sems, local_sem):
        x, y, c, chips = _place()
        me, sibling = (x, y, c), (x, y, 1 - c)

        def rows(dev):
            return all_ref.at[pl.ds(pl.multiple_of((4 * dev[0] + 2 * dev[1] + dev[2]) * n_rows, 8), n_rows), :]

        def copy(k, origin, to, src=None):
            return pltpu.make_async_remote_copy(
                src_ref=rows(origin) if src is None else src, dst_ref=rows(origin), send_sem=send_sems.at[k],
                recv_sem=recv_sems.at[k], device_id=to, device_id_type=MESH_ID)

        mine = pltpu.make_async_copy(p_ref, rows(me), local_sem)
        mine.start()
        first = [copy(0, me, sibling, src=p_ref)]
        first += [copy(1 + k, me, (chip[0], chip[1], c), src=p_ref) for k, chip in enumerate(chips)]
        for cp in first:
            cp.start()
        passed = []
        for k, chip in enumerate(chips):
            origin = (chip[0], chip[1], c)
            copy(1 + k, origin, me).wait_recv()
            fwd = copy(4 + k, origin, sibling)
            fwd.start()
            passed.append(fwd)
        copy(0, sibling, me).wait_recv()
        for k, chip in enumerate(chips):
            copy(4 + k, (chip[0], chip[1], 1 - c), me).wait_recv()
        for cp in first + passed:
            cp.wait_send()
        mine.wait()
        tot = all_ref[0:n_rows, :]
        for d in range(1, N_DEV):
            tot = tot + all_ref[d * n_rows:(d + 1) * n_rows, :]
        tot_ref[...] = tot

    vmem = pl.BlockSpec(memory_space=pltpu.VMEM)
    return pl.pallas_call(
        body, name="allreduce_small",
        out_shape=jax.ShapeDtypeStruct((n_rows, 128), F32),
        in_specs=[vmem], out_specs=vmem,
        scratch_shapes=[pltpu.VMEM((N_DEV * n_rows, 128), F32), pltpu.SemaphoreType.DMA((7,)),
                        pltpu.SemaphoreType.DMA((7,)), pltpu.SemaphoreType.DMA],
        compiler_params=pltpu.CompilerParams(vmem_limit_bytes=VMEM_LIMIT_V7X),
    )(part)


def _hosted(phases, step, n_steps, pass_on_step=None):
    phase_a, phase_b, phase_c = phases
    if pass_on_step is None:
        pass_on_step = (3 * n_steps) // 4

    def at_start():
        pl.when(step == 0)(phase_a)

    def at_end():
        pl.when(step == pass_on_step)(phase_b)
        pl.when(step == n_steps - 1)(phase_c)

    return at_start, at_end


def _layer_norm_stats(vv):
    mu = jnp.mean(vv, axis=-1, keepdims=True)
    xc = vv - mu
    rs = lax.rsqrt(jnp.mean(xc * xc, axis=-1, keepdims=True) + NORM_EPS)
    return xc * rs, rs


STAT_POST_RMS, STAT_LN_RSTD = (slice(k * CHUNK, (k + 1) * CHUNK) for k in range(2))
N_STATS = 2


def _over_lanes(stat, width):
    return jnp.concatenate([stat] * (width // CHUNK), axis=1)


def _blockdiag(v, lo):
    zero = jnp.zeros_like(v)
    return jnp.concatenate([jnp.where(lo, v, zero), jnp.where(lo, zero, v)], axis=0)


def _softmax_sink(s, sink):
    m = jnp.maximum(jnp.max(s, axis=-1, keepdims=True), sink)
    e = jnp.exp(s - m)
    esink = jnp.exp(sink - m)
    den = jnp.sum(e, axis=-1, keepdims=True) + esink
    return e / den


def _kv_rows(cur_ref, halo_ref, r0, c):
    prev_in_tile = cur_ref[pl.ds(pl.multiple_of(jnp.maximum(r0 - CHUNK, 0), CHUNK), CHUNK), :]
    prev = jnp.where(c == 0, halo_ref[...], prev_in_tile)
    kv2 = jnp.concatenate([prev, cur_ref[pl.ds(r0, CHUNK), :]], axis=0)
    k2, v2 = kv2[:, 0:KV_WIDTH], kv2[:, KV_WIDTH:2 * KV_WIDTH]
    return (k2, _swap_halves(k2)), (v2, _swap_halves(v2))


def _window_square(over_keys, prev):
    return jnp.where(prev, over_keys[:, 0:CHUNK], over_keys[:, CHUNK:2 * CHUNK])


def _window_keys(square, prev):
    zero = jnp.zeros_like(square)
    return jnp.concatenate([jnp.where(prev, square, zero), jnp.where(prev, zero, square)], axis=1)


def _ahead(shape=(CHUNK, CHUNK)):
    return lax.broadcasted_iota(jnp.int32, shape, 1) - lax.broadcasted_iota(jnp.int32, shape, 0)


def _dead_mask(first, ahead):
    return ahead > jnp.where(first, 0, CHUNK)


def _head_of(p, half):
    return 2 * p + half, int(half != p // 2)


_HEADS_BY_COPY = tuple(tuple((p, half) for p in range(4) for half in range(2) if _head_of(p, half)[1] == sw)
                       for sw in range(2))


def _masked_halves(tile, lo):
    zero = jnp.zeros_like(tile)
    return {0: jnp.where(lo, tile, zero), 1: jnp.where(lo, zero, tile)}


def _query_stacks(q_tiles, lo):
    qm = {p: _masked_halves(q_tiles[p] * QK_SCALE, lo) for p in range(4)}
    return [jnp.concatenate([qm[p][half] for p, half in members], axis=0) for members in _HEADS_BY_COPY]


def _attention_probs(q_stacks, ks, bias_ref, sink_ref, dead, prev):
    probs = {}
    for sw, members in enumerate(_HEADS_BY_COPY):
        s_stack = _dot_nt(q_stacks[sw], ks[sw])
        for i, (p, half) in enumerate(members):
            head = 2 * p + half
            s = _window_square(s_stack[i * CHUNK:(i + 1) * CHUNK, :], prev) + bias_ref[head]
            s = jnp.where(dead, NEG, s)
            probs[p, half] = _softmax_sink(s, sink_ref[head]).astype(BF16)
    return probs


def _attention_values(probs, vs, prev):
    outs, p_stacks = {}, []
    for sw, members in enumerate(_HEADS_BY_COPY):
        p_stack = jnp.concatenate([_window_keys(probs[m], prev) for m in members], axis=0)
        r_stack = _dot(p_stack, vs[sw])
        p_stacks.append(p_stack)
        for i, m in enumerate(members):
            outs[m] = r_stack[i * CHUNK:(i + 1) * CHUNK, :]
    return outs, p_stacks


def _fwd_layer(x, g_pre, w_in_t, ln_g, ln_b, wcat, bs, sinks, bias, w_out, g_post, seq_len, gather=None, target=None):
    n_tok = x.shape[0]
    tb = TOK_TILE
    n_chunks = tb // CHUNK
    n_tiles = n_tok // tb
    n_in = 11
    assert gather is None or target is None

    def body(*refs):
        (x_ref, gpre_ref, wt_ref, lng_ref, lnb_ref, wcat_ref, bs_ref, sink_ref, bias_ref, wout_ref,
         gpost_ref) = refs[:n_in]
        at_start = at_end = lambda: None
        if gather is not None:
            (wts_ref, wos_ref, a_ref, q_ref, kv_ref, bz_ref, probs_ref, stats_ref, mixed_ref, act_ref, att_ref, xn_ref, y_ref,
             wt_full, wo_full, ycat_s, halo_ref, send_sems, recv_sems, local_sems) = refs[n_in:]
            phases = _gather_phases((wts_ref.at[gather[2]], wos_ref.at[gather[2]]), (wt_full, wo_full), send_sems,
                                    recv_sems, local_sems)
            at_start, at_end = _hosted(phases, pl.program_id(0), n_tiles)
        elif target is not None:
            (target_ref, a_ref, q_ref, kv_ref, bz_ref, probs_ref, stats_ref, mixed_ref, act_ref, att_ref, xn_ref, y_ref,
             loss_ref, ycat_s, halo_ref) = refs[n_in:]
        else:
            (a_ref, q_ref, kv_ref, bz_ref, probs_ref, stats_ref, mixed_ref, act_ref, att_ref, xn_ref, y_ref, ycat_s,
             halo_ref) = refs[n_in:]
        at_start()
        i = pl.program_id(0)
        lo = _lo_mask((CHUNK, CHUNK))
        ahead = _ahead()
        prev = ahead > 0

        @pl.when(i == 0)
        def _():
            halo_ref[...] = jnp.zeros_like(halo_ref)

        xf = x_ref[...]
        r1 = lax.rsqrt(jnp.mean(xf * xf, axis=-1, keepdims=True) + NORM_EPS)
        h = ((xf * r1) * gpre_ref[...]).astype(BF16)
        a_ref[...] = _dot_nt(h, wt_ref[0:Q_OFF, :])
        q_ref[...] = _dot_nt(h, wt_ref[Q_OFF:K_OFF, :]).astype(BF16)
        kv_ref[...] = _dot_nt(h, wt_ref[K_OFF:BZ_OFF, :]).astype(BF16)
        bz_ref[...] = _dot_nt(h, wt_ref[BZ_OFF:IN_WIDTH, :])

        def chunk(c, carry):
            r0 = pl.multiple_of(c * CHUNK, CHUNK)
            rows = pl.ds(r0, CHUNK)
            u, gu = _gelu_and_grad(a_ref[rows, 0:A_WIDTH])
            vv, gv = _gelu_and_grad(a_ref[rows, A_WIDTH:2 * A_WIDTH])
            xhat, rs = _layer_norm_stats(vv)
            for k, act in enumerate((u, gu, xhat, gv)):
                act_ref[rows, k * A_WIDTH:(k + 1) * A_WIDTH] = act
            stats_ref[rows, STAT_LN_RSTD] = jnp.broadcast_to(rs, (CHUNK, CHUNK))
            vnb = (xhat * lng_ref[...] + lnb_ref[...]).astype(BF16)
            for p in range(4):
                blk = slice(p * CHUNK, (p + 1) * CHUNK)
                mixed = _dot(wcat_ref[p], _blockdiag(vnb[:, blk], lo)) + bs_ref[p]
                mixed_ref[rows, blk] = mixed
                sz = _silu(a_ref[rows, 2 * A_WIDTH + p * CHUNK:2 * A_WIDTH + (p + 1) * CHUNK])
                ycat_s[rows, blk] = ((u[:, blk] * mixed) * sz).astype(BF16)
            ks, vs = _kv_rows(kv_ref, halo_ref, r0, c)
            dead = _dead_mask(lax.rem(i * tb + r0, seq_len) == 0, ahead)
            q_tiles = [q_ref[rows, p * CHUNK:(p + 1) * CHUNK] for p in range(4)]
            probs = _attention_probs(_query_stacks(q_tiles, lo), ks, bias_ref, sink_ref, dead, prev)
            for (p, half), prob in probs.items():
                head = 2 * p + half
                probs_ref[rows, head * CHUNK:(head + 1) * CHUNK] = prob
            outs, _ = _attention_values(probs, vs, prev)
            for p in range(4):
                blk = slice(p * CHUNK, (p + 1) * CHUNK)
                o = jnp.where(lo, outs[p, 0], outs[p, 1])
                att_ref[rows, blk] = o
                ycat_s[rows, B_WIDTH + p * CHUNK:B_WIDTH + (p + 1) * CHUNK] = (o * _silu(bz_ref[rows, blk])).astype(BF16)
            return carry

        lax.fori_loop(0, n_chunks, chunk, 0, unroll=CHUNK_UNROLL)
        halo_ref[...] = kv_ref[tb - CHUNK:tb, :]
        y = _dot(ycat_s[...], wout_ref[...])
        r = lax.rsqrt(jnp.mean(y * y, axis=-1, keepdims=True) + NORM_EPS)
        yn = y * r
        y_ref[...] = yn
        stats_ref[:, STAT_POST_RMS] = jnp.broadcast_to(r, (tb, CHUNK))
        xn = x_ref[...] + yn * gpost_ref[...]
        if target is None:
            xn_ref[...] = xn
        else:
            d = xn - target_ref[...]
            xn_ref[...] = d * (1.0 / D_MODEL)

            @pl.when(i == 0)
            def _():
                loss_ref[0, 0] = 0.0

            loss_ref[0, 0] += 0.5 * jnp.sum(jnp.mean(d * d, axis=-1, keepdims=True))
        at_end()

    tile = lambda w: pl.BlockSpec((tb, w), lambda i: (i, 0))
    whole = lambda shape, **kw: pl.BlockSpec(shape, lambda i: (0,) * len(shape), **kw)
    in_specs = [tile(D_MODEL), whole((1, D_MODEL)), whole((IN_WIDTH, D_MODEL), pipeline_mode=pl.Buffered(1)),
                whole((1, A_WIDTH)), whole((1, A_WIDTH)), whole((4, CHUNK, 2 * CHUNK)), whole((4, CHUNK, CHUNK)), SMEM,
                whole((HEADS, CHUNK, CHUNK)), whole((D_MODEL, D_MODEL)), whole((1, D_MODEL))]
    out_shape = [jax.ShapeDtypeStruct((n_tok, Q_OFF), F32), jax.ShapeDtypeStruct((n_tok, B_WIDTH), BF16),
                 jax.ShapeDtypeStruct((n_tok, 2 * KV_WIDTH), BF16), jax.ShapeDtypeStruct((n_tok, B_WIDTH), F32),
                 jax.ShapeDtypeStruct((n_tok, HEADS * CHUNK), BF16), jax.ShapeDtypeStruct((n_tok, N_STATS * CHUNK), F32),
                 jax.ShapeDtypeStruct((n_tok, A_WIDTH), F32), jax.ShapeDtypeStruct((n_tok, 4 * A_WIDTH), F32),
                 jax.ShapeDtypeStruct((n_tok, B_WIDTH), F32),
                 jax.ShapeDtypeStruct((n_tok, D_MODEL), F32), jax.ShapeDtypeStruct((n_tok, D_MODEL), F32)]
    out_specs = [tile(Q_OFF), tile(B_WIDTH), tile(2 * KV_WIDTH), tile(B_WIDTH), tile(HEADS * CHUNK), tile(N_STATS * CHUNK),
                 tile(A_WIDTH), tile(4 * A_WIDTH), tile(B_WIDTH), tile(D_MODEL), tile(D_MODEL)]
    scratch = [pltpu.VMEM((tb, D_MODEL), BF16), pltpu.VMEM((CHUNK, 2 * KV_WIDTH), BF16)]
    args = [x, g_pre, w_in_t, ln_g, ln_b, wcat, bs, sinks, bias, w_out, g_post]
    if gather is not None:
        wt_shards, wo_shards, _ = gather
        in_specs += [ANY, ANY]
        args += [wt_shards, wo_shards]
        out_shape += [jax.ShapeDtypeStruct((N_CHIPS * wt_shards.shape[1], D_MODEL), BF16),
                      jax.ShapeDtypeStruct((N_CHIPS * wo_shards.shape[1], D_MODEL), BF16)]
        out_specs += [ANY, ANY]
        scratch += _comm_scratch(N_GATHER_SEMS)
    if target is not None:
        in_specs.append(tile(D_MODEL))
        args.append(target)
        out_shape.append(jax.ShapeDtypeStruct((1, 1), F32))
        out_specs.append(SMEM)
    name = "fwd_layer" + ("" if gather is None else "_gather") + ("" if target is None else "_loss")
    return pl.pallas_call(
        body, name=name, grid=(n_tiles,),
        out_shape=tuple(out_shape), in_specs=in_specs, out_specs=tuple(out_specs), scratch_shapes=scratch,
        compiler_params=_cparams(1),
    )(*args)


def _bwd_mix(dout, y, a, q, kv, bz, probs, stats, mixed, act, att, ln_g, ln_b, wtcat, w_out, g_post, exchange=None,
             allgather=None):
    n_tok = y.shape[0]
    tb = TOK_TILE
    n_chunks = tb // CHUNK
    n_tiles = n_tok // tb

    def body(*refs):
        refs = list(refs)
        take = lambda n: [refs.pop(0) for _ in range(n)]
        (dout_ref, y_ref, az_ref, q_ref, kv_ref, halo_ref, bz_ref, probs_ref, stats_ref, mixed_ref, act_ref, att_ref,
         lng_ref, lnb_ref, wtcat_ref, wout_ref, gpost_ref) = take(17)
        if exchange is not None:
            dwt_in, dwo_in = take(2)
        if allgather is not None:
            (small_in,) = take(1)
        dproj_ref, dwout_ref, dws_ref, dmix_ref, dlng_ref, dlnb_ref, dgpost_ref, dbias_ref = take(8)
        if exchange is not None:
            pt_ref, po_ref = take(2)
        if allgather is not None:
            (small_all,) = take(1)
        ycat_s, dy_s, dyc_s, dkv_s, carry_s, dwout_s, stage_s, dws_s = take(8)
        starts, ends = [], []
        if exchange is not None:
            hosted = _hosted(_exchange_phases((dwt_in, dwo_in), (pt_ref, po_ref), *take(3)), pl.program_id(0), n_tiles)
            starts.append(hosted[0])
            ends.append(hosted[1])
        if allgather is not None:
            hosted = _hosted(_exchange_phases((small_in,), (small_all,), *take(3), by_chip=False), pl.program_id(0),
                             n_tiles)
            starts.append(hosted[0])
            ends.append(hosted[1])
        at_start = lambda: [f() for f in starts]
        at_end = lambda: [f() for f in ends]
        at_start()
        step = pl.program_id(0)
        lo = _lo_mask((CHUNK, CHUNK))
        prev = _ahead() > 0

        @pl.when(step == 0)
        def _():
            dwout_s[...] = jnp.zeros_like(dwout_s)
            dws_s[...] = jnp.zeros_like(dws_s)
            dmix_ref[...] = jnp.zeros_like(dmix_ref)
            dlng_ref[...] = jnp.zeros_like(dlng_ref)
            dlnb_ref[...] = jnp.zeros_like(dlnb_ref)
            dgpost_ref[...] = jnp.zeros_like(dgpost_ref)
            dbias_ref[...] = jnp.zeros_like(dbias_ref)
            carry_s[...] = jnp.zeros_like(carry_s)

        for r0 in range(0, tb, CHUNK):
            rows = slice(r0, r0 + CHUNK)
            yn = y_ref[rows, :]
            dout = dout_ref[rows, :]
            r = _over_lanes(stats_ref[rows, STAT_POST_RMS], D_MODEL)
            dgpost_ref[...] += jnp.sum(dout * yn, axis=0, keepdims=True)
            dyn = dout * gpost_ref[...]
            dy_s[rows, :] = (r * (dyn - yn * jnp.mean(dyn * yn, axis=-1, keepdims=True))).astype(BF16)
        dyc_s[...] = _dot_nt(dy_s[...], wout_ref[...])
        dkv_s[0:tb, :] = jnp.zeros((tb, 2 * KV_WIDTH), F32)
        dkv_s[tb:tb + CHUNK, :] = carry_s[...]

        def chunk(c, carry):
            r0 = pl.multiple_of(c * CHUNK, CHUNK)
            rows = pl.ds(r0, CHUNK)
            u, gu, xhat, gv = (act_ref[rows, k * A_WIDTH:(k + 1) * A_WIDTH] for k in range(4))
            rs = _over_lanes(stats_ref[rows, STAT_LN_RSTD], A_WIDTH)
            vnb = (xhat * lng_ref[...] + lnb_ref[...]).astype(BF16)
            d_vn, d_u, d_az = [], [], []
            for p in range(4):
                blk = slice(p * CHUNK, (p + 1) * CHUNK)
                mixed = mixed_ref[rows, blk]
                sz, gz = _silu_and_grad(az_ref[rows, blk])
                ub = u[:, blk]
                dya = dyc_s[rows, blk]
                um = ub * mixed
                ycat_s[rows, blk] = (um * sz).astype(BF16)
                d_mixed = (dya * ub) * sz
                d_u.append((dya * mixed) * sz)
                d_az.append((dya * um) * gz)
                dmix_ref[:, blk] += d_mixed
                dmbd = _blockdiag(d_mixed.astype(BF16), lo)
                d_vn.append(_dot(wtcat_ref[p], dmbd))
                dws_s[p] += _dot_nt(dmbd, vnb[:, blk])
            d_vn = jnp.concatenate(d_vn, axis=1)
            dlng_ref[...] += jnp.sum(d_vn * xhat, axis=0, keepdims=True)
            dlnb_ref[...] += jnp.sum(d_vn, axis=0, keepdims=True)
            dxh = d_vn * lng_ref[...]
            d_vv = rs * (dxh - jnp.mean(dxh, axis=-1, keepdims=True)
                         - xhat * jnp.mean(dxh * xhat, axis=-1, keepdims=True))
            dproj_ref[rows, 0:A_WIDTH] = (jnp.concatenate(d_u, axis=1) * gu).astype(BF16)
            dproj_ref[rows, A_WIDTH:2 * A_WIDTH] = (d_vv * gv).astype(BF16)
            dproj_ref[rows, 2 * A_WIDTH:Q_OFF] = jnp.concatenate(d_az, axis=1).astype(BF16)
            ks, vs = _kv_rows(kv_ref, halo_ref, r0, c)
            q_stacks = _query_stacks([q_ref[rows, p * CHUNK:(p + 1) * CHUNK] for p in range(4)], lo)
            probs = {(p, half): probs_ref[rows, (2 * p + half) * CHUNK:(2 * p + half + 1) * CHUNK]
                     for p in range(4) for half in range(2)}
            p_stacks = [jnp.concatenate([_window_keys(probs[m], prev) for m in members], axis=0)
                        for members in _HEADS_BY_COPY]
            dom = {}
            for p in range(4):
                blk = slice(p * CHUNK, (p + 1) * CHUNK)
                sb, gb = _silu_and_grad(bz_ref[rows, blk])
                dyb = dyc_s[rows, B_WIDTH + p * CHUNK:B_WIDTH + (p + 1) * CHUNK]
                o = att_ref[rows, blk]
                ycat_s[rows, B_WIDTH + p * CHUNK:B_WIDTH + (p + 1) * CHUNK] = (o * sb).astype(BF16)
                dproj_ref[rows, BZ_OFF + p * CHUNK:BZ_OFF + (p + 1) * CHUNK] = ((dyb * o) * gb).astype(BF16)
                dom[p] = _masked_halves((dyb * sb).astype(BF16), lo)
            dqs, dk_by_copy, dv_by_copy = {}, [], []
            for sw, members in enumerate(_HEADS_BY_COPY):
                do_stack = jnp.concatenate([dom[p][half] for p, half in members], axis=0)
                dp_stack = _dot_nt(do_stack, vs[sw])
                dsws = []
                for i, (p, half) in enumerate(members):
                    head = 2 * p + half
                    dp = _window_square(dp_stack[i * CHUNK:(i + 1) * CHUNK, :], prev)
                    prob = probs[p, half].astype(F32)
                    delta = jnp.sum(prob * dp, axis=-1, keepdims=True)
                    ds = prob * (dp - delta)
                    dbias_ref[head] += ds
                    dsws.append(_window_keys(ds.astype(BF16), prev))
                ds_stack = jnp.concatenate(dsws, axis=0)
                dq_stack = _dot(ds_stack, ks[sw])
                for i, m in enumerate(members):
                    dqs[m] = dq_stack[i * CHUNK:(i + 1) * CHUNK, :]
                dk_by_copy.append(_dot_tn(ds_stack, q_stacks[sw]))
                dv_by_copy.append(_dot_tn(p_stacks[sw], do_stack))
            for p in range(4):
                dq = jnp.where(lo, dqs[p, 0], dqs[p, 1]) * QK_SCALE
                dproj_ref[rows, Q_OFF + p * CHUNK:Q_OFF + (p + 1) * CHUNK] = dq.astype(BF16)
            both = pl.ds(r0, 2 * CHUNK)
            dkv_s[both, 0:KV_WIDTH] += dk_by_copy[0] + _swap_halves(dk_by_copy[1])
            dkv_s[both, KV_WIDTH:2 * KV_WIDTH] += dv_by_copy[0] + _swap_halves(dv_by_copy[1])
            return carry

        lax.fori_loop(0, n_chunks, chunk, 0, unroll=BWD_CHUNK_UNROLL)
        dwout_s[...] += _dot_tn(ycat_s[...], dy_s[...])
        dproj_ref[:, K_OFF:BZ_OFF] = dkv_s[CHUNK:CHUNK + tb, :].astype(BF16)
        carry_s[...] = dkv_s[0:CHUNK, :]

        @pl.when(step == n_tiles - 1)
        def _():
            row = lax.broadcasted_iota(jnp.int32, (2 * CHUNK, CHUNK), 0)
            col = lax.broadcasted_iota(jnp.int32, (2 * CHUNK, CHUNK), 1)
            causal = col <= jnp.where(row >= CHUNK, row - CHUNK, row)
            for p in range(4):
                dws_ref[p] = jnp.where(causal, dws_s[p], 0.0).astype(BF16)
            rows = stage_s.shape[0]
            for r0 in range(0, D_MODEL, rows):
                stage_s[...] = dwout_s[r0:r0 + rows, :].astype(BF16)
                pltpu.sync_copy(stage_s, dwout_ref.at[r0:r0 + rows, :])

        at_end()

    tile = lambda w: pl.BlockSpec((tb, w), lambda s: (n_tiles - 1 - s, 0))
    whole = lambda shape, **kw: pl.BlockSpec(shape, lambda s: (0,) * len(shape), **kw)
    once = dict(pipeline_mode=pl.Buffered(1))
    gate_cols = pl.BlockSpec((tb, A_WIDTH), lambda s: (n_tiles - 1 - s, 2))
    in_specs = [tile(D_MODEL), tile(D_MODEL), gate_cols, tile(B_WIDTH), tile(2 * KV_WIDTH),
                pl.BlockSpec((CHUNK, 2 * KV_WIDTH), lambda s: (jnp.maximum((n_tiles - 1 - s) * n_chunks - 1, 0), 0)),
                tile(B_WIDTH), tile(HEADS * CHUNK), tile(N_STATS * CHUNK), tile(A_WIDTH), tile(4 * A_WIDTH), tile(B_WIDTH),
                whole((1, A_WIDTH)), whole((1, A_WIDTH)), whole((4, CHUNK, 2 * CHUNK)), whole((D_MODEL, D_MODEL), **once),
                whole((1, D_MODEL))]
    args = [dout, y, a, q, kv, kv, bz, probs, stats, mixed, act, att, ln_g, ln_b, wtcat, w_out, g_post]
    small_shapes = [jax.ShapeDtypeStruct((D_MODEL, D_MODEL), BF16), jax.ShapeDtypeStruct((4, 2 * CHUNK, CHUNK), BF16),
                    jax.ShapeDtypeStruct((CHUNK, A_WIDTH), F32), jax.ShapeDtypeStruct((1, A_WIDTH), F32),
                    jax.ShapeDtypeStruct((1, A_WIDTH), F32), jax.ShapeDtypeStruct((1, D_MODEL), F32),
                    jax.ShapeDtypeStruct((HEADS, CHUNK, CHUNK), F32)]
    small_specs = [ANY, whole((4, 2 * CHUNK, CHUNK)), whole((CHUNK, A_WIDTH)), whole((1, A_WIDTH)), whole((1, A_WIDTH)),
                   whole((1, D_MODEL)), whole((HEADS, CHUNK, CHUNK))]
    scratch = [pltpu.VMEM((tb, D_MODEL), BF16), pltpu.VMEM((tb, D_MODEL), BF16), pltpu.VMEM((tb, D_MODEL), F32),
               pltpu.VMEM((tb + CHUNK, 2 * KV_WIDTH), F32), pltpu.VMEM((CHUNK, 2 * KV_WIDTH), F32),
               pltpu.VMEM((D_MODEL, D_MODEL), F32), pltpu.VMEM((2 * CHUNK, D_MODEL), BF16),
               pltpu.VMEM((4, 2 * CHUNK, CHUNK), F32)]
    out_shape = [jax.ShapeDtypeStruct((n_tok, IN_WIDTH), BF16)] + small_shapes
    out_specs = [tile(IN_WIDTH)] + small_specs
    if exchange is not None:
        dwt, dwo = exchange
        in_specs += [ANY, ANY]
        args += [dwt, dwo]
        out_shape += [jax.ShapeDtypeStruct((N_DEV, dwt.shape[0] // N_CHIPS, D_MODEL), BF16),
                      jax.ShapeDtypeStruct((N_DEV, dwo.shape[0] // N_CHIPS, D_MODEL), BF16)]
        out_specs += [ANY, ANY]
    if allgather is not None:
        in_specs.append(ANY)
        args.append(allgather)
        out_shape.append(jax.ShapeDtypeStruct((N_DEV,) + allgather.shape, allgather.dtype))
        out_specs.append(ANY)
    for hosted in (exchange, allgather):
        if hosted is not None:
            scratch += _comm_scratch(N_EXCHANGE_SEMS)
    return pl.pallas_call(
        body, name="bwd_mix" + ("" if exchange is None and allgather is None else "_exchange"), grid=(n_tiles,),
        out_shape=tuple(out_shape), in_specs=in_specs, out_specs=tuple(out_specs), scratch_shapes=scratch,
        compiler_params=_cparams(1),
    )(*args)


def _bwd_in(dproj, x, dout, g_pre, w_in_t):
    n_tok = x.shape[0]
    tm = TOK_TILE
    n_tiles = n_tok // tm

    def body(dp_ref, x_ref, dout_ref, g_ref, wt_ref, dx_ref, dwt_ref, dg_ref, acc_s):
        i = pl.program_id(0)

        @pl.when(i == 0)
        def _():
            acc_s[...] = jnp.zeros_like(acc_s)
            dg_ref[...] = jnp.zeros_like(dg_ref)

        xf = x_ref[...]
        r = lax.rsqrt(jnp.mean(xf * xf, axis=-1, keepdims=True) + NORM_EPS)
        xn = xf * r
        h = (xn * g_ref[...]).astype(BF16)
        dp = dp_ref[...]
        dh = _dot(dp, wt_ref[...])
        acc_s[...] += _dot_tn(dp, h)
        dg_ref[...] += jnp.sum(dh * xn, axis=0, keepdims=True)
        dhn = dh * g_ref[...]
        dx_ref[...] = dout_ref[...] + r * (dhn - xn * jnp.mean(dhn * xn, axis=-1, keepdims=True))

        @pl.when(i == n_tiles - 1)
        def _():
            dwt_ref[...] = acc_s[...].astype(BF16)

    return pl.pallas_call(
        body, name="bwd_in", grid=(n_tiles,),
        out_shape=(jax.ShapeDtypeStruct((n_tok, D_MODEL), F32), jax.ShapeDtypeStruct((IN_WIDTH, D_MODEL), BF16),
                   jax.ShapeDtypeStruct((1, D_MODEL), F32)),
        in_specs=[pl.BlockSpec((tm, IN_WIDTH), lambda i: (i, 0)), pl.BlockSpec((tm, D_MODEL), lambda i: (i, 0)),
                  pl.BlockSpec((tm, D_MODEL), lambda i: (i, 0)), pl.BlockSpec((1, D_MODEL), lambda i: (0, 0)),
                  pl.BlockSpec((IN_WIDTH, D_MODEL), lambda i: (0, 0), pipeline_mode=pl.Buffered(1))],
        out_specs=(pl.BlockSpec((tm, D_MODEL), lambda i: (i, 0)),
                   pl.BlockSpec((IN_WIDTH, D_MODEL), lambda i: (0, 0), pipeline_mode=pl.Buffered(1)),
                   pl.BlockSpec((1, D_MODEL), lambda i: (0, 0))),
        scratch_shapes=[pltpu.VMEM((IN_WIDTH, D_MODEL), F32)],
        compiler_params=_cparams(1),
    )(dproj, x, dout, g_pre, w_in_t)


def _riders(exchanges):
    args = [a for arrays, _ in exchanges for a in arrays]
    shapes = [jax.ShapeDtypeStruct((N_DEV, a.shape[0] // N_CHIPS if by_chip else a.shape[0]) + a.shape[1:], a.dtype)
              for arrays, by_chip in exchanges for a in arrays]
    scratch = [s for _ in exchanges for s in _comm_scratch(N_EXCHANGE_SEMS)]

    def bind(in_refs, out_refs, sem_refs, step, n_steps, pass_on_step=None):
        in_refs, out_refs, sem_refs = list(in_refs), list(out_refs), list(sem_refs)
        starts, ends = [], []
        for arrays, by_chip in exchanges:
            n = len(arrays)
            phases = _exchange_phases(tuple(in_refs[:n]), tuple(out_refs[:n]), *sem_refs[:3], by_chip=by_chip)
            del in_refs[:n], out_refs[:n], sem_refs[:3]
            at_start, at_end = _hosted(phases, step, n_steps, pass_on_step)
            starts.append(at_start)
            ends.append(at_end)
        return (lambda: [f() for f in starts]), (lambda: [f() for f in ends])

    return args, [ANY] * len(args), shapes, [ANY] * len(shapes), scratch, bind


def _bwd_in_dw(dproj, x, g_pre, exchanges):
    n_tok = x.shape[0]
    tm = TOK_TILE
    n_tiles = n_tok // tm
    r_args, r_in_specs, r_shapes, r_out_specs, r_scratch, bind = _riders(exchanges)

    def body(*refs):
        dp_ref, x_ref, g_ref = refs[:3]
        in_refs = refs[3:3 + len(r_args)]
        dwt_ref = refs[3 + len(r_args)]
        out_refs = refs[4 + len(r_args):4 + 2 * len(r_args)]
        acc_s = refs[4 + 2 * len(r_args)]
        i = pl.program_id(0)
        at_start, at_end = bind(in_refs, out_refs, refs[5 + 2 * len(r_args):], i, n_tiles)
        at_start()

        @pl.when(i == 0)
        def _():
            acc_s[...] = jnp.zeros_like(acc_s)

        xf = x_ref[...]
        r = lax.rsqrt(jnp.mean(xf * xf, axis=-1, keepdims=True) + NORM_EPS)
        acc_s[...] += _dot_tn(dp_ref[...], ((xf * r) * g_ref[...]).astype(BF16))

        @pl.when(i == n_tiles - 1)
        def _():
            dwt_ref[...] = acc_s[...].astype(BF16)

        at_end()

    return pl.pallas_call(
        body, name="bwd_in_dw", grid=(n_tiles,),
        out_shape=(jax.ShapeDtypeStruct((IN_WIDTH, D_MODEL), BF16), *r_shapes),
        in_specs=[pl.BlockSpec((tm, IN_WIDTH), lambda i: (i, 0)), pl.BlockSpec((tm, D_MODEL), lambda i: (i, 0)),
                  pl.BlockSpec((1, D_MODEL), lambda i: (0, 0))] + r_in_specs,
        out_specs=(pl.BlockSpec((IN_WIDTH, D_MODEL), lambda i: (0, 0), pipeline_mode=pl.Buffered(1)), *r_out_specs),
        scratch_shapes=[pltpu.VMEM((IN_WIDTH, D_MODEL), F32)] + r_scratch,
        compiler_params=_cparams(1),
    )(dproj, x, g_pre, *r_args)


def _bwd_in_dx(dproj, x, dout, g_pre, w_in_t, exchanges):
    n_tok = x.shape[0]
    tm = TOK_TILE
    n_tiles = n_tok // tm
    r_args, r_in_specs, r_shapes, r_out_specs, r_scratch, bind = _riders(exchanges)

    def body(*refs):
        dp_ref, x_ref, dout_ref, g_ref, wt_ref = refs[:5]
        in_refs = refs[5:5 + len(r_args)]
        dx_ref, dg_ref = refs[5 + len(r_args):7 + len(r_args)]
        out_refs = refs[7 + len(r_args):7 + 2 * len(r_args)]
        i = pl.program_id(0)
        at_start, at_end = bind(in_refs, out_refs, refs[7 + 2 * len(r_args):], i, n_tiles, pass_on_step=n_tiles - 1)
        at_start()

        @pl.when(i == 0)
        def _():
            dg_ref[...] = jnp.zeros_like(dg_ref)

        xf = x_ref[...]
        r = lax.rsqrt(jnp.mean(xf * xf, axis=-1, keepdims=True) + NORM_EPS)
        xn = xf * r
        dh = _dot(dp_ref[...], wt_ref[...])
        dg_ref[...] += jnp.sum(dh * xn, axis=0, keepdims=True)
        dhn = dh * g_ref[...]
        dx_ref[...] = dout_ref[...] + r * (dhn - xn * jnp.mean(dhn * xn, axis=-1, keepdims=True))
        at_end()

    tile = lambda w: pl.BlockSpec((tm, w), lambda i: (i, 0))
    return pl.pallas_call(
        body, name="bwd_in_dx", grid=(n_tiles,),
        out_shape=(jax.ShapeDtypeStruct((n_tok, D_MODEL), F32), jax.ShapeDtypeStruct((1, D_MODEL), F32), *r_shapes),
        in_specs=[tile(IN_WIDTH), tile(D_MODEL), tile(D_MODEL), pl.BlockSpec((1, D_MODEL), lambda i: (0, 0)),
                  pl.BlockSpec((IN_WIDTH, D_MODEL), lambda i: (0, 0), pipeline_mode=pl.Buffered(1))] + r_in_specs,
        out_specs=(tile(D_MODEL), pl.BlockSpec((1, D_MODEL), lambda i: (0, 0)), *r_out_specs),
        scratch_shapes=r_scratch,
        compiler_params=_cparams(1),
    )(dproj, x, dout, g_pre, w_in_t, *r_args)


def _adam_update(w, g, m, v):
    nm = ADAM_B1 * m + (1.0 - ADAM_B1) * g
    nv = ADAM_B2 * v + (1.0 - ADAM_B2) * (g * g)
    m_hat = nm / (1.0 - ADAM_B1 ** ADAM_STEP)
    v_hat = nv / (1.0 - ADAM_B2 ** ADAM_STEP)
    return -ADAM_LR * (m_hat / (jnp.sqrt(v_hat) + ADAM_EPS) + ADAM_WD * w), nm, nv


def _sum_slots(p_ref):
    tot = p_ref[0].astype(F32)
    for d in range(1, N_DEV):
        tot = tot + p_ref[d].astype(F32)
    return tot


def _adamw_parts(parts, w, m, v, name):
    n_layers, n_rows, n_cols = w.shape
    tr = n_rows
    while tr * n_cols > ADAMW_BLOCK_ELEMS:
        tr //= 2
    n_blocks = n_rows // tr

    def body(*refs):
        p_refs = refs[:n_layers]
        w_ref, m_ref, v_ref, g_ref, d_ref, nm_ref, nv_ref = refs[n_layers:]
        for k in range(n_layers):
            @pl.when(pl.program_id(0) == k)
            def _(k=k):
                g = _sum_slots(p_refs[k])
                g_ref[0] = g
                d_ref[0], nm_ref[0], nv_ref[0] = _adam_update(w_ref[0], g, m_ref[0], v_ref[0])

    def part_spec(k):
        return pl.BlockSpec((N_DEV, tr, n_cols), lambda l, i: (0, jnp.clip(i + (l - k) * n_blocks, 0, n_blocks - 1), 0))

    spec = pl.BlockSpec((1, tr, n_cols), lambda l, i: (l, i, 0))
    shape = jax.ShapeDtypeStruct((n_layers, n_rows, n_cols), F32)
    return pl.pallas_call(
        body, name="adamw_" + name, grid=(n_layers, n_blocks), out_shape=(shape,) * 4,
        in_specs=[part_spec(k) for k in range(n_layers)] + [spec, spec, spec], out_specs=(spec,) * 4,
        compiler_params=_cparams(2),
    )(*parts, w, m, v)


def _adamw_small(quads):
    n = len(quads)

    def body(*refs):
        ins, outs = refs[:4 * n], refs[4 * n:]
        for j in range(n):
            w_ref, g_ref, m_ref, v_ref = ins[4 * j:4 * j + 4]
            d_ref, nm_ref, nv_ref = outs[3 * j:3 * j + 3]
            d_ref[...], nm_ref[...], nv_ref[...] = _adam_update(w_ref[...], g_ref[...], m_ref[...], v_ref[...])

    vmem = pl.BlockSpec(memory_space=pltpu.VMEM)
    shapes = tuple(jax.ShapeDtypeStruct(w.shape, F32) for w, *_ in quads for _ in range(3))
    results = pl.pallas_call(
        body, name="adamw_small", out_shape=shapes, in_specs=[vmem] * (4 * n), out_specs=(vmem,) * (3 * n),
    )(*[a for quad in quads for a in quad])
    return [results[3 * j:3 * j + 3] for j in range(n)]


_SMALL = ("pre_norm_g", "ln_v_g", "ln_v_b", "b_spatial", "sinks", "rel_bias", "post_norm_g")


def _pack(pieces):
    blocks, first_rows, n = [], [], 0
    for p in pieces:
        flat = p.reshape(-1)
        flat = jnp.concatenate([flat, jnp.zeros(((-flat.shape[0]) % (8 * 128),), F32)]).reshape(-1, 128)
        blocks.append(flat)
        first_rows.append(n)
        n += flat.shape[0]
    return jnp.concatenate(blocks, axis=0), first_rows


def _unpack(block, first_row, shape):
    size = math.prod(shape)
    return block[first_row:first_row + -(-size // 128)].reshape(-1)[:size].reshape(shape)


def kernel(x, pre_norm_g, w_in, ln_v_g, ln_v_b, w_spatial, b_spatial, sinks, rel_bias, w_out, post_norm_g, loss_target, m_pre_norm_g, m_w_in, m_ln_v_g, m_ln_v_b, m_w_spatial, m_b_spatial, m_sinks, m_rel_bias, m_w_out, m_post_norm_g, v_pre_norm_g, v_w_in, v_ln_v_g, v_ln_v_b, v_w_spatial, v_b_spatial, v_sinks, v_rel_bias, v_w_out, v_post_norm_g):
    weights = dict(pre_norm_g=pre_norm_g, w_in=w_in, ln_v_g=ln_v_g, ln_v_b=ln_v_b, w_spatial=w_spatial, b_spatial=b_spatial,
                   sinks=sinks, rel_bias=rel_bias, w_out=w_out, post_norm_g=post_norm_g)
    mom_m = dict(pre_norm_g=m_pre_norm_g, w_in=m_w_in, ln_v_g=m_ln_v_g, ln_v_b=m_ln_v_b, w_spatial=m_w_spatial,
                 b_spatial=m_b_spatial, sinks=m_sinks, rel_bias=m_rel_bias, w_out=m_w_out, post_norm_g=m_post_norm_g)
    mom_v = dict(pre_norm_g=v_pre_norm_g, w_in=v_w_in, ln_v_g=v_ln_v_g, ln_v_b=v_ln_v_b, w_spatial=v_w_spatial,
                 b_spatial=v_b_spatial, sinks=v_sinks, rel_bias=v_rel_bias, w_out=v_w_out, post_norm_g=v_post_norm_g)
    n_seq, seq_len, _ = x.shape
    n_layers = w_in.shape[0]
    x2 = x.reshape(n_seq * seq_len, D_MODEL)
    target2 = loss_target.reshape(n_seq * seq_len, D_MODEL)
    row = lambda p, l: p[l][None]

    wt_shards = jnp.swapaxes(w_in, 1, 2).astype(BF16)
    wo_shards = w_out.astype(BF16)
    bucket = _window_buckets()
    bias = _bias_table(rel_bias, bucket)
    wcat, wtcat, bs = _spatial_tables(w_spatial, jnp.swapaxes(b_spatial, 1, 2))

    wt, wo = [None] * n_layers, [None] * n_layers
    wt[0], wo[0] = _gather_weights(wt_shards, wo_shards, 0)
    xs, saved = [x2], []
    for l in range(n_layers):
        layer_args = (xs[-1], row(pre_norm_g, l), wt[l], row(ln_v_g, l), row(ln_v_b, l), wcat[l], bs[l], sinks[l], bias,
                      wo[l], row(post_norm_g, l), seq_len)
        if l + 1 < n_layers:
            *acts, xn, y, wt[l + 1], wo[l + 1] = _fwd_layer(*layer_args, gather=(wt_shards, wo_shards, l + 1))
            xs.append(xn)
        else:
            *acts, dx, y, loss = _fwd_layer(*layer_args, target=target2)
        saved.append((y, *acts))

    small = [None] * n_layers
    parts = [None] * n_layers
    waiting = None
    for l in reversed(range(n_layers)):
        hosted = {} if waiting is None else dict(exchange=waiting[:2], allgather=waiting[2])
        dproj, *rest = _bwd_mix(dx, *saved[l], row(ln_v_g, l), row(ln_v_b, l), wtcat[l], wo[l], row(post_norm_g, l), **hosted)
        dwo, dws, dmix, dlng, dlnb, dgpost, dbias = rest[:7]
        dws = dws.reshape(HEADS * CHUNK, CHUNK)
        if waiting is not None:
            parts[l + 1] = list(rest[7:])
        if l > 0:
            dx, dwt, dgpre = _bwd_in(dproj, xs[l], dx, row(pre_norm_g, l), wt[l])
            waiting = (dwt, dwo, dws)
        else:
            dwt, po, ws_all = _bwd_in_dw(dproj, xs[l], row(pre_norm_g, l), [((dwo,), True), ((dws,), False)])
            dx, dgpre, pt = _bwd_in_dx(dproj, xs[l], dx, row(pre_norm_g, l), wt[l], [((dwt,), True)])
            parts[l] = [pt, po, ws_all]
        small[l] = dict(pre_norm_g=dgpre[0], ln_v_g=dlng[0], ln_v_b=dlnb[0], dmix=dmix, dbias=dbias,
                        post_norm_g=dgpost[0])

    db = _b_spatial_grad([s["dmix"] for s in small])
    pieces = {n: jnp.stack([s[n] for s in small]) for n in ("pre_norm_g", "ln_v_g", "ln_v_b", "post_norm_g")}
    pieces["b_spatial"] = jnp.swapaxes(db[:, :, :HEADS], 1, 2)
    pieces["rel_bias"], pieces["sinks"] = _rel_bias_grad([s["dbias"] for s in small], bucket)
    packed, first_rows = _pack([pieces[n] for n in _SMALL] + [loss])
    total = _allreduce_small(packed)
    grad = {n: _unpack(total, r, weights[n].shape) for n, r in zip(_SMALL, first_rows)}
    loss_out = total[first_rows[-1], 0]

    delta, new_m, new_v = {}, {}, {}
    two_d = lambda p: p.reshape(-1, p.shape[-1])
    updates = _adamw_small([tuple(two_d(p[n]) for p in (weights, grad, mom_m, mom_v)) for n in _SMALL])
    for n, update in zip(_SMALL, updates):
        delta[n], new_m[n], new_v[n] = (u.reshape(weights[n].shape) for u in update)

    t3 = lambda p: jnp.swapaxes(p, 1, 2)
    flat3 = lambda p: p.reshape(n_layers, HEADS * CHUNK, CHUNK)
    for n, which, view, back in (("w_in", 0, t3, t3), ("w_out", 1, lambda p: p, lambda p: p),
                                 ("w_spatial", 2, flat3, lambda p: p.reshape(w_spatial.shape))):
        results = _adamw_parts([parts[l][which] for l in range(n_layers)], view(weights[n]), view(mom_m[n]),
                               view(mom_v[n]), n)
        grad[n], delta[n], new_m[n], new_v[n] = (back(r) for r in results)

    names = tuple(weights)
    return (loss_out, dx.reshape(x.shape), *[grad[n] for n in names], *[delta[n] for n in names],
            *[new_m[n] for n in names], *[new_v[n] for n in names])
```
